```python
import math
import jax, jax.numpy as jnp
from jax import lax
import numpy as np

D_MODEL = 2048
BATCH = 8
SEQ = 4096
DEPTH = 1

D_MIX = D_MODEL
D_SSM = D_MIX // 2
SSM_GROUP = 16
SSM_GROUPS = D_SSM // SSM_GROUP
SSM_STATE = 64
D_SGU = D_MIX - D_SSM
SGU_CHUNK = 128
SGU_HEADS = 8
SGU_HEAD_DIM = D_SGU // SGU_HEADS
D_FFN = -(-8 * D_MODEL // (3 * 256)) * 256
PLE_DIM = 256
EPS = 1e-6
DT_MIN = 1e-3
DT_MAX = 1e-1
LAMBDA_RE_MAX = -1e-4

kernel_name = "hybrid_s5_sgu_parallel_heads"


def rmsnorm(x, g):
    xf = x.astype(jnp.float32)
    r = lax.rsqrt(jnp.mean(xf * xf, axis=-1, keepdims=True) + EPS)
    return (xf * r).astype(x.dtype) * g


def layernorm(x, g, b):
    xf = x.astype(jnp.float32)
    mu = jnp.mean(xf, axis=-1, keepdims=True)
    xc = xf - mu
    r = lax.rsqrt(jnp.mean(xc * xc, axis=-1, keepdims=True) + EPS)
    return (xc * r).astype(x.dtype) * g + b


def _complex_linear_combine(e1, e2):
    a1r, a1i, b1r, b1i = e1
    a2r, a2i, b2r, b2i = e2
    ar = a2r * a1r - a2i * a1i
    ai = a2r * a1i + a2i * a1r
    br = a2r * b1r - a2i * b1i + b2r
    bi = a2r * b1i + a2i * b1r + b2i
    return (ar, ai, br, bi)


def s5_mixer(u, lam_re, lam_im, log_step, b_re, b_im, c_re, c_im, d, glu_w, glu_b):
    bsz, L, _ = u.shape
    f32 = jnp.float32
    ug = u.reshape(bsz, L, SSM_GROUPS, SSM_GROUP).astype(f32)
    lr = jnp.minimum(lam_re.astype(f32), LAMBDA_RE_MAX)
    li = lam_im.astype(f32)
    dt = jnp.exp(log_step.astype(f32))[:, None]
    mag = jnp.exp(lr * dt)
    ang = li * dt
    abar_re = mag * jnp.cos(ang)
    abar_im = mag * jnp.sin(ang)
    nr = abar_re - 1.0
    ni = abar_im
    den = lr * lr + li * li
    q_re = (nr * lr + ni * li) / den
    q_im = (ni * lr - nr * li) / den
    b_re32 = b_re.astype(f32)
    b_im32 = b_im.astype(f32)
    bbar_re = q_re[..., None] * b_re32 - q_im[..., None] * b_im32
    bbar_im = q_re[..., None] * b_im32 + q_im[..., None] * b_re32
    bu_re = jnp.einsum('blgh,gph->blgp', ug, bbar_re)
    bu_im = jnp.einsum('blgh,gph->blgp', ug, bbar_im)
    a_re = jnp.broadcast_to(abar_re, bu_re.shape)
    a_im = jnp.broadcast_to(abar_im, bu_im.shape)
    _, _, s_re, s_im = lax.associative_scan(
        _complex_linear_combine, (a_re, a_im, bu_re, bu_im), axis=1)
    y = (jnp.einsum('blgp,ghp->blgh', s_re, c_re.astype(f32))
         - jnp.einsum('blgp,ghp->blgh', s_im, c_im.astype(f32))
         + d.astype(f32) * ug)
    y = y.reshape(bsz, L, D_SSM).astype(u.dtype)
    y = jax.nn.gelu(y)
    return y * jax.nn.sigmoid(y @ glu_w + glu_b)


def sgu_mixer(u, v, ln_g, ln_b, w_s, b_s):
    bsz, L, _ = u.shape
    u = jax.nn.gelu(u)
    v = layernorm(jax.nn.gelu(v), ln_g, ln_b)
    vc = v.reshape(bsz, L // SGU_CHUNK, SGU_CHUNK, SGU_HEADS, SGU_HEAD_DIM)
    mask = jnp.tril(jnp.ones((SGU_CHUNK, SGU_CHUNK), dtype=bool))
    w = jnp.where(mask[None], w_s, jnp.zeros_like(w_s))
    s = jnp.einsum('hts,bnshc->bnthc', w, vc) + b_s.T[None, None, :, :, None]
    return u * s.reshape(bsz, L, D_SGU)


def _fwd_setup_inputs(seed: int = 0) -> dict:
    key = jax.random.key(seed)
    ks = jax.random.split(key, 32)
    f32 = jnp.float32
    nrm = lambda k, shape, scale: jax.random.normal(k, shape, f32) * scale
    gain = lambda k, shape: 1.0 + 0.01 * jax.random.normal(k, shape, f32)
    x = jax.random.normal(ks[0], (BATCH, SEQ, D_MODEL), f32)
    p = jax.random.normal(ks[1], (DEPTH, BATCH, SEQ, PLE_DIM), f32)
    norm_mix_g = gain(ks[2], (DEPTH, D_MODEL))
    w_in = nrm(ks[3], (DEPTH, D_MODEL, D_SSM + 2 * D_SGU), D_MODEL ** -0.5)
    ssm_lambda_re = -0.5 + 0.01 * jax.random.normal(ks[4], (DEPTH, SSM_GROUPS, SSM_STATE), f32)
    ssm_lambda_im = (jnp.pi * jnp.arange(SSM_STATE, dtype=f32))[None, None, :] \
        + 0.01 * jax.random.normal(ks[5], (DEPTH, SSM_GROUPS, SSM_STATE), f32)
    ssm_log_step = math.log(DT_MIN) + jax.random.uniform(ks[6], (DEPTH, SSM_GROUPS), f32) \
        * (math.log(DT_MAX) - math.log(DT_MIN))
    bs = (2.0 * SSM_GROUP) ** -0.5
    cs = (2.0 * SSM_STATE) ** -0.5
    ssm_b_re = nrm(ks[7], (DEPTH, SSM_GROUPS, SSM_STATE, SSM_GROUP), bs)
    ssm_b_im = nrm(ks[8], (DEPTH, SSM_GROUPS, SSM_STATE, SSM_GROUP), bs)
    ssm_c_re = nrm(ks[9], (DEPTH, SSM_GROUPS, SSM_GROUP, SSM_STATE), cs)
    ssm_c_im = nrm(ks[10], (DEPTH, SSM_GROUPS, SSM_GROUP, SSM_STATE), cs)
    ssm_d = nrm(ks[11], (DEPTH, SSM_GROUPS, SSM_GROUP), 0.5)
    ssm_glu_w = nrm(ks[12], (DEPTH, D_SSM, D_SSM), D_SSM ** -0.5)
    ssm_glu_b = nrm(ks[13], (DEPTH, D_SSM), 0.01)
    sgu_ln_g = gain(ks[14], (DEPTH, D_SGU))
    sgu_ln_b = nrm(ks[15], (DEPTH, D_SGU), 0.01)
    sgu_w = nrm(ks[16], (DEPTH, SGU_HEADS, SGU_CHUNK, SGU_CHUNK), SGU_CHUNK ** -0.5)
    sgu_b = gain(ks[17], (DEPTH, SGU_HEADS, SGU_CHUNK))
    out_norm_ssm_g = gain(ks[18], (DEPTH, D_SSM))
    out_norm_sgu_g = gain(ks[19], (DEPTH, D_SGU))
    w_out = nrm(ks[20], (DEPTH, D_MIX, D_MODEL), D_MIX ** -0.5)
    norm_ffn_g = gain(ks[21], (DEPTH, D_MODEL))
    w_ffn_in = nrm(ks[22], (DEPTH, D_MODEL, 2 * D_FFN), D_MODEL ** -0.5)
    w_ffn_out = nrm(ks[23], (DEPTH, D_FFN, D_MODEL), D_FFN ** -0.5)
    norm_ple_g = gain(ks[24], (DEPTH, D_MODEL))
    w_ple_gate = nrm(ks[25], (DEPTH, D_MODEL, D_MODEL), D_MODEL ** -0.5)
    b_ple_gate = nrm(ks[26], (DEPTH, D_MODEL), 0.01)
    w_ple_proj = nrm(ks[27], (DEPTH, PLE_DIM, D_MODEL), PLE_DIM ** -0.5)
    final_norm_g = gain(ks[28], (D_MODEL,))
    return {
        "x": x, "p": p, "norm_mix_g": norm_mix_g, "w_in": w_in,
        "ssm_lambda_re": ssm_lambda_re, "ssm_lambda_im": ssm_lambda_im,
        "ssm_log_step": ssm_log_step, "ssm_b_re": ssm_b_re, "ssm_b_im": ssm_b_im,
        "ssm_c_re": ssm_c_re, "ssm_c_im": ssm_c_im, "ssm_d": ssm_d,
        "ssm_glu_w": ssm_glu_w, "ssm_glu_b": ssm_glu_b,
        "sgu_ln_g": sgu_ln_g, "sgu_ln_b": sgu_ln_b, "sgu_w": sgu_w, "sgu_b": sgu_b,
        "out_norm_ssm_g": out_norm_ssm_g, "out_norm_sgu_g": out_norm_sgu_g,
        "w_out": w_out, "norm_ffn_g": norm_ffn_g, "w_ffn_in": w_ffn_in,
        "w_ffn_out": w_ffn_out, "norm_ple_g": norm_ple_g, "w_ple_gate": w_ple_gate,
        "b_ple_gate": b_ple_gate, "w_ple_proj": w_ple_proj, "final_norm_g": final_norm_g,
    }


def _fwd_reference(x, p, norm_mix_g, w_in, ssm_lambda_re, ssm_lambda_im, ssm_log_step,
              ssm_b_re, ssm_b_im, ssm_c_re, ssm_c_im, ssm_d, ssm_glu_w, ssm_glu_b,
              sgu_ln_g, sgu_ln_b, sgu_w, sgu_b, out_norm_ssm_g, out_norm_sgu_g,
              w_out, norm_ffn_g, w_ffn_in, w_ffn_out, norm_ple_g, w_ple_gate,
              b_ple_gate, w_ple_proj, final_norm_g):
    for i in range(DEPTH):
        h = rmsnorm(x, norm_mix_g[i])
        z = h @ w_in[i]
        z_ssm = z[..., :D_SSM]
        z_u = z[..., D_SSM:D_SSM + D_SGU]
        z_v = z[..., D_SSM + D_SGU:]
        y_a = s5_mixer(z_ssm, ssm_lambda_re[i], ssm_lambda_im[i], ssm_log_step[i],
                       ssm_b_re[i], ssm_b_im[i], ssm_c_re[i], ssm_c_im[i], ssm_d[i],
                       ssm_glu_w[i], ssm_glu_b[i])
        y_b = sgu_mixer(z_u, z_v, sgu_ln_g[i], sgu_ln_b[i], sgu_w[i], sgu_b[i])
        y = jnp.concatenate([rmsnorm(y_a, out_norm_ssm_g[i]),
                             rmsnorm(y_b, out_norm_sgu_g[i])], axis=-1)
        x = x + y @ w_out[i]
        h = rmsnorm(x, norm_ffn_g[i])
        gu = h @ w_ffn_in[i]
        x = x + (jax.nn.silu(gu[..., :D_FFN]) * gu[..., D_FFN:]) @ w_ffn_out[i]
        h = rmsnorm(x, norm_ple_g[i])
        gate = jax.nn.sigmoid(h @ w_ple_gate[i] + b_ple_gate[i])
        x = x + gate * (p[i] @ w_ple_proj[i])
    return rmsnorm(x, final_norm_g)


import jax as _jax
import jax.numpy as _jnp

TWIN_FORMAT = 'train_step'
FWD_PARAMS = ['x', 'p', 'norm_mix_g', 'w_in', 'ssm_lambda_re', 'ssm_lambda_im', 'ssm_log_step', 'ssm_b_re', 'ssm_b_im', 'ssm_c_re', 'ssm_c_im', 'ssm_d', 'ssm_glu_w', 'ssm_glu_b', 'sgu_ln_g', 'sgu_ln_b', 'sgu_w', 'sgu_b', 'out_norm_ssm_g', 'out_norm_sgu_g', 'w_out', 'norm_ffn_g', 'w_ffn_in', 'w_ffn_out', 'norm_ple_g', 'w_ple_gate', 'b_ple_gate', 'w_ple_proj', 'final_norm_g']
TWIN_WEIGHTS = ['norm_mix_g', 'w_in', 'ssm_lambda_re', 'ssm_lambda_im', 'ssm_log_step', 'ssm_b_re', 'ssm_b_im', 'ssm_c_re', 'ssm_c_im', 'ssm_d', 'ssm_glu_w', 'ssm_glu_b', 'sgu_ln_g', 'sgu_ln_b', 'sgu_w', 'sgu_b', 'out_norm_ssm_g', 'out_norm_sgu_g', 'w_out', 'norm_ffn_g', 'w_ffn_in', 'w_ffn_out', 'norm_ple_g', 'w_ple_gate', 'b_ple_gate', 'w_ple_proj', 'final_norm_g']
TWIN_DIFF_INPUT = 'x'
TWIN_INPUTS = ['x', 'p', 'norm_mix_g', 'w_in', 'ssm_lambda_re', 'ssm_lambda_im', 'ssm_log_step', 'ssm_b_re', 'ssm_b_im', 'ssm_c_re', 'ssm_c_im', 'ssm_d', 'ssm_glu_w', 'ssm_glu_b', 'sgu_ln_g', 'sgu_ln_b', 'sgu_w', 'sgu_b', 'out_norm_ssm_g', 'out_norm_sgu_g', 'w_out', 'norm_ffn_g', 'w_ffn_in', 'w_ffn_out', 'norm_ple_g', 'w_ple_gate', 'b_ple_gate', 'w_ple_proj', 'final_norm_g', 'loss_target', 'm_norm_mix_g', 'm_w_in', 'm_ssm_lambda_re', 'm_ssm_lambda_im', 'm_ssm_log_step', 'm_ssm_b_re', 'm_ssm_b_im', 'm_ssm_c_re', 'm_ssm_c_im', 'm_ssm_d', 'm_ssm_glu_w', 'm_ssm_glu_b', 'm_sgu_ln_g', 'm_sgu_ln_b', 'm_sgu_w', 'm_sgu_b', 'm_out_norm_ssm_g', 'm_out_norm_sgu_g', 'm_w_out', 'm_norm_ffn_g', 'm_w_ffn_in', 'm_w_ffn_out', 'm_norm_ple_g', 'm_w_ple_gate', 'm_b_ple_gate', 'm_w_ple_proj', 'm_final_norm_g', 'v_norm_mix_g', 'v_w_in', 'v_ssm_lambda_re', 'v_ssm_lambda_im', 'v_ssm_log_step', 'v_ssm_b_re', 'v_ssm_b_im', 'v_ssm_c_re', 'v_ssm_c_im', 'v_ssm_d', 'v_ssm_glu_w', 'v_ssm_glu_b', 'v_sgu_ln_g', 'v_sgu_ln_b', 'v_sgu_w', 'v_sgu_b', 'v_out_norm_ssm_g', 'v_out_norm_sgu_g', 'v_w_out', 'v_norm_ffn_g', 'v_w_ffn_in', 'v_w_ffn_out', 'v_norm_ple_g', 'v_w_ple_gate', 'v_b_ple_gate', 'v_w_ple_proj', 'v_final_norm_g']
TWIN_OUTPUTS = ['loss', 'grad_x', 'grad_norm_mix_g', 'grad_w_in', 'grad_ssm_lambda_re', 'grad_ssm_lambda_im', 'grad_ssm_log_step', 'grad_ssm_b_re', 'grad_ssm_b_im', 'grad_ssm_c_re', 'grad_ssm_c_im', 'grad_ssm_d', 'grad_ssm_glu_w', 'grad_ssm_glu_b', 'grad_sgu_ln_g', 'grad_sgu_ln_b', 'grad_sgu_w', 'grad_sgu_b', 'grad_out_norm_ssm_g', 'grad_out_norm_sgu_g', 'grad_w_out', 'grad_norm_ffn_g', 'grad_w_ffn_in', 'grad_w_ffn_out', 'grad_norm_ple_g', 'grad_w_ple_gate', 'grad_b_ple_gate', 'grad_w_ple_proj', 'grad_final_norm_g', 'delta_norm_mix_g', 'delta_w_in', 'delta_ssm_lambda_re', 'delta_ssm_lambda_im', 'delta_ssm_log_step', 'delta_ssm_b_re', 'delta_ssm_b_im', 'delta_ssm_c_re', 'delta_ssm_c_im', 'delta_ssm_d', 'delta_ssm_glu_w', 'delta_ssm_glu_b', 'delta_sgu_ln_g', 'delta_sgu_ln_b', 'delta_sgu_w', 'delta_sgu_b', 'delta_out_norm_ssm_g', 'delta_out_norm_sgu_g', 'delta_w_out', 'delta_norm_ffn_g', 'delta_w_ffn_in', 'delta_w_ffn_out', 'delta_norm_ple_g', 'delta_w_ple_gate', 'delta_b_ple_gate', 'delta_w_ple_proj', 'delta_final_norm_g', 'new_m_norm_mix_g', 'new_m_w_in', 'new_m_ssm_lambda_re', 'new_m_ssm_lambda_im', 'new_m_ssm_log_step', 'new_m_ssm_b_re', 'new_m_ssm_b_im', 'new_m_ssm_c_re', 'new_m_ssm_c_im', 'new_m_ssm_d', 'new_m_ssm_glu_w', 'new_m_ssm_glu_b', 'new_m_sgu_ln_g', 'new_m_sgu_ln_b', 'new_m_sgu_w', 'new_m_sgu_b', 'new_m_out_norm_ssm_g', 'new_m_out_norm_sgu_g', 'new_m_w_out', 'new_m_norm_ffn_g', 'new_m_w_ffn_in', 'new_m_w_ffn_out', 'new_m_norm_ple_g', 'new_m_w_ple_gate', 'new_m_b_ple_gate', 'new_m_w_ple_proj', 'new_m_final_norm_g', 'new_v_norm_mix_g', 'new_v_w_in', 'new_v_ssm_lambda_re', 'new_v_ssm_lambda_im', 'new_v_ssm_log_step', 'new_v_ssm_b_re', 'new_v_ssm_b_im', 'new_v_ssm_c_re', 'new_v_ssm_c_im', 'new_v_ssm_d', 'new_v_ssm_glu_w', 'new_v_ssm_glu_b', 'new_v_sgu_ln_g', 'new_v_sgu_ln_b', 'new_v_sgu_w', 'new_v_sgu_b', 'new_v_out_norm_ssm_g', 'new_v_out_norm_sgu_g', 'new_v_w_out', 'new_v_norm_ffn_g', 'new_v_w_ffn_in', 'new_v_w_ffn_out', 'new_v_norm_ple_g', 'new_v_w_ple_gate', 'new_v_b_ple_gate', 'new_v_w_ple_proj', 'new_v_final_norm_g']
TWIN_LEAF_KINDS = {'loss': 'loss', 'grad_x': 'grad_x', 'grad_norm_mix_g': 'grad_w', 'grad_w_in': 'grad_w', 'grad_ssm_lambda_re': 'grad_w', 'grad_ssm_lambda_im': 'grad_w', 'grad_ssm_log_step': 'grad_w', 'grad_ssm_b_re': 'grad_w', 'grad_ssm_b_im': 'grad_w', 'grad_ssm_c_re': 'grad_w', 'grad_ssm_c_im': 'grad_w', 'grad_ssm_d': 'grad_w', 'grad_ssm_glu_w': 'grad_w', 'grad_ssm_glu_b': 'grad_w', 'grad_sgu_ln_g': 'grad_w', 'grad_sgu_ln_b': 'grad_w', 'grad_sgu_w': 'grad_w', 'grad_sgu_b': 'grad_w', 'grad_out_norm_ssm_g': 'grad_w', 'grad_out_norm_sgu_g': 'grad_w', 'grad_w_out': 'grad_w', 'grad_norm_ffn_g': 'grad_w', 'grad_w_ffn_in': 'grad_w', 'grad_w_ffn_out': 'grad_w', 'grad_norm_ple_g': 'grad_w', 'grad_w_ple_gate': 'grad_w', 'grad_b_ple_gate': 'grad_w', 'grad_w_ple_proj': 'grad_w', 'grad_final_norm_g': 'grad_w', 'delta_norm_mix_g': 'delta_w', 'delta_w_in': 'delta_w', 'delta_ssm_lambda_re': 'delta_w', 'delta_ssm_lambda_im': 'delta_w', 'delta_ssm_log_step': 'delta_w', 'delta_ssm_b_re': 'delta_w', 'delta_ssm_b_im': 'delta_w', 'delta_ssm_c_re': 'delta_w', 'delta_ssm_c_im': 'delta_w', 'delta_ssm_d': 'delta_w', 'delta_ssm_glu_w': 'delta_w', 'delta_ssm_glu_b': 'delta_w', 'delta_sgu_ln_g': 'delta_w', 'delta_sgu_ln_b': 'delta_w', 'delta_sgu_w': 'delta_w', 'delta_sgu_b': 'delta_w', 'delta_out_norm_ssm_g': 'delta_w', 'delta_out_norm_sgu_g': 'delta_w', 'delta_w_out': 'delta_w', 'delta_norm_ffn_g': 'delta_w', 'delta_w_ffn_in': 'delta_w', 'delta_w_ffn_out': 'delta_w', 'delta_norm_ple_g': 'delta_w', 'delta_w_ple_gate': 'delta_w', 'delta_b_ple_gate': 'delta_w', 'delta_w_ple_proj': 'delta_w', 'delta_final_norm_g': 'delta_w', 'new_m_norm_mix_g': 'new_m', 'new_m_w_in': 'new_m', 'new_m_ssm_lambda_re': 'new_m', 'new_m_ssm_lambda_im': 'new_m', 'new_m_ssm_log_step': 'new_m', 'new_m_ssm_b_re': 'new_m', 'new_m_ssm_b_im': 'new_m', 'new_m_ssm_c_re': 'new_m', 'new_m_ssm_c_im': 'new_m', 'new_m_ssm_d': 'new_m', 'new_m_ssm_glu_w': 'new_m', 'new_m_ssm_glu_b': 'new_m', 'new_m_sgu_ln_g': 'new_m', 'new_m_sgu_ln_b': 'new_m', 'new_m_sgu_w': 'new_m', 'new_m_sgu_b': 'new_m', 'new_m_out_norm_ssm_g': 'new_m', 'new_m_out_norm_sgu_g': 'new_m', 'new_m_w_out': 'new_m', 'new_m_norm_ffn_g': 'new_m', 'new_m_w_ffn_in': 'new_m', 'new_m_w_ffn_out': 'new_m', 'new_m_norm_ple_g': 'new_m', 'new_m_w_ple_gate': 'new_m', 'new_m_b_ple_gate': 'new_m', 'new_m_w_ple_proj': 'new_m', 'new_m_final_norm_g': 'new_m', 'new_v_norm_mix_g': 'new_v', 'new_v_w_in': 'new_v', 'new_v_ssm_lambda_re': 'new_v', 'new_v_ssm_lambda_im': 'new_v', 'new_v_ssm_log_step': 'new_v', 'new_v_ssm_b_re': 'new_v', 'new_v_ssm_b_im': 'new_v', 'new_v_ssm_c_re': 'new_v', 'new_v_ssm_c_im': 'new_v', 'new_v_ssm_d': 'new_v', 'new_v_ssm_glu_w': 'new_v', 'new_v_ssm_glu_b': 'new_v', 'new_v_sgu_ln_g': 'new_v', 'new_v_sgu_ln_b': 'new_v', 'new_v_sgu_w': 'new_v', 'new_v_sgu_b': 'new_v', 'new_v_out_norm_ssm_g': 'new_v', 'new_v_out_norm_sgu_g': 'new_v', 'new_v_w_out': 'new_v', 'new_v_norm_ffn_g': 'new_v', 'new_v_w_ffn_in': 'new_v', 'new_v_w_ffn_out': 'new_v', 'new_v_norm_ple_g': 'new_v', 'new_v_w_ple_gate': 'new_v', 'new_v_b_ple_gate': 'new_v', 'new_v_w_ple_proj': 'new_v', 'new_v_final_norm_g': 'new_v'}


def _forward(args):
    return _fwd_reference(*[args[k] for k in FWD_PARAMS])


def _output_shape():
    def fwd():
        inp = _fwd_setup_inputs(0)
        return _fwd_reference(*[inp[k] for k in FWD_PARAMS])
    out = _jax.eval_shape(fwd)
    return out.shape, out.dtype

N_MICROBATCH = 1
ADAM_LR = 0.001
ADAM_B1 = 0.9
ADAM_B2 = 0.999
ADAM_EPS = 1e-08
ADAM_WD = 0.01
ADAM_STEP = 10
PER_EXAMPLE_BATCH_AXIS = {'x': 0, 'p': 1, 'loss_target': 0}
SHARED_INPUTS = []
_WEIGHT_DTYPES = {'norm_mix_g': _jnp.float32, 'w_in': _jnp.float32, 'ssm_lambda_re': _jnp.float32, 'ssm_lambda_im': _jnp.float32, 'ssm_log_step': _jnp.float32, 'ssm_b_re': _jnp.float32, 'ssm_b_im': _jnp.float32, 'ssm_c_re': _jnp.float32, 'ssm_c_im': _jnp.float32, 'ssm_d': _jnp.float32, 'ssm_glu_w': _jnp.float32, 'ssm_glu_b': _jnp.float32, 'sgu_ln_g': _jnp.float32, 'sgu_ln_b': _jnp.float32, 'sgu_w': _jnp.float32, 'sgu_b': _jnp.float32, 'out_norm_ssm_g': _jnp.float32, 'out_norm_sgu_g': _jnp.float32, 'w_out': _jnp.float32, 'norm_ffn_g': _jnp.float32, 'w_ffn_in': _jnp.float32, 'w_ffn_out': _jnp.float32, 'norm_ple_g': _jnp.float32, 'w_ple_gate': _jnp.float32, 'b_ple_gate': _jnp.float32, 'w_ple_proj': _jnp.float32, 'final_norm_g': _jnp.float32}
MOMENT_SCALE = {'norm_mix_g': 7.384931e-02, 'w_in': 5.959131e-02, 'ssm_lambda_re': 6.371796e-03, 'ssm_lambda_im': 6.991930e-03, 'ssm_log_step': 6.106015e+00, 'ssm_b_re': 4.392555e-03, 'ssm_b_im': 4.341252e-03, 'ssm_c_re': 8.397039e-03, 'ssm_c_im': 8.513606e-03, 'ssm_d': 1.395776e-01, 'ssm_glu_w': 9.505570e-03, 'ssm_glu_b': 3.331427e-02, 'sgu_ln_g': 3.640433e-02, 'sgu_ln_b': 3.631869e-02, 'sgu_w': 3.579627e-02, 'sgu_b': 5.292836e-02, 'out_norm_ssm_g': 6.961110e-02, 'out_norm_sgu_g': 6.750332e-02, 'w_out': 6.546098e-02, 'norm_ffn_g': 4.737343e-02, 'w_ffn_in': 2.013078e-02, 'w_ffn_out': 3.283944e-02, 'norm_ple_g': 1.150337e-02, 'w_ple_gate': 1.167555e-02, 'b_ple_gate': 1.437036e-02, 'w_ple_proj': 2.966719e-02, 'final_norm_g': 1.603665e+01}


def _to_microbatches(a, axis):
    t = _jnp.moveaxis(a, axis, 0)
    t = t.reshape((N_MICROBATCH, t.shape[0] // N_MICROBATCH) + t.shape[1:])
    return _jnp.moveaxis(t, 1, axis + 1)


def setup_inputs(seed: int = 0) -> dict:
    inp = _fwd_setup_inputs(seed)
    key = _jax.random.fold_in(_jax.random.key(seed), 7919)
    shape, _ = _output_shape()
    out = dict(inp)
    out["loss_target"] = _jax.random.normal(_jax.random.fold_in(key, 0), shape, _jnp.float32)
    for i, name in enumerate(TWIN_WEIGHTS):
        w = inp[name].astype(_jnp.float32)
        if MOMENT_SCALE is None:
            s = _jnp.sqrt(_jnp.mean(_jnp.square(w)) + 1e-30)
        else:
            s = MOMENT_SCALE[name]
        km, kv = _jax.random.split(_jax.random.fold_in(key, i + 1))
        out[name] = w
        out["m_" + name] = s * _jax.random.normal(km, w.shape, _jnp.float32)
        out["v_" + name] = (s * s) * _jax.random.uniform(kv, w.shape, _jnp.float32, 0.5, 1.5)
    if N_MICROBATCH > 1:
        for name, axis in PER_EXAMPLE_BATCH_AXIS.items():
            out[name] = _to_microbatches(out[name], axis)
    return {'x': out['x'], 'p': out['p'], 'norm_mix_g': out['norm_mix_g'], 'w_in': out['w_in'], 'ssm_lambda_re': out['ssm_lambda_re'], 'ssm_lambda_im': out['ssm_lambda_im'], 'ssm_log_step': out['ssm_log_step'], 'ssm_b_re': out['ssm_b_re'], 'ssm_b_im': out['ssm_b_im'], 'ssm_c_re': out['ssm_c_re'], 'ssm_c_im': out['ssm_c_im'], 'ssm_d': out['ssm_d'], 'ssm_glu_w': out['ssm_glu_w'], 'ssm_glu_b': out['ssm_glu_b'], 'sgu_ln_g': out['sgu_ln_g'], 'sgu_ln_b': out['sgu_ln_b'], 'sgu_w': out['sgu_w'], 'sgu_b': out['sgu_b'], 'out_norm_ssm_g': out['out_norm_ssm_g'], 'out_norm_sgu_g': out['out_norm_sgu_g'], 'w_out': out['w_out'], 'norm_ffn_g': out['norm_ffn_g'], 'w_ffn_in': out['w_ffn_in'], 'w_ffn_out': out['w_ffn_out'], 'norm_ple_g': out['norm_ple_g'], 'w_ple_gate': out['w_ple_gate'], 'b_ple_gate': out['b_ple_gate'], 'w_ple_proj': out['w_ple_proj'], 'final_norm_g': out['final_norm_g'], 'loss_target': out['loss_target'], 'm_norm_mix_g': out['m_norm_mix_g'], 'm_w_in': out['m_w_in'], 'm_ssm_lambda_re': out['m_ssm_lambda_re'], 'm_ssm_lambda_im': out['m_ssm_lambda_im'], 'm_ssm_log_step': out['m_ssm_log_step'], 'm_ssm_b_re': out['m_ssm_b_re'], 'm_ssm_b_im': out['m_ssm_b_im'], 'm_ssm_c_re': out['m_ssm_c_re'], 'm_ssm_c_im': out['m_ssm_c_im'], 'm_ssm_d': out['m_ssm_d'], 'm_ssm_glu_w': out['m_ssm_glu_w'], 'm_ssm_glu_b': out['m_ssm_glu_b'], 'm_sgu_ln_g': out['m_sgu_ln_g'], 'm_sgu_ln_b': out['m_sgu_ln_b'], 'm_sgu_w': out['m_sgu_w'], 'm_sgu_b': out['m_sgu_b'], 'm_out_norm_ssm_g': out['m_out_norm_ssm_g'], 'm_out_norm_sgu_g': out['m_out_norm_sgu_g'], 'm_w_out': out['m_w_out'], 'm_norm_ffn_g': out['m_norm_ffn_g'], 'm_w_ffn_in': out['m_w_ffn_in'], 'm_w_ffn_out': out['m_w_ffn_out'], 'm_norm_ple_g': out['m_norm_ple_g'], 'm_w_ple_gate': out['m_w_ple_gate'], 'm_b_ple_gate': out['m_b_ple_gate'], 'm_w_ple_proj': out['m_w_ple_proj'], 'm_final_norm_g': out['m_final_norm_g'], 'v_norm_mix_g': out['v_norm_mix_g'], 'v_w_in': out['v_w_in'], 'v_ssm_lambda_re': out['v_ssm_lambda_re'], 'v_ssm_lambda_im': out['v_ssm_lambda_im'], 'v_ssm_log_step': out['v_ssm_log_step'], 'v_ssm_b_re': out['v_ssm_b_re'], 'v_ssm_b_im': out['v_ssm_b_im'], 'v_ssm_c_re': out['v_ssm_c_re'], 'v_ssm_c_im': out['v_ssm_c_im'], 'v_ssm_d': out['v_ssm_d'], 'v_ssm_glu_w': out['v_ssm_glu_w'], 'v_ssm_glu_b': out['v_ssm_glu_b'], 'v_sgu_ln_g': out['v_sgu_ln_g'], 'v_sgu_ln_b': out['v_sgu_ln_b'], 'v_sgu_w': out['v_sgu_w'], 'v_sgu_b': out['v_sgu_b'], 'v_out_norm_ssm_g': out['v_out_norm_ssm_g'], 'v_out_norm_sgu_g': out['v_out_norm_sgu_g'], 'v_w_out': out['v_w_out'], 'v_norm_ffn_g': out['v_norm_ffn_g'], 'v_w_ffn_in': out['v_w_ffn_in'], 'v_w_ffn_out': out['v_w_ffn_out'], 'v_norm_ple_g': out['v_norm_ple_g'], 'v_w_ple_gate': out['v_w_ple_gate'], 'v_b_ple_gate': out['v_b_ple_gate'], 'v_w_ple_proj': out['v_w_ple_proj'], 'v_final_norm_g': out['v_final_norm_g']}


def _loss(weights, diff, rest, loss_target):
    with _jax.named_scope("forward"):
        args = {**rest, TWIN_DIFF_INPUT: diff, **{k: w.astype(_WEIGHT_DTYPES[k]) for k, w in weights.items()}}
        y = _forward(args)
    with _jax.named_scope("loss_head"):
        err = _jnp.square(y.astype(_jnp.float32) - loss_target)
        return 0.5 * _jnp.sum(_jnp.mean(err, axis=-1)) if err.ndim else 0.5 * err


def _adamw(w, g, m, v):
    m = ADAM_B1 * m + (1.0 - ADAM_B1) * g
    v = ADAM_B2 * v + (1.0 - ADAM_B2) * _jnp.square(g)
    m_hat = m / (1.0 - ADAM_B1 ** ADAM_STEP)
    v_hat = v / (1.0 - ADAM_B2 ** ADAM_STEP)
    delta = -ADAM_LR * (m_hat / (_jnp.sqrt(v_hat) + ADAM_EPS) + ADAM_WD * w)
    return delta, m, v


def reference(x, p, norm_mix_g, w_in, ssm_lambda_re, ssm_lambda_im, ssm_log_step, ssm_b_re, ssm_b_im, ssm_c_re, ssm_c_im, ssm_d, ssm_glu_w, ssm_glu_b, sgu_ln_g, sgu_ln_b, sgu_w, sgu_b, out_norm_ssm_g, out_norm_sgu_g, w_out, norm_ffn_g, w_ffn_in, w_ffn_out, norm_ple_g, w_ple_gate, b_ple_gate, w_ple_proj, final_norm_g, loss_target, m_norm_mix_g, m_w_in, m_ssm_lambda_re, m_ssm_lambda_im, m_ssm_log_step, m_ssm_b_re, m_ssm_b_im, m_ssm_c_re, m_ssm_c_im, m_ssm_d, m_ssm_glu_w, m_ssm_glu_b, m_sgu_ln_g, m_sgu_ln_b, m_sgu_w, m_sgu_b, m_out_norm_ssm_g, m_out_norm_sgu_g, m_w_out, m_norm_ffn_g, m_w_ffn_in, m_w_ffn_out, m_norm_ple_g, m_w_ple_gate, m_b_ple_gate, m_w_ple_proj, m_final_norm_g, v_norm_mix_g, v_w_in, v_ssm_lambda_re, v_ssm_lambda_im, v_ssm_log_step, v_ssm_b_re, v_ssm_b_im, v_ssm_c_re, v_ssm_c_im, v_ssm_d, v_ssm_glu_w, v_ssm_glu_b, v_sgu_ln_g, v_sgu_ln_b, v_sgu_w, v_sgu_b, v_out_norm_ssm_g, v_out_norm_sgu_g, v_w_out, v_norm_ffn_g, v_w_ffn_in, v_w_ffn_out, v_norm_ple_g, v_w_ple_gate, v_b_ple_gate, v_w_ple_proj, v_final_norm_g):
    given = dict(x=x, p=p, norm_mix_g=norm_mix_g, w_in=w_in, ssm_lambda_re=ssm_lambda_re, ssm_lambda_im=ssm_lambda_im, ssm_log_step=ssm_log_step, ssm_b_re=ssm_b_re, ssm_b_im=ssm_b_im, ssm_c_re=ssm_c_re, ssm_c_im=ssm_c_im, ssm_d=ssm_d, ssm_glu_w=ssm_glu_w, ssm_glu_b=ssm_glu_b, sgu_ln_g=sgu_ln_g, sgu_ln_b=sgu_ln_b, sgu_w=sgu_w, sgu_b=sgu_b, out_norm_ssm_g=out_norm_ssm_g, out_norm_sgu_g=out_norm_sgu_g, w_out=w_out, norm_ffn_g=norm_ffn_g, w_ffn_in=w_ffn_in, w_ffn_out=w_ffn_out, norm_ple_g=norm_ple_g, w_ple_gate=w_ple_gate, b_ple_gate=b_ple_gate, w_ple_proj=w_ple_proj, final_norm_g=final_norm_g, loss_target=loss_target, m_norm_mix_g=m_norm_mix_g, m_w_in=m_w_in, m_ssm_lambda_re=m_ssm_lambda_re, m_ssm_lambda_im=m_ssm_lambda_im, m_ssm_log_step=m_ssm_log_step, m_ssm_b_re=m_ssm_b_re, m_ssm_b_im=m_ssm_b_im, m_ssm_c_re=m_ssm_c_re, m_ssm_c_im=m_ssm_c_im, m_ssm_d=m_ssm_d, m_ssm_glu_w=m_ssm_glu_w, m_ssm_glu_b=m_ssm_glu_b, m_sgu_ln_g=m_sgu_ln_g, m_sgu_ln_b=m_sgu_ln_b, m_sgu_w=m_sgu_w, m_sgu_b=m_sgu_b, m_out_norm_ssm_g=m_out_norm_ssm_g, m_out_norm_sgu_g=m_out_norm_sgu_g, m_w_out=m_w_out, m_norm_ffn_g=m_norm_ffn_g, m_w_ffn_in=m_w_ffn_in, m_w_ffn_out=m_w_ffn_out, m_norm_ple_g=m_norm_ple_g, m_w_ple_gate=m_w_ple_gate, m_b_ple_gate=m_b_ple_gate, m_w_ple_proj=m_w_ple_proj, m_final_norm_g=m_final_norm_g, v_norm_mix_g=v_norm_mix_g, v_w_in=v_w_in, v_ssm_lambda_re=v_ssm_lambda_re, v_ssm_lambda_im=v_ssm_lambda_im, v_ssm_log_step=v_ssm_log_step, v_ssm_b_re=v_ssm_b_re, v_ssm_b_im=v_ssm_b_im, v_ssm_c_re=v_ssm_c_re, v_ssm_c_im=v_ssm_c_im, v_ssm_d=v_ssm_d, v_ssm_glu_w=v_ssm_glu_w, v_ssm_glu_b=v_ssm_glu_b, v_sgu_ln_g=v_sgu_ln_g, v_sgu_ln_b=v_sgu_ln_b, v_sgu_w=v_sgu_w, v_sgu_b=v_sgu_b, v_out_norm_ssm_g=v_out_norm_ssm_g, v_out_norm_sgu_g=v_out_norm_sgu_g, v_w_out=v_w_out, v_norm_ffn_g=v_norm_ffn_g, v_w_ffn_in=v_w_ffn_in, v_w_ffn_out=v_w_ffn_out, v_norm_ple_g=v_norm_ple_g, v_w_ple_gate=v_w_ple_gate, v_b_ple_gate=v_b_ple_gate, v_w_ple_proj=v_w_ple_proj, v_final_norm_g=v_final_norm_g)
    weights = {n: given[n] for n in TWIN_WEIGHTS}
    shared = {n: given[n] for n in SHARED_INPUTS}
    per_example = {n: given[n] for n in ['x', 'p']}
    grad_fn = _jax.value_and_grad(_loss, argnums=(0, 1))

    def one_microbatch(ex, loss_target):
        ex = dict(ex)
        diff = ex.pop(TWIN_DIFF_INPUT)
        return grad_fn(weights, diff, {**shared, **ex}, loss_target)

    if N_MICROBATCH == 1:
        loss, (grad_w, grad_x) = one_microbatch(per_example, given["loss_target"])
    else:
        def body(carry, xs):
            loss_sum, grad_sum = carry
            l_k, (gw_k, gx_k) = one_microbatch(xs[0], xs[1])
            with _jax.named_scope("update"):
                return (loss_sum + l_k, _jax.tree.map(_jnp.add, grad_sum, gw_k)), gx_k

        init = (_jnp.zeros((), _jnp.float32), _jax.tree.map(_jnp.zeros_like, weights))
        (loss, grad_w), grad_x = _jax.lax.scan(body, init, (per_example, given["loss_target"]))
    with _jax.named_scope("update"):
        delta_w, new_m, new_v = {}, {}, {}
        for n in TWIN_WEIGHTS:
            delta_w[n], new_m[n], new_v[n] = _adamw(weights[n], grad_w[n], given["m_" + n], given["v_" + n])
    return (loss, grad_x, *[grad_w[n] for n in TWIN_WEIGHTS], *[delta_w[n] for n in TWIN_WEIGHTS],
            *[new_m[n] for n in TWIN_WEIGHTS], *[new_v[n] for n in TWIN_WEIGHTS])
```

```python
import functools
import math

import jax
import jax.numpy as jnp
from jax import lax
from jax.experimental import pallas as pl
from jax.experimental.pallas import tpu as pltpu

F32 = jnp.float32
BF16 = jnp.bfloat16
MESH = pl.DeviceIdType.MESH
ANY = pl.BlockSpec(memory_space=pl.ANY)

N_DEV = 8
EPS = 1e-6
LAMBDA_RE_MAX = -1e-4
SSM_GROUP = 16
SSM_STATE = 64
SSM_SUPER = 16
SGU_CHUNK = 128
ADAM_LR, ADAM_B1, ADAM_B2, ADAM_EPS, ADAM_WD, ADAM_STEP = 0.001, 0.9, 0.999, 1e-08, 0.01, 10
VMEM_LIMIT = 52 * 1024 * 1024
LANE = 128
PACK_ALIGN = 8 * LANE

_GELU_C = math.sqrt(2.0 / math.pi)


def _params(sem=None):
    return pltpu.CompilerParams(dimension_semantics=sem, vmem_limit_bytes=VMEM_LIMIT)


def _tile(dim, pref, unit=LANE):
    if dim <= pref:
        return dim
    t = (pref // unit) * unit
    while t >= unit:
        if dim % t == 0:
            return t
        t -= unit
    return dim


def _gelu(x):
    return 0.5 * x * (1.0 + jnp.tanh(_GELU_C * (x + 0.044715 * x * x * x)))


def _gelu_grad(x):
    t = jnp.tanh(_GELU_C * (x + 0.044715 * x * x * x))
    return 0.5 * (1.0 + t) + 0.5 * x * (1.0 - t * t) * (_GELU_C * (1.0 + 3.0 * 0.044715 * x * x))


def _rms(x):
    return lax.rsqrt(jnp.mean(x * x, axis=-1, keepdims=True) + EPS)


def _rmsnorm_bwd(dy, x, r, g):
    dyg = dy * g
    return r * dyg - x * (r * r * r) * jnp.mean(dyg * x, axis=-1, keepdims=True)


def _rowsum(v):
    return jnp.sum(v, axis=0, keepdims=True)


def mm_nn(a, b, *, name, out_dtype, tm, tn, tk, residual=None):
    M, K = a.shape
    blocked = b.ndim == 3
    if blocked:
        nb, _, Nb = b.shape
        N = nb * Nb
        tn = _tile(Nb, tn)
        per = Nb // tn
    else:
        N = b.shape[1]
        tn = _tile(N, tn)
    tm, tk = _tile(M, tm, 8), _tile(K, tk)
    nj, ni, nk = N // tn, M // tm, K // tk
    has_res = residual is not None

    def body(*refs):
        a_ref, b_ref = refs[0], refs[1]
        r_ref = refs[2] if has_res else None
        o_ref = refs[3] if has_res else refs[2]

        def finish(acc):
            if has_res:
                acc = acc + r_ref[...]
            o_ref[...] = acc.astype(o_ref.dtype)

        part = jnp.dot(a_ref[...], b_ref[...], preferred_element_type=F32)
        if nk == 1:
            finish(part)
        else:
            acc_ref = refs[-1]
            k = pl.program_id(2)

            @pl.when(k == 0)
            def _():
                acc_ref[...] = part

            @pl.when(k > 0)
            def _():
                acc_ref[...] += part

            @pl.when(k == nk - 1)
            def _():
                finish(acc_ref[...])

    if blocked:
        b_spec = pl.BlockSpec((None, tk, tn), lambda j, i, k: (j // per, k, j % per))
    else:
        b_spec = pl.BlockSpec((tk, tn), lambda j, i, k: (k, j))
    in_specs = [pl.BlockSpec((tm, tk), lambda j, i, k: (i, k)), b_spec]
    args = [a, b]
    if has_res:
        in_specs.append(pl.BlockSpec((tm, tn), lambda j, i, k: (i, j)))
        args.append(residual)
    return pl.pallas_call(
        body, name=name, grid=(nj, ni, nk),
        in_specs=in_specs,
        out_specs=pl.BlockSpec((tm, tn), lambda j, i, k: (i, j)),
        out_shape=jax.ShapeDtypeStruct((M, N), out_dtype),
        scratch_shapes=[] if nk == 1 else [pltpu.VMEM((tm, tn), F32)],
        compiler_params=_params(("parallel", "parallel", "arbitrary")),
    )(*args)


def mm_nt(a, w, *, name, out_dtype, tm, tko, tc):
    M, N = a.shape
    blocked = w.ndim == 3
    if blocked:
        nb, Ko, Nb = w.shape
        tc = _tile(Nb, tc)
        per = Nb // tc
    else:
        Ko = w.shape[0]
        tc = _tile(N, tc)
    tm, tko = _tile(M, tm, 8), _tile(Ko, tko)
    njo, ni, nc = Ko // tko, M // tm, N // tc

    def body(a_ref, w_ref, o_ref, *scratch):
        part = lax.dot_general(a_ref[...], w_ref[...], (((1,), (1,)), ((), ())),
                               preferred_element_type=F32)
        if nc == 1:
            o_ref[...] = part.astype(o_ref.dtype)
        else:
            acc_ref = scratch[0]
            c = pl.program_id(2)

            @pl.when(c == 0)
            def _():
                acc_ref[...] = part

            @pl.when(c > 0)
            def _():
                acc_ref[...] += part

            @pl.when(c == nc - 1)
            def _():
                o_ref[...] = acc_ref[...].astype(o_ref.dtype)

    if blocked:
        w_spec = pl.BlockSpec((None, tko, tc), lambda j, i, c: (c // per, j, c % per))
    else:
        w_spec = pl.BlockSpec((tko, tc), lambda j, i, c: (j, c))
    return pl.pallas_call(
        body, name=name, grid=(njo, ni, nc),
        in_specs=[pl.BlockSpec((tm, tc), lambda j, i, c: (i, c)), w_spec],
        out_specs=pl.BlockSpec((tm, tko), lambda j, i, c: (i, j)),
        out_shape=jax.ShapeDtypeStruct((M, Ko), out_dtype),
        scratch_shapes=[] if nc == 1 else [pltpu.VMEM((tm, tko), F32)],
        compiler_params=_params(("parallel", "parallel", "arbitrary")),
    )(a, w)


def mm_tn(a, g, *, name, out_dtype, tm, tko, tno, out_blocks=None):
    M, K = a.shape
    N = g.shape[1]
    if out_blocks:
        Nb = N // out_blocks
        tno = _tile(Nb, tno)
        per = Nb // tno
    else:
        tno = _tile(N, tno)
    tm, tko = _tile(M, tm), _tile(K, tko)
    njo, njn, nm = K // tko, N // tno, M // tm

    def body(a_ref, g_ref, o_ref, *scratch):
        part = lax.dot_general(a_ref[...], g_ref[...], (((0,), (0,)), ((), ())),
                               preferred_element_type=F32)
        if nm == 1:
            o_ref[...] = part.astype(o_ref.dtype)
        else:
            acc_ref = scratch[0]
            m = pl.program_id(2)

            @pl.when(m == 0)
            def _():
                acc_ref[...] = part

            @pl.when(m > 0)
            def _():
                acc_ref[...] += part

            @pl.when(m == nm - 1)
            def _():
                o_ref[...] = acc_ref[...].astype(o_ref.dtype)

    if out_blocks:
        o_spec = pl.BlockSpec((None, tko, tno), lambda jo, jn, m: (jn // per, jo, jn % per))
        o_shape = jax.ShapeDtypeStruct((out_blocks, K, Nb), out_dtype)
    else:
        o_spec = pl.BlockSpec((tko, tno), lambda jo, jn, m: (jo, jn))
        o_shape = jax.ShapeDtypeStruct((K, N), out_dtype)
    return pl.pallas_call(
        body, name=name, grid=(njo, njn, nm),
        in_specs=[pl.BlockSpec((tm, tko), lambda jo, jn, m: (m, jo)),
                  pl.BlockSpec((tm, tno), lambda jo, jn, m: (m, jn))],
        out_specs=o_spec, out_shape=o_shape,
        scratch_shapes=[] if nm == 1 else [pltpu.VMEM((tko, tno), F32)],
        compiler_params=_params(("parallel", "parallel", "arbitrary")),
    )(a, g)


def ffn_in_swiglu(h, w_blk, *, name, tm):
    M, K = h.shape
    nb, _, Nb = w_blk.shape
    nh = nb // 2
    F = nh * Nb
    tm = _tile(M, tm, 8)

    def body(h_ref, wg_ref, wu_ref, act_ref, gate_ref, up_ref):
        hv = h_ref[...]
        gate = jnp.dot(hv, wg_ref[...], preferred_element_type=F32)
        up = jnp.dot(hv, wu_ref[...], preferred_element_type=F32)
        gate_ref[...] = gate
        up_ref[...] = up
        act_ref[...] = (gate * jax.nn.sigmoid(gate) * up).astype(act_ref.dtype)

    o_spec = pl.BlockSpec((tm, Nb), lambda j, i: (i, j))
    return pl.pallas_call(
        body, name=name, grid=(nh, M // tm),
        in_specs=[pl.BlockSpec((tm, K), lambda j, i: (i, 0)),
                  pl.BlockSpec((None, K, Nb), lambda j, i: (j, 0, 0)),
                  pl.BlockSpec((None, K, Nb), lambda j, i: (j + nh, 0, 0))],
        out_specs=[o_spec, o_spec, o_spec],
        out_shape=[jax.ShapeDtypeStruct((M, F), BF16), jax.ShapeDtypeStruct((M, F), F32),
                   jax.ShapeDtypeStruct((M, F), F32)],
        compiler_params=_params(("parallel", "parallel")),
    )(h, w_blk, w_blk)


def _row_spec(tm, d, col=0):
    return pl.BlockSpec((tm, d), lambda i: (i, col))


def _vec_spec(d):
    return pl.BlockSpec((1, d), lambda i: (0, 0))


def norm_fwd(x, g, *, name, tm=256):
    L, D = x.shape
    tm = _tile(L, tm, 8)

    def body(x_ref, g_ref, h_ref):
        xv = x_ref[...]
        h_ref[...] = (xv * _rms(xv) * g_ref[...]).astype(h_ref.dtype)

    return pl.pallas_call(
        body, name=name, grid=(L // tm,),
        in_specs=[_row_spec(tm, D), _vec_spec(D)],
        out_specs=_row_spec(tm, D),
        out_shape=jax.ShapeDtypeStruct((L, D), BF16),
        compiler_params=_params(("parallel",)),
    )(x, g)


def norm_bwd(dh, xin, g, dres, *, name, want_bf16, tm=128):
    L, D = xin.shape
    tm = _tile(L, tm, 8)

    def body(dh_ref, x_ref, g_ref, dres_ref, dx_ref, *rest):
        dg_ref = rest[-1]
        xv, dhv = x_ref[...], dh_ref[...]
        r = _rms(xv)
        dx = dres_ref[...] + _rmsnorm_bwd(dhv, xv, r, g_ref[...])
        dx_ref[...] = dx
        if want_bf16:
            rest[0][...] = dx.astype(BF16)
        part = _rowsum(dhv * xv * r)

        @pl.when(pl.program_id(0) == 0)
        def _():
            dg_ref[...] = part

        @pl.when(pl.program_id(0) > 0)
        def _():
            dg_ref[...] += part

    out_specs = [_row_spec(tm, D)] + ([_row_spec(tm, D)] if want_bf16 else []) + [_vec_spec(D)]
    out_shape = ([jax.ShapeDtypeStruct((L, D), F32)]
                 + ([jax.ShapeDtypeStruct((L, D), BF16)] if want_bf16 else [])
                 + [jax.ShapeDtypeStruct((1, D), F32)])
    return pl.pallas_call(
        body, name=name, grid=(L // tm,),
        in_specs=[_row_spec(tm, D), _row_spec(tm, D), _vec_spec(D), _row_spec(tm, D)],
        out_specs=out_specs, out_shape=out_shape,
        compiler_params=_params(("arbitrary",)),
    )(dh, xin, g, dres)


def glu_pre(y0, *, name, tm=256):
    L, D = y0.shape
    tm = _tile(L, tm, 8)

    def body(y_ref, o_ref):
        o_ref[...] = _gelu(y_ref[...]).astype(o_ref.dtype)

    return pl.pallas_call(
        body, name=name, grid=(L // tm,),
        in_specs=[_row_spec(tm, D)], out_specs=_row_spec(tm, D),
        out_shape=jax.ShapeDtypeStruct((L, D), BF16),
        compiler_params=_params(("parallel",)),
    )(y0)


def glu_post(y0, t, b_glu, g_a, *, name, tm=256):
    L, D = y0.shape
    tm = _tile(L, tm, 8)

    def body(y_ref, t_ref, b_ref, g_ref, o_ref):
        ya = _gelu(y_ref[...]) * jax.nn.sigmoid(t_ref[...] + b_ref[...])
        o_ref[...] = (ya * _rms(ya) * g_ref[...]).astype(o_ref.dtype)

    return pl.pallas_call(
        body, name=name, grid=(L // tm,),
        in_specs=[_row_spec(tm, D), _row_spec(tm, D), _vec_spec(D), _vec_spec(D)],
        out_specs=_row_spec(tm, D),
        out_shape=jax.ShapeDtypeStruct((L, D), BF16),
        compiler_params=_params(("parallel",)),
    )(y0, t, b_glu, g_a)


def glu_post_bwd(y0, t, b_glu, g_a, dycat, *, name, tm=128):
    L, D = y0.shape
    tm = _tile(L, tm, 8)

    def body(y_ref, t_ref, b_ref, g_ref, dn_ref, dt_ref, dd_ref, dga_ref, dbg_ref):
        ya1 = _gelu(y_ref[...])
        sg = jax.nn.sigmoid(t_ref[...] + b_ref[...])
        ya = ya1 * sg
        ra = _rms(ya)
        dn = dn_ref[...]
        dya = _rmsnorm_bwd(dn, ya, ra, g_ref[...])
        dt = dya * ya1 * sg * (1.0 - sg)
        dt_ref[...] = dt.astype(BF16)
        dd_ref[...] = dya * sg
        p_ga, p_bg = _rowsum(dn * ya * ra), _rowsum(dt)

        @pl.when(pl.program_id(0) == 0)
        def _():
            dga_ref[...] = p_ga
            dbg_ref[...] = p_bg

        @pl.when(pl.program_id(0) > 0)
        def _():
            dga_ref[...] += p_ga
            dbg_ref[...] += p_bg

    return pl.pallas_call(
        body, name=name, grid=(L // tm,),
        in_specs=[_row_spec(tm, D), _row_spec(tm, D), _vec_spec(D), _vec_spec(D), _row_spec(tm, D, 0)],
        out_specs=[_row_spec(tm, D), _row_spec(tm, D), _vec_spec(D), _vec_spec(D)],
        out_shape=[jax.ShapeDtypeStruct((L, D), BF16), jax.ShapeDtypeStruct((L, D), F32),
                   jax.ShapeDtypeStruct((1, D), F32), jax.ShapeDtypeStruct((1, D), F32)],
        compiler_params=_params(("arbitrary",)),
    )(y0, t, b_glu, g_a, dycat)


def head_and_loss(x2, gpre, b_g, pp, g_f, tgt, *, name, tm=128):
    L, D = x2.shape
    tm = _tile(L, tm, 8)

    def body(x2_ref, gp_ref, bg_ref, pp_ref, gf_ref, tg_ref,
             dx3_ref, dpre_ref, dpp_ref, loss_ref, dgf_ref, dbg_ref):
        gate = jax.nn.sigmoid(gp_ref[...] + bg_ref[...])
        ppv = pp_ref[...]
        x3 = x2_ref[...] + gate * ppv
        r = _rms(x3)
        xn = x3 * r
        gf = gf_ref[...]
        err = xn * gf - tg_ref[...]
        loss = 0.5 * jnp.sum(jnp.mean(err * err, axis=-1, keepdims=True), axis=0, keepdims=True)
        dout = err * (1.0 / D)
        dx3 = _rmsnorm_bwd(dout, x3, r, gf)
        dx3_ref[...] = dx3
        dpre = dx3 * ppv * gate * (1.0 - gate)
        dpre_ref[...] = dpre.astype(BF16)
        dpp_ref[...] = (dx3 * gate).astype(BF16)
        p_gf, p_bg = _rowsum(dout * xn), _rowsum(dpre)
        p_loss = jnp.broadcast_to(loss, loss_ref.shape)

        @pl.when(pl.program_id(0) == 0)
        def _():
            loss_ref[...] = p_loss
            dgf_ref[...] = p_gf
            dbg_ref[...] = p_bg

        @pl.when(pl.program_id(0) > 0)
        def _():
            loss_ref[...] += p_loss
            dgf_ref[...] += p_gf
            dbg_ref[...] += p_bg

    rs = _row_spec(tm, D)
    return pl.pallas_call(
        body, name=name, grid=(L // tm,),
        in_specs=[rs, rs, _vec_spec(D), rs, _vec_spec(D), rs],
        out_specs=[rs, rs, rs, pl.BlockSpec((8, LANE), lambda i: (0, 0)), _vec_spec(D), _vec_spec(D)],
        out_shape=[jax.ShapeDtypeStruct((L, D), F32), jax.ShapeDtypeStruct((L, D), BF16),
                   jax.ShapeDtypeStruct((L, D), BF16), jax.ShapeDtypeStruct((8, LANE), F32),
                   jax.ShapeDtypeStruct((1, D), F32), jax.ShapeDtypeStruct((1, D), F32)],
        compiler_params=_params(("arbitrary",)),
    )(x2, gpre, b_g, pp, g_f, tgt)


def swiglu_bwd(dact, gate, up, *, name, tm=256, tn=512):
    L, F = dact.shape
    tm, tn = _tile(L, tm, 8), _tile(F, tn)

    def body(da_ref, g_ref, u_ref, dg_ref, du_ref):
        da, gv = da_ref[...], g_ref[...]
        sg = jax.nn.sigmoid(gv)
        dg_ref[...] = (da * u_ref[...] * sg * (1.0 + gv * (1.0 - sg))).astype(BF16)
        du_ref[...] = (da * gv * sg).astype(BF16)

    spec = pl.BlockSpec((tm, tn), lambda i, j: (i, j))
    return pl.pallas_call(
        body, name=name, grid=(L // tm, F // tn),
        in_specs=[spec, spec, spec], out_specs=[spec, spec],
        out_shape=[jax.ShapeDtypeStruct((L, F), BF16), jax.ShapeDtypeStruct((L, F), BF16)],
        compiler_params=_params(("parallel", "parallel")),
    )(dact, gate, up)


def _sgu_forward_values(zu, zv, lng, lnb, w_ref, bs_ref, s_scr, heads, hd):
    u1 = _gelu(zu)
    v1 = _gelu(zv)
    xc = v1 - jnp.mean(v1, axis=-1, keepdims=True)
    r = lax.rsqrt(jnp.mean(xc * xc, axis=-1, keepdims=True) + EPS)
    xhat = xc * r
    v2 = xhat * lng + lnb
    tril = (lax.broadcasted_iota(jnp.int32, (SGU_CHUNK, SGU_CHUNK), 0)
            >= lax.broadcasted_iota(jnp.int32, (SGU_CHUNK, SGU_CHUNK), 1))
    for h in range(heads):
        wm = jnp.where(tril, w_ref[h], 0.0).astype(BF16)
        cols = slice(h * hd, (h + 1) * hd)
        s_scr[:, cols] = jnp.dot(wm, v2[:, cols].astype(BF16), preferred_element_type=F32) + bs_ref[h]
    return u1, xhat, r, v2, tril


def sgu_fwd(z, ln_g, ln_b, w_s, b_s, g_b, *, name, d_sgu):
    L = z.shape[0]
    heads = w_s.shape[0]
    hd = d_sgu // heads

    def body(zu_ref, zv_ref, lng_ref, lnb_ref, w_ref, bs_ref, gb_ref, o_ref, s_scr):
        u1, _, _, _, _ = _sgu_forward_values(zu_ref[...], zv_ref[...], lng_ref[...], lnb_ref[...],
                                             w_ref, bs_ref, s_scr, heads, hd)
        yb = u1 * s_scr[...]
        o_ref[...] = (yb * _rms(yb) * gb_ref[...]).astype(o_ref.dtype)

    blk = lambda col: pl.BlockSpec((SGU_CHUNK, d_sgu), lambda n: (n, col))
    return pl.pallas_call(
        body, name=name, grid=(L // SGU_CHUNK,),
        in_specs=[blk(1), blk(2), _vec_spec(d_sgu), _vec_spec(d_sgu),
                  pl.BlockSpec(w_s.shape, lambda n: (0, 0, 0)), pl.BlockSpec(b_s.shape, lambda n: (0, 0, 0)),
                  _vec_spec(d_sgu)],
        out_specs=blk(0),
        out_shape=jax.ShapeDtypeStruct((L, d_sgu), BF16),
        scratch_shapes=[pltpu.VMEM((SGU_CHUNK, d_sgu), F32)],
        compiler_params=_params(("parallel",)),
    )(z, z, ln_g, ln_b, w_s, b_s, g_b)


def sgu_bwd(z, dycat, ln_g, ln_b, w_s, b_s, g_b, *, name, d_sgu):
    L = z.shape[0]
    heads = w_s.shape[0]
    hd = d_sgu // heads

    def body(zu_ref, zv_ref, dn_ref, lng_ref, lnb_ref, w_ref, bs_ref, gb_ref,
             dzu_ref, dzv_ref, dw_ref, dbs_ref, dlng_ref, dlnb_ref, dgb_ref, s_scr, dv_scr):
        first = pl.program_id(0) == 0
        zu, zv, lng = zu_ref[...], zv_ref[...], lng_ref[...]
        u1, xhat, r, v2, tril = _sgu_forward_values(zu, zv, lng, lnb_ref[...], w_ref, bs_ref, s_scr, heads, hd)
        s = s_scr[...]
        yb = u1 * s
        rb = _rms(yb)
        dn = dn_ref[...]
        dyb = _rmsnorm_bwd(dn, yb, rb, gb_ref[...])
        dzu_ref[...] = (dyb * s * _gelu_grad(zu)).astype(BF16)
        ds = dyb * u1
        for h in range(heads):
            cols = slice(h * hd, (h + 1) * hd)
            ds_h = ds[:, cols]
            ds_hb = ds_h.astype(BF16)
            wm = jnp.where(tril, w_ref[h], 0.0).astype(BF16)
            dw_h = jnp.where(tril, lax.dot_general(ds_hb, v2[:, cols].astype(BF16), (((1,), (1,)), ((), ())),
                                                   preferred_element_type=F32), 0.0)
            db_h = jnp.sum(ds_h, axis=1, keepdims=True)
            dv_scr[:, cols] = lax.dot_general(wm, ds_hb, (((0,), (0,)), ((), ())), preferred_element_type=F32)

            @pl.when(first)
            def _():
                dw_ref[h] = dw_h
                dbs_ref[h] = db_h

            @pl.when(jnp.logical_not(first))
            def _():
                dw_ref[h] += dw_h
                dbs_ref[h] += db_h

        dv2 = dv_scr[...]
        dxh = dv2 * lng
        dv1 = r * (dxh - jnp.mean(dxh, axis=-1, keepdims=True)
                   - xhat * jnp.mean(dxh * xhat, axis=-1, keepdims=True))
        dzv_ref[...] = (dv1 * _gelu_grad(zv)).astype(BF16)
        p_lng, p_lnb, p_gb = _rowsum(dv2 * xhat), _rowsum(dv2), _rowsum(dn * yb * rb)

        @pl.when(first)
        def _():
            dlng_ref[...] = p_lng
            dlnb_ref[...] = p_lnb
            dgb_ref[...] = p_gb

        @pl.when(jnp.logical_not(first))
        def _():
            dlng_ref[...] += p_lng
            dlnb_ref[...] += p_lnb
            dgb_ref[...] += p_gb

    blk = lambda col: pl.BlockSpec((SGU_CHUNK, d_sgu), lambda n: (n, col))
    full3 = lambda shape: pl.BlockSpec(shape, lambda n: (0, 0, 0))
    return pl.pallas_call(
        body, name=name, grid=(L // SGU_CHUNK,),
        in_specs=[blk(1), blk(2), blk(1), _vec_spec(d_sgu), _vec_spec(d_sgu),
                  full3(w_s.shape), full3(b_s.shape), _vec_spec(d_sgu)],
        out_specs=[blk(0), blk(0), full3(w_s.shape), full3(b_s.shape),
                   _vec_spec(d_sgu), _vec_spec(d_sgu), _vec_spec(d_sgu)],
        out_shape=[jax.ShapeDtypeStruct((L, d_sgu), BF16), jax.ShapeDtypeStruct((L, d_sgu), BF16),
                   jax.ShapeDtypeStruct(w_s.shape, F32), jax.ShapeDtypeStruct(b_s.shape, F32),
                   jax.ShapeDtypeStruct((1, d_sgu), F32), jax.ShapeDtypeStruct((1, d_sgu), F32),
                   jax.ShapeDtypeStruct((1, d_sgu), F32)],
        scratch_shapes=[pltpu.VMEM((SGU_CHUNK, d_sgu), F32), pltpu.VMEM((SGU_CHUNK, d_sgu), F32)],
        compiler_params=_params(("arbitrary",)),
    )(z, z, dycat, ln_g, ln_b, w_s, b_s, g_b)


def _disc_lambda(lam_re, lam_im, log_step):
    lr = jnp.minimum(lam_re, LAMBDA_RE_MAX)
    li = lam_im
    dt = jnp.exp(log_step)
    mag = jnp.exp(lr * dt)
    ang = li * dt
    a_re = mag * jnp.cos(ang)
    a_im = mag * jnp.sin(ang)
    nr = a_re - 1.0
    ni = a_im
    den = lr * lr + li * li
    return a_re, a_im, (nr * lr + ni * li) / den, (ni * lr - nr * li) / den


def _disc_b(q_re, q_im, b_re, b_im):
    return q_re * b_re - q_im * b_im, q_re * b_im + q_im * b_re


def disc_lambda_fwd(lam_re, lam_im, log_step, *, name):
    def body(lr_ref, li_ref, ls_ref, ar_ref, ai_ref, qr_ref, qi_ref):
        ar_ref[...], ai_ref[...], qr_ref[...], qi_ref[...] = _disc_lambda(lr_ref[...], li_ref[...], ls_ref[...])

    sd = jax.ShapeDtypeStruct(lam_re.shape, F32)
    return pl.pallas_call(body, name=name, out_shape=[sd, sd, sd, sd], compiler_params=_params())(
        lam_re, lam_im, log_step)


def disc_lambda_bwd(lam_re, lam_im, log_step, cts, *, name):
    def body(lr_ref, li_ref, ls_ref, c0, c1, c2, c3, dlr_ref, dli_ref, dls_ref):
        _, vjp = jax.vjp(_disc_lambda, lr_ref[...], li_ref[...], ls_ref[...])
        dlr_ref[...], dli_ref[...], dls_ref[...] = vjp((c0[...], c1[...], c2[...], c3[...]))

    sd = jax.ShapeDtypeStruct(lam_re.shape, F32)
    return pl.pallas_call(body, name=name, out_shape=[sd, sd, jax.ShapeDtypeStruct(log_step.shape, F32)],
                          compiler_params=_params())(lam_re, lam_im, log_step, *cts)


def disc_b_fwd(q_re, q_im, b_re, b_im, *, name):
    def body(qr_ref, qi_ref, br_ref, bi_ref, or_ref, oi_ref):
        or_ref[...], oi_ref[...] = _disc_b(qr_ref[...], qi_ref[...], br_ref[...], bi_ref[...])

    sd = jax.ShapeDtypeStruct(b_re.shape, F32)
    return pl.pallas_call(body, name=name, out_shape=[sd, sd], compiler_params=_params())(q_re, q_im, b_re, b_im)


def disc_b_bwd(q_re, q_im, b_re, b_im, ct_re, ct_im, *, name):
    def body(qr_ref, qi_ref, br_ref, bi_ref, cr_ref, ci_ref, dqr_ref, dqi_ref, dbr_ref, dbi_ref):
        _, vjp = jax.vjp(_disc_b, qr_ref[...], qi_ref[...], br_ref[...], bi_ref[...])
        dqr_ref[...], dqi_ref[...], dbr_ref[...], dbi_ref[...] = vjp((cr_ref[...], ci_ref[...]))

    sq, sb = jax.ShapeDtypeStruct(q_re.shape, F32), jax.ShapeDtypeStruct(b_re.shape, F32)
    return pl.pallas_call(body, name=name, out_shape=[sq, sq, sb, sb], compiler_params=_params())(
        q_re, q_im, b_re, b_im, ct_re, ct_im)


def _lti_scan(xr, xi, ar, ai, reverse):
    T = xr.shape[0]
    row = lax.broadcasted_iota(jnp.int32, xr.shape, 0)
    k = 1
    while k < T:
        shift = T - k if reverse else k
        keep = (row < T - k) if reverse else (row >= k)
        sr = jnp.where(keep, pltpu.roll(xr, shift, 0), 0.0)
        si = jnp.where(keep, pltpu.roll(xi, shift, 0), 0.0)
        xr, xi = xr + ar * sr - ai * si, xi + ar * si + ai * sr
        ar, ai = ar * ar - ai * ai, 2.0 * ar * ai
        k *= 2
    return xr, xi


def _ssm_chunk(L):
    return _tile(L, 256, 8)


def ssm_fwd(z, bs_re, bs_im, cs_re, cs_im, a_re, a_im, d, *, name):
    L = z.shape[0]
    NK, C, S = bs_re.shape
    T = _ssm_chunk(L)

    def body(u_ref, br_ref, bi_ref, cr_ref, ci_ref, ar_ref, ai_ref, d_ref,
             y_ref, sr_ref, si_ref, car_re, car_im, pw_re, pw_im):
        i = pl.program_id(1)
        ar, ai = ar_ref[...], ai_ref[...]

        @pl.when(i == 0)
        def _():
            row = lax.broadcasted_iota(jnp.int32, (T, S), 0)
            pr, pi = _lti_scan(jnp.where(row == 0, ar, 0.0), jnp.where(row == 0, ai, 0.0), ar, ai, False)
            pw_re[...] = pr
            pw_im[...] = pi
            car_re[...] = jnp.zeros_like(car_re)
            car_im[...] = jnp.zeros_like(car_im)

        u = u_ref[...]
        ub = u.astype(BF16)
        xr = jnp.dot(ub, br_ref[...], preferred_element_type=F32)
        xi = jnp.dot(ub, bi_ref[...], preferred_element_type=F32)
        xr, xi = _lti_scan(xr, xi, ar, ai, False)
        cr, ci = car_re[...], car_im[...]
        pr, pi = pw_re[...], pw_im[...]
        s_re = xr + pr * cr - pi * ci
        s_im = xi + pr * ci + pi * cr
        sr_ref[...] = s_re
        si_ref[...] = s_im
        car_re[...] = s_re[T - 1:T, :]
        car_im[...] = s_im[T - 1:T, :]
        y_ref[...] = (jnp.dot(s_re.astype(BF16), cr_ref[...], preferred_element_type=F32)
                      - jnp.dot(s_im.astype(BF16), ci_ref[...], preferred_element_type=F32)
                      + d_ref[...] * u)

    kspec = lambda shape: pl.BlockSpec((None,) + shape, lambda k, i: (k, 0, 0))
    return pl.pallas_call(
        body, name=name, grid=(NK, L // T),
        in_specs=[pl.BlockSpec((T, C), lambda k, i: (i, k)),
                  kspec((C, S)), kspec((C, S)), kspec((S, C)), kspec((S, C)),
                  kspec((1, S)), kspec((1, S)), kspec((1, C))],
        out_specs=[pl.BlockSpec((T, C), lambda k, i: (i, k)),
                   pl.BlockSpec((T, S), lambda k, i: (i, k)), pl.BlockSpec((T, S), lambda k, i: (i, k))],
        out_shape=[jax.ShapeDtypeStruct((L, NK * C), F32), jax.ShapeDtypeStruct((L, NK * S), F32),
                   jax.ShapeDtypeStruct((L, NK * S), F32)],
        scratch_shapes=[pltpu.VMEM((1, S), F32), pltpu.VMEM((1, S), F32),
                        pltpu.VMEM((T, S), F32), pltpu.VMEM((T, S), F32)],
        compiler_params=_params(("parallel", "arbitrary")),
    )(z, bs_re, bs_im, cs_re, cs_im, a_re, a_im, d)


def ssm_bwd(z, y0, dd_direct, dd_mm, s_re, s_im, bst_re, bst_im, cst_re, cst_im, a_re, a_im, d, *, name):
    L = z.shape[0]
    NK, S, C = bst_re.shape
    T = _ssm_chunk(L)
    nchunk = L // T
    tail = T // 8

    def body(u_ref, y_ref, d1_ref, d2_ref, sr_ref, si_ref, pr_ref, pi_ref,
             btr_ref, bti_ref, ctr_ref, cti_ref, ar_ref, ai_ref, d_ref,
             du_ref, dbr_ref, dbi_ref, dcr_ref, dci_ref, dar_ref, dai_ref, dd_ref,
             car_re, car_im, pw_re, pw_im):
        i = pl.program_id(1)
        chunk = nchunk - 1 - i
        ar, ai = ar_ref[...], ai_ref[...]
        nai = -ai
        row = lax.broadcasted_iota(jnp.int32, (T, S), 0)

        @pl.when(i == 0)
        def _():
            qr, qi = _lti_scan(jnp.where(row == T - 1, ar, 0.0), jnp.where(row == T - 1, nai, 0.0), ar, nai, True)
            pw_re[...] = qr
            pw_im[...] = qi
            car_re[...] = jnp.zeros_like(car_re)
            car_im[...] = jnp.zeros_like(car_im)

        u = u_ref[...]
        dy = (d1_ref[...] + d2_ref[...]) * _gelu_grad(y_ref[...])
        dyb = dy.astype(BF16)
        gr = jnp.dot(dyb, ctr_ref[...], preferred_element_type=F32)
        gi = -jnp.dot(dyb, cti_ref[...], preferred_element_type=F32)
        lr, li = _lti_scan(gr, gi, ar, nai, True)
        cr, ci = car_re[...], car_im[...]
        qr, qi = pw_re[...], pw_im[...]
        lr = lr + qr * cr - qi * ci
        li = li + qr * ci + qi * cr
        car_re[...] = lr[0:1, :]
        car_im[...] = li[0:1, :]

        s_re, s_im = sr_ref[...], si_ref[...]
        has_prev = (chunk > 0).astype(F32)
        prev_re = pr_ref[7:8, :] * has_prev
        prev_im = pi_ref[7:8, :] * has_prev
        sp_re = jnp.where(row == 0, prev_re, pltpu.roll(s_re, 1, 0))
        sp_im = jnp.where(row == 0, prev_im, pltpu.roll(s_im, 1, 0))
        p_ar = _rowsum(lr * sp_re + li * sp_im)
        p_ai = _rowsum(li * sp_re - lr * sp_im)

        lrb, lib, ub = lr.astype(BF16), li.astype(BF16), u.astype(BF16)
        du = (dy * d_ref[...] + jnp.dot(lrb, btr_ref[...], preferred_element_type=F32)
              + jnp.dot(lib, bti_ref[...], preferred_element_type=F32))
        du_ref[...] = du.astype(BF16)
        tdot = lambda p, q: lax.dot_general(p, q, (((0,), (0,)), ((), ())), preferred_element_type=F32)
        p_br, p_bi = tdot(ub, lrb), tdot(ub, lib)
        p_cr, p_ci = tdot(s_re.astype(BF16), dyb), -tdot(s_im.astype(BF16), dyb)
        p_dd = _rowsum(dy * u)

        @pl.when(i == 0)
        def _():
            dar_ref[...] = p_ar
            dai_ref[...] = p_ai
            dbr_ref[...] = p_br
            dbi_ref[...] = p_bi
            dcr_ref[...] = p_cr
            dci_ref[...] = p_ci
            dd_ref[...] = p_dd

        @pl.when(i > 0)
        def _():
            dar_ref[...] += p_ar
            dai_ref[...] += p_ai
            dbr_ref[...] += p_br
            dbi_ref[...] += p_bi
            dcr_ref[...] += p_cr
            dci_ref[...] += p_ci
            dd_ref[...] += p_dd

    rev = lambda k, i: (nchunk - 1 - i, k)
    prev = lambda k, i: (jnp.maximum((nchunk - 1 - i) * tail - 1, 0), k)
    kspec = lambda shape: pl.BlockSpec((None,) + shape, lambda k, i: (k, 0, 0))
    return pl.pallas_call(
        body, name=name, grid=(NK, nchunk),
        in_specs=[pl.BlockSpec((T, C), rev), pl.BlockSpec((T, C), rev), pl.BlockSpec((T, C), rev),
                  pl.BlockSpec((T, C), rev), pl.BlockSpec((T, S), rev), pl.BlockSpec((T, S), rev),
                  pl.BlockSpec((8, S), prev), pl.BlockSpec((8, S), prev),
                  kspec((S, C)), kspec((S, C)), kspec((C, S)), kspec((C, S)),
                  kspec((1, S)), kspec((1, S)), kspec((1, C))],
        out_specs=[pl.BlockSpec((T, C), rev), kspec((C, S)), kspec((C, S)), kspec((S, C)), kspec((S, C)),
                   kspec((1, S)), kspec((1, S)), kspec((1, C))],
        out_shape=[jax.ShapeDtypeStruct((L, NK * C), BF16),
                   jax.ShapeDtypeStruct((NK, C, S), F32), jax.ShapeDtypeStruct((NK, C, S), F32),
                   jax.ShapeDtypeStruct((NK, S, C), F32), jax.ShapeDtypeStruct((NK, S, C), F32),
                   jax.ShapeDtypeStruct((NK, 1, S), F32), jax.ShapeDtypeStruct((NK, 1, S), F32),
                   jax.ShapeDtypeStruct((NK, 1, C), F32)],
        scratch_shapes=[pltpu.VMEM((1, S), F32), pltpu.VMEM((1, S), F32),
                        pltpu.VMEM((T, S), F32), pltpu.VMEM((T, S), F32)],
        compiler_params=_params(("parallel", "arbitrary")),
    )(z, y0, dd_direct, dd_mm, s_re, s_im, s_re, s_im, bst_re, bst_im, cst_re, cst_im, a_re, a_im, d)


def _block_diag(v):
    NK, SG, R, Q = v.shape
    eye = jnp.eye(SG, dtype=v.dtype)
    return (v[:, :, :, None, :] * eye[None, :, None, :, None]).reshape(NK, SG * R, SG * Q)


def _block_diag_part(m, SG):
    NK, RR, QQ = m.shape
    R, Q = RR // SG, QQ // SG
    eye = jnp.eye(SG, dtype=m.dtype)
    return jnp.sum(m.reshape(NK, SG, R, SG, Q) * eye[None, :, None, :, None], axis=3)


def _position():
    return lax.axis_index("x"), lax.axis_index("y"), lax.axis_index("c")


def _other_chips(x, y):
    return [(1 - x, y), (x, 1 - y), (1 - x, 1 - y)]


def all_gather_blocks(shards, *, name):
    n = len(shards)

    def body(*refs):
        ins, outs = refs[:n], refs[n:2 * n]
        send_sems, recv_sems, local_sems = refs[2 * n:]
        x, y, c = _position()
        me, sibling = (x, y, c), (x, y, 1 - c)
        chips = _other_chips(x, y)

        def block(a, pos):
            return outs[a].at[4 * pos[0] + 2 * pos[1] + pos[2]]

        def copy(a, k, pos, to, src=None):
            return pltpu.make_async_remote_copy(
                src_ref=block(a, pos) if src is None else src, dst_ref=block(a, pos),
                send_sem=send_sems.at[7 * a + k], recv_sem=recv_sems.at[7 * a + k],
                device_id=to, device_id_type=MESH)

        mine = [pltpu.make_async_copy(ins[a], block(a, me), local_sems.at[a]) for a in range(n)]
        for cp in mine:
            cp.start()
        first = []
        for a in range(n):
            first.append(copy(a, 0, me, sibling, src=ins[a]))
            first += [copy(a, 1 + j, me, (*chip, c), src=ins[a]) for j, chip in enumerate(chips)]
        for cp in first:
            cp.start()
        passed = []
        for a in range(n):
            for j, chip in enumerate(chips):
                copy(a, 1 + j, (*chip, c), me).wait_recv()
                fwd = copy(a, 4 + j, (*chip, c), sibling)
                fwd.start()
                passed.append(fwd)
        for a in range(n):
            copy(a, 0, sibling, me).wait_recv()
            for j, chip in enumerate(chips):
                copy(a, 4 + j, (*chip, 1 - c), me).wait_recv()
        for cp in first + passed:
            cp.wait_send()
        for cp in mine:
            cp.wait()

    return pl.pallas_call(
        body, name=name,
        in_specs=[ANY] * n, out_specs=[ANY] * n,
        out_shape=[jax.ShapeDtypeStruct((N_DEV,) + s.shape, s.dtype) for s in shards],
        scratch_shapes=[pltpu.SemaphoreType.DMA((7 * n,)), pltpu.SemaphoreType.DMA((7 * n,)),
                        pltpu.SemaphoreType.DMA((n,))],
    )(*shards)


def sibling_exchange(grads, *, name):
    n = len(grads)

    def body(*refs):
        ins, mine, theirs = refs[:n], refs[n:2 * n], refs[2 * n:3 * n]
        send_sems, recv_sems, local_sems = refs[3 * n:]
        x, y, c = _position()
        remote, local = [], []
        for a in range(n):
            for q in range(4):
                remote.append(pltpu.make_async_remote_copy(
                    src_ref=ins[a].at[2 * q + 1 - c], dst_ref=theirs[a].at[q],
                    send_sem=send_sems.at[4 * a + q], recv_sem=recv_sems.at[4 * a + q],
                    device_id=(x, y, 1 - c), device_id_type=MESH))
                local.append(pltpu.make_async_copy(ins[a].at[2 * q + c], mine[a].at[q], local_sems.at[4 * a + q]))
        for cp in remote + local:
            cp.start()
        for cp in remote + local:
            cp.wait()

    quarter = [jax.ShapeDtypeStruct((4,) + g.shape[1:], g.dtype) for g in grads]
    return pl.pallas_call(
        body, name=name,
        in_specs=[ANY] * n, out_specs=[ANY] * (2 * n), out_shape=quarter + quarter,
        scratch_shapes=[pltpu.SemaphoreType.DMA((4 * n,)), pltpu.SemaphoreType.DMA((4 * n,)),
                        pltpu.SemaphoreType.DMA((4 * n,))],
    )(*grads)


def chip_exchange(parts, *, name):
    n = len(parts)

    def body(*refs):
        ins, outs = refs[:n], refs[n:2 * n]
        send_sems, recv_sems = refs[2 * n:]
        x, y, c = _position()
        copies = []
        for a in range(n):
            for j, chip in enumerate(_other_chips(x, y)):
                copies.append(pltpu.make_async_remote_copy(
                    src_ref=ins[a].at[2 * chip[0] + chip[1]], dst_ref=outs[a].at[j],
                    send_sem=send_sems.at[3 * a + j], recv_sem=recv_sems.at[3 * a + j],
                    device_id=(*chip, c), device_id_type=MESH))
        for cp in copies:
            cp.start()
        for cp in copies:
            cp.wait()

    return pl.pallas_call(
        body, name=name,
        in_specs=[ANY] * n, out_specs=[ANY] * n,
        out_shape=[jax.ShapeDtypeStruct((3,) + p.shape[1:], p.dtype) for p in parts],
        scratch_shapes=[pltpu.SemaphoreType.DMA((3 * n,)), pltpu.SemaphoreType.DMA((3 * n,))],
    )(*parts)


def add_pairs(a, b, *, name, tm=512):
    _, R, C = a.shape
    tm = _tile(R, tm, 16)

    def body(a_ref, b_ref, o_ref):
        o_ref[...] = (a_ref[...].astype(F32) + b_ref[...].astype(F32)).astype(o_ref.dtype)

    spec = pl.BlockSpec((None, tm, C), lambda q, i: (q, i, 0))
    return pl.pallas_call(
        body, name=name, grid=(4, R // tm),
        in_specs=[spec, spec], out_specs=spec,
        out_shape=jax.ShapeDtypeStruct(a.shape, BF16),
        compiler_params=_params(("parallel", "parallel")),
    )(a, b)


def _adamw(w, g, m, v):
    m = ADAM_B1 * m + (1.0 - ADAM_B1) * g
    v = ADAM_B2 * v + (1.0 - ADAM_B2) * (g * g)
    m_hat = m / (1.0 - ADAM_B1 ** ADAM_STEP)
    v_hat = v / (1.0 - ADAM_B2 ** ADAM_STEP)
    delta = -ADAM_LR * (m_hat / (jnp.sqrt(v_hat) + ADAM_EPS) + ADAM_WD * w)
    return delta, m, v


def adamw_sharded(w, m, v, own_a, own_b, others, *, name, tm=256):
    R, C = w.shape
    tm = _tile(R, tm, 16)

    def body(w_ref, m_ref, v_ref, a_ref, b_ref, o_ref, g_ref, d_ref, nm_ref, nv_ref):
        g = a_ref[...].astype(F32) + b_ref[...].astype(F32)
        for j in range(3):
            g = g + o_ref[j].astype(F32)
        g_ref[...] = g
        d_ref[...], nm_ref[...], nv_ref[...] = _adamw(w_ref[...], g, m_ref[...], v_ref[...])

    spec = pl.BlockSpec((tm, C), lambda i: (i, 0))
    sd = jax.ShapeDtypeStruct((R, C), F32)
    return pl.pallas_call(
        body, name=name, grid=(R // tm,),
        in_specs=[spec, spec, spec, spec, spec, pl.BlockSpec((3, tm, C), lambda i: (0, i, 0))],
        out_specs=[spec, spec, spec, spec], out_shape=[sd, sd, sd, sd],
        compiler_params=_params(("parallel",)),
    )(w, m, v, own_a, own_b, others)


def adamw_packed(w, m, v, gathered, *, name, tm=512):
    R, C = w.shape
    tm = _tile(R, tm, 8)

    def body(w_ref, m_ref, v_ref, ga_ref, g_ref, d_ref, nm_ref, nv_ref):
        g = ga_ref[0]
        for dev in range(1, N_DEV):
            g = g + ga_ref[dev]
        g_ref[...] = g
        d_ref[...], nm_ref[...], nv_ref[...] = _adamw(w_ref[...], g, m_ref[...], v_ref[...])

    spec = pl.BlockSpec((tm, C), lambda i: (i, 0))
    sd = jax.ShapeDtypeStruct((R, C), F32)
    return pl.pallas_call(
        body, name=name, grid=(R // tm,),
        in_specs=[spec, spec, spec, pl.BlockSpec((N_DEV, tm, C), lambda i: (0, i, 0))],
        out_specs=[spec, spec, spec, spec], out_shape=[sd, sd, sd, sd],
        compiler_params=_params(("parallel",)),
    )(w, m, v, gathered)


def _pack(arrays):
    parts = []
    for a in arrays:
        flat = a.reshape(-1).astype(F32)
        parts.append(jnp.pad(flat, (0, (-flat.size) % PACK_ALIGN)))
    return jnp.concatenate(parts).reshape(-1, LANE)


def _unpack(packed, shapes):
    flat = packed.reshape(-1)
    out, off = [], 0
    for shape in shapes:
        size = math.prod(shape)
        out.append(flat[off:off + size].reshape(shape))
        off += size + (-size) % PACK_ALIGN
    return out


SHARDED = ("w_in", "ssm_glu_w", "w_out", "w_ffn_in", "w_ffn_out", "w_ple_gate", "w_ple_proj")
SMALL = ("norm_mix_g", "ssm_lambda_re", "ssm_lambda_im", "ssm_log_step", "ssm_b_re", "ssm_b_im", "ssm_c_re",
         "ssm_c_im", "ssm_d", "ssm_glu_b", "sgu_ln_g", "sgu_ln_b", "sgu_w", "sgu_b", "out_norm_ssm_g",
         "out_norm_sgu_g", "norm_ffn_g", "norm_ple_g", "b_ple_gate", "final_norm_g")
WEIGHTS = ("norm_mix_g", "w_in", "ssm_lambda_re", "ssm_lambda_im", "ssm_log_step", "ssm_b_re", "ssm_b_im",
           "ssm_c_re", "ssm_c_im", "ssm_d", "ssm_glu_w", "ssm_glu_b", "sgu_ln_g", "sgu_ln_b", "sgu_w", "sgu_b",
           "out_norm_ssm_g", "out_norm_sgu_g", "w_out", "norm_ffn_g", "w_ffn_in", "w_ffn_out", "norm_ple_g",
           "w_ple_gate", "b_ple_gate", "w_ple_proj", "final_norm_g")


def _step(x, p, loss_target, w, m, v):
    L, D = x.shape[1], x.shape[2]
    x2d, p2d, tgt = x.reshape(L, D), p.reshape(L, -1), loss_target.reshape(L, D)
    d_ssm = w["ssm_glu_w"].shape[2]
    d_sgu = w["sgu_ln_g"].shape[1]
    G, P, H = w["ssm_b_re"].shape[1:]
    SG = min(SSM_SUPER, G)
    NK = G // SG
    row = lambda a: a.reshape(1, -1)

    shard2d = {n: w[n].reshape(w[n].shape[1:]) for n in SHARDED}
    gathered = all_gather_blocks([shard2d[n].astype(BF16) for n in SHARDED], name="gather_weights")
    wg = dict(zip(SHARDED, gathered))
    w_in_blk = wg["w_in"]
    w_glu = wg["ssm_glu_w"].reshape(d_ssm, d_ssm)
    w_out = wg["w_out"].reshape(D, D)
    w_ffn_in_blk = wg["w_ffn_in"]
    F = w_ffn_in_blk.shape[2] * 4
    w_ffn_out = wg["w_ffn_out"].reshape(F, D)
    w_gate = wg["w_ple_gate"].reshape(D, D)
    w_ple_blk = wg["w_ple_proj"]

    lam_re, lam_im, log_step = w["ssm_lambda_re"][0], w["ssm_lambda_im"][0], w["ssm_log_step"][0].reshape(G, 1)
    a_re, a_im, q_re, q_im = disc_lambda_fwd(lam_re, lam_im, log_step, name="s5_discretise_lambda")
    bt_re = w["ssm_b_re"][0].transpose(2, 0, 1).reshape(H, G * P)
    bt_im = w["ssm_b_im"][0].transpose(2, 0, 1).reshape(H, G * P)
    bbar_re, bbar_im = disc_b_fwd(row(q_re), row(q_im), bt_re, bt_im, name="s5_discretise_b")
    to_bs = lambda t: _block_diag(t.reshape(H, NK, SG, P).transpose(1, 2, 0, 3))
    to_cs = lambda t: _block_diag(t.reshape(NK, SG, H, P).transpose(0, 1, 3, 2))
    bs_re, bs_im = to_bs(bbar_re), to_bs(bbar_im)
    cs_re, cs_im = to_cs(w["ssm_c_re"][0]), to_cs(w["ssm_c_im"][0])
    a_re_k, a_im_k = a_re.reshape(NK, 1, SG * P), a_im.reshape(NK, 1, SG * P)
    d_k = w["ssm_d"][0].reshape(NK, 1, SG * H)
    bf = lambda t: t.astype(BF16)
    tr = lambda t: jnp.swapaxes(t, 1, 2)

    h1 = norm_fwd(x2d, w["norm_mix_g"], name="norm_mix")
    z = mm_nn(h1, w_in_blk, name="in_proj", out_dtype=F32, tm=512, tn=512, tk=2048)
    y0, s_re, s_im = ssm_fwd(z, bf(bs_re), bf(bs_im), bf(cs_re), bf(cs_im), a_re_k, a_im_k, d_k, name="s5_scan")
    ya1 = glu_pre(y0, name="s5_gelu")
    t_glu = mm_nn(ya1, w_glu, name="s5_glu_proj", out_dtype=F32, tm=512, tn=512, tk=2048)
    n_a = glu_post(y0, t_glu, w["ssm_glu_b"], w["out_norm_ssm_g"], name="s5_glu_norm")
    b_s3 = w["sgu_b"][0][:, :, None]
    n_b = sgu_fwd(z, w["sgu_ln_g"], w["sgu_ln_b"], w["sgu_w"][0], b_s3, w["out_norm_sgu_g"], name="sgu", d_sgu=d_sgu)
    ycat = jnp.concatenate([n_a, n_b], axis=1)
    x1 = mm_nn(ycat, w_out, name="out_proj", out_dtype=F32, tm=512, tn=512, tk=2048, residual=x2d)
    h2 = norm_fwd(x1, w["norm_ffn_g"], name="norm_ffn")
    act, gate_ff, up_ff = ffn_in_swiglu(h2, w_ffn_in_blk, name="ffn_in_swiglu", tm=256)
    x2 = mm_nn(act, w_ffn_out, name="ffn_out", out_dtype=F32, tm=1024, tn=1024, tk=512, residual=x1)
    h3 = norm_fwd(x2, w["norm_ple_g"], name="norm_ple")
    gpre = mm_nn(h3, w_gate, name="ple_gate", out_dtype=F32, tm=512, tn=512, tk=2048)
    pp = mm_nn(bf(p2d), w_ple_blk, name="ple_proj", out_dtype=F32, tm=512, tn=512, tk=2048)

    dx3, dpre, dpp, loss_part, d_final_g, d_b_gate = head_and_loss(
        x2, gpre, w["b_ple_gate"], pp, row(w["final_norm_g"]), tgt, name="head_and_loss")
    gw = {}
    gw["w_ple_gate"] = mm_tn(h3, dpre, name="d_w_ple_gate", out_dtype=BF16, tm=512, tko=1024, tno=1024)
    gw["w_ple_proj"] = mm_tn(bf(p2d), dpp, name="d_w_ple_proj", out_dtype=BF16, tm=512, tko=1024, tno=1024,
                             out_blocks=N_DEV)
    dh3 = mm_nt(dpre, w_gate, name="d_h_ple", out_dtype=F32, tm=512, tko=1024, tc=2048)
    dx2, dx2b, d_ple_g = norm_bwd(dh3, x2, w["norm_ple_g"], dx3, name="d_norm_ple", want_bf16=True)
    dact = mm_nt(dx2b, w_ffn_out, name="d_act", out_dtype=F32, tm=512, tko=1408, tc=2048)
    gw["w_ffn_out"] = mm_tn(act, dx2b, name="d_w_ffn_out", out_dtype=BF16, tm=512, tko=1408, tno=1024)
    dgate, dup = swiglu_bwd(dact, gate_ff, up_ff, name="d_swiglu")
    dgu = jnp.concatenate([dgate, dup], axis=1)
    gw["w_ffn_in"] = mm_tn(h2, dgu, name="d_w_ffn_in", out_dtype=BF16, tm=512, tko=1024, tno=1408, out_blocks=N_DEV)
    dh2 = mm_nt(dgu, w_ffn_in_blk, name="d_h_ffn", out_dtype=F32, tm=1024, tko=1024, tc=1408)
    dx1, dx1b, d_ffn_g = norm_bwd(dh2, x1, w["norm_ffn_g"], dx2, name="d_norm_ffn", want_bf16=True)
    dycat = mm_nt(dx1b, w_out, name="d_ycat", out_dtype=F32, tm=512, tko=1024, tc=2048)
    gw["w_out"] = mm_tn(ycat, dx1b, name="d_w_out", out_dtype=BF16, tm=512, tko=1024, tno=1024)
    dzu, dzv, d_sgu_w, d_sgu_b, d_ln_g, d_ln_b, d_g_b = sgu_bwd(
        z, dycat, w["sgu_ln_g"], w["sgu_ln_b"], w["sgu_w"][0], b_s3, w["out_norm_sgu_g"], name="d_sgu", d_sgu=d_sgu)
    dt_glu, dd_direct, d_g_a, d_glu_b = glu_post_bwd(
        y0, t_glu, w["ssm_glu_b"], w["out_norm_ssm_g"], dycat, name="d_s5_glu_norm")
    gw["ssm_glu_w"] = mm_tn(ya1, dt_glu, name="d_w_glu", out_dtype=BF16, tm=512, tko=1024, tno=1024)
    dd_mm = mm_nt(dt_glu, w_glu, name="d_s5_glu_proj", out_dtype=F32, tm=512, tko=1024, tc=2048)
    du, d_bs_re, d_bs_im, d_cs_re, d_cs_im, d_a_re, d_a_im, d_d = ssm_bwd(
        z, y0, dd_direct, dd_mm, s_re, s_im, bf(tr(bs_re)), bf(tr(bs_im)), bf(tr(cs_re)), bf(tr(cs_im)),
        a_re_k, a_im_k, d_k, name="d_s5_scan")
    dz = jnp.concatenate([du, dzu, dzv], axis=1)
    gw["w_in"] = mm_tn(h1, dz, name="d_w_in", out_dtype=BF16, tm=512, tko=1024, tno=1024, out_blocks=N_DEV)
    dh1 = mm_nt(dz, w_in_blk, name="d_h_mix", out_dtype=F32, tm=512, tko=1024, tc=2048)
    grad_x, d_mix_g = norm_bwd(dh1, x2d, w["norm_mix_g"], dx1, name="d_norm_mix", want_bf16=False)

    from_bs = lambda t: _block_diag_part(t, SG).transpose(2, 0, 1, 3).reshape(H, G * P)
    from_cs = lambda t: _block_diag_part(t, SG).transpose(0, 1, 3, 2).reshape(1, G, H, P)
    d_q_re, d_q_im, d_bt_re, d_bt_im = disc_b_bwd(row(q_re), row(q_im), bt_re, bt_im, from_bs(d_bs_re),
                                                  from_bs(d_bs_im), name="d_s5_discretise_b")
    d_lam_re, d_lam_im, d_log_step = disc_lambda_bwd(
        lam_re, lam_im, log_step,
        (d_a_re.reshape(G, P), d_a_im.reshape(G, P), d_q_re.reshape(G, P), d_q_im.reshape(G, P)),
        name="d_s5_discretise_lambda")
    from_bt = lambda t: t.reshape(H, G, P).transpose(1, 2, 0).reshape(1, G, P, H)
    small_grads = {
        "norm_mix_g": d_mix_g, "ssm_lambda_re": d_lam_re, "ssm_lambda_im": d_lam_im, "ssm_log_step": d_log_step,
        "ssm_b_re": from_bt(d_bt_re), "ssm_b_im": from_bt(d_bt_im), "ssm_c_re": from_cs(d_cs_re),
        "ssm_c_im": from_cs(d_cs_im), "ssm_d": d_d, "ssm_glu_b": d_glu_b, "sgu_ln_g": d_ln_g, "sgu_ln_b": d_ln_b,
        "sgu_w": d_sgu_w, "sgu_b": d_sgu_b, "out_norm_ssm_g": d_g_a, "out_norm_sgu_g": d_g_b,
        "norm_ffn_g": d_ffn_g, "norm_ple_g": d_ple_g, "b_ple_gate": d_b_gate, "final_norm_g": d_final_g,
    }

    x_pos, y_pos, c_pos = _position()
    grads8 = [gw[n].reshape((N_DEV,) + shard2d[n].shape) for n in SHARDED]
    halves = sibling_exchange(grads8, name="grads_to_sibling")
    mine, theirs = halves[:len(SHARDED)], halves[len(SHARDED):]
    as3d = lambda t: t.reshape(t.shape[0], -1, t.shape[-1])
    parts = [add_pairs(as3d(a), as3d(b), name="chip_sum_" + n).reshape(a.shape)
             for n, a, b in zip(SHARDED, mine, theirs)]
    others = chip_exchange(parts, name="grads_to_chips")
    q_me = 2 * x_pos + y_pos
    out = {}
    for n, a, b, o in zip(SHARDED, mine, theirs, others):
        own_a = lax.dynamic_index_in_dim(a, q_me, 0, keepdims=False)
        own_b = lax.dynamic_index_in_dim(b, q_me, 0, keepdims=False)
        res = adamw_sharded(shard2d[n], m[n].reshape(shard2d[n].shape), v[n].reshape(shard2d[n].shape),
                            own_a, own_b, o, name="adamw_" + n)
        out[n] = [r.reshape(w[n].shape) for r in res]

    pack_g = _pack([loss_part] + [small_grads[n] for n in SMALL])
    zero_tile = jnp.zeros((8, LANE), F32)
    pack_w = _pack([zero_tile] + [w[n] for n in SMALL])
    pack_m = _pack([zero_tile] + [m[n] for n in SMALL])
    pack_v = _pack([zero_tile] + [v[n] for n in SMALL])
    (all_g,) = all_gather_blocks([pack_g], name="gather_small_grads")
    res = adamw_packed(pack_w, pack_m, pack_v, all_g, name="adamw_small")
    shapes = [(8, LANE)] + [w[n].shape for n in SMALL]
    unpacked = [_unpack(r, shapes) for r in res]
    loss = unpacked[0][0][0, 0]
    for i, n in enumerate(SMALL):
        out[n] = [u[i + 1] for u in unpacked]

    grads = [out[n][0] for n in WEIGHTS]
    deltas = [out[n][1] for n in WEIGHTS]
    new_m = [out[n][2] for n in WEIGHTS]
    new_v = [out[n][3] for n in WEIGHTS]
    return (loss, grad_x.reshape(x.shape), *grads, *deltas, *new_m, *new_v)


def kernel(x, p, norm_mix_g, w_in, ssm_lambda_re, ssm_lambda_im, ssm_log_step, ssm_b_re, ssm_b_im, ssm_c_re, ssm_c_im, ssm_d, ssm_glu_w, ssm_glu_b, sgu_ln_g, sgu_ln_b, sgu_w, sgu_b, out_norm_ssm_g, out_norm_sgu_g, w_out, norm_ffn_g, w_ffn_in, w_ffn_out, norm_ple_g, w_ple_gate, b_ple_gate, w_ple_proj, final_norm_g, loss_target, m_norm_mix_g, m_w_in, m_ssm_lambda_re, m_ssm_lambda_im, m_ssm_log_step, m_ssm_b_re, m_ssm_b_im, m_ssm_c_re, m_ssm_c_im, m_ssm_d, m_ssm_glu_w, m_ssm_glu_b, m_sgu_ln_g, m_sgu_ln_b, m_sgu_w, m_sgu_b, m_out_norm_ssm_g, m_out_norm_sgu_g, m_w_out, m_norm_ffn_g, m_w_ffn_in, m_w_ffn_out, m_norm_ple_g, m_w_ple_gate, m_b_ple_gate, m_w_ple_proj, m_final_norm_g, v_norm_mix_g, v_w_in, v_ssm_lambda_re, v_ssm_lambda_im, v_ssm_log_step, v_ssm_b_re, v_ssm_b_im, v_ssm_c_re, v_ssm_c_im, v_ssm_d, v_ssm_glu_w, v_ssm_glu_b, v_sgu_ln_g, v_sgu_ln_b, v_sgu_w, v_sgu_b, v_out_norm_ssm_g, v_out_norm_sgu_g, v_w_out, v_norm_ffn_g, v_w_ffn_in, v_w_ffn_out, v_norm_ple_g, v_w_ple_gate, v_b_ple_gate, v_w_ple_proj, v_final_norm_g):
    given = dict(locals())
    w = {n: given[n] for n in WEIGHTS}
    m = {n: given["m_" + n] for n in WEIGHTS}
    v = {n: given["v_" + n] for n in WEIGHTS}
    return _step(x, p, loss_target, w, m, v)
```

```python
import functools
import math

import jax
import jax.numpy as jnp
from jax import lax
from jax.experimental import pallas as pl
from jax.experimental.pallas import tpu as pltpu

F32 = jnp.float32
BF16 = jnp.bfloat16
MESH = pl.DeviceIdType.MESH
ANY = pl.BlockSpec(memory_space=pl.ANY)

N_DEV = 8
EPS = 1e-6
LAMBDA_RE_MAX = -1e-4
SSM_GROUP = 16
SSM_STATE = 64
SSM_SUPER = 16
SGU_CHUNK = 128
ADAM_LR, ADAM_B1, ADAM_B2, ADAM_EPS, ADAM_WD, ADAM_STEP = 0.001, 0.9, 0.999, 1e-08, 0.01, 10
VMEM_LIMIT = 52 * 1024 * 1024
LANE = 128
PACK_ALIGN = 8 * LANE

_GELU_C = math.sqrt(2.0 / math.pi)


def _params(sem=None):
    return pltpu.CompilerParams(dimension_semantics=sem, vmem_limit_bytes=VMEM_LIMIT)


def _tile(dim, pref, unit=LANE):
    if dim <= pref:
        return dim
    t = (pref // unit) * unit
    while t >= unit:
        if dim % t == 0:
            return t
        t -= unit
    return dim


def _gelu(x):
    return 0.5 * x * (1.0 + jnp.tanh(_GELU_C * (x + 0.044715 * x * x * x)))


def _gelu_grad(x):
    t = jnp.tanh(_GELU_C * (x + 0.044715 * x * x * x))
    return 0.5 * (1.0 + t) + 0.5 * x * (1.0 - t * t) * (_GELU_C * (1.0 + 3.0 * 0.044715 * x * x))


def _rms(x):
    return lax.rsqrt(jnp.mean(x * x, axis=-1, keepdims=True) + EPS)


def _rmsnorm_bwd(dy, x, r, g):
    dyg = dy * g
    return r * dyg - x * (r * r * r) * jnp.mean(dyg * x, axis=-1, keepdims=True)


def _rowsum(v):
    return jnp.sum(v, axis=0, keepdims=True)


class Background:
    def __init__(self, inputs, out_shapes, scratch, phases):
        self.inputs, self.out_shapes, self.scratch, self.phases = list(inputs), list(out_shapes), list(scratch), phases

    def emit(self, step, nsteps, ins, outs, scratch):
        last = len(self.phases) - 1
        for p, phase in enumerate(self.phases):
            at = 0 if p == 0 else (nsteps - 1 if p == last else (nsteps * p) // last)

            @pl.when(step == at)
            def _():
                phase(ins, outs, scratch)


def _carrier(bg, n_in, n_out, n_scratch, grid):
    nbi = len(bg.inputs) if bg else 0
    nbo = len(bg.out_shapes) if bg else 0
    nsteps = math.prod(grid)

    def split(refs):
        ins = refs[:n_in]
        bg_ins = refs[n_in:n_in + nbi]
        outs = refs[n_in + nbi:n_in + nbi + n_out]
        bg_outs = refs[n_in + nbi + n_out:n_in + nbi + n_out + nbo]
        rest = refs[n_in + nbi + n_out + nbo:]
        scratch, bg_scratch = rest[:n_scratch], rest[n_scratch:]

        def run_background():
            if bg is None:
                return
            step = pl.program_id(0)
            for axis in range(1, len(grid)):
                step = step * grid[axis] + pl.program_id(axis)
            bg.emit(step, nsteps, bg_ins, bg_outs, bg_scratch)

        return ins, outs, scratch, run_background

    if bg is None:
        return [], [], [], [], [], split
    return [ANY] * nbi, list(bg.inputs), [ANY] * nbo, list(bg.out_shapes), list(bg.scratch), split


def _semantics(bg, sem):
    return tuple("arbitrary" for _ in sem) if bg is not None else sem


def _results(res, n_out, bg):
    res = list(res) if isinstance(res, (list, tuple)) else [res]
    own = res[0] if n_out == 1 else res[:n_out]
    return (own, res[n_out:]) if bg is not None else own


def mm_nn(a, b, *, name, out_dtype, tm, tn, tk, residual=None, bg=None):
    M, K = a.shape
    blocked = b.ndim == 3
    if blocked:
        nb, _, Nb = b.shape
        N = nb * Nb
        tn = _tile(Nb, tn)
        per = Nb // tn
    else:
        N = b.shape[1]
        tn = _tile(N, tn)
    tm, tk = _tile(M, tm, 8), _tile(K, tk)
    nj, ni, nk = N // tn, M // tm, K // tk
    has_res = residual is not None
    grid = (nj, ni, nk)
    bg_in_specs, bg_args, bg_out_specs, bg_out_shapes, bg_scratch, split = _carrier(
        bg, 3 if has_res else 2, 1, 0 if nk == 1 else 1, grid)

    def body(*refs):
        ins, (o_ref,), scratch, run_background = split(refs)
        run_background()
        a_ref, b_ref = ins[0], ins[1]
        r_ref = ins[2] if has_res else None

        def finish(acc):
            if has_res:
                acc = acc + r_ref[...]
            o_ref[...] = acc.astype(o_ref.dtype)

        part = jnp.dot(a_ref[...], b_ref[...], preferred_element_type=F32)
        if nk == 1:
            finish(part)
        else:
            acc_ref = scratch[0]
            k = pl.program_id(2)

            @pl.when(k == 0)
            def _():
                acc_ref[...] = part

            @pl.when(k > 0)
            def _():
                acc_ref[...] += part

            @pl.when(k == nk - 1)
            def _():
                finish(acc_ref[...])

    if blocked:
        b_spec = pl.BlockSpec((None, tk, tn), lambda j, i, k: (j // per, k, j % per))
    else:
        b_spec = pl.BlockSpec((tk, tn), lambda j, i, k: (k, j))
    in_specs = [pl.BlockSpec((tm, tk), lambda j, i, k: (i, k)), b_spec]
    args = [a, b]
    if has_res:
        in_specs.append(pl.BlockSpec((tm, tn), lambda j, i, k: (i, j)))
        args.append(residual)
    res = pl.pallas_call(
        body, name=name, grid=grid,
        in_specs=in_specs + bg_in_specs,
        out_specs=[pl.BlockSpec((tm, tn), lambda j, i, k: (i, j))] + bg_out_specs,
        out_shape=[jax.ShapeDtypeStruct((M, N), out_dtype)] + bg_out_shapes,
        scratch_shapes=([] if nk == 1 else [pltpu.VMEM((tm, tn), F32)]) + bg_scratch,
        compiler_params=_params(_semantics(bg, ("parallel", "parallel", "arbitrary"))),
    )(*args, *bg_args)
    return _results(res, 1, bg)


def mm_nt(a, w, *, name, out_dtype, tm, tko, tc, bg=None):
    M, N = a.shape
    blocked = w.ndim == 3
    if blocked:
        nb, Ko, Nb = w.shape
        tc = _tile(Nb, tc)
        per = Nb // tc
    else:
        Ko = w.shape[0]
        tc = _tile(N, tc)
    tm, tko = _tile(M, tm, 8), _tile(Ko, tko)
    njo, ni, nc = Ko // tko, M // tm, N // tc
    grid = (njo, ni, nc)
    bg_in_specs, bg_args, bg_out_specs, bg_out_shapes, bg_scratch, split = _carrier(
        bg, 2, 1, 0 if nc == 1 else 1, grid)

    def body(*refs):
        (a_ref, w_ref), (o_ref,), scratch, run_background = split(refs)
        run_background()
        part = lax.dot_general(a_ref[...], w_ref[...], (((1,), (1,)), ((), ())),
                               preferred_element_type=F32)
        if nc == 1:
            o_ref[...] = part.astype(o_ref.dtype)
        else:
            acc_ref = scratch[0]
            c = pl.program_id(2)

            @pl.when(c == 0)
            def _():
                acc_ref[...] = part

            @pl.when(c > 0)
            def _():
                acc_ref[...] += part

            @pl.when(c == nc - 1)
            def _():
                o_ref[...] = acc_ref[...].astype(o_ref.dtype)

    if blocked:
        w_spec = pl.BlockSpec((None, tko, tc), lambda j, i, c: (c // per, j, c % per))
    else:
        w_spec = pl.BlockSpec((tko, tc), lambda j, i, c: (j, c))
    res = pl.pallas_call(
        body, name=name, grid=grid,
        in_specs=[pl.BlockSpec((tm, tc), lambda j, i, c: (i, c)), w_spec] + bg_in_specs,
        out_specs=[pl.BlockSpec((tm, tko), lambda j, i, c: (i, j))] + bg_out_specs,
        out_shape=[jax.ShapeDtypeStruct((M, Ko), out_dtype)] + bg_out_shapes,
        scratch_shapes=([] if nc == 1 else [pltpu.VMEM((tm, tko), F32)]) + bg_scratch,
        compiler_params=_params(_semantics(bg, ("parallel", "parallel", "arbitrary"))),
    )(a, w, *bg_args)
    return _results(res, 1, bg)


def mm_tn(a, g, *, name, out_dtype, tm, tko, tno, out_blocks=None, bg=None):
    M, K = a.shape
    N = g.shape[1]
    if out_blocks:
        Nb = N // out_blocks
        tno = _tile(Nb, tno)
        per = Nb // tno
    else:
        tno = _tile(N, tno)
    tm, tko = _tile(M, tm), _tile(K, tko)
    njo, njn, nm = K // tko, N // tno, M // tm
    grid = (njo, njn, nm)
    bg_in_specs, bg_args, bg_out_specs, bg_out_shapes, bg_scratch, split = _carrier(
        bg, 2, 1, 0 if nm == 1 else 1, grid)

    def body(*refs):
        (a_ref, g_ref), (o_ref,), scratch, run_background = split(refs)
        run_background()
        part = lax.dot_general(a_ref[...], g_ref[...], (((0,), (0,)), ((), ())),
                               preferred_element_type=F32)
        if nm == 1:
            o_ref[...] = part.astype(o_ref.dtype)
        else:
            acc_ref = scratch[0]
            m = pl.program_id(2)

            @pl.when(m == 0)
            def _():
                acc_ref[...] = part

            @pl.when(m > 0)
            def _():
                acc_ref[...] += part

            @pl.when(m == nm - 1)
            def _():
                o_ref[...] = acc_ref[...].astype(o_ref.dtype)

    if out_blocks:
        o_spec = pl.BlockSpec((None, tko, tno), lambda jo, jn, m: (jn // per, jo, jn % per))
        o_shape = jax.ShapeDtypeStruct((out_blocks, K, Nb), out_dtype)
    else:
        o_spec = pl.BlockSpec((tko, tno), lambda jo, jn, m: (jo, jn))
        o_shape = jax.ShapeDtypeStruct((K, N), out_dtype)
    res = pl.pallas_call(
        body, name=name, grid=grid,
        in_specs=[pl.BlockSpec((tm, tko), lambda jo, jn, m: (m, jo)),
                  pl.BlockSpec((tm, tno), lambda jo, jn, m: (m, jn))] + bg_in_specs,
        out_specs=[o_spec] + bg_out_specs, out_shape=[o_shape] + bg_out_shapes,
        scratch_shapes=([] if nm == 1 else [pltpu.VMEM((tko, tno), F32)]) + bg_scratch,
        compiler_params=_params(_semantics(bg, ("parallel", "parallel", "arbitrary"))),
    )(a, g, *bg_args)
    return _results(res, 1, bg)


def ffn_in_swiglu(h, w_blk, *, name, tm, bg=None):
    M, K = h.shape
    nb, _, Nb = w_blk.shape
    nh = nb // 2
    F = nh * Nb
    tm = _tile(M, tm, 8)
    grid = (nh, M // tm)
    bg_in_specs, bg_args, bg_out_specs, bg_out_shapes, bg_scratch, split = _carrier(bg, 3, 3, 0, grid)

    def body(*refs):
        (h_ref, wg_ref, wu_ref), (act_ref, gate_ref, up_ref), _, run_background = split(refs)
        run_background()
        hv = h_ref[...]
        gate = jnp.dot(hv, wg_ref[...], preferred_element_type=F32)
        up = jnp.dot(hv, wu_ref[...], preferred_element_type=F32)
        gate_ref[...] = gate
        up_ref[...] = up
        act_ref[...] = (gate * jax.nn.sigmoid(gate) * up).astype(act_ref.dtype)

    o_spec = pl.BlockSpec((tm, Nb), lambda j, i: (i, j))
    res = pl.pallas_call(
        body, name=name, grid=grid,
        in_specs=[pl.BlockSpec((tm, K), lambda j, i: (i, 0)),
                  pl.BlockSpec((None, K, Nb), lambda j, i: (j, 0, 0)),
                  pl.BlockSpec((None, K, Nb), lambda j, i: (j + nh, 0, 0))] + bg_in_specs,
        out_specs=[o_spec, o_spec, o_spec] + bg_out_specs,
        out_shape=[jax.ShapeDtypeStruct((M, F), BF16), jax.ShapeDtypeStruct((M, F), F32),
                   jax.ShapeDtypeStruct((M, F), F32)] + bg_out_shapes,
        scratch_shapes=bg_scratch,
        compiler_params=_params(_semantics(bg, ("parallel", "parallel"))),
    )(h, w_blk, w_blk, *bg_args)
    return _results(res, 3, bg)


def _row_spec(tm, d, col=0):
    return pl.BlockSpec((tm, d), lambda i: (i, col))


def _vec_spec(d):
    return pl.BlockSpec((1, d), lambda i: (0, 0))


def norm_fwd(x, g, *, name, tm=256):
    L, D = x.shape
    tm = _tile(L, tm, 8)

    def body(x_ref, g_ref, h_ref):
        xv = x_ref[...]
        h_ref[...] = (xv * _rms(xv) * g_ref[...]).astype(h_ref.dtype)

    return pl.pallas_call(
        body, name=name, grid=(L // tm,),
        in_specs=[_row_spec(tm, D), _vec_spec(D)],
        out_specs=_row_spec(tm, D),
        out_shape=jax.ShapeDtypeStruct((L, D), BF16),
        compiler_params=_params(("parallel",)),
    )(x, g)


def norm_bwd(dh, xin, g, dres, *, name, want_bf16, tm=128):
    L, D = xin.shape
    tm = _tile(L, tm, 8)

    def body(dh_ref, x_ref, g_ref, dres_ref, dx_ref, *rest):
        dg_ref = rest[-1]
        xv, dhv = x_ref[...], dh_ref[...]
        r = _rms(xv)
        dx = dres_ref[...] + _rmsnorm_bwd(dhv, xv, r, g_ref[...])
        dx_ref[...] = dx
        if want_bf16:
            rest[0][...] = dx.astype(BF16)
        part = _rowsum(dhv * xv * r)

        @pl.when(pl.program_id(0) == 0)
        def _():
            dg_ref[...] = part

        @pl.when(pl.program_id(0) > 0)
        def _():
            dg_ref[...] += part

    out_specs = [_row_spec(tm, D)] + ([_row_spec(tm, D)] if want_bf16 else []) + [_vec_spec(D)]
    out_shape = ([jax.ShapeDtypeStruct((L, D), F32)]
                 + ([jax.ShapeDtypeStruct((L, D), BF16)] if want_bf16 else [])
                 + [jax.ShapeDtypeStruct((1, D), F32)])
    return pl.pallas_call(
        body, name=name, grid=(L // tm,),
        in_specs=[_row_spec(tm, D), _row_spec(tm, D), _vec_spec(D), _row_spec(tm, D)],
        out_specs=out_specs, out_shape=out_shape,
        compiler_params=_params(("arbitrary",)),
    )(dh, xin, g, dres)


def glu_pre(y0, *, name, tm=256):
    L, D = y0.shape
    tm = _tile(L, tm, 8)

    def body(y_ref, o_ref):
        o_ref[...] = _gelu(y_ref[...]).astype(o_ref.dtype)

    return pl.pallas_call(
        body, name=name, grid=(L // tm,),
        in_specs=[_row_spec(tm, D)], out_specs=_row_spec(tm, D),
        out_shape=jax.ShapeDtypeStruct((L, D), BF16),
        compiler_params=_params(("parallel",)),
    )(y0)


def glu_post(y0, t, b_glu, g_a, *, name, tm=256):
    L, D = y0.shape
    tm = _tile(L, tm, 8)

    def body(y_ref, t_ref, b_ref, g_ref, o_ref):
        ya = _gelu(y_ref[...]) * jax.nn.sigmoid(t_ref[...] + b_ref[...])
        o_ref[...] = (ya * _rms(ya) * g_ref[...]).astype(o_ref.dtype)

    return pl.pallas_call(
        body, name=name, grid=(L // tm,),
        in_specs=[_row_spec(tm, D), _row_spec(tm, D), _vec_spec(D), _vec_spec(D)],
        out_specs=_row_spec(tm, D),
        out_shape=jax.ShapeDtypeStruct((L, D), BF16),
        compiler_params=_params(("parallel",)),
    )(y0, t, b_glu, g_a)


def glu_post_bwd(y0, t, b_glu, g_a, dycat, *, name, tm=128):
    L, D = y0.shape
    tm = _tile(L, tm, 8)

    def body(y_ref, t_ref, b_ref, g_ref, dn_ref, dt_ref, dd_ref, dga_ref, dbg_ref):
        ya1 = _gelu(y_ref[...])
        sg = jax.nn.sigmoid(t_ref[...] + b_ref[...])
        ya = ya1 * sg
        ra = _rms(ya)
        dn = dn_ref[...]
        dya = _rmsnorm_bwd(dn, ya, ra, g_ref[...])
        dt = dya * ya1 * sg * (1.0 - sg)
        dt_ref[...] = dt.astype(BF16)
        dd_ref[...] = dya * sg
        p_ga, p_bg = _rowsum(dn * ya * ra), _rowsum(dt)

        @pl.when(pl.program_id(0) == 0)
        def _():
            dga_ref[...] = p_ga
            dbg_ref[...] = p_bg

        @pl.when(pl.program_id(0) > 0)
        def _():
            dga_ref[...] += p_ga
            dbg_ref[...] += p_bg

    return pl.pallas_call(
        body, name=name, grid=(L // tm,),
        in_specs=[_row_spec(tm, D), _row_spec(tm, D), _vec_spec(D), _vec_spec(D), _row_spec(tm, D, 0)],
        out_specs=[_row_spec(tm, D), _row_spec(tm, D), _vec_spec(D), _vec_spec(D)],
        out_shape=[jax.ShapeDtypeStruct((L, D), BF16), jax.ShapeDtypeStruct((L, D), F32),
                   jax.ShapeDtypeStruct((1, D), F32), jax.ShapeDtypeStruct((1, D), F32)],
        compiler_params=_params(("arbitrary",)),
    )(y0, t, b_glu, g_a, dycat)


def head_and_loss(x2, gpre, b_g, pp, g_f, tgt, *, name, tm=128):
    L, D = x2.shape
    tm = _tile(L, tm, 8)

    def body(x2_ref, gp_ref, bg_ref, pp_ref, gf_ref, tg_ref,
             dx3_ref, dpre_ref, dpp_ref, loss_ref, dgf_ref, dbg_ref):
        gate = jax.nn.sigmoid(gp_ref[...] + bg_ref[...])
        ppv = pp_ref[...]
        x3 = x2_ref[...] + gate * ppv
        r = _rms(x3)
        xn = x3 * r
        gf = gf_ref[...]
        err = xn * gf - tg_ref[...]
        loss = 0.5 * jnp.sum(jnp.mean(err * err, axis=-1, keepdims=True), axis=0, keepdims=True)
        dout = err * (1.0 / D)
        dx3 = _rmsnorm_bwd(dout, x3, r, gf)
        dx3_ref[...] = dx3
        dpre = dx3 * ppv * gate * (1.0 - gate)
        dpre_ref[...] = dpre.astype(BF16)
        dpp_ref[...] = (dx3 * gate).astype(BF16)
        p_gf, p_bg = _rowsum(dout * xn), _rowsum(dpre)
        p_loss = jnp.broadcast_to(loss, loss_ref.shape)

        @pl.when(pl.program_id(0) == 0)
        def _():
            loss_ref[...] = p_loss
            dgf_ref[...] = p_gf
            dbg_ref[...] = p_bg

        @pl.when(pl.program_id(0) > 0)
        def _():
            loss_ref[...] += p_loss
            dgf_ref[...] += p_gf
            dbg_ref[...] += p_bg

    rs = _row_spec(tm, D)
    return pl.pallas_call(
        body, name=name, grid=(L // tm,),
        in_specs=[rs, rs, _vec_spec(D), rs, _vec_spec(D), rs],
        out_specs=[rs, rs, rs, pl.BlockSpec((8, LANE), lambda i: (0, 0)), _vec_spec(D), _vec_spec(D)],
        out_shape=[jax.ShapeDtypeStruct((L, D), F32), jax.ShapeDtypeStruct((L, D), BF16),
                   jax.ShapeDtypeStruct((L, D), BF16), jax.ShapeDtypeStruct((8, LANE), F32),
                   jax.ShapeDtypeStruct((1, D), F32), jax.ShapeDtypeStruct((1, D), F32)],
        compiler_params=_params(("arbitrary",)),
    )(x2, gpre, b_g, pp, g_f, tgt)


def swiglu_bwd(dact, gate, up, *, name, tm=256, tn=512):
    L, F = dact.shape
    tm, tn = _tile(L, tm, 8), _tile(F, tn)

    def body(da_ref, g_ref, u_ref, dg_ref, du_ref):
        da, gv = da_ref[...], g_ref[...]
        sg = jax.nn.sigmoid(gv)
        dg_ref[...] = (da * u_ref[...] * sg * (1.0 + gv * (1.0 - sg))).astype(BF16)
        du_ref[...] = (da * gv * sg).astype(BF16)

    spec = pl.BlockSpec((tm, tn), lambda i, j: (i, j))
    return pl.pallas_call(
        body, name=name, grid=(L // tm, F // tn),
        in_specs=[spec, spec, spec], out_specs=[spec, spec],
        out_shape=[jax.ShapeDtypeStruct((L, F), BF16), jax.ShapeDtypeStruct((L, F), BF16)],
        compiler_params=_params(("parallel", "parallel")),
    )(dact, gate, up)


def _sgu_forward_values(zu, zv, lng, lnb, w_ref, bs_ref, s_scr, heads, hd):
    u1 = _gelu(zu)
    v1 = _gelu(zv)
    xc = v1 - jnp.mean(v1, axis=-1, keepdims=True)
    r = lax.rsqrt(jnp.mean(xc * xc, axis=-1, keepdims=True) + EPS)
    xhat = xc * r
    v2 = xhat * lng + lnb
    tril = (lax.broadcasted_iota(jnp.int32, (SGU_CHUNK, SGU_CHUNK), 0)
            >= lax.broadcasted_iota(jnp.int32, (SGU_CHUNK, SGU_CHUNK), 1))
    for h in range(heads):
        wm = jnp.where(tril, w_ref[h], 0.0).astype(BF16)
        cols = slice(h * hd, (h + 1) * hd)
        s_scr[:, cols] = jnp.dot(wm, v2[:, cols].astype(BF16), preferred_element_type=F32) + bs_ref[h]
    return u1, xhat, r, v2, tril


def sgu_fwd(z, ln_g, ln_b, w_s, b_s, g_b, *, name, d_sgu):
    L = z.shape[0]
    heads = w_s.shape[0]
    hd = d_sgu // heads

    def body(zu_ref, zv_ref, lng_ref, lnb_ref, w_ref, bs_ref, gb_ref, o_ref, s_scr):
        u1, _, _, _, _ = _sgu_forward_values(zu_ref[...], zv_ref[...], lng_ref[...], lnb_ref[...],
                                             w_ref, bs_ref, s_scr, heads, hd)
        yb = u1 * s_scr[...]
        o_ref[...] = (yb * _rms(yb) * gb_ref[...]).astype(o_ref.dtype)

    blk = lambda col: pl.BlockSpec((SGU_CHUNK, d_sgu), lambda n: (n, col))
    return pl.pallas_call(
        body, name=name, grid=(L // SGU_CHUNK,),
        in_specs=[blk(1), blk(2), _vec_spec(d_sgu), _vec_spec(d_sgu),
                  pl.BlockSpec(w_s.shape, lambda n: (0, 0, 0)), pl.BlockSpec(b_s.shape, lambda n: (0, 0, 0)),
                  _vec_spec(d_sgu)],
        out_specs=blk(0),
        out_shape=jax.ShapeDtypeStruct((L, d_sgu), BF16),
        scratch_shapes=[pltpu.VMEM((SGU_CHUNK, d_sgu), F32)],
        compiler_params=_params(("parallel",)),
    )(z, z, ln_g, ln_b, w_s, b_s, g_b)


def sgu_bwd(z, dycat, ln_g, ln_b, w_s, b_s, g_b, *, name, d_sgu):
    L = z.shape[0]
    heads = w_s.shape[0]
    hd = d_sgu // heads

    def body(zu_ref, zv_ref, dn_ref, lng_ref, lnb_ref, w_ref, bs_ref, gb_ref,
             dzu_ref, dzv_ref, dw_ref, dbs_ref, dlng_ref, dlnb_ref, dgb_ref, s_scr, dv_scr):
        first = pl.program_id(0) == 0
        zu, zv, lng = zu_ref[...], zv_ref[...], lng_ref[...]
        u1, xhat, r, v2, tril = _sgu_forward_values(zu, zv, lng, lnb_ref[...], w_ref, bs_ref, s_scr, heads, hd)
        s = s_scr[...]
        yb = u1 * s
        rb = _rms(yb)
        dn = dn_ref[...]
        dyb = _rmsnorm_bwd(dn, yb, rb, gb_ref[...])
        dzu_ref[...] = (dyb * s * _gelu_grad(zu)).astype(BF16)
        ds = dyb * u1
        for h in range(heads):
            cols = slice(h * hd, (h + 1) * hd)
            ds_h = ds[:, cols]
            ds_hb = ds_h.astype(BF16)
            wm = jnp.where(tril, w_ref[h], 0.0).astype(BF16)
            dw_h = jnp.where(tril, lax.dot_general(ds_hb, v2[:, cols].astype(BF16), (((1,), (1,)), ((), ())),
                                                   preferred_element_type=F32), 0.0)
            db_h = jnp.sum(ds_h, axis=1, keepdims=True)
            dv_scr[:, cols] = lax.dot_general(wm, ds_hb, (((0,), (0,)), ((), ())), preferred_element_type=F32)

            @pl.when(first)
            def _():
                dw_ref[h] = dw_h
                dbs_ref[h] = db_h

            @pl.when(jnp.logical_not(first))
            def _():
                dw_ref[h] += dw_h
                dbs_ref[h] += db_h

        dv2 = dv_scr[...]
        dxh = dv2 * lng
        dv1 = r * (dxh - jnp.mean(dxh, axis=-1, keepdims=True)
                   - xhat * jnp.mean(dxh * xhat, axis=-1, keepdims=True))
        dzv_ref[...] = (dv1 * _gelu_grad(zv)).astype(BF16)
        p_lng, p_lnb, p_gb = _rowsum(dv2 * xhat), _rowsum(dv2), _rowsum(dn * yb * rb)

        @pl.when(first)
        def _():
            dlng_ref[...] = p_lng
            dlnb_ref[...] = p_lnb
            dgb_ref[...] = p_gb

        @pl.when(jnp.logical_not(first))
        def _():
            dlng_ref[...] += p_lng
            dlnb_ref[...] += p_lnb
            dgb_ref[...] += p_gb

    blk = lambda col: pl.BlockSpec((SGU_CHUNK, d_sgu), lambda n: (n, col))
    full3 = lambda shape: pl.BlockSpec(shape, lambda n: (0, 0, 0))
    return pl.pallas_call(
        body, name=name, grid=(L // SGU_CHUNK,),
        in_specs=[blk(1), blk(2), blk(1), _vec_spec(d_sgu), _vec_spec(d_sgu),
                  full3(w_s.shape), full3(b_s.shape), _vec_spec(d_sgu)],
        out_specs=[blk(0), blk(0), full3(w_s.shape), full3(b_s.shape),
                   _vec_spec(d_sgu), _vec_spec(d_sgu), _vec_spec(d_sgu)],
        out_shape=[jax.ShapeDtypeStruct((L, d_sgu), BF16), jax.ShapeDtypeStruct((L, d_sgu), BF16),
                   jax.ShapeDtypeStruct(w_s.shape, F32), jax.ShapeDtypeStruct(b_s.shape, F32),
                   jax.ShapeDtypeStruct((1, d_sgu), F32), jax.ShapeDtypeStruct((1, d_sgu), F32),
                   jax.ShapeDtypeStruct((1, d_sgu), F32)],
        scratch_shapes=[pltpu.VMEM((SGU_CHUNK, d_sgu), F32), pltpu.VMEM((SGU_CHUNK, d_sgu), F32)],
        compiler_params=_params(("arbitrary",)),
    )(z, z, dycat, ln_g, ln_b, w_s, b_s, g_b)


def _disc_lambda(lam_re, lam_im, log_step):
    lr = jnp.minimum(lam_re, LAMBDA_RE_MAX)
    li = lam_im
    dt = jnp.exp(log_step)
    mag = jnp.exp(lr * dt)
    ang = li * dt
    a_re = mag * jnp.cos(ang)
    a_im = mag * jnp.sin(ang)
    nr = a_re - 1.0
    ni = a_im
    den = lr * lr + li * li
    return a_re, a_im, (nr * lr + ni * li) / den, (ni * lr - nr * li) / den


def _disc_b(q_re, q_im, b_re, b_im):
    return q_re * b_re - q_im * b_im, q_re * b_im + q_im * b_re


def disc_lambda_fwd(lam_re, lam_im, log_step, *, name):
    def body(lr_ref, li_ref, ls_ref, ar_ref, ai_ref, qr_ref, qi_ref):
        ar_ref[...], ai_ref[...], qr_ref[...], qi_ref[...] = _disc_lambda(lr_ref[...], li_ref[...], ls_ref[...])

    sd = jax.ShapeDtypeStruct(lam_re.shape, F32)
    return pl.pallas_call(body, name=name, out_shape=[sd, sd, sd, sd], compiler_params=_params())(
        lam_re, lam_im, log_step)


def disc_lambda_bwd(lam_re, lam_im, log_step, cts, *, name):
    def body(lr_ref, li_ref, ls_ref, c0, c1, c2, c3, dlr_ref, dli_ref, dls_ref):
        _, vjp = jax.vjp(_disc_lambda, lr_ref[...], li_ref[...], ls_ref[...])
        dlr_ref[...], dli_ref[...], dls_ref[...] = vjp((c0[...], c1[...], c2[...], c3[...]))

    sd = jax.ShapeDtypeStruct(lam_re.shape, F32)
    return pl.pallas_call(body, name=name, out_shape=[sd, sd, jax.ShapeDtypeStruct(log_step.shape, F32)],
                          compiler_params=_params())(lam_re, lam_im, log_step, *cts)


def disc_b_fwd(q_re, q_im, b_re, b_im, *, name):
    def body(qr_ref, qi_ref, br_ref, bi_ref, or_ref, oi_ref):
        or_ref[...], oi_ref[...] = _disc_b(qr_ref[...], qi_ref[...], br_ref[...], bi_ref[...])

    sd = jax.ShapeDtypeStruct(b_re.shape, F32)
    return pl.pallas_call(body, name=name, out_shape=[sd, sd], compiler_params=_params())(q_re, q_im, b_re, b_im)


def disc_b_bwd(q_re, q_im, b_re, b_im, ct_re, ct_im, *, name):
    def body(qr_ref, qi_ref, br_ref, bi_ref, cr_ref, ci_ref, dqr_ref, dqi_ref, dbr_ref, dbi_ref):
        _, vjp = jax.vjp(_disc_b, qr_ref[...], qi_ref[...], br_ref[...], bi_ref[...])
        dqr_ref[...], dqi_ref[...], dbr_ref[...], dbi_ref[...] = vjp((cr_ref[...], ci_ref[...]))

    sq, sb = jax.ShapeDtypeStruct(q_re.shape, F32), jax.ShapeDtypeStruct(b_re.shape, F32)
    return pl.pallas_call(body, name=name, out_shape=[sq, sq, sb, sb], compiler_params=_params())(
        q_re, q_im, b_re, b_im, ct_re, ct_im)


def _lti_scan(xr, xi, ar, ai, reverse):
    T = xr.shape[0]
    row = lax.broadcasted_iota(jnp.int32, xr.shape, 0)
    k = 1
    while k < T:
        shift = T - k if reverse else k
        keep = (row < T - k) if reverse else (row >= k)
        sr = jnp.where(keep, pltpu.roll(xr, shift, 0), 0.0)
        si = jnp.where(keep, pltpu.roll(xi, shift, 0), 0.0)
        xr, xi = xr + ar * sr - ai * si, xi + ar * si + ai * sr
        ar, ai = ar * ar - ai * ai, 2.0 * ar * ai
        k *= 2
    return xr, xi


def _ssm_chunk(L):
    return _tile(L, 256, 8)


def ssm_fwd(z, bs_re, bs_im, cs_re, cs_im, a_re, a_im, d, *, name, bg=None):
    L = z.shape[0]
    NK, C, S = bs_re.shape
    T = _ssm_chunk(L)
    grid = (NK, L // T)
    bg_in_specs, bg_args, bg_out_specs, bg_out_shapes, bg_scratch, split = _carrier(bg, 8, 3, 4, grid)

    def body(*refs):
        ((u_ref, br_ref, bi_ref, cr_ref, ci_ref, ar_ref, ai_ref, d_ref), (y_ref, sr_ref, si_ref),
         (car_re, car_im, pw_re, pw_im), run_background) = split(refs)
        run_background()
        i = pl.program_id(1)
        ar, ai = ar_ref[...], ai_ref[...]

        @pl.when(i == 0)
        def _():
            row = lax.broadcasted_iota(jnp.int32, (T, S), 0)
            pr, pi = _lti_scan(jnp.where(row == 0, ar, 0.0), jnp.where(row == 0, ai, 0.0), ar, ai, False)
            pw_re[...] = pr
            pw_im[...] = pi
            car_re[...] = jnp.zeros_like(car_re)
            car_im[...] = jnp.zeros_like(car_im)

        u = u_ref[...]
        ub = u.astype(BF16)
        xr = jnp.dot(ub, br_ref[...], preferred_element_type=F32)
        xi = jnp.dot(ub, bi_ref[...], preferred_element_type=F32)
        xr, xi = _lti_scan(xr, xi, ar, ai, False)
        cr, ci = car_re[...], car_im[...]
        pr, pi = pw_re[...], pw_im[...]
        s_re = xr + pr * cr - pi * ci
        s_im = xi + pr * ci + pi * cr
        sr_ref[...] = s_re
        si_ref[...] = s_im
        car_re[...] = s_re[T - 1:T, :]
        car_im[...] = s_im[T - 1:T, :]
        y_ref[...] = (jnp.dot(s_re.astype(BF16), cr_ref[...], preferred_element_type=F32)
                      - jnp.dot(s_im.astype(BF16), ci_ref[...], preferred_element_type=F32)
                      + d_ref[...] * u)

    kspec = lambda shape: pl.BlockSpec((None,) + shape, lambda k, i: (k, 0, 0))
    res = pl.pallas_call(
        body, name=name, grid=grid,
        in_specs=[pl.BlockSpec((T, C), lambda k, i: (i, k)),
                  kspec((C, S)), kspec((C, S)), kspec((S, C)), kspec((S, C)),
                  kspec((1, S)), kspec((1, S)), kspec((1, C))] + bg_in_specs,
        out_specs=[pl.BlockSpec((T, C), lambda k, i: (i, k)),
                   pl.BlockSpec((T, S), lambda k, i: (i, k)), pl.BlockSpec((T, S), lambda k, i: (i, k))] + bg_out_specs,
        out_shape=[jax.ShapeDtypeStruct((L, NK * C), F32), jax.ShapeDtypeStruct((L, NK * S), F32),
                   jax.ShapeDtypeStruct((L, NK * S), F32)] + bg_out_shapes,
        scratch_shapes=[pltpu.VMEM((1, S), F32), pltpu.VMEM((1, S), F32),
                        pltpu.VMEM((T, S), F32), pltpu.VMEM((T, S), F32)] + bg_scratch,
        compiler_params=_params(_semantics(bg, ("parallel", "arbitrary"))),
    )(z, bs_re, bs_im, cs_re, cs_im, a_re, a_im, d, *bg_args)
    return _results(res, 3, bg)


def ssm_bwd(z, y0, dd_direct, dd_mm, s_re, s_im, bst_re, bst_im, cst_re, cst_im, a_re, a_im, d, *, name):
    L = z.shape[0]
    NK, S, C = bst_re.shape
    T = _ssm_chunk(L)
    nchunk = L // T
    tail = T // 8

    def body(u_ref, y_ref, d1_ref, d2_ref, sr_ref, si_ref, pr_ref, pi_ref,
             btr_ref, bti_ref, ctr_ref, cti_ref, ar_ref, ai_ref, d_ref,
             du_ref, dbr_ref, dbi_ref, dcr_ref, dci_ref, dar_ref, dai_ref, dd_ref,
             car_re, car_im, pw_re, pw_im):
        i = pl.program_id(1)
        chunk = nchunk - 1 - i
        ar, ai = ar_ref[...], ai_ref[...]
        nai = -ai
        row = lax.broadcasted_iota(jnp.int32, (T, S), 0)

        @pl.when(i == 0)
        def _():
            qr, qi = _lti_scan(jnp.where(row == T - 1, ar, 0.0), jnp.where(row == T - 1, nai, 0.0), ar, nai, True)
            pw_re[...] = qr
            pw_im[...] = qi
            car_re[...] = jnp.zeros_like(car_re)
            car_im[...] = jnp.zeros_like(car_im)

        u = u_ref[...]
        dy = (d1_ref[...] + d2_ref[...]) * _gelu_grad(y_ref[...])
        dyb = dy.astype(BF16)
        gr = jnp.dot(dyb, ctr_ref[...], preferred_element_type=F32)
        gi = -jnp.dot(dyb, cti_ref[...], preferred_element_type=F32)
        lr, li = _lti_scan(gr, gi, ar, nai, True)
        cr, ci = car_re[...], car_im[...]
        qr, qi = pw_re[...], pw_im[...]
        lr = lr + qr * cr - qi * ci
        li = li + qr * ci + qi * cr
        car_re[...] = lr[0:1, :]
        car_im[...] = li[0:1, :]

        s_re, s_im = sr_ref[...], si_ref[...]
        has_prev = (chunk > 0).astype(F32)
        prev_re = pr_ref[7:8, :] * has_prev
        prev_im = pi_ref[7:8, :] * has_prev
        sp_re = jnp.where(row == 0, prev_re, pltpu.roll(s_re, 1, 0))
        sp_im = jnp.where(row == 0, prev_im, pltpu.roll(s_im, 1, 0))
        p_ar = _rowsum(lr * sp_re + li * sp_im)
        p_ai = _rowsum(li * sp_re - lr * sp_im)

        lrb, lib, ub = lr.astype(BF16), li.astype(BF16), u.astype(BF16)
        du = (dy * d_ref[...] + jnp.dot(lrb, btr_ref[...], preferred_element_type=F32)
              + jnp.dot(lib, bti_ref[...], preferred_element_type=F32))
        du_ref[...] = du.astype(BF16)
        tdot = lambda p, q: lax.dot_general(p, q, (((0,), (0,)), ((), ())), preferred_element_type=F32)
        p_br, p_bi = tdot(ub, lrb), tdot(ub, lib)
        p_cr, p_ci = tdot(s_re.astype(BF16), dyb), -tdot(s_im.astype(BF16), dyb)
        p_dd = _rowsum(dy * u)

        @pl.when(i == 0)
        def _():
            dar_ref[...] = p_ar
            dai_ref[...] = p_ai
            dbr_ref[...] = p_br
            dbi_ref[...] = p_bi
            dcr_ref[...] = p_cr
            dci_ref[...] = p_ci
            dd_ref[...] = p_dd

        @pl.when(i > 0)
        def _():
            dar_ref[...] += p_ar
            dai_ref[...] += p_ai
            dbr_ref[...] += p_br
            dbi_ref[...] += p_bi
            dcr_ref[...] += p_cr
            dci_ref[...] += p_ci
            dd_ref[...] += p_dd

    rev = lambda k, i: (nchunk - 1 - i, k)
    prev = lambda k, i: (jnp.maximum((nchunk - 1 - i) * tail - 1, 0), k)
    kspec = lambda shape: pl.BlockSpec((None,) + shape, lambda k, i: (k, 0, 0))
    return pl.pallas_call(
        body, name=name, grid=(NK, nchunk),
        in_specs=[pl.BlockSpec((T, C), rev), pl.BlockSpec((T, C), rev), pl.BlockSpec((T, C), rev),
                  pl.BlockSpec((T, C), rev), pl.BlockSpec((T, S), rev), pl.BlockSpec((T, S), rev),
                  pl.BlockSpec((8, S), prev), pl.BlockSpec((8, S), prev),
                  kspec((S, C)), kspec((S, C)), kspec((C, S)), kspec((C, S)),
                  kspec((1, S)), kspec((1, S)), kspec((1, C))],
        out_specs=[pl.BlockSpec((T, C), rev), kspec((C, S)), kspec((C, S)), kspec((S, C)), kspec((S, C)),
                   kspec((1, S)), kspec((1, S)), kspec((1, C))],
        out_shape=[jax.ShapeDtypeStruct((L, NK * C), BF16),
                   jax.ShapeDtypeStruct((NK, C, S), F32), jax.ShapeDtypeStruct((NK, C, S), F32),
                   jax.ShapeDtypeStruct((NK, S, C), F32), jax.ShapeDtypeStruct((NK, S, C), F32),
                   jax.ShapeDtypeStruct((NK, 1, S), F32), jax.ShapeDtypeStruct((NK, 1, S), F32),
                   jax.ShapeDtypeStruct((NK, 1, C), F32)],
        scratch_shapes=[pltpu.VMEM((1, S), F32), pltpu.VMEM((1, S), F32),
                        pltpu.VMEM((T, S), F32), pltpu.VMEM((T, S), F32)],
        compiler_params=_params(("parallel", "arbitrary")),
    )(z, y0, dd_direct, dd_mm, s_re, s_im, s_re, s_im, bst_re, bst_im, cst_re, cst_im, a_re, a_im, d)


def _block_diag(v):
    NK, SG, R, Q = v.shape
    eye = jnp.eye(SG, dtype=v.dtype)
    return (v[:, :, :, None, :] * eye[None, :, None, :, None]).reshape(NK, SG * R, SG * Q)


def _block_diag_part(m, SG):
    NK, RR, QQ = m.shape
    R, Q = RR // SG, QQ // SG
    eye = jnp.eye(SG, dtype=m.dtype)
    return jnp.sum(m.reshape(NK, SG, R, SG, Q) * eye[None, :, None, :, None], axis=3)


def _position():
    return lax.axis_index("x"), lax.axis_index("y"), lax.axis_index("c")


def _other_chips(x, y):
    return [(1 - x, y), (x, 1 - y), (1 - x, 1 - y)]


def _gather_phases(n):
    def parts(ins, outs, sems):
        send_sems, recv_sems, local_sems = sems
        x, y, c = _position()
        me, sibling = (x, y, c), (x, y, 1 - c)
        chips = _other_chips(x, y)

        def block(a, pos):
            return outs[a].at[4 * pos[0] + 2 * pos[1] + pos[2]]

        def copy(a, k, pos, to, src=None):
            return pltpu.make_async_remote_copy(
                src_ref=block(a, pos) if src is None else src, dst_ref=block(a, pos),
                send_sem=send_sems.at[7 * a + k], recv_sem=recv_sems.at[7 * a + k],
                device_id=to, device_id_type=MESH)

        mine = [pltpu.make_async_copy(ins[a], block(a, me), local_sems.at[a]) for a in range(n)]
        first = []
        for a in range(n):
            first.append(copy(a, 0, me, sibling, src=ins[a]))
            first += [copy(a, 1 + j, me, (*chip, c), src=ins[a]) for j, chip in enumerate(chips)]
        passed = [copy(a, 4 + j, (*chip, c), sibling) for a in range(n) for j, chip in enumerate(chips)]
        arrived = [copy(a, 1 + j, (*chip, c), me) for a in range(n) for j, chip in enumerate(chips)]
        from_sibling = []
        for a in range(n):
            from_sibling.append(copy(a, 0, sibling, me))
            from_sibling += [copy(a, 4 + j, (*chip, 1 - c), me) for j, chip in enumerate(chips)]
        return mine, first, passed, arrived, from_sibling

    def send(ins, outs, sems):
        mine, first, _, _, _ = parts(ins, outs, sems)
        for cp in mine + first:
            cp.start()

    def forward(ins, outs, sems):
        _, _, passed, arrived, _ = parts(ins, outs, sems)
        for got, fwd in zip(arrived, passed):
            got.wait_recv()
            fwd.start()

    def finish(ins, outs, sems):
        mine, first, passed, _, from_sibling = parts(ins, outs, sems)
        for cp in from_sibling:
            cp.wait_recv()
        for cp in first + passed:
            cp.wait_send()
        for cp in mine:
            cp.wait()

    return [send, forward, finish]


def _gather_shapes(shards):
    n = len(shards)
    return ([jax.ShapeDtypeStruct((N_DEV,) + s.shape, s.dtype) for s in shards],
            [pltpu.SemaphoreType.DMA((7 * n,)), pltpu.SemaphoreType.DMA((7 * n,)), pltpu.SemaphoreType.DMA((n,))])


def gather_background(shards):
    out_shapes, scratch = _gather_shapes(shards)
    return Background(shards, out_shapes, scratch, _gather_phases(len(shards)))


def all_gather_blocks(shards, *, name):
    n = len(shards)
    out_shapes, scratch = _gather_shapes(shards)

    def body(*refs):
        for phase in _gather_phases(n):
            phase(refs[:n], refs[n:2 * n], refs[2 * n:])

    return pl.pallas_call(
        body, name=name, in_specs=[ANY] * n, out_specs=[ANY] * n, out_shape=out_shapes, scratch_shapes=scratch,
    )(*shards)


def sibling_exchange(grads, *, name):
    n = len(grads)

    def body(*refs):
        ins, outs = refs[:n], refs[n:2 * n]
        send_sems, recv_sems = refs[2 * n:]
        x, y, c = _position()
        copies = []
        for a in range(n):
            for q in range(4):
                copies.append(pltpu.make_async_remote_copy(
                    src_ref=ins[a].at[2 * q + 1 - c], dst_ref=outs[a].at[q],
                    send_sem=send_sems.at[4 * a + q], recv_sem=recv_sems.at[4 * a + q],
                    device_id=(x, y, 1 - c), device_id_type=MESH))
        for cp in copies:
            cp.start()
        for cp in copies:
            cp.wait()

    return pl.pallas_call(
        body, name=name,
        in_specs=[ANY] * n, out_specs=[ANY] * n,
        out_shape=[jax.ShapeDtypeStruct((4,) + g.shape[1:], g.dtype) for g in grads],
        scratch_shapes=[pltpu.SemaphoreType.DMA((4 * n,)), pltpu.SemaphoreType.DMA((4 * n,))],
    )(*grads)


def _chip_exchange_phases(n):
    def copies(ins, outs, sems):
        x, y, c = _position()
        return [pltpu.make_async_remote_copy(
            src_ref=ins[a].at[2 * chip[0] + chip[1]], dst_ref=outs[a].at[j],
            send_sem=sems[0].at[3 * a + j], recv_sem=sems[1].at[3 * a + j],
            device_id=(*chip, c), device_id_type=MESH)
            for a in range(n) for j, chip in enumerate(_other_chips(x, y))]

    def send(ins, outs, sems):
        for cp in copies(ins, outs, sems):
            cp.start()

    def finish(ins, outs, sems):
        for cp in copies(ins, outs, sems):
            cp.wait()

    return [send, finish]


def chip_exchange_background(parts):
    n = len(parts)
    return Background(parts, [jax.ShapeDtypeStruct((3,) + p.shape[1:], p.dtype) for p in parts],
                      [pltpu.SemaphoreType.DMA((3 * n,)), pltpu.SemaphoreType.DMA((3 * n,))],
                      _chip_exchange_phases(n))


def add_pairs(grads, theirs, core, *, name, tm=512):
    _, R, C = theirs.shape
    tm = _tile(R, tm, 16)

    def body(core_ref, a_ref, b_ref, o_ref):
        o_ref[...] = (a_ref[...].astype(F32) + b_ref[...].astype(F32)).astype(o_ref.dtype)

    spec = pl.BlockSpec((None, tm, C), lambda q, i, core_ref: (q, i, 0))
    return pl.pallas_call(
        body, name=name,
        grid_spec=pltpu.PrefetchScalarGridSpec(
            num_scalar_prefetch=1, grid=(4, R // tm),
            in_specs=[pl.BlockSpec((None, tm, C), lambda q, i, core_ref: (2 * q + core_ref[0], i, 0)), spec],
            out_specs=spec),
        out_shape=jax.ShapeDtypeStruct(theirs.shape, BF16),
        compiler_params=_params(("parallel", "parallel")),
    )(core, grads, theirs)


def _adamw(w, g, m, v):
    m = ADAM_B1 * m + (1.0 - ADAM_B1) * g
    v = ADAM_B2 * v + (1.0 - ADAM_B2) * (g * g)
    m_hat = m / (1.0 - ADAM_B1 ** ADAM_STEP)
    v_hat = v / (1.0 - ADAM_B2 ** ADAM_STEP)
    delta = -ADAM_LR * (m_hat / (jnp.sqrt(v_hat) + ADAM_EPS) + ADAM_WD * w)
    return delta, m, v


def adamw_sharded(w, m, v, own_a, own_b, others, *, name, tm=256):
    R, C = w.shape
    tm = _tile(R, tm, 16)

    def body(w_ref, m_ref, v_ref, a_ref, b_ref, o_ref, g_ref, d_ref, nm_ref, nv_ref):
        g = a_ref[...].astype(F32) + b_ref[...].astype(F32)
        for j in range(3):
            g = g + o_ref[j].astype(F32)
        g_ref[...] = g
        d_ref[...], nm_ref[...], nv_ref[...] = _adamw(w_ref[...], g, m_ref[...], v_ref[...])

    spec = pl.BlockSpec((tm, C), lambda i: (i, 0))
    sd = jax.ShapeDtypeStruct((R, C), F32)
    return pl.pallas_call(
        body, name=name, grid=(R // tm,),
        in_specs=[spec, spec, spec, spec, spec, pl.BlockSpec((3, tm, C), lambda i: (0, i, 0))],
        out_specs=[spec, spec, spec, spec], out_shape=[sd, sd, sd, sd],
        compiler_params=_params(("parallel",)),
    )(w, m, v, own_a, own_b, others)


def adamw_packed(w, m, v, gathered, *, name, tm=512):
    R, C = w.shape
    tm = _tile(R, tm, 8)

    def body(w_ref, m_ref, v_ref, ga_ref, g_ref, d_ref, nm_ref, nv_ref):
        g = ga_ref[0]
        for dev in range(1, N_DEV):
            g = g + ga_ref[dev]
        g_ref[...] = g
        d_ref[...], nm_ref[...], nv_ref[...] = _adamw(w_ref[...], g, m_ref[...], v_ref[...])

    spec = pl.BlockSpec((tm, C), lambda i: (i, 0))
    sd = jax.ShapeDtypeStruct((R, C), F32)
    return pl.pallas_call(
        body, name=name, grid=(R // tm,),
        in_specs=[spec, spec, spec, pl.BlockSpec((N_DEV, tm, C), lambda i: (0, i, 0))],
        out_specs=[spec, spec, spec, spec], out_shape=[sd, sd, sd, sd],
        compiler_params=_params(("parallel",)),
    )(w, m, v, gathered)


def _pack(arrays):
    parts = []
    for a in arrays:
        flat = a.reshape(-1).astype(F32)
        parts.append(jnp.pad(flat, (0, (-flat.size) % PACK_ALIGN)))
    return jnp.concatenate(parts).reshape(-1, LANE)


def _unpack(packed, shapes):
    flat = packed.reshape(-1)
    out, off = [], 0
    for shape in shapes:
        size = math.prod(shape)
        out.append(flat[off:off + size].reshape(shape))
        off += size + (-size) % PACK_ALIGN
    return out


SHARDED = ("w_in", "ssm_glu_w", "w_out", "w_ffn_in", "w_ffn_out", "w_ple_gate", "w_ple_proj")
SMALL = ("norm_mix_g", "ssm_lambda_re", "ssm_lambda_im", "ssm_log_step", "ssm_b_re", "ssm_b_im", "ssm_c_re",
         "ssm_c_im", "ssm_d", "ssm_glu_b", "sgu_ln_g", "sgu_ln_b", "sgu_w", "sgu_b", "out_norm_ssm_g",
         "out_norm_sgu_g", "norm_ffn_g", "norm_ple_g", "b_ple_gate", "final_norm_g")
WEIGHTS = ("norm_mix_g", "w_in", "ssm_lambda_re", "ssm_lambda_im", "ssm_log_step", "ssm_b_re", "ssm_b_im",
           "ssm_c_re", "ssm_c_im", "ssm_d", "ssm_glu_w", "ssm_glu_b", "sgu_ln_g", "sgu_ln_b", "sgu_w", "sgu_b",
           "out_norm_ssm_g", "out_norm_sgu_g", "w_out", "norm_ffn_g", "w_ffn_in", "w_ffn_out", "norm_ple_g",
           "w_ple_gate", "b_ple_gate", "w_ple_proj", "final_norm_g")


def _step(x, p, loss_target, w, m, v):
    L, D = x.shape[1], x.shape[2]
    x2d, p2d, tgt = x.reshape(L, D), p.reshape(L, -1), loss_target.reshape(L, D)
    d_ssm = w["ssm_glu_w"].shape[2]
    d_sgu = w["sgu_ln_g"].shape[1]
    G, P, H = w["ssm_b_re"].shape[1:]
    SG = min(SSM_SUPER, G)
    NK = G // SG
    row = lambda a: a.reshape(1, -1)

    shard2d = {n: w[n].reshape(w[n].shape[1:]) for n in SHARDED}
    shard_bf = {n: shard2d[n].astype(BF16) for n in SHARDED}
    (w_in_blk,) = all_gather_blocks([shard_bf["w_in"]], name="gather_w_in")
    F = shard2d["w_ffn_in"].shape[1] * 4

    lam_re, lam_im, log_step = w["ssm_lambda_re"][0], w["ssm_lambda_im"][0], w["ssm_log_step"][0].reshape(G, 1)
    a_re, a_im, q_re, q_im = disc_lambda_fwd(lam_re, lam_im, log_step, name="s5_discretise_lambda")
    bt_re = w["ssm_b_re"][0].transpose(2, 0, 1).reshape(H, G * P)
    bt_im = w["ssm_b_im"][0].transpose(2, 0, 1).reshape(H, G * P)
    bbar_re, bbar_im = disc_b_fwd(row(q_re), row(q_im), bt_re, bt_im, name="s5_discretise_b")
    to_bs = lambda t: _block_diag(t.reshape(H, NK, SG, P).transpose(1, 2, 0, 3))
    to_cs = lambda t: _block_diag(t.reshape(NK, SG, H, P).transpose(0, 1, 3, 2))
    bs_re, bs_im = to_bs(bbar_re), to_bs(bbar_im)
    cs_re, cs_im = to_cs(w["ssm_c_re"][0]), to_cs(w["ssm_c_im"][0])
    a_re_k, a_im_k = a_re.reshape(NK, 1, SG * P), a_im.reshape(NK, 1, SG * P)
    d_k = w["ssm_d"][0].reshape(NK, 1, SG * H)
    bf = lambda t: t.astype(BF16)
    tr = lambda t: jnp.swapaxes(t, 1, 2)

    h1 = norm_fwd(x2d, w["norm_mix_g"], name="norm_mix")
    z, (w_glu, w_out) = mm_nn(h1, w_in_blk, name="in_proj", out_dtype=F32, tm=512, tn=512, tk=2048,
                              bg=gather_background([shard_bf["ssm_glu_w"], shard_bf["w_out"]]))
    w_glu, w_out = w_glu.reshape(d_ssm, d_ssm), w_out.reshape(D, D)
    (y0, s_re, s_im), (w_ffn_in_blk,) = ssm_fwd(
        z, bf(bs_re), bf(bs_im), bf(cs_re), bf(cs_im), a_re_k, a_im_k, d_k, name="s5_scan",
        bg=gather_background([shard_bf["w_ffn_in"]]))
    ya1 = glu_pre(y0, name="s5_gelu")
    t_glu = mm_nn(ya1, w_glu, name="s5_glu_proj", out_dtype=F32, tm=512, tn=512, tk=2048)
    n_a = glu_post(y0, t_glu, w["ssm_glu_b"], w["out_norm_ssm_g"], name="s5_glu_norm")
    b_s3 = w["sgu_b"][0][:, :, None]
    n_b = sgu_fwd(z, w["sgu_ln_g"], w["sgu_ln_b"], w["sgu_w"][0], b_s3, w["out_norm_sgu_g"], name="sgu", d_sgu=d_sgu)
    ycat = jnp.concatenate([n_a, n_b], axis=1)
    x1 = mm_nn(ycat, w_out, name="out_proj", out_dtype=F32, tm=512, tn=512, tk=2048, residual=x2d)
    h2 = norm_fwd(x1, w["norm_ffn_g"], name="norm_ffn")
    (act, gate_ff, up_ff), (w_ffn_out, w_gate, w_ple_blk) = ffn_in_swiglu(
        h2, w_ffn_in_blk, name="ffn_in_swiglu", tm=256,
        bg=gather_background([shard_bf["w_ffn_out"], shard_bf["w_ple_gate"], shard_bf["w_ple_proj"]]))
    w_ffn_out, w_gate = w_ffn_out.reshape(F, D), w_gate.reshape(D, D)
    x2 = mm_nn(act, w_ffn_out, name="ffn_out", out_dtype=F32, tm=1024, tn=1024, tk=512, residual=x1)
    h3 = norm_fwd(x2, w["norm_ple_g"], name="norm_ple")
    gpre = mm_nn(h3, w_gate, name="ple_gate", out_dtype=F32, tm=512, tn=512, tk=2048)
    pp = mm_nn(bf(p2d), w_ple_blk, name="ple_proj", out_dtype=F32, tm=512, tn=512, tk=2048)

    dx3, dpre, dpp, loss_part, d_final_g, d_b_gate = head_and_loss(
        x2, gpre, w["b_ple_gate"], pp, row(w["final_norm_g"]), tgt, name="head_and_loss")
    x_pos, y_pos, c_pos = _position()
    me, q_me = 4 * x_pos + 2 * y_pos + c_pos, 2 * x_pos + y_pos
    core = jnp.reshape(c_pos, (1,)).astype(jnp.int32)
    own, others = {}, {}

    def chip_partials(named, tag):
        names = list(named)
        g8 = [named[n].reshape((N_DEV,) + shard2d[n].shape) for n in names]
        theirs = sibling_exchange(g8, name="grads_to_sibling_" + tag)
        for n, g, t in zip(names, g8, theirs):
            own[n] = (lax.dynamic_index_in_dim(g, me, 0, keepdims=False),
                      lax.dynamic_index_in_dim(t, q_me, 0, keepdims=False))
        return names, chip_exchange_background(
            [add_pairs(g, t, core, name="chip_sum_" + n) for n, g, t in zip(names, g8, theirs)])

    d_w_gate = mm_tn(h3, dpre, name="d_w_ple_gate", out_dtype=BF16, tm=512, tko=1024, tno=1024)
    d_w_ple = mm_tn(bf(p2d), dpp, name="d_w_ple_proj", out_dtype=BF16, tm=512, tko=1024, tno=1024, out_blocks=N_DEV)
    names, bg = chip_partials({"w_ple_gate": d_w_gate, "w_ple_proj": d_w_ple}, "ple")
    dh3, got = mm_nt(dpre, w_gate, name="d_h_ple", out_dtype=F32, tm=512, tko=1024, tc=2048, bg=bg)
    others.update(zip(names, got))
    dx2, dx2b, d_ple_g = norm_bwd(dh3, x2, w["norm_ple_g"], dx3, name="d_norm_ple", want_bf16=True)
    dact = mm_nt(dx2b, w_ffn_out, name="d_act", out_dtype=F32, tm=512, tko=1408, tc=2048)
    d_w_ffn_out = mm_tn(act, dx2b, name="d_w_ffn_out", out_dtype=BF16, tm=512, tko=1408, tno=1024)
    names, bg = chip_partials({"w_ffn_out": d_w_ffn_out}, "ffn_out")
    dgate, dup = swiglu_bwd(dact, gate_ff, up_ff, name="d_swiglu")
    dgu = jnp.concatenate([dgate, dup], axis=1)
    d_w_ffn_in, got = mm_tn(h2, dgu, name="d_w_ffn_in", out_dtype=BF16, tm=512, tko=1024, tno=1408,
                            out_blocks=N_DEV, bg=bg)
    others.update(zip(names, got))
    names, bg = chip_partials({"w_ffn_in": d_w_ffn_in}, "ffn_in")
    dh2, got = mm_nt(dgu, w_ffn_in_blk, name="d_h_ffn", out_dtype=F32, tm=1024, tko=1024, tc=1408, bg=bg)
    others.update(zip(names, got))
    dx1, dx1b, d_ffn_g = norm_bwd(dh2, x1, w["norm_ffn_g"], dx2, name="d_norm_ffn", want_bf16=True)
    dycat = mm_nt(dx1b, w_out, name="d_ycat", out_dtype=F32, tm=512, tko=1024, tc=2048)
    d_w_out = mm_tn(ycat, dx1b, name="d_w_out", out_dtype=BF16, tm=512, tko=1024, tno=1024)
    dzu, dzv, d_sgu_w, d_sgu_b, d_ln_g, d_ln_b, d_g_b = sgu_bwd(
        z, dycat, w["sgu_ln_g"], w["sgu_ln_b"], w["sgu_w"][0], b_s3, w["out_norm_sgu_g"], name="d_sgu", d_sgu=d_sgu)
    dt_glu, dd_direct, d_g_a, d_glu_b = glu_post_bwd(
        y0, t_glu, w["ssm_glu_b"], w["out_norm_ssm_g"], dycat, name="d_s5_glu_norm")
    d_w_glu = mm_tn(ya1, dt_glu, name="d_w_glu", out_dtype=BF16, tm=512, tko=1024, tno=1024)
    names, bg = chip_partials({"w_out": d_w_out, "ssm_glu_w": d_w_glu}, "out_glu")
    dd_mm = mm_nt(dt_glu, w_glu, name="d_s5_glu_proj", out_dtype=F32, tm=512, tko=1024, tc=2048)
    du, d_bs_re, d_bs_im, d_cs_re, d_cs_im, d_a_re, d_a_im, d_d = ssm_bwd(
        z, y0, dd_direct, dd_mm, s_re, s_im, bf(tr(bs_re)), bf(tr(bs_im)), bf(tr(cs_re)), bf(tr(cs_im)),
        a_re_k, a_im_k, d_k, name="d_s5_scan")
    dz = jnp.concatenate([du, dzu, dzv], axis=1)
    d_w_in, got = mm_tn(h1, dz, name="d_w_in", out_dtype=BF16, tm=512, tko=1024, tno=1024, out_blocks=N_DEV, bg=bg)
    others.update(zip(names, got))
    names, bg = chip_partials({"w_in": d_w_in}, "w_in")
    dh1, got = mm_nt(dz, w_in_blk, name="d_h_mix", out_dtype=F32, tm=512, tko=1024, tc=2048, bg=bg)
    others.update(zip(names, got))
    grad_x, d_mix_g = norm_bwd(dh1, x2d, w["norm_mix_g"], dx1, name="d_norm_mix", want_bf16=False)

    from_bs = lambda t: _block_diag_part(t, SG).transpose(2, 0, 1, 3).reshape(H, G * P)
    from_cs = lambda t: _block_diag_part(t, SG).transpose(0, 1, 3, 2).reshape(1, G, H, P)
    d_q_re, d_q_im, d_bt_re, d_bt_im = disc_b_bwd(row(q_re), row(q_im), bt_re, bt_im, from_bs(d_bs_re),
                                                  from_bs(d_bs_im), name="d_s5_discretise_b")
    d_lam_re, d_lam_im, d_log_step = disc_lambda_bwd(
        lam_re, lam_im, log_step,
        (d_a_re.reshape(G, P), d_a_im.reshape(G, P), d_q_re.reshape(G, P), d_q_im.reshape(G, P)),
        name="d_s5_discretise_lambda")
    from_bt = lambda t: t.reshape(H, G, P).transpose(1, 2, 0).reshape(1, G, P, H)
    small_grads = {
        "norm_mix_g": d_mix_g, "ssm_lambda_re": d_lam_re, "ssm_lambda_im": d_lam_im, "ssm_log_step": d_log_step,
        "ssm_b_re": from_bt(d_bt_re), "ssm_b_im": from_bt(d_bt_im), "ssm_c_re": from_cs(d_cs_re),
        "ssm_c_im": from_cs(d_cs_im), "ssm_d": d_d, "ssm_glu_b": d_glu_b, "sgu_ln_g": d_ln_g, "sgu_ln_b": d_ln_b,
        "sgu_w": d_sgu_w, "sgu_b": d_sgu_b, "out_norm_ssm_g": d_g_a, "out_norm_sgu_g": d_g_b,
        "norm_ffn_g": d_ffn_g, "norm_ple_g": d_ple_g, "b_ple_gate": d_b_gate, "final_norm_g": d_final_g,
    }

    out = {}
    for n in SHARDED:
        res = adamw_sharded(shard2d[n], m[n].reshape(shard2d[n].shape), v[n].reshape(shard2d[n].shape),
                            own[n][0], own[n][1], others[n], name="adamw_" + n)
        out[n] = [r.reshape(w[n].shape) for r in res]

    pack_g = _pack([loss_part] + [small_grads[n] for n in SMALL])
    zero_tile = jnp.zeros((8, LANE), F32)
    pack_w = _pack([zero_tile] + [w[n] for n in SMALL])
    pack_m = _pack([zero_tile] + [m[n] for n in SMALL])
    pack_v = _pack([zero_tile] + [v[n] for n in SMALL])
    (all_g,) = all_gather_blocks([pack_g], name="gather_small_grads")
    res = adamw_packed(pack_w, pack_m, pack_v, all_g, name="adamw_small")
    shapes = [(8, LANE)] + [w[n].shape for n in SMALL]
    unpacked = [_unpack(r, shapes) for r in res]
    loss = unpacked[0][0][0, 0]
    for i, n in enumerate(SMALL):
        out[n] = [u[i + 1] for u in unpacked]

    grads = [out[n][0] for n in WEIGHTS]
    deltas = [out[n][1] for n in WEIGHTS]
    new_m = [out[n][2] for n in WEIGHTS]
    new_v = [out[n][3] for n in WEIGHTS]
    return (loss, grad_x.reshape(x.shape), *grads, *deltas, *new_m, *new_v)


def kernel(x, p, norm_mix_g, w_in, ssm_lambda_re, ssm_lambda_im, ssm_log_step, ssm_b_re, ssm_b_im, ssm_c_re, ssm_c_im, ssm_d, ssm_glu_w, ssm_glu_b, sgu_ln_g, sgu_ln_b, sgu_w, sgu_b, out_norm_ssm_g, out_norm_sgu_g, w_out, norm_ffn_g, w_ffn_in, w_ffn_out, norm_ple_g, w_ple_gate, b_ple_gate, w_ple_proj, final_norm_g, loss_target, m_norm_mix_g, m_w_in, m_ssm_lambda_re, m_ssm_lambda_im, m_ssm_log_step, m_ssm_b_re, m_ssm_b_im, m_ssm_c_re, m_ssm_c_im, m_ssm_d, m_ssm_glu_w, m_ssm_glu_b, m_sgu_ln_g, m_sgu_ln_b, m_sgu_w, m_sgu_b, m_out_norm_ssm_g, m_out_norm_sgu_g, m_w_out, m_norm_ffn_g, m_w_ffn_in, m_w_ffn_out, m_norm_ple_g, m_w_ple_gate, m_b_ple_gate, m_w_ple_proj, m_final_norm_g, v_norm_mix_g, v_w_in, v_ssm_lambda_re, v_ssm_lambda_im, v_ssm_log_step, v_ssm_b_re, v_ssm_b_im, v_ssm_c_re, v_ssm_c_im, v_ssm_d, v_ssm_glu_w, v_ssm_glu_b, v_sgu_ln_g, v_sgu_ln_b, v_sgu_w, v_sgu_b, v_out_norm_ssm_g, v_out_norm_sgu_g, v_w_out, v_norm_ffn_g, v_w_ffn_in, v_w_ffn_out, v_norm_ple_g, v_w_ple_gate, v_b_ple_gate, v_w_ple_proj, v_final_norm_g):
    given = dict(locals())
    w = {n: given[n] for n in WEIGHTS}
    m = {n: given["m_" + n] for n in WEIGHTS}
    v = {n: given["v_" + n] for n in WEIGHTS}
    return _step(x, p, loss_target, w, m, v)
```

```python
import functools
import math

import jax
import jax.numpy as jnp
from jax import lax
from jax.experimental import pallas as pl
from jax.experimental.pallas import tpu as pltpu

F32 = jnp.float32
BF16 = jnp.bfloat16
MESH = pl.DeviceIdType.MESH
ANY = pl.BlockSpec(memory_space=pl.ANY)

N_DEV = 8
EPS = 1e-6
LAMBDA_RE_MAX = -1e-4
SSM_GROUP = 16
SSM_STATE = 64
SSM_SUPER = 16
SGU_CHUNK = 128
ADAM_LR, ADAM_B1, ADAM_B2, ADAM_EPS, ADAM_WD, ADAM_STEP = 0.001, 0.9, 0.999, 1e-08, 0.01, 10
VMEM_LIMIT = 52 * 1024 * 1024
LANE = 128
PACK_ALIGN = 8 * LANE
PACK_ROWS = 512
GATHER_FORWARD_AT = 0.85

_GELU_C = math.sqrt(2.0 / math.pi)


def _params(sem=None):
    return pltpu.CompilerParams(dimension_semantics=sem, vmem_limit_bytes=VMEM_LIMIT)


def _tile(dim, pref, unit=LANE):
    if dim <= pref:
        return dim
    t = (pref // unit) * unit
    while t >= unit:
        if dim % t == 0:
            return t
        t -= unit
    return dim


def _gelu(x):
    return 0.5 * x * (1.0 + jnp.tanh(_GELU_C * (x + 0.044715 * x * x * x)))


def _gelu_grad(x):
    t = jnp.tanh(_GELU_C * (x + 0.044715 * x * x * x))
    return 0.5 * (1.0 + t) + 0.5 * x * (1.0 - t * t) * (_GELU_C * (1.0 + 3.0 * 0.044715 * x * x))


def _rms(x):
    return lax.rsqrt(jnp.mean(x * x, axis=-1, keepdims=True) + EPS)


def _rmsnorm_bwd(dy, x, r, g):
    dyg = dy * g
    return r * dyg - x * (r * r * r) * jnp.mean(dyg * x, axis=-1, keepdims=True)


def _rowsum(v):
    return jnp.sum(v, axis=0, keepdims=True)


class Background:
    def __init__(self, inputs, out_shapes, scratch, phases):
        self.inputs, self.out_shapes, self.scratch, self.phases = list(inputs), list(out_shapes), list(scratch), phases

    def emit(self, step, nsteps, ins, outs, scratch):
        for place, phase in self.phases:
            at = min(int(place * nsteps), nsteps - 1)

            @pl.when(step == at)
            def _():
                phase(ins, outs, scratch)


def combine_backgrounds(first, second):
    ni, no, ns = len(first.inputs), len(first.out_shapes), len(first.scratch)
    phases = [(place, lambda i, o, s, f=fn: f(i[:ni], o[:no], s[:ns])) for place, fn in first.phases]
    phases += [(place, lambda i, o, s, f=fn: f(i[ni:], o[no:], s[ns:])) for place, fn in second.phases]
    return Background(first.inputs + second.inputs, first.out_shapes + second.out_shapes,
                      first.scratch + second.scratch, sorted(phases, key=lambda p: p[0]))


def _carrier(bg, n_in, n_out, n_scratch, grid):
    nbi = len(bg.inputs) if bg else 0
    nbo = len(bg.out_shapes) if bg else 0
    nsteps = math.prod(grid)

    def split(refs):
        ins = refs[:n_in]
        bg_ins = refs[n_in:n_in + nbi]
        outs = refs[n_in + nbi:n_in + nbi + n_out]
        bg_outs = refs[n_in + nbi + n_out:n_in + nbi + n_out + nbo]
        rest = refs[n_in + nbi + n_out + nbo:]
        scratch, bg_scratch = rest[:n_scratch], rest[n_scratch:]

        def run_background():
            if bg is None:
                return
            step = pl.program_id(0)
            for axis in range(1, len(grid)):
                step = step * grid[axis] + pl.program_id(axis)
            bg.emit(step, nsteps, bg_ins, bg_outs, bg_scratch)

        return ins, outs, scratch, run_background

    if bg is None:
        return [], [], [], [], [], split
    return [ANY] * nbi, list(bg.inputs), [ANY] * nbo, list(bg.out_shapes), list(bg.scratch), split


def _semantics(bg, sem):
    return tuple("arbitrary" for _ in sem) if bg is not None else sem


def _results(res, n_out, bg):
    res = list(res) if isinstance(res, (list, tuple)) else [res]
    own = res[0] if n_out == 1 else res[:n_out]
    return (own, res[n_out:]) if bg is not None else own


def mm_nn(a, b, *, name, out_dtype, tm, tn, tk, residual=None, bg=None):
    M, K = a.shape
    blocked = b.ndim == 3
    if blocked:
        nb, _, Nb = b.shape
        N = nb * Nb
        tn = _tile(Nb, tn)
        per = Nb // tn
    else:
        N = b.shape[1]
        tn = _tile(N, tn)
    tm, tk = _tile(M, tm, 8), _tile(K, tk)
    nj, ni, nk = N // tn, M // tm, K // tk
    has_res = residual is not None
    grid = (nj, ni, nk)
    bg_in_specs, bg_args, bg_out_specs, bg_out_shapes, bg_scratch, split = _carrier(
        bg, 3 if has_res else 2, 1, 0 if nk == 1 else 1, grid)

    def body(*refs):
        ins, (o_ref,), scratch, run_background = split(refs)
        run_background()
        a_ref, b_ref = ins[0], ins[1]
        r_ref = ins[2] if has_res else None

        def finish(acc):
            if has_res:
                acc = acc + r_ref[...]
            o_ref[...] = acc.astype(o_ref.dtype)

        part = jnp.dot(a_ref[...], b_ref[...], preferred_element_type=F32)
        if nk == 1:
            finish(part)
        else:
            acc_ref = scratch[0]
            k = pl.program_id(2)

            @pl.when(k == 0)
            def _():
                acc_ref[...] = part

            @pl.when(k > 0)
            def _():
                acc_ref[...] += part

            @pl.when(k == nk - 1)
            def _():
                finish(acc_ref[...])

    if blocked:
        b_spec = pl.BlockSpec((None, tk, tn), lambda j, i, k: (j // per, k, j % per))
    else:
        b_spec = pl.BlockSpec((tk, tn), lambda j, i, k: (k, j))
    in_specs = [pl.BlockSpec((tm, tk), lambda j, i, k: (i, k)), b_spec]
    args = [a, b]
    if has_res:
        in_specs.append(pl.BlockSpec((tm, tn), lambda j, i, k: (i, j)))
        args.append(residual)
    res = pl.pallas_call(
        body, name=name, grid=grid,
        in_specs=in_specs + bg_in_specs,
        out_specs=[pl.BlockSpec((tm, tn), lambda j, i, k: (i, j))] + bg_out_specs,
        out_shape=[jax.ShapeDtypeStruct((M, N), out_dtype)] + bg_out_shapes,
        scratch_shapes=([] if nk == 1 else [pltpu.VMEM((tm, tn), F32)]) + bg_scratch,
        compiler_params=_params(_semantics(bg, ("parallel", "parallel", "arbitrary"))),
    )(*args, *bg_args)
    return _results(res, 1, bg)


def mm_nt(a, w, *, name, out_dtype, tm, tko, tc, bg=None):
    M, N = a.shape
    blocked = w.ndim == 3
    if blocked:
        nb, Ko, Nb = w.shape
        tc = _tile(Nb, tc)
        per = Nb // tc
    else:
        Ko = w.shape[0]
        tc = _tile(N, tc)
    tm, tko = _tile(M, tm, 8), _tile(Ko, tko)
    njo, ni, nc = Ko // tko, M // tm, N // tc
    grid = (njo, ni, nc)
    bg_in_specs, bg_args, bg_out_specs, bg_out_shapes, bg_scratch, split = _carrier(
        bg, 2, 1, 0 if nc == 1 else 1, grid)

    def body(*refs):
        (a_ref, w_ref), (o_ref,), scratch, run_background = split(refs)
        run_background()
        part = lax.dot_general(a_ref[...], w_ref[...], (((1,), (1,)), ((), ())),
                               preferred_element_type=F32)
        if nc == 1:
            o_ref[...] = part.astype(o_ref.dtype)
        else:
            acc_ref = scratch[0]
            c = pl.program_id(2)

            @pl.when(c == 0)
            def _():
                acc_ref[...] = part

            @pl.when(c > 0)
            def _():
                acc_ref[...] += part

            @pl.when(c == nc - 1)
            def _():
                o_ref[...] = acc_ref[...].astype(o_ref.dtype)

    if blocked:
        w_spec = pl.BlockSpec((None, tko, tc), lambda j, i, c: (c // per, j, c % per))
    else:
        w_spec = pl.BlockSpec((tko, tc), lambda j, i, c: (j, c))
    res = pl.pallas_call(
        body, name=name, grid=grid,
        in_specs=[pl.BlockSpec((tm, tc), lambda j, i, c: (i, c)), w_spec] + bg_in_specs,
        out_specs=[pl.BlockSpec((tm, tko), lambda j, i, c: (i, j))] + bg_out_specs,
        out_shape=[jax.ShapeDtypeStruct((M, Ko), out_dtype)] + bg_out_shapes,
        scratch_shapes=([] if nc == 1 else [pltpu.VMEM((tm, tko), F32)]) + bg_scratch,
        compiler_params=_params(_semantics(bg, ("parallel", "parallel", "arbitrary"))),
    )(a, w, *bg_args)
    return _results(res, 1, bg)


def mm_tn(a, g, *, name, out_dtype, tm, tko, tno, out_blocks=None, bg=None):
    M, K = a.shape
    N = g.shape[1]
    if out_blocks:
        Nb = N // out_blocks
        tno = _tile(Nb, tno)
        per = Nb // tno
    else:
        tno = _tile(N, tno)
    tm, tko = _tile(M, tm), _tile(K, tko)
    njo, njn, nm = K // tko, N // tno, M // tm
    grid = (njo, njn, nm)
    bg_in_specs, bg_args, bg_out_specs, bg_out_shapes, bg_scratch, split = _carrier(
        bg, 2, 1, 0 if nm == 1 else 1, grid)

    def body(*refs):
        (a_ref, g_ref), (o_ref,), scratch, run_background = split(refs)
        run_background()
        part = lax.dot_general(a_ref[...], g_ref[...], (((0,), (0,)), ((), ())),
                               preferred_element_type=F32)
        if nm == 1:
            o_ref[...] = part.astype(o_ref.dtype)
        else:
            acc_ref = scratch[0]
            m = pl.program_id(2)

            @pl.when(m == 0)
            def _():
                acc_ref[...] = part

            @pl.when(m > 0)
            def _():
                acc_ref[...] += part

            @pl.when(m == nm - 1)
            def _():
                o_ref[...] = acc_ref[...].astype(o_ref.dtype)

    if out_blocks:
        o_spec = pl.BlockSpec((None, tko, tno), lambda jo, jn, m: (jn // per, jo, jn % per))
        o_shape = jax.ShapeDtypeStruct((out_blocks, K, Nb), out_dtype)
    else:
        o_spec = pl.BlockSpec((tko, tno), lambda jo, jn, m: (jo, jn))
        o_shape = jax.ShapeDtypeStruct((K, N), out_dtype)
    res = pl.pallas_call(
        body, name=name, grid=grid,
        in_specs=[pl.BlockSpec((tm, tko), lambda jo, jn, m: (m, jo)),
                  pl.BlockSpec((tm, tno), lambda jo, jn, m: (m, jn))] + bg_in_specs,
        out_specs=[o_spec] + bg_out_specs, out_shape=[o_shape] + bg_out_shapes,
        scratch_shapes=([] if nm == 1 else [pltpu.VMEM((tko, tno), F32)]) + bg_scratch,
        compiler_params=_params(_semantics(bg, ("parallel", "parallel", "arbitrary"))),
    )(a, g, *bg_args)
    return _results(res, 1, bg)


def ffn_in_swiglu(h, w_blk, *, name, tm, bg=None):
    M, K = h.shape
    nb, _, Nb = w_blk.shape
    nh = nb // 2
    F = nh * Nb
    tm = _tile(M, tm, 8)
    grid = (nh, M // tm)
    bg_in_specs, bg_args, bg_out_specs, bg_out_shapes, bg_scratch, split = _carrier(bg, 3, 3, 0, grid)

    def body(*refs):
        (h_ref, wg_ref, wu_ref), (act_ref, gate_ref, up_ref), _, run_background = split(refs)
        run_background()
        hv = h_ref[...]
        gate = jnp.dot(hv, wg_ref[...], preferred_element_type=F32)
        up = jnp.dot(hv, wu_ref[...], preferred_element_type=F32)
        gate_ref[...] = gate
        up_ref[...] = up
        act_ref[...] = (gate * jax.nn.sigmoid(gate) * up).astype(act_ref.dtype)

    o_spec = pl.BlockSpec((tm, Nb), lambda j, i: (i, j))
    res = pl.pallas_call(
        body, name=name, grid=grid,
        in_specs=[pl.BlockSpec((tm, K), lambda j, i: (i, 0)),
                  pl.BlockSpec((None, K, Nb), lambda j, i: (j, 0, 0)),
                  pl.BlockSpec((None, K, Nb), lambda j, i: (j + nh, 0, 0))] + bg_in_specs,
        out_specs=[o_spec, o_spec, o_spec] + bg_out_specs,
        out_shape=[jax.ShapeDtypeStruct((M, F), BF16), jax.ShapeDtypeStruct((M, F), F32),
                   jax.ShapeDtypeStruct((M, F), F32)] + bg_out_shapes,
        scratch_shapes=bg_scratch,
        compiler_params=_params(_semantics(bg, ("parallel", "parallel"))),
    )(h, w_blk, w_blk, *bg_args)
    return _results(res, 3, bg)


def _row_spec(tm, d, col=0):
    return pl.BlockSpec((tm, d), lambda i: (i, col))


def _vec_spec(d):
    return pl.BlockSpec((1, d), lambda i: (0, 0))


def norm_fwd(x, g, *, name, tm=256):
    L, D = x.shape
    tm = _tile(L, tm, 8)

    def body(x_ref, g_ref, h_ref):
        xv = x_ref[...]
        h_ref[...] = (xv * _rms(xv) * g_ref[...]).astype(h_ref.dtype)

    return pl.pallas_call(
        body, name=name, grid=(L // tm,),
        in_specs=[_row_spec(tm, D), _vec_spec(D)],
        out_specs=_row_spec(tm, D),
        out_shape=jax.ShapeDtypeStruct((L, D), BF16),
        compiler_params=_params(("parallel",)),
    )(x, g)


def norm_bwd(dh, xin, g, dres, *, name, want_bf16, tm=128):
    L, D = xin.shape
    tm = _tile(L, tm, 8)

    def body(dh_ref, x_ref, g_ref, dres_ref, dx_ref, *rest):
        dg_ref = rest[-1]
        xv, dhv = x_ref[...], dh_ref[...]
        r = _rms(xv)
        dx = dres_ref[...] + _rmsnorm_bwd(dhv, xv, r, g_ref[...])
        dx_ref[...] = dx
        if want_bf16:
            rest[0][...] = dx.astype(BF16)
        part = _rowsum(dhv * xv * r)

        @pl.when(pl.program_id(0) == 0)
        def _():
            dg_ref[...] = part

        @pl.when(pl.program_id(0) > 0)
        def _():
            dg_ref[...] += part

    out_specs = [_row_spec(tm, D)] + ([_row_spec(tm, D)] if want_bf16 else []) + [_vec_spec(D)]
    out_shape = ([jax.ShapeDtypeStruct((L, D), F32)]
                 + ([jax.ShapeDtypeStruct((L, D), BF16)] if want_bf16 else [])
                 + [jax.ShapeDtypeStruct((1, D), F32)])
    return pl.pallas_call(
        body, name=name, grid=(L // tm,),
        in_specs=[_row_spec(tm, D), _row_spec(tm, D), _vec_spec(D), _row_spec(tm, D)],
        out_specs=out_specs, out_shape=out_shape,
        compiler_params=_params(("arbitrary",)),
    )(dh, xin, g, dres)


def glu_pre(y0, *, name, tm=256):
    L, D = y0.shape
    tm = _tile(L, tm, 8)

    def body(y_ref, o_ref):
        o_ref[...] = _gelu(y_ref[...]).astype(o_ref.dtype)

    return pl.pallas_call(
        body, name=name, grid=(L // tm,),
        in_specs=[_row_spec(tm, D)], out_specs=_row_spec(tm, D),
        out_shape=jax.ShapeDtypeStruct((L, D), BF16),
        compiler_params=_params(("parallel",)),
    )(y0)


def glu_post(y0, t, b_glu, g_a, *, name, tm=256):
    L, D = y0.shape
    tm = _tile(L, tm, 8)

    def body(y_ref, t_ref, b_ref, g_ref, o_ref):
        ya = _gelu(y_ref[...]) * jax.nn.sigmoid(t_ref[...] + b_ref[...])
        o_ref[...] = (ya * _rms(ya) * g_ref[...]).astype(o_ref.dtype)

    return pl.pallas_call(
        body, name=name, grid=(L // tm,),
        in_specs=[_row_spec(tm, D), _row_spec(tm, D), _vec_spec(D), _vec_spec(D)],
        out_specs=_row_spec(tm, D),
        out_shape=jax.ShapeDtypeStruct((L, D), BF16),
        compiler_params=_params(("parallel",)),
    )(y0, t, b_glu, g_a)


def glu_post_bwd(y0, t, b_glu, g_a, dycat, *, name, tm=128):
    L, D = y0.shape
    tm = _tile(L, tm, 8)

    def body(y_ref, t_ref, b_ref, g_ref, dn_ref, dt_ref, dd_ref, dga_ref, dbg_ref):
        ya1 = _gelu(y_ref[...])
        sg = jax.nn.sigmoid(t_ref[...] + b_ref[...])
        ya = ya1 * sg
        ra = _rms(ya)
        dn = dn_ref[...]
        dya = _rmsnorm_bwd(dn, ya, ra, g_ref[...])
        dt = dya * ya1 * sg * (1.0 - sg)
        dt_ref[...] = dt.astype(BF16)
        dd_ref[...] = dya * sg
        p_ga, p_bg = _rowsum(dn * ya * ra), _rowsum(dt)

        @pl.when(pl.program_id(0) == 0)
        def _():
            dga_ref[...] = p_ga
            dbg_ref[...] = p_bg

        @pl.when(pl.program_id(0) > 0)
        def _():
            dga_ref[...] += p_ga
            dbg_ref[...] += p_bg

    return pl.pallas_call(
        body, name=name, grid=(L // tm,),
        in_specs=[_row_spec(tm, D), _row_spec(tm, D), _vec_spec(D), _vec_spec(D), _row_spec(tm, D, 0)],
        out_specs=[_row_spec(tm, D), _row_spec(tm, D), _vec_spec(D), _vec_spec(D)],
        out_shape=[jax.ShapeDtypeStruct((L, D), BF16), jax.ShapeDtypeStruct((L, D), F32),
                   jax.ShapeDtypeStruct((1, D), F32), jax.ShapeDtypeStruct((1, D), F32)],
        compiler_params=_params(("arbitrary",)),
    )(y0, t, b_glu, g_a, dycat)


def head_and_loss(x2, gpre, b_g, pp, g_f, tgt, *, name, tm=128):
    L, D = x2.shape
    tm = _tile(L, tm, 8)

    def body(x2_ref, gp_ref, bg_ref, pp_ref, gf_ref, tg_ref,
             dx3_ref, dpre_ref, dpp_ref, loss_ref, dgf_ref, dbg_ref):
        gate = jax.nn.sigmoid(gp_ref[...] + bg_ref[...])
        ppv = pp_ref[...]
        x3 = x2_ref[...] + gate * ppv
        r = _rms(x3)
        xn = x3 * r
        gf = gf_ref[...]
        err = xn * gf - tg_ref[...]
        loss = 0.5 * jnp.sum(jnp.mean(err * err, axis=-1, keepdims=True), axis=0, keepdims=True)
        dout = err * (1.0 / D)
        dx3 = _rmsnorm_bwd(dout, x3, r, gf)
        dx3_ref[...] = dx3
        dpre = dx3 * ppv * gate * (1.0 - gate)
        dpre_ref[...] = dpre.astype(BF16)
        dpp_ref[...] = (dx3 * gate).astype(BF16)
        p_gf, p_bg = _rowsum(dout * xn), _rowsum(dpre)
        p_loss = jnp.broadcast_to(loss, loss_ref.shape)

        @pl.when(pl.program_id(0) == 0)
        def _():
            loss_ref[...] = p_loss
            dgf_ref[...] = p_gf
            dbg_ref[...] = p_bg

        @pl.when(pl.program_id(0) > 0)
        def _():
            loss_ref[...] += p_loss
            dgf_ref[...] += p_gf
            dbg_ref[...] += p_bg

    rs = _row_spec(tm, D)
    return pl.pallas_call(
        body, name=name, grid=(L // tm,),
        in_specs=[rs, rs, _vec_spec(D), rs, _vec_spec(D), rs],
        out_specs=[rs, rs, rs, pl.BlockSpec((8, LANE), lambda i: (0, 0)), _vec_spec(D), _vec_spec(D)],
        out_shape=[jax.ShapeDtypeStruct((L, D), F32), jax.ShapeDtypeStruct((L, D), BF16),
                   jax.ShapeDtypeStruct((L, D), BF16), jax.ShapeDtypeStruct((8, LANE), F32),
                   jax.ShapeDtypeStruct((1, D), F32), jax.ShapeDtypeStruct((1, D), F32)],
        compiler_params=_params(("arbitrary",)),
    )(x2, gpre, b_g, pp, g_f, tgt)


def swiglu_bwd(dact, gate, up, *, name, tm=256, tn=512):
    L, F = dact.shape
    tm, tn = _tile(L, tm, 8), _tile(F, tn)

    def body(da_ref, g_ref, u_ref, dg_ref, du_ref):
        da, gv = da_ref[...], g_ref[...]
        sg = jax.nn.sigmoid(gv)
        dg_ref[...] = (da * u_ref[...] * sg * (1.0 + gv * (1.0 - sg))).astype(BF16)
        du_ref[...] = (da * gv * sg).astype(BF16)

    spec = pl.BlockSpec((tm, tn), lambda i, j: (i, j))
    return pl.pallas_call(
        body, name=name, grid=(L // tm, F // tn),
        in_specs=[spec, spec, spec], out_specs=[spec, spec],
        out_shape=[jax.ShapeDtypeStruct((L, F), BF16), jax.ShapeDtypeStruct((L, F), BF16)],
        compiler_params=_params(("parallel", "parallel")),
    )(dact, gate, up)


def _sgu_forward_values(zu, zv, lng, lnb, w_ref, bs_ref, s_scr, heads, hd):
    u1 = _gelu(zu)
    v1 = _gelu(zv)
    xc = v1 - jnp.mean(v1, axis=-1, keepdims=True)
    r = lax.rsqrt(jnp.mean(xc * xc, axis=-1, keepdims=True) + EPS)
    xhat = xc * r
    v2 = xhat * lng + lnb
    tril = (lax.broadcasted_iota(jnp.int32, (SGU_CHUNK, SGU_CHUNK), 0)
            >= lax.broadcasted_iota(jnp.int32, (SGU_CHUNK, SGU_CHUNK), 1))
    for h in range(heads):
        wm = jnp.where(tril, w_ref[h], 0.0).astype(BF16)
        cols = slice(h * hd, (h + 1) * hd)
        s_scr[:, cols] = jnp.dot(wm, v2[:, cols].astype(BF16), preferred_element_type=F32) + bs_ref[h]
    return u1, xhat, r, v2, tril


def sgu_fwd(z, ln_g, ln_b, w_s, b_s, g_b, *, name, d_sgu):
    L = z.shape[0]
    heads = w_s.shape[0]
    hd = d_sgu // heads

    def body(zu_ref, zv_ref, lng_ref, lnb_ref, w_ref, bs_ref, gb_ref, o_ref, s_scr):
        u1, _, _, _, _ = _sgu_forward_values(zu_ref[...], zv_ref[...], lng_ref[...], lnb_ref[...],
                                             w_ref, bs_ref, s_scr, heads, hd)
        yb = u1 * s_scr[...]
        o_ref[...] = (yb * _rms(yb) * gb_ref[...]).astype(o_ref.dtype)

    blk = lambda col: pl.BlockSpec((SGU_CHUNK, d_sgu), lambda n: (n, col))
    return pl.pallas_call(
        body, name=name, grid=(L // SGU_CHUNK,),
        in_specs=[blk(1), blk(2), _vec_spec(d_sgu), _vec_spec(d_sgu),
                  pl.BlockSpec(w_s.shape, lambda n: (0, 0, 0)), pl.BlockSpec(b_s.shape, lambda n: (0, 0, 0)),
                  _vec_spec(d_sgu)],
        out_specs=blk(0),
        out_shape=jax.ShapeDtypeStruct((L, d_sgu), BF16),
        scratch_shapes=[pltpu.VMEM((SGU_CHUNK, d_sgu), F32)],
        compiler_params=_params(("parallel",)),
    )(z, z, ln_g, ln_b, w_s, b_s, g_b)


def sgu_bwd(z, dycat, ln_g, ln_b, w_s, b_s, g_b, *, name, d_sgu):
    L = z.shape[0]
    heads = w_s.shape[0]
    hd = d_sgu // heads

    def body(zu_ref, zv_ref, dn_ref, lng_ref, lnb_ref, w_ref, bs_ref, gb_ref,
             dzu_ref, dzv_ref, dw_ref, dbs_ref, dlng_ref, dlnb_ref, dgb_ref, s_scr, dv_scr):
        first = pl.program_id(0) == 0
        zu, zv, lng = zu_ref[...], zv_ref[...], lng_ref[...]
        u1, xhat, r, v2, tril = _sgu_forward_values(zu, zv, lng, lnb_ref[...], w_ref, bs_ref, s_scr, heads, hd)
        s = s_scr[...]
        yb = u1 * s
        rb = _rms(yb)
        dn = dn_ref[...]
        dyb = _rmsnorm_bwd(dn, yb, rb, gb_ref[...])
        dzu_ref[...] = (dyb * s * _gelu_grad(zu)).astype(BF16)
        ds = dyb * u1
        for h in range(heads):
            cols = slice(h * hd, (h + 1) * hd)
            ds_h = ds[:, cols]
            ds_hb = ds_h.astype(BF16)
            wm = jnp.where(tril, w_ref[h], 0.0).astype(BF16)
            dw_h = jnp.where(tril, lax.dot_general(ds_hb, v2[:, cols].astype(BF16), (((1,), (1,)), ((), ())),
                                                   preferred_element_type=F32), 0.0)
            db_h = jnp.sum(ds_h, axis=1, keepdims=True)
            dv_scr[:, cols] = lax.dot_general(wm, ds_hb, (((0,), (0,)), ((), ())), preferred_element_type=F32)

            @pl.when(first)
            def _():
                dw_ref[h] = dw_h
                dbs_ref[h] = db_h

            @pl.when(jnp.logical_not(first))
            def _():
                dw_ref[h] += dw_h
                dbs_ref[h] += db_h

        dv2 = dv_scr[...]
        dxh = dv2 * lng
        dv1 = r * (dxh - jnp.mean(dxh, axis=-1, keepdims=True)
                   - xhat * jnp.mean(dxh * xhat, axis=-1, keepdims=True))
        dzv_ref[...] = (dv1 * _gelu_grad(zv)).astype(BF16)
        p_lng, p_lnb, p_gb = _rowsum(dv2 * xhat), _rowsum(dv2), _rowsum(dn * yb * rb)

        @pl.when(first)
        def _():
            dlng_ref[...] = p_lng
            dlnb_ref[...] = p_lnb
            dgb_ref[...] = p_gb

        @pl.when(jnp.logical_not(first))
        def _():
            dlng_ref[...] += p_lng
            dlnb_ref[...] += p_lnb
            dgb_ref[...] += p_gb

    blk = lambda col: pl.BlockSpec((SGU_CHUNK, d_sgu), lambda n: (n, col))
    full3 = lambda shape: pl.BlockSpec(shape, lambda n: (0, 0, 0))
    return pl.pallas_call(
        body, name=name, grid=(L // SGU_CHUNK,),
        in_specs=[blk(1), blk(2), blk(1), _vec_spec(d_sgu), _vec_spec(d_sgu),
                  full3(w_s.shape), full3(b_s.shape), _vec_spec(d_sgu)],
        out_specs=[blk(0), blk(0), full3(w_s.shape), full3(b_s.shape),
                   _vec_spec(d_sgu), _vec_spec(d_sgu), _vec_spec(d_sgu)],
        out_shape=[jax.ShapeDtypeStruct((L, d_sgu), BF16), jax.ShapeDtypeStruct((L, d_sgu), BF16),
                   jax.ShapeDtypeStruct(w_s.shape, F32), jax.ShapeDtypeStruct(b_s.shape, F32),
                   jax.ShapeDtypeStruct((1, d_sgu), F32), jax.ShapeDtypeStruct((1, d_sgu), F32),
                   jax.ShapeDtypeStruct((1, d_sgu), F32)],
        scratch_shapes=[pltpu.VMEM((SGU_CHUNK, d_sgu), F32), pltpu.VMEM((SGU_CHUNK, d_sgu), F32)],
        compiler_params=_params(("arbitrary",)),
    )(z, z, dycat, ln_g, ln_b, w_s, b_s, g_b)


def _disc_lambda(lam_re, lam_im, log_step):
    lr = jnp.minimum(lam_re, LAMBDA_RE_MAX)
    li = lam_im
    dt = jnp.exp(log_step)
    mag = jnp.exp(lr * dt)
    ang = li * dt
    a_re = mag * jnp.cos(ang)
    a_im = mag * jnp.sin(ang)
    nr = a_re - 1.0
    ni = a_im
    den = lr * lr + li * li
    return a_re, a_im, (nr * lr + ni * li) / den, (ni * lr - nr * li) / den


def _disc_b(q_re, q_im, b_re, b_im):
    return q_re * b_re - q_im * b_im, q_re * b_im + q_im * b_re


def disc_lambda_fwd(lam_re, lam_im, log_step, *, name):
    def body(lr_ref, li_ref, ls_ref, ar_ref, ai_ref, qr_ref, qi_ref):
        ar_ref[...], ai_ref[...], qr_ref[...], qi_ref[...] = _disc_lambda(lr_ref[...], li_ref[...], ls_ref[...])

    sd = jax.ShapeDtypeStruct(lam_re.shape, F32)
    return pl.pallas_call(body, name=name, out_shape=[sd, sd, sd, sd], compiler_params=_params())(
        lam_re, lam_im, log_step)


def disc_lambda_bwd(lam_re, lam_im, log_step, cts, *, name):
    def body(lr_ref, li_ref, ls_ref, c0, c1, c2, c3, dlr_ref, dli_ref, dls_ref):
        _, vjp = jax.vjp(_disc_lambda, lr_ref[...], li_ref[...], ls_ref[...])
        dlr_ref[...], dli_ref[...], dls_ref[...] = vjp((c0[...], c1[...], c2[...], c3[...]))

    sd = jax.ShapeDtypeStruct(lam_re.shape, F32)
    return pl.pallas_call(body, name=name, out_shape=[sd, sd, jax.ShapeDtypeStruct(log_step.shape, F32)],
                          compiler_params=_params())(lam_re, lam_im, log_step, *cts)


def disc_b_fwd(q_re, q_im, b_re, b_im, *, name):
    def body(qr_ref, qi_ref, br_ref, bi_ref, or_ref, oi_ref):
        or_ref[...], oi_ref[...] = _disc_b(qr_ref[...], qi_ref[...], br_ref[...], bi_ref[...])

    sd = jax.ShapeDtypeStruct(b_re.shape, F32)
    return pl.pallas_call(body, name=name, out_shape=[sd, sd], compiler_params=_params())(q_re, q_im, b_re, b_im)


def disc_b_bwd(q_re, q_im, b_re, b_im, ct_re, ct_im, *, name):
    def body(qr_ref, qi_ref, br_ref, bi_ref, cr_ref, ci_ref, dqr_ref, dqi_ref, dbr_ref, dbi_ref):
        _, vjp = jax.vjp(_disc_b, qr_ref[...], qi_ref[...], br_ref[...], bi_ref[...])
        dqr_ref[...], dqi_ref[...], dbr_ref[...], dbi_ref[...] = vjp((cr_ref[...], ci_ref[...]))

    sq, sb = jax.ShapeDtypeStruct(q_re.shape, F32), jax.ShapeDtypeStruct(b_re.shape, F32)
    return pl.pallas_call(body, name=name, out_shape=[sq, sq, sb, sb], compiler_params=_params())(
        q_re, q_im, b_re, b_im, ct_re, ct_im)


def _lti_scan(xr, xi, ar, ai, reverse):
    T = xr.shape[0]
    row = lax.broadcasted_iota(jnp.int32, xr.shape, 0)
    k = 1
    while k < T:
        shift = T - k if reverse else k
        keep = (row < T - k) if reverse else (row >= k)
        sr = jnp.where(keep, pltpu.roll(xr, shift, 0), 0.0)
        si = jnp.where(keep, pltpu.roll(xi, shift, 0), 0.0)
        xr, xi = xr + ar * sr - ai * si, xi + ar * si + ai * sr
        ar, ai = ar * ar - ai * ai, 2.0 * ar * ai
        k *= 2
    return xr, xi


def _ssm_chunk(L):
    return _tile(L, 256, 8)


def ssm_fwd(z, bs_re, bs_im, cs_re, cs_im, a_re, a_im, d, *, name, bg=None):
    L = z.shape[0]
    NK, C, S = bs_re.shape
    T = _ssm_chunk(L)
    grid = (NK, L // T)
    bg_in_specs, bg_args, bg_out_specs, bg_out_shapes, bg_scratch, split = _carrier(bg, 8, 3, 4, grid)

    def body(*refs):
        ((u_ref, br_ref, bi_ref, cr_ref, ci_ref, ar_ref, ai_ref, d_ref), (y_ref, sr_ref, si_ref),
         (car_re, car_im, pw_re, pw_im), run_background) = split(refs)
        run_background()
        i = pl.program_id(1)
        ar, ai = ar_ref[...], ai_ref[...]

        @pl.when(i == 0)
        def _():
            row = lax.broadcasted_iota(jnp.int32, (T, S), 0)
            pr, pi = _lti_scan(jnp.where(row == 0, ar, 0.0), jnp.where(row == 0, ai, 0.0), ar, ai, False)
            pw_re[...] = pr
            pw_im[...] = pi
            car_re[...] = jnp.zeros_like(car_re)
            car_im[...] = jnp.zeros_like(car_im)

        u = u_ref[...]
        ub = u.astype(BF16)
        xr = jnp.dot(ub, br_ref[...], preferred_element_type=F32)
        xi = jnp.dot(ub, bi_ref[...], preferred_element_type=F32)
        xr, xi = _lti_scan(xr, xi, ar, ai, False)
        cr, ci = car_re[...], car_im[...]
        pr, pi = pw_re[...], pw_im[...]
        s_re = xr + pr * cr - pi * ci
        s_im = xi + pr * ci + pi * cr
        sr_ref[...] = s_re
        si_ref[...] = s_im
        car_re[...] = s_re[T - 1:T, :]
        car_im[...] = s_im[T - 1:T, :]
        y_ref[...] = (jnp.dot(s_re.astype(BF16), cr_ref[...], preferred_element_type=F32)
                      - jnp.dot(s_im.astype(BF16), ci_ref[...], preferred_element_type=F32)
                      + d_ref[...] * u)

    kspec = lambda shape: pl.BlockSpec((None,) + shape, lambda k, i: (k, 0, 0))
    res = pl.pallas_call(
        body, name=name, grid=grid,
        in_specs=[pl.BlockSpec((T, C), lambda k, i: (i, k)),
                  kspec((C, S)), kspec((C, S)), kspec((S, C)), kspec((S, C)),
                  kspec((1, S)), kspec((1, S)), kspec((1, C))] + bg_in_specs,
        out_specs=[pl.BlockSpec((T, C), lambda k, i: (i, k)),
                   pl.BlockSpec((T, S), lambda k, i: (i, k)), pl.BlockSpec((T, S), lambda k, i: (i, k))] + bg_out_specs,
        out_shape=[jax.ShapeDtypeStruct((L, NK * C), F32), jax.ShapeDtypeStruct((L, NK * S), F32),
                   jax.ShapeDtypeStruct((L, NK * S), F32)] + bg_out_shapes,
        scratch_shapes=[pltpu.VMEM((1, S), F32), pltpu.VMEM((1, S), F32),
                        pltpu.VMEM((T, S), F32), pltpu.VMEM((T, S), F32)] + bg_scratch,
        compiler_params=_params(_semantics(bg, ("parallel", "arbitrary"))),
    )(z, bs_re, bs_im, cs_re, cs_im, a_re, a_im, d, *bg_args)
    return _results(res, 3, bg)


def ssm_bwd(z, y0, dd_direct, dd_mm, s_re, s_im, bst_re, bst_im, cst_re, cst_im, a_re, a_im, d, *, name):
    L = z.shape[0]
    NK, S, C = bst_re.shape
    T = _ssm_chunk(L)
    nchunk = L // T
    tail = T // 8

    def body(u_ref, y_ref, d1_ref, d2_ref, sr_ref, si_ref, pr_ref, pi_ref,
             btr_ref, bti_ref, ctr_ref, cti_ref, ar_ref, ai_ref, d_ref,
             du_ref, dbr_ref, dbi_ref, dcr_ref, dci_ref, dar_ref, dai_ref, dd_ref,
             car_re, car_im, pw_re, pw_im):
        i = pl.program_id(1)
        chunk = nchunk - 1 - i
        ar, ai = ar_ref[...], ai_ref[...]
        nai = -ai
        row = lax.broadcasted_iota(jnp.int32, (T, S), 0)

        @pl.when(i == 0)
        def _():
            qr, qi = _lti_scan(jnp.where(row == T - 1, ar, 0.0), jnp.where(row == T - 1, nai, 0.0), ar, nai, True)
            pw_re[...] = qr
            pw_im[...] = qi
            car_re[...] = jnp.zeros_like(car_re)
            car_im[...] = jnp.zeros_like(car_im)

        u = u_ref[...]
        dy = (d1_ref[...] + d2_ref[...]) * _gelu_grad(y_ref[...])
        dyb = dy.astype(BF16)
        gr = jnp.dot(dyb, ctr_ref[...], preferred_element_type=F32)
        gi = -jnp.dot(dyb, cti_ref[...], preferred_element_type=F32)
        lr, li = _lti_scan(gr, gi, ar, nai, True)
        cr, ci = car_re[...], car_im[...]
        qr, qi = pw_re[...], pw_im[...]
        lr = lr + qr * cr - qi * ci
        li = li + qr * ci + qi * cr
        car_re[...] = lr[0:1, :]
        car_im[...] = li[0:1, :]

        s_re, s_im = sr_ref[...], si_ref[...]
        has_prev = (chunk > 0).astype(F32)
        prev_re = pr_ref[7:8, :] * has_prev
        prev_im = pi_ref[7:8, :] * has_prev
        sp_re = jnp.where(row == 0, prev_re, pltpu.roll(s_re, 1, 0))
        sp_im = jnp.where(row == 0, prev_im, pltpu.roll(s_im, 1, 0))
        p_ar = _rowsum(lr * sp_re + li * sp_im)
        p_ai = _rowsum(li * sp_re - lr * sp_im)

        lrb, lib, ub = lr.astype(BF16), li.astype(BF16), u.astype(BF16)
        du = (dy * d_ref[...] + jnp.dot(lrb, btr_ref[...], preferred_element_type=F32)
              + jnp.dot(lib, bti_ref[...], preferred_element_type=F32))
        du_ref[...] = du.astype(BF16)
        tdot = lambda p, q: lax.dot_general(p, q, (((0,), (0,)), ((), ())), preferred_element_type=F32)
        p_br, p_bi = tdot(ub, lrb), tdot(ub, lib)
        p_cr, p_ci = tdot(s_re.astype(BF16), dyb), -tdot(s_im.astype(BF16), dyb)
        p_dd = _rowsum(dy * u)

        @pl.when(i == 0)
        def _():
            dar_ref[...] = p_ar
            dai_ref[...] = p_ai
            dbr_ref[...] = p_br
            dbi_ref[...] = p_bi
            dcr_ref[...] = p_cr
            dci_ref[...] = p_ci
            dd_ref[...] = p_dd

        @pl.when(i > 0)
        def _():
            dar_ref[...] += p_ar
            dai_ref[...] += p_ai
            dbr_ref[...] += p_br
            dbi_ref[...] += p_bi
            dcr_ref[...] += p_cr
            dci_ref[...] += p_ci
            dd_ref[...] += p_dd

    rev = lambda k, i: (nchunk - 1 - i, k)
    prev = lambda k, i: (jnp.maximum((nchunk - 1 - i) * tail - 1, 0), k)
    kspec = lambda shape: pl.BlockSpec((None,) + shape, lambda k, i: (k, 0, 0))
    return pl.pallas_call(
        body, name=name, grid=(NK, nchunk),
        in_specs=[pl.BlockSpec((T, C), rev), pl.BlockSpec((T, C), rev), pl.BlockSpec((T, C), rev),
                  pl.BlockSpec((T, C), rev), pl.BlockSpec((T, S), rev), pl.BlockSpec((T, S), rev),
                  pl.BlockSpec((8, S), prev), pl.BlockSpec((8, S), prev),
                  kspec((S, C)), kspec((S, C)), kspec((C, S)), kspec((C, S)),
                  kspec((1, S)), kspec((1, S)), kspec((1, C))],
        out_specs=[pl.BlockSpec((T, C), rev), kspec((C, S)), kspec((C, S)), kspec((S, C)), kspec((S, C)),
                   kspec((1, S)), kspec((1, S)), kspec((1, C))],
        out_shape=[jax.ShapeDtypeStruct((L, NK * C), BF16),
                   jax.ShapeDtypeStruct((NK, C, S), F32), jax.ShapeDtypeStruct((NK, C, S), F32),
                   jax.ShapeDtypeStruct((NK, S, C), F32), jax.ShapeDtypeStruct((NK, S, C), F32),
                   jax.ShapeDtypeStruct((NK, 1, S), F32), jax.ShapeDtypeStruct((NK, 1, S), F32),
                   jax.ShapeDtypeStruct((NK, 1, C), F32)],
        scratch_shapes=[pltpu.VMEM((1, S), F32), pltpu.VMEM((1, S), F32),
                        pltpu.VMEM((T, S), F32), pltpu.VMEM((T, S), F32)],
        compiler_params=_params(("parallel", "arbitrary")),
    )(z, y0, dd_direct, dd_mm, s_re, s_im, s_re, s_im, bst_re, bst_im, cst_re, cst_im, a_re, a_im, d)


def _block_diag(v):
    NK, SG, R, Q = v.shape
    eye = jnp.eye(SG, dtype=v.dtype)
    return (v[:, :, :, None, :] * eye[None, :, None, :, None]).reshape(NK, SG * R, SG * Q)


def _block_diag_part(m, SG):
    NK, RR, QQ = m.shape
    R, Q = RR // SG, QQ // SG
    eye = jnp.eye(SG, dtype=m.dtype)
    return jnp.sum(m.reshape(NK, SG, R, SG, Q) * eye[None, :, None, :, None], axis=3)


def _position():
    return lax.axis_index("x"), lax.axis_index("y"), lax.axis_index("c")


def _other_chips(x, y):
    return [(1 - x, y), (x, 1 - y), (1 - x, 1 - y)]


def _gather_phases(n):
    def parts(ins, outs, sems):
        send_sems, recv_sems, local_sems = sems
        x, y, c = _position()
        me, sibling = (x, y, c), (x, y, 1 - c)
        chips = _other_chips(x, y)

        def block(a, pos):
            return outs[a].at[4 * pos[0] + 2 * pos[1] + pos[2]]

        def copy(a, k, pos, to, src=None):
            return pltpu.make_async_remote_copy(
                src_ref=block(a, pos) if src is None else src, dst_ref=block(a, pos),
                send_sem=send_sems.at[7 * a + k], recv_sem=recv_sems.at[7 * a + k],
                device_id=to, device_id_type=MESH)

        mine = [pltpu.make_async_copy(ins[a], block(a, me), local_sems.at[a]) for a in range(n)]
        first = []
        for a in range(n):
            first.append(copy(a, 0, me, sibling, src=ins[a]))
            first += [copy(a, 1 + j, me, (*chip, c), src=ins[a]) for j, chip in enumerate(chips)]
        passed = [copy(a, 4 + j, (*chip, c), sibling) for a in range(n) for j, chip in enumerate(chips)]
        arrived = [copy(a, 1 + j, (*chip, c), me) for a in range(n) for j, chip in enumerate(chips)]
        from_sibling = []
        for a in range(n):
            from_sibling.append(copy(a, 0, sibling, me))
            from_sibling += [copy(a, 4 + j, (*chip, 1 - c), me) for j, chip in enumerate(chips)]
        return mine, first, passed, arrived, from_sibling

    def send(ins, outs, sems):
        mine, first, _, _, _ = parts(ins, outs, sems)
        for cp in mine + first:
            cp.start()

    def forward(ins, outs, sems):
        _, _, passed, arrived, _ = parts(ins, outs, sems)
        for got, fwd in zip(arrived, passed):
            got.wait_recv()
            fwd.start()

    def finish(ins, outs, sems):
        mine, first, passed, _, from_sibling = parts(ins, outs, sems)
        for cp in from_sibling:
            cp.wait_recv()
        for cp in first + passed:
            cp.wait_send()
        for cp in mine:
            cp.wait()

    return [(0.0, send), (GATHER_FORWARD_AT, forward), (1.0, finish)]


def _gather_shapes(shards):
    n = len(shards)
    return ([jax.ShapeDtypeStruct((N_DEV,) + s.shape, s.dtype) for s in shards],
            [pltpu.SemaphoreType.DMA((7 * n,)), pltpu.SemaphoreType.DMA((7 * n,)), pltpu.SemaphoreType.DMA((n,))])


def gather_background(shards):
    out_shapes, scratch = _gather_shapes(shards)
    return Background(shards, out_shapes, scratch, _gather_phases(len(shards)))


def all_gather_blocks(shards, *, name):
    n = len(shards)
    out_shapes, scratch = _gather_shapes(shards)

    def body(*refs):
        for _, phase in _gather_phases(n):
            phase(refs[:n], refs[n:2 * n], refs[2 * n:])

    return pl.pallas_call(
        body, name=name, in_specs=[ANY] * n, out_specs=[ANY] * n, out_shape=out_shapes, scratch_shapes=scratch,
    )(*shards)


def sibling_exchange(grads, *, name):
    n = len(grads)

    def body(*refs):
        ins, outs = refs[:n], refs[n:2 * n]
        send_sems, recv_sems = refs[2 * n:]
        x, y, c = _position()
        copies = []
        for a in range(n):
            for q in range(4):
                copies.append(pltpu.make_async_remote_copy(
                    src_ref=ins[a].at[2 * q + 1 - c], dst_ref=outs[a].at[q],
                    send_sem=send_sems.at[4 * a + q], recv_sem=recv_sems.at[4 * a + q],
                    device_id=(x, y, 1 - c), device_id_type=MESH))
        for cp in copies:
            cp.start()
        for cp in copies:
            cp.wait()

    return pl.pallas_call(
        body, name=name,
        in_specs=[ANY] * n, out_specs=[ANY] * n,
        out_shape=[jax.ShapeDtypeStruct((4,) + g.shape[1:], g.dtype) for g in grads],
        scratch_shapes=[pltpu.SemaphoreType.DMA((4 * n,)), pltpu.SemaphoreType.DMA((4 * n,))],
    )(*grads)


def _chip_exchange_phases(n):
    def copies(ins, outs, sems):
        x, y, c = _position()
        return [pltpu.make_async_remote_copy(
            src_ref=ins[a].at[2 * chip[0] + chip[1]], dst_ref=outs[a].at[j],
            send_sem=sems[0].at[3 * a + j], recv_sem=sems[1].at[3 * a + j],
            device_id=(*chip, c), device_id_type=MESH)
            for a in range(n) for j, chip in enumerate(_other_chips(x, y))]

    def send(ins, outs, sems):
        for cp in copies(ins, outs, sems):
            cp.start()

    def finish(ins, outs, sems):
        for cp in copies(ins, outs, sems):
            cp.wait()

    return [(0.0, send), (1.0, finish)]


def chip_exchange_background(parts):
    n = len(parts)
    return Background(parts, [jax.ShapeDtypeStruct((3,) + p.shape[1:], p.dtype) for p in parts],
                      [pltpu.SemaphoreType.DMA((3 * n,)), pltpu.SemaphoreType.DMA((3 * n,))],
                      _chip_exchange_phases(n))


def add_pairs(grads, theirs, core, *, name, tm=512):
    _, R, C = theirs.shape
    tm = _tile(R, tm, 16)

    def body(core_ref, a_ref, b_ref, o_ref):
        o_ref[...] = (a_ref[...].astype(F32) + b_ref[...].astype(F32)).astype(o_ref.dtype)

    spec = pl.BlockSpec((None, tm, C), lambda q, i, core_ref: (q, i, 0))
    return pl.pallas_call(
        body, name=name,
        grid_spec=pltpu.PrefetchScalarGridSpec(
            num_scalar_prefetch=1, grid=(4, R // tm),
            in_specs=[pl.BlockSpec((None, tm, C), lambda q, i, core_ref: (2 * q + core_ref[0], i, 0)), spec],
            out_specs=spec),
        out_shape=jax.ShapeDtypeStruct(theirs.shape, BF16),
        compiler_params=_params(("parallel", "parallel")),
    )(core, grads, theirs)


def _adamw(w, g, m, v):
    m = ADAM_B1 * m + (1.0 - ADAM_B1) * g
    v = ADAM_B2 * v + (1.0 - ADAM_B2) * (g * g)
    m_hat = m / (1.0 - ADAM_B1 ** ADAM_STEP)
    v_hat = v / (1.0 - ADAM_B2 ** ADAM_STEP)
    delta = -ADAM_LR * (m_hat / (jnp.sqrt(v_hat) + ADAM_EPS) + ADAM_WD * w)
    return delta, m, v


def adamw_sharded(w, m, v, own_a, own_b, others, *, name, tm=256):
    R, C = w.shape
    tm = _tile(R, tm, 16)

    def body(w_ref, m_ref, v_ref, a_ref, b_ref, o_ref, g_ref, d_ref, nm_ref, nv_ref):
        g = a_ref[...].astype(F32) + b_ref[...].astype(F32)
        for j in range(3):
            g = g + o_ref[j].astype(F32)
        g_ref[...] = g
        d_ref[...], nm_ref[...], nv_ref[...] = _adamw(w_ref[...], g, m_ref[...], v_ref[...])

    spec = pl.BlockSpec((tm, C), lambda i: (i, 0))
    sd = jax.ShapeDtypeStruct((R, C), F32)
    return pl.pallas_call(
        body, name=name, grid=(R // tm,),
        in_specs=[spec, spec, spec, spec, spec, pl.BlockSpec((3, tm, C), lambda i: (0, i, 0))],
        out_specs=[spec, spec, spec, spec], out_shape=[sd, sd, sd, sd],
        compiler_params=_params(("parallel",)),
    )(w, m, v, own_a, own_b, others)


def adamw_packed(w, m, v, gathered, *, name, tm=512):
    R, C = w.shape
    tm = _tile(R, tm, 8)

    def body(w_ref, m_ref, v_ref, ga_ref, g_ref, d_ref, nm_ref, nv_ref):
        g = ga_ref[0]
        for dev in range(1, N_DEV):
            g = g + ga_ref[dev]
        g_ref[...] = g
        d_ref[...], nm_ref[...], nv_ref[...] = _adamw(w_ref[...], g, m_ref[...], v_ref[...])

    spec = pl.BlockSpec((tm, C), lambda i: (i, 0))
    sd = jax.ShapeDtypeStruct((R, C), F32)
    return pl.pallas_call(
        body, name=name, grid=(R // tm,),
        in_specs=[spec, spec, spec, pl.BlockSpec((N_DEV, tm, C), lambda i: (0, i, 0))],
        out_specs=[spec, spec, spec, spec], out_shape=[sd, sd, sd, sd],
        compiler_params=_params(("parallel",)),
    )(w, m, v, gathered)


def _pack(arrays, row_multiple=8):
    parts, total = [], 0
    for a in arrays:
        flat = a.reshape(-1).astype(F32)
        parts.append(jnp.pad(flat, (0, (-flat.size) % PACK_ALIGN)))
        total += parts[-1].size
    tail = (-total) % (row_multiple * LANE)
    if tail:
        parts.append(jnp.zeros((tail,), F32))
    return jnp.concatenate(parts).reshape(-1, LANE)


def _unpack(packed, shapes):
    flat = packed.reshape(-1)
    out, off = [], 0
    for shape in shapes:
        size = math.prod(shape)
        out.append(flat[off:off + size].reshape(shape))
        off += size + (-size) % PACK_ALIGN
    return out


SHARDED = ("w_in", "ssm_glu_w", "w_out", "w_ffn_in", "w_ffn_out", "w_ple_gate", "w_ple_proj")
SMALL_LAST = ("norm_mix_g",)
SMALL = ("ssm_lambda_re", "ssm_lambda_im", "ssm_log_step", "ssm_b_re", "ssm_b_im", "ssm_c_re",
         "ssm_c_im", "ssm_d", "ssm_glu_b", "sgu_ln_g", "sgu_ln_b", "sgu_w", "sgu_b", "out_norm_ssm_g",
         "out_norm_sgu_g", "norm_ffn_g", "norm_ple_g", "b_ple_gate", "final_norm_g")
WEIGHTS = ("norm_mix_g", "w_in", "ssm_lambda_re", "ssm_lambda_im", "ssm_log_step", "ssm_b_re", "ssm_b_im",
           "ssm_c_re", "ssm_c_im", "ssm_d", "ssm_glu_w", "ssm_glu_b", "sgu_ln_g", "sgu_ln_b", "sgu_w", "sgu_b",
           "out_norm_ssm_g", "out_norm_sgu_g", "w_out", "norm_ffn_g", "w_ffn_in", "w_ffn_out", "norm_ple_g",
           "w_ple_gate", "b_ple_gate", "w_ple_proj", "final_norm_g")


def _step(x, p, loss_target, w, m, v):
    L, D = x.shape[1], x.shape[2]
    x2d, p2d, tgt = x.reshape(L, D), p.reshape(L, -1), loss_target.reshape(L, D)
    d_ssm = w["ssm_glu_w"].shape[2]
    d_sgu = w["sgu_ln_g"].shape[1]
    G, P, H = w["ssm_b_re"].shape[1:]
    SG = min(SSM_SUPER, G)
    NK = G // SG
    row = lambda a: a.reshape(1, -1)

    shard2d = {n: w[n].reshape(w[n].shape[1:]) for n in SHARDED}
    shard_bf = {n: shard2d[n].astype(BF16) for n in SHARDED}
    (w_ple_blk,) = all_gather_blocks([shard_bf["w_ple_proj"]], name="gather_w_ple")
    bf = lambda t: t.astype(BF16)
    pp, (w_in_blk,) = mm_nn(bf(p2d), w_ple_blk, name="ple_proj", out_dtype=F32, tm=512, tn=512, tk=2048,
                            bg=gather_background([shard_bf["w_in"]]))
    w_in = jnp.transpose(w_in_blk, (1, 0, 2)).reshape(D, -1)
    F = shard2d["w_ffn_in"].shape[1] * 4

    lam_re, lam_im, log_step = w["ssm_lambda_re"][0], w["ssm_lambda_im"][0], w["ssm_log_step"][0].reshape(G, 1)
    a_re, a_im, q_re, q_im = disc_lambda_fwd(lam_re, lam_im, log_step, name="s5_discretise_lambda")
    bt_re = w["ssm_b_re"][0].transpose(2, 0, 1).reshape(H, G * P)
    bt_im = w["ssm_b_im"][0].transpose(2, 0, 1).reshape(H, G * P)
    bbar_re, bbar_im = disc_b_fwd(row(q_re), row(q_im), bt_re, bt_im, name="s5_discretise_b")
    to_bs = lambda t: _block_diag(t.reshape(H, NK, SG, P).transpose(1, 2, 0, 3))
    to_cs = lambda t: _block_diag(t.reshape(NK, SG, H, P).transpose(0, 1, 3, 2))
    bs_re, bs_im = to_bs(bbar_re), to_bs(bbar_im)
    cs_re, cs_im = to_cs(w["ssm_c_re"][0]), to_cs(w["ssm_c_im"][0])
    a_re_k, a_im_k = a_re.reshape(NK, 1, SG * P), a_im.reshape(NK, 1, SG * P)
    d_k = w["ssm_d"][0].reshape(NK, 1, SG * H)
    bf = lambda t: t.astype(BF16)
    tr = lambda t: jnp.swapaxes(t, 1, 2)

    h1 = norm_fwd(x2d, w["norm_mix_g"], name="norm_mix")
    z, (w_glu, w_out) = mm_nn(h1, w_in, name="in_proj", out_dtype=F32, tm=512, tn=1024, tk=2048,
                              bg=gather_background([shard_bf["ssm_glu_w"], shard_bf["w_out"]]))
    w_glu, w_out = w_glu.reshape(d_ssm, d_ssm), w_out.reshape(D, D)
    (y0, s_re, s_im), (w_ffn_in_blk,) = ssm_fwd(
        z, bf(bs_re), bf(bs_im), bf(cs_re), bf(cs_im), a_re_k, a_im_k, d_k, name="s5_scan",
        bg=gather_background([shard_bf["w_ffn_in"]]))
    ya1 = glu_pre(y0, name="s5_gelu")
    t_glu = mm_nn(ya1, w_glu, name="s5_glu_proj", out_dtype=F32, tm=512, tn=512, tk=2048)
    n_a = glu_post(y0, t_glu, w["ssm_glu_b"], w["out_norm_ssm_g"], name="s5_glu_norm")
    b_s3 = w["sgu_b"][0][:, :, None]
    n_b = sgu_fwd(z, w["sgu_ln_g"], w["sgu_ln_b"], w["sgu_w"][0], b_s3, w["out_norm_sgu_g"], name="sgu", d_sgu=d_sgu)
    ycat = jnp.concatenate([n_a, n_b], axis=1)
    x1 = mm_nn(ycat, w_out, name="out_proj", out_dtype=F32, tm=512, tn=512, tk=2048, residual=x2d)
    h2 = norm_fwd(x1, w["norm_ffn_g"], name="norm_ffn")
    (act, gate_ff, up_ff), (w_ffn_out, w_gate) = ffn_in_swiglu(
        h2, w_ffn_in_blk, name="ffn_in_swiglu", tm=256,
        bg=gather_background([shard_bf["w_ffn_out"], shard_bf["w_ple_gate"]]))
    w_ffn_out, w_gate = w_ffn_out.reshape(F, D), w_gate.reshape(D, D)
    x2 = mm_nn(act, w_ffn_out, name="ffn_out", out_dtype=F32, tm=512, tn=512, tk=F, residual=x1)
    h3 = norm_fwd(x2, w["norm_ple_g"], name="norm_ple")
    gpre = mm_nn(h3, w_gate, name="ple_gate", out_dtype=F32, tm=512, tn=1024, tk=2048)

    dx3, dpre, dpp, loss_part, d_final_g, d_b_gate = head_and_loss(
        x2, gpre, w["b_ple_gate"], pp, row(w["final_norm_g"]), tgt, name="head_and_loss")
    x_pos, y_pos, c_pos = _position()
    me, q_me = 4 * x_pos + 2 * y_pos + c_pos, 2 * x_pos + y_pos
    core = jnp.reshape(c_pos, (1,)).astype(jnp.int32)
    own, others = {}, {}

    def chip_partials(named, tag):
        names = list(named)
        g8 = [named[n].reshape((N_DEV,) + shard2d[n].shape) for n in names]
        theirs = sibling_exchange(g8, name="grads_to_sibling_" + tag)
        for n, g, t in zip(names, g8, theirs):
            own[n] = (lax.dynamic_index_in_dim(g, me, 0, keepdims=False),
                      lax.dynamic_index_in_dim(t, q_me, 0, keepdims=False))
        return names, chip_exchange_background(
            [add_pairs(g, t, core, name="chip_sum_" + n) for n, g, t in zip(names, g8, theirs)])

    d_w_gate = mm_tn(h3, dpre, name="d_w_ple_gate", out_dtype=BF16, tm=L, tko=1024, tno=1024)
    d_w_ple = mm_tn(bf(p2d), dpp, name="d_w_ple_proj", out_dtype=BF16, tm=L, tko=1024, tno=1024, out_blocks=N_DEV)
    names, bg = chip_partials({"w_ple_gate": d_w_gate, "w_ple_proj": d_w_ple}, "ple")
    dh3, got = mm_nt(dpre, w_gate, name="d_h_ple", out_dtype=F32, tm=512, tko=1024, tc=2048, bg=bg)
    others.update(zip(names, got))
    dx2, dx2b, d_ple_g = norm_bwd(dh3, x2, w["norm_ple_g"], dx3, name="d_norm_ple", want_bf16=True)
    dact = mm_nt(dx2b, w_ffn_out, name="d_act", out_dtype=F32, tm=512, tko=1408, tc=2048)
    d_w_ffn_out = mm_tn(act, dx2b, name="d_w_ffn_out", out_dtype=BF16, tm=L, tko=1408, tno=512)
    names, bg = chip_partials({"w_ffn_out": d_w_ffn_out}, "ffn_out")
    dgate, dup = swiglu_bwd(dact, gate_ff, up_ff, name="d_swiglu")
    dgu = jnp.concatenate([dgate, dup], axis=1)
    d_w_ffn_in, got = mm_tn(h2, dgu, name="d_w_ffn_in", out_dtype=BF16, tm=L, tko=512, tno=1408,
                            out_blocks=N_DEV, bg=bg)
    others.update(zip(names, got))
    names, bg = chip_partials({"w_ffn_in": d_w_ffn_in}, "ffn_in")
    dh2, got = mm_nt(dgu, w_ffn_in_blk, name="d_h_ffn", out_dtype=F32, tm=1024, tko=1024, tc=1408, bg=bg)
    others.update(zip(names, got))
    dx1, dx1b, d_ffn_g = norm_bwd(dh2, x1, w["norm_ffn_g"], dx2, name="d_norm_ffn", want_bf16=True)
    dycat = mm_nt(dx1b, w_out, name="d_ycat", out_dtype=F32, tm=512, tko=1024, tc=2048)
    d_w_out = mm_tn(ycat, dx1b, name="d_w_out", out_dtype=BF16, tm=L, tko=1024, tno=1024)
    dzu, dzv, d_sgu_w, d_sgu_b, d_ln_g, d_ln_b, d_g_b = sgu_bwd(
        z, dycat, w["sgu_ln_g"], w["sgu_ln_b"], w["sgu_w"][0], b_s3, w["out_norm_sgu_g"], name="d_sgu", d_sgu=d_sgu)
    dt_glu, dd_direct, d_g_a, d_glu_b = glu_post_bwd(
        y0, t_glu, w["ssm_glu_b"], w["out_norm_ssm_g"], dycat, name="d_s5_glu_norm")
    d_w_glu = mm_tn(ya1, dt_glu, name="d_w_glu", out_dtype=BF16, tm=L, tko=1024, tno=1024)
    names, bg = chip_partials({"w_out": d_w_out, "ssm_glu_w": d_w_glu}, "out_glu")
    dd_mm = mm_nt(dt_glu, w_glu, name="d_s5_glu_proj", out_dtype=F32, tm=512, tko=1024, tc=2048)
    du, d_bs_re, d_bs_im, d_cs_re, d_cs_im, d_a_re, d_a_im, d_d = ssm_bwd(
        z, y0, dd_direct, dd_mm, s_re, s_im, bf(tr(bs_re)), bf(tr(bs_im)), bf(tr(cs_re)), bf(tr(cs_im)),
        a_re_k, a_im_k, d_k, name="d_s5_scan")
    dz = jnp.concatenate([du, dzu, dzv], axis=1)
    d_w_in, got = mm_tn(h1, dz, name="d_w_in", out_dtype=BF16, tm=L, tko=1024, tno=1024, out_blocks=N_DEV, bg=bg)
    others.update(zip(names, got))

    from_bs = lambda t: _block_diag_part(t, SG).transpose(2, 0, 1, 3).reshape(H, G * P)
    from_cs = lambda t: _block_diag_part(t, SG).transpose(0, 1, 3, 2).reshape(1, G, H, P)
    d_q_re, d_q_im, d_bt_re, d_bt_im = disc_b_bwd(row(q_re), row(q_im), bt_re, bt_im, from_bs(d_bs_re),
                                                  from_bs(d_bs_im), name="d_s5_discretise_b")
    d_lam_re, d_lam_im, d_log_step = disc_lambda_bwd(
        lam_re, lam_im, log_step,
        (d_a_re.reshape(G, P), d_a_im.reshape(G, P), d_q_re.reshape(G, P), d_q_im.reshape(G, P)),
        name="d_s5_discretise_lambda")
    from_bt = lambda t: t.reshape(H, G, P).transpose(1, 2, 0).reshape(1, G, P, H)
    small_grads = {
        "ssm_lambda_re": d_lam_re, "ssm_lambda_im": d_lam_im, "ssm_log_step": d_log_step,
        "ssm_b_re": from_bt(d_bt_re), "ssm_b_im": from_bt(d_bt_im), "ssm_c_re": from_cs(d_cs_re),
        "ssm_c_im": from_cs(d_cs_im), "ssm_d": d_d, "ssm_glu_b": d_glu_b, "sgu_ln_g": d_ln_g, "sgu_ln_b": d_ln_b,
        "sgu_w": d_sgu_w, "sgu_b": d_sgu_b, "out_norm_ssm_g": d_g_a, "out_norm_sgu_g": d_g_b,
        "norm_ffn_g": d_ffn_g, "norm_ple_g": d_ple_g, "b_ple_gate": d_b_gate, "final_norm_g": d_final_g,
    }

    pack_g = _pack([loss_part] + [small_grads[n] for n in SMALL], PACK_ROWS)
    names, bg = chip_partials({"w_in": d_w_in}, "w_in")
    dh1, got = mm_nt(dz, w_in, name="d_h_mix", out_dtype=F32, tm=512, tko=1024, tc=3 * d_sgu,
                     bg=combine_backgrounds(bg, gather_background([pack_g])))
    others.update(zip(names, got[:-1]))
    all_g = got[-1]
    grad_x, d_mix_g = norm_bwd(dh1, x2d, w["norm_mix_g"], dx1, name="d_norm_mix", want_bf16=False)

    out = {}
    for n in SHARDED:
        res = adamw_sharded(shard2d[n], m[n].reshape(shard2d[n].shape), v[n].reshape(shard2d[n].shape),
                            own[n][0], own[n][1], others[n], name="adamw_" + n)
        out[n] = [r.reshape(w[n].shape) for r in res]

    zero_tile = jnp.zeros((8, LANE), F32)
    res = adamw_packed(_pack([zero_tile] + [w[n] for n in SMALL], PACK_ROWS),
                       _pack([zero_tile] + [m[n] for n in SMALL], PACK_ROWS),
                       _pack([zero_tile] + [v[n] for n in SMALL], PACK_ROWS), all_g, name="adamw_small")
    shapes = [(8, LANE)] + [w[n].shape for n in SMALL]
    unpacked = [_unpack(r, shapes) for r in res]
    loss = unpacked[0][0][0, 0]
    for i, n in enumerate(SMALL):
        out[n] = [u[i + 1] for u in unpacked]
    (last_g,) = all_gather_blocks([_pack([d_mix_g])], name="gather_last_grad")
    res = adamw_packed(*[_pack([t[n] for n in SMALL_LAST]) for t in (w, m, v)], last_g, name="adamw_last")
    unpacked = [_unpack(r, [w[n].shape for n in SMALL_LAST]) for r in res]
    for i, n in enumerate(SMALL_LAST):
        out[n] = [u[i] for u in unpacked]

    grads = [out[n][0] for n in WEIGHTS]
    deltas = [out[n][1] for n in WEIGHTS]
    new_m = [out[n][2] for n in WEIGHTS]
    new_v = [out[n][3] for n in WEIGHTS]
    return (loss, grad_x.reshape(x.shape), *grads, *deltas, *new_m, *new_v)


def kernel(x, p, norm_mix_g, w_in, ssm_lambda_re, ssm_lambda_im, ssm_log_step, ssm_b_re, ssm_b_im, ssm_c_re, ssm_c_im, ssm_d, ssm_glu_w, ssm_glu_b, sgu_ln_g, sgu_ln_b, sgu_w, sgu_b, out_norm_ssm_g, out_norm_sgu_g, w_out, norm_ffn_g, w_ffn_in, w_ffn_out, norm_ple_g, w_ple_gate, b_ple_gate, w_ple_proj, final_norm_g, loss_target, m_norm_mix_g, m_w_in, m_ssm_lambda_re, m_ssm_lambda_im, m_ssm_log_step, m_ssm_b_re, m_ssm_b_im, m_ssm_c_re, m_ssm_c_im, m_ssm_d, m_ssm_glu_w, m_ssm_glu_b, m_sgu_ln_g, m_sgu_ln_b, m_sgu_w, m_sgu_b, m_out_norm_ssm_g, m_out_norm_sgu_g, m_w_out, m_norm_ffn_g, m_w_ffn_in, m_w_ffn_out, m_norm_ple_g, m_w_ple_gate, m_b_ple_gate, m_w_ple_proj, m_final_norm_g, v_norm_mix_g, v_w_in, v_ssm_lambda_re, v_ssm_lambda_im, v_ssm_log_step, v_ssm_b_re, v_ssm_b_im, v_ssm_c_re, v_ssm_c_im, v_ssm_d, v_ssm_glu_w, v_ssm_glu_b, v_sgu_ln_g, v_sgu_ln_b, v_sgu_w, v_sgu_b, v_out_norm_ssm_g, v_out_norm_sgu_g, v_w_out, v_norm_ffn_g, v_w_ffn_in, v_w_ffn_out, v_norm_ple_g, v_w_ple_gate, v_b_ple_gate, v_w_ple_proj, v_final_norm_g):
    given = dict(locals())
    w = {n: given[n] for n in WEIGHTS}
    m = {n: given["m_" + n] for n in WEIGHTS}
    v = {n: given["v_" + n] for n in WEIGHTS}
    return _step(x, p, loss_target, w, m, v)
```

```python
import functools
import math

import jax
import jax.numpy as jnp
from jax import lax
from jax.experimental import pallas as pl
from jax.experimental.pallas import tpu as pltpu

F32 = jnp.float32
BF16 = jnp.bfloat16
MESH = pl.DeviceIdType.MESH
ANY = pl.BlockSpec(memory_space=pl.ANY)

N_DEV = 8
EPS = 1e-6
LAMBDA_RE_MAX = -1e-4
SSM_GROUP = 16
SSM_STATE = 64
SSM_SUPER = 16
SGU_CHUNK = 128
ADAM_LR, ADAM_B1, ADAM_B2, ADAM_EPS, ADAM_WD, ADAM_STEP = 0.001, 0.9, 0.999, 1e-08, 0.01, 10
VMEM_LIMIT = 52 * 1024 * 1024
LANE = 128
GATHER_FORWARD_AT = 0.85

_GELU_C = math.sqrt(2.0 / math.pi)


def _params(sem=None):
    return pltpu.CompilerParams(dimension_semantics=sem, vmem_limit_bytes=VMEM_LIMIT)


def _tile(dim, pref, unit=LANE):
    if dim <= pref:
        return dim
    t = (pref // unit) * unit
    while t >= unit:
        if dim % t == 0:
            return t
        t -= unit
    return dim


def _gelu(x):
    return 0.5 * x * (1.0 + jnp.tanh(_GELU_C * (x + 0.044715 * x * x * x)))


def _gelu_grad(x):
    t = jnp.tanh(_GELU_C * (x + 0.044715 * x * x * x))
    return 0.5 * (1.0 + t) + 0.5 * x * (1.0 - t * t) * (_GELU_C * (1.0 + 3.0 * 0.044715 * x * x))


def _rms(x):
    return lax.rsqrt(jnp.mean(x * x, axis=-1, keepdims=True) + EPS)


def _rmsnorm_bwd(dy, x, r, g):
    dyg = dy * g
    return r * dyg - x * (r * r * r) * jnp.mean(dyg * x, axis=-1, keepdims=True)


def _rowsum(v):
    return jnp.sum(v, axis=0, keepdims=True)


class Background:
    def __init__(self, inputs, out_shapes, scratch, phases):
        self.inputs, self.out_shapes, self.scratch, self.phases = list(inputs), list(out_shapes), list(scratch), phases

    def emit(self, step, nsteps, ins, outs, scratch):
        for place, phase in self.phases:
            at = min(int(place * nsteps), nsteps - 1)

            @pl.when(step == at)
            def _():
                phase(ins, outs, scratch)


def combine_backgrounds(first, second):
    ni, no, ns = len(first.inputs), len(first.out_shapes), len(first.scratch)
    phases = [(place, lambda i, o, s, f=fn: f(i[:ni], o[:no], s[:ns])) for place, fn in first.phases]
    phases += [(place, lambda i, o, s, f=fn: f(i[ni:], o[no:], s[ns:])) for place, fn in second.phases]
    return Background(first.inputs + second.inputs, first.out_shapes + second.out_shapes,
                      first.scratch + second.scratch, sorted(phases, key=lambda p: p[0]))


def _carrier(bg, n_in, n_out, n_scratch, grid):
    nbi = len(bg.inputs) if bg else 0
    nbo = len(bg.out_shapes) if bg else 0
    nsteps = math.prod(grid)

    def split(refs):
        ins = refs[:n_in]
        bg_ins = refs[n_in:n_in + nbi]
        outs = refs[n_in + nbi:n_in + nbi + n_out]
        bg_outs = refs[n_in + nbi + n_out:n_in + nbi + n_out + nbo]
        rest = refs[n_in + nbi + n_out + nbo:]
        scratch, bg_scratch = rest[:n_scratch], rest[n_scratch:]

        def run_background():
            if bg is None:
                return
            step = pl.program_id(0)
            for axis in range(1, len(grid)):
                step = step * grid[axis] + pl.program_id(axis)
            bg.emit(step, nsteps, bg_ins, bg_outs, bg_scratch)

        return ins, outs, scratch, run_background

    if bg is None:
        return [], [], [], [], [], split
    return [ANY] * nbi, list(bg.inputs), [ANY] * nbo, list(bg.out_shapes), list(bg.scratch), split


def _semantics(bg, sem):
    return tuple("arbitrary" for _ in sem) if bg is not None else sem


def _results(res, n_out, bg):
    res = list(res) if isinstance(res, (list, tuple)) else [res]
    own = res[0] if n_out == 1 else res[:n_out]
    return (own, res[n_out:]) if bg is not None else own


def mm_nn(a, b, *, name, out_dtype, tm, tn, tk, residual=None, bg=None):
    M, K = a.shape
    blocked = b.ndim == 3
    if blocked:
        nb, _, Nb = b.shape
        N = nb * Nb
        tn = _tile(Nb, tn)
        per = Nb // tn
    else:
        N = b.shape[1]
        tn = _tile(N, tn)
    tm, tk = _tile(M, tm, 8), _tile(K, tk)
    nj, ni, nk = N // tn, M // tm, K // tk
    has_res = residual is not None
    grid = (nj, ni, nk)
    bg_in_specs, bg_args, bg_out_specs, bg_out_shapes, bg_scratch, split = _carrier(
        bg, 3 if has_res else 2, 1, 0 if nk == 1 else 1, grid)

    def body(*refs):
        ins, (o_ref,), scratch, run_background = split(refs)
        run_background()
        a_ref, b_ref = ins[0], ins[1]
        r_ref = ins[2] if has_res else None

        def finish(acc):
            if has_res:
                acc = acc + r_ref[...]
            o_ref[...] = acc.astype(o_ref.dtype)

        part = jnp.dot(a_ref[...], b_ref[...], preferred_element_type=F32)
        if nk == 1:
            finish(part)
        else:
            acc_ref = scratch[0]
            k = pl.program_id(2)

            @pl.when(k == 0)
            def _():
                acc_ref[...] = part

            @pl.when(k > 0)
            def _():
                acc_ref[...] += part

            @pl.when(k == nk - 1)
            def _():
                finish(acc_ref[...])

    if blocked:
        b_spec = pl.BlockSpec((None, tk, tn), lambda j, i, k: (j // per, k, j % per))
    else:
        b_spec = pl.BlockSpec((tk, tn), lambda j, i, k: (k, j))
    in_specs = [pl.BlockSpec((tm, tk), lambda j, i, k: (i, k)), b_spec]
    args = [a, b]
    if has_res:
        in_specs.append(pl.BlockSpec((tm, tn), lambda j, i, k: (i, j)))
        args.append(residual)
    res = pl.pallas_call(
        body, name=name, grid=grid,
        in_specs=in_specs + bg_in_specs,
        out_specs=[pl.BlockSpec((tm, tn), lambda j, i, k: (i, j))] + bg_out_specs,
        out_shape=[jax.ShapeDtypeStruct((M, N), out_dtype)] + bg_out_shapes,
        scratch_shapes=([] if nk == 1 else [pltpu.VMEM((tm, tn), F32)]) + bg_scratch,
        compiler_params=_params(_semantics(bg, ("parallel", "parallel", "arbitrary"))),
    )(*args, *bg_args)
    return _results(res, 1, bg)


def mm_nt(a, w, *, name, out_dtype, tm, tko, tc, a2=None, bg=None):
    M, N = a.shape
    if a2 is not None:
        N = 2 * N
    blocked = w.ndim == 3
    if blocked:
        nb, Ko, Nb = w.shape
        tc = _tile(Nb, tc)
        per = Nb // tc
    else:
        Ko = w.shape[0]
        tc = _tile(N, tc)
    tm, tko = _tile(M, tm, 8), _tile(Ko, tko)
    njo, ni, nc = Ko // tko, M // tm, N // tc
    grid = (njo, ni, nc)
    half = nc // 2
    bg_in_specs, bg_args, bg_out_specs, bg_out_shapes, bg_scratch, split = _carrier(
        bg, 2 if a2 is None else 3, 1, 0 if nc == 1 else 1, grid)

    def body(*refs):
        ins, (o_ref,), scratch, run_background = split(refs)
        run_background()
        a_val = ins[0][...]
        if a2 is not None:
            a_val = jnp.where(pl.program_id(2) < half, a_val, ins[1][...])
        part = lax.dot_general(a_val, ins[-1][...], (((1,), (1,)), ((), ())),
                               preferred_element_type=F32)
        if nc == 1:
            o_ref[...] = part.astype(o_ref.dtype)
        else:
            acc_ref = scratch[0]
            c = pl.program_id(2)

            @pl.when(c == 0)
            def _():
                acc_ref[...] = part

            @pl.when(c > 0)
            def _():
                acc_ref[...] += part

            @pl.when(c == nc - 1)
            def _():
                o_ref[...] = acc_ref[...].astype(o_ref.dtype)

    if blocked:
        w_spec = pl.BlockSpec((None, tko, tc), lambda j, i, c: (c // per, j, c % per))
    else:
        w_spec = pl.BlockSpec((tko, tc), lambda j, i, c: (j, c))
    if a2 is None:
        a_specs, a_args = [pl.BlockSpec((tm, tc), lambda j, i, c: (i, c))], [a]
    else:
        a_specs = [pl.BlockSpec((tm, tc), lambda j, i, c: (i, jnp.minimum(c, half - 1))),
                   pl.BlockSpec((tm, tc), lambda j, i, c: (i, jnp.maximum(c - half, 0)))]
        a_args = [a, a2]
    res = pl.pallas_call(
        body, name=name, grid=grid,
        in_specs=a_specs + [w_spec] + bg_in_specs,
        out_specs=[pl.BlockSpec((tm, tko), lambda j, i, c: (i, j))] + bg_out_specs,
        out_shape=[jax.ShapeDtypeStruct((M, Ko), out_dtype)] + bg_out_shapes,
        scratch_shapes=([] if nc == 1 else [pltpu.VMEM((tm, tko), F32)]) + bg_scratch,
        compiler_params=_params(_semantics(bg, ("parallel", "parallel", "arbitrary"))),
    )(*a_args, w, *bg_args)
    return _results(res, 1, bg)


def mm_tn(a, g, *, name, out_dtype, tm, tko, tno, out_blocks=None, block_offset=0, total_blocks=None, into=None,
          bg=None):
    M, K = a.shape
    N = g.shape[1]
    if out_blocks:
        Nb = N // out_blocks
        tno = _tile(Nb, tno)
        per = Nb // tno
    else:
        tno = _tile(N, tno)
    tm, tko = _tile(M, tm), _tile(K, tko)
    njo, njn, nm = K // tko, N // tno, M // tm
    grid = (njo, njn, nm)
    bg_in_specs, bg_args, bg_out_specs, bg_out_shapes, bg_scratch, split = _carrier(
        bg, 2 if into is None else 3, 1, 0 if nm == 1 else 1, grid)

    def body(*refs):
        ins, (o_ref,), scratch, run_background = split(refs)
        a_ref, g_ref = ins[0], ins[1]
        run_background()
        part = lax.dot_general(a_ref[...], g_ref[...], (((0,), (0,)), ((), ())),
                               preferred_element_type=F32)
        if nm == 1:
            o_ref[...] = part.astype(o_ref.dtype)
        else:
            acc_ref = scratch[0]
            m = pl.program_id(2)

            @pl.when(m == 0)
            def _():
                acc_ref[...] = part

            @pl.when(m > 0)
            def _():
                acc_ref[...] += part

            @pl.when(m == nm - 1)
            def _():
                o_ref[...] = acc_ref[...].astype(o_ref.dtype)

    if out_blocks:
        o_spec = pl.BlockSpec((None, tko, tno), lambda jo, jn, m: (jn // per + block_offset, jo, jn % per))
        o_shape = jax.ShapeDtypeStruct((total_blocks or out_blocks, K, Nb), out_dtype)
    else:
        o_spec = pl.BlockSpec((tko, tno), lambda jo, jn, m: (jo, jn))
        o_shape = jax.ShapeDtypeStruct((K, N), out_dtype)
    res = pl.pallas_call(
        body, name=name, grid=grid,
        in_specs=[pl.BlockSpec((tm, tko), lambda jo, jn, m: (m, jo)),
                  pl.BlockSpec((tm, tno), lambda jo, jn, m: (m, jn))] + ([] if into is None else [ANY]) + bg_in_specs,
        out_specs=[o_spec] + bg_out_specs, out_shape=[o_shape] + bg_out_shapes,
        scratch_shapes=([] if nm == 1 else [pltpu.VMEM((tko, tno), F32)]) + bg_scratch,
        input_output_aliases={} if into is None else {2: 0},
        compiler_params=_params(_semantics(bg, ("parallel", "parallel", "arbitrary"))),
    )(a, g, *([] if into is None else [into]), *bg_args)
    return _results(res, 1, bg)


def ffn_in_swiglu(h, w_blk, *, name, tm, bg=None):
    M, K = h.shape
    nb, _, Nb = w_blk.shape
    nh = nb // 2
    F = nh * Nb
    tm = _tile(M, tm, 8)
    grid = (nh, M // tm)
    bg_in_specs, bg_args, bg_out_specs, bg_out_shapes, bg_scratch, split = _carrier(bg, 3, 3, 0, grid)

    def body(*refs):
        (h_ref, wg_ref, wu_ref), (act_ref, gate_ref, up_ref), _, run_background = split(refs)
        run_background()
        hv = h_ref[...]
        gate = jnp.dot(hv, wg_ref[...], preferred_element_type=F32)
        up = jnp.dot(hv, wu_ref[...], preferred_element_type=F32)
        gate_ref[...] = gate
        up_ref[...] = up
        act_ref[...] = (gate * jax.nn.sigmoid(gate) * up).astype(act_ref.dtype)

    o_spec = pl.BlockSpec((tm, Nb), lambda j, i: (i, j))
    res = pl.pallas_call(
        body, name=name, grid=grid,
        in_specs=[pl.BlockSpec((tm, K), lambda j, i: (i, 0)),
                  pl.BlockSpec((None, K, Nb), lambda j, i: (j, 0, 0)),
                  pl.BlockSpec((None, K, Nb), lambda j, i: (j + nh, 0, 0))] + bg_in_specs,
        out_specs=[o_spec, o_spec, o_spec] + bg_out_specs,
        out_shape=[jax.ShapeDtypeStruct((M, F), BF16), jax.ShapeDtypeStruct((M, F), F32),
                   jax.ShapeDtypeStruct((M, F), F32)] + bg_out_shapes,
        scratch_shapes=bg_scratch,
        compiler_params=_params(_semantics(bg, ("parallel", "parallel"))),
    )(h, w_blk, w_blk, *bg_args)
    return _results(res, 3, bg)


def _row_spec(tm, d, col=0):
    return pl.BlockSpec((tm, d), lambda i: (i, col))


def _vec_spec(d):
    return pl.BlockSpec((1, d), lambda i: (0, 0))


def norm_fwd(x, g, *, name, tm=256):
    L, D = x.shape
    tm = _tile(L, tm, 8)

    def body(x_ref, g_ref, h_ref):
        xv = x_ref[...]
        h_ref[...] = (xv * _rms(xv) * g_ref[...]).astype(h_ref.dtype)

    return pl.pallas_call(
        body, name=name, grid=(L // tm,),
        in_specs=[_row_spec(tm, D), _vec_spec(D)],
        out_specs=_row_spec(tm, D),
        out_shape=jax.ShapeDtypeStruct((L, D), BF16),
        compiler_params=_params(("parallel",)),
    )(x, g)


def norm_bwd(dh, xin, g, dres, *, name, want_bf16, tm=128):
    L, D = xin.shape
    tm = _tile(L, tm, 8)

    def body(dh_ref, x_ref, g_ref, dres_ref, dx_ref, *rest):
        dg_ref = rest[-1]
        xv, dhv = x_ref[...], dh_ref[...]
        r = _rms(xv)
        dx = dres_ref[...] + _rmsnorm_bwd(dhv, xv, r, g_ref[...])
        dx_ref[...] = dx
        if want_bf16:
            rest[0][...] = dx.astype(BF16)
        part = _rowsum(dhv * xv * r)

        @pl.when(pl.program_id(0) == 0)
        def _():
            dg_ref[...] = part

        @pl.when(pl.program_id(0) > 0)
        def _():
            dg_ref[...] += part

    out_specs = [_row_spec(tm, D)] + ([_row_spec(tm, D)] if want_bf16 else []) + [_vec_spec(D)]
    out_shape = ([jax.ShapeDtypeStruct((L, D), F32)]
                 + ([jax.ShapeDtypeStruct((L, D), BF16)] if want_bf16 else [])
                 + [jax.ShapeDtypeStruct((1, D), F32)])
    return pl.pallas_call(
        body, name=name, grid=(L // tm,),
        in_specs=[_row_spec(tm, D), _row_spec(tm, D), _vec_spec(D), _row_spec(tm, D)],
        out_specs=out_specs, out_shape=out_shape,
        compiler_params=_params(("arbitrary",)),
    )(dh, xin, g, dres)


def glu_pre(y0, *, name, tm=256):
    L, D = y0.shape
    tm = _tile(L, tm, 8)

    def body(y_ref, o_ref):
        o_ref[...] = _gelu(y_ref[...]).astype(o_ref.dtype)

    return pl.pallas_call(
        body, name=name, grid=(L // tm,),
        in_specs=[_row_spec(tm, D)], out_specs=_row_spec(tm, D),
        out_shape=jax.ShapeDtypeStruct((L, D), BF16),
        compiler_params=_params(("parallel",)),
    )(y0)


def glu_post(y0, t, b_glu, g_a, *, name, tm=256):
    L, D = y0.shape
    tm = _tile(L, tm, 8)

    def body(y_ref, t_ref, b_ref, g_ref, o_ref):
        ya = _gelu(y_ref[...]) * jax.nn.sigmoid(t_ref[...] + b_ref[...])
        o_ref[...] = (ya * _rms(ya) * g_ref[...]).astype(o_ref.dtype)

    return pl.pallas_call(
        body, name=name, grid=(L // tm,),
        in_specs=[_row_spec(tm, D), _row_spec(tm, D), _vec_spec(D), _vec_spec(D)],
        out_specs=_row_spec(tm, D),
        out_shape=jax.ShapeDtypeStruct((L, D), BF16),
        compiler_params=_params(("parallel",)),
    )(y0, t, b_glu, g_a)


def glu_post_bwd(y0, t, b_glu, g_a, dycat, *, name, tm=128):
    L, D = y0.shape
    tm = _tile(L, tm, 8)

    def body(y_ref, t_ref, b_ref, g_ref, dn_ref, dt_ref, dd_ref, dga_ref, dbg_ref):
        ya1 = _gelu(y_ref[...])
        sg = jax.nn.sigmoid(t_ref[...] + b_ref[...])
        ya = ya1 * sg
        ra = _rms(ya)
        dn = dn_ref[...]
        dya = _rmsnorm_bwd(dn, ya, ra, g_ref[...])
        dt = dya * ya1 * sg * (1.0 - sg)
        dt_ref[...] = dt.astype(BF16)
        dd_ref[...] = dya * sg
        p_ga, p_bg = _rowsum(dn * ya * ra), _rowsum(dt)

        @pl.when(pl.program_id(0) == 0)
        def _():
            dga_ref[...] = p_ga
            dbg_ref[...] = p_bg

        @pl.when(pl.program_id(0) > 0)
        def _():
            dga_ref[...] += p_ga
            dbg_ref[...] += p_bg

    return pl.pallas_call(
        body, name=name, grid=(L // tm,),
        in_specs=[_row_spec(tm, D), _row_spec(tm, D), _vec_spec(D), _vec_spec(D), _row_spec(tm, D, 0)],
        out_specs=[_row_spec(tm, D), _row_spec(tm, D), _vec_spec(D), _vec_spec(D)],
        out_shape=[jax.ShapeDtypeStruct((L, D), BF16), jax.ShapeDtypeStruct((L, D), F32),
                   jax.ShapeDtypeStruct((1, D), F32), jax.ShapeDtypeStruct((1, D), F32)],
        compiler_params=_params(("arbitrary",)),
    )(y0, t, b_glu, g_a, dycat)


def head_and_loss(x2, gpre, b_g, pp, g_f, tgt, *, name, tm=128):
    L, D = x2.shape
    tm = _tile(L, tm, 8)

    def body(x2_ref, gp_ref, bg_ref, pp_ref, gf_ref, tg_ref,
             dx3_ref, dpre_ref, dpp_ref, loss_ref, dgf_ref, dbg_ref):
        gate = jax.nn.sigmoid(gp_ref[...] + bg_ref[...])
        ppv = pp_ref[...]
        x3 = x2_ref[...] + gate * ppv
        r = _rms(x3)
        xn = x3 * r
        gf = gf_ref[...]
        err = xn * gf - tg_ref[...]
        loss = 0.5 * jnp.sum(jnp.mean(err * err, axis=-1, keepdims=True), axis=0, keepdims=True)
        dout = err * (1.0 / D)
        dx3 = _rmsnorm_bwd(dout, x3, r, gf)
        dx3_ref[...] = dx3
        dpre = dx3 * ppv * gate * (1.0 - gate)
        dpre_ref[...] = dpre.astype(BF16)
        dpp_ref[...] = (dx3 * gate).astype(BF16)
        p_gf, p_bg = _rowsum(dout * xn), _rowsum(dpre)
        p_loss = jnp.broadcast_to(loss, loss_ref.shape)

        @pl.when(pl.program_id(0) == 0)
        def _():
            loss_ref[...] = p_loss
            dgf_ref[...] = p_gf
            dbg_ref[...] = p_bg

        @pl.when(pl.program_id(0) > 0)
        def _():
            loss_ref[...] += p_loss
            dgf_ref[...] += p_gf
            dbg_ref[...] += p_bg

    rs = _row_spec(tm, D)
    return pl.pallas_call(
        body, name=name, grid=(L // tm,),
        in_specs=[rs, rs, _vec_spec(D), rs, _vec_spec(D), rs],
        out_specs=[rs, rs, rs, pl.BlockSpec((8, LANE), lambda i: (0, 0)), _vec_spec(D), _vec_spec(D)],
        out_shape=[jax.ShapeDtypeStruct((L, D), F32), jax.ShapeDtypeStruct((L, D), BF16),
                   jax.ShapeDtypeStruct((L, D), BF16), jax.ShapeDtypeStruct((8, LANE), F32),
                   jax.ShapeDtypeStruct((1, D), F32), jax.ShapeDtypeStruct((1, D), F32)],
        compiler_params=_params(("arbitrary",)),
    )(x2, gpre, b_g, pp, g_f, tgt)


def ffn_out_bwd_swiglu(dx, w, gate, up, *, name, tm=512, tf=1408):
    M, D = dx.shape
    F = w.shape[0]
    tm, tf = _tile(M, tm, 8), _tile(F, tf)

    def body(dx_ref, w_ref, g_ref, u_ref, dg_ref, du_ref):
        da = lax.dot_general(dx_ref[...], w_ref[...], (((1,), (1,)), ((), ())), preferred_element_type=F32)
        gv = g_ref[...]
        sg = jax.nn.sigmoid(gv)
        dg_ref[...] = (da * u_ref[...] * sg * (1.0 + gv * (1.0 - sg))).astype(BF16)
        du_ref[...] = (da * gv * sg).astype(BF16)

    spec = pl.BlockSpec((tm, tf), lambda j, i: (i, j))
    return pl.pallas_call(
        body, name=name, grid=(F // tf, M // tm),
        in_specs=[pl.BlockSpec((tm, D), lambda j, i: (i, 0)), pl.BlockSpec((tf, D), lambda j, i: (j, 0)), spec, spec],
        out_specs=[spec, spec],
        out_shape=[jax.ShapeDtypeStruct((M, F), BF16), jax.ShapeDtypeStruct((M, F), BF16)],
        compiler_params=_params(("parallel", "parallel")),
    )(dx, w, gate, up)


def _sgu_forward_values(zu, zv, lng, lnb, w_ref, bs_ref, s_scr, heads, hd):
    u1 = _gelu(zu)
    v1 = _gelu(zv)
    xc = v1 - jnp.mean(v1, axis=-1, keepdims=True)
    r = lax.rsqrt(jnp.mean(xc * xc, axis=-1, keepdims=True) + EPS)
    xhat = xc * r
    v2 = xhat * lng + lnb
    tril = (lax.broadcasted_iota(jnp.int32, (SGU_CHUNK, SGU_CHUNK), 0)
            >= lax.broadcasted_iota(jnp.int32, (SGU_CHUNK, SGU_CHUNK), 1))
    for h in range(heads):
        wm = jnp.where(tril, w_ref[h], 0.0).astype(BF16)
        cols = slice(h * hd, (h + 1) * hd)
        s_scr[:, cols] = jnp.dot(wm, v2[:, cols].astype(BF16), preferred_element_type=F32) + bs_ref[h]
    return u1, xhat, r, v2, tril


def sgu_fwd(z, ln_g, ln_b, w_s, b_s, g_b, *, name, d_sgu):
    L = z.shape[0]
    heads = w_s.shape[0]
    hd = d_sgu // heads

    def body(zu_ref, zv_ref, lng_ref, lnb_ref, w_ref, bs_ref, gb_ref, o_ref, s_scr):
        u1, _, _, _, _ = _sgu_forward_values(zu_ref[...], zv_ref[...], lng_ref[...], lnb_ref[...],
                                             w_ref, bs_ref, s_scr, heads, hd)
        yb = u1 * s_scr[...]
        o_ref[...] = (yb * _rms(yb) * gb_ref[...]).astype(o_ref.dtype)

    blk = lambda col: pl.BlockSpec((SGU_CHUNK, d_sgu), lambda n: (n, col))
    return pl.pallas_call(
        body, name=name, grid=(L // SGU_CHUNK,),
        in_specs=[blk(1), blk(2), _vec_spec(d_sgu), _vec_spec(d_sgu),
                  pl.BlockSpec(w_s.shape, lambda n: (0, 0, 0)), pl.BlockSpec(b_s.shape, lambda n: (0, 0, 0)),
                  _vec_spec(d_sgu)],
        out_specs=blk(0),
        out_shape=jax.ShapeDtypeStruct((L, d_sgu), BF16),
        scratch_shapes=[pltpu.VMEM((SGU_CHUNK, d_sgu), F32)],
        compiler_params=_params(("parallel",)),
    )(z, z, ln_g, ln_b, w_s, b_s, g_b)


def sgu_bwd(z, dycat, ln_g, ln_b, w_s, b_s, g_b, *, name, d_sgu):
    L = z.shape[0]
    heads = w_s.shape[0]
    hd = d_sgu // heads

    def body(zu_ref, zv_ref, dn_ref, lng_ref, lnb_ref, w_ref, bs_ref, gb_ref,
             dzu_ref, dzv_ref, dw_ref, dbs_ref, dlng_ref, dlnb_ref, dgb_ref, s_scr, dv_scr):
        first = pl.program_id(0) == 0
        zu, zv, lng = zu_ref[...], zv_ref[...], lng_ref[...]
        u1, xhat, r, v2, tril = _sgu_forward_values(zu, zv, lng, lnb_ref[...], w_ref, bs_ref, s_scr, heads, hd)
        s = s_scr[...]
        yb = u1 * s
        rb = _rms(yb)
        dn = dn_ref[...]
        dyb = _rmsnorm_bwd(dn, yb, rb, gb_ref[...])
        dzu_ref[...] = (dyb * s * _gelu_grad(zu)).astype(BF16)
        ds = dyb * u1
        for h in range(heads):
            cols = slice(h * hd, (h + 1) * hd)
            ds_h = ds[:, cols]
            ds_hb = ds_h.astype(BF16)
            wm = jnp.where(tril, w_ref[h], 0.0).astype(BF16)
            dw_h = jnp.where(tril, lax.dot_general(ds_hb, v2[:, cols].astype(BF16), (((1,), (1,)), ((), ())),
                                                   preferred_element_type=F32), 0.0)
            db_h = jnp.sum(ds_h, axis=1, keepdims=True)
            dv_scr[:, cols] = lax.dot_general(wm, ds_hb, (((0,), (0,)), ((), ())), preferred_element_type=F32)

            @pl.when(first)
            def _():
                dw_ref[h] = dw_h
                dbs_ref[h] = db_h

            @pl.when(jnp.logical_not(first))
            def _():
                dw_ref[h] += dw_h
                dbs_ref[h] += db_h

        dv2 = dv_scr[...]
        dxh = dv2 * lng
        dv1 = r * (dxh - jnp.mean(dxh, axis=-1, keepdims=True)
                   - xhat * jnp.mean(dxh * xhat, axis=-1, keepdims=True))
        dzv_ref[...] = (dv1 * _gelu_grad(zv)).astype(BF16)
        p_lng, p_lnb, p_gb = _rowsum(dv2 * xhat), _rowsum(dv2), _rowsum(dn * yb * rb)

        @pl.when(first)
        def _():
            dlng_ref[...] = p_lng
            dlnb_ref[...] = p_lnb
            dgb_ref[...] = p_gb

        @pl.when(jnp.logical_not(first))
        def _():
            dlng_ref[...] += p_lng
            dlnb_ref[...] += p_lnb
            dgb_ref[...] += p_gb

    blk = lambda col: pl.BlockSpec((SGU_CHUNK, d_sgu), lambda n: (n, col))
    full3 = lambda shape: pl.BlockSpec(shape, lambda n: (0, 0, 0))
    return pl.pallas_call(
        body, name=name, grid=(L // SGU_CHUNK,),
        in_specs=[blk(1), blk(2), blk(1), _vec_spec(d_sgu), _vec_spec(d_sgu),
                  full3(w_s.shape), full3(b_s.shape), _vec_spec(d_sgu)],
        out_specs=[blk(0), blk(0), full3(w_s.shape), full3(b_s.shape),
                   _vec_spec(d_sgu), _vec_spec(d_sgu), _vec_spec(d_sgu)],
        out_shape=[jax.ShapeDtypeStruct((L, d_sgu), BF16), jax.ShapeDtypeStruct((L, d_sgu), BF16),
                   jax.ShapeDtypeStruct(w_s.shape, F32), jax.ShapeDtypeStruct(b_s.shape, F32),
                   jax.ShapeDtypeStruct((1, d_sgu), F32), jax.ShapeDtypeStruct((1, d_sgu), F32),
                   jax.ShapeDtypeStruct((1, d_sgu), F32)],
        scratch_shapes=[pltpu.VMEM((SGU_CHUNK, d_sgu), F32), pltpu.VMEM((SGU_CHUNK, d_sgu), F32)],
        compiler_params=_params(("arbitrary",)),
    )(z, z, dycat, ln_g, ln_b, w_s, b_s, g_b)


def _disc_lambda(lam_re, lam_im, log_step):
    lr = jnp.minimum(lam_re, LAMBDA_RE_MAX)
    li = lam_im
    dt = jnp.exp(log_step)
    mag = jnp.exp(lr * dt)
    ang = li * dt
    a_re = mag * jnp.cos(ang)
    a_im = mag * jnp.sin(ang)
    nr = a_re - 1.0
    ni = a_im
    den = lr * lr + li * li
    return a_re, a_im, (nr * lr + ni * li) / den, (ni * lr - nr * li) / den


def _disc_b(q_re, q_im, b_re, b_im):
    return q_re * b_re - q_im * b_im, q_re * b_im + q_im * b_re


def disc_lambda_fwd(lam_re, lam_im, log_step, *, name):
    def body(lr_ref, li_ref, ls_ref, ar_ref, ai_ref, qr_ref, qi_ref):
        ar_ref[...], ai_ref[...], qr_ref[...], qi_ref[...] = _disc_lambda(lr_ref[...], li_ref[...], ls_ref[...])

    sd = jax.ShapeDtypeStruct(lam_re.shape, F32)
    return pl.pallas_call(body, name=name, out_shape=[sd, sd, sd, sd], compiler_params=_params())(
        lam_re, lam_im, log_step)


def disc_lambda_bwd(lam_re, lam_im, log_step, cts, *, name):
    def body(lr_ref, li_ref, ls_ref, c0, c1, c2, c3, dlr_ref, dli_ref, dls_ref):
        _, vjp = jax.vjp(_disc_lambda, lr_ref[...], li_ref[...], ls_ref[...])
        dlr_ref[...], dli_ref[...], dls_ref[...] = vjp((c0[...], c1[...], c2[...], c3[...]))

    sd = jax.ShapeDtypeStruct(lam_re.shape, F32)
    return pl.pallas_call(body, name=name, out_shape=[sd, sd, jax.ShapeDtypeStruct(log_step.shape, F32)],
                          compiler_params=_params())(lam_re, lam_im, log_step, *cts)


def disc_b_fwd(q_re, q_im, b_re, b_im, *, name):
    def body(qr_ref, qi_ref, br_ref, bi_ref, or_ref, oi_ref):
        or_ref[...], oi_ref[...] = _disc_b(qr_ref[...], qi_ref[...], br_ref[...], bi_ref[...])

    sd = jax.ShapeDtypeStruct(b_re.shape, F32)
    return pl.pallas_call(body, name=name, out_shape=[sd, sd], compiler_params=_params())(q_re, q_im, b_re, b_im)


def disc_b_bwd(q_re, q_im, b_re, b_im, ct_re, ct_im, *, name):
    def body(qr_ref, qi_ref, br_ref, bi_ref, cr_ref, ci_ref, dqr_ref, dqi_ref, dbr_ref, dbi_ref):
        _, vjp = jax.vjp(_disc_b, qr_ref[...], qi_ref[...], br_ref[...], bi_ref[...])
        dqr_ref[...], dqi_ref[...], dbr_ref[...], dbi_ref[...] = vjp((cr_ref[...], ci_ref[...]))

    sq, sb = jax.ShapeDtypeStruct(q_re.shape, F32), jax.ShapeDtypeStruct(b_re.shape, F32)
    return pl.pallas_call(body, name=name, out_shape=[sq, sq, sb, sb], compiler_params=_params())(
        q_re, q_im, b_re, b_im, ct_re, ct_im)


def _lti_scan(xr, xi, ar, ai, reverse):
    T = xr.shape[0]
    row = lax.broadcasted_iota(jnp.int32, xr.shape, 0)
    k = 1
    while k < T:
        shift = T - k if reverse else k
        keep = (row < T - k) if reverse else (row >= k)
        sr = jnp.where(keep, pltpu.roll(xr, shift, 0), 0.0)
        si = jnp.where(keep, pltpu.roll(xi, shift, 0), 0.0)
        xr, xi = xr + ar * sr - ai * si, xi + ar * si + ai * sr
        ar, ai = ar * ar - ai * ai, 2.0 * ar * ai
        k *= 2
    return xr, xi


def _ssm_chunk(L):
    return _tile(L, 256, 8)


def ssm_fwd(z, bs_re, bs_im, cs_re, cs_im, a_re, a_im, d, *, name, bg=None):
    L = z.shape[0]
    NK, C, S = bs_re.shape
    T = _ssm_chunk(L)
    grid = (NK, L // T)
    bg_in_specs, bg_args, bg_out_specs, bg_out_shapes, bg_scratch, split = _carrier(bg, 8, 3, 4, grid)

    def body(*refs):
        ((u_ref, br_ref, bi_ref, cr_ref, ci_ref, ar_ref, ai_ref, d_ref), (y_ref, sr_ref, si_ref),
         (car_re, car_im, pw_re, pw_im), run_background) = split(refs)
        run_background()
        i = pl.program_id(1)
        ar, ai = ar_ref[...], ai_ref[...]

        @pl.when(i == 0)
        def _():
            row = lax.broadcasted_iota(jnp.int32, (T, S), 0)
            pr, pi = _lti_scan(jnp.where(row == 0, ar, 0.0), jnp.where(row == 0, ai, 0.0), ar, ai, False)
            pw_re[...] = pr
            pw_im[...] = pi
            car_re[...] = jnp.zeros_like(car_re)
            car_im[...] = jnp.zeros_like(car_im)

        u = u_ref[...]
        ub = u.astype(BF16)
        xr = jnp.dot(ub, br_ref[...], preferred_element_type=F32)
        xi = jnp.dot(ub, bi_ref[...], preferred_element_type=F32)
        xr, xi = _lti_scan(xr, xi, ar, ai, False)
        cr, ci = car_re[...], car_im[...]
        pr, pi = pw_re[...], pw_im[...]
        s_re = xr + pr * cr - pi * ci
        s_im = xi + pr * ci + pi * cr
        sr_ref[...] = s_re
        si_ref[...] = s_im
        car_re[...] = s_re[T - 1:T, :]
        car_im[...] = s_im[T - 1:T, :]
        y_ref[...] = (jnp.dot(s_re.astype(BF16), cr_ref[...], preferred_element_type=F32)
                      - jnp.dot(s_im.astype(BF16), ci_ref[...], preferred_element_type=F32)
                      + d_ref[...] * u)

    kspec = lambda shape: pl.BlockSpec((None,) + shape, lambda k, i: (k, 0, 0))
    res = pl.pallas_call(
        body, name=name, grid=grid,
        in_specs=[pl.BlockSpec((T, C), lambda k, i: (i, k)),
                  kspec((C, S)), kspec((C, S)), kspec((S, C)), kspec((S, C)),
                  kspec((1, S)), kspec((1, S)), kspec((1, C))] + bg_in_specs,
        out_specs=[pl.BlockSpec((T, C), lambda k, i: (i, k)),
                   pl.BlockSpec((T, S), lambda k, i: (i, k)), pl.BlockSpec((T, S), lambda k, i: (i, k))] + bg_out_specs,
        out_shape=[jax.ShapeDtypeStruct((L, NK * C), F32), jax.ShapeDtypeStruct((L, NK * S), F32),
                   jax.ShapeDtypeStruct((L, NK * S), F32)] + bg_out_shapes,
        scratch_shapes=[pltpu.VMEM((1, S), F32), pltpu.VMEM((1, S), F32),
                        pltpu.VMEM((T, S), F32), pltpu.VMEM((T, S), F32)] + bg_scratch,
        compiler_params=_params(_semantics(bg, ("parallel", "arbitrary"))),
    )(z, bs_re, bs_im, cs_re, cs_im, a_re, a_im, d, *bg_args)
    return _results(res, 3, bg)


def ssm_bwd(z, y0, dd_direct, dd_mm, s_re, s_im, bst_re, bst_im, cst_re, cst_im, a_re, a_im, d, *, name):
    L = z.shape[0]
    NK, S, C = bst_re.shape
    T = _ssm_chunk(L)
    nchunk = L // T
    tail = T // 8

    def body(u_ref, y_ref, d1_ref, d2_ref, sr_ref, si_ref, pr_ref, pi_ref,
             btr_ref, bti_ref, ctr_ref, cti_ref, ar_ref, ai_ref, d_ref,
             du_ref, dbr_ref, dbi_ref, dcr_ref, dci_ref, dar_ref, dai_ref, dd_ref,
             car_re, car_im, pw_re, pw_im):
        i = pl.program_id(1)
        chunk = nchunk - 1 - i
        ar, ai = ar_ref[...], ai_ref[...]
        nai = -ai
        row = lax.broadcasted_iota(jnp.int32, (T, S), 0)

        @pl.when(i == 0)
        def _():
            qr, qi = _lti_scan(jnp.where(row == T - 1, ar, 0.0), jnp.where(row == T - 1, nai, 0.0), ar, nai, True)
            pw_re[...] = qr
            pw_im[...] = qi
            car_re[...] = jnp.zeros_like(car_re)
            car_im[...] = jnp.zeros_like(car_im)

        u = u_ref[...]
        dy = (d1_ref[...] + d2_ref[...]) * _gelu_grad(y_ref[...])
        dyb = dy.astype(BF16)
        gr = jnp.dot(dyb, ctr_ref[...], preferred_element_type=F32)
        gi = -jnp.dot(dyb, cti_ref[...], preferred_element_type=F32)
        lr, li = _lti_scan(gr, gi, ar, nai, True)
        cr, ci = car_re[...], car_im[...]
        qr, qi = pw_re[...], pw_im[...]
        lr = lr + qr * cr - qi * ci
        li = li + qr * ci + qi * cr
        car_re[...] = lr[0:1, :]
        car_im[...] = li[0:1, :]

        s_re, s_im = sr_ref[...], si_ref[...]
        has_prev = (chunk > 0).astype(F32)
        prev_re = pr_ref[7:8, :] * has_prev
        prev_im = pi_ref[7:8, :] * has_prev
        sp_re = jnp.where(row == 0, prev_re, pltpu.roll(s_re, 1, 0))
        sp_im = jnp.where(row == 0, prev_im, pltpu.roll(s_im, 1, 0))
        p_ar = _rowsum(lr * sp_re + li * sp_im)
        p_ai = _rowsum(li * sp_re - lr * sp_im)

        lrb, lib, ub = lr.astype(BF16), li.astype(BF16), u.astype(BF16)
        du = (dy * d_ref[...] + jnp.dot(lrb, btr_ref[...], preferred_element_type=F32)
              + jnp.dot(lib, bti_ref[...], preferred_element_type=F32))
        du_ref[...] = du.astype(BF16)
        tdot = lambda p, q: lax.dot_general(p, q, (((0,), (0,)), ((), ())), preferred_element_type=F32)
        p_br, p_bi = tdot(ub, lrb), tdot(ub, lib)
        p_cr, p_ci = tdot(s_re.astype(BF16), dyb), -tdot(s_im.astype(BF16), dyb)
        p_dd = _rowsum(dy * u)

        @pl.when(i == 0)
        def _():
            dar_ref[...] = p_ar
            dai_ref[...] = p_ai
            dbr_ref[...] = p_br
            dbi_ref[...] = p_bi
            dcr_ref[...] = p_cr
            dci_ref[...] = p_ci
            dd_ref[...] = p_dd

        @pl.when(i > 0)
        def _():
            dar_ref[...] += p_ar
            dai_ref[...] += p_ai
            dbr_ref[...] += p_br
            dbi_ref[...] += p_bi
            dcr_ref[...] += p_cr
            dci_ref[...] += p_ci
            dd_ref[...] += p_dd

    rev = lambda k, i: (nchunk - 1 - i, k)
    prev = lambda k, i: (jnp.maximum((nchunk - 1 - i) * tail - 1, 0), k)
    kspec = lambda shape: pl.BlockSpec((None,) + shape, lambda k, i: (k, 0, 0))
    return pl.pallas_call(
        body, name=name, grid=(NK, nchunk),
        in_specs=[pl.BlockSpec((T, C), rev), pl.BlockSpec((T, C), rev), pl.BlockSpec((T, C), rev),
                  pl.BlockSpec((T, C), rev), pl.BlockSpec((T, S), rev), pl.BlockSpec((T, S), rev),
                  pl.BlockSpec((8, S), prev), pl.BlockSpec((8, S), prev),
                  kspec((S, C)), kspec((S, C)), kspec((C, S)), kspec((C, S)),
                  kspec((1, S)), kspec((1, S)), kspec((1, C))],
        out_specs=[pl.BlockSpec((T, C), rev), kspec((C, S)), kspec((C, S)), kspec((S, C)), kspec((S, C)),
                   kspec((1, S)), kspec((1, S)), kspec((1, C))],
        out_shape=[jax.ShapeDtypeStruct((L, NK * C), BF16),
                   jax.ShapeDtypeStruct((NK, C, S), F32), jax.ShapeDtypeStruct((NK, C, S), F32),
                   jax.ShapeDtypeStruct((NK, S, C), F32), jax.ShapeDtypeStruct((NK, S, C), F32),
                   jax.ShapeDtypeStruct((NK, 1, S), F32), jax.ShapeDtypeStruct((NK, 1, S), F32),
                   jax.ShapeDtypeStruct((NK, 1, C), F32)],
        scratch_shapes=[pltpu.VMEM((1, S), F32), pltpu.VMEM((1, S), F32),
                        pltpu.VMEM((T, S), F32), pltpu.VMEM((T, S), F32)],
        compiler_params=_params(("parallel", "arbitrary")),
    )(z, y0, dd_direct, dd_mm, s_re, s_im, s_re, s_im, bst_re, bst_im, cst_re, cst_im, a_re, a_im, d)


def _block_diag(v):
    NK, SG, R, Q = v.shape
    eye = jnp.eye(SG, dtype=v.dtype)
    return (v[:, :, :, None, :] * eye[None, :, None, :, None]).reshape(NK, SG * R, SG * Q)


def _block_diag_part(m, SG):
    NK, RR, QQ = m.shape
    R, Q = RR // SG, QQ // SG
    eye = jnp.eye(SG, dtype=m.dtype)
    return jnp.sum(m.reshape(NK, SG, R, SG, Q) * eye[None, :, None, :, None], axis=3)


def _position():
    return lax.axis_index("x"), lax.axis_index("y"), lax.axis_index("c")


def _other_chips(x, y):
    return [(1 - x, y), (x, 1 - y), (1 - x, 1 - y)]


def _gather_phases(n):
    def parts(ins, outs, sems):
        send_sems, recv_sems, local_sems = sems
        x, y, c = _position()
        me, sibling = (x, y, c), (x, y, 1 - c)
        chips = _other_chips(x, y)

        def block(a, pos):
            return outs[a].at[4 * pos[0] + 2 * pos[1] + pos[2]]

        def copy(a, k, pos, to, src=None):
            return pltpu.make_async_remote_copy(
                src_ref=block(a, pos) if src is None else src, dst_ref=block(a, pos),
                send_sem=send_sems.at[7 * a + k], recv_sem=recv_sems.at[7 * a + k],
                device_id=to, device_id_type=MESH)

        mine = [pltpu.make_async_copy(ins[a], block(a, me), local_sems.at[a]) for a in range(n)]
        first = []
        for a in range(n):
            first.append(copy(a, 0, me, sibling, src=ins[a]))
            first += [copy(a, 1 + j, me, (*chip, c), src=ins[a]) for j, chip in enumerate(chips)]
        passed = [copy(a, 4 + j, (*chip, c), sibling) for a in range(n) for j, chip in enumerate(chips)]
        arrived = [copy(a, 1 + j, (*chip, c), me) for a in range(n) for j, chip in enumerate(chips)]
        from_sibling = []
        for a in range(n):
            from_sibling.append(copy(a, 0, sibling, me))
            from_sibling += [copy(a, 4 + j, (*chip, 1 - c), me) for j, chip in enumerate(chips)]
        return mine, first, passed, arrived, from_sibling

    def send(ins, outs, sems):
        mine, first, _, _, _ = parts(ins, outs, sems)
        for cp in mine + first:
            cp.start()

    def forward(ins, outs, sems):
        _, _, passed, arrived, _ = parts(ins, outs, sems)
        for got, fwd in zip(arrived, passed):
            got.wait_recv()
            fwd.start()

    def finish(ins, outs, sems):
        mine, first, passed, _, from_sibling = parts(ins, outs, sems)
        for cp in from_sibling:
            cp.wait_recv()
        for cp in first + passed:
            cp.wait_send()
        for cp in mine:
            cp.wait()

    return [(0.0, send), (GATHER_FORWARD_AT, forward), (1.0, finish)]


def _gather_shapes(shards):
    n = len(shards)
    return ([jax.ShapeDtypeStruct((N_DEV,) + s.shape, s.dtype) for s in shards],
            [pltpu.SemaphoreType.DMA((7 * n,)), pltpu.SemaphoreType.DMA((7 * n,)), pltpu.SemaphoreType.DMA((n,))])


def gather_background(shards):
    out_shapes, scratch = _gather_shapes(shards)
    return Background(shards, out_shapes, scratch, _gather_phases(len(shards)))


def all_gather_blocks(shards, *, name):
    n = len(shards)
    out_shapes, scratch = _gather_shapes(shards)

    def body(*refs):
        for _, phase in _gather_phases(n):
            phase(refs[:n], refs[n:2 * n], refs[2 * n:])

    return pl.pallas_call(
        body, name=name, in_specs=[ANY] * n, out_specs=[ANY] * n, out_shape=out_shapes, scratch_shapes=scratch,
    )(*shards)


def sibling_exchange(grads, *, name):
    n = len(grads)

    def body(*refs):
        ins, outs = refs[:n], refs[n:2 * n]
        send_sems, recv_sems = refs[2 * n:]
        x, y, c = _position()
        copies = []
        for a in range(n):
            for q in range(4):
                copies.append(pltpu.make_async_remote_copy(
                    src_ref=ins[a].at[2 * q + 1 - c], dst_ref=outs[a].at[q],
                    send_sem=send_sems.at[4 * a + q], recv_sem=recv_sems.at[4 * a + q],
                    device_id=(x, y, 1 - c), device_id_type=MESH))
        for cp in copies:
            cp.start()
        for cp in copies:
            cp.wait()

    return pl.pallas_call(
        body, name=name,
        in_specs=[ANY] * n, out_specs=[ANY] * n,
        out_shape=[jax.ShapeDtypeStruct((4,) + g.shape[1:], g.dtype) for g in grads],
        scratch_shapes=[pltpu.SemaphoreType.DMA((4 * n,)), pltpu.SemaphoreType.DMA((4 * n,))],
    )(*grads)


def _chip_exchange_phases(n):
    def copies(ins, outs, sems):
        x, y, c = _position()
        return [pltpu.make_async_remote_copy(
            src_ref=ins[a].at[2 * chip[0] + chip[1]], dst_ref=outs[a].at[j],
            send_sem=sems[0].at[3 * a + j], recv_sem=sems[1].at[3 * a + j],
            device_id=(*chip, c), device_id_type=MESH)
            for a in range(n) for j, chip in enumerate(_other_chips(x, y))]

    def send(ins, outs, sems):
        for cp in copies(ins, outs, sems):
            cp.start()

    def finish(ins, outs, sems):
        for cp in copies(ins, outs, sems):
            cp.wait()

    return [(0.0, send), (1.0, finish)]


def chip_exchange_background(parts):
    n = len(parts)
    return Background(parts, [jax.ShapeDtypeStruct((3,) + p.shape[1:], p.dtype) for p in parts],
                      [pltpu.SemaphoreType.DMA((3 * n,)), pltpu.SemaphoreType.DMA((3 * n,))],
                      _chip_exchange_phases(n))


def add_pairs(grads, theirs, core, *, name, tm=512):
    _, R, C = theirs.shape
    tm = _tile(R, tm, 16)

    def body(core_ref, a_ref, b_ref, o_ref):
        o_ref[...] = (a_ref[...].astype(F32) + b_ref[...].astype(F32)).astype(o_ref.dtype)

    spec = pl.BlockSpec((None, tm, C), lambda q, i, core_ref: (q, i, 0))
    return pl.pallas_call(
        body, name=name,
        grid_spec=pltpu.PrefetchScalarGridSpec(
            num_scalar_prefetch=1, grid=(4, R // tm),
            in_specs=[pl.BlockSpec((None, tm, C), lambda q, i, core_ref: (2 * q + core_ref[0], i, 0)), spec],
            out_specs=spec),
        out_shape=jax.ShapeDtypeStruct(theirs.shape, BF16),
        compiler_params=_params(("parallel", "parallel")),
    )(core, grads, theirs)


def _adamw(w, g, m, v):
    m = ADAM_B1 * m + (1.0 - ADAM_B1) * g
    v = ADAM_B2 * v + (1.0 - ADAM_B2) * (g * g)
    m_hat = m / (1.0 - ADAM_B1 ** ADAM_STEP)
    v_hat = v / (1.0 - ADAM_B2 ** ADAM_STEP)
    delta = -ADAM_LR * (m_hat / (jnp.sqrt(v_hat) + ADAM_EPS) + ADAM_WD * w)
    return delta, m, v


def adamw_sharded(w, m, v, grads, theirs, others, where, *, name, tm=256):
    R, C = w.shape
    tm = _tile(R, tm, 16)

    def body(where_ref, w_ref, m_ref, v_ref, a_ref, b_ref, o_ref, g_ref, d_ref, nm_ref, nv_ref):
        g = a_ref[...].astype(F32) + b_ref[...].astype(F32)
        for j in range(3):
            g = g + o_ref[j].astype(F32)
        g_ref[...] = g
        d_ref[...], nm_ref[...], nv_ref[...] = _adamw(w_ref[...], g, m_ref[...], v_ref[...])

    spec = pl.BlockSpec((tm, C), lambda i, where_ref: (i, 0))
    sd = jax.ShapeDtypeStruct((R, C), F32)
    return pl.pallas_call(
        body, name=name,
        grid_spec=pltpu.PrefetchScalarGridSpec(
            num_scalar_prefetch=1, grid=(R // tm,),
            in_specs=[spec, spec, spec,
                      pl.BlockSpec((None, tm, C), lambda i, where_ref: (where_ref[0], i, 0)),
                      pl.BlockSpec((None, tm, C), lambda i, where_ref: (where_ref[1], i, 0)),
                      pl.BlockSpec((3, tm, C), lambda i, where_ref: (0, i, 0))],
            out_specs=[spec, spec, spec, spec]),
        out_shape=[sd, sd, sd, sd],
        compiler_params=_params(("parallel",)),
    )(where, w, m, v, grads, theirs, others)


def adamw_replicated(ws, ms, vs, gathered, *, name):
    n = len(ws)

    def body(*refs):
        w_refs, m_refs, v_refs, ga_refs = refs[:n], refs[n:2 * n], refs[2 * n:3 * n], refs[3 * n:4 * n]
        outs = refs[4 * n:]
        for k in range(n):
            g = ga_refs[k][0]
            for dev in range(1, N_DEV):
                g = g + ga_refs[k][dev]
            outs[k][...] = g
            outs[n + k][...], outs[2 * n + k][...], outs[3 * n + k][...] = _adamw(
                w_refs[k][...], g, m_refs[k][...], v_refs[k][...])

    shapes = [jax.ShapeDtypeStruct(t.shape, F32) for t in ws]
    res = pl.pallas_call(body, name=name, out_shape=shapes * 4, compiler_params=_params())(*ws, *ms, *vs, *gathered)
    return res[:n], res[n:2 * n], res[2 * n:3 * n], res[3 * n:]


SHARDED = ("w_in", "ssm_glu_w", "w_out", "w_ffn_in", "w_ffn_out", "w_ple_gate", "w_ple_proj")
SMALL_LAST = ("norm_mix_g",)
SMALL_WIDE = ("ssm_b_re", "ssm_b_im", "ssm_c_re", "ssm_c_im")
SMALL = ("ssm_lambda_re", "ssm_lambda_im", "ssm_log_step", "ssm_b_re", "ssm_b_im", "ssm_c_re",
         "ssm_c_im", "ssm_d", "ssm_glu_b", "sgu_ln_g", "sgu_ln_b", "sgu_w", "sgu_b", "out_norm_ssm_g",
         "out_norm_sgu_g", "norm_ffn_g", "norm_ple_g", "b_ple_gate", "final_norm_g")
WEIGHTS = ("norm_mix_g", "w_in", "ssm_lambda_re", "ssm_lambda_im", "ssm_log_step", "ssm_b_re", "ssm_b_im",
           "ssm_c_re", "ssm_c_im", "ssm_d", "ssm_glu_w", "ssm_glu_b", "sgu_ln_g", "sgu_ln_b", "sgu_w", "sgu_b",
           "out_norm_ssm_g", "out_norm_sgu_g", "w_out", "norm_ffn_g", "w_ffn_in", "w_ffn_out", "norm_ple_g",
           "w_ple_gate", "b_ple_gate", "w_ple_proj", "final_norm_g")


def _step(x, p, loss_target, w, m, v):
    L, D = x.shape[1], x.shape[2]
    x2d, p2d, tgt = x.reshape(L, D), p.reshape(L, -1), loss_target.reshape(L, D)
    d_ssm = w["ssm_glu_w"].shape[2]
    d_sgu = w["sgu_ln_g"].shape[1]
    G, P, H = w["ssm_b_re"].shape[1:]
    SG = min(SSM_SUPER, G)
    NK = G // SG
    row = lambda a: a.reshape(1, -1)

    shard2d = {n: w[n].reshape(w[n].shape[1:]) for n in SHARDED}
    shard_bf = {n: shard2d[n].astype(BF16) for n in SHARDED}
    (w_ple_blk,) = all_gather_blocks([shard_bf["w_ple_proj"]], name="gather_w_ple")
    bf = lambda t: t.astype(BF16)
    pp, (w_in_blk,) = mm_nn(bf(p2d), w_ple_blk, name="ple_proj", out_dtype=F32, tm=512, tn=512, tk=2048,
                            bg=gather_background([shard_bf["w_in"]]))
    w_in = jnp.transpose(w_in_blk, (1, 0, 2)).reshape(D, -1)
    F = shard2d["w_ffn_in"].shape[1] * 4

    lam_re, lam_im, log_step = w["ssm_lambda_re"][0], w["ssm_lambda_im"][0], w["ssm_log_step"][0].reshape(G, 1)
    a_re, a_im, q_re, q_im = disc_lambda_fwd(lam_re, lam_im, log_step, name="s5_discretise_lambda")
    bt_re = w["ssm_b_re"][0].transpose(2, 0, 1).reshape(H, G * P)
    bt_im = w["ssm_b_im"][0].transpose(2, 0, 1).reshape(H, G * P)
    bbar_re, bbar_im = disc_b_fwd(row(q_re), row(q_im), bt_re, bt_im, name="s5_discretise_b")
    to_bs = lambda t: _block_diag(t.reshape(H, NK, SG, P).transpose(1, 2, 0, 3))
    to_cs = lambda t: _block_diag(t.reshape(NK, SG, H, P).transpose(0, 1, 3, 2))
    bs_re, bs_im = to_bs(bbar_re), to_bs(bbar_im)
    cs_re, cs_im = to_cs(w["ssm_c_re"][0]), to_cs(w["ssm_c_im"][0])
    a_re_k, a_im_k = a_re.reshape(NK, 1, SG * P), a_im.reshape(NK, 1, SG * P)
    d_k = w["ssm_d"][0].reshape(NK, 1, SG * H)
    bf = lambda t: t.astype(BF16)
    tr = lambda t: jnp.swapaxes(t, 1, 2)

    h1 = norm_fwd(x2d, w["norm_mix_g"], name="norm_mix")
    z, (w_glu, w_out) = mm_nn(h1, w_in, name="in_proj", out_dtype=F32, tm=512, tn=1024, tk=2048,
                              bg=gather_background([shard_bf["ssm_glu_w"], shard_bf["w_out"]]))
    w_glu, w_out = w_glu.reshape(d_ssm, d_ssm), w_out.reshape(D, D)
    (y0, s_re, s_im), (w_ffn_in_blk,) = ssm_fwd(
        z, bf(bs_re), bf(bs_im), bf(cs_re), bf(cs_im), a_re_k, a_im_k, d_k, name="s5_scan",
        bg=gather_background([shard_bf["w_ffn_in"]]))
    ya1 = glu_pre(y0, name="s5_gelu")
    t_glu = mm_nn(ya1, w_glu, name="s5_glu_proj", out_dtype=F32, tm=512, tn=512, tk=2048)
    n_a = glu_post(y0, t_glu, w["ssm_glu_b"], w["out_norm_ssm_g"], name="s5_glu_norm")
    b_s3 = w["sgu_b"][0][:, :, None]
    n_b = sgu_fwd(z, w["sgu_ln_g"], w["sgu_ln_b"], w["sgu_w"][0], b_s3, w["out_norm_sgu_g"], name="sgu", d_sgu=d_sgu)
    ycat = jnp.concatenate([n_a, n_b], axis=1)
    x1 = mm_nn(ycat, w_out, name="out_proj", out_dtype=F32, tm=512, tn=512, tk=2048, residual=x2d)
    h2 = norm_fwd(x1, w["norm_ffn_g"], name="norm_ffn")
    (act, gate_ff, up_ff), (w_ffn_out, w_gate) = ffn_in_swiglu(
        h2, w_ffn_in_blk, name="ffn_in_swiglu", tm=256,
        bg=gather_background([shard_bf["w_ffn_out"], shard_bf["w_ple_gate"]]))
    w_ffn_out, w_gate = w_ffn_out.reshape(F, D), w_gate.reshape(D, D)
    x2 = mm_nn(act, w_ffn_out, name="ffn_out", out_dtype=F32, tm=512, tn=512, tk=F, residual=x1)
    h3 = norm_fwd(x2, w["norm_ple_g"], name="norm_ple")
    gpre = mm_nn(h3, w_gate, name="ple_gate", out_dtype=F32, tm=512, tn=1024, tk=2048)

    dx3, dpre, dpp, loss_part, d_final_g, d_b_gate = head_and_loss(
        x2, gpre, w["b_ple_gate"], pp, row(w["final_norm_g"]), tgt, name="head_and_loss")
    x_pos, y_pos, c_pos = _position()
    where = jnp.stack([4 * x_pos + 2 * y_pos + c_pos, 2 * x_pos + y_pos]).astype(jnp.int32)
    core = jnp.reshape(c_pos, (1,)).astype(jnp.int32)
    own, others = {}, {}

    def chip_partials(named, tag):
        names = list(named)
        g8 = [named[n].reshape((N_DEV,) + shard2d[n].shape) for n in names]
        theirs = sibling_exchange(g8, name="grads_to_sibling_" + tag)
        own.update(zip(names, zip(g8, theirs)))
        return names, chip_exchange_background(
            [add_pairs(g, t, core, name="chip_sum_" + n) for n, g, t in zip(names, g8, theirs)])

    d_w_gate = mm_tn(h3, dpre, name="d_w_ple_gate", out_dtype=BF16, tm=L, tko=1024, tno=1024)
    d_w_ple = mm_tn(bf(p2d), dpp, name="d_w_ple_proj", out_dtype=BF16, tm=L, tko=1024, tno=1024, out_blocks=N_DEV)
    names, bg = chip_partials({"w_ple_gate": d_w_gate, "w_ple_proj": d_w_ple}, "ple")
    dh3, got = mm_nt(dpre, w_gate, name="d_h_ple", out_dtype=F32, tm=512, tko=1024, tc=2048, bg=bg)
    others.update(zip(names, got))
    dx2, dx2b, d_ple_g = norm_bwd(dh3, x2, w["norm_ple_g"], dx3, name="d_norm_ple", want_bf16=True)
    d_w_ffn_out = mm_tn(act, dx2b, name="d_w_ffn_out", out_dtype=BF16, tm=L, tko=1408, tno=512)
    names, bg = chip_partials({"w_ffn_out": d_w_ffn_out}, "ffn_out")
    dgate, dup = ffn_out_bwd_swiglu(dx2b, w_ffn_out, gate_ff, up_ff, name="d_act_swiglu")
    half = N_DEV // 2
    d_w_ffn_in, got = mm_tn(h2, dgate, name="d_w_ffn_in_gate", out_dtype=BF16, tm=L, tko=512, tno=1408,
                            out_blocks=half, total_blocks=N_DEV, bg=bg)
    others.update(zip(names, got))
    d_w_ffn_in = mm_tn(h2, dup, name="d_w_ffn_in_up", out_dtype=BF16, tm=L, tko=512, tno=1408,
                       out_blocks=half, block_offset=half, total_blocks=N_DEV, into=d_w_ffn_in)
    names, bg = chip_partials({"w_ffn_in": d_w_ffn_in}, "ffn_in")
    dh2, got = mm_nt(dgate, w_ffn_in_blk, a2=dup, name="d_h_ffn", out_dtype=F32, tm=1024, tko=1024, tc=1408, bg=bg)
    others.update(zip(names, got))
    dx1, dx1b, d_ffn_g = norm_bwd(dh2, x1, w["norm_ffn_g"], dx2, name="d_norm_ffn", want_bf16=True)
    dycat = mm_nt(dx1b, w_out, name="d_ycat", out_dtype=F32, tm=512, tko=1024, tc=2048)
    d_w_out = mm_tn(ycat, dx1b, name="d_w_out", out_dtype=BF16, tm=L, tko=1024, tno=1024)
    dzu, dzv, d_sgu_w, d_sgu_b, d_ln_g, d_ln_b, d_g_b = sgu_bwd(
        z, dycat, w["sgu_ln_g"], w["sgu_ln_b"], w["sgu_w"][0], b_s3, w["out_norm_sgu_g"], name="d_sgu", d_sgu=d_sgu)
    dt_glu, dd_direct, d_g_a, d_glu_b = glu_post_bwd(
        y0, t_glu, w["ssm_glu_b"], w["out_norm_ssm_g"], dycat, name="d_s5_glu_norm")
    d_w_glu = mm_tn(ya1, dt_glu, name="d_w_glu", out_dtype=BF16, tm=L, tko=1024, tno=1024)
    names, bg = chip_partials({"w_out": d_w_out, "ssm_glu_w": d_w_glu}, "out_glu")
    dd_mm = mm_nt(dt_glu, w_glu, name="d_s5_glu_proj", out_dtype=F32, tm=512, tko=1024, tc=2048)
    du, d_bs_re, d_bs_im, d_cs_re, d_cs_im, d_a_re, d_a_im, d_d = ssm_bwd(
        z, y0, dd_direct, dd_mm, s_re, s_im, bf(tr(bs_re)), bf(tr(bs_im)), bf(tr(cs_re)), bf(tr(cs_im)),
        a_re_k, a_im_k, d_k, name="d_s5_scan")
    dz = jnp.concatenate([du, dzu, dzv], axis=1)
    d_w_in, got = mm_tn(h1, dz, name="d_w_in", out_dtype=BF16, tm=L, tko=1024, tno=1024, out_blocks=N_DEV, bg=bg)
    others.update(zip(names, got))

    from_bs = lambda t: _block_diag_part(t, SG).transpose(2, 0, 1, 3).reshape(H, G * P)
    from_cs = lambda t: _block_diag_part(t, SG).transpose(0, 1, 3, 2).reshape(1, G, H, P)
    d_q_re, d_q_im, d_bt_re, d_bt_im = disc_b_bwd(row(q_re), row(q_im), bt_re, bt_im, from_bs(d_bs_re),
                                                  from_bs(d_bs_im), name="d_s5_discretise_b")
    d_lam_re, d_lam_im, d_log_step = disc_lambda_bwd(
        lam_re, lam_im, log_step,
        (d_a_re.reshape(G, P), d_a_im.reshape(G, P), d_q_re.reshape(G, P), d_q_im.reshape(G, P)),
        name="d_s5_discretise_lambda")
    from_bt = lambda t: t.reshape(H, G, P).transpose(1, 2, 0).reshape(1, G, P, H)
    small_grads = {
        "ssm_lambda_re": d_lam_re, "ssm_lambda_im": d_lam_im, "ssm_log_step": d_log_step,
        "ssm_b_re": from_bt(d_bt_re), "ssm_b_im": from_bt(d_bt_im), "ssm_c_re": from_cs(d_cs_re),
        "ssm_c_im": from_cs(d_cs_im), "ssm_d": d_d, "ssm_glu_b": d_glu_b, "sgu_ln_g": d_ln_g, "sgu_ln_b": d_ln_b,
        "sgu_w": d_sgu_w, "sgu_b": d_sgu_b, "out_norm_ssm_g": d_g_a, "out_norm_sgu_g": d_g_b,
        "norm_ffn_g": d_ffn_g, "norm_ple_g": d_ple_g, "b_ple_gate": d_b_gate, "final_norm_g": d_final_g,
    }

    def work_shape(n):
        s = w[n].shape
        return (1,) + s if len(s) == 1 else (s if len(s) == 2 else s[1:])

    names, bg = chip_partials({"w_in": d_w_in}, "w_in")
    bg_small = gather_background([loss_part] + [small_grads[n].reshape(work_shape(n)) for n in SMALL])
    dh1, got = mm_nt(dz, w_in, name="d_h_mix", out_dtype=F32, tm=512, tko=1024, tc=3 * d_sgu,
                     bg=combine_backgrounds(bg, bg_small))
    others.update(zip(names, got[:len(names)]))
    all_loss, all_small = got[len(names)], dict(zip(SMALL, got[len(names) + 1:]))
    grad_x, d_mix_g = norm_bwd(dh1, x2d, w["norm_mix_g"], dx1, name="d_norm_mix", want_bf16=False)

    out = {}
    for n in SHARDED:
        res = adamw_sharded(shard2d[n], m[n].reshape(shard2d[n].shape), v[n].reshape(shard2d[n].shape),
                            own[n][0], own[n][1], others[n], where, name="adamw_" + n)
        out[n] = [r.reshape(w[n].shape) for r in res]

    def replicated(names_, gathered, name, with_loss=None):
        zero = [jnp.zeros((8, LANE), F32)] if with_loss is not None else []
        res = adamw_replicated(*[zero + [t[n].reshape(work_shape(n)) for n in names_] for t in (w, m, v)],
                               ([with_loss] if with_loss is not None else []) + list(gathered), name=name)
        skip = len(zero)
        for i, n in enumerate(names_):
            out[n] = [r[skip + i].reshape(w[n].shape) for r in res]
        return res[0][0]

    loss = replicated([n for n in SMALL if n not in SMALL_WIDE], [all_small[n] for n in SMALL if n not in SMALL_WIDE],
                      "adamw_small", with_loss=all_loss)[0, 0]
    for n in SMALL_WIDE:
        replicated([n], [all_small[n]], "adamw_" + n)
    replicated(list(SMALL_LAST), all_gather_blocks([d_mix_g], name="gather_last_grad"), "adamw_last")

    grads = [out[n][0] for n in WEIGHTS]
    deltas = [out[n][1] for n in WEIGHTS]
    new_m = [out[n][2] for n in WEIGHTS]
    new_v = [out[n][3] for n in WEIGHTS]
    return (loss, grad_x.reshape(x.shape), *grads, *deltas, *new_m, *new_v)


def kernel(x, p, norm_mix_g, w_in, ssm_lambda_re, ssm_lambda_im, ssm_log_step, ssm_b_re, ssm_b_im, ssm_c_re, ssm_c_im, ssm_d, ssm_glu_w, ssm_glu_b, sgu_ln_g, sgu_ln_b, sgu_w, sgu_b, out_norm_ssm_g, out_norm_sgu_g, w_out, norm_ffn_g, w_ffn_in, w_ffn_out, norm_ple_g, w_ple_gate, b_ple_gate, w_ple_proj, final_norm_g, loss_target, m_norm_mix_g, m_w_in, m_ssm_lambda_re, m_ssm_lambda_im, m_ssm_log_step, m_ssm_b_re, m_ssm_b_im, m_ssm_c_re, m_ssm_c_im, m_ssm_d, m_ssm_glu_w, m_ssm_glu_b, m_sgu_ln_g, m_sgu_ln_b, m_sgu_w, m_sgu_b, m_out_norm_ssm_g, m_out_norm_sgu_g, m_w_out, m_norm_ffn_g, m_w_ffn_in, m_w_ffn_out, m_norm_ple_g, m_w_ple_gate, m_b_ple_gate, m_w_ple_proj, m_final_norm_g, v_norm_mix_g, v_w_in, v_ssm_lambda_re, v_ssm_lambda_im, v_ssm_log_step, v_ssm_b_re, v_ssm_b_im, v_ssm_c_re, v_ssm_c_im, v_ssm_d, v_ssm_glu_w, v_ssm_glu_b, v_sgu_ln_g, v_sgu_ln_b, v_sgu_w, v_sgu_b, v_out_norm_ssm_g, v_out_norm_sgu_g, v_w_out, v_norm_ffn_g, v_w_ffn_in, v_w_ffn_out, v_norm_ple_g, v_w_ple_gate, v_b_ple_gate, v_w_ple_proj, v_final_norm_g):
    given = dict(locals())
    w = {n: given[n] for n in WEIGHTS}
    m = {n: given["m_" + n] for n in WEIGHTS}
    v = {n: given["v_" + n] for n in WEIGHTS}
    return _step(x, p, loss_target, w, m, v)
```

```python
import functools
import math

import jax
import jax.numpy as jnp
from jax import lax
from jax.experimental import pallas as pl
from jax.experimental.pallas import tpu as pltpu

F32 = jnp.float32
BF16 = jnp.bfloat16
MESH = pl.DeviceIdType.MESH
ANY = pl.BlockSpec(memory_space=pl.ANY)

N_DEV = 8
EPS = 1e-6
LAMBDA_RE_MAX = -1e-4
SSM_GROUP = 16
SSM_STATE = 64
SSM_SUPER = 16
SGU_CHUNK = 128
ADAM_LR, ADAM_B1, ADAM_B2, ADAM_EPS, ADAM_WD, ADAM_STEP = 0.001, 0.9, 0.999, 1e-08, 0.01, 10
VMEM_LIMIT = 52 * 1024 * 1024
LANE = 128
GATHER_FORWARD_AT = 0.85

_GELU_C = math.sqrt(2.0 / math.pi)


def _params(sem=None):
    return pltpu.CompilerParams(dimension_semantics=sem, vmem_limit_bytes=VMEM_LIMIT)


def _tile(dim, pref, unit=LANE):
    if dim <= pref:
        return dim
    t = (pref // unit) * unit
    while t >= unit:
        if dim % t == 0:
            return t
        t -= unit
    return dim


def _gelu(x):
    return 0.5 * x * (1.0 + jnp.tanh(_GELU_C * (x + 0.044715 * x * x * x)))


def _gelu_grad(x):
    t = jnp.tanh(_GELU_C * (x + 0.044715 * x * x * x))
    return 0.5 * (1.0 + t) + 0.5 * x * (1.0 - t * t) * (_GELU_C * (1.0 + 3.0 * 0.044715 * x * x))


def _rms(x):
    return lax.rsqrt(jnp.mean(x * x, axis=-1, keepdims=True) + EPS)


def _rmsnorm_bwd(dy, x, r, g):
    dyg = dy * g
    return r * dyg - x * (r * r * r) * jnp.mean(dyg * x, axis=-1, keepdims=True)


def _rowsum(v):
    return jnp.sum(v, axis=0, keepdims=True)


class Background:
    def __init__(self, inputs, out_shapes, scratch, phases):
        self.inputs, self.out_shapes, self.scratch, self.phases = list(inputs), list(out_shapes), list(scratch), phases

    def emit(self, step, nsteps, ins, outs, scratch):
        for place, phase in self.phases:
            at = min(int(place * nsteps), nsteps - 1)

            @pl.when(step == at)
            def _():
                phase(ins, outs, scratch)


def combine_backgrounds(first, second):
    ni, no, ns = len(first.inputs), len(first.out_shapes), len(first.scratch)
    phases = [(place, lambda i, o, s, f=fn: f(i[:ni], o[:no], s[:ns])) for place, fn in first.phases]
    phases += [(place, lambda i, o, s, f=fn: f(i[ni:], o[no:], s[ns:])) for place, fn in second.phases]
    return Background(first.inputs + second.inputs, first.out_shapes + second.out_shapes,
                      first.scratch + second.scratch, sorted(phases, key=lambda p: p[0]))


def _carrier(bg, n_in, n_out, n_scratch, grid):
    nbi = len(bg.inputs) if bg else 0
    nbo = len(bg.out_shapes) if bg else 0
    nsteps = math.prod(grid)

    def split(refs):
        ins = refs[:n_in]
        bg_ins = refs[n_in:n_in + nbi]
        outs = refs[n_in + nbi:n_in + nbi + n_out]
        bg_outs = refs[n_in + nbi + n_out:n_in + nbi + n_out + nbo]
        rest = refs[n_in + nbi + n_out + nbo:]
        scratch, bg_scratch = rest[:n_scratch], rest[n_scratch:]

        def run_background():
            if bg is None:
                return
            step = pl.program_id(0)
            for axis in range(1, len(grid)):
                step = step * grid[axis] + pl.program_id(axis)
            bg.emit(step, nsteps, bg_ins, bg_outs, bg_scratch)

        return ins, outs, scratch, run_background

    if bg is None:
        return [], [], [], [], [], split
    return [ANY] * nbi, list(bg.inputs), [ANY] * nbo, list(bg.out_shapes), list(bg.scratch), split


def _semantics(bg, sem):
    return tuple("arbitrary" for _ in sem) if bg is not None else sem


def _results(res, n_out, bg):
    res = list(res) if isinstance(res, (list, tuple)) else [res]
    own = res[0] if n_out == 1 else res[:n_out]
    return (own, res[n_out:]) if bg is not None else own


def mm_nn(a, b, *, name, out_dtype, tm, tn, tk, residual=None, bg=None):
    M, K = a.shape
    blocked = b.ndim == 3
    if blocked:
        nb, _, Nb = b.shape
        N = nb * Nb
        tn = _tile(Nb, tn)
        per = Nb // tn
    else:
        N = b.shape[1]
        tn = _tile(N, tn)
    tm, tk = _tile(M, tm, 8), _tile(K, tk)
    nj, ni, nk = N // tn, M // tm, K // tk
    has_res = residual is not None
    grid = (nj, ni, nk)
    bg_in_specs, bg_args, bg_out_specs, bg_out_shapes, bg_scratch, split = _carrier(
        bg, 3 if has_res else 2, 1, 0 if nk == 1 else 1, grid)

    def body(*refs):
        ins, (o_ref,), scratch, run_background = split(refs)
        run_background()
        a_ref, b_ref = ins[0], ins[1]
        r_ref = ins[2] if has_res else None

        def finish(acc):
            if has_res:
                acc = acc + r_ref[...]
            o_ref[...] = acc.astype(o_ref.dtype)

        part = jnp.dot(a_ref[...], b_ref[...], preferred_element_type=F32)
        if nk == 1:
            finish(part)
        else:
            acc_ref = scratch[0]
            k = pl.program_id(2)

            @pl.when(k == 0)
            def _():
                acc_ref[...] = part

            @pl.when(k > 0)
            def _():
                acc_ref[...] += part

            @pl.when(k == nk - 1)
            def _():
                finish(acc_ref[...])

    if blocked:
        b_spec = pl.BlockSpec((None, tk, tn), lambda j, i, k: (j // per, k, j % per))
    else:
        b_spec = pl.BlockSpec((tk, tn), lambda j, i, k: (k, j))
    in_specs = [pl.BlockSpec((tm, tk), lambda j, i, k: (i, k)), b_spec]
    args = [a, b]
    if has_res:
        in_specs.append(pl.BlockSpec((tm, tn), lambda j, i, k: (i, j)))
        args.append(residual)
    res = pl.pallas_call(
        body, name=name, grid=grid,
        in_specs=in_specs + bg_in_specs,
        out_specs=[pl.BlockSpec((tm, tn), lambda j, i, k: (i, j))] + bg_out_specs,
        out_shape=[jax.ShapeDtypeStruct((M, N), out_dtype)] + bg_out_shapes,
        scratch_shapes=([] if nk == 1 else [pltpu.VMEM((tm, tn), F32)]) + bg_scratch,
        compiler_params=_params(_semantics(bg, ("parallel", "parallel", "arbitrary"))),
    )(*args, *bg_args)
    return _results(res, 1, bg)


def mm_nt(a, w, *, name, out_dtype, tm, tko, tc, a2=None, bg=None):
    M, N = a.shape
    if a2 is not None:
        N = 2 * N
    blocked = w.ndim == 3
    if blocked:
        nb, Ko, Nb = w.shape
        tc = _tile(Nb, tc)
        per = Nb // tc
    else:
        Ko = w.shape[0]
        tc = _tile(N, tc)
    tm, tko = _tile(M, tm, 8), _tile(Ko, tko)
    njo, ni, nc = Ko // tko, M // tm, N // tc
    grid = (njo, ni, nc)
    half = nc // 2
    bg_in_specs, bg_args, bg_out_specs, bg_out_shapes, bg_scratch, split = _carrier(
        bg, 2 if a2 is None else 3, 1, 0 if nc == 1 else 1, grid)

    def body(*refs):
        ins, (o_ref,), scratch, run_background = split(refs)
        run_background()
        a_val = ins[0][...]
        if a2 is not None:
            a_val = jnp.where(pl.program_id(2) < half, a_val, ins[1][...])
        part = lax.dot_general(a_val, ins[-1][...], (((1,), (1,)), ((), ())),
                               preferred_element_type=F32)
        if nc == 1:
            o_ref[...] = part.astype(o_ref.dtype)
        else:
            acc_ref = scratch[0]
            c = pl.program_id(2)

            @pl.when(c == 0)
            def _():
                acc_ref[...] = part

            @pl.when(c > 0)
            def _():
                acc_ref[...] += part

            @pl.when(c == nc - 1)
            def _():
                o_ref[...] = acc_ref[...].astype(o_ref.dtype)

    if blocked:
        w_spec = pl.BlockSpec((None, tko, tc), lambda j, i, c: (c // per, j, c % per))
    else:
        w_spec = pl.BlockSpec((tko, tc), lambda j, i, c: (j, c))
    if a2 is None:
        a_specs, a_args = [pl.BlockSpec((tm, tc), lambda j, i, c: (i, c))], [a]
    else:
        a_specs = [pl.BlockSpec((tm, tc), lambda j, i, c: (i, jnp.minimum(c, half - 1))),
                   pl.BlockSpec((tm, tc), lambda j, i, c: (i, jnp.maximum(c - half, 0)))]
        a_args = [a, a2]
    res = pl.pallas_call(
        body, name=name, grid=grid,
        in_specs=a_specs + [w_spec] + bg_in_specs,
        out_specs=[pl.BlockSpec((tm, tko), lambda j, i, c: (i, j))] + bg_out_specs,
        out_shape=[jax.ShapeDtypeStruct((M, Ko), out_dtype)] + bg_out_shapes,
        scratch_shapes=([] if nc == 1 else [pltpu.VMEM((tm, tko), F32)]) + bg_scratch,
        compiler_params=_params(_semantics(bg, ("parallel", "parallel", "arbitrary"))),
    )(*a_args, w, *bg_args)
    return _results(res, 1, bg)


def mm_tn(a, g, *, name, out_dtype, tm, tko, tno, out_blocks=None, block_offset=0, total_blocks=None, into=None,
          bg=None):
    M, K = a.shape
    N = g.shape[1]
    if out_blocks:
        Nb = N // out_blocks
        tno = _tile(Nb, tno)
        per = Nb // tno
    else:
        tno = _tile(N, tno)
    tm, tko = _tile(M, tm), _tile(K, tko)
    njo, njn, nm = K // tko, N // tno, M // tm
    grid = (njo, njn, nm)
    bg_in_specs, bg_args, bg_out_specs, bg_out_shapes, bg_scratch, split = _carrier(
        bg, 2 if into is None else 3, 1, 0 if nm == 1 else 1, grid)

    def body(*refs):
        ins, (o_ref,), scratch, run_background = split(refs)
        a_ref, g_ref = ins[0], ins[1]
        run_background()
        part = lax.dot_general(a_ref[...], g_ref[...], (((0,), (0,)), ((), ())),
                               preferred_element_type=F32)
        if nm == 1:
            o_ref[...] = part.astype(o_ref.dtype)
        else:
            acc_ref = scratch[0]
            m = pl.program_id(2)

            @pl.when(m == 0)
            def _():
                acc_ref[...] = part

            @pl.when(m > 0)
            def _():
                acc_ref[...] += part

            @pl.when(m == nm - 1)
            def _():
                o_ref[...] = acc_ref[...].astype(o_ref.dtype)

    if out_blocks:
        o_spec = pl.BlockSpec((None, tko, tno), lambda jo, jn, m: (jn // per + block_offset, jo, jn % per))
        o_shape = jax.ShapeDtypeStruct((total_blocks or out_blocks, K, Nb), out_dtype)
    else:
        o_spec = pl.BlockSpec((tko, tno), lambda jo, jn, m: (jo, jn))
        o_shape = jax.ShapeDtypeStruct((K, N), out_dtype)
    res = pl.pallas_call(
        body, name=name, grid=grid,
        in_specs=[pl.BlockSpec((tm, tko), lambda jo, jn, m: (m, jo)),
                  pl.BlockSpec((tm, tno), lambda jo, jn, m: (m, jn))] + ([] if into is None else [ANY]) + bg_in_specs,
        out_specs=[o_spec] + bg_out_specs, out_shape=[o_shape] + bg_out_shapes,
        scratch_shapes=([] if nm == 1 else [pltpu.VMEM((tko, tno), F32)]) + bg_scratch,
        input_output_aliases={} if into is None else {2: 0},
        compiler_params=_params(_semantics(bg, ("parallel", "parallel", "arbitrary"))),
    )(a, g, *([] if into is None else [into]), *bg_args)
    return _results(res, 1, bg)


def ffn_in_swiglu(h, w_blk, *, name, tm, bg=None):
    M, K = h.shape
    nb, _, Nb = w_blk.shape
    nh = nb // 2
    F = nh * Nb
    tm = _tile(M, tm, 8)
    grid = (nh, M // tm)
    bg_in_specs, bg_args, bg_out_specs, bg_out_shapes, bg_scratch, split = _carrier(bg, 3, 3, 0, grid)

    def body(*refs):
        (h_ref, wg_ref, wu_ref), (act_ref, gate_ref, up_ref), _, run_background = split(refs)
        run_background()
        hv = h_ref[...]
        gate = jnp.dot(hv, wg_ref[...], preferred_element_type=F32)
        up = jnp.dot(hv, wu_ref[...], preferred_element_type=F32)
        gate_ref[...] = gate
        up_ref[...] = up
        act_ref[...] = (gate * jax.nn.sigmoid(gate) * up).astype(act_ref.dtype)

    o_spec = pl.BlockSpec((tm, Nb), lambda j, i: (i, j))
    res = pl.pallas_call(
        body, name=name, grid=grid,
        in_specs=[pl.BlockSpec((tm, K), lambda j, i: (i, 0)),
                  pl.BlockSpec((None, K, Nb), lambda j, i: (j, 0, 0)),
                  pl.BlockSpec((None, K, Nb), lambda j, i: (j + nh, 0, 0))] + bg_in_specs,
        out_specs=[o_spec, o_spec, o_spec] + bg_out_specs,
        out_shape=[jax.ShapeDtypeStruct((M, F), BF16), jax.ShapeDtypeStruct((M, F), F32),
                   jax.ShapeDtypeStruct((M, F), F32)] + bg_out_shapes,
        scratch_shapes=bg_scratch,
        compiler_params=_params(_semantics(bg, ("parallel", "parallel"))),
    )(h, w_blk, w_blk, *bg_args)
    return _results(res, 3, bg)


def _row_spec(tm, d, col=0):
    return pl.BlockSpec((tm, d), lambda i: (i, col))


def _vec_spec(d):
    return pl.BlockSpec((1, d), lambda i: (0, 0))


def norm_fwd(x, g, *, name, tm=256):
    L, D = x.shape
    tm = _tile(L, tm, 8)

    def body(x_ref, g_ref, h_ref):
        xv = x_ref[...]
        h_ref[...] = (xv * _rms(xv) * g_ref[...]).astype(h_ref.dtype)

    return pl.pallas_call(
        body, name=name, grid=(L // tm,),
        in_specs=[_row_spec(tm, D), _vec_spec(D)],
        out_specs=_row_spec(tm, D),
        out_shape=jax.ShapeDtypeStruct((L, D), BF16),
        compiler_params=_params(("parallel",)),
    )(x, g)


def norm_bwd(dh, xin, g, dres, *, name, want_bf16, tm=128):
    L, D = xin.shape
    tm = _tile(L, tm, 8)

    def body(dh_ref, x_ref, g_ref, dres_ref, dx_ref, *rest):
        dg_ref = rest[-1]
        xv, dhv = x_ref[...], dh_ref[...]
        r = _rms(xv)
        dx = dres_ref[...] + _rmsnorm_bwd(dhv, xv, r, g_ref[...])
        dx_ref[...] = dx
        if want_bf16:
            rest[0][...] = dx.astype(BF16)
        part = _rowsum(dhv * xv * r)

        @pl.when(pl.program_id(0) == 0)
        def _():
            dg_ref[...] = part

        @pl.when(pl.program_id(0) > 0)
        def _():
            dg_ref[...] += part

    out_specs = [_row_spec(tm, D)] + ([_row_spec(tm, D)] if want_bf16 else []) + [_vec_spec(D)]
    out_shape = ([jax.ShapeDtypeStruct((L, D), F32)]
                 + ([jax.ShapeDtypeStruct((L, D), BF16)] if want_bf16 else [])
                 + [jax.ShapeDtypeStruct((1, D), F32)])
    return pl.pallas_call(
        body, name=name, grid=(L // tm,),
        in_specs=[_row_spec(tm, D), _row_spec(tm, D), _vec_spec(D), _row_spec(tm, D)],
        out_specs=out_specs, out_shape=out_shape,
        compiler_params=_params(("arbitrary",)),
    )(dh, xin, g, dres)


def glu_pre(y0, *, name, tm=256):
    L, D = y0.shape
    tm = _tile(L, tm, 8)

    def body(y_ref, o_ref):
        o_ref[...] = _gelu(y_ref[...]).astype(o_ref.dtype)

    return pl.pallas_call(
        body, name=name, grid=(L // tm,),
        in_specs=[_row_spec(tm, D)], out_specs=_row_spec(tm, D),
        out_shape=jax.ShapeDtypeStruct((L, D), BF16),
        compiler_params=_params(("parallel",)),
    )(y0)


def glu_post(y0, t, b_glu, g_a, *, name, tm=256):
    L, D = y0.shape
    tm = _tile(L, tm, 8)

    def body(y_ref, t_ref, b_ref, g_ref, o_ref):
        ya = _gelu(y_ref[...]) * jax.nn.sigmoid(t_ref[...] + b_ref[...])
        o_ref[...] = (ya * _rms(ya) * g_ref[...]).astype(o_ref.dtype)

    return pl.pallas_call(
        body, name=name, grid=(L // tm,),
        in_specs=[_row_spec(tm, D), _row_spec(tm, D), _vec_spec(D), _vec_spec(D)],
        out_specs=_row_spec(tm, D),
        out_shape=jax.ShapeDtypeStruct((L, D), BF16),
        compiler_params=_params(("parallel",)),
    )(y0, t, b_glu, g_a)


def glu_post_bwd(y0, t, b_glu, g_a, dycat, *, name, tm=128):
    L, D = y0.shape
    tm = _tile(L, tm, 8)

    def body(y_ref, t_ref, b_ref, g_ref, dn_ref, dt_ref, dd_ref, dga_ref, dbg_ref):
        ya1 = _gelu(y_ref[...])
        sg = jax.nn.sigmoid(t_ref[...] + b_ref[...])
        ya = ya1 * sg
        ra = _rms(ya)
        dn = dn_ref[...]
        dya = _rmsnorm_bwd(dn, ya, ra, g_ref[...])
        dt = dya * ya1 * sg * (1.0 - sg)
        dt_ref[...] = dt.astype(BF16)
        dd_ref[...] = dya * sg
        p_ga, p_bg = _rowsum(dn * ya * ra), _rowsum(dt)

        @pl.when(pl.program_id(0) == 0)
        def _():
            dga_ref[...] = p_ga
            dbg_ref[...] = p_bg

        @pl.when(pl.program_id(0) > 0)
        def _():
            dga_ref[...] += p_ga
            dbg_ref[...] += p_bg

    return pl.pallas_call(
        body, name=name, grid=(L // tm,),
        in_specs=[_row_spec(tm, D), _row_spec(tm, D), _vec_spec(D), _vec_spec(D), _row_spec(tm, D, 0)],
        out_specs=[_row_spec(tm, D), _row_spec(tm, D), _vec_spec(D), _vec_spec(D)],
        out_shape=[jax.ShapeDtypeStruct((L, D), BF16), jax.ShapeDtypeStruct((L, D), F32),
                   jax.ShapeDtypeStruct((1, D), F32), jax.ShapeDtypeStruct((1, D), F32)],
        compiler_params=_params(("arbitrary",)),
    )(y0, t, b_glu, g_a, dycat)


def head_and_loss(x2, gpre, b_g, pp, g_f, tgt, *, name, tm=128):
    L, D = x2.shape
    tm = _tile(L, tm, 8)

    def body(x2_ref, gp_ref, bg_ref, pp_ref, gf_ref, tg_ref,
             dx3_ref, dpre_ref, dpp_ref, loss_ref, dgf_ref, dbg_ref):
        gate = jax.nn.sigmoid(gp_ref[...] + bg_ref[...])
        ppv = pp_ref[...]
        x3 = x2_ref[...] + gate * ppv
        r = _rms(x3)
        xn = x3 * r
        gf = gf_ref[...]
        err = xn * gf - tg_ref[...]
        loss = 0.5 * jnp.sum(jnp.mean(err * err, axis=-1, keepdims=True), axis=0, keepdims=True)
        dout = err * (1.0 / D)
        dx3 = _rmsnorm_bwd(dout, x3, r, gf)
        dx3_ref[...] = dx3
        dpre = dx3 * ppv * gate * (1.0 - gate)
        dpre_ref[...] = dpre.astype(BF16)
        dpp_ref[...] = (dx3 * gate).astype(BF16)
        p_gf, p_bg = _rowsum(dout * xn), _rowsum(dpre)
        p_loss = jnp.broadcast_to(loss, loss_ref.shape)

        @pl.when(pl.program_id(0) == 0)
        def _():
            loss_ref[...] = p_loss
            dgf_ref[...] = p_gf
            dbg_ref[...] = p_bg

        @pl.when(pl.program_id(0) > 0)
        def _():
            loss_ref[...] += p_loss
            dgf_ref[...] += p_gf
            dbg_ref[...] += p_bg

    rs = _row_spec(tm, D)
    return pl.pallas_call(
        body, name=name, grid=(L // tm,),
        in_specs=[rs, rs, _vec_spec(D), rs, _vec_spec(D), rs],
        out_specs=[rs, rs, rs, pl.BlockSpec((8, LANE), lambda i: (0, 0)), _vec_spec(D), _vec_spec(D)],
        out_shape=[jax.ShapeDtypeStruct((L, D), F32), jax.ShapeDtypeStruct((L, D), BF16),
                   jax.ShapeDtypeStruct((L, D), BF16), jax.ShapeDtypeStruct((8, LANE), F32),
                   jax.ShapeDtypeStruct((1, D), F32), jax.ShapeDtypeStruct((1, D), F32)],
        compiler_params=_params(("arbitrary",)),
    )(x2, gpre, b_g, pp, g_f, tgt)


def ffn_out_bwd_swiglu(dx, w, gate, up, *, name, tm=512, tf=1408):
    M, D = dx.shape
    F = w.shape[0]
    tm, tf = _tile(M, tm, 8), _tile(F, tf)

    def body(dx_ref, w_ref, g_ref, u_ref, dg_ref, du_ref):
        da = lax.dot_general(dx_ref[...], w_ref[...], (((1,), (1,)), ((), ())), preferred_element_type=F32)
        gv = g_ref[...]
        sg = jax.nn.sigmoid(gv)
        dg_ref[...] = (da * u_ref[...] * sg * (1.0 + gv * (1.0 - sg))).astype(BF16)
        du_ref[...] = (da * gv * sg).astype(BF16)

    spec = pl.BlockSpec((tm, tf), lambda j, i: (i, j))
    return pl.pallas_call(
        body, name=name, grid=(F // tf, M // tm),
        in_specs=[pl.BlockSpec((tm, D), lambda j, i: (i, 0)), pl.BlockSpec((tf, D), lambda j, i: (j, 0)), spec, spec],
        out_specs=[spec, spec],
        out_shape=[jax.ShapeDtypeStruct((M, F), BF16), jax.ShapeDtypeStruct((M, F), BF16)],
        compiler_params=_params(("parallel", "parallel")),
    )(dx, w, gate, up)


def _sgu_forward_values(zu, zv, lng, lnb, w_ref, bs_ref, s_scr, heads, hd):
    u1 = _gelu(zu)
    v1 = _gelu(zv)
    xc = v1 - jnp.mean(v1, axis=-1, keepdims=True)
    r = lax.rsqrt(jnp.mean(xc * xc, axis=-1, keepdims=True) + EPS)
    xhat = xc * r
    v2 = xhat * lng + lnb
    tril = (lax.broadcasted_iota(jnp.int32, (SGU_CHUNK, SGU_CHUNK), 0)
            >= lax.broadcasted_iota(jnp.int32, (SGU_CHUNK, SGU_CHUNK), 1))
    for h in range(heads):
        wm = jnp.where(tril, w_ref[h], 0.0).astype(BF16)
        cols = slice(h * hd, (h + 1) * hd)
        s_scr[:, cols] = jnp.dot(wm, v2[:, cols].astype(BF16), preferred_element_type=F32) + bs_ref[h]
    return u1, xhat, r, v2, tril


def sgu_fwd(z, ln_g, ln_b, w_s, b_s, g_b, *, name, d_sgu):
    L = z.shape[0]
    heads = w_s.shape[0]
    hd = d_sgu // heads

    def body(zu_ref, zv_ref, lng_ref, lnb_ref, w_ref, bs_ref, gb_ref, o_ref, s_scr):
        u1, _, _, _, _ = _sgu_forward_values(zu_ref[...], zv_ref[...], lng_ref[...], lnb_ref[...],
                                             w_ref, bs_ref, s_scr, heads, hd)
        yb = u1 * s_scr[...]
        o_ref[...] = (yb * _rms(yb) * gb_ref[...]).astype(o_ref.dtype)

    blk = lambda col: pl.BlockSpec((SGU_CHUNK, d_sgu), lambda n: (n, col))
    return pl.pallas_call(
        body, name=name, grid=(L // SGU_CHUNK,),
        in_specs=[blk(1), blk(2), _vec_spec(d_sgu), _vec_spec(d_sgu),
                  pl.BlockSpec(w_s.shape, lambda n: (0, 0, 0)), pl.BlockSpec(b_s.shape, lambda n: (0, 0, 0)),
                  _vec_spec(d_sgu)],
        out_specs=blk(0),
        out_shape=jax.ShapeDtypeStruct((L, d_sgu), BF16),
        scratch_shapes=[pltpu.VMEM((SGU_CHUNK, d_sgu), F32)],
        compiler_params=_params(("parallel",)),
    )(z, z, ln_g, ln_b, w_s, b_s, g_b)


def sgu_bwd(z, dycat, ln_g, ln_b, w_s, b_s, g_b, *, name, d_sgu):
    L = z.shape[0]
    heads = w_s.shape[0]
    hd = d_sgu // heads

    def body(zu_ref, zv_ref, dn_ref, lng_ref, lnb_ref, w_ref, bs_ref, gb_ref,
             dzu_ref, dzv_ref, dw_ref, dbs_ref, dlng_ref, dlnb_ref, dgb_ref, s_scr, dv_scr):
        first = pl.program_id(0) == 0
        zu, zv, lng = zu_ref[...], zv_ref[...], lng_ref[...]
        u1, xhat, r, v2, tril = _sgu_forward_values(zu, zv, lng, lnb_ref[...], w_ref, bs_ref, s_scr, heads, hd)
        s = s_scr[...]
        yb = u1 * s
        rb = _rms(yb)
        dn = dn_ref[...]
        dyb = _rmsnorm_bwd(dn, yb, rb, gb_ref[...])
        dzu_ref[...] = (dyb * s * _gelu_grad(zu)).astype(BF16)
        ds = dyb * u1
        for h in range(heads):
            cols = slice(h * hd, (h + 1) * hd)
            ds_h = ds[:, cols]
            ds_hb = ds_h.astype(BF16)
            wm = jnp.where(tril, w_ref[h], 0.0).astype(BF16)
            dw_h = jnp.where(tril, lax.dot_general(ds_hb, v2[:, cols].astype(BF16), (((1,), (1,)), ((), ())),
                                                   preferred_element_type=F32), 0.0)
            db_h = jnp.sum(ds_h.T, axis=0, keepdims=True)
            dv_scr[:, cols] = lax.dot_general(wm, ds_hb, (((0,), (0,)), ((), ())), preferred_element_type=F32)

            @pl.when(first)
            def _():
                dw_ref[h] = dw_h
                dbs_ref[h] = db_h

            @pl.when(jnp.logical_not(first))
            def _():
                dw_ref[h] += dw_h
                dbs_ref[h] += db_h

        dv2 = dv_scr[...]
        dxh = dv2 * lng
        dv1 = r * (dxh - jnp.mean(dxh, axis=-1, keepdims=True)
                   - xhat * jnp.mean(dxh * xhat, axis=-1, keepdims=True))
        dzv_ref[...] = (dv1 * _gelu_grad(zv)).astype(BF16)
        p_lng, p_lnb, p_gb = _rowsum(dv2 * xhat), _rowsum(dv2), _rowsum(dn * yb * rb)

        @pl.when(first)
        def _():
            dlng_ref[...] = p_lng
            dlnb_ref[...] = p_lnb
            dgb_ref[...] = p_gb

        @pl.when(jnp.logical_not(first))
        def _():
            dlng_ref[...] += p_lng
            dlnb_ref[...] += p_lnb
            dgb_ref[...] += p_gb

    blk = lambda col: pl.BlockSpec((SGU_CHUNK, d_sgu), lambda n: (n, col))
    full3 = lambda shape: pl.BlockSpec(shape, lambda n: (0, 0, 0))
    return pl.pallas_call(
        body, name=name, grid=(L // SGU_CHUNK,),
        in_specs=[blk(1), blk(2), blk(1), _vec_spec(d_sgu), _vec_spec(d_sgu),
                  full3(w_s.shape), full3(b_s.shape), _vec_spec(d_sgu)],
        out_specs=[blk(0), blk(0), full3(w_s.shape), full3((heads, 1, SGU_CHUNK)),
                   _vec_spec(d_sgu), _vec_spec(d_sgu), _vec_spec(d_sgu)],
        out_shape=[jax.ShapeDtypeStruct((L, d_sgu), BF16), jax.ShapeDtypeStruct((L, d_sgu), BF16),
                   jax.ShapeDtypeStruct(w_s.shape, F32), jax.ShapeDtypeStruct((heads, 1, SGU_CHUNK), F32),
                   jax.ShapeDtypeStruct((1, d_sgu), F32), jax.ShapeDtypeStruct((1, d_sgu), F32),
                   jax.ShapeDtypeStruct((1, d_sgu), F32)],
        scratch_shapes=[pltpu.VMEM((SGU_CHUNK, d_sgu), F32), pltpu.VMEM((SGU_CHUNK, d_sgu), F32)],
        compiler_params=_params(("arbitrary",)),
    )(z, z, dycat, ln_g, ln_b, w_s, b_s, g_b)


def _disc_lambda(lam_re, lam_im, log_step):
    lr = jnp.minimum(lam_re, LAMBDA_RE_MAX)
    li = lam_im
    dt = jnp.exp(log_step)
    mag = jnp.exp(lr * dt)
    ang = li * dt
    a_re = mag * jnp.cos(ang)
    a_im = mag * jnp.sin(ang)
    nr = a_re - 1.0
    ni = a_im
    den = lr * lr + li * li
    return a_re, a_im, (nr * lr + ni * li) / den, (ni * lr - nr * li) / den


def _disc_b(q_re, q_im, b_re, b_im):
    return q_re * b_re - q_im * b_im, q_re * b_im + q_im * b_re


def disc_lambda_fwd(lam_re, lam_im, log_step, *, name):
    def body(lr_ref, li_ref, ls_ref, ar_ref, ai_ref, qr_ref, qi_ref):
        ar_ref[...], ai_ref[...], qr_ref[...], qi_ref[...] = _disc_lambda(lr_ref[...], li_ref[...], ls_ref[...])

    sd = jax.ShapeDtypeStruct(lam_re.shape, F32)
    return pl.pallas_call(body, name=name, out_shape=[sd, sd, sd, sd], compiler_params=_params())(
        lam_re, lam_im, log_step)


def disc_lambda_bwd(lam_re, lam_im, log_step, cts, *, name):
    def body(lr_ref, li_ref, ls_ref, c0, c1, c2, c3, dlr_ref, dli_ref, dls_ref):
        _, vjp = jax.vjp(_disc_lambda, lr_ref[...], li_ref[...], ls_ref[...])
        dlr_ref[...], dli_ref[...], dls_ref[...] = vjp((c0[...], c1[...], c2[...], c3[...]))

    sd = jax.ShapeDtypeStruct(lam_re.shape, F32)
    return pl.pallas_call(body, name=name, out_shape=[sd, sd, jax.ShapeDtypeStruct(log_step.shape, F32)],
                          compiler_params=_params())(lam_re, lam_im, log_step, *cts)


def disc_b_fwd(q_re, q_im, b_re, b_im, *, name):
    def body(qr_ref, qi_ref, br_ref, bi_ref, or_ref, oi_ref):
        or_ref[...], oi_ref[...] = _disc_b(qr_ref[...], qi_ref[...], br_ref[...], bi_ref[...])

    sd = jax.ShapeDtypeStruct(b_re.shape, F32)
    return pl.pallas_call(body, name=name, out_shape=[sd, sd], compiler_params=_params())(q_re, q_im, b_re, b_im)


def disc_b_bwd(q_re, q_im, b_re, b_im, ct_re, ct_im, *, name):
    def body(qr_ref, qi_ref, br_ref, bi_ref, cr_ref, ci_ref, dqr_ref, dqi_ref, dbr_ref, dbi_ref):
        _, vjp = jax.vjp(_disc_b, qr_ref[...], qi_ref[...], br_ref[...], bi_ref[...])
        dqr_ref[...], dqi_ref[...], dbr_ref[...], dbi_ref[...] = vjp((cr_ref[...], ci_ref[...]))

    sq, sb = jax.ShapeDtypeStruct(q_re.shape, F32), jax.ShapeDtypeStruct(b_re.shape, F32)
    return pl.pallas_call(body, name=name, out_shape=[sq, sq, sb, sb], compiler_params=_params())(
        q_re, q_im, b_re, b_im, ct_re, ct_im)


def _lti_scan(xr, xi, ar, ai, reverse):
    T = xr.shape[0]
    row = lax.broadcasted_iota(jnp.int32, xr.shape, 0)
    k = 1
    while k < T:
        shift = T - k if reverse else k
        keep = (row < T - k) if reverse else (row >= k)
        sr = jnp.where(keep, pltpu.roll(xr, shift, 0), 0.0)
        si = jnp.where(keep, pltpu.roll(xi, shift, 0), 0.0)
        xr, xi = xr + ar * sr - ai * si, xi + ar * si + ai * sr
        ar, ai = ar * ar - ai * ai, 2.0 * ar * ai
        k *= 2
    return xr, xi


SUBLANES = 8


def _scan_rows(x_re, x_im, o_re, o_im, ar, ai, cr, ci, reverse):
    T, n = x_re.shape
    groups = T // SUBLANES
    row = lax.broadcasted_iota(jnp.int32, (SUBLANES, n), 0)
    edge = SUBLANES - 1 if reverse else 0
    pr, pi = _lti_scan(jnp.where(row == edge, ar, 0.0), jnp.where(row == edge, ai, 0.0), ar, ai, reverse)
    pows = []
    for _ in range(3):
        pows.append((jnp.broadcast_to(ar, (SUBLANES, n)), jnp.broadcast_to(ai, (SUBLANES, n))))
        ar, ai = ar * ar - ai * ai, 2.0 * ar * ai

    def group(i, carry):
        cr, ci = carry
        at = pl.multiple_of((groups - 1 - i if reverse else i) * SUBLANES, SUBLANES)
        xr, xi = x_re[pl.ds(at, SUBLANES), :], x_im[pl.ds(at, SUBLANES), :]
        for level, (qr, qi) in enumerate(pows):
            k = 1 << level
            keep = (row < SUBLANES - k) if reverse else (row >= k)
            sr = jnp.where(keep, pltpu.roll(xr, SUBLANES - k if reverse else k, 0), 0.0)
            si = jnp.where(keep, pltpu.roll(xi, SUBLANES - k if reverse else k, 0), 0.0)
            xr, xi = xr + qr * sr - qi * si, xi + qr * si + qi * sr
        xr, xi = xr + pr * cr - pi * ci, xi + pr * ci + pi * cr
        o_re[pl.ds(at, SUBLANES), :] = xr
        o_im[pl.ds(at, SUBLANES), :] = xi
        last = 0 if reverse else SUBLANES - 1
        return xr[last:last + 1, :], xi[last:last + 1, :]

    return lax.fori_loop(0, groups, group, (cr, ci), unroll=2)


def _ssm_chunk(L):
    return _tile(L, 256, 8)


def ssm_fwd(z, bs_re, bs_im, cs_re, cs_im, a_re, a_im, d, *, name, bg=None):
    L = z.shape[0]
    NK, C, S = bs_re.shape
    T = _ssm_chunk(L)
    grid = (NK, L // T)
    bg_in_specs, bg_args, bg_out_specs, bg_out_shapes, bg_scratch, split = _carrier(bg, 8, 3, 4, grid)

    def body(*refs):
        ((u_ref, br_ref, bi_ref, cr_ref, ci_ref, ar_ref, ai_ref, d_ref), (y_ref, sr_ref, si_ref),
         (car_re, car_im, bu_re, bu_im), run_background) = split(refs)
        run_background()
        i = pl.program_id(1)
        ar, ai = ar_ref[...], ai_ref[...]

        @pl.when(i == 0)
        def _():
            car_re[...] = jnp.zeros_like(car_re)
            car_im[...] = jnp.zeros_like(car_im)

        u = u_ref[...]
        ub = u.astype(BF16)
        bu_re[...] = jnp.dot(ub, br_ref[...], preferred_element_type=F32)
        bu_im[...] = jnp.dot(ub, bi_ref[...], preferred_element_type=F32)
        car_re[...], car_im[...] = _scan_rows(bu_re, bu_im, sr_ref, si_ref, ar, ai, car_re[...], car_im[...], False)
        y_ref[...] = (jnp.dot(sr_ref[...].astype(BF16), cr_ref[...], preferred_element_type=F32)
                      - jnp.dot(si_ref[...].astype(BF16), ci_ref[...], preferred_element_type=F32)
                      + d_ref[...] * u)

    kspec = lambda shape: pl.BlockSpec((None,) + shape, lambda k, i: (k, 0, 0))
    res = pl.pallas_call(
        body, name=name, grid=grid,
        in_specs=[pl.BlockSpec((T, C), lambda k, i: (i, k)),
                  kspec((C, S)), kspec((C, S)), kspec((S, C)), kspec((S, C)),
                  kspec((1, S)), kspec((1, S)), kspec((1, C))] + bg_in_specs,
        out_specs=[pl.BlockSpec((T, C), lambda k, i: (i, k)),
                   pl.BlockSpec((T, S), lambda k, i: (i, k)), pl.BlockSpec((T, S), lambda k, i: (i, k))] + bg_out_specs,
        out_shape=[jax.ShapeDtypeStruct((L, NK * C), F32), jax.ShapeDtypeStruct((L, NK * S), F32),
                   jax.ShapeDtypeStruct((L, NK * S), F32)] + bg_out_shapes,
        scratch_shapes=[pltpu.VMEM((1, S), F32), pltpu.VMEM((1, S), F32),
                        pltpu.VMEM((T, S), F32), pltpu.VMEM((T, S), F32)] + bg_scratch,
        compiler_params=_params(_semantics(bg, ("parallel", "arbitrary"))),
    )(z, bs_re, bs_im, cs_re, cs_im, a_re, a_im, d, *bg_args)
    return _results(res, 3, bg)


def ssm_bwd(z, y0, dd_direct, dd_mm, s_re, s_im, bst_re, bst_im, cst_re, cst_im, a_re, a_im, d, *, name):
    L = z.shape[0]
    NK, S, C = bst_re.shape
    T = _ssm_chunk(L)
    nchunk = L // T
    tail = T // 8

    def body(u_ref, y_ref, d1_ref, d2_ref, sr_ref, si_ref, pr_ref, pi_ref,
             btr_ref, bti_ref, ctr_ref, cti_ref, ar_ref, ai_ref, d_ref,
             du_ref, dbr_ref, dbi_ref, dcr_ref, dci_ref, dar_ref, dai_ref, dd_ref,
             car_re, car_im, lam_re, lam_im):
        i = pl.program_id(1)
        chunk = nchunk - 1 - i
        ar, ai = ar_ref[...], ai_ref[...]
        row = lax.broadcasted_iota(jnp.int32, (T, S), 0)

        @pl.when(i == 0)
        def _():
            car_re[...] = jnp.zeros_like(car_re)
            car_im[...] = jnp.zeros_like(car_im)

        u = u_ref[...]
        dy = (d1_ref[...] + d2_ref[...]) * _gelu_grad(y_ref[...])
        dyb = dy.astype(BF16)
        lam_re[...] = jnp.dot(dyb, ctr_ref[...], preferred_element_type=F32)
        lam_im[...] = -jnp.dot(dyb, cti_ref[...], preferred_element_type=F32)
        car_re[...], car_im[...] = _scan_rows(lam_re, lam_im, lam_re, lam_im, ar, -ai, car_re[...], car_im[...], True)
        lr, li = lam_re[...], lam_im[...]

        s_re, s_im = sr_ref[...], si_ref[...]
        has_prev = (chunk > 0).astype(F32)
        prev_re = pr_ref[7:8, :] * has_prev
        prev_im = pi_ref[7:8, :] * has_prev
        sp_re = jnp.where(row == 0, prev_re, pltpu.roll(s_re, 1, 0))
        sp_im = jnp.where(row == 0, prev_im, pltpu.roll(s_im, 1, 0))
        p_ar = _rowsum(lr * sp_re + li * sp_im)
        p_ai = _rowsum(li * sp_re - lr * sp_im)

        lrb, lib, ub = lr.astype(BF16), li.astype(BF16), u.astype(BF16)
        du = (dy * d_ref[...] + jnp.dot(lrb, btr_ref[...], preferred_element_type=F32)
              + jnp.dot(lib, bti_ref[...], preferred_element_type=F32))
        du_ref[...] = du.astype(BF16)
        tdot = lambda p, q: lax.dot_general(p, q, (((0,), (0,)), ((), ())), preferred_element_type=F32)
        p_br, p_bi = tdot(ub, lrb), tdot(ub, lib)
        p_cr, p_ci = tdot(s_re.astype(BF16), dyb), -tdot(s_im.astype(BF16), dyb)
        p_dd = _rowsum(dy * u)

        @pl.when(i == 0)
        def _():
            dar_ref[...] = p_ar
            dai_ref[...] = p_ai
            dbr_ref[...] = p_br
            dbi_ref[...] = p_bi
            dcr_ref[...] = p_cr
            dci_ref[...] = p_ci
            dd_ref[...] = p_dd

        @pl.when(i > 0)
        def _():
            dar_ref[...] += p_ar
            dai_ref[...] += p_ai
            dbr_ref[...] += p_br
            dbi_ref[...] += p_bi
            dcr_ref[...] += p_cr
            dci_ref[...] += p_ci
            dd_ref[...] += p_dd

    rev = lambda k, i: (nchunk - 1 - i, k)
    prev = lambda k, i: (jnp.maximum((nchunk - 1 - i) * tail - 1, 0), k)
    kspec = lambda shape: pl.BlockSpec((None,) + shape, lambda k, i: (k, 0, 0))
    return pl.pallas_call(
        body, name=name, grid=(NK, nchunk),
        in_specs=[pl.BlockSpec((T, C), rev), pl.BlockSpec((T, C), rev), pl.BlockSpec((T, C), rev),
                  pl.BlockSpec((T, C), rev), pl.BlockSpec((T, S), rev), pl.BlockSpec((T, S), rev),
                  pl.BlockSpec((8, S), prev), pl.BlockSpec((8, S), prev),
                  kspec((S, C)), kspec((S, C)), kspec((C, S)), kspec((C, S)),
                  kspec((1, S)), kspec((1, S)), kspec((1, C))],
        out_specs=[pl.BlockSpec((T, C), rev), kspec((C, S)), kspec((C, S)), kspec((S, C)), kspec((S, C)),
                   kspec((1, S)), kspec((1, S)), kspec((1, C))],
        out_shape=[jax.ShapeDtypeStruct((L, NK * C), BF16),
                   jax.ShapeDtypeStruct((NK, C, S), F32), jax.ShapeDtypeStruct((NK, C, S), F32),
                   jax.ShapeDtypeStruct((NK, S, C), F32), jax.ShapeDtypeStruct((NK, S, C), F32),
                   jax.ShapeDtypeStruct((NK, 1, S), F32), jax.ShapeDtypeStruct((NK, 1, S), F32),
                   jax.ShapeDtypeStruct((NK, 1, C), F32)],
        scratch_shapes=[pltpu.VMEM((1, S), F32), pltpu.VMEM((1, S), F32),
                        pltpu.VMEM((T, S), F32), pltpu.VMEM((T, S), F32)],
        compiler_params=_params(("parallel", "arbitrary")),
    )(z, y0, dd_direct, dd_mm, s_re, s_im, s_re, s_im, bst_re, bst_im, cst_re, cst_im, a_re, a_im, d)


def _block_diag(v):
    NK, SG, R, Q = v.shape
    eye = jnp.eye(SG, dtype=v.dtype)
    return (v[:, :, :, None, :] * eye[None, :, None, :, None]).reshape(NK, SG * R, SG * Q)


def _block_diag_part(m, SG):
    NK, RR, QQ = m.shape
    R, Q = RR // SG, QQ // SG
    eye = jnp.eye(SG, dtype=m.dtype)
    return jnp.sum(m.reshape(NK, SG, R, SG, Q) * eye[None, :, None, :, None], axis=3)


def _position():
    return lax.axis_index("x"), lax.axis_index("y"), lax.axis_index("c")


def _other_chips(x, y):
    return [(1 - x, y), (x, 1 - y), (1 - x, 1 - y)]


def _gather_phases(n):
    def parts(ins, outs, sems):
        send_sems, recv_sems, local_sems = sems
        x, y, c = _position()
        me, sibling = (x, y, c), (x, y, 1 - c)
        chips = _other_chips(x, y)

        def block(a, pos):
            index = 4 * pos[0] + 2 * pos[1] + pos[2]
            return outs[a].at[pl.ds(index, 1)] if _is_row(ins[a]) else outs[a].at[index]

        def copy(a, k, pos, to, src=None):
            return pltpu.make_async_remote_copy(
                src_ref=block(a, pos) if src is None else src, dst_ref=block(a, pos),
                send_sem=send_sems.at[7 * a + k], recv_sem=recv_sems.at[7 * a + k],
                device_id=to, device_id_type=MESH)

        mine = [pltpu.make_async_copy(ins[a], block(a, me), local_sems.at[a]) for a in range(n)]
        first = []
        for a in range(n):
            first.append(copy(a, 0, me, sibling, src=ins[a]))
            first += [copy(a, 1 + j, me, (*chip, c), src=ins[a]) for j, chip in enumerate(chips)]
        passed = [copy(a, 4 + j, (*chip, c), sibling) for a in range(n) for j, chip in enumerate(chips)]
        arrived = [copy(a, 1 + j, (*chip, c), me) for a in range(n) for j, chip in enumerate(chips)]
        from_sibling = []
        for a in range(n):
            from_sibling.append(copy(a, 0, sibling, me))
            from_sibling += [copy(a, 4 + j, (*chip, 1 - c), me) for j, chip in enumerate(chips)]
        return mine, first, passed, arrived, from_sibling

    def send(ins, outs, sems):
        mine, first, _, _, _ = parts(ins, outs, sems)
        for cp in mine + first:
            cp.start()

    def forward(ins, outs, sems):
        _, _, passed, arrived, _ = parts(ins, outs, sems)
        for got, fwd in zip(arrived, passed):
            got.wait_recv()
            fwd.start()

    def finish(ins, outs, sems):
        mine, first, passed, _, from_sibling = parts(ins, outs, sems)
        for cp in from_sibling:
            cp.wait_recv()
        for cp in first + passed:
            cp.wait_send()
        for cp in mine:
            cp.wait()

    return [(0.0, send), (GATHER_FORWARD_AT, forward), (1.0, finish)]


def _is_row(a):
    return len(a.shape) == 2 and a.shape[0] == 1


def _gather_shapes(shards):
    n = len(shards)
    return ([jax.ShapeDtypeStruct((N_DEV,) + (s.shape[1:] if _is_row(s) else s.shape), s.dtype) for s in shards],
            [pltpu.SemaphoreType.DMA((7 * n,)), pltpu.SemaphoreType.DMA((7 * n,)), pltpu.SemaphoreType.DMA((n,))])


def gather_background(shards):
    out_shapes, scratch = _gather_shapes(shards)
    return Background(shards, out_shapes, scratch, _gather_phases(len(shards)))


def all_gather_blocks(shards, *, name):
    n = len(shards)
    out_shapes, scratch = _gather_shapes(shards)

    def body(*refs):
        for _, phase in _gather_phases(n):
            phase(refs[:n], refs[n:2 * n], refs[2 * n:])

    return pl.pallas_call(
        body, name=name, in_specs=[ANY] * n, out_specs=[ANY] * n, out_shape=out_shapes, scratch_shapes=scratch,
    )(*shards)


def sibling_exchange(grads, *, name):
    n = len(grads)

    def body(*refs):
        ins, outs = refs[:n], refs[n:2 * n]
        send_sems, recv_sems = refs[2 * n:]
        x, y, c = _position()
        copies = []
        for a in range(n):
            for q in range(4):
                copies.append(pltpu.make_async_remote_copy(
                    src_ref=ins[a].at[2 * q + 1 - c], dst_ref=outs[a].at[q],
                    send_sem=send_sems.at[4 * a + q], recv_sem=recv_sems.at[4 * a + q],
                    device_id=(x, y, 1 - c), device_id_type=MESH))
        for cp in copies:
            cp.start()
        for cp in copies:
            cp.wait()

    return pl.pallas_call(
        body, name=name,
        in_specs=[ANY] * n, out_specs=[ANY] * n,
        out_shape=[jax.ShapeDtypeStruct((4,) + g.shape[1:], g.dtype) for g in grads],
        scratch_shapes=[pltpu.SemaphoreType.DMA((4 * n,)), pltpu.SemaphoreType.DMA((4 * n,))],
    )(*grads)


def _chip_exchange_phases(n):
    def copies(ins, outs, sems):
        x, y, c = _position()
        return [pltpu.make_async_remote_copy(
            src_ref=ins[a].at[2 * chip[0] + chip[1]], dst_ref=outs[a].at[j],
            send_sem=sems[0].at[3 * a + j], recv_sem=sems[1].at[3 * a + j],
            device_id=(*chip, c), device_id_type=MESH)
            for a in range(n) for j, chip in enumerate(_other_chips(x, y))]

    def send(ins, outs, sems):
        for cp in copies(ins, outs, sems):
            cp.start()

    def finish(ins, outs, sems):
        for cp in copies(ins, outs, sems):
            cp.wait()

    return [(0.0, send), (1.0, finish)]


def chip_exchange_background(parts):
    n = len(parts)
    return Background(parts, [jax.ShapeDtypeStruct((3,) + p.shape[1:], p.dtype) for p in parts],
                      [pltpu.SemaphoreType.DMA((3 * n,)), pltpu.SemaphoreType.DMA((3 * n,))],
                      _chip_exchange_phases(n))


def add_pairs(grads, theirs, core, *, name, tm=512):
    _, R, C = theirs.shape
    tm = _tile(R, tm, 16)

    def body(core_ref, a_ref, b_ref, o_ref):
        o_ref[...] = (a_ref[...].astype(F32) + b_ref[...].astype(F32)).astype(o_ref.dtype)

    spec = pl.BlockSpec((None, tm, C), lambda q, i, core_ref: (q, i, 0))
    return pl.pallas_call(
        body, name=name,
        grid_spec=pltpu.PrefetchScalarGridSpec(
            num_scalar_prefetch=1, grid=(4, R // tm),
            in_specs=[pl.BlockSpec((None, tm, C), lambda q, i, core_ref: (2 * q + core_ref[0], i, 0)), spec],
            out_specs=spec),
        out_shape=jax.ShapeDtypeStruct(theirs.shape, BF16),
        compiler_params=_params(("parallel", "parallel")),
    )(core, grads, theirs)


def _adamw(w, g, m, v):
    m = ADAM_B1 * m + (1.0 - ADAM_B1) * g
    v = ADAM_B2 * v + (1.0 - ADAM_B2) * (g * g)
    m_hat = m / (1.0 - ADAM_B1 ** ADAM_STEP)
    v_hat = v / (1.0 - ADAM_B2 ** ADAM_STEP)
    delta = -ADAM_LR * (m_hat / (jnp.sqrt(v_hat) + ADAM_EPS) + ADAM_WD * w)
    return delta, m, v


def adamw_sharded(w, m, v, grads, theirs, others, where, *, name, tm=256):
    R, C = w.shape
    tm = _tile(R, tm, 16)

    def body(where_ref, w_ref, m_ref, v_ref, a_ref, b_ref, o_ref, g_ref, d_ref, nm_ref, nv_ref):
        g = a_ref[...].astype(F32) + b_ref[...].astype(F32)
        for j in range(3):
            g = g + o_ref[j].astype(F32)
        g_ref[...] = g
        d_ref[...], nm_ref[...], nv_ref[...] = _adamw(w_ref[...], g, m_ref[...], v_ref[...])

    spec = pl.BlockSpec((tm, C), lambda i, where_ref: (i, 0))
    sd = jax.ShapeDtypeStruct((R, C), F32)
    return pl.pallas_call(
        body, name=name,
        grid_spec=pltpu.PrefetchScalarGridSpec(
            num_scalar_prefetch=1, grid=(R // tm,),
            in_specs=[spec, spec, spec,
                      pl.BlockSpec((None, tm, C), lambda i, where_ref: (where_ref[0], i, 0)),
                      pl.BlockSpec((None, tm, C), lambda i, where_ref: (where_ref[1], i, 0)),
                      pl.BlockSpec((3, tm, C), lambda i, where_ref: (0, i, 0))],
            out_specs=[spec, spec, spec, spec]),
        out_shape=[sd, sd, sd, sd],
        compiler_params=_params(("parallel",)),
    )(where, w, m, v, grads, theirs, others)


def sum_gathered(gathered, *, name):
    n = len(gathered)

    def body(*refs):
        for ga_ref, o_ref in zip(refs[:n], refs[n:]):
            rows = len(ga_ref.shape) == 2
            total = ga_ref[0:1] if rows else ga_ref[0]
            for dev in range(1, N_DEV):
                total = total + (ga_ref[dev:dev + 1] if rows else ga_ref[dev])
            o_ref[...] = total

    shapes = [jax.ShapeDtypeStruct((1,) + g.shape[1:] if g.ndim == 2 else g.shape[1:], F32) for g in gathered]
    return pl.pallas_call(body, name=name, out_shape=shapes, compiler_params=_params())(*gathered)


def adamw_replicated(ws, ms, vs, gs, *, name):
    n = len(ws)

    def body(*refs):
        w_refs, m_refs, v_refs, g_refs = refs[:n], refs[n:2 * n], refs[2 * n:3 * n], refs[3 * n:4 * n]
        outs = refs[4 * n:]
        for k in range(n):
            outs[k][...], outs[n + k][...], outs[2 * n + k][...] = _adamw(
                w_refs[k][...], g_refs[k][...], m_refs[k][...], v_refs[k][...])

    shapes = [jax.ShapeDtypeStruct(t.shape, F32) for t in ws]
    res = pl.pallas_call(body, name=name, out_shape=shapes * 3, compiler_params=_params())(*ws, *ms, *vs, *gs)
    return res[:n], res[n:2 * n], res[2 * n:]


SHARDED = ("w_in", "ssm_glu_w", "w_out", "w_ffn_in", "w_ffn_out", "w_ple_gate", "w_ple_proj")
SMALL_LAST = ("norm_mix_g",)
SMALL_WIDE = ("ssm_b_re", "ssm_b_im", "ssm_c_re", "ssm_c_im")
SMALL = ("ssm_lambda_re", "ssm_lambda_im", "ssm_log_step", "ssm_b_re", "ssm_b_im", "ssm_c_re",
         "ssm_c_im", "ssm_d", "ssm_glu_b", "sgu_ln_g", "sgu_ln_b", "sgu_w", "sgu_b", "out_norm_ssm_g",
         "out_norm_sgu_g", "norm_ffn_g", "norm_ple_g", "b_ple_gate", "final_norm_g")
WEIGHTS = ("norm_mix_g", "w_in", "ssm_lambda_re", "ssm_lambda_im", "ssm_log_step", "ssm_b_re", "ssm_b_im",
           "ssm_c_re", "ssm_c_im", "ssm_d", "ssm_glu_w", "ssm_glu_b", "sgu_ln_g", "sgu_ln_b", "sgu_w", "sgu_b",
           "out_norm_ssm_g", "out_norm_sgu_g", "w_out", "norm_ffn_g", "w_ffn_in", "w_ffn_out", "norm_ple_g",
           "w_ple_gate", "b_ple_gate", "w_ple_proj", "final_norm_g")


def _step(x, p, loss_target, w, m, v):
    L, D = x.shape[1], x.shape[2]
    x2d, p2d, tgt = x.reshape(L, D), p.reshape(L, -1), loss_target.reshape(L, D)
    d_ssm = w["ssm_glu_w"].shape[2]
    d_sgu = w["sgu_ln_g"].shape[1]
    G, P, H = w["ssm_b_re"].shape[1:]
    SG = min(SSM_SUPER, G)
    NK = G // SG
    row = lambda a: a.reshape(1, -1)

    shard2d = {n: w[n].reshape(w[n].shape[1:]) for n in SHARDED}
    shard_bf = {n: shard2d[n].astype(BF16) for n in SHARDED}
    (w_ple_blk,) = all_gather_blocks([shard_bf["w_ple_proj"]], name="gather_w_ple")
    bf = lambda t: t.astype(BF16)
    pp, (w_in_blk,) = mm_nn(bf(p2d), w_ple_blk, name="ple_proj", out_dtype=F32, tm=512, tn=512, tk=2048,
                            bg=gather_background([shard_bf["w_in"]]))
    w_in = jnp.transpose(w_in_blk, (1, 0, 2)).reshape(D, -1)
    F = shard2d["w_ffn_in"].shape[1] * 4

    lam_re, lam_im, log_step = w["ssm_lambda_re"][0], w["ssm_lambda_im"][0], w["ssm_log_step"][0].reshape(G, 1)
    a_re, a_im, q_re, q_im = disc_lambda_fwd(lam_re, lam_im, log_step, name="s5_discretise_lambda")
    bt_re = w["ssm_b_re"][0].transpose(2, 0, 1).reshape(H, G * P)
    bt_im = w["ssm_b_im"][0].transpose(2, 0, 1).reshape(H, G * P)
    bbar_re, bbar_im = disc_b_fwd(row(q_re), row(q_im), bt_re, bt_im, name="s5_discretise_b")
    to_bs = lambda t: _block_diag(t.reshape(H, NK, SG, P).transpose(1, 2, 0, 3))
    to_cs = lambda t: _block_diag(t.reshape(NK, SG, H, P).transpose(0, 1, 3, 2))
    bs_re, bs_im = to_bs(bbar_re), to_bs(bbar_im)
    cs_re, cs_im = to_cs(w["ssm_c_re"][0]), to_cs(w["ssm_c_im"][0])
    a_re_k, a_im_k = a_re.reshape(NK, 1, SG * P), a_im.reshape(NK, 1, SG * P)
    d_k = w["ssm_d"][0].reshape(NK, 1, SG * H)
    bf = lambda t: t.astype(BF16)
    tr = lambda t: jnp.swapaxes(t, 1, 2)

    h1 = norm_fwd(x2d, w["norm_mix_g"], name="norm_mix")
    z, (w_glu, w_out) = mm_nn(h1, w_in, name="in_proj", out_dtype=F32, tm=512, tn=1024, tk=2048,
                              bg=gather_background([shard_bf["ssm_glu_w"], shard_bf["w_out"]]))
    w_glu, w_out = w_glu.reshape(d_ssm, d_ssm), w_out.reshape(D, D)
    (y0, s_re, s_im), (w_ffn_in_blk,) = ssm_fwd(
        z, bf(bs_re), bf(bs_im), bf(cs_re), bf(cs_im), a_re_k, a_im_k, d_k, name="s5_scan",
        bg=gather_background([shard_bf["w_ffn_in"]]))
    ya1 = glu_pre(y0, name="s5_gelu")
    t_glu = mm_nn(ya1, w_glu, name="s5_glu_proj", out_dtype=F32, tm=512, tn=512, tk=2048)
    n_a = glu_post(y0, t_glu, w["ssm_glu_b"], w["out_norm_ssm_g"], name="s5_glu_norm")
    b_s3 = w["sgu_b"][0][:, :, None]
    n_b = sgu_fwd(z, w["sgu_ln_g"], w["sgu_ln_b"], w["sgu_w"][0], b_s3, w["out_norm_sgu_g"], name="sgu", d_sgu=d_sgu)
    ycat = jnp.concatenate([n_a, n_b], axis=1)
    x1 = mm_nn(ycat, w_out, name="out_proj", out_dtype=F32, tm=512, tn=512, tk=2048, residual=x2d)
    h2 = norm_fwd(x1, w["norm_ffn_g"], name="norm_ffn")
    (act, gate_ff, up_ff), (w_ffn_out, w_gate) = ffn_in_swiglu(
        h2, w_ffn_in_blk, name="ffn_in_swiglu", tm=256,
        bg=gather_background([shard_bf["w_ffn_out"], shard_bf["w_ple_gate"]]))
    w_ffn_out, w_gate = w_ffn_out.reshape(F, D), w_gate.reshape(D, D)
    x2 = mm_nn(act, w_ffn_out, name="ffn_out", out_dtype=F32, tm=512, tn=512, tk=F, residual=x1)
    h3 = norm_fwd(x2, w["norm_ple_g"], name="norm_ple")
    gpre = mm_nn(h3, w_gate, name="ple_gate", out_dtype=F32, tm=512, tn=1024, tk=2048)

    dx3, dpre, dpp, loss_part, d_final_g, d_b_gate = head_and_loss(
        x2, gpre, w["b_ple_gate"], pp, row(w["final_norm_g"]), tgt, name="head_and_loss")
    x_pos, y_pos, c_pos = _position()
    where = jnp.stack([4 * x_pos + 2 * y_pos + c_pos, 2 * x_pos + y_pos]).astype(jnp.int32)
    core = jnp.reshape(c_pos, (1,)).astype(jnp.int32)
    own, others = {}, {}

    def chip_partials(named, tag):
        names = list(named)
        g8 = [named[n].reshape((N_DEV,) + shard2d[n].shape) for n in names]
        theirs = sibling_exchange(g8, name="grads_to_sibling_" + tag)
        own.update(zip(names, zip(g8, theirs)))
        return names, chip_exchange_background(
            [add_pairs(g, t, core, name="chip_sum_" + n) for n, g, t in zip(names, g8, theirs)])

    d_w_gate = mm_tn(h3, dpre, name="d_w_ple_gate", out_dtype=BF16, tm=L, tko=1024, tno=1024)
    d_w_ple = mm_tn(bf(p2d), dpp, name="d_w_ple_proj", out_dtype=BF16, tm=L, tko=1024, tno=1024, out_blocks=N_DEV)
    names, bg = chip_partials({"w_ple_gate": d_w_gate, "w_ple_proj": d_w_ple}, "ple")
    dh3, got = mm_nt(dpre, w_gate, name="d_h_ple", out_dtype=F32, tm=512, tko=1024, tc=2048, bg=bg)
    others.update(zip(names, got))
    dx2, dx2b, d_ple_g = norm_bwd(dh3, x2, w["norm_ple_g"], dx3, name="d_norm_ple", want_bf16=True)
    d_w_ffn_out = mm_tn(act, dx2b, name="d_w_ffn_out", out_dtype=BF16, tm=L, tko=1408, tno=512)
    names, bg = chip_partials({"w_ffn_out": d_w_ffn_out}, "ffn_out")
    dgate, dup = ffn_out_bwd_swiglu(dx2b, w_ffn_out, gate_ff, up_ff, name="d_act_swiglu")
    half = N_DEV // 2
    d_w_ffn_in, got = mm_tn(h2, dgate, name="d_w_ffn_in_gate", out_dtype=BF16, tm=L, tko=512, tno=1408,
                            out_blocks=half, total_blocks=N_DEV, bg=bg)
    others.update(zip(names, got))
    d_w_ffn_in = mm_tn(h2, dup, name="d_w_ffn_in_up", out_dtype=BF16, tm=L, tko=512, tno=1408,
                       out_blocks=half, block_offset=half, total_blocks=N_DEV, into=d_w_ffn_in)
    names, bg = chip_partials({"w_ffn_in": d_w_ffn_in}, "ffn_in")
    dh2, got = mm_nt(dgate, w_ffn_in_blk, a2=dup, name="d_h_ffn", out_dtype=F32, tm=1024, tko=1024, tc=1408, bg=bg)
    others.update(zip(names, got))
    dx1, dx1b, d_ffn_g = norm_bwd(dh2, x1, w["norm_ffn_g"], dx2, name="d_norm_ffn", want_bf16=True)
    dycat = mm_nt(dx1b, w_out, name="d_ycat", out_dtype=F32, tm=512, tko=1024, tc=2048)
    d_w_out = mm_tn(ycat, dx1b, name="d_w_out", out_dtype=BF16, tm=L, tko=1024, tno=1024)
    dzu, dzv, d_sgu_w, d_sgu_b, d_ln_g, d_ln_b, d_g_b = sgu_bwd(
        z, dycat, w["sgu_ln_g"], w["sgu_ln_b"], w["sgu_w"][0], b_s3, w["out_norm_sgu_g"], name="d_sgu", d_sgu=d_sgu)
    dt_glu, dd_direct, d_g_a, d_glu_b = glu_post_bwd(
        y0, t_glu, w["ssm_glu_b"], w["out_norm_ssm_g"], dycat, name="d_s5_glu_norm")
    d_w_glu = mm_tn(ya1, dt_glu, name="d_w_glu", out_dtype=BF16, tm=L, tko=1024, tno=1024)
    names, bg = chip_partials({"w_out": d_w_out, "ssm_glu_w": d_w_glu}, "out_glu")
    dd_mm = mm_nt(dt_glu, w_glu, name="d_s5_glu_proj", out_dtype=F32, tm=512, tko=1024, tc=2048)
    du, d_bs_re, d_bs_im, d_cs_re, d_cs_im, d_a_re, d_a_im, d_d = ssm_bwd(
        z, y0, dd_direct, dd_mm, s_re, s_im, bf(tr(bs_re)), bf(tr(bs_im)), bf(tr(cs_re)), bf(tr(cs_im)),
        a_re_k, a_im_k, d_k, name="d_s5_scan")
    dz = jnp.concatenate([du, dzu, dzv], axis=1)
    d_w_in, got = mm_tn(h1, dz, name="d_w_in", out_dtype=BF16, tm=L, tko=1024, tno=1024, out_blocks=N_DEV, bg=bg)
    others.update(zip(names, got))

    from_bs = lambda t: _block_diag_part(t, SG).transpose(2, 0, 1, 3).reshape(H, G * P)
    from_cs = lambda t: _block_diag_part(t, SG).transpose(3, 0, 1, 2).reshape(H, G * P)
    d_q_re, d_q_im, d_bt_re, d_bt_im = disc_b_bwd(row(q_re), row(q_im), bt_re, bt_im, from_bs(d_bs_re),
                                                  from_bs(d_bs_im), name="d_s5_discretise_b")
    d_lam_re, d_lam_im, d_log_step = disc_lambda_bwd(
        lam_re, lam_im, log_step,
        (d_a_re.reshape(G, P), d_a_im.reshape(G, P), d_q_re.reshape(G, P), d_q_im.reshape(G, P)),
        name="d_s5_discretise_lambda")
    small_grads = {
        "ssm_lambda_re": d_lam_re, "ssm_lambda_im": d_lam_im, "ssm_log_step": d_log_step,
        "ssm_b_re": d_bt_re, "ssm_b_im": d_bt_im, "ssm_c_re": from_cs(d_cs_re), "ssm_c_im": from_cs(d_cs_im),
        "ssm_d": d_d, "ssm_glu_b": d_glu_b, "sgu_ln_g": d_ln_g, "sgu_ln_b": d_ln_b,
        "sgu_w": d_sgu_w, "sgu_b": d_sgu_b, "out_norm_ssm_g": d_g_a, "out_norm_sgu_g": d_g_b,
        "norm_ffn_g": d_ffn_g, "norm_ple_g": d_ple_g, "b_ple_gate": d_b_gate, "final_norm_g": d_final_g,
    }

    names, bg = chip_partials({"w_in": d_w_in}, "w_in")
    dh1, got = mm_nt(dz, w_in, name="d_h_mix", out_dtype=F32, tm=512, tko=1024, tc=3 * d_sgu,
                     bg=combine_backgrounds(bg, gather_background([loss_part] + [small_grads[n] for n in SMALL])))
    others.update(zip(names, got[:len(names)]))
    sums = sum_gathered(got[len(names):], name="sum_small_grads")
    grad_x, d_mix_g = norm_bwd(dh1, x2d, w["norm_mix_g"], dx1, name="d_norm_mix", want_bf16=False)
    loss, small_sum = sums[0][0, 0], dict(zip(SMALL, sums[1:]))
    (small_sum["norm_mix_g"],) = sum_gathered(all_gather_blocks([d_mix_g], name="gather_last_grad"),
                                              name="sum_last_grad")

    out = {}
    for n in SHARDED:
        res = adamw_sharded(shard2d[n], m[n].reshape(shard2d[n].shape), v[n].reshape(shard2d[n].shape),
                            own[n][0], own[n][1], others[n], where, name="adamw_" + n)
        out[n] = [r.reshape(w[n].shape) for r in res]

    def work_shape(n):
        s = w[n].shape
        return (1,) + s if len(s) == 1 else (s if len(s) == 2 else s[1:])

    for n in ("ssm_b_re", "ssm_b_im"):
        small_sum[n] = small_sum[n].reshape(H, G, P).transpose(1, 2, 0)
    for n in ("ssm_c_re", "ssm_c_im"):
        small_sum[n] = small_sum[n].reshape(H, G, P).transpose(1, 0, 2)

    def replicated(names_, name):
        gs = [small_sum[n].reshape(work_shape(n)) for n in names_]
        res = adamw_replicated(*[[t[n].reshape(work_shape(n)) for n in names_] for t in (w, m, v)], gs, name=name)
        for i, n in enumerate(names_):
            out[n] = [r.reshape(w[n].shape) for r in (gs[i], res[0][i], res[1][i], res[2][i])]

    replicated([n for n in SMALL + SMALL_LAST if n not in SMALL_WIDE], "adamw_small")
    replicated(list(SMALL_WIDE), "adamw_s5_b_c")

    grads = [out[n][0] for n in WEIGHTS]
    deltas = [out[n][1] for n in WEIGHTS]
    new_m = [out[n][2] for n in WEIGHTS]
    new_v = [out[n][3] for n in WEIGHTS]
    return (loss, grad_x.reshape(x.shape), *grads, *deltas, *new_m, *new_v)


def kernel(x, p, norm_mix_g, w_in, ssm_lambda_re, ssm_lambda_im, ssm_log_step, ssm_b_re, ssm_b_im, ssm_c_re, ssm_c_im, ssm_d, ssm_glu_w, ssm_glu_b, sgu_ln_g, sgu_ln_b, sgu_w, sgu_b, out_norm_ssm_g, out_norm_sgu_g, w_out, norm_ffn_g, w_ffn_in, w_ffn_out, norm_ple_g, w_ple_gate, b_ple_gate, w_ple_proj, final_norm_g, loss_target, m_norm_mix_g, m_w_in, m_ssm_lambda_re, m_ssm_lambda_im, m_ssm_log_step, m_ssm_b_re, m_ssm_b_im, m_ssm_c_re, m_ssm_c_im, m_ssm_d, m_ssm_glu_w, m_ssm_glu_b, m_sgu_ln_g, m_sgu_ln_b, m_sgu_w, m_sgu_b, m_out_norm_ssm_g, m_out_norm_sgu_g, m_w_out, m_norm_ffn_g, m_w_ffn_in, m_w_ffn_out, m_norm_ple_g, m_w_ple_gate, m_b_ple_gate, m_w_ple_proj, m_final_norm_g, v_norm_mix_g, v_w_in, v_ssm_lambda_re, v_ssm_lambda_im, v_ssm_log_step, v_ssm_b_re, v_ssm_b_im, v_ssm_c_re, v_ssm_c_im, v_ssm_d, v_ssm_glu_w, v_ssm_glu_b, v_sgu_ln_g, v_sgu_ln_b, v_sgu_w, v_sgu_b, v_out_norm_ssm_g, v_out_norm_sgu_g, v_w_out, v_norm_ffn_g, v_w_ffn_in, v_w_ffn_out, v_norm_ple_g, v_w_ple_gate, v_b_ple_gate, v_w_ple_proj, v_final_norm_g):
    given = dict(locals())
    w = {n: given[n] for n in WEIGHTS}
    m = {n: given["m_" + n] for n in WEIGHTS}
    v = {n: given["v_" + n] for n in WEIGHTS}
    return _step(x, p, loss_target, w, m, v)
```

```python
import functools
import math

import jax
import jax.numpy as jnp
from jax import lax
from jax.experimental import pallas as pl
from jax.experimental.pallas import tpu as pltpu

F32 = jnp.float32
BF16 = jnp.bfloat16
MESH = pl.DeviceIdType.MESH
ANY = pl.BlockSpec(memory_space=pl.ANY)

N_DEV = 8
EPS = 1e-6
LAMBDA_RE_MAX = -1e-4
SSM_GROUP = 16
SSM_STATE = 64
SSM_SUPER = 16
SGU_CHUNK = 128
ADAM_LR, ADAM_B1, ADAM_B2, ADAM_EPS, ADAM_WD, ADAM_STEP = 0.001, 0.9, 0.999, 1e-08, 0.01, 10
VMEM_LIMIT = 52 * 1024 * 1024
LANE = 128
GATHER_FORWARD_AT = 0.85

_GELU_C = math.sqrt(2.0 / math.pi)


def _params(sem=None):
    return pltpu.CompilerParams(dimension_semantics=sem, vmem_limit_bytes=VMEM_LIMIT)


def _tile(dim, pref, unit=LANE):
    if dim <= pref:
        return dim
    t = (pref // unit) * unit
    while t >= unit:
        if dim % t == 0:
            return t
        t -= unit
    return dim


def _gelu(x):
    return 0.5 * x * (1.0 + jnp.tanh(_GELU_C * (x + 0.044715 * x * x * x)))


def _gelu_grad(x):
    t = jnp.tanh(_GELU_C * (x + 0.044715 * x * x * x))
    return 0.5 * (1.0 + t) + 0.5 * x * (1.0 - t * t) * (_GELU_C * (1.0 + 3.0 * 0.044715 * x * x))


def _gelu_and_grad(x):
    t = jnp.tanh(_GELU_C * (x + 0.044715 * x * x * x))
    return (0.5 * x * (1.0 + t),
            0.5 * (1.0 + t) + 0.5 * x * (1.0 - t * t) * (_GELU_C * (1.0 + 3.0 * 0.044715 * x * x)))


def _rms(x):
    return lax.rsqrt(jnp.mean(x * x, axis=-1, keepdims=True) + EPS)


def _rmsnorm_bwd(dy, x, r, g):
    dyg = dy * g
    return r * dyg - x * (r * r * r) * jnp.mean(dyg * x, axis=-1, keepdims=True)


def _rowsum(v):
    return jnp.sum(v, axis=0, keepdims=True)


class Background:
    def __init__(self, inputs, out_shapes, scratch, phases):
        self.inputs, self.out_shapes, self.scratch, self.phases = list(inputs), list(out_shapes), list(scratch), phases
        self.aliases = {}

    def emit(self, step, nsteps, ins, outs, scratch):
        for place, phase in self.phases:
            at = min(int(place * nsteps), nsteps - 1)

            @pl.when(step == at)
            def _():
                phase(ins, outs, scratch)


def _carrier(bg, n_in, n_out, n_scratch, grid):
    nbi = len(bg.inputs) if bg else 0
    nbo = len(bg.out_shapes) if bg else 0
    nsteps = math.prod(grid)

    def split(refs):
        ins = refs[:n_in]
        bg_ins = refs[n_in:n_in + nbi]
        outs = refs[n_in + nbi:n_in + nbi + n_out]
        bg_outs = refs[n_in + nbi + n_out:n_in + nbi + n_out + nbo]
        rest = refs[n_in + nbi + n_out + nbo:]
        scratch, bg_scratch = rest[:n_scratch], rest[n_scratch:]

        def run_background():
            if bg is None:
                return
            step = pl.program_id(0)
            for axis in range(1, len(grid)):
                step = step * grid[axis] + pl.program_id(axis)
            bg.emit(step, nsteps, bg_ins, bg_outs, bg_scratch)

        return ins, outs, scratch, run_background

    if bg is None:
        return [], [], [], [], [], split
    return [ANY] * nbi, list(bg.inputs), [ANY] * nbo, list(bg.out_shapes), list(bg.scratch), split


def _semantics(bg, sem):
    return tuple("arbitrary" for _ in sem) if bg is not None else sem


def _results(res, n_out, bg):
    res = list(res) if isinstance(res, (list, tuple)) else [res]
    own = res[0] if n_out == 1 else res[:n_out]
    return (own, res[n_out:]) if bg is not None else own


def mm_nn(a, b, *, name, out_dtype, tm, tn, tk, residual=None, bg=None):
    M, K = a.shape
    blocked = b.ndim == 3
    if blocked:
        nb, _, Nb = b.shape
        N = nb * Nb
        tn = _tile(Nb, tn)
        per = Nb // tn
    else:
        N = b.shape[1]
        tn = _tile(N, tn)
    tm, tk = _tile(M, tm, 8), _tile(K, tk)
    nj, ni, nk = N // tn, M // tm, K // tk
    has_res = residual is not None
    grid = (nj, ni, nk)
    bg_in_specs, bg_args, bg_out_specs, bg_out_shapes, bg_scratch, split = _carrier(
        bg, 3 if has_res else 2, 1, 0 if nk == 1 else 1, grid)

    def body(*refs):
        ins, (o_ref,), scratch, run_background = split(refs)
        run_background()
        a_ref, b_ref = ins[0], ins[1]
        r_ref = ins[2] if has_res else None

        def finish(acc):
            if has_res:
                acc = acc + r_ref[...]
            o_ref[...] = acc.astype(o_ref.dtype)

        part = jnp.dot(a_ref[...], b_ref[...], preferred_element_type=F32)
        if nk == 1:
            finish(part)
        else:
            acc_ref = scratch[0]
            k = pl.program_id(2)

            @pl.when(k == 0)
            def _():
                acc_ref[...] = part

            @pl.when(k > 0)
            def _():
                acc_ref[...] += part

            @pl.when(k == nk - 1)
            def _():
                finish(acc_ref[...])

    if blocked:
        b_spec = pl.BlockSpec((None, tk, tn), lambda j, i, k: (j // per, k, j % per))
    else:
        b_spec = pl.BlockSpec((tk, tn), lambda j, i, k: (k, j))
    in_specs = [pl.BlockSpec((tm, tk), lambda j, i, k: (i, k)), b_spec]
    args = [a, b]
    if has_res:
        in_specs.append(pl.BlockSpec((tm, tn), lambda j, i, k: (i, j)))
        args.append(residual)
    res = pl.pallas_call(
        body, name=name, grid=grid,
        in_specs=in_specs + bg_in_specs,
        out_specs=[pl.BlockSpec((tm, tn), lambda j, i, k: (i, j))] + bg_out_specs,
        out_shape=[jax.ShapeDtypeStruct((M, N), out_dtype)] + bg_out_shapes,
        input_output_aliases={len(args) + k: 1 + o for k, o in (bg.aliases if bg else {}).items()},
        scratch_shapes=([] if nk == 1 else [pltpu.VMEM((tm, tn), F32)]) + bg_scratch,
        compiler_params=_params(_semantics(bg, ("parallel", "parallel", "arbitrary"))),
    )(*args, *bg_args)
    return _results(res, 1, bg)


def mm_nt(a, w, *, name, out_dtype, tm, tko, tc, a2=None, bg=None):
    M, N = a.shape
    if a2 is not None:
        N = 2 * N
    blocked = w.ndim == 3
    if blocked:
        nb, Ko, Nb = w.shape
        tc = _tile(Nb, tc)
        per = Nb // tc
    else:
        Ko = w.shape[0]
        tc = _tile(N, tc)
    tm, tko = _tile(M, tm, 8), _tile(Ko, tko)
    njo, ni, nc = Ko // tko, M // tm, N // tc
    grid = (njo, ni, nc)
    half = nc // 2
    bg_in_specs, bg_args, bg_out_specs, bg_out_shapes, bg_scratch, split = _carrier(
        bg, 2 if a2 is None else 3, 1, 0 if nc == 1 else 1, grid)

    def body(*refs):
        ins, (o_ref,), scratch, run_background = split(refs)
        run_background()
        a_val = ins[0][...]
        if a2 is not None:
            a_val = jnp.where(pl.program_id(2) < half, a_val, ins[1][...])
        part = lax.dot_general(a_val, ins[-1][...], (((1,), (1,)), ((), ())),
                               preferred_element_type=F32)
        if nc == 1:
            o_ref[...] = part.astype(o_ref.dtype)
        else:
            acc_ref = scratch[0]
            c = pl.program_id(2)

            @pl.when(c == 0)
            def _():
                acc_ref[...] = part

            @pl.when(c > 0)
            def _():
                acc_ref[...] += part

            @pl.when(c == nc - 1)
            def _():
                o_ref[...] = acc_ref[...].astype(o_ref.dtype)

    if blocked:
        w_spec = pl.BlockSpec((None, tko, tc), lambda j, i, c: (c // per, j, c % per))
    else:
        w_spec = pl.BlockSpec((tko, tc), lambda j, i, c: (j, c))
    if a2 is None:
        a_specs, a_args = [pl.BlockSpec((tm, tc), lambda j, i, c: (i, c))], [a]
    else:
        a_specs = [pl.BlockSpec((tm, tc), lambda j, i, c: (i, jnp.minimum(c, half - 1))),
                   pl.BlockSpec((tm, tc), lambda j, i, c: (i, jnp.maximum(c - half, 0)))]
        a_args = [a, a2]
    res = pl.pallas_call(
        body, name=name, grid=grid,
        in_specs=a_specs + [w_spec] + bg_in_specs,
        out_specs=[pl.BlockSpec((tm, tko), lambda j, i, c: (i, j))] + bg_out_specs,
        out_shape=[jax.ShapeDtypeStruct((M, Ko), out_dtype)] + bg_out_shapes,
        scratch_shapes=([] if nc == 1 else [pltpu.VMEM((tm, tko), F32)]) + bg_scratch,
        compiler_params=_params(_semantics(bg, ("parallel", "parallel", "arbitrary"))),
    )(*a_args, w, *bg_args)
    return _results(res, 1, bg)


def mm_tn(a, g, *, name, out_dtype, tm, tko, tno, out_blocks=None, block_offset=0, total_blocks=None, into=None,
          bg=None):
    M, K = a.shape
    N = g.shape[1]
    if out_blocks:
        Nb = N // out_blocks
        tno = _tile(Nb, tno)
        per = Nb // tno
    else:
        tno = _tile(N, tno)
    tm, tko = _tile(M, tm), _tile(K, tko)
    njo, njn, nm = K // tko, N // tno, M // tm
    grid = (njo, njn, nm)
    bg_in_specs, bg_args, bg_out_specs, bg_out_shapes, bg_scratch, split = _carrier(
        bg, 2 if into is None else 3, 1, 0 if nm == 1 else 1, grid)

    def body(*refs):
        ins, (o_ref,), scratch, run_background = split(refs)
        a_ref, g_ref = ins[0], ins[1]
        run_background()
        part = lax.dot_general(a_ref[...], g_ref[...], (((0,), (0,)), ((), ())),
                               preferred_element_type=F32)
        if nm == 1:
            o_ref[...] = part.astype(o_ref.dtype)
        else:
            acc_ref = scratch[0]
            m = pl.program_id(2)

            @pl.when(m == 0)
            def _():
                acc_ref[...] = part

            @pl.when(m > 0)
            def _():
                acc_ref[...] += part

            @pl.when(m == nm - 1)
            def _():
                o_ref[...] = acc_ref[...].astype(o_ref.dtype)

    if out_blocks:
        o_spec = pl.BlockSpec((None, tko, tno), lambda jo, jn, m: (jn // per + block_offset, jo, jn % per))
        o_shape = jax.ShapeDtypeStruct((total_blocks or out_blocks, K, Nb), out_dtype)
    else:
        o_spec = pl.BlockSpec((tko, tno), lambda jo, jn, m: (jo, jn))
        o_shape = jax.ShapeDtypeStruct((K, N), out_dtype)
    res = pl.pallas_call(
        body, name=name, grid=grid,
        in_specs=[pl.BlockSpec((tm, tko), lambda jo, jn, m: (m, jo)),
                  pl.BlockSpec((tm, tno), lambda jo, jn, m: (m, jn))] + ([] if into is None else [ANY]) + bg_in_specs,
        out_specs=[o_spec] + bg_out_specs, out_shape=[o_shape] + bg_out_shapes,
        scratch_shapes=([] if nm == 1 else [pltpu.VMEM((tko, tno), F32)]) + bg_scratch,
        input_output_aliases={} if into is None else {2: 0},
        compiler_params=_params(_semantics(bg, ("parallel", "parallel", "arbitrary"))),
    )(a, g, *([] if into is None else [into]), *bg_args)
    return _results(res, 1, bg)


def ffn_in_swiglu(h, w_blk, *, name, tm, bg=None):
    M, K = h.shape
    nb, _, Nb = w_blk.shape
    nh = nb // 2
    F = nh * Nb
    tm = _tile(M, tm, 8)
    grid = (nh, M // tm)
    bg_in_specs, bg_args, bg_out_specs, bg_out_shapes, bg_scratch, split = _carrier(bg, 3, 3, 0, grid)

    def body(*refs):
        (h_ref, wg_ref, wu_ref), (act_ref, gate_ref, up_ref), _, run_background = split(refs)
        run_background()
        hv = h_ref[...]
        gate = jnp.dot(hv, wg_ref[...], preferred_element_type=F32)
        up = jnp.dot(hv, wu_ref[...], preferred_element_type=F32)
        gate_ref[...] = gate
        up_ref[...] = up
        act_ref[...] = (gate * jax.nn.sigmoid(gate) * up).astype(act_ref.dtype)

    o_spec = pl.BlockSpec((tm, Nb), lambda j, i: (i, j))
    res = pl.pallas_call(
        body, name=name, grid=grid,
        in_specs=[pl.BlockSpec((tm, K), lambda j, i: (i, 0)),
                  pl.BlockSpec((None, K, Nb), lambda j, i: (j, 0, 0)),
                  pl.BlockSpec((None, K, Nb), lambda j, i: (j + nh, 0, 0))] + bg_in_specs,
        out_specs=[o_spec, o_spec, o_spec] + bg_out_specs,
        out_shape=[jax.ShapeDtypeStruct((M, F), BF16), jax.ShapeDtypeStruct((M, F), F32),
                   jax.ShapeDtypeStruct((M, F), F32)] + bg_out_shapes,
        scratch_shapes=bg_scratch,
        compiler_params=_params(_semantics(bg, ("parallel", "parallel"))),
    )(h, w_blk, w_blk, *bg_args)
    return _results(res, 3, bg)


def _row_spec(tm, d, col=0):
    return pl.BlockSpec((tm, d), lambda i: (i, col))


def _vec_spec(d):
    return pl.BlockSpec((1, d), lambda i: (0, 0))


def norm_fwd(x, g, *, name, tm=256):
    L, D = x.shape
    tm = _tile(L, tm, 8)

    def body(x_ref, g_ref, h_ref):
        xv = x_ref[...]
        h_ref[...] = (xv * _rms(xv) * g_ref[...]).astype(h_ref.dtype)

    return pl.pallas_call(
        body, name=name, grid=(L // tm,),
        in_specs=[_row_spec(tm, D), _vec_spec(D)],
        out_specs=_row_spec(tm, D),
        out_shape=jax.ShapeDtypeStruct((L, D), BF16),
        compiler_params=_params(("parallel",)),
    )(x, g)


def norm_bwd(dh, xin, g, dres, *, name, want_bf16, tm=128):
    L, D = xin.shape
    tm = _tile(L, tm, 8)

    def body(dh_ref, x_ref, g_ref, dres_ref, dx_ref, *rest):
        dg_ref = rest[-1]
        xv, dhv = x_ref[...], dh_ref[...]
        r = _rms(xv)
        dx = dres_ref[...] + _rmsnorm_bwd(dhv, xv, r, g_ref[...])
        dx_ref[...] = dx
        if want_bf16:
            rest[0][...] = dx.astype(BF16)
        part = _rowsum(dhv * xv * r)

        @pl.when(pl.program_id(0) == 0)
        def _():
            dg_ref[...] = part

        @pl.when(pl.program_id(0) > 0)
        def _():
            dg_ref[...] += part

    out_specs = [_row_spec(tm, D)] + ([_row_spec(tm, D)] if want_bf16 else []) + [_vec_spec(D)]
    out_shape = ([jax.ShapeDtypeStruct((L, D), F32)]
                 + ([jax.ShapeDtypeStruct((L, D), BF16)] if want_bf16 else [])
                 + [jax.ShapeDtypeStruct((1, D), F32)])
    return pl.pallas_call(
        body, name=name, grid=(L // tm,),
        in_specs=[_row_spec(tm, D), _row_spec(tm, D), _vec_spec(D), _row_spec(tm, D)],
        out_specs=out_specs, out_shape=out_shape,
        compiler_params=_params(("arbitrary",)),
    )(dh, xin, g, dres)


def glu_pre(y0, *, name, tm=256):
    L, D = y0.shape
    tm = _tile(L, tm, 8)

    def body(y_ref, o_ref):
        o_ref[...] = _gelu(y_ref[...]).astype(o_ref.dtype)

    return pl.pallas_call(
        body, name=name, grid=(L // tm,),
        in_specs=[_row_spec(tm, D)], out_specs=_row_spec(tm, D),
        out_shape=jax.ShapeDtypeStruct((L, D), BF16),
        compiler_params=_params(("parallel",)),
    )(y0)


def glu_post(y0, t, b_glu, g_a, *, name, tm=256):
    L, D = y0.shape
    tm = _tile(L, tm, 8)

    def body(y_ref, t_ref, b_ref, g_ref, o_ref):
        ya = _gelu(y_ref[...]) * jax.nn.sigmoid(t_ref[...] + b_ref[...])
        o_ref[...] = (ya * _rms(ya) * g_ref[...]).astype(o_ref.dtype)

    return pl.pallas_call(
        body, name=name, grid=(L // tm,),
        in_specs=[_row_spec(tm, D), _row_spec(tm, D), _vec_spec(D), _vec_spec(D)],
        out_specs=_row_spec(tm, D),
        out_shape=jax.ShapeDtypeStruct((L, D), BF16),
        compiler_params=_params(("parallel",)),
    )(y0, t, b_glu, g_a)


def glu_post_bwd(y0, t, b_glu, g_a, dycat, *, name, tm=128):
    L, D = y0.shape
    tm = _tile(L, tm, 8)

    def body(y_ref, t_ref, b_ref, g_ref, dn_ref, dt_ref, dd_ref, dga_ref, dbg_ref):
        ya1 = _gelu(y_ref[...])
        sg = jax.nn.sigmoid(t_ref[...] + b_ref[...])
        ya = ya1 * sg
        ra = _rms(ya)
        dn = dn_ref[...]
        dya = _rmsnorm_bwd(dn, ya, ra, g_ref[...])
        dt = dya * ya1 * sg * (1.0 - sg)
        dt_ref[...] = dt.astype(BF16)
        dd_ref[...] = dya * sg
        p_ga, p_bg = _rowsum(dn * ya * ra), _rowsum(dt)

        @pl.when(pl.program_id(0) == 0)
        def _():
            dga_ref[...] = p_ga
            dbg_ref[...] = p_bg

        @pl.when(pl.program_id(0) > 0)
        def _():
            dga_ref[...] += p_ga
            dbg_ref[...] += p_bg

    return pl.pallas_call(
        body, name=name, grid=(L // tm,),
        in_specs=[_row_spec(tm, D), _row_spec(tm, D), _vec_spec(D), _vec_spec(D), _row_spec(tm, D, 0)],
        out_specs=[_row_spec(tm, D), _row_spec(tm, D), _vec_spec(D), _vec_spec(D)],
        out_shape=[jax.ShapeDtypeStruct((L, D), BF16), jax.ShapeDtypeStruct((L, D), F32),
                   jax.ShapeDtypeStruct((1, D), F32), jax.ShapeDtypeStruct((1, D), F32)],
        compiler_params=_params(("arbitrary",)),
    )(y0, t, b_glu, g_a, dycat)


def head_and_loss(x2, gpre, b_g, pp, g_f, tgt, *, name, tm=128):
    L, D = x2.shape
    tm = _tile(L, tm, 8)

    def body(x2_ref, gp_ref, bg_ref, pp_ref, gf_ref, tg_ref,
             dx3_ref, dpre_ref, dpp_ref, loss_ref, dgf_ref, dbg_ref):
        gate = jax.nn.sigmoid(gp_ref[...] + bg_ref[...])
        ppv = pp_ref[...]
        x3 = x2_ref[...] + gate * ppv
        r = _rms(x3)
        xn = x3 * r
        gf = gf_ref[...]
        err = xn * gf - tg_ref[...]
        loss = 0.5 * jnp.sum(jnp.mean(err * err, axis=-1, keepdims=True), axis=0, keepdims=True)
        dout = err * (1.0 / D)
        dx3 = _rmsnorm_bwd(dout, x3, r, gf)
        dx3_ref[...] = dx3
        dpre = dx3 * ppv * gate * (1.0 - gate)
        dpre_ref[...] = dpre.astype(BF16)
        dpp_ref[...] = (dx3 * gate).astype(BF16)
        p_gf, p_bg = _rowsum(dout * xn), _rowsum(dpre)
        p_loss = jnp.broadcast_to(loss, loss_ref.shape)

        @pl.when(pl.program_id(0) == 0)
        def _():
            loss_ref[...] = p_loss
            dgf_ref[...] = p_gf
            dbg_ref[...] = p_bg

        @pl.when(pl.program_id(0) > 0)
        def _():
            loss_ref[...] += p_loss
            dgf_ref[...] += p_gf
            dbg_ref[...] += p_bg

    rs = _row_spec(tm, D)
    return pl.pallas_call(
        body, name=name, grid=(L // tm,),
        in_specs=[rs, rs, _vec_spec(D), rs, _vec_spec(D), rs],
        out_specs=[rs, rs, rs, pl.BlockSpec((8, LANE), lambda i: (0, 0)), _vec_spec(D), _vec_spec(D)],
        out_shape=[jax.ShapeDtypeStruct((L, D), F32), jax.ShapeDtypeStruct((L, D), BF16),
                   jax.ShapeDtypeStruct((L, D), BF16), jax.ShapeDtypeStruct((8, LANE), F32),
                   jax.ShapeDtypeStruct((1, D), F32), jax.ShapeDtypeStruct((1, D), F32)],
        compiler_params=_params(("arbitrary",)),
    )(x2, gpre, b_g, pp, g_f, tgt)


def ffn_out_bwd_swiglu(dx, w, gate, up, *, name, tm=512, tf=1408):
    M, D = dx.shape
    F = w.shape[0]
    tm, tf = _tile(M, tm, 8), _tile(F, tf)

    def body(dx_ref, w_ref, g_ref, u_ref, dg_ref, du_ref):
        da = lax.dot_general(dx_ref[...], w_ref[...], (((1,), (1,)), ((), ())), preferred_element_type=F32)
        gv = g_ref[...]
        sg = jax.nn.sigmoid(gv)
        dg_ref[...] = (da * u_ref[...] * sg * (1.0 + gv * (1.0 - sg))).astype(BF16)
        du_ref[...] = (da * gv * sg).astype(BF16)

    spec = pl.BlockSpec((tm, tf), lambda j, i: (i, j))
    return pl.pallas_call(
        body, name=name, grid=(F // tf, M // tm),
        in_specs=[pl.BlockSpec((tm, D), lambda j, i: (i, 0)), pl.BlockSpec((tf, D), lambda j, i: (j, 0)), spec, spec],
        out_specs=[spec, spec],
        out_shape=[jax.ShapeDtypeStruct((M, F), BF16), jax.ShapeDtypeStruct((M, F), BF16)],
        compiler_params=_params(("parallel", "parallel")),
    )(dx, w, gate, up)


def _sgu_forward_values(u1, v1, lng, lnb, w_ref, bs_ref, s_scr, heads, hd):
    xc = v1 - jnp.mean(v1, axis=-1, keepdims=True)
    r = lax.rsqrt(jnp.mean(xc * xc, axis=-1, keepdims=True) + EPS)
    xhat = xc * r
    v2 = xhat * lng + lnb
    tril = (lax.broadcasted_iota(jnp.int32, (SGU_CHUNK, SGU_CHUNK), 0)
            >= lax.broadcasted_iota(jnp.int32, (SGU_CHUNK, SGU_CHUNK), 1))
    for h in range(heads):
        wm = jnp.where(tril, w_ref[h], 0.0).astype(BF16)
        cols = slice(h * hd, (h + 1) * hd)
        s_scr[:, cols] = jnp.dot(wm, v2[:, cols].astype(BF16), preferred_element_type=F32) + bs_ref[h]
    return xhat, r, v2, tril


def sgu_fwd(z, ln_g, ln_b, w_s, b_s, g_b, *, name, d_sgu):
    L = z.shape[0]
    heads = w_s.shape[0]
    hd = d_sgu // heads

    def body(zu_ref, zv_ref, lng_ref, lnb_ref, w_ref, bs_ref, gb_ref, o_ref, s_scr):
        u1 = _gelu(zu_ref[...])
        _sgu_forward_values(u1, _gelu(zv_ref[...]), lng_ref[...], lnb_ref[...], w_ref, bs_ref, s_scr, heads, hd)
        yb = u1 * s_scr[...]
        o_ref[...] = (yb * _rms(yb) * gb_ref[...]).astype(o_ref.dtype)

    blk = lambda col: pl.BlockSpec((SGU_CHUNK, d_sgu), lambda n: (n, col))
    return pl.pallas_call(
        body, name=name, grid=(L // SGU_CHUNK,),
        in_specs=[blk(1), blk(2), _vec_spec(d_sgu), _vec_spec(d_sgu),
                  pl.BlockSpec(w_s.shape, lambda n: (0, 0, 0)), pl.BlockSpec(b_s.shape, lambda n: (0, 0, 0)),
                  _vec_spec(d_sgu)],
        out_specs=blk(0),
        out_shape=jax.ShapeDtypeStruct((L, d_sgu), BF16),
        scratch_shapes=[pltpu.VMEM((SGU_CHUNK, d_sgu), F32)],
        compiler_params=_params(("parallel",)),
    )(z, z, ln_g, ln_b, w_s, b_s, g_b)


def sgu_bwd(z, dycat, ln_g, ln_b, w_s, b_s, g_b, *, name, d_sgu):
    L = z.shape[0]
    heads = w_s.shape[0]
    hd = d_sgu // heads

    def body(zu_ref, zv_ref, dn_ref, lng_ref, lnb_ref, w_ref, bs_ref, gb_ref,
             dzu_ref, dzv_ref, dw_ref, dbs_ref, dlng_ref, dlnb_ref, dgb_ref, s_scr, dv_scr):
        first = pl.program_id(0) == 0
        lng = lng_ref[...]
        u1, du1 = _gelu_and_grad(zu_ref[...])
        v1, dv1_dz = _gelu_and_grad(zv_ref[...])
        xhat, r, v2, tril = _sgu_forward_values(u1, v1, lng, lnb_ref[...], w_ref, bs_ref, s_scr, heads, hd)
        s = s_scr[...]
        yb = u1 * s
        rb = _rms(yb)
        dn = dn_ref[...]
        dyb = _rmsnorm_bwd(dn, yb, rb, gb_ref[...])
        dzu_ref[...] = (dyb * s * du1).astype(BF16)
        ds = dyb * u1
        for h in range(heads):
            cols = slice(h * hd, (h + 1) * hd)
            ds_h = ds[:, cols]
            ds_hb = ds_h.astype(BF16)
            wm = jnp.where(tril, w_ref[h], 0.0).astype(BF16)
            dw_h = jnp.where(tril, lax.dot_general(ds_hb, v2[:, cols].astype(BF16), (((1,), (1,)), ((), ())),
                                                   preferred_element_type=F32), 0.0)
            db_h = jnp.sum(ds_h.T, axis=0, keepdims=True)
            dv_scr[:, cols] = lax.dot_general(wm, ds_hb, (((0,), (0,)), ((), ())), preferred_element_type=F32)

            @pl.when(first)
            def _():
                dw_ref[h] = dw_h
                dbs_ref[h] = db_h

            @pl.when(jnp.logical_not(first))
            def _():
                dw_ref[h] += dw_h
                dbs_ref[h] += db_h

        dv2 = dv_scr[...]
        dxh = dv2 * lng
        dv1 = r * (dxh - jnp.mean(dxh, axis=-1, keepdims=True)
                   - xhat * jnp.mean(dxh * xhat, axis=-1, keepdims=True))
        dzv_ref[...] = (dv1 * dv1_dz).astype(BF16)
        p_lng, p_lnb, p_gb = _rowsum(dv2 * xhat), _rowsum(dv2), _rowsum(dn * yb * rb)

        @pl.when(first)
        def _():
            dlng_ref[...] = p_lng
            dlnb_ref[...] = p_lnb
            dgb_ref[...] = p_gb

        @pl.when(jnp.logical_not(first))
        def _():
            dlng_ref[...] += p_lng
            dlnb_ref[...] += p_lnb
            dgb_ref[...] += p_gb

    blk = lambda col: pl.BlockSpec((SGU_CHUNK, d_sgu), lambda n: (n, col))
    full3 = lambda shape: pl.BlockSpec(shape, lambda n: (0, 0, 0))
    return pl.pallas_call(
        body, name=name, grid=(L // SGU_CHUNK,),
        in_specs=[blk(1), blk(2), blk(1), _vec_spec(d_sgu), _vec_spec(d_sgu),
                  full3(w_s.shape), full3(b_s.shape), _vec_spec(d_sgu)],
        out_specs=[blk(0), blk(0), full3(w_s.shape), full3((heads, 1, SGU_CHUNK)),
                   _vec_spec(d_sgu), _vec_spec(d_sgu), _vec_spec(d_sgu)],
        out_shape=[jax.ShapeDtypeStruct((L, d_sgu), BF16), jax.ShapeDtypeStruct((L, d_sgu), BF16),
                   jax.ShapeDtypeStruct(w_s.shape, F32), jax.ShapeDtypeStruct((heads, 1, SGU_CHUNK), F32),
                   jax.ShapeDtypeStruct((1, d_sgu), F32), jax.ShapeDtypeStruct((1, d_sgu), F32),
                   jax.ShapeDtypeStruct((1, d_sgu), F32)],
        scratch_shapes=[pltpu.VMEM((SGU_CHUNK, d_sgu), F32), pltpu.VMEM((SGU_CHUNK, d_sgu), F32)],
        compiler_params=_params(("arbitrary",)),
    )(z, z, dycat, ln_g, ln_b, w_s, b_s, g_b)


def _disc_lambda(lam_re, lam_im, log_step):
    lr = jnp.minimum(lam_re, LAMBDA_RE_MAX)
    li = lam_im
    dt = jnp.exp(log_step)
    mag = jnp.exp(lr * dt)
    ang = li * dt
    a_re = mag * jnp.cos(ang)
    a_im = mag * jnp.sin(ang)
    nr = a_re - 1.0
    ni = a_im
    den = lr * lr + li * li
    return a_re, a_im, (nr * lr + ni * li) / den, (ni * lr - nr * li) / den


def _disc_b(q_re, q_im, b_re, b_im):
    return q_re * b_re - q_im * b_im, q_re * b_im + q_im * b_re


def disc_lambda_fwd(lam_re, lam_im, log_step, *, name):
    def body(lr_ref, li_ref, ls_ref, ar_ref, ai_ref, qr_ref, qi_ref):
        ar_ref[...], ai_ref[...], qr_ref[...], qi_ref[...] = _disc_lambda(lr_ref[...], li_ref[...], ls_ref[...])

    sd = jax.ShapeDtypeStruct(lam_re.shape, F32)
    return pl.pallas_call(body, name=name, out_shape=[sd, sd, sd, sd], compiler_params=_params())(
        lam_re, lam_im, log_step)


def disc_lambda_bwd(lam_re, lam_im, log_step, cts, *, name):
    def body(lr_ref, li_ref, ls_ref, c0, c1, c2, c3, dlr_ref, dli_ref, dls_ref):
        _, vjp = jax.vjp(_disc_lambda, lr_ref[...], li_ref[...], ls_ref[...])
        dlr_ref[...], dli_ref[...], dls_ref[...] = vjp((c0[...], c1[...], c2[...], c3[...]))

    sd = jax.ShapeDtypeStruct(lam_re.shape, F32)
    return pl.pallas_call(body, name=name, out_shape=[sd, sd, jax.ShapeDtypeStruct(log_step.shape, F32)],
                          compiler_params=_params())(lam_re, lam_im, log_step, *cts)


def disc_b_fwd(q_re, q_im, b_re, b_im, *, name):
    def body(qr_ref, qi_ref, br_ref, bi_ref, or_ref, oi_ref):
        or_ref[...], oi_ref[...] = _disc_b(qr_ref[...], qi_ref[...], br_ref[...], bi_ref[...])

    sd = jax.ShapeDtypeStruct(b_re.shape, F32)
    return pl.pallas_call(body, name=name, out_shape=[sd, sd], compiler_params=_params())(q_re, q_im, b_re, b_im)


def disc_b_bwd(q_re, q_im, b_re, b_im, ct_re, ct_im, *, name):
    def body(qr_ref, qi_ref, br_ref, bi_ref, cr_ref, ci_ref, dqr_ref, dqi_ref, dbr_ref, dbi_ref):
        _, vjp = jax.vjp(_disc_b, qr_ref[...], qi_ref[...], br_ref[...], bi_ref[...])
        dqr_ref[...], dqi_ref[...], dbr_ref[...], dbi_ref[...] = vjp((cr_ref[...], ci_ref[...]))

    sq, sb = jax.ShapeDtypeStruct(q_re.shape, F32), jax.ShapeDtypeStruct(b_re.shape, F32)
    return pl.pallas_call(body, name=name, out_shape=[sq, sq, sb, sb], compiler_params=_params())(
        q_re, q_im, b_re, b_im, ct_re, ct_im)


def _lti_scan(xr, xi, ar, ai, reverse):
    T = xr.shape[0]
    row = lax.broadcasted_iota(jnp.int32, xr.shape, 0)
    k = 1
    while k < T:
        shift = T - k if reverse else k
        keep = (row < T - k) if reverse else (row >= k)
        sr = jnp.where(keep, pltpu.roll(xr, shift, 0), 0.0)
        si = jnp.where(keep, pltpu.roll(xi, shift, 0), 0.0)
        xr, xi = xr + ar * sr - ai * si, xi + ar * si + ai * sr
        ar, ai = ar * ar - ai * ai, 2.0 * ar * ai
        k *= 2
    return xr, xi


SUBLANES = 8


def _scan_rows(x_re, x_im, o_re, o_im, ar, ai, cr, ci, reverse):
    T, n = x_re.shape
    groups = T // SUBLANES
    row = lax.broadcasted_iota(jnp.int32, (SUBLANES, n), 0)
    edge = SUBLANES - 1 if reverse else 0
    pr, pi = _lti_scan(jnp.where(row == edge, ar, 0.0), jnp.where(row == edge, ai, 0.0), ar, ai, reverse)
    pows = []
    for level in range(3):
        k = 1 << level
        keep = (row < SUBLANES - k) if reverse else (row >= k)
        pows.append((jnp.where(keep, ar, 0.0), jnp.where(keep, ai, 0.0)))
        ar, ai = ar * ar - ai * ai, 2.0 * ar * ai

    def group(i, carry):
        cr, ci = carry
        at = pl.multiple_of((groups - 1 - i if reverse else i) * SUBLANES, SUBLANES)
        xr, xi = x_re[pl.ds(at, SUBLANES), :], x_im[pl.ds(at, SUBLANES), :]
        for level, (qr, qi) in enumerate(pows):
            shift = SUBLANES - (1 << level) if reverse else 1 << level
            sr, si = pltpu.roll(xr, shift, 0), pltpu.roll(xi, shift, 0)
            xr, xi = xr + qr * sr - qi * si, xi + qr * si + qi * sr
        xr, xi = xr + pr * cr - pi * ci, xi + pr * ci + pi * cr
        o_re[pl.ds(at, SUBLANES), :] = xr
        o_im[pl.ds(at, SUBLANES), :] = xi
        last = 0 if reverse else SUBLANES - 1
        return xr[last:last + 1, :], xi[last:last + 1, :]

    return lax.fori_loop(0, groups, group, (cr, ci), unroll=2)


def _ssm_chunk(L):
    return _tile(L, 256, 8)


def ssm_fwd(z, bs_re, bs_im, cs_re, cs_im, a_re, a_im, d, *, name, bg=None):
    L = z.shape[0]
    NK, C, S = bs_re.shape
    T = _ssm_chunk(L)
    grid = (NK, L // T)
    bg_in_specs, bg_args, bg_out_specs, bg_out_shapes, bg_scratch, split = _carrier(bg, 8, 3, 4, grid)

    def body(*refs):
        ((u_ref, br_ref, bi_ref, cr_ref, ci_ref, ar_ref, ai_ref, d_ref), (y_ref, sr_ref, si_ref),
         (car_re, car_im, bu_re, bu_im), run_background) = split(refs)
        run_background()
        i = pl.program_id(1)
        ar, ai = ar_ref[...], ai_ref[...]

        @pl.when(i == 0)
        def _():
            car_re[...] = jnp.zeros_like(car_re)
            car_im[...] = jnp.zeros_like(car_im)

        u = u_ref[...]
        ub = u.astype(BF16)
        bu_re[...] = jnp.dot(ub, br_ref[...], preferred_element_type=F32)
        bu_im[...] = jnp.dot(ub, bi_ref[...], preferred_element_type=F32)
        car_re[...], car_im[...] = _scan_rows(bu_re, bu_im, sr_ref, si_ref, ar, ai, car_re[...], car_im[...], False)
        y_ref[...] = (jnp.dot(sr_ref[...].astype(BF16), cr_ref[...], preferred_element_type=F32)
                      - jnp.dot(si_ref[...].astype(BF16), ci_ref[...], preferred_element_type=F32)
                      + d_ref[...] * u)

    kspec = lambda shape: pl.BlockSpec((None,) + shape, lambda k, i: (k, 0, 0))
    res = pl.pallas_call(
        body, name=name, grid=grid,
        in_specs=[pl.BlockSpec((T, C), lambda k, i: (i, k)),
                  kspec((C, S)), kspec((C, S)), kspec((S, C)), kspec((S, C)),
                  kspec((1, S)), kspec((1, S)), kspec((1, C))] + bg_in_specs,
        out_specs=[pl.BlockSpec((T, C), lambda k, i: (i, k)),
                   pl.BlockSpec((T, S), lambda k, i: (i, k)), pl.BlockSpec((T, S), lambda k, i: (i, k))] + bg_out_specs,
        out_shape=[jax.ShapeDtypeStruct((L, NK * C), F32), jax.ShapeDtypeStruct((L, NK * S), F32),
                   jax.ShapeDtypeStruct((L, NK * S), F32)] + bg_out_shapes,
        scratch_shapes=[pltpu.VMEM((1, S), F32), pltpu.VMEM((1, S), F32),
                        pltpu.VMEM((T, S), F32), pltpu.VMEM((T, S), F32)] + bg_scratch,
        compiler_params=_params(_semantics(bg, ("parallel", "arbitrary"))),
    )(z, bs_re, bs_im, cs_re, cs_im, a_re, a_im, d, *bg_args)
    return _results(res, 3, bg)


def ssm_bwd(z, y0, dd_direct, dd_mm, s_re, s_im, bs_re, bs_im, cs_re, cs_im, a_re, a_im, d, *, name, bg=None):
    L = z.shape[0]
    NK, C, S = bs_re.shape
    T = _ssm_chunk(L)
    nchunk = L // T
    tail = T // 8
    grid = (NK, nchunk)
    bg_in_specs, bg_args, bg_out_specs, bg_out_shapes, bg_scratch, split = _carrier(bg, 15, 8, 4, grid)
    nt_dot = lambda p, q: lax.dot_general(p, q, (((1,), (1,)), ((), ())), preferred_element_type=F32)

    def body(*refs):
        ((u_ref, y_ref, d1_ref, d2_ref, sr_ref, si_ref, pr_ref, pi_ref,
          br_ref, bi_ref, cr_ref, ci_ref, ar_ref, ai_ref, d_ref),
         (du_ref, dbr_ref, dbi_ref, dcr_ref, dci_ref, dar_ref, dai_ref, dd_ref),
         (car_re, car_im, lam_re, lam_im), run_background) = split(refs)
        run_background()
        i = pl.program_id(1)
        chunk = nchunk - 1 - i
        ar, ai = ar_ref[...], ai_ref[...]
        row = lax.broadcasted_iota(jnp.int32, (T, S), 0)

        @pl.when(i == 0)
        def _():
            car_re[...] = jnp.zeros_like(car_re)
            car_im[...] = jnp.zeros_like(car_im)

        u = u_ref[...]
        dy = (d1_ref[...] + d2_ref[...]) * _gelu_grad(y_ref[...])
        dyb = dy.astype(BF16)
        lam_re[...] = nt_dot(dyb, cr_ref[...])
        lam_im[...] = -nt_dot(dyb, ci_ref[...])
        car_re[...], car_im[...] = _scan_rows(lam_re, lam_im, lam_re, lam_im, ar, -ai, car_re[...], car_im[...], True)
        lr, li = lam_re[...], lam_im[...]

        s_re, s_im = sr_ref[...], si_ref[...]
        has_prev = (chunk > 0).astype(F32)
        prev_re = pr_ref[7:8, :] * has_prev
        prev_im = pi_ref[7:8, :] * has_prev
        sp_re = jnp.where(row == 0, prev_re, pltpu.roll(s_re, 1, 0))
        sp_im = jnp.where(row == 0, prev_im, pltpu.roll(s_im, 1, 0))
        p_ar = _rowsum(lr * sp_re + li * sp_im)
        p_ai = _rowsum(li * sp_re - lr * sp_im)

        lrb, lib, ub = lr.astype(BF16), li.astype(BF16), u.astype(BF16)
        du = dy * d_ref[...] + nt_dot(lrb, br_ref[...]) + nt_dot(lib, bi_ref[...])
        du_ref[...] = du.astype(BF16)
        tdot = lambda p, q: lax.dot_general(p, q, (((0,), (0,)), ((), ())), preferred_element_type=F32)
        p_br, p_bi = tdot(ub, lrb), tdot(ub, lib)
        p_cr, p_ci = tdot(s_re.astype(BF16), dyb), -tdot(s_im.astype(BF16), dyb)
        p_dd = _rowsum(dy * u)

        @pl.when(i == 0)
        def _():
            dar_ref[...] = p_ar
            dai_ref[...] = p_ai
            dbr_ref[...] = p_br
            dbi_ref[...] = p_bi
            dcr_ref[...] = p_cr
            dci_ref[...] = p_ci
            dd_ref[...] = p_dd

        @pl.when(i > 0)
        def _():
            dar_ref[...] += p_ar
            dai_ref[...] += p_ai
            dbr_ref[...] += p_br
            dbi_ref[...] += p_bi
            dcr_ref[...] += p_cr
            dci_ref[...] += p_ci
            dd_ref[...] += p_dd

    rev = lambda k, i: (nchunk - 1 - i, k)
    prev = lambda k, i: (jnp.maximum((nchunk - 1 - i) * tail - 1, 0), k)
    kspec = lambda shape: pl.BlockSpec((None,) + shape, lambda k, i: (k, 0, 0))
    res = pl.pallas_call(
        body, name=name, grid=grid,
        in_specs=[pl.BlockSpec((T, C), rev), pl.BlockSpec((T, C), rev), pl.BlockSpec((T, C), rev),
                  pl.BlockSpec((T, C), rev), pl.BlockSpec((T, S), rev), pl.BlockSpec((T, S), rev),
                  pl.BlockSpec((8, S), prev), pl.BlockSpec((8, S), prev),
                  kspec((C, S)), kspec((C, S)), kspec((S, C)), kspec((S, C)),
                  kspec((1, S)), kspec((1, S)), kspec((1, C))] + bg_in_specs,
        out_specs=[pl.BlockSpec((T, C), rev), kspec((C, S)), kspec((C, S)), kspec((S, C)), kspec((S, C)),
                   kspec((1, S)), kspec((1, S)), kspec((1, C))] + bg_out_specs,
        out_shape=[jax.ShapeDtypeStruct((L, NK * C), BF16),
                   jax.ShapeDtypeStruct((NK, C, S), F32), jax.ShapeDtypeStruct((NK, C, S), F32),
                   jax.ShapeDtypeStruct((NK, S, C), F32), jax.ShapeDtypeStruct((NK, S, C), F32),
                   jax.ShapeDtypeStruct((NK, 1, S), F32), jax.ShapeDtypeStruct((NK, 1, S), F32),
                   jax.ShapeDtypeStruct((NK, 1, C), F32)] + bg_out_shapes,
        scratch_shapes=[pltpu.VMEM((1, S), F32), pltpu.VMEM((1, S), F32),
                        pltpu.VMEM((T, S), F32), pltpu.VMEM((T, S), F32)] + bg_scratch,
        compiler_params=_params(_semantics(bg, ("parallel", "arbitrary"))),
    )(z, y0, dd_direct, dd_mm, s_re, s_im, s_re, s_im, bs_re, bs_im, cs_re, cs_im, a_re, a_im, d, *bg_args)
    return _results(res, 8, bg)


def _block_diag(v):
    NK, SG, R, Q = v.shape
    eye = jnp.eye(SG, dtype=v.dtype)
    return (v[:, :, :, None, :] * eye[None, :, None, :, None]).reshape(NK, SG * R, SG * Q)


def _block_diag_part(m, SG):
    NK, RR, QQ = m.shape
    R, Q = RR // SG, QQ // SG
    eye = jnp.eye(SG, dtype=m.dtype)
    return jnp.sum(m.reshape(NK, SG, R, SG, Q) * eye[None, :, None, :, None], axis=3)


def _position():
    return lax.axis_index("x"), lax.axis_index("y"), lax.axis_index("c")


def _other_chips(x, y):
    return [(1 - x, y), (x, 1 - y), (1 - x, 1 - y)]


def _gather_phases(n, rows=None):
    def parts(ins, outs, sems):
        send_sems, recv_sems, local_sems = sems
        x, y, c = _position()
        me, sibling = (x, y, c), (x, y, 1 - c)
        chips = _other_chips(x, y)

        def block(a, pos):
            index = 4 * pos[0] + 2 * pos[1] + pos[2]
            if rows is not None:
                return outs[a].at[index, pl.ds(*rows)]
            return outs[a].at[pl.ds(index, 1)] if _is_row(ins[a]) else outs[a].at[index]

        def copy(a, k, pos, to, src=None):
            return pltpu.make_async_remote_copy(
                src_ref=block(a, pos) if src is None else src, dst_ref=block(a, pos),
                send_sem=send_sems.at[7 * a + k], recv_sem=recv_sems.at[7 * a + k],
                device_id=to, device_id_type=MESH)

        shard = [ins[a] if rows is None else ins[a].at[pl.ds(*rows)] for a in range(n)]
        mine = [pltpu.make_async_copy(shard[a], block(a, me), local_sems.at[a]) for a in range(n)]
        first = []
        for a in range(n):
            first.append(copy(a, 0, me, sibling, src=shard[a]))
            first += [copy(a, 1 + j, me, (*chip, c), src=shard[a]) for j, chip in enumerate(chips)]
        passed = [copy(a, 4 + j, (*chip, c), sibling) for a in range(n) for j, chip in enumerate(chips)]
        arrived = [copy(a, 1 + j, (*chip, c), me) for a in range(n) for j, chip in enumerate(chips)]
        from_sibling = []
        for a in range(n):
            from_sibling.append(copy(a, 0, sibling, me))
            from_sibling += [copy(a, 4 + j, (*chip, 1 - c), me) for j, chip in enumerate(chips)]
        return mine, first, passed, arrived, from_sibling

    def send(ins, outs, sems):
        mine, first, _, _, _ = parts(ins, outs, sems)
        for cp in mine + first:
            cp.start()

    def forward(ins, outs, sems):
        _, _, passed, arrived, _ = parts(ins, outs, sems)
        for got, fwd in zip(arrived, passed):
            got.wait_recv()
            fwd.start()

    def finish(ins, outs, sems):
        mine, first, passed, _, from_sibling = parts(ins, outs, sems)
        for cp in from_sibling:
            cp.wait_recv()
        for cp in first + passed:
            cp.wait_send()
        for cp in mine:
            cp.wait()

    return [(0.0, send), (GATHER_FORWARD_AT, forward), (1.0, finish)]


def _is_row(a):
    return len(a.shape) == 2 and a.shape[0] == 1


def _gather_shapes(shards):
    n = len(shards)
    return ([jax.ShapeDtypeStruct((N_DEV,) + (s.shape[1:] if _is_row(s) else s.shape), s.dtype) for s in shards],
            [pltpu.SemaphoreType.DMA((7 * n,)), pltpu.SemaphoreType.DMA((7 * n,)), pltpu.SemaphoreType.DMA((n,))])


def gather_background(shards, rows=None, into=None):
    out_shapes, scratch = _gather_shapes(shards)
    bg = Background(list(shards) + list(into or []), out_shapes, scratch, _gather_phases(len(shards), rows))
    bg.aliases = {len(shards) + k: k for k in range(len(into or []))}
    return bg


def all_gather_blocks(shards, *, name):
    n = len(shards)
    out_shapes, scratch = _gather_shapes(shards)

    def body(*refs):
        for _, phase in _gather_phases(n):
            phase(refs[:n], refs[n:2 * n], refs[2 * n:])

    return pl.pallas_call(
        body, name=name, in_specs=[ANY] * n, out_specs=[ANY] * n, out_shape=out_shapes, scratch_shapes=scratch,
    )(*shards)


def sibling_exchange(grads, *, name):
    n = len(grads)

    def body(*refs):
        ins, outs = refs[:n], refs[n:2 * n]
        send_sems, recv_sems = refs[2 * n:]
        x, y, c = _position()
        copies = []
        for a in range(n):
            for q in range(4):
                copies.append(pltpu.make_async_remote_copy(
                    src_ref=ins[a].at[2 * q + 1 - c], dst_ref=outs[a].at[q],
                    send_sem=send_sems.at[4 * a + q], recv_sem=recv_sems.at[4 * a + q],
                    device_id=(x, y, 1 - c), device_id_type=MESH))
        for cp in copies:
            cp.start()
        for cp in copies:
            cp.wait()

    return pl.pallas_call(
        body, name=name,
        in_specs=[ANY] * n, out_specs=[ANY] * n,
        out_shape=[jax.ShapeDtypeStruct((4,) + g.shape[1:], g.dtype) for g in grads],
        scratch_shapes=[pltpu.SemaphoreType.DMA((4 * n,)), pltpu.SemaphoreType.DMA((4 * n,))],
    )(*grads)


def _chip_exchange_phases(n):
    def copies(ins, outs, sems):
        x, y, c = _position()
        return [pltpu.make_async_remote_copy(
            src_ref=ins[a].at[2 * chip[0] + chip[1]], dst_ref=outs[a].at[j],
            send_sem=sems[0].at[3 * a + j], recv_sem=sems[1].at[3 * a + j],
            device_id=(*chip, c), device_id_type=MESH)
            for a in range(n) for j, chip in enumerate(_other_chips(x, y))]

    def send(ins, outs, sems):
        for cp in copies(ins, outs, sems):
            cp.start()

    def finish(ins, outs, sems):
        for cp in copies(ins, outs, sems):
            cp.wait()

    return [(0.0, send), (1.0, finish)]


def chip_exchange_background(parts):
    n = len(parts)
    return Background(parts, [jax.ShapeDtypeStruct((3,) + p.shape[1:], p.dtype) for p in parts],
                      [pltpu.SemaphoreType.DMA((3 * n,)), pltpu.SemaphoreType.DMA((3 * n,))],
                      _chip_exchange_phases(n))


def add_pairs(grads, theirs, core, *, name, tm=512):
    _, R, C = theirs.shape
    tm = _tile(R, tm, 16)

    def body(core_ref, a_ref, b_ref, o_ref):
        o_ref[...] = (a_ref[...].astype(F32) + b_ref[...].astype(F32)).astype(o_ref.dtype)

    spec = pl.BlockSpec((None, tm, C), lambda q, i, core_ref: (q, i, 0))
    return pl.pallas_call(
        body, name=name,
        grid_spec=pltpu.PrefetchScalarGridSpec(
            num_scalar_prefetch=1, grid=(4, R // tm),
            in_specs=[pl.BlockSpec((None, tm, C), lambda q, i, core_ref: (2 * q + core_ref[0], i, 0)), spec],
            out_specs=spec),
        out_shape=jax.ShapeDtypeStruct(theirs.shape, BF16),
        compiler_params=_params(("parallel", "parallel")),
    )(core, grads, theirs)


def _adamw(w, g, m, v):
    m = ADAM_B1 * m + (1.0 - ADAM_B1) * g
    v = ADAM_B2 * v + (1.0 - ADAM_B2) * (g * g)
    m_hat = m / (1.0 - ADAM_B1 ** ADAM_STEP)
    v_hat = v / (1.0 - ADAM_B2 ** ADAM_STEP)
    delta = -ADAM_LR * (m_hat / (jnp.sqrt(v_hat) + ADAM_EPS) + ADAM_WD * w)
    return delta, m, v


def adamw_sharded(w, m, v, grads, theirs, others, where, *, name, tm=256):
    R, C = w.shape
    tm = _tile(R, tm, 16)

    def body(where_ref, w_ref, m_ref, v_ref, a_ref, b_ref, o_ref, g_ref, d_ref, nm_ref, nv_ref):
        g = a_ref[...].astype(F32) + b_ref[...].astype(F32)
        for j in range(3):
            g = g + o_ref[j].astype(F32)
        g_ref[...] = g
        d_ref[...], nm_ref[...], nv_ref[...] = _adamw(w_ref[...], g, m_ref[...], v_ref[...])

    spec = pl.BlockSpec((tm, C), lambda i, where_ref: (i, 0))
    sd = jax.ShapeDtypeStruct((R, C), F32)
    return pl.pallas_call(
        body, name=name,
        grid_spec=pltpu.PrefetchScalarGridSpec(
            num_scalar_prefetch=1, grid=(R // tm,),
            in_specs=[spec, spec, spec,
                      pl.BlockSpec((None, tm, C), lambda i, where_ref: (where_ref[0], i, 0)),
                      pl.BlockSpec((None, tm, C), lambda i, where_ref: (where_ref[1], i, 0)),
                      pl.BlockSpec((3, tm, C), lambda i, where_ref: (0, i, 0))],
            out_specs=[spec, spec, spec, spec]),
        out_shape=[sd, sd, sd, sd],
        compiler_params=_params(("parallel",)),
    )(where, w, m, v, grads, theirs, others)


def sum_gathered(gathered, *, name):
    n = len(gathered)

    def body(*refs):
        for ga_ref, o_ref in zip(refs[:n], refs[n:]):
            rows = len(ga_ref.shape) == 2
            total = ga_ref[0:1] if rows else ga_ref[0]
            for dev in range(1, N_DEV):
                total = total + (ga_ref[dev:dev + 1] if rows else ga_ref[dev])
            o_ref[...] = total

    shapes = [jax.ShapeDtypeStruct((1,) + g.shape[1:] if g.ndim == 2 else g.shape[1:], F32) for g in gathered]
    return pl.pallas_call(body, name=name, out_shape=shapes, compiler_params=_params())(*gathered)


def adamw_replicated(ws, ms, vs, gs, *, name):
    n = len(ws)

    def body(*refs):
        w_refs, m_refs, v_refs, g_refs = refs[:n], refs[n:2 * n], refs[2 * n:3 * n], refs[3 * n:4 * n]
        outs = refs[4 * n:]
        for k in range(n):
            outs[k][...], outs[n + k][...], outs[2 * n + k][...] = _adamw(
                w_refs[k][...], g_refs[k][...], m_refs[k][...], v_refs[k][...])

    shapes = [jax.ShapeDtypeStruct(t.shape, F32) for t in ws]
    res = pl.pallas_call(body, name=name, out_shape=shapes * 3, compiler_params=_params())(*ws, *ms, *vs, *gs)
    return res[:n], res[n:2 * n], res[2 * n:]


SHARDED = ("w_in", "ssm_glu_w", "w_out", "w_ffn_in", "w_ffn_out", "w_ple_gate", "w_ple_proj")
SMALL_LAST = ("norm_mix_g",)
SMALL_WIDE = ("ssm_b_re", "ssm_b_im", "ssm_c_re", "ssm_c_im")
SMALL = ("ssm_lambda_re", "ssm_lambda_im", "ssm_log_step", "ssm_b_re", "ssm_b_im", "ssm_c_re",
         "ssm_c_im", "ssm_d", "ssm_glu_b", "sgu_ln_g", "sgu_ln_b", "sgu_w", "sgu_b", "out_norm_ssm_g",
         "out_norm_sgu_g", "norm_ffn_g", "norm_ple_g", "b_ple_gate", "final_norm_g")
WEIGHTS = ("norm_mix_g", "w_in", "ssm_lambda_re", "ssm_lambda_im", "ssm_log_step", "ssm_b_re", "ssm_b_im",
           "ssm_c_re", "ssm_c_im", "ssm_d", "ssm_glu_w", "ssm_glu_b", "sgu_ln_g", "sgu_ln_b", "sgu_w", "sgu_b",
           "out_norm_ssm_g", "out_norm_sgu_g", "w_out", "norm_ffn_g", "w_ffn_in", "w_ffn_out", "norm_ple_g",
           "w_ple_gate", "b_ple_gate", "w_ple_proj", "final_norm_g")


def _step(x, p, loss_target, w, m, v):
    L, D = x.shape[1], x.shape[2]
    x2d, p2d, tgt = x.reshape(L, D), p.reshape(L, -1), loss_target.reshape(L, D)
    d_ssm = w["ssm_glu_w"].shape[2]
    d_sgu = w["sgu_ln_g"].shape[1]
    G, P, H = w["ssm_b_re"].shape[1:]
    SG = min(SSM_SUPER, G)
    NK = G // SG
    row = lambda a: a.reshape(1, -1)

    shard2d = {n: w[n].reshape(w[n].shape[1:]) for n in SHARDED}
    shard_bf = {n: shard2d[n].astype(BF16) for n in SHARDED}
    (w_ple_blk,) = all_gather_blocks([shard_bf["w_ple_proj"]], name="gather_w_ple")
    bf = lambda t: t.astype(BF16)
    pp, (w_in_blk,) = mm_nn(bf(p2d), w_ple_blk, name="ple_proj", out_dtype=F32, tm=512, tn=512, tk=2048,
                            bg=gather_background([shard_bf["w_in"]]))
    w_in = jnp.transpose(w_in_blk, (1, 0, 2)).reshape(D, -1)
    F = shard2d["w_ffn_in"].shape[1] * 4

    lam_re, lam_im, log_step = w["ssm_lambda_re"][0], w["ssm_lambda_im"][0], w["ssm_log_step"][0].reshape(G, 1)
    a_re, a_im, q_re, q_im = disc_lambda_fwd(lam_re, lam_im, log_step, name="s5_discretise_lambda")
    bt_re = w["ssm_b_re"][0].transpose(2, 0, 1).reshape(H, G * P)
    bt_im = w["ssm_b_im"][0].transpose(2, 0, 1).reshape(H, G * P)
    bbar_re, bbar_im = disc_b_fwd(row(q_re), row(q_im), bt_re, bt_im, name="s5_discretise_b")
    to_bs = lambda t: _block_diag(t.reshape(H, NK, SG, P).transpose(1, 2, 0, 3))
    to_cs = lambda t: _block_diag(t.reshape(NK, SG, H, P).transpose(0, 1, 3, 2))
    bs_re, bs_im = to_bs(bbar_re), to_bs(bbar_im)
    cs_re, cs_im = to_cs(w["ssm_c_re"][0]), to_cs(w["ssm_c_im"][0])
    a_re_k, a_im_k = a_re.reshape(NK, 1, SG * P), a_im.reshape(NK, 1, SG * P)
    d_k = w["ssm_d"][0].reshape(NK, 1, SG * H)
    bf = lambda t: t.astype(BF16)

    h1 = norm_fwd(x2d, w["norm_mix_g"], name="norm_mix")
    z, (w_glu, w_out) = mm_nn(h1, w_in, name="in_proj", out_dtype=F32, tm=512, tn=1024, tk=2048,
                              bg=gather_background([shard_bf["ssm_glu_w"], shard_bf["w_out"]]))
    w_glu, w_out = w_glu.reshape(d_ssm, d_ssm), w_out.reshape(D, D)
    first_rows = (D * 11 // 16) // 16 * 16
    s5_mats = (bf(bs_re), bf(bs_im), bf(cs_re), bf(cs_im), a_re_k, a_im_k, d_k)
    (y0, s_re, s_im), (w_ffn_in_part,) = ssm_fwd(
        z, *s5_mats, name="s5_scan", bg=gather_background([shard_bf["w_ffn_in"]], rows=(0, first_rows)))
    ya1 = glu_pre(y0, name="s5_gelu")
    t_glu = mm_nn(ya1, w_glu, name="s5_glu_proj", out_dtype=F32, tm=512, tn=512, tk=2048)
    n_a = glu_post(y0, t_glu, w["ssm_glu_b"], w["out_norm_ssm_g"], name="s5_glu_norm")
    b_s3 = w["sgu_b"][0][:, :, None]
    n_b = sgu_fwd(z, w["sgu_ln_g"], w["sgu_ln_b"], w["sgu_w"][0], b_s3, w["out_norm_sgu_g"], name="sgu", d_sgu=d_sgu)
    ycat = jnp.concatenate([n_a, n_b], axis=1)
    x1, (w_ffn_in_blk,) = mm_nn(
        ycat, w_out, name="out_proj", out_dtype=F32, tm=512, tn=512, tk=2048, residual=x2d,
        bg=gather_background([shard_bf["w_ffn_in"]], rows=(first_rows, D - first_rows), into=[w_ffn_in_part]))
    h2 = norm_fwd(x1, w["norm_ffn_g"], name="norm_ffn")
    (act, gate_ff, up_ff), (w_ffn_out, w_gate) = ffn_in_swiglu(
        h2, w_ffn_in_blk, name="ffn_in_swiglu", tm=256,
        bg=gather_background([shard_bf["w_ffn_out"], shard_bf["w_ple_gate"]]))
    w_ffn_out, w_gate = w_ffn_out.reshape(F, D), w_gate.reshape(D, D)
    x2 = mm_nn(act, w_ffn_out, name="ffn_out", out_dtype=F32, tm=512, tn=512, tk=F, residual=x1)
    h3 = norm_fwd(x2, w["norm_ple_g"], name="norm_ple")
    gpre = mm_nn(h3, w_gate, name="ple_gate", out_dtype=F32, tm=512, tn=1024, tk=2048)

    dx3, dpre, dpp, loss_part, d_final_g, d_b_gate = head_and_loss(
        x2, gpre, w["b_ple_gate"], pp, row(w["final_norm_g"]), tgt, name="head_and_loss")
    x_pos, y_pos, c_pos = _position()
    where = jnp.stack([4 * x_pos + 2 * y_pos + c_pos, 2 * x_pos + y_pos]).astype(jnp.int32)
    core = jnp.reshape(c_pos, (1,)).astype(jnp.int32)
    own, others = {}, {}

    def chip_partials(named, tag):
        names = list(named)
        g8 = [named[n].reshape((N_DEV,) + shard2d[n].shape) for n in names]
        theirs = sibling_exchange(g8, name="grads_to_sibling_" + tag)
        own.update(zip(names, zip(g8, theirs)))
        return names, chip_exchange_background(
            [add_pairs(g, t, core, name="chip_sum_" + n) for n, g, t in zip(names, g8, theirs)])

    d_w_gate = mm_tn(h3, dpre, name="d_w_ple_gate", out_dtype=BF16, tm=L, tko=1024, tno=1024)
    d_w_ple = mm_tn(bf(p2d), dpp, name="d_w_ple_proj", out_dtype=BF16, tm=L, tko=1024, tno=1024, out_blocks=N_DEV)
    names, bg = chip_partials({"w_ple_gate": d_w_gate, "w_ple_proj": d_w_ple}, "ple")
    dh3, got = mm_nt(dpre, w_gate, name="d_h_ple", out_dtype=F32, tm=512, tko=1024, tc=2048, bg=bg)
    others.update(zip(names, got))
    dx2, dx2b, d_ple_g = norm_bwd(dh3, x2, w["norm_ple_g"], dx3, name="d_norm_ple", want_bf16=True)
    d_w_ffn_out = mm_tn(act, dx2b, name="d_w_ffn_out", out_dtype=BF16, tm=L, tko=1408, tno=512)
    names, bg = chip_partials({"w_ffn_out": d_w_ffn_out}, "ffn_out")
    dgate, dup = ffn_out_bwd_swiglu(dx2b, w_ffn_out, gate_ff, up_ff, name="d_act_swiglu")
    half = N_DEV // 2
    d_w_ffn_in, got = mm_tn(h2, dgate, name="d_w_ffn_in_gate", out_dtype=BF16, tm=L, tko=512, tno=1408,
                            out_blocks=half, total_blocks=N_DEV, bg=bg)
    others.update(zip(names, got))
    d_w_ffn_in = mm_tn(h2, dup, name="d_w_ffn_in_up", out_dtype=BF16, tm=L, tko=512, tno=1408,
                       out_blocks=half, block_offset=half, total_blocks=N_DEV, into=d_w_ffn_in)
    names, bg = chip_partials({"w_ffn_in": d_w_ffn_in}, "ffn_in")
    dh2, got = mm_nt(dgate, w_ffn_in_blk, a2=dup, name="d_h_ffn", out_dtype=F32, tm=1024, tko=1024, tc=1408, bg=bg)
    others.update(zip(names, got))
    dx1, dx1b, d_ffn_g = norm_bwd(dh2, x1, w["norm_ffn_g"], dx2, name="d_norm_ffn", want_bf16=True)
    dycat = mm_nt(dx1b, w_out, name="d_ycat", out_dtype=F32, tm=512, tko=1024, tc=2048)
    d_w_out = mm_tn(ycat, dx1b, name="d_w_out", out_dtype=BF16, tm=L, tko=1024, tno=1024)
    dzu, dzv, d_sgu_w, d_sgu_b, d_ln_g, d_ln_b, d_g_b = sgu_bwd(
        z, dycat, w["sgu_ln_g"], w["sgu_ln_b"], w["sgu_w"][0], b_s3, w["out_norm_sgu_g"], name="d_sgu", d_sgu=d_sgu)
    dt_glu, dd_direct, d_g_a, d_glu_b = glu_post_bwd(
        y0, t_glu, w["ssm_glu_b"], w["out_norm_ssm_g"], dycat, name="d_s5_glu_norm")
    d_w_glu = mm_tn(ya1, dt_glu, name="d_w_glu", out_dtype=BF16, tm=L, tko=1024, tno=1024)
    names, bg = chip_partials({"w_out": d_w_out, "ssm_glu_w": d_w_glu}, "out_glu")
    dd_mm = mm_nt(dt_glu, w_glu, name="d_s5_glu_proj", out_dtype=F32, tm=512, tko=1024, tc=2048)
    (du, d_bs_re, d_bs_im, d_cs_re, d_cs_im, d_a_re, d_a_im, d_d), got = ssm_bwd(
        z, y0, dd_direct, dd_mm, s_re, s_im, *s5_mats, name="d_s5_scan", bg=bg)
    others.update(zip(names, got))
    dz = jnp.concatenate([du, dzu, dzv], axis=1)

    from_bs = lambda t: _block_diag_part(t, SG).transpose(2, 0, 1, 3).reshape(H, G * P)
    from_cs = lambda t: _block_diag_part(t, SG).transpose(3, 0, 1, 2).reshape(H, G * P)
    d_q_re, d_q_im, d_bt_re, d_bt_im = disc_b_bwd(row(q_re), row(q_im), bt_re, bt_im, from_bs(d_bs_re),
                                                  from_bs(d_bs_im), name="d_s5_discretise_b")
    d_lam_re, d_lam_im, d_log_step = disc_lambda_bwd(
        lam_re, lam_im, log_step,
        (d_a_re.reshape(G, P), d_a_im.reshape(G, P), d_q_re.reshape(G, P), d_q_im.reshape(G, P)),
        name="d_s5_discretise_lambda")
    small_grads = {
        "ssm_lambda_re": d_lam_re, "ssm_lambda_im": d_lam_im, "ssm_log_step": d_log_step,
        "ssm_b_re": d_bt_re, "ssm_b_im": d_bt_im, "ssm_c_re": from_cs(d_cs_re), "ssm_c_im": from_cs(d_cs_im),
        "ssm_d": d_d, "ssm_glu_b": d_glu_b, "sgu_ln_g": d_ln_g, "sgu_ln_b": d_ln_b,
        "sgu_w": d_sgu_w, "sgu_b": d_sgu_b, "out_norm_ssm_g": d_g_a, "out_norm_sgu_g": d_g_b,
        "norm_ffn_g": d_ffn_g, "norm_ple_g": d_ple_g, "b_ple_gate": d_b_gate, "final_norm_g": d_final_g,
    }

    d_w_in, got = mm_tn(h1, dz, name="d_w_in", out_dtype=BF16, tm=L, tko=1024, tno=1024, out_blocks=N_DEV,
                        bg=gather_background([loss_part] + [small_grads[n] for n in SMALL]))
    sums = sum_gathered(got, name="sum_small_grads")
    names, bg = chip_partials({"w_in": d_w_in}, "w_in")
    dh1, got = mm_nt(dz, w_in, name="d_h_mix", out_dtype=F32, tm=512, tko=1024, tc=3 * d_sgu, bg=bg)
    others.update(zip(names, got))
    grad_x, d_mix_g = norm_bwd(dh1, x2d, w["norm_mix_g"], dx1, name="d_norm_mix", want_bf16=False)
    loss, small_sum = sums[0][0, 0], dict(zip(SMALL, sums[1:]))
    (small_sum["norm_mix_g"],) = sum_gathered(all_gather_blocks([d_mix_g], name="gather_last_grad"),
                                              name="sum_last_grad")

    out = {}
    for n in SHARDED:
        res = adamw_sharded(shard2d[n], m[n].reshape(shard2d[n].shape), v[n].reshape(shard2d[n].shape),
                            own[n][0], own[n][1], others[n], where, name="adamw_" + n)
        out[n] = [r.reshape(w[n].shape) for r in res]

    def work_shape(n):
        s = w[n].shape
        return (1,) + s if len(s) == 1 else (s if len(s) == 2 else s[1:])

    for n in ("ssm_b_re", "ssm_b_im"):
        small_sum[n] = small_sum[n].reshape(H, G, P).transpose(1, 2, 0)
    for n in ("ssm_c_re", "ssm_c_im"):
        small_sum[n] = small_sum[n].reshape(H, G, P).transpose(1, 0, 2)

    def replicated(names_, name):
        gs = [small_sum[n].reshape(work_shape(n)) for n in names_]
        res = adamw_replicated(*[[t[n].reshape(work_shape(n)) for n in names_] for t in (w, m, v)], gs, name=name)
        for i, n in enumerate(names_):
            out[n] = [r.reshape(w[n].shape) for r in (gs[i], res[0][i], res[1][i], res[2][i])]

    replicated([n for n in SMALL + SMALL_LAST if n not in SMALL_WIDE], "adamw_small")
    replicated(list(SMALL_WIDE), "adamw_s5_b_c")

    grads = [out[n][0] for n in WEIGHTS]
    deltas = [out[n][1] for n in WEIGHTS]
    new_m = [out[n][2] for n in WEIGHTS]
    new_v = [out[n][3] for n in WEIGHTS]
    return (loss, grad_x.reshape(x.shape), *grads, *deltas, *new_m, *new_v)


def kernel(x, p, norm_mix_g, w_in, ssm_lambda_re, ssm_lambda_im, ssm_log_step, ssm_b_re, ssm_b_im, ssm_c_re, ssm_c_im, ssm_d, ssm_glu_w, ssm_glu_b, sgu_ln_g, sgu_ln_b, sgu_w, sgu_b, out_norm_ssm_g, out_norm_sgu_g, w_out, norm_ffn_g, w_ffn_in, w_ffn_out, norm_ple_g, w_ple_gate, b_ple_gate, w_ple_proj, final_norm_g, loss_target, m_norm_mix_g, m_w_in, m_ssm_lambda_re, m_ssm_lambda_im, m_ssm_log_step, m_ssm_b_re, m_ssm_b_im, m_ssm_c_re, m_ssm_c_im, m_ssm_d, m_ssm_glu_w, m_ssm_glu_b, m_sgu_ln_g, m_sgu_ln_b, m_sgu_w, m_sgu_b, m_out_norm_ssm_g, m_out_norm_sgu_g, m_w_out, m_norm_ffn_g, m_w_ffn_in, m_w_ffn_out, m_norm_ple_g, m_w_ple_gate, m_b_ple_gate, m_w_ple_proj, m_final_norm_g, v_norm_mix_g, v_w_in, v_ssm_lambda_re, v_ssm_lambda_im, v_ssm_log_step, v_ssm_b_re, v_ssm_b_im, v_ssm_c_re, v_ssm_c_im, v_ssm_d, v_ssm_glu_w, v_ssm_glu_b, v_sgu_ln_g, v_sgu_ln_b, v_sgu_w, v_sgu_b, v_out_norm_ssm_g, v_out_norm_sgu_g, v_w_out, v_norm_ffn_g, v_w_ffn_in, v_w_ffn_out, v_norm_ple_g, v_w_ple_gate, v_b_ple_gate, v_w_ple_proj, v_final_norm_g):
    given = dict(locals())
    w = {n: given[n] for n in WEIGHTS}
    m = {n: given["m_" + n] for n in WEIGHTS}
    v = {n: given["v_" + n] for n in WEIGHTS}
    return _step(x, p, loss_target, w, m, v)
```

```python
import functools
import math

import jax
import jax.numpy as jnp
from jax import lax
from jax.experimental import pallas as pl
from jax.experimental.pallas import tpu as pltpu

F32 = jnp.float32
BF16 = jnp.bfloat16
MESH = pl.DeviceIdType.MESH
ANY = pl.BlockSpec(memory_space=pl.ANY)

N_DEV = 8
EPS = 1e-6
LAMBDA_RE_MAX = -1e-4
SSM_GROUP = 16
SSM_STATE = 64
SSM_SUPER = 16
SGU_CHUNK = 128
ADAM_LR, ADAM_B1, ADAM_B2, ADAM_EPS, ADAM_WD, ADAM_STEP = 0.001, 0.9, 0.999, 1e-08, 0.01, 10
VMEM_LIMIT = 52 * 1024 * 1024
LANE = 128
GATHER_FORWARD_AT = 0.85

_GELU_C = math.sqrt(2.0 / math.pi)


def _params(sem=None):
    return pltpu.CompilerParams(dimension_semantics=sem, vmem_limit_bytes=VMEM_LIMIT)


def _tile(dim, pref, unit=LANE):
    if dim <= pref:
        return dim
    t = (pref // unit) * unit
    while t >= unit:
        if dim % t == 0:
            return t
        t -= unit
    return dim


def _gelu(x):
    return 0.5 * x * (1.0 + jnp.tanh(_GELU_C * (x + 0.044715 * x * x * x)))


def _gelu_grad(x):
    t = jnp.tanh(_GELU_C * (x + 0.044715 * x * x * x))
    return 0.5 * (1.0 + t) + 0.5 * x * (1.0 - t * t) * (_GELU_C * (1.0 + 3.0 * 0.044715 * x * x))


def _gelu_and_grad(x):
    t = jnp.tanh(_GELU_C * (x + 0.044715 * x * x * x))
    return (0.5 * x * (1.0 + t),
            0.5 * (1.0 + t) + 0.5 * x * (1.0 - t * t) * (_GELU_C * (1.0 + 3.0 * 0.044715 * x * x)))


def _rms(x):
    return lax.rsqrt(jnp.mean(x * x, axis=-1, keepdims=True) + EPS)


def _rmsnorm_bwd(dy, x, r, g):
    dyg = dy * g
    return r * dyg - x * (r * r * r) * jnp.mean(dyg * x, axis=-1, keepdims=True)


def _rowsum(v):
    return jnp.sum(v, axis=0, keepdims=True)


class Background:
    def __init__(self, inputs, out_shapes, scratch, phases):
        self.inputs, self.out_shapes, self.scratch, self.phases = list(inputs), list(out_shapes), list(scratch), phases
        self.aliases = {}

    def emit(self, step, nsteps, ins, outs, scratch):
        for place, phase in self.phases:
            at = min(int(place * nsteps), nsteps - 1)

            @pl.when(step == at)
            def _():
                phase(ins, outs, scratch)


def _carrier(bg, n_in, n_out, n_scratch, grid):
    nbi = len(bg.inputs) if bg else 0
    nbo = len(bg.out_shapes) if bg else 0
    nsteps = math.prod(grid)

    def split(refs):
        ins = refs[:n_in]
        bg_ins = refs[n_in:n_in + nbi]
        outs = refs[n_in + nbi:n_in + nbi + n_out]
        bg_outs = refs[n_in + nbi + n_out:n_in + nbi + n_out + nbo]
        rest = refs[n_in + nbi + n_out + nbo:]
        scratch, bg_scratch = rest[:n_scratch], rest[n_scratch:]

        def run_background():
            if bg is None:
                return
            step = pl.program_id(0)
            for axis in range(1, len(grid)):
                step = step * grid[axis] + pl.program_id(axis)
            bg.emit(step, nsteps, bg_ins, bg_outs, bg_scratch)

        return ins, outs, scratch, run_background

    if bg is None:
        return [], [], [], [], [], split
    return [ANY] * nbi, list(bg.inputs), [ANY] * nbo, list(bg.out_shapes), list(bg.scratch), split


def _semantics(bg, sem):
    return tuple("arbitrary" for _ in sem) if bg is not None else sem


def _results(res, n_out, bg):
    res = list(res) if isinstance(res, (list, tuple)) else [res]
    own = res[0] if n_out == 1 else res[:n_out]
    return (own, res[n_out:]) if bg is not None else own


def mm_nn(a, b, *, name, out_dtype, tm, tn, tk, residual=None, bg=None):
    M, K = a.shape
    blocked = b.ndim == 3
    if blocked:
        nb, _, Nb = b.shape
        N = nb * Nb
        tn = _tile(Nb, tn)
        per = Nb // tn
    else:
        N = b.shape[1]
        tn = _tile(N, tn)
    tm, tk = _tile(M, tm, 8), _tile(K, tk)
    nj, ni, nk = N // tn, M // tm, K // tk
    has_res = residual is not None
    grid = (nj, ni, nk)
    bg_in_specs, bg_args, bg_out_specs, bg_out_shapes, bg_scratch, split = _carrier(
        bg, 3 if has_res else 2, 1, 0 if nk == 1 else 1, grid)

    def body(*refs):
        ins, (o_ref,), scratch, run_background = split(refs)
        run_background()
        a_ref, b_ref = ins[0], ins[1]
        r_ref = ins[2] if has_res else None

        def finish(acc):
            if has_res:
                acc = acc + r_ref[...]
            o_ref[...] = acc.astype(o_ref.dtype)

        part = jnp.dot(a_ref[...], b_ref[...], preferred_element_type=F32)
        if nk == 1:
            finish(part)
        else:
            acc_ref = scratch[0]
            k = pl.program_id(2)

            @pl.when(k == 0)
            def _():
                acc_ref[...] = part

            @pl.when(k > 0)
            def _():
                acc_ref[...] += part

            @pl.when(k == nk - 1)
            def _():
                finish(acc_ref[...])

    if blocked:
        b_spec = pl.BlockSpec((None, tk, tn), lambda j, i, k: (j // per, k, j % per))
    else:
        b_spec = pl.BlockSpec((tk, tn), lambda j, i, k: (k, j))
    in_specs = [pl.BlockSpec((tm, tk), lambda j, i, k: (i, k)), b_spec]
    args = [a, b]
    if has_res:
        in_specs.append(pl.BlockSpec((tm, tn), lambda j, i, k: (i, j)))
        args.append(residual)
    res = pl.pallas_call(
        body, name=name, grid=grid,
        in_specs=in_specs + bg_in_specs,
        out_specs=[pl.BlockSpec((tm, tn), lambda j, i, k: (i, j))] + bg_out_specs,
        out_shape=[jax.ShapeDtypeStruct((M, N), out_dtype)] + bg_out_shapes,
        input_output_aliases={len(args) + k: 1 + o for k, o in (bg.aliases if bg else {}).items()},
        scratch_shapes=([] if nk == 1 else [pltpu.VMEM((tm, tn), F32)]) + bg_scratch,
        compiler_params=_params(_semantics(bg, ("parallel", "parallel", "arbitrary"))),
    )(*args, *bg_args)
    return _results(res, 1, bg)


def mm_nt(a, w, *, name, out_dtype, tm, tko, tc, a2=None, bg=None):
    M, N = a.shape
    if a2 is not None:
        N = 2 * N
    blocked = w.ndim == 3
    if blocked:
        nb, Ko, Nb = w.shape
        tc = _tile(Nb, tc)
        per = Nb // tc
    else:
        Ko = w.shape[0]
        tc = _tile(N, tc)
    tm, tko = _tile(M, tm, 8), _tile(Ko, tko)
    njo, ni, nc = Ko // tko, M // tm, N // tc
    grid = (njo, ni, nc)
    half = nc // 2
    bg_in_specs, bg_args, bg_out_specs, bg_out_shapes, bg_scratch, split = _carrier(
        bg, 2 if a2 is None else 3, 1, 0 if nc == 1 else 1, grid)

    def body(*refs):
        ins, (o_ref,), scratch, run_background = split(refs)
        run_background()
        a_val = ins[0][...]
        if a2 is not None:
            a_val = jnp.where(pl.program_id(2) < half, a_val, ins[1][...])
        part = lax.dot_general(a_val, ins[-1][...], (((1,), (1,)), ((), ())),
                               preferred_element_type=F32)
        if nc == 1:
            o_ref[...] = part.astype(o_ref.dtype)
        else:
            acc_ref = scratch[0]
            c = pl.program_id(2)

            @pl.when(c == 0)
            def _():
                acc_ref[...] = part

            @pl.when(c > 0)
            def _():
                acc_ref[...] += part

            @pl.when(c == nc - 1)
            def _():
                o_ref[...] = acc_ref[...].astype(o_ref.dtype)

    if blocked:
        w_spec = pl.BlockSpec((None, tko, tc), lambda j, i, c: (c // per, j, c % per))
    else:
        w_spec = pl.BlockSpec((tko, tc), lambda j, i, c: (j, c))
    if a2 is None:
        a_specs, a_args = [pl.BlockSpec((tm, tc), lambda j, i, c: (i, c))], [a]
    else:
        a_specs = [pl.BlockSpec((tm, tc), lambda j, i, c: (i, jnp.minimum(c, half - 1))),
                   pl.BlockSpec((tm, tc), lambda j, i, c: (i, jnp.maximum(c - half, 0)))]
        a_args = [a, a2]
    res = pl.pallas_call(
        body, name=name, grid=grid,
        in_specs=a_specs + [w_spec] + bg_in_specs,
        out_specs=[pl.BlockSpec((tm, tko), lambda j, i, c: (i, j))] + bg_out_specs,
        out_shape=[jax.ShapeDtypeStruct((M, Ko), out_dtype)] + bg_out_shapes,
        scratch_shapes=([] if nc == 1 else [pltpu.VMEM((tm, tko), F32)]) + bg_scratch,
        compiler_params=_params(_semantics(bg, ("parallel", "parallel", "arbitrary"))),
    )(*a_args, w, *bg_args)
    return _results(res, 1, bg)


def mm_tn(a, g, *, name, out_dtype, tm, tko, tno, out_blocks=None, block_offset=0, total_blocks=None, into=None,
          bg=None):
    M, K = a.shape
    N = g.shape[1]
    if out_blocks:
        Nb = N // out_blocks
        tno = _tile(Nb, tno)
        per = Nb // tno
    else:
        tno = _tile(N, tno)
    tm, tko = _tile(M, tm), _tile(K, tko)
    njo, njn, nm = K // tko, N // tno, M // tm
    grid = (njo, njn, nm)
    bg_in_specs, bg_args, bg_out_specs, bg_out_shapes, bg_scratch, split = _carrier(
        bg, 2 if into is None else 3, 1, 0 if nm == 1 else 1, grid)

    def body(*refs):
        ins, (o_ref,), scratch, run_background = split(refs)
        a_ref, g_ref = ins[0], ins[1]
        run_background()
        part = lax.dot_general(a_ref[...], g_ref[...], (((0,), (0,)), ((), ())),
                               preferred_element_type=F32)
        if nm == 1:
            o_ref[...] = part.astype(o_ref.dtype)
        else:
            acc_ref = scratch[0]
            m = pl.program_id(2)

            @pl.when(m == 0)
            def _():
                acc_ref[...] = part

            @pl.when(m > 0)
            def _():
                acc_ref[...] += part

            @pl.when(m == nm - 1)
            def _():
                o_ref[...] = acc_ref[...].astype(o_ref.dtype)

    if out_blocks:
        o_spec = pl.BlockSpec((None, tko, tno), lambda jo, jn, m: (jn // per + block_offset, jo, jn % per))
        o_shape = jax.ShapeDtypeStruct((total_blocks or out_blocks, K, Nb), out_dtype)
    else:
        o_spec = pl.BlockSpec((tko, tno), lambda jo, jn, m: (jo, jn))
        o_shape = jax.ShapeDtypeStruct((K, N), out_dtype)
    res = pl.pallas_call(
        body, name=name, grid=grid,
        in_specs=[pl.BlockSpec((tm, tko), lambda jo, jn, m: (m, jo)),
                  pl.BlockSpec((tm, tno), lambda jo, jn, m: (m, jn))] + ([] if into is None else [ANY]) + bg_in_specs,
        out_specs=[o_spec] + bg_out_specs, out_shape=[o_shape] + bg_out_shapes,
        scratch_shapes=([] if nm == 1 else [pltpu.VMEM((tko, tno), F32)]) + bg_scratch,
        input_output_aliases={} if into is None else {2: 0},
        compiler_params=_params(_semantics(bg, ("parallel", "parallel", "arbitrary"))),
    )(a, g, *([] if into is None else [into]), *bg_args)
    return _results(res, 1, bg)


def ffn_in_swiglu(h, w_blk, *, name, tm, bg=None):
    M, K = h.shape
    nb, _, Nb = w_blk.shape
    nh = nb // 2
    F = nh * Nb
    tm = _tile(M, tm, 8)
    grid = (nh, M // tm)
    bg_in_specs, bg_args, bg_out_specs, bg_out_shapes, bg_scratch, split = _carrier(bg, 3, 3, 0, grid)

    def body(*refs):
        (h_ref, wg_ref, wu_ref), (act_ref, gate_ref, up_ref), _, run_background = split(refs)
        run_background()
        hv = h_ref[...]
        gate = jnp.dot(hv, wg_ref[...], preferred_element_type=F32)
        up = jnp.dot(hv, wu_ref[...], preferred_element_type=F32)
        gate_ref[...] = gate
        up_ref[...] = up
        act_ref[...] = (gate * jax.nn.sigmoid(gate) * up).astype(act_ref.dtype)

    o_spec = pl.BlockSpec((tm, Nb), lambda j, i: (i, j))
    res = pl.pallas_call(
        body, name=name, grid=grid,
        in_specs=[pl.BlockSpec((tm, K), lambda j, i: (i, 0)),
                  pl.BlockSpec((None, K, Nb), lambda j, i: (j, 0, 0)),
                  pl.BlockSpec((None, K, Nb), lambda j, i: (j + nh, 0, 0))] + bg_in_specs,
        out_specs=[o_spec, o_spec, o_spec] + bg_out_specs,
        out_shape=[jax.ShapeDtypeStruct((M, F), BF16), jax.ShapeDtypeStruct((M, F), F32),
                   jax.ShapeDtypeStruct((M, F), F32)] + bg_out_shapes,
        scratch_shapes=bg_scratch,
        compiler_params=_params(_semantics(bg, ("parallel", "parallel"))),
    )(h, w_blk, w_blk, *bg_args)
    return _results(res, 3, bg)


def _row_spec(tm, d, col=0):
    return pl.BlockSpec((tm, d), lambda i: (i, col))


def _vec_spec(d):
    return pl.BlockSpec((1, d), lambda i: (0, 0))


def norm_fwd(x, g, *, name, tm=256):
    L, D = x.shape
    tm = _tile(L, tm, 8)

    def body(x_ref, g_ref, h_ref):
        xv = x_ref[...]
        h_ref[...] = (xv * _rms(xv) * g_ref[...]).astype(h_ref.dtype)

    return pl.pallas_call(
        body, name=name, grid=(L // tm,),
        in_specs=[_row_spec(tm, D), _vec_spec(D)],
        out_specs=_row_spec(tm, D),
        out_shape=jax.ShapeDtypeStruct((L, D), BF16),
        compiler_params=_params(("parallel",)),
    )(x, g)


def norm_bwd(dh, xin, g, dres, *, name, want_bf16, tm=128):
    L, D = xin.shape
    tm = _tile(L, tm, 8)

    def body(dh_ref, x_ref, g_ref, dres_ref, dx_ref, *rest):
        dg_ref = rest[-1]
        xv, dhv = x_ref[...], dh_ref[...]
        r = _rms(xv)
        dx = dres_ref[...] + _rmsnorm_bwd(dhv, xv, r, g_ref[...])
        dx_ref[...] = dx
        if want_bf16:
            rest[0][...] = dx.astype(BF16)
        part = _rowsum(dhv * xv * r)

        @pl.when(pl.program_id(0) == 0)
        def _():
            dg_ref[...] = part

        @pl.when(pl.program_id(0) > 0)
        def _():
            dg_ref[...] += part

    out_specs = [_row_spec(tm, D)] + ([_row_spec(tm, D)] if want_bf16 else []) + [_vec_spec(D)]
    out_shape = ([jax.ShapeDtypeStruct((L, D), F32)]
                 + ([jax.ShapeDtypeStruct((L, D), BF16)] if want_bf16 else [])
                 + [jax.ShapeDtypeStruct((1, D), F32)])
    return pl.pallas_call(
        body, name=name, grid=(L // tm,),
        in_specs=[_row_spec(tm, D), _row_spec(tm, D), _vec_spec(D), _row_spec(tm, D)],
        out_specs=out_specs, out_shape=out_shape,
        compiler_params=_params(("arbitrary",)),
    )(dh, xin, g, dres)


def glu_pre(y0, *, name, tm=256):
    L, D = y0.shape
    tm = _tile(L, tm, 8)

    def body(y_ref, o_ref):
        o_ref[...] = _gelu(y_ref[...]).astype(o_ref.dtype)

    return pl.pallas_call(
        body, name=name, grid=(L // tm,),
        in_specs=[_row_spec(tm, D)], out_specs=_row_spec(tm, D),
        out_shape=jax.ShapeDtypeStruct((L, D), BF16),
        compiler_params=_params(("parallel",)),
    )(y0)


def glu_post(y0, t, b_glu, g_a, *, name, tm=256):
    L, D = y0.shape
    tm = _tile(L, tm, 8)

    def body(y_ref, t_ref, b_ref, g_ref, o_ref):
        ya = _gelu(y_ref[...]) * jax.nn.sigmoid(t_ref[...] + b_ref[...])
        o_ref[...] = (ya * _rms(ya) * g_ref[...]).astype(o_ref.dtype)

    return pl.pallas_call(
        body, name=name, grid=(L // tm,),
        in_specs=[_row_spec(tm, D), _row_spec(tm, D), _vec_spec(D), _vec_spec(D)],
        out_specs=_row_spec(tm, D),
        out_shape=jax.ShapeDtypeStruct((L, D), BF16),
        compiler_params=_params(("parallel",)),
    )(y0, t, b_glu, g_a)


def glu_post_bwd(y0, t, b_glu, g_a, dycat, *, name, tm=128):
    L, D = y0.shape
    tm = _tile(L, tm, 8)

    def body(y_ref, t_ref, b_ref, g_ref, dn_ref, dt_ref, dd_ref, dga_ref, dbg_ref):
        ya1 = _gelu(y_ref[...])
        sg = jax.nn.sigmoid(t_ref[...] + b_ref[...])
        ya = ya1 * sg
        ra = _rms(ya)
        dn = dn_ref[...]
        dya = _rmsnorm_bwd(dn, ya, ra, g_ref[...])
        dt = dya * ya1 * sg * (1.0 - sg)
        dt_ref[...] = dt.astype(BF16)
        dd_ref[...] = dya * sg
        p_ga, p_bg = _rowsum(dn * ya * ra), _rowsum(dt)

        @pl.when(pl.program_id(0) == 0)
        def _():
            dga_ref[...] = p_ga
            dbg_ref[...] = p_bg

        @pl.when(pl.program_id(0) > 0)
        def _():
            dga_ref[...] += p_ga
            dbg_ref[...] += p_bg

    return pl.pallas_call(
        body, name=name, grid=(L // tm,),
        in_specs=[_row_spec(tm, D), _row_spec(tm, D), _vec_spec(D), _vec_spec(D), _row_spec(tm, D, 0)],
        out_specs=[_row_spec(tm, D), _row_spec(tm, D), _vec_spec(D), _vec_spec(D)],
        out_shape=[jax.ShapeDtypeStruct((L, D), BF16), jax.ShapeDtypeStruct((L, D), F32),
                   jax.ShapeDtypeStruct((1, D), F32), jax.ShapeDtypeStruct((1, D), F32)],
        compiler_params=_params(("arbitrary",)),
    )(y0, t, b_glu, g_a, dycat)


def head_and_loss(x2, gpre, b_g, pp, g_f, tgt, *, name, tm=128):
    L, D = x2.shape
    tm = _tile(L, tm, 8)

    def body(x2_ref, gp_ref, bg_ref, pp_ref, gf_ref, tg_ref,
             dx3_ref, dpre_ref, dpp_ref, loss_ref, dgf_ref, dbg_ref):
        gate = jax.nn.sigmoid(gp_ref[...] + bg_ref[...])
        ppv = pp_ref[...]
        x3 = x2_ref[...] + gate * ppv
        r = _rms(x3)
        xn = x3 * r
        gf = gf_ref[...]
        err = xn * gf - tg_ref[...]
        loss = 0.5 * jnp.sum(jnp.mean(err * err, axis=-1, keepdims=True), axis=0, keepdims=True)
        dout = err * (1.0 / D)
        dx3 = _rmsnorm_bwd(dout, x3, r, gf)
        dx3_ref[...] = dx3
        dpre = dx3 * ppv * gate * (1.0 - gate)
        dpre_ref[...] = dpre.astype(BF16)
        dpp_ref[...] = (dx3 * gate).astype(BF16)
        p_gf, p_bg = _rowsum(dout * xn), _rowsum(dpre)
        p_loss = jnp.broadcast_to(loss, loss_ref.shape)

        @pl.when(pl.program_id(0) == 0)
        def _():
            loss_ref[...] = p_loss
            dgf_ref[...] = p_gf
            dbg_ref[...] = p_bg

        @pl.when(pl.program_id(0) > 0)
        def _():
            loss_ref[...] += p_loss
            dgf_ref[...] += p_gf
            dbg_ref[...] += p_bg

    rs = _row_spec(tm, D)
    return pl.pallas_call(
        body, name=name, grid=(L // tm,),
        in_specs=[rs, rs, _vec_spec(D), rs, _vec_spec(D), rs],
        out_specs=[rs, rs, rs, pl.BlockSpec((8, LANE), lambda i: (0, 0)), _vec_spec(D), _vec_spec(D)],
        out_shape=[jax.ShapeDtypeStruct((L, D), F32), jax.ShapeDtypeStruct((L, D), BF16),
                   jax.ShapeDtypeStruct((L, D), BF16), jax.ShapeDtypeStruct((8, LANE), F32),
                   jax.ShapeDtypeStruct((1, D), F32), jax.ShapeDtypeStruct((1, D), F32)],
        compiler_params=_params(("arbitrary",)),
    )(x2, gpre, b_g, pp, g_f, tgt)


def ffn_out_bwd_swiglu(dx, w, gate, up, *, name, tm=512, tf=1408):
    M, D = dx.shape
    F = w.shape[0]
    tm, tf = _tile(M, tm, 8), _tile(F, tf)

    def body(dx_ref, w_ref, g_ref, u_ref, dg_ref, du_ref):
        da = lax.dot_general(dx_ref[...], w_ref[...], (((1,), (1,)), ((), ())), preferred_element_type=F32)
        gv = g_ref[...]
        sg = jax.nn.sigmoid(gv)
        dg_ref[...] = (da * u_ref[...] * sg * (1.0 + gv * (1.0 - sg))).astype(BF16)
        du_ref[...] = (da * gv * sg).astype(BF16)

    spec = pl.BlockSpec((tm, tf), lambda j, i: (i, j))
    return pl.pallas_call(
        body, name=name, grid=(F // tf, M // tm),
        in_specs=[pl.BlockSpec((tm, D), lambda j, i: (i, 0)), pl.BlockSpec((tf, D), lambda j, i: (j, 0)), spec, spec],
        out_specs=[spec, spec],
        out_shape=[jax.ShapeDtypeStruct((M, F), BF16), jax.ShapeDtypeStruct((M, F), BF16)],
        compiler_params=_params(("parallel", "parallel")),
    )(dx, w, gate, up)


def _sgu_forward_values(u1, v1, lng, lnb, w_ref, bs_ref, s_scr, heads, hd):
    xc = v1 - jnp.mean(v1, axis=-1, keepdims=True)
    r = lax.rsqrt(jnp.mean(xc * xc, axis=-1, keepdims=True) + EPS)
    xhat = xc * r
    v2 = xhat * lng + lnb
    tril = (lax.broadcasted_iota(jnp.int32, (SGU_CHUNK, SGU_CHUNK), 0)
            >= lax.broadcasted_iota(jnp.int32, (SGU_CHUNK, SGU_CHUNK), 1))
    for h in range(heads):
        wm = jnp.where(tril, w_ref[h], 0.0).astype(BF16)
        cols = slice(h * hd, (h + 1) * hd)
        s_scr[:, cols] = jnp.dot(wm, v2[:, cols].astype(BF16), preferred_element_type=F32) + bs_ref[h]
    return xhat, r, v2, tril


def sgu_fwd(z, ln_g, ln_b, w_s, b_s, g_b, *, name, d_sgu):
    L = z.shape[0]
    heads = w_s.shape[0]
    hd = d_sgu // heads

    def body(zu_ref, zv_ref, lng_ref, lnb_ref, w_ref, bs_ref, gb_ref, o_ref, s_scr):
        u1 = _gelu(zu_ref[...])
        _sgu_forward_values(u1, _gelu(zv_ref[...]), lng_ref[...], lnb_ref[...], w_ref, bs_ref, s_scr, heads, hd)
        yb = u1 * s_scr[...]
        o_ref[...] = (yb * _rms(yb) * gb_ref[...]).astype(o_ref.dtype)

    blk = lambda col: pl.BlockSpec((SGU_CHUNK, d_sgu), lambda n: (n, col))
    return pl.pallas_call(
        body, name=name, grid=(L // SGU_CHUNK,),
        in_specs=[blk(1), blk(2), _vec_spec(d_sgu), _vec_spec(d_sgu),
                  pl.BlockSpec(w_s.shape, lambda n: (0, 0, 0)), pl.BlockSpec(b_s.shape, lambda n: (0, 0, 0)),
                  _vec_spec(d_sgu)],
        out_specs=blk(0),
        out_shape=jax.ShapeDtypeStruct((L, d_sgu), BF16),
        scratch_shapes=[pltpu.VMEM((SGU_CHUNK, d_sgu), F32)],
        compiler_params=_params(("parallel",)),
    )(z, z, ln_g, ln_b, w_s, b_s, g_b)


def sgu_bwd(z, dycat, ln_g, ln_b, w_s, b_s, g_b, *, name, d_sgu):
    L = z.shape[0]
    heads = w_s.shape[0]
    hd = d_sgu // heads

    def body(zu_ref, zv_ref, dn_ref, lng_ref, lnb_ref, w_ref, bs_ref, gb_ref,
             dzu_ref, dzv_ref, dw_ref, dbs_ref, dlng_ref, dlnb_ref, dgb_ref, s_scr, dv_scr):
        first = pl.program_id(0) == 0
        lng = lng_ref[...]
        u1, du1 = _gelu_and_grad(zu_ref[...])
        v1, dv1_dz = _gelu_and_grad(zv_ref[...])
        xhat, r, v2, tril = _sgu_forward_values(u1, v1, lng, lnb_ref[...], w_ref, bs_ref, s_scr, heads, hd)
        s = s_scr[...]
        yb = u1 * s
        rb = _rms(yb)
        dn = dn_ref[...]
        dyb = _rmsnorm_bwd(dn, yb, rb, gb_ref[...])
        dzu_ref[...] = (dyb * s * du1).astype(BF16)
        ds = dyb * u1
        for h in range(heads):
            cols = slice(h * hd, (h + 1) * hd)
            ds_h = ds[:, cols]
            ds_hb = ds_h.astype(BF16)
            wm = jnp.where(tril, w_ref[h], 0.0).astype(BF16)
            dw_h = jnp.where(tril, lax.dot_general(ds_hb, v2[:, cols].astype(BF16), (((1,), (1,)), ((), ())),
                                                   preferred_element_type=F32), 0.0)
            db_h = jnp.sum(ds_h.T, axis=0, keepdims=True)
            dv_scr[:, cols] = lax.dot_general(wm, ds_hb, (((0,), (0,)), ((), ())), preferred_element_type=F32)

            @pl.when(first)
            def _():
                dw_ref[h] = dw_h
                dbs_ref[h] = db_h

            @pl.when(jnp.logical_not(first))
            def _():
                dw_ref[h] += dw_h
                dbs_ref[h] += db_h

        dv2 = dv_scr[...]
        dxh = dv2 * lng
        dv1 = r * (dxh - jnp.mean(dxh, axis=-1, keepdims=True)
                   - xhat * jnp.mean(dxh * xhat, axis=-1, keepdims=True))
        dzv_ref[...] = (dv1 * dv1_dz).astype(BF16)
        p_lng, p_lnb, p_gb = _rowsum(dv2 * xhat), _rowsum(dv2), _rowsum(dn * yb * rb)

        @pl.when(first)
        def _():
            dlng_ref[...] = p_lng
            dlnb_ref[...] = p_lnb
            dgb_ref[...] = p_gb

        @pl.when(jnp.logical_not(first))
        def _():
            dlng_ref[...] += p_lng
            dlnb_ref[...] += p_lnb
            dgb_ref[...] += p_gb

    blk = lambda col: pl.BlockSpec((SGU_CHUNK, d_sgu), lambda n: (n, col))
    full3 = lambda shape: pl.BlockSpec(shape, lambda n: (0, 0, 0))
    return pl.pallas_call(
        body, name=name, grid=(L // SGU_CHUNK,),
        in_specs=[blk(1), blk(2), blk(1), _vec_spec(d_sgu), _vec_spec(d_sgu),
                  full3(w_s.shape), full3(b_s.shape), _vec_spec(d_sgu)],
        out_specs=[blk(0), blk(0), full3(w_s.shape), full3((heads, 1, SGU_CHUNK)),
                   _vec_spec(d_sgu), _vec_spec(d_sgu), _vec_spec(d_sgu)],
        out_shape=[jax.ShapeDtypeStruct((L, d_sgu), BF16), jax.ShapeDtypeStruct((L, d_sgu), BF16),
                   jax.ShapeDtypeStruct(w_s.shape, F32), jax.ShapeDtypeStruct((heads, 1, SGU_CHUNK), F32),
                   jax.ShapeDtypeStruct((1, d_sgu), F32), jax.ShapeDtypeStruct((1, d_sgu), F32),
                   jax.ShapeDtypeStruct((1, d_sgu), F32)],
        scratch_shapes=[pltpu.VMEM((SGU_CHUNK, d_sgu), F32), pltpu.VMEM((SGU_CHUNK, d_sgu), F32)],
        compiler_params=_params(("arbitrary",)),
    )(z, z, dycat, ln_g, ln_b, w_s, b_s, g_b)


def _disc_lambda(lam_re, lam_im, log_step):
    lr = jnp.minimum(lam_re, LAMBDA_RE_MAX)
    li = lam_im
    dt = jnp.exp(log_step)
    mag = jnp.exp(lr * dt)
    ang = li * dt
    a_re = mag * jnp.cos(ang)
    a_im = mag * jnp.sin(ang)
    nr = a_re - 1.0
    ni = a_im
    den = lr * lr + li * li
    return a_re, a_im, (nr * lr + ni * li) / den, (ni * lr - nr * li) / den


def _disc_b(q_re, q_im, b_re, b_im):
    return q_re * b_re - q_im * b_im, q_re * b_im + q_im * b_re


def disc_lambda_fwd(lam_re, lam_im, log_step, *, name):
    def body(lr_ref, li_ref, ls_ref, ar_ref, ai_ref, qr_ref, qi_ref):
        ar_ref[...], ai_ref[...], qr_ref[...], qi_ref[...] = _disc_lambda(lr_ref[...], li_ref[...], ls_ref[...])

    sd = jax.ShapeDtypeStruct(lam_re.shape, F32)
    return pl.pallas_call(body, name=name, out_shape=[sd, sd, sd, sd], compiler_params=_params())(
        lam_re, lam_im, log_step)


def disc_lambda_bwd(lam_re, lam_im, log_step, cts, *, name):
    def body(lr_ref, li_ref, ls_ref, c0, c1, c2, c3, dlr_ref, dli_ref, dls_ref):
        _, vjp = jax.vjp(_disc_lambda, lr_ref[...], li_ref[...], ls_ref[...])
        dlr_ref[...], dli_ref[...], dls_ref[...] = vjp((c0[...], c1[...], c2[...], c3[...]))

    sd = jax.ShapeDtypeStruct(lam_re.shape, F32)
    return pl.pallas_call(body, name=name, out_shape=[sd, sd, jax.ShapeDtypeStruct(log_step.shape, F32)],
                          compiler_params=_params())(lam_re, lam_im, log_step, *cts)


def disc_b_fwd(q_re, q_im, b_re, b_im, *, name):
    def body(qr_ref, qi_ref, br_ref, bi_ref, or_ref, oi_ref):
        or_ref[...], oi_ref[...] = _disc_b(qr_ref[...], qi_ref[...], br_ref[...], bi_ref[...])

    sd = jax.ShapeDtypeStruct(b_re.shape, F32)
    return pl.pallas_call(body, name=name, out_shape=[sd, sd], compiler_params=_params())(q_re, q_im, b_re, b_im)


def disc_b_bwd(q_re, q_im, b_re, b_im, ct_re, ct_im, *, name):
    def body(qr_ref, qi_ref, br_ref, bi_ref, cr_ref, ci_ref, dqr_ref, dqi_ref, dbr_ref, dbi_ref):
        _, vjp = jax.vjp(_disc_b, qr_ref[...], qi_ref[...], br_ref[...], bi_ref[...])
        dqr_ref[...], dqi_ref[...], dbr_ref[...], dbi_ref[...] = vjp((cr_ref[...], ci_ref[...]))

    sq, sb = jax.ShapeDtypeStruct(q_re.shape, F32), jax.ShapeDtypeStruct(b_re.shape, F32)
    return pl.pallas_call(body, name=name, out_shape=[sq, sq, sb, sb], compiler_params=_params())(
        q_re, q_im, b_re, b_im, ct_re, ct_im)


def _lti_scan(xr, xi, ar, ai, reverse):
    T = xr.shape[0]
    row = lax.broadcasted_iota(jnp.int32, xr.shape, 0)
    k = 1
    while k < T:
        shift = T - k if reverse else k
        keep = (row < T - k) if reverse else (row >= k)
        sr = jnp.where(keep, pltpu.roll(xr, shift, 0), 0.0)
        si = jnp.where(keep, pltpu.roll(xi, shift, 0), 0.0)
        xr, xi = xr + ar * sr - ai * si, xi + ar * si + ai * sr
        ar, ai = ar * ar - ai * ai, 2.0 * ar * ai
        k *= 2
    return xr, xi


SUBLANES = 8


def _scan_rows(x_re, x_im, o_re, o_im, ar, ai, cr, ci, reverse):
    T, n = x_re.shape
    groups = T // SUBLANES
    row = lax.broadcasted_iota(jnp.int32, (SUBLANES, n), 0)
    edge = SUBLANES - 1 if reverse else 0
    pr, pi = _lti_scan(jnp.where(row == edge, ar, 0.0), jnp.where(row == edge, ai, 0.0), ar, ai, reverse)
    pows = []
    for level in range(3):
        k = 1 << level
        keep = (row < SUBLANES - k) if reverse else (row >= k)
        pows.append((jnp.where(keep, ar, 0.0), jnp.where(keep, ai, 0.0)))
        ar, ai = ar * ar - ai * ai, 2.0 * ar * ai

    def group(i, carry):
        cr, ci = carry
        at = pl.multiple_of((groups - 1 - i if reverse else i) * SUBLANES, SUBLANES)
        xr, xi = x_re[pl.ds(at, SUBLANES), :], x_im[pl.ds(at, SUBLANES), :]
        for level, (qr, qi) in enumerate(pows):
            shift = SUBLANES - (1 << level) if reverse else 1 << level
            sr, si = pltpu.roll(xr, shift, 0), pltpu.roll(xi, shift, 0)
            xr, xi = xr + qr * sr - qi * si, xi + qr * si + qi * sr
        xr, xi = xr + pr * cr - pi * ci, xi + pr * ci + pi * cr
        o_re[pl.ds(at, SUBLANES), :] = xr
        o_im[pl.ds(at, SUBLANES), :] = xi
        return spread(xr[last:last + 1, :]), spread(xi[last:last + 1, :])

    last = 0 if reverse else SUBLANES - 1
    spread = lambda v: jnp.broadcast_to(v, (SUBLANES, n))
    cr, ci = lax.fori_loop(0, groups, group, (spread(cr), spread(ci)), unroll=2)
    return cr[0:1, :], ci[0:1, :]


def _ssm_chunk(L):
    return _tile(L, 512, 8)


def ssm_fwd(z, bs_re, bs_im, cs_re, cs_im, a_re, a_im, d, *, name, bg=None):
    L = z.shape[0]
    NK, C, S = bs_re.shape
    T = _ssm_chunk(L)
    grid = (NK, L // T)
    bg_in_specs, bg_args, bg_out_specs, bg_out_shapes, bg_scratch, split = _carrier(bg, 8, 3, 4, grid)

    def body(*refs):
        ((u_ref, br_ref, bi_ref, cr_ref, ci_ref, ar_ref, ai_ref, d_ref), (y_ref, sr_ref, si_ref),
         (car_re, car_im, bu_re, bu_im), run_background) = split(refs)
        run_background()
        i = pl.program_id(1)
        ar, ai = ar_ref[...], ai_ref[...]

        @pl.when(i == 0)
        def _():
            car_re[...] = jnp.zeros_like(car_re)
            car_im[...] = jnp.zeros_like(car_im)

        u = u_ref[...]
        ub = u.astype(BF16)
        bu_re[...] = jnp.dot(ub, br_ref[...], preferred_element_type=F32)
        bu_im[...] = jnp.dot(ub, bi_ref[...], preferred_element_type=F32)
        car_re[...], car_im[...] = _scan_rows(bu_re, bu_im, sr_ref, si_ref, ar, ai, car_re[...], car_im[...], False)
        y_ref[...] = (jnp.dot(sr_ref[...].astype(BF16), cr_ref[...], preferred_element_type=F32)
                      - jnp.dot(si_ref[...].astype(BF16), ci_ref[...], preferred_element_type=F32)
                      + d_ref[...] * u)

    kspec = lambda shape: pl.BlockSpec((None,) + shape, lambda k, i: (k, 0, 0))
    res = pl.pallas_call(
        body, name=name, grid=grid,
        in_specs=[pl.BlockSpec((T, C), lambda k, i: (i, k)),
                  kspec((C, S)), kspec((C, S)), kspec((S, C)), kspec((S, C)),
                  kspec((1, S)), kspec((1, S)), kspec((1, C))] + bg_in_specs,
        out_specs=[pl.BlockSpec((T, C), lambda k, i: (i, k)),
                   pl.BlockSpec((T, S), lambda k, i: (i, k)), pl.BlockSpec((T, S), lambda k, i: (i, k))] + bg_out_specs,
        out_shape=[jax.ShapeDtypeStruct((L, NK * C), F32), jax.ShapeDtypeStruct((L, NK * S), F32),
                   jax.ShapeDtypeStruct((L, NK * S), F32)] + bg_out_shapes,
        scratch_shapes=[pltpu.VMEM((1, S), F32), pltpu.VMEM((1, S), F32),
                        pltpu.VMEM((T, S), F32), pltpu.VMEM((T, S), F32)] + bg_scratch,
        compiler_params=_params(_semantics(bg, ("parallel", "arbitrary"))),
    )(z, bs_re, bs_im, cs_re, cs_im, a_re, a_im, d, *bg_args)
    return _results(res, 3, bg)


def ssm_bwd(z, y0, dd_direct, dd_mm, s_re, s_im, bs_re, bs_im, cs_re, cs_im, a_re, a_im, d, *, name, bg=None):
    L = z.shape[0]
    NK, C, S = bs_re.shape
    T = _ssm_chunk(L)
    nchunk = L // T
    tail = T // 8
    grid = (NK, nchunk)
    bg_in_specs, bg_args, bg_out_specs, bg_out_shapes, bg_scratch, split = _carrier(bg, 15, 8, 4, grid)
    nt_dot = lambda p, q: lax.dot_general(p, q, (((1,), (1,)), ((), ())), preferred_element_type=F32)

    def body(*refs):
        ((u_ref, y_ref, d1_ref, d2_ref, sr_ref, si_ref, pr_ref, pi_ref,
          br_ref, bi_ref, cr_ref, ci_ref, ar_ref, ai_ref, d_ref),
         (du_ref, dbr_ref, dbi_ref, dcr_ref, dci_ref, dar_ref, dai_ref, dd_ref),
         (car_re, car_im, lam_re, lam_im), run_background) = split(refs)
        run_background()
        i = pl.program_id(1)
        chunk = nchunk - 1 - i
        ar, ai = ar_ref[...], ai_ref[...]
        row = lax.broadcasted_iota(jnp.int32, (T, S), 0)

        @pl.when(i == 0)
        def _():
            car_re[...] = jnp.zeros_like(car_re)
            car_im[...] = jnp.zeros_like(car_im)

        u = u_ref[...]
        dy = (d1_ref[...] + d2_ref[...]) * _gelu_grad(y_ref[...])
        dyb = dy.astype(BF16)
        lam_re[...] = nt_dot(dyb, cr_ref[...])
        lam_im[...] = -nt_dot(dyb, ci_ref[...])
        car_re[...], car_im[...] = _scan_rows(lam_re, lam_im, lam_re, lam_im, ar, -ai, car_re[...], car_im[...], True)
        lr, li = lam_re[...], lam_im[...]

        s_re, s_im = sr_ref[...], si_ref[...]
        has_prev = (chunk > 0).astype(F32)
        prev_re = pr_ref[7:8, :] * has_prev
        prev_im = pi_ref[7:8, :] * has_prev
        sp_re = jnp.where(row == 0, prev_re, pltpu.roll(s_re, 1, 0))
        sp_im = jnp.where(row == 0, prev_im, pltpu.roll(s_im, 1, 0))
        p_ar = _rowsum(lr * sp_re + li * sp_im)
        p_ai = _rowsum(li * sp_re - lr * sp_im)

        lrb, lib, ub = lr.astype(BF16), li.astype(BF16), u.astype(BF16)
        du = dy * d_ref[...] + nt_dot(lrb, br_ref[...]) + nt_dot(lib, bi_ref[...])
        du_ref[...] = du.astype(BF16)
        tdot = lambda p, q: lax.dot_general(p, q, (((0,), (0,)), ((), ())), preferred_element_type=F32)
        p_br, p_bi = tdot(ub, lrb), tdot(ub, lib)
        p_cr, p_ci = tdot(s_re.astype(BF16), dyb), -tdot(s_im.astype(BF16), dyb)
        p_dd = _rowsum(dy * u)

        @pl.when(i == 0)
        def _():
            dar_ref[...] = p_ar
            dai_ref[...] = p_ai
            dbr_ref[...] = p_br
            dbi_ref[...] = p_bi
            dcr_ref[...] = p_cr
            dci_ref[...] = p_ci
            dd_ref[...] = p_dd

        @pl.when(i > 0)
        def _():
            dar_ref[...] += p_ar
            dai_ref[...] += p_ai
            dbr_ref[...] += p_br
            dbi_ref[...] += p_bi
            dcr_ref[...] += p_cr
            dci_ref[...] += p_ci
            dd_ref[...] += p_dd

    rev = lambda k, i: (nchunk - 1 - i, k)
    prev = lambda k, i: (jnp.maximum((nchunk - 1 - i) * tail - 1, 0), k)
    kspec = lambda shape: pl.BlockSpec((None,) + shape, lambda k, i: (k, 0, 0))
    res = pl.pallas_call(
        body, name=name, grid=grid,
        in_specs=[pl.BlockSpec((T, C), rev), pl.BlockSpec((T, C), rev), pl.BlockSpec((T, C), rev),
                  pl.BlockSpec((T, C), rev), pl.BlockSpec((T, S), rev), pl.BlockSpec((T, S), rev),
                  pl.BlockSpec((8, S), prev), pl.BlockSpec((8, S), prev),
                  kspec((C, S)), kspec((C, S)), kspec((S, C)), kspec((S, C)),
                  kspec((1, S)), kspec((1, S)), kspec((1, C))] + bg_in_specs,
        out_specs=[pl.BlockSpec((T, C), rev), kspec((C, S)), kspec((C, S)), kspec((S, C)), kspec((S, C)),
                   kspec((1, S)), kspec((1, S)), kspec((1, C))] + bg_out_specs,
        out_shape=[jax.ShapeDtypeStruct((L, NK * C), BF16),
                   jax.ShapeDtypeStruct((NK, C, S), F32), jax.ShapeDtypeStruct((NK, C, S), F32),
                   jax.ShapeDtypeStruct((NK, S, C), F32), jax.ShapeDtypeStruct((NK, S, C), F32),
                   jax.ShapeDtypeStruct((NK, 1, S), F32), jax.ShapeDtypeStruct((NK, 1, S), F32),
                   jax.ShapeDtypeStruct((NK, 1, C), F32)] + bg_out_shapes,
        scratch_shapes=[pltpu.VMEM((1, S), F32), pltpu.VMEM((1, S), F32),
                        pltpu.VMEM((T, S), F32), pltpu.VMEM((T, S), F32)] + bg_scratch,
        compiler_params=_params(_semantics(bg, ("parallel", "arbitrary"))),
    )(z, y0, dd_direct, dd_mm, s_re, s_im, s_re, s_im, bs_re, bs_im, cs_re, cs_im, a_re, a_im, d, *bg_args)
    return _results(res, 8, bg)


def _block_diag(v):
    NK, SG, R, Q = v.shape
    eye = jnp.eye(SG, dtype=v.dtype)
    return (v[:, :, :, None, :] * eye[None, :, None, :, None]).reshape(NK, SG * R, SG * Q)


def _block_diag_part(m, SG):
    NK, RR, QQ = m.shape
    R, Q = RR // SG, QQ // SG
    eye = jnp.eye(SG, dtype=m.dtype)
    return jnp.sum(m.reshape(NK, SG, R, SG, Q) * eye[None, :, None, :, None], axis=3)


def _position():
    return lax.axis_index("x"), lax.axis_index("y"), lax.axis_index("c")


def _other_chips(x, y):
    return [(1 - x, y), (x, 1 - y), (1 - x, 1 - y)]


def _gather_phases(n, rows=None):
    def parts(ins, outs, sems):
        send_sems, recv_sems, local_sems = sems
        x, y, c = _position()
        me, sibling = (x, y, c), (x, y, 1 - c)
        chips = _other_chips(x, y)

        def block(a, pos):
            index = 4 * pos[0] + 2 * pos[1] + pos[2]
            if rows is not None:
                return outs[a].at[index, pl.ds(*rows)]
            return outs[a].at[pl.ds(index, 1)] if _is_row(ins[a]) else outs[a].at[index]

        def copy(a, k, pos, to, src=None):
            return pltpu.make_async_remote_copy(
                src_ref=block(a, pos) if src is None else src, dst_ref=block(a, pos),
                send_sem=send_sems.at[7 * a + k], recv_sem=recv_sems.at[7 * a + k],
                device_id=to, device_id_type=MESH)

        shard = [ins[a] if rows is None else ins[a].at[pl.ds(*rows)] for a in range(n)]
        mine = [pltpu.make_async_copy(shard[a], block(a, me), local_sems.at[a]) for a in range(n)]
        first = []
        for a in range(n):
            first.append(copy(a, 0, me, sibling, src=shard[a]))
            first += [copy(a, 1 + j, me, (*chip, c), src=shard[a]) for j, chip in enumerate(chips)]
        passed = [copy(a, 4 + j, (*chip, c), sibling) for a in range(n) for j, chip in enumerate(chips)]
        arrived = [copy(a, 1 + j, (*chip, c), me) for a in range(n) for j, chip in enumerate(chips)]
        from_sibling = []
        for a in range(n):
            from_sibling.append(copy(a, 0, sibling, me))
            from_sibling += [copy(a, 4 + j, (*chip, 1 - c), me) for j, chip in enumerate(chips)]
        return mine, first, passed, arrived, from_sibling

    def send(ins, outs, sems):
        mine, first, _, _, _ = parts(ins, outs, sems)
        for cp in mine + first:
            cp.start()

    def forward(ins, outs, sems):
        _, _, passed, arrived, _ = parts(ins, outs, sems)
        for got, fwd in zip(arrived, passed):
            got.wait_recv()
            fwd.start()

    def finish(ins, outs, sems):
        mine, first, passed, _, from_sibling = parts(ins, outs, sems)
        for cp in from_sibling:
            cp.wait_recv()
        for cp in first + passed:
            cp.wait_send()
        for cp in mine:
            cp.wait()

    return [(0.0, send), (GATHER_FORWARD_AT, forward), (1.0, finish)]


def _is_row(a):
    return len(a.shape) == 2 and a.shape[0] == 1


def _gather_shapes(shards):
    n = len(shards)
    return ([jax.ShapeDtypeStruct((N_DEV,) + (s.shape[1:] if _is_row(s) else s.shape), s.dtype) for s in shards],
            [pltpu.SemaphoreType.DMA((7 * n,)), pltpu.SemaphoreType.DMA((7 * n,)), pltpu.SemaphoreType.DMA((n,))])


def gather_background(shards, rows=None, into=None):
    out_shapes, scratch = _gather_shapes(shards)
    bg = Background(list(shards) + list(into or []), out_shapes, scratch, _gather_phases(len(shards), rows))
    bg.aliases = {len(shards) + k: k for k in range(len(into or []))}
    return bg


def all_gather_blocks(shards, *, name):
    n = len(shards)
    out_shapes, scratch = _gather_shapes(shards)

    def body(*refs):
        for _, phase in _gather_phases(n):
            phase(refs[:n], refs[n:2 * n], refs[2 * n:])

    return pl.pallas_call(
        body, name=name, in_specs=[ANY] * n, out_specs=[ANY] * n, out_shape=out_shapes, scratch_shapes=scratch,
    )(*shards)


def sibling_exchange(grads, *, name):
    n = len(grads)
    bg = sibling_exchange_background(grads)

    def body(*refs):
        for _, phase in bg.phases:
            phase(refs[:n], refs[n:2 * n], refs[2 * n:])

    return pl.pallas_call(
        body, name=name, in_specs=[ANY] * n, out_specs=[ANY] * n, out_shape=bg.out_shapes, scratch_shapes=bg.scratch,
    )(*grads)


def sibling_exchange_background(grads):
    n = len(grads)

    def copies(ins, outs, sems):
        x, y, c = _position()
        return [pltpu.make_async_remote_copy(
            src_ref=ins[a].at[2 * q + 1 - c], dst_ref=outs[a].at[q],
            send_sem=sems[0].at[4 * a + q], recv_sem=sems[1].at[4 * a + q],
            device_id=(x, y, 1 - c), device_id_type=MESH)
            for a in range(n) for q in range(4)]

    def send(ins, outs, sems):
        for cp in copies(ins, outs, sems):
            cp.start()

    def finish(ins, outs, sems):
        for cp in copies(ins, outs, sems):
            cp.wait()

    return Background(grads, [jax.ShapeDtypeStruct((4,) + g.shape[1:], g.dtype) for g in grads],
                      [pltpu.SemaphoreType.DMA((4 * n,)), pltpu.SemaphoreType.DMA((4 * n,))],
                      [(0.0, send), (1.0, finish)])


def _chip_exchange_phases(n):
    def copies(ins, outs, sems):
        x, y, c = _position()
        return [pltpu.make_async_remote_copy(
            src_ref=ins[a].at[2 * chip[0] + chip[1]], dst_ref=outs[a].at[j],
            send_sem=sems[0].at[3 * a + j], recv_sem=sems[1].at[3 * a + j],
            device_id=(*chip, c), device_id_type=MESH)
            for a in range(n) for j, chip in enumerate(_other_chips(x, y))]

    def send(ins, outs, sems):
        for cp in copies(ins, outs, sems):
            cp.start()

    def finish(ins, outs, sems):
        for cp in copies(ins, outs, sems):
            cp.wait()

    return [(0.0, send), (1.0, finish)]


def chip_exchange_background(parts):
    n = len(parts)
    return Background(parts, [jax.ShapeDtypeStruct((3,) + p.shape[1:], p.dtype) for p in parts],
                      [pltpu.SemaphoreType.DMA((3 * n,)), pltpu.SemaphoreType.DMA((3 * n,))],
                      _chip_exchange_phases(n))


def add_pairs(grads, theirs, core, *, name, tm=512):
    _, R, C = theirs.shape
    tm = _tile(R, tm, 16)

    def body(core_ref, a_ref, b_ref, o_ref):
        o_ref[...] = (a_ref[...].astype(F32) + b_ref[...].astype(F32)).astype(o_ref.dtype)

    spec = pl.BlockSpec((None, tm, C), lambda q, i, core_ref: (q, i, 0))
    return pl.pallas_call(
        body, name=name,
        grid_spec=pltpu.PrefetchScalarGridSpec(
            num_scalar_prefetch=1, grid=(4, R // tm),
            in_specs=[pl.BlockSpec((None, tm, C), lambda q, i, core_ref: (2 * q + core_ref[0], i, 0)), spec],
            out_specs=spec),
        out_shape=jax.ShapeDtypeStruct(theirs.shape, BF16),
        compiler_params=_params(("parallel", "parallel")),
    )(core, grads, theirs)


def _adamw(w, g, m, v):
    m = ADAM_B1 * m + (1.0 - ADAM_B1) * g
    v = ADAM_B2 * v + (1.0 - ADAM_B2) * (g * g)
    m_hat = m / (1.0 - ADAM_B1 ** ADAM_STEP)
    v_hat = v / (1.0 - ADAM_B2 ** ADAM_STEP)
    delta = -ADAM_LR * (m_hat / (jnp.sqrt(v_hat) + ADAM_EPS) + ADAM_WD * w)
    return delta, m, v


def adamw_sharded(w, m, v, grads, theirs, others, where, *, name, tm=256):
    R, C = w.shape
    tm = _tile(R, tm, 16)

    def body(where_ref, w_ref, m_ref, v_ref, a_ref, b_ref, o_ref, g_ref, d_ref, nm_ref, nv_ref):
        g = a_ref[...].astype(F32) + b_ref[...].astype(F32)
        for j in range(3):
            g = g + o_ref[j].astype(F32)
        g_ref[...] = g
        d_ref[...], nm_ref[...], nv_ref[...] = _adamw(w_ref[...], g, m_ref[...], v_ref[...])

    spec = pl.BlockSpec((tm, C), lambda i, where_ref: (i, 0))
    sd = jax.ShapeDtypeStruct((R, C), F32)
    return pl.pallas_call(
        body, name=name,
        grid_spec=pltpu.PrefetchScalarGridSpec(
            num_scalar_prefetch=1, grid=(R // tm,),
            in_specs=[spec, spec, spec,
                      pl.BlockSpec((None, tm, C), lambda i, where_ref: (where_ref[0], i, 0)),
                      pl.BlockSpec((None, tm, C), lambda i, where_ref: (where_ref[1], i, 0)),
                      pl.BlockSpec((3, tm, C), lambda i, where_ref: (0, i, 0))],
            out_specs=[spec, spec, spec, spec]),
        out_shape=[sd, sd, sd, sd],
        compiler_params=_params(("parallel",)),
    )(where, w, m, v, grads, theirs, others)


def sum_gathered(gathered, *, name):
    n = len(gathered)

    def body(*refs):
        for ga_ref, o_ref in zip(refs[:n], refs[n:]):
            rows = len(ga_ref.shape) == 2
            total = ga_ref[0:1] if rows else ga_ref[0]
            for dev in range(1, N_DEV):
                total = total + (ga_ref[dev:dev + 1] if rows else ga_ref[dev])
            o_ref[...] = total

    shapes = [jax.ShapeDtypeStruct((1,) + g.shape[1:] if g.ndim == 2 else g.shape[1:], F32) for g in gathered]
    return pl.pallas_call(body, name=name, out_shape=shapes, compiler_params=_params())(*gathered)


def adamw_replicated(ws, ms, vs, gs, *, name):
    n = len(ws)

    def body(*refs):
        w_refs, m_refs, v_refs, g_refs = refs[:n], refs[n:2 * n], refs[2 * n:3 * n], refs[3 * n:4 * n]
        outs = refs[4 * n:]
        for k in range(n):
            outs[k][...], outs[n + k][...], outs[2 * n + k][...] = _adamw(
                w_refs[k][...], g_refs[k][...], m_refs[k][...], v_refs[k][...])

    shapes = [jax.ShapeDtypeStruct(t.shape, F32) for t in ws]
    res = pl.pallas_call(body, name=name, out_shape=shapes * 3, compiler_params=_params())(*ws, *ms, *vs, *gs)
    return res[:n], res[n:2 * n], res[2 * n:]


SHARDED = ("w_in", "ssm_glu_w", "w_out", "w_ffn_in", "w_ffn_out", "w_ple_gate", "w_ple_proj")
SMALL_LAST = ("norm_mix_g",)
SMALL_WIDE = ("ssm_b_re", "ssm_b_im", "ssm_c_re", "ssm_c_im")
SMALL = ("ssm_lambda_re", "ssm_lambda_im", "ssm_log_step", "ssm_b_re", "ssm_b_im", "ssm_c_re",
         "ssm_c_im", "ssm_d", "ssm_glu_b", "sgu_ln_g", "sgu_ln_b", "sgu_w", "sgu_b", "out_norm_ssm_g",
         "out_norm_sgu_g", "norm_ffn_g", "norm_ple_g", "b_ple_gate", "final_norm_g")
WEIGHTS = ("norm_mix_g", "w_in", "ssm_lambda_re", "ssm_lambda_im", "ssm_log_step", "ssm_b_re", "ssm_b_im",
           "ssm_c_re", "ssm_c_im", "ssm_d", "ssm_glu_w", "ssm_glu_b", "sgu_ln_g", "sgu_ln_b", "sgu_w", "sgu_b",
           "out_norm_ssm_g", "out_norm_sgu_g", "w_out", "norm_ffn_g", "w_ffn_in", "w_ffn_out", "norm_ple_g",
           "w_ple_gate", "b_ple_gate", "w_ple_proj", "final_norm_g")


def _step(x, p, loss_target, w, m, v):
    L, D = x.shape[1], x.shape[2]
    x2d, p2d, tgt = x.reshape(L, D), p.reshape(L, -1), loss_target.reshape(L, D)
    d_ssm = w["ssm_glu_w"].shape[2]
    d_sgu = w["sgu_ln_g"].shape[1]
    G, P, H = w["ssm_b_re"].shape[1:]
    SG = min(SSM_SUPER, G)
    NK = G // SG
    row = lambda a: a.reshape(1, -1)

    shard2d = {n: w[n].reshape(w[n].shape[1:]) for n in SHARDED}
    shard_bf = {n: shard2d[n].astype(BF16) for n in SHARDED}
    (w_ple_blk,) = all_gather_blocks([shard_bf["w_ple_proj"]], name="gather_w_ple")
    bf = lambda t: t.astype(BF16)
    pp, (w_in_blk,) = mm_nn(bf(p2d), w_ple_blk, name="ple_proj", out_dtype=F32, tm=512, tn=512, tk=2048,
                            bg=gather_background([shard_bf["w_in"]]))
    w_in = jnp.transpose(w_in_blk, (1, 0, 2)).reshape(D, -1)
    F = shard2d["w_ffn_in"].shape[1] * 4

    lam_re, lam_im, log_step = w["ssm_lambda_re"][0], w["ssm_lambda_im"][0], w["ssm_log_step"][0].reshape(G, 1)
    a_re, a_im, q_re, q_im = disc_lambda_fwd(lam_re, lam_im, log_step, name="s5_discretise_lambda")
    bt_re = w["ssm_b_re"][0].transpose(2, 0, 1).reshape(H, G * P)
    bt_im = w["ssm_b_im"][0].transpose(2, 0, 1).reshape(H, G * P)
    bbar_re, bbar_im = disc_b_fwd(row(q_re), row(q_im), bt_re, bt_im, name="s5_discretise_b")
    to_bs = lambda t: _block_diag(t.reshape(H, NK, SG, P).transpose(1, 2, 0, 3))
    to_cs = lambda t: _block_diag(t.reshape(NK, SG, H, P).transpose(0, 1, 3, 2))
    bs_re, bs_im = to_bs(bbar_re), to_bs(bbar_im)
    cs_re, cs_im = to_cs(w["ssm_c_re"][0]), to_cs(w["ssm_c_im"][0])
    a_re_k, a_im_k = a_re.reshape(NK, 1, SG * P), a_im.reshape(NK, 1, SG * P)
    d_k = w["ssm_d"][0].reshape(NK, 1, SG * H)
    bf = lambda t: t.astype(BF16)

    h1 = norm_fwd(x2d, w["norm_mix_g"], name="norm_mix")
    z, (w_glu, w_out) = mm_nn(h1, w_in, name="in_proj", out_dtype=F32, tm=512, tn=1024, tk=2048,
                              bg=gather_background([shard_bf["ssm_glu_w"], shard_bf["w_out"]]))
    w_glu, w_out = w_glu.reshape(d_ssm, d_ssm), w_out.reshape(D, D)
    first_rows = (D * 11 // 16) // 16 * 16
    s5_mats = (bf(bs_re), bf(bs_im), bf(cs_re), bf(cs_im), a_re_k, a_im_k, d_k)
    (y0, s_re, s_im), (w_ffn_in_part,) = ssm_fwd(
        z, *s5_mats, name="s5_scan", bg=gather_background([shard_bf["w_ffn_in"]], rows=(0, first_rows)))
    ya1 = glu_pre(y0, name="s5_gelu")
    t_glu = mm_nn(ya1, w_glu, name="s5_glu_proj", out_dtype=F32, tm=512, tn=512, tk=2048)
    n_a = glu_post(y0, t_glu, w["ssm_glu_b"], w["out_norm_ssm_g"], name="s5_glu_norm")
    b_s3 = w["sgu_b"][0][:, :, None]
    n_b = sgu_fwd(z, w["sgu_ln_g"], w["sgu_ln_b"], w["sgu_w"][0], b_s3, w["out_norm_sgu_g"], name="sgu", d_sgu=d_sgu)
    ycat = jnp.concatenate([n_a, n_b], axis=1)
    x1, (w_ffn_in_blk,) = mm_nn(
        ycat, w_out, name="out_proj", out_dtype=F32, tm=512, tn=512, tk=2048, residual=x2d,
        bg=gather_background([shard_bf["w_ffn_in"]], rows=(first_rows, D - first_rows), into=[w_ffn_in_part]))
    h2 = norm_fwd(x1, w["norm_ffn_g"], name="norm_ffn")
    (act, gate_ff, up_ff), (w_ffn_out, w_gate) = ffn_in_swiglu(
        h2, w_ffn_in_blk, name="ffn_in_swiglu", tm=256,
        bg=gather_background([shard_bf["w_ffn_out"], shard_bf["w_ple_gate"]]))
    w_ffn_out, w_gate = w_ffn_out.reshape(F, D), w_gate.reshape(D, D)
    x2 = mm_nn(act, w_ffn_out, name="ffn_out", out_dtype=F32, tm=512, tn=512, tk=F, residual=x1)
    h3 = norm_fwd(x2, w["norm_ple_g"], name="norm_ple")
    gpre = mm_nn(h3, w_gate, name="ple_gate", out_dtype=F32, tm=512, tn=1024, tk=2048)

    dx3, dpre, dpp, loss_part, d_final_g, d_b_gate = head_and_loss(
        x2, gpre, w["b_ple_gate"], pp, row(w["final_norm_g"]), tgt, name="head_and_loss")
    x_pos, y_pos, c_pos = _position()
    where = jnp.stack([4 * x_pos + 2 * y_pos + c_pos, 2 * x_pos + y_pos]).astype(jnp.int32)
    core = jnp.reshape(c_pos, (1,)).astype(jnp.int32)
    own, others = {}, {}

    def blocks(named):
        g8 = {n: t.reshape((N_DEV,) + shard2d[n].shape) for n, t in named.items()}
        return g8, sibling_exchange_background(list(g8.values()))

    def chip_sums(g8, theirs):
        own.update(zip(g8, zip(g8.values(), theirs)))
        return [add_pairs(g, t, core, name="chip_sum_" + n) for (n, g), t in zip(g8.items(), theirs)]

    d_w_gate = mm_tn(h3, dpre, name="d_w_ple_gate", out_dtype=BF16, tm=L, tko=1024, tno=1024)
    d_w_ple = mm_tn(bf(p2d), dpp, name="d_w_ple_proj", out_dtype=BF16, tm=L, tko=1024, tno=1024, out_blocks=N_DEV)
    g8_ple, bg = blocks({"w_ple_gate": d_w_gate, "w_ple_proj": d_w_ple})
    dh3, theirs = mm_nt(dpre, w_gate, name="d_h_ple", out_dtype=F32, tm=512, tko=1024, tc=2048, bg=bg)
    bg = chip_exchange_background(chip_sums(g8_ple, theirs))
    dx2, dx2b, d_ple_g = norm_bwd(dh3, x2, w["norm_ple_g"], dx3, name="d_norm_ple", want_bf16=True)
    d_w_ffn_out, got = mm_tn(act, dx2b, name="d_w_ffn_out", out_dtype=BF16, tm=L, tko=1408, tno=512, bg=bg)
    others.update(zip(g8_ple, got))
    g8_fo, bg = blocks({"w_ffn_out": d_w_ffn_out})
    dgate, dup = ffn_out_bwd_swiglu(dx2b, w_ffn_out, gate_ff, up_ff, name="d_act_swiglu")
    half = N_DEV // 2
    d_w_ffn_in, theirs = mm_tn(h2, dgate, name="d_w_ffn_in_gate", out_dtype=BF16, tm=L, tko=512, tno=1408,
                               out_blocks=half, total_blocks=N_DEV, bg=bg)
    bg = chip_exchange_background(chip_sums(g8_fo, theirs))
    d_w_ffn_in, got = mm_tn(h2, dup, name="d_w_ffn_in_up", out_dtype=BF16, tm=L, tko=512, tno=1408,
                            out_blocks=half, block_offset=half, total_blocks=N_DEV, into=d_w_ffn_in, bg=bg)
    others.update(zip(g8_fo, got))
    g8_fi, bg = blocks({"w_ffn_in": d_w_ffn_in})
    dh2, theirs = mm_nt(dgate, w_ffn_in_blk, a2=dup, name="d_h_ffn", out_dtype=F32, tm=1024, tko=1024, tc=1408, bg=bg)
    late_parts = chip_sums(g8_fi, theirs)
    dx1, dx1b, d_ffn_g = norm_bwd(dh2, x1, w["norm_ffn_g"], dx2, name="d_norm_ffn", want_bf16=True)
    dycat = mm_nt(dx1b, w_out, name="d_ycat", out_dtype=F32, tm=512, tko=1024, tc=2048)
    d_w_out = mm_tn(ycat, dx1b, name="d_w_out", out_dtype=BF16, tm=L, tko=1024, tno=1024)
    g8_out, bg = blocks({"w_out": d_w_out})
    dzu, dzv, d_sgu_w, d_sgu_b, d_ln_g, d_ln_b, d_g_b = sgu_bwd(
        z, dycat, w["sgu_ln_g"], w["sgu_ln_b"], w["sgu_w"][0], b_s3, w["out_norm_sgu_g"], name="d_sgu", d_sgu=d_sgu)
    dt_glu, dd_direct, d_g_a, d_glu_b = glu_post_bwd(
        y0, t_glu, w["ssm_glu_b"], w["out_norm_ssm_g"], dycat, name="d_s5_glu_norm")
    d_w_glu, theirs = mm_tn(ya1, dt_glu, name="d_w_glu", out_dtype=BF16, tm=L, tko=1024, tno=1024, bg=bg)
    late_parts += chip_sums(g8_out, theirs)
    g8_glu, bg = blocks({"ssm_glu_w": d_w_glu})
    dd_mm, theirs = mm_nt(dt_glu, w_glu, name="d_s5_glu_proj", out_dtype=F32, tm=512, tko=1024, tc=2048, bg=bg)
    late_parts += chip_sums(g8_glu, theirs)
    (du, d_bs_re, d_bs_im, d_cs_re, d_cs_im, d_a_re, d_a_im, d_d), got = ssm_bwd(
        z, y0, dd_direct, dd_mm, s_re, s_im, *s5_mats, name="d_s5_scan", bg=chip_exchange_background(late_parts))
    others.update(zip(("w_ffn_in", "w_out", "ssm_glu_w"), got))
    dz = jnp.concatenate([du, dzu, dzv], axis=1)

    from_bs = lambda t: _block_diag_part(t, SG).transpose(2, 0, 1, 3).reshape(H, G * P)
    from_cs = lambda t: _block_diag_part(t, SG).transpose(3, 0, 1, 2).reshape(H, G * P)
    d_q_re, d_q_im, d_bt_re, d_bt_im = disc_b_bwd(row(q_re), row(q_im), bt_re, bt_im, from_bs(d_bs_re),
                                                  from_bs(d_bs_im), name="d_s5_discretise_b")
    d_lam_re, d_lam_im, d_log_step = disc_lambda_bwd(
        lam_re, lam_im, log_step,
        (d_a_re.reshape(G, P), d_a_im.reshape(G, P), d_q_re.reshape(G, P), d_q_im.reshape(G, P)),
        name="d_s5_discretise_lambda")
    small_grads = {
        "ssm_lambda_re": d_lam_re, "ssm_lambda_im": d_lam_im, "ssm_log_step": d_log_step,
        "ssm_b_re": d_bt_re, "ssm_b_im": d_bt_im, "ssm_c_re": from_cs(d_cs_re), "ssm_c_im": from_cs(d_cs_im),
        "ssm_d": d_d, "ssm_glu_b": d_glu_b, "sgu_ln_g": d_ln_g, "sgu_ln_b": d_ln_b,
        "sgu_w": d_sgu_w, "sgu_b": d_sgu_b, "out_norm_ssm_g": d_g_a, "out_norm_sgu_g": d_g_b,
        "norm_ffn_g": d_ffn_g, "norm_ple_g": d_ple_g, "b_ple_gate": d_b_gate, "final_norm_g": d_final_g,
    }

    d_w_in, got = mm_tn(h1, dz, name="d_w_in", out_dtype=BF16, tm=L, tko=1024, tno=1024, out_blocks=N_DEV,
                        bg=gather_background([loss_part] + [small_grads[n] for n in SMALL]))
    sums = sum_gathered(got, name="sum_small_grads")
    g8_in, _ = blocks({"w_in": d_w_in})
    theirs = sibling_exchange(list(g8_in.values()), name="grads_to_sibling_w_in")
    dh1, got = mm_nt(dz, w_in, name="d_h_mix", out_dtype=F32, tm=512, tko=1024, tc=3 * d_sgu,
                     bg=chip_exchange_background(chip_sums(g8_in, theirs)))
    others.update(zip(g8_in, got))
    grad_x, d_mix_g = norm_bwd(dh1, x2d, w["norm_mix_g"], dx1, name="d_norm_mix", want_bf16=False)
    loss, small_sum = sums[0][0, 0], dict(zip(SMALL, sums[1:]))
    (small_sum["norm_mix_g"],) = sum_gathered(all_gather_blocks([d_mix_g], name="gather_last_grad"),
                                              name="sum_last_grad")

    out = {}
    for n in SHARDED:
        res = adamw_sharded(shard2d[n], m[n].reshape(shard2d[n].shape), v[n].reshape(shard2d[n].shape),
                            own[n][0], own[n][1], others[n], where, name="adamw_" + n)
        out[n] = [r.reshape(w[n].shape) for r in res]

    def work_shape(n):
        s = w[n].shape
        return (1,) + s if len(s) == 1 else (s if len(s) == 2 else s[1:])

    for n in ("ssm_b_re", "ssm_b_im"):
        small_sum[n] = small_sum[n].reshape(H, G, P).transpose(1, 2, 0)
    for n in ("ssm_c_re", "ssm_c_im"):
        small_sum[n] = small_sum[n].reshape(H, G, P).transpose(1, 0, 2)

    def replicated(names_, name):
        gs = [small_sum[n].reshape(work_shape(n)) for n in names_]
        res = adamw_replicated(*[[t[n].reshape(work_shape(n)) for n in names_] for t in (w, m, v)], gs, name=name)
        for i, n in enumerate(names_):
            out[n] = [r.reshape(w[n].shape) for r in (gs[i], res[0][i], res[1][i], res[2][i])]

    replicated([n for n in SMALL + SMALL_LAST if n not in SMALL_WIDE], "adamw_small")
    replicated(list(SMALL_WIDE), "adamw_s5_b_c")

    grads = [out[n][0] for n in WEIGHTS]
    deltas = [out[n][1] for n in WEIGHTS]
    new_m = [out[n][2] for n in WEIGHTS]
    new_v = [out[n][3] for n in WEIGHTS]
    return (loss, grad_x.reshape(x.shape), *grads, *deltas, *new_m, *new_v)


def kernel(x, p, norm_mix_g, w_in, ssm_lambda_re, ssm_lambda_im, ssm_log_step, ssm_b_re, ssm_b_im, ssm_c_re, ssm_c_im, ssm_d, ssm_glu_w, ssm_glu_b, sgu_ln_g, sgu_ln_b, sgu_w, sgu_b, out_norm_ssm_g, out_norm_sgu_g, w_out, norm_ffn_g, w_ffn_in, w_ffn_out, norm_ple_g, w_ple_gate, b_ple_gate, w_ple_proj, final_norm_g, loss_target, m_norm_mix_g, m_w_in, m_ssm_lambda_re, m_ssm_lambda_im, m_ssm_log_step, m_ssm_b_re, m_ssm_b_im, m_ssm_c_re, m_ssm_c_im, m_ssm_d, m_ssm_glu_w, m_ssm_glu_b, m_sgu_ln_g, m_sgu_ln_b, m_sgu_w, m_sgu_b, m_out_norm_ssm_g, m_out_norm_sgu_g, m_w_out, m_norm_ffn_g, m_w_ffn_in, m_w_ffn_out, m_norm_ple_g, m_w_ple_gate, m_b_ple_gate, m_w_ple_proj, m_final_norm_g, v_norm_mix_g, v_w_in, v_ssm_lambda_re, v_ssm_lambda_im, v_ssm_log_step, v_ssm_b_re, v_ssm_b_im, v_ssm_c_re, v_ssm_c_im, v_ssm_d, v_ssm_glu_w, v_ssm_glu_b, v_sgu_ln_g, v_sgu_ln_b, v_sgu_w, v_sgu_b, v_out_norm_ssm_g, v_out_norm_sgu_g, v_w_out, v_norm_ffn_g, v_w_ffn_in, v_w_ffn_out, v_norm_ple_g, v_w_ple_gate, v_b_ple_gate, v_w_ple_proj, v_final_norm_g):
    given = dict(locals())
    w = {n: given[n] for n in WEIGHTS}
    m = {n: given["m_" + n] for n in WEIGHTS}
    v = {n: given["v_" + n] for n in WEIGHTS}
    return _step(x, p, loss_target, w, m, v)
```

```python
import functools
import math

import jax
import jax.numpy as jnp
from jax import lax
from jax.experimental import pallas as pl
from jax.experimental.pallas import tpu as pltpu

F32 = jnp.float32
BF16 = jnp.bfloat16
MESH = pl.DeviceIdType.MESH
ANY = pl.BlockSpec(memory_space=pl.ANY)

N_DEV = 8
EPS = 1e-6
LAMBDA_RE_MAX = -1e-4
SSM_GROUP = 16
SSM_STATE = 64
SSM_SUPER = 16
SGU_CHUNK = 128
ADAM_LR, ADAM_B1, ADAM_B2, ADAM_EPS, ADAM_WD, ADAM_STEP = 0.001, 0.9, 0.999, 1e-08, 0.01, 10
VMEM_LIMIT = 52 * 1024 * 1024
LANE = 128
GATHER_FORWARD_AT = 0.85

_GELU_C = math.sqrt(2.0 / math.pi)


def _params(sem=None):
    return pltpu.CompilerParams(dimension_semantics=sem, vmem_limit_bytes=VMEM_LIMIT)


def _tile(dim, pref, unit=LANE):
    if dim <= pref:
        return dim
    t = (pref // unit) * unit
    while t >= unit:
        if dim % t == 0:
            return t
        t -= unit
    return dim


def _gelu(x):
    return 0.5 * x * (1.0 + jnp.tanh(_GELU_C * (x + 0.044715 * x * x * x)))


def _gelu_grad(x):
    t = jnp.tanh(_GELU_C * (x + 0.044715 * x * x * x))
    return 0.5 * (1.0 + t) + 0.5 * x * (1.0 - t * t) * (_GELU_C * (1.0 + 3.0 * 0.044715 * x * x))


def _gelu_and_grad(x):
    t = jnp.tanh(_GELU_C * (x + 0.044715 * x * x * x))
    return (0.5 * x * (1.0 + t),
            0.5 * (1.0 + t) + 0.5 * x * (1.0 - t * t) * (_GELU_C * (1.0 + 3.0 * 0.044715 * x * x)))


def _rms(x):
    return lax.rsqrt(jnp.mean(x * x, axis=-1, keepdims=True) + EPS)


def _rmsnorm_bwd(dy, x, r, g):
    dyg = dy * g
    return r * dyg - x * (r * r * r) * jnp.mean(dyg * x, axis=-1, keepdims=True)


def _rowsum(v):
    return jnp.sum(v, axis=0, keepdims=True)


class Background:
    def __init__(self, inputs, out_shapes, scratch, phases):
        self.inputs, self.out_shapes, self.scratch, self.phases = list(inputs), list(out_shapes), list(scratch), phases
        self.aliases = {}

    def emit(self, step, nsteps, ins, outs, scratch):
        for place, phase in self.phases:
            at = min(int(place * nsteps), nsteps - 1)

            @pl.when(step == at)
            def _():
                phase(ins, outs, scratch)


def _carrier(bg, n_in, n_out, n_scratch, grid):
    nbi = len(bg.inputs) if bg else 0
    nbo = len(bg.out_shapes) if bg else 0
    nsteps = math.prod(grid)

    def split(refs):
        ins = refs[:n_in]
        bg_ins = refs[n_in:n_in + nbi]
        outs = refs[n_in + nbi:n_in + nbi + n_out]
        bg_outs = refs[n_in + nbi + n_out:n_in + nbi + n_out + nbo]
        rest = refs[n_in + nbi + n_out + nbo:]
        scratch, bg_scratch = rest[:n_scratch], rest[n_scratch:]

        def run_background():
            if bg is None:
                return
            step = pl.program_id(0)
            for axis in range(1, len(grid)):
                step = step * grid[axis] + pl.program_id(axis)
            bg.emit(step, nsteps, bg_ins, bg_outs, bg_scratch)

        return ins, outs, scratch, run_background

    if bg is None:
        return [], [], [], [], [], split
    return [ANY] * nbi, list(bg.inputs), [ANY] * nbo, list(bg.out_shapes), list(bg.scratch), split


def _semantics(bg, sem):
    return tuple("arbitrary" for _ in sem) if bg is not None else sem


def _results(res, n_out, bg):
    res = list(res) if isinstance(res, (list, tuple)) else [res]
    own = res[0] if n_out == 1 else res[:n_out]
    return (own, res[n_out:]) if bg is not None else own


def mm_nn(a, b, *, name, out_dtype, tm, tn, tk, residual=None, bg=None):
    M, K = a.shape
    blocked = b.ndim == 3
    if blocked:
        nb, _, Nb = b.shape
        N = nb * Nb
        tn = _tile(Nb, tn)
        per = Nb // tn
    else:
        N = b.shape[1]
        tn = _tile(N, tn)
    tm, tk = _tile(M, tm, 8), _tile(K, tk)
    nj, ni, nk = N // tn, M // tm, K // tk
    has_res = residual is not None
    grid = (nj, ni, nk)
    bg_in_specs, bg_args, bg_out_specs, bg_out_shapes, bg_scratch, split = _carrier(
        bg, 3 if has_res else 2, 1, 0 if nk == 1 else 1, grid)

    def body(*refs):
        ins, (o_ref,), scratch, run_background = split(refs)
        run_background()
        a_ref, b_ref = ins[0], ins[1]
        r_ref = ins[2] if has_res else None

        def finish(acc):
            if has_res:
                acc = acc + r_ref[...]
            o_ref[...] = acc.astype(o_ref.dtype)

        part = jnp.dot(a_ref[...], b_ref[...], preferred_element_type=F32)
        if nk == 1:
            finish(part)
        else:
            acc_ref = scratch[0]
            k = pl.program_id(2)

            @pl.when(k == 0)
            def _():
                acc_ref[...] = part

            @pl.when(k > 0)
            def _():
                acc_ref[...] += part

            @pl.when(k == nk - 1)
            def _():
                finish(acc_ref[...])

    if blocked:
        b_spec = pl.BlockSpec((None, tk, tn), lambda j, i, k: (j // per, k, j % per))
    else:
        b_spec = pl.BlockSpec((tk, tn), lambda j, i, k: (k, j))
    in_specs = [pl.BlockSpec((tm, tk), lambda j, i, k: (i, k)), b_spec]
    args = [a, b]
    if has_res:
        in_specs.append(pl.BlockSpec((tm, tn), lambda j, i, k: (i, j)))
        args.append(residual)
    res = pl.pallas_call(
        body, name=name, grid=grid,
        in_specs=in_specs + bg_in_specs,
        out_specs=[pl.BlockSpec((tm, tn), lambda j, i, k: (i, j))] + bg_out_specs,
        out_shape=[jax.ShapeDtypeStruct((M, N), out_dtype)] + bg_out_shapes,
        input_output_aliases={len(args) + k: 1 + o for k, o in (bg.aliases if bg else {}).items()},
        scratch_shapes=([] if nk == 1 else [pltpu.VMEM((tm, tn), F32)]) + bg_scratch,
        compiler_params=_params(_semantics(bg, ("parallel", "parallel", "arbitrary"))),
    )(*args, *bg_args)
    return _results(res, 1, bg)


def mm_nt(a, w, *, name, out_dtype, tm, tko, tc, a2=None, bg=None):
    M, N = a.shape
    if a2 is not None:
        N = 2 * N
    blocked = w.ndim == 3
    if blocked:
        nb, Ko, Nb = w.shape
        tc = _tile(Nb, tc)
        per = Nb // tc
    else:
        Ko = w.shape[0]
        tc = _tile(N, tc)
    tm, tko = _tile(M, tm, 8), _tile(Ko, tko)
    njo, ni, nc = Ko // tko, M // tm, N // tc
    grid = (njo, ni, nc)
    half = nc // 2
    bg_in_specs, bg_args, bg_out_specs, bg_out_shapes, bg_scratch, split = _carrier(
        bg, 2 if a2 is None else 3, 1, 0 if nc == 1 else 1, grid)

    def body(*refs):
        ins, (o_ref,), scratch, run_background = split(refs)
        run_background()
        a_val = ins[0][...]
        if a2 is not None:
            a_val = jnp.where(pl.program_id(2) < half, a_val, ins[1][...])
        part = lax.dot_general(a_val, ins[-1][...], (((1,), (1,)), ((), ())),
                               preferred_element_type=F32)
        if nc == 1:
            o_ref[...] = part.astype(o_ref.dtype)
        else:
            acc_ref = scratch[0]
            c = pl.program_id(2)

            @pl.when(c == 0)
            def _():
                acc_ref[...] = part

            @pl.when(c > 0)
            def _():
                acc_ref[...] += part

            @pl.when(c == nc - 1)
            def _():
                o_ref[...] = acc_ref[...].astype(o_ref.dtype)

    if blocked:
        w_spec = pl.BlockSpec((None, tko, tc), lambda j, i, c: (c // per, j, c % per))
    else:
        w_spec = pl.BlockSpec((tko, tc), lambda j, i, c: (j, c))
    if a2 is None:
        a_specs, a_args = [pl.BlockSpec((tm, tc), lambda j, i, c: (i, c))], [a]
    else:
        a_specs = [pl.BlockSpec((tm, tc), lambda j, i, c: (i, jnp.minimum(c, half - 1))),
                   pl.BlockSpec((tm, tc), lambda j, i, c: (i, jnp.maximum(c - half, 0)))]
        a_args = [a, a2]
    res = pl.pallas_call(
        body, name=name, grid=grid,
        in_specs=a_specs + [w_spec] + bg_in_specs,
        out_specs=[pl.BlockSpec((tm, tko), lambda j, i, c: (i, j))] + bg_out_specs,
        out_shape=[jax.ShapeDtypeStruct((M, Ko), out_dtype)] + bg_out_shapes,
        scratch_shapes=([] if nc == 1 else [pltpu.VMEM((tm, tko), F32)]) + bg_scratch,
        compiler_params=_params(_semantics(bg, ("parallel", "parallel", "arbitrary"))),
    )(*a_args, w, *bg_args)
    return _results(res, 1, bg)


def mm_tn(a, g, *, name, out_dtype, tm, tko, tno, out_blocks=None, block_offset=0, total_blocks=None, into=None,
          bg=None):
    M, K = a.shape
    N = g.shape[1]
    if out_blocks:
        Nb = N // out_blocks
        tno = _tile(Nb, tno)
        per = Nb // tno
    else:
        tno = _tile(N, tno)
    tm, tko = _tile(M, tm), _tile(K, tko)
    njo, njn, nm = K // tko, N // tno, M // tm
    grid = (njo, njn, nm)
    bg_in_specs, bg_args, bg_out_specs, bg_out_shapes, bg_scratch, split = _carrier(
        bg, 2 if into is None else 3, 1, 0 if nm == 1 else 1, grid)

    def body(*refs):
        ins, (o_ref,), scratch, run_background = split(refs)
        a_ref, g_ref = ins[0], ins[1]
        run_background()
        part = lax.dot_general(a_ref[...], g_ref[...], (((0,), (0,)), ((), ())),
                               preferred_element_type=F32)
        if nm == 1:
            o_ref[...] = part.astype(o_ref.dtype)
        else:
            acc_ref = scratch[0]
            m = pl.program_id(2)

            @pl.when(m == 0)
            def _():
                acc_ref[...] = part

            @pl.when(m > 0)
            def _():
                acc_ref[...] += part

            @pl.when(m == nm - 1)
            def _():
                o_ref[...] = acc_ref[...].astype(o_ref.dtype)

    if out_blocks:
        o_spec = pl.BlockSpec((None, tko, tno), lambda jo, jn, m: (jn // per + block_offset, jo, jn % per))
        o_shape = jax.ShapeDtypeStruct((total_blocks or out_blocks, K, Nb), out_dtype)
    else:
        o_spec = pl.BlockSpec((tko, tno), lambda jo, jn, m: (jo, jn))
        o_shape = jax.ShapeDtypeStruct((K, N), out_dtype)
    res = pl.pallas_call(
        body, name=name, grid=grid,
        in_specs=[pl.BlockSpec((tm, tko), lambda jo, jn, m: (m, jo)),
                  pl.BlockSpec((tm, tno), lambda jo, jn, m: (m, jn))] + ([] if into is None else [ANY]) + bg_in_specs,
        out_specs=[o_spec] + bg_out_specs, out_shape=[o_shape] + bg_out_shapes,
        scratch_shapes=([] if nm == 1 else [pltpu.VMEM((tko, tno), F32)]) + bg_scratch,
        input_output_aliases={} if into is None else {2: 0},
        compiler_params=_params(_semantics(bg, ("parallel", "parallel", "arbitrary"))),
    )(a, g, *([] if into is None else [into]), *bg_args)
    return _results(res, 1, bg)


def ffn_in_swiglu(h, w_blk, *, name, tm, bg=None):
    M, K = h.shape
    nb, _, Nb = w_blk.shape
    nh = nb // 2
    F = nh * Nb
    tm = _tile(M, tm, 8)
    grid = (nh, M // tm)
    bg_in_specs, bg_args, bg_out_specs, bg_out_shapes, bg_scratch, split = _carrier(bg, 3, 3, 0, grid)

    def body(*refs):
        (h_ref, wg_ref, wu_ref), (act_ref, gate_ref, up_ref), _, run_background = split(refs)
        run_background()
        hv = h_ref[...]
        gate = jnp.dot(hv, wg_ref[...], preferred_element_type=F32)
        up = jnp.dot(hv, wu_ref[...], preferred_element_type=F32)
        gate_ref[...] = gate
        up_ref[...] = up
        act_ref[...] = (gate * jax.nn.sigmoid(gate) * up).astype(act_ref.dtype)

    o_spec = pl.BlockSpec((tm, Nb), lambda j, i: (i, j))
    res = pl.pallas_call(
        body, name=name, grid=grid,
        in_specs=[pl.BlockSpec((tm, K), lambda j, i: (i, 0)),
                  pl.BlockSpec((None, K, Nb), lambda j, i: (j, 0, 0)),
                  pl.BlockSpec((None, K, Nb), lambda j, i: (j + nh, 0, 0))] + bg_in_specs,
        out_specs=[o_spec, o_spec, o_spec] + bg_out_specs,
        out_shape=[jax.ShapeDtypeStruct((M, F), BF16), jax.ShapeDtypeStruct((M, F), F32),
                   jax.ShapeDtypeStruct((M, F), F32)] + bg_out_shapes,
        scratch_shapes=bg_scratch,
        compiler_params=_params(_semantics(bg, ("parallel", "parallel"))),
    )(h, w_blk, w_blk, *bg_args)
    return _results(res, 3, bg)


def _row_spec(tm, d, col=0):
    return pl.BlockSpec((tm, d), lambda i: (i, col))


def _vec_spec(d):
    return pl.BlockSpec((1, d), lambda i: (0, 0))


def norm_fwd(x, g, *, name, tm=256):
    L, D = x.shape
    tm = _tile(L, tm, 8)

    def body(x_ref, g_ref, h_ref):
        xv = x_ref[...]
        h_ref[...] = (xv * _rms(xv) * g_ref[...]).astype(h_ref.dtype)

    return pl.pallas_call(
        body, name=name, grid=(L // tm,),
        in_specs=[_row_spec(tm, D), _vec_spec(D)],
        out_specs=_row_spec(tm, D),
        out_shape=jax.ShapeDtypeStruct((L, D), BF16),
        compiler_params=_params(("parallel",)),
    )(x, g)


def norm_bwd(dh, xin, g, dres, *, name, want_bf16, tm=128):
    L, D = xin.shape
    tm = _tile(L, tm, 8)

    def body(dh_ref, x_ref, g_ref, dres_ref, dx_ref, *rest):
        dg_ref = rest[-1]
        xv, dhv = x_ref[...], dh_ref[...]
        r = _rms(xv)
        dx = dres_ref[...] + _rmsnorm_bwd(dhv, xv, r, g_ref[...])
        dx_ref[...] = dx
        if want_bf16:
            rest[0][...] = dx.astype(BF16)
        part = _rowsum(dhv * xv * r)

        @pl.when(pl.program_id(0) == 0)
        def _():
            dg_ref[...] = part

        @pl.when(pl.program_id(0) > 0)
        def _():
            dg_ref[...] += part

    out_specs = [_row_spec(tm, D)] + ([_row_spec(tm, D)] if want_bf16 else []) + [_vec_spec(D)]
    out_shape = ([jax.ShapeDtypeStruct((L, D), F32)]
                 + ([jax.ShapeDtypeStruct((L, D), BF16)] if want_bf16 else [])
                 + [jax.ShapeDtypeStruct((1, D), F32)])
    return pl.pallas_call(
        body, name=name, grid=(L // tm,),
        in_specs=[_row_spec(tm, D), _row_spec(tm, D), _vec_spec(D), _row_spec(tm, D)],
        out_specs=out_specs, out_shape=out_shape,
        compiler_params=_params(("arbitrary",)),
    )(dh, xin, g, dres)


def glu_pre(y0, *, name, tm=256):
    L, D = y0.shape
    tm = _tile(L, tm, 8)

    def body(y_ref, o_ref):
        o_ref[...] = _gelu(y_ref[...]).astype(o_ref.dtype)

    return pl.pallas_call(
        body, name=name, grid=(L // tm,),
        in_specs=[_row_spec(tm, D)], out_specs=_row_spec(tm, D),
        out_shape=jax.ShapeDtypeStruct((L, D), BF16),
        compiler_params=_params(("parallel",)),
    )(y0)


def glu_post(y0, t, b_glu, g_a, *, name, tm=256):
    L, D = y0.shape
    tm = _tile(L, tm, 8)

    def body(y_ref, t_ref, b_ref, g_ref, o_ref):
        ya = _gelu(y_ref[...]) * jax.nn.sigmoid(t_ref[...] + b_ref[...])
        o_ref[...] = (ya * _rms(ya) * g_ref[...]).astype(o_ref.dtype)

    return pl.pallas_call(
        body, name=name, grid=(L // tm,),
        in_specs=[_row_spec(tm, D), _row_spec(tm, D), _vec_spec(D), _vec_spec(D)],
        out_specs=_row_spec(tm, D),
        out_shape=jax.ShapeDtypeStruct((L, D), BF16),
        compiler_params=_params(("parallel",)),
    )(y0, t, b_glu, g_a)


def glu_post_bwd(y0, t, b_glu, g_a, dycat, *, name, tm=128):
    L, D = y0.shape
    tm = _tile(L, tm, 8)

    def body(y_ref, t_ref, b_ref, g_ref, dn_ref, dt_ref, dd_ref, dga_ref, dbg_ref):
        ya1 = _gelu(y_ref[...])
        sg = jax.nn.sigmoid(t_ref[...] + b_ref[...])
        ya = ya1 * sg
        ra = _rms(ya)
        dn = dn_ref[...]
        dya = _rmsnorm_bwd(dn, ya, ra, g_ref[...])
        dt = dya * ya1 * sg * (1.0 - sg)
        dt_ref[...] = dt.astype(BF16)
        dd_ref[...] = dya * sg
        p_ga, p_bg = _rowsum(dn * ya * ra), _rowsum(dt)

        @pl.when(pl.program_id(0) == 0)
        def _():
            dga_ref[...] = p_ga
            dbg_ref[...] = p_bg

        @pl.when(pl.program_id(0) > 0)
        def _():
            dga_ref[...] += p_ga
            dbg_ref[...] += p_bg

    return pl.pallas_call(
        body, name=name, grid=(L // tm,),
        in_specs=[_row_spec(tm, D), _row_spec(tm, D), _vec_spec(D), _vec_spec(D), _row_spec(tm, D, 0)],
        out_specs=[_row_spec(tm, D), _row_spec(tm, D), _vec_spec(D), _vec_spec(D)],
        out_shape=[jax.ShapeDtypeStruct((L, D), BF16), jax.ShapeDtypeStruct((L, D), F32),
                   jax.ShapeDtypeStruct((1, D), F32), jax.ShapeDtypeStruct((1, D), F32)],
        compiler_params=_params(("arbitrary",)),
    )(y0, t, b_glu, g_a, dycat)


def head_and_loss(x2, gpre, b_g, pp, g_f, tgt, *, name, tm=128):
    L, D = x2.shape
    tm = _tile(L, tm, 8)

    def body(x2_ref, gp_ref, bg_ref, pp_ref, gf_ref, tg_ref,
             dx3_ref, dpre_ref, dpp_ref, loss_ref, dgf_ref, dbg_ref):
        gate = jax.nn.sigmoid(gp_ref[...] + bg_ref[...])
        ppv = pp_ref[...]
        x3 = x2_ref[...] + gate * ppv
        r = _rms(x3)
        xn = x3 * r
        gf = gf_ref[...]
        err = xn * gf - tg_ref[...]
        loss = 0.5 * jnp.sum(jnp.mean(err * err, axis=-1, keepdims=True), axis=0, keepdims=True)
        dout = err * (1.0 / D)
        dx3 = _rmsnorm_bwd(dout, x3, r, gf)
        dx3_ref[...] = dx3
        dpre = dx3 * ppv * gate * (1.0 - gate)
        dpre_ref[...] = dpre.astype(BF16)
        dpp_ref[...] = (dx3 * gate).astype(BF16)
        p_gf, p_bg = _rowsum(dout * xn), _rowsum(dpre)
        p_loss = jnp.broadcast_to(loss, loss_ref.shape)

        @pl.when(pl.program_id(0) == 0)
        def _():
            loss_ref[...] = p_loss
            dgf_ref[...] = p_gf
            dbg_ref[...] = p_bg

        @pl.when(pl.program_id(0) > 0)
        def _():
            loss_ref[...] += p_loss
            dgf_ref[...] += p_gf
            dbg_ref[...] += p_bg

    rs = _row_spec(tm, D)
    return pl.pallas_call(
        body, name=name, grid=(L // tm,),
        in_specs=[rs, rs, _vec_spec(D), rs, _vec_spec(D), rs],
        out_specs=[rs, rs, rs, pl.BlockSpec((8, LANE), lambda i: (0, 0)), _vec_spec(D), _vec_spec(D)],
        out_shape=[jax.ShapeDtypeStruct((L, D), F32), jax.ShapeDtypeStruct((L, D), BF16),
                   jax.ShapeDtypeStruct((L, D), BF16), jax.ShapeDtypeStruct((8, LANE), F32),
                   jax.ShapeDtypeStruct((1, D), F32), jax.ShapeDtypeStruct((1, D), F32)],
        compiler_params=_params(("arbitrary",)),
    )(x2, gpre, b_g, pp, g_f, tgt)


def ffn_out_bwd_swiglu(dx, w, gate, up, *, name, tm=512, tf=1408):
    M, D = dx.shape
    F = w.shape[0]
    tm, tf = _tile(M, tm, 8), _tile(F, tf)

    def body(dx_ref, w_ref, g_ref, u_ref, dg_ref, du_ref):
        da = lax.dot_general(dx_ref[...], w_ref[...], (((1,), (1,)), ((), ())), preferred_element_type=F32)
        gv = g_ref[...]
        sg = jax.nn.sigmoid(gv)
        dg_ref[...] = (da * u_ref[...] * sg * (1.0 + gv * (1.0 - sg))).astype(BF16)
        du_ref[...] = (da * gv * sg).astype(BF16)

    spec = pl.BlockSpec((tm, tf), lambda j, i: (i, j))
    return pl.pallas_call(
        body, name=name, grid=(F // tf, M // tm),
        in_specs=[pl.BlockSpec((tm, D), lambda j, i: (i, 0)), pl.BlockSpec((tf, D), lambda j, i: (j, 0)), spec, spec],
        out_specs=[spec, spec],
        out_shape=[jax.ShapeDtypeStruct((M, F), BF16), jax.ShapeDtypeStruct((M, F), BF16)],
        compiler_params=_params(("parallel", "parallel")),
    )(dx, w, gate, up)


def _sgu_forward_values(u1, v1, lng, lnb, w_ref, bs_ref, s_scr, heads, hd):
    xc = v1 - jnp.mean(v1, axis=-1, keepdims=True)
    r = lax.rsqrt(jnp.mean(xc * xc, axis=-1, keepdims=True) + EPS)
    xhat = xc * r
    v2 = xhat * lng + lnb
    tril = (lax.broadcasted_iota(jnp.int32, (SGU_CHUNK, SGU_CHUNK), 0)
            >= lax.broadcasted_iota(jnp.int32, (SGU_CHUNK, SGU_CHUNK), 1))
    for h in range(heads):
        wm = jnp.where(tril, w_ref[h], 0.0).astype(BF16)
        cols = slice(h * hd, (h + 1) * hd)
        s_scr[:, cols] = jnp.dot(wm, v2[:, cols].astype(BF16), preferred_element_type=F32) + bs_ref[h]
    return xhat, r, v2, tril


def sgu_fwd(z, ln_g, ln_b, w_s, b_s, g_b, *, name, d_sgu):
    L = z.shape[0]
    heads = w_s.shape[0]
    hd = d_sgu // heads

    def body(zu_ref, zv_ref, lng_ref, lnb_ref, w_ref, bs_ref, gb_ref, o_ref, s_scr):
        u1 = _gelu(zu_ref[...])
        _sgu_forward_values(u1, _gelu(zv_ref[...]), lng_ref[...], lnb_ref[...], w_ref, bs_ref, s_scr, heads, hd)
        yb = u1 * s_scr[...]
        o_ref[...] = (yb * _rms(yb) * gb_ref[...]).astype(o_ref.dtype)

    blk = lambda col: pl.BlockSpec((SGU_CHUNK, d_sgu), lambda n: (n, col))
    return pl.pallas_call(
        body, name=name, grid=(L // SGU_CHUNK,),
        in_specs=[blk(1), blk(2), _vec_spec(d_sgu), _vec_spec(d_sgu),
                  pl.BlockSpec(w_s.shape, lambda n: (0, 0, 0)), pl.BlockSpec(b_s.shape, lambda n: (0, 0, 0)),
                  _vec_spec(d_sgu)],
        out_specs=blk(0),
        out_shape=jax.ShapeDtypeStruct((L, d_sgu), BF16),
        scratch_shapes=[pltpu.VMEM((SGU_CHUNK, d_sgu), F32)],
        compiler_params=_params(("parallel",)),
    )(z, z, ln_g, ln_b, w_s, b_s, g_b)


def sgu_bwd(z, dycat, ln_g, ln_b, w_s, b_s, g_b, *, name, d_sgu):
    L = z.shape[0]
    heads = w_s.shape[0]
    hd = d_sgu // heads

    def body(zu_ref, zv_ref, dn_ref, lng_ref, lnb_ref, w_ref, bs_ref, gb_ref,
             dzu_ref, dzv_ref, dw_ref, dbs_ref, dlng_ref, dlnb_ref, dgb_ref, s_scr, dv_scr):
        first = pl.program_id(0) == 0
        lng = lng_ref[...]
        u1, du1 = _gelu_and_grad(zu_ref[...])
        v1, dv1_dz = _gelu_and_grad(zv_ref[...])
        xhat, r, v2, tril = _sgu_forward_values(u1, v1, lng, lnb_ref[...], w_ref, bs_ref, s_scr, heads, hd)
        s = s_scr[...]
        yb = u1 * s
        rb = _rms(yb)
        dn = dn_ref[...]
        dyb = _rmsnorm_bwd(dn, yb, rb, gb_ref[...])
        dzu_ref[...] = (dyb * s * du1).astype(BF16)
        ds = dyb * u1
        for h in range(heads):
            cols = slice(h * hd, (h + 1) * hd)
            ds_h = ds[:, cols]
            ds_hb = ds_h.astype(BF16)
            wm = jnp.where(tril, w_ref[h], 0.0).astype(BF16)
            dw_h = jnp.where(tril, lax.dot_general(ds_hb, v2[:, cols].astype(BF16), (((1,), (1,)), ((), ())),
                                                   preferred_element_type=F32), 0.0)
            db_h = jnp.sum(ds_h.T, axis=0, keepdims=True)
            dv_scr[:, cols] = lax.dot_general(wm, ds_hb, (((0,), (0,)), ((), ())), preferred_element_type=F32)

            @pl.when(first)
            def _():
                dw_ref[h] = dw_h
                dbs_ref[h] = db_h

            @pl.when(jnp.logical_not(first))
            def _():
                dw_ref[h] += dw_h
                dbs_ref[h] += db_h

        dv2 = dv_scr[...]
        dxh = dv2 * lng
        dv1 = r * (dxh - jnp.mean(dxh, axis=-1, keepdims=True)
                   - xhat * jnp.mean(dxh * xhat, axis=-1, keepdims=True))
        dzv_ref[...] = (dv1 * dv1_dz).astype(BF16)
        p_lng, p_lnb, p_gb = _rowsum(dv2 * xhat), _rowsum(dv2), _rowsum(dn * yb * rb)

        @pl.when(first)
        def _():
            dlng_ref[...] = p_lng
            dlnb_ref[...] = p_lnb
            dgb_ref[...] = p_gb

        @pl.when(jnp.logical_not(first))
        def _():
            dlng_ref[...] += p_lng
            dlnb_ref[...] += p_lnb
            dgb_ref[...] += p_gb

    blk = lambda col: pl.BlockSpec((SGU_CHUNK, d_sgu), lambda n: (n, col))
    full3 = lambda shape: pl.BlockSpec(shape, lambda n: (0, 0, 0))
    return pl.pallas_call(
        body, name=name, grid=(L // SGU_CHUNK,),
        in_specs=[blk(1), blk(2), blk(1), _vec_spec(d_sgu), _vec_spec(d_sgu),
                  full3(w_s.shape), full3(b_s.shape), _vec_spec(d_sgu)],
        out_specs=[blk(0), blk(0), full3(w_s.shape), full3((heads, 1, SGU_CHUNK)),
                   _vec_spec(d_sgu), _vec_spec(d_sgu), _vec_spec(d_sgu)],
        out_shape=[jax.ShapeDtypeStruct((L, d_sgu), BF16), jax.ShapeDtypeStruct((L, d_sgu), BF16),
                   jax.ShapeDtypeStruct(w_s.shape, F32), jax.ShapeDtypeStruct((heads, 1, SGU_CHUNK), F32),
                   jax.ShapeDtypeStruct((1, d_sgu), F32), jax.ShapeDtypeStruct((1, d_sgu), F32),
                   jax.ShapeDtypeStruct((1, d_sgu), F32)],
        scratch_shapes=[pltpu.VMEM((SGU_CHUNK, d_sgu), F32), pltpu.VMEM((SGU_CHUNK, d_sgu), F32)],
        compiler_params=_params(("arbitrary",)),
    )(z, z, dycat, ln_g, ln_b, w_s, b_s, g_b)


def _disc_lambda(lam_re, lam_im, log_step):
    lr = jnp.minimum(lam_re, LAMBDA_RE_MAX)
    li = lam_im
    dt = jnp.exp(log_step)
    mag = jnp.exp(lr * dt)
    ang = li * dt
    a_re = mag * jnp.cos(ang)
    a_im = mag * jnp.sin(ang)
    nr = a_re - 1.0
    ni = a_im
    den = lr * lr + li * li
    return a_re, a_im, (nr * lr + ni * li) / den, (ni * lr - nr * li) / den


def _disc_b(q_re, q_im, b_re, b_im):
    return q_re * b_re - q_im * b_im, q_re * b_im + q_im * b_re


def disc_lambda_fwd(lam_re, lam_im, log_step, *, name):
    def body(lr_ref, li_ref, ls_ref, ar_ref, ai_ref, qr_ref, qi_ref):
        ar_ref[...], ai_ref[...], qr_ref[...], qi_ref[...] = _disc_lambda(lr_ref[...], li_ref[...], ls_ref[...])

    sd = jax.ShapeDtypeStruct(lam_re.shape, F32)
    return pl.pallas_call(body, name=name, out_shape=[sd, sd, sd, sd], compiler_params=_params())(
        lam_re, lam_im, log_step)


def disc_lambda_bwd(lam_re, lam_im, log_step, cts, *, name):
    def body(lr_ref, li_ref, ls_ref, c0, c1, c2, c3, dlr_ref, dli_ref, dls_ref):
        _, vjp = jax.vjp(_disc_lambda, lr_ref[...], li_ref[...], ls_ref[...])
        dlr_ref[...], dli_ref[...], dls_ref[...] = vjp((c0[...], c1[...], c2[...], c3[...]))

    sd = jax.ShapeDtypeStruct(lam_re.shape, F32)
    return pl.pallas_call(body, name=name, out_shape=[sd, sd, jax.ShapeDtypeStruct(log_step.shape, F32)],
                          compiler_params=_params())(lam_re, lam_im, log_step, *cts)


def disc_b_fwd(q_re, q_im, b_re, b_im, *, name):
    def body(qr_ref, qi_ref, br_ref, bi_ref, or_ref, oi_ref):
        or_ref[...], oi_ref[...] = _disc_b(qr_ref[...], qi_ref[...], br_ref[...], bi_ref[...])

    sd = jax.ShapeDtypeStruct(b_re.shape, F32)
    return pl.pallas_call(body, name=name, out_shape=[sd, sd], compiler_params=_params())(q_re, q_im, b_re, b_im)


def disc_b_bwd(q_re, q_im, b_re, b_im, ct_re, ct_im, *, name):
    def body(qr_ref, qi_ref, br_ref, bi_ref, cr_ref, ci_ref, dqr_ref, dqi_ref, dbr_ref, dbi_ref):
        _, vjp = jax.vjp(_disc_b, qr_ref[...], qi_ref[...], br_ref[...], bi_ref[...])
        dqr_ref[...], dqi_ref[...], dbr_ref[...], dbi_ref[...] = vjp((cr_ref[...], ci_ref[...]))

    sq, sb = jax.ShapeDtypeStruct(q_re.shape, F32), jax.ShapeDtypeStruct(b_re.shape, F32)
    return pl.pallas_call(body, name=name, out_shape=[sq, sq, sb, sb], compiler_params=_params())(
        q_re, q_im, b_re, b_im, ct_re, ct_im)


def _lti_scan(xr, xi, ar, ai, reverse):
    T = xr.shape[0]
    row = lax.broadcasted_iota(jnp.int32, xr.shape, 0)
    k = 1
    while k < T:
        shift = T - k if reverse else k
        keep = (row < T - k) if reverse else (row >= k)
        sr = jnp.where(keep, pltpu.roll(xr, shift, 0), 0.0)
        si = jnp.where(keep, pltpu.roll(xi, shift, 0), 0.0)
        xr, xi = xr + ar * sr - ai * si, xi + ar * si + ai * sr
        ar, ai = ar * ar - ai * ai, 2.0 * ar * ai
        k *= 2
    return xr, xi


SUBLANES = 8


def _scan_rows(x_re, x_im, o_re, o_im, ar, ai, cr, ci, reverse):
    T, n = x_re.shape
    groups = T // SUBLANES
    row = lax.broadcasted_iota(jnp.int32, (SUBLANES, n), 0)
    edge = SUBLANES - 1 if reverse else 0
    pr, pi = _lti_scan(jnp.where(row == edge, ar, 0.0), jnp.where(row == edge, ai, 0.0), ar, ai, reverse)
    pows = []
    for level in range(3):
        k = 1 << level
        keep = (row < SUBLANES - k) if reverse else (row >= k)
        pows.append((jnp.where(keep, ar, 0.0), jnp.where(keep, ai, 0.0)))
        ar, ai = ar * ar - ai * ai, 2.0 * ar * ai

    def group(i, carry):
        cr, ci = carry
        at = pl.multiple_of((groups - 1 - i if reverse else i) * SUBLANES, SUBLANES)
        xr, xi = x_re[pl.ds(at, SUBLANES), :], x_im[pl.ds(at, SUBLANES), :]
        for level, (qr, qi) in enumerate(pows):
            shift = SUBLANES - (1 << level) if reverse else 1 << level
            sr, si = pltpu.roll(xr, shift, 0), pltpu.roll(xi, shift, 0)
            xr, xi = xr + qr * sr - qi * si, xi + qr * si + qi * sr
        xr, xi = xr + pr * cr - pi * ci, xi + pr * ci + pi * cr
        o_re[pl.ds(at, SUBLANES), :] = xr
        o_im[pl.ds(at, SUBLANES), :] = xi
        return spread(xr[last:last + 1, :]), spread(xi[last:last + 1, :])

    last = 0 if reverse else SUBLANES - 1
    spread = lambda v: jnp.broadcast_to(v, (SUBLANES, n))
    cr, ci = lax.fori_loop(0, groups, group, (spread(cr), spread(ci)), unroll=2)
    return cr[0:1, :], ci[0:1, :]


def _ssm_chunk(L):
    return _tile(L, 512, 8)


def _same_group(rows, cols):
    r = lax.broadcasted_iota(jnp.int32, (rows, cols), 0) // SSM_GROUP
    c = lax.broadcasted_iota(jnp.int32, (rows, cols), 1) // SSM_STATE
    return r == c


def _expand_groups(compact):
    H, S = compact.shape
    tiled = jnp.concatenate([compact] * (S // SSM_STATE), axis=0)
    return jnp.where(_same_group(tiled.shape[0], S), tiled, 0.0).astype(BF16)


def _collapse_groups(dense):
    C, S = dense.shape
    masked = jnp.where(_same_group(C, S), dense, 0.0)
    total = masked[0:SSM_GROUP]
    for g in range(1, C // SSM_GROUP):
        total = total + masked[g * SSM_GROUP:(g + 1) * SSM_GROUP]
    return total


def ssm_fwd(z, bt_re, bt_im, ct_re, ct_im, a_re, a_im, d, *, name, bg=None):
    L = z.shape[0]
    NK, _, S = a_re.shape
    H = bt_re.shape[0]
    C = S // SSM_STATE * SSM_GROUP
    T = _ssm_chunk(L)
    grid = (NK, L // T)
    bg_in_specs, bg_args, bg_out_specs, bg_out_shapes, bg_scratch, split = _carrier(bg, 8, 3, 8, grid)
    nt_dot = lambda p, q: lax.dot_general(p, q, (((1,), (1,)), ((), ())), preferred_element_type=F32)

    def body(*refs):
        ((u_ref, btr_ref, bti_ref, ctr_ref, cti_ref, ar_ref, ai_ref, d_ref), (y_ref, sr_ref, si_ref),
         (car_re, car_im, bu_re, bu_im, b_re, b_im, c_re, c_im), run_background) = split(refs)
        run_background()
        i = pl.program_id(1)
        ar, ai = ar_ref[...], ai_ref[...]

        @pl.when(i == 0)
        def _():
            car_re[...] = jnp.zeros_like(car_re)
            car_im[...] = jnp.zeros_like(car_im)
            b_re[...] = _expand_groups(btr_ref[...])
            b_im[...] = _expand_groups(bti_ref[...])
            c_re[...] = _expand_groups(ctr_ref[...])
            c_im[...] = _expand_groups(cti_ref[...])

        u = u_ref[...]
        ub = u.astype(BF16)
        bu_re[...] = jnp.dot(ub, b_re[...], preferred_element_type=F32)
        bu_im[...] = jnp.dot(ub, b_im[...], preferred_element_type=F32)
        car_re[...], car_im[...] = _scan_rows(bu_re, bu_im, sr_ref, si_ref, ar, ai, car_re[...], car_im[...], False)
        y_ref[...] = (nt_dot(sr_ref[...].astype(BF16), c_re[...]) - nt_dot(si_ref[...].astype(BF16), c_im[...])
                      + d_ref[...] * u)

    kspec = lambda shape: pl.BlockSpec((None,) + shape, lambda k, i: (k, 0, 0))
    compact = pl.BlockSpec((H, S), lambda k, i: (0, k))
    res = pl.pallas_call(
        body, name=name, grid=grid,
        in_specs=[pl.BlockSpec((T, C), lambda k, i: (i, k)), compact, compact, compact, compact,
                  kspec((1, S)), kspec((1, S)), kspec((1, C))] + bg_in_specs,
        out_specs=[pl.BlockSpec((T, C), lambda k, i: (i, k)),
                   pl.BlockSpec((T, S), lambda k, i: (i, k)), pl.BlockSpec((T, S), lambda k, i: (i, k))] + bg_out_specs,
        out_shape=[jax.ShapeDtypeStruct((L, NK * C), F32), jax.ShapeDtypeStruct((L, NK * S), F32),
                   jax.ShapeDtypeStruct((L, NK * S), F32)] + bg_out_shapes,
        scratch_shapes=[pltpu.VMEM((1, S), F32), pltpu.VMEM((1, S), F32),
                        pltpu.VMEM((T, S), F32), pltpu.VMEM((T, S), F32)]
        + [pltpu.VMEM((C, S), BF16)] * 4 + bg_scratch,
        compiler_params=_params(_semantics(bg, ("parallel", "arbitrary"))),
    )(z, bt_re, bt_im, ct_re, ct_im, a_re, a_im, d, *bg_args)
    return _results(res, 3, bg)


def ssm_bwd(z, y0, dd_direct, dd_mm, s_re, s_im, bt_re, bt_im, ct_re, ct_im, a_re, a_im, d, *, name, bg=None):
    L = z.shape[0]
    NK, _, S = a_re.shape
    H = bt_re.shape[0]
    C = S // SSM_STATE * SSM_GROUP
    T = _ssm_chunk(L)
    nchunk = L // T
    tail = T // 8
    grid = (NK, nchunk)
    bg_in_specs, bg_args, bg_out_specs, bg_out_shapes, bg_scratch, split = _carrier(bg, 15, 8, 12, grid)
    nt_dot = lambda p, q: lax.dot_general(p, q, (((1,), (1,)), ((), ())), preferred_element_type=F32)

    def body(*refs):
        ((u_ref, y_ref, d1_ref, d2_ref, sr_ref, si_ref, pr_ref, pi_ref,
          btr_ref, bti_ref, ctr_ref, cti_ref, ar_ref, ai_ref, d_ref),
         (du_ref, dbr_ref, dbi_ref, dcr_ref, dci_ref, dar_ref, dai_ref, dd_ref),
         (car_re, car_im, lam_re, lam_im, b_re, b_im, c_re, c_im, acc_br, acc_bi, acc_cr, acc_ci),
         run_background) = split(refs)
        run_background()
        i = pl.program_id(1)
        chunk = nchunk - 1 - i
        ar, ai = ar_ref[...], ai_ref[...]
        row = lax.broadcasted_iota(jnp.int32, (T, S), 0)

        @pl.when(i == 0)
        def _():
            car_re[...] = jnp.zeros_like(car_re)
            car_im[...] = jnp.zeros_like(car_im)
            b_re[...] = _expand_groups(btr_ref[...])
            b_im[...] = _expand_groups(bti_ref[...])
            c_re[...] = _expand_groups(ctr_ref[...])
            c_im[...] = _expand_groups(cti_ref[...])

        u = u_ref[...]
        dy = (d1_ref[...] + d2_ref[...]) * _gelu_grad(y_ref[...])
        dyb = dy.astype(BF16)
        lam_re[...] = jnp.dot(dyb, c_re[...], preferred_element_type=F32)
        lam_im[...] = -jnp.dot(dyb, c_im[...], preferred_element_type=F32)
        car_re[...], car_im[...] = _scan_rows(lam_re, lam_im, lam_re, lam_im, ar, -ai, car_re[...], car_im[...], True)
        lr, li = lam_re[...], lam_im[...]

        s_re, s_im = sr_ref[...], si_ref[...]
        has_prev = (chunk > 0).astype(F32)
        prev_re = pr_ref[7:8, :] * has_prev
        prev_im = pi_ref[7:8, :] * has_prev
        sp_re = jnp.where(row == 0, prev_re, pltpu.roll(s_re, 1, 0))
        sp_im = jnp.where(row == 0, prev_im, pltpu.roll(s_im, 1, 0))
        p_ar = _rowsum(lr * sp_re + li * sp_im)
        p_ai = _rowsum(li * sp_re - lr * sp_im)

        lrb, lib, ub = lr.astype(BF16), li.astype(BF16), u.astype(BF16)
        du = dy * d_ref[...] + nt_dot(lrb, b_re[...]) + nt_dot(lib, b_im[...])
        du_ref[...] = du.astype(BF16)
        tdot = lambda p, q: lax.dot_general(p, q, (((0,), (0,)), ((), ())), preferred_element_type=F32)
        p_br, p_bi = tdot(ub, lrb), tdot(ub, lib)
        p_cr, p_ci = tdot(dyb, s_re.astype(BF16)), -tdot(dyb, s_im.astype(BF16))
        p_dd = _rowsum(dy * u)

        @pl.when(i == 0)
        def _():
            dar_ref[...] = p_ar
            dai_ref[...] = p_ai
            acc_br[...] = p_br
            acc_bi[...] = p_bi
            acc_cr[...] = p_cr
            acc_ci[...] = p_ci
            dd_ref[...] = p_dd

        @pl.when(i > 0)
        def _():
            dar_ref[...] += p_ar
            dai_ref[...] += p_ai
            acc_br[...] += p_br
            acc_bi[...] += p_bi
            acc_cr[...] += p_cr
            acc_ci[...] += p_ci
            dd_ref[...] += p_dd

        @pl.when(i == nchunk - 1)
        def _():
            dbr_ref[...] = _collapse_groups(acc_br[...])
            dbi_ref[...] = _collapse_groups(acc_bi[...])
            dcr_ref[...] = _collapse_groups(acc_cr[...])
            dci_ref[...] = _collapse_groups(acc_ci[...])

    rev = lambda k, i: (nchunk - 1 - i, k)
    prev = lambda k, i: (jnp.maximum((nchunk - 1 - i) * tail - 1, 0), k)
    kspec = lambda shape: pl.BlockSpec((None,) + shape, lambda k, i: (k, 0, 0))
    compact = pl.BlockSpec((H, S), lambda k, i: (0, k))
    compact_shape = jax.ShapeDtypeStruct((H, NK * S), F32)
    res = pl.pallas_call(
        body, name=name, grid=grid,
        in_specs=[pl.BlockSpec((T, C), rev), pl.BlockSpec((T, C), rev), pl.BlockSpec((T, C), rev),
                  pl.BlockSpec((T, C), rev), pl.BlockSpec((T, S), rev), pl.BlockSpec((T, S), rev),
                  pl.BlockSpec((8, S), prev), pl.BlockSpec((8, S), prev),
                  compact, compact, compact, compact,
                  kspec((1, S)), kspec((1, S)), kspec((1, C))] + bg_in_specs,
        out_specs=[pl.BlockSpec((T, C), rev), compact, compact, compact, compact,
                   kspec((1, S)), kspec((1, S)), kspec((1, C))] + bg_out_specs,
        out_shape=[jax.ShapeDtypeStruct((L, NK * C), BF16),
                   compact_shape, compact_shape, compact_shape, compact_shape,
                   jax.ShapeDtypeStruct((NK, 1, S), F32), jax.ShapeDtypeStruct((NK, 1, S), F32),
                   jax.ShapeDtypeStruct((NK, 1, C), F32)] + bg_out_shapes,
        scratch_shapes=[pltpu.VMEM((1, S), F32), pltpu.VMEM((1, S), F32),
                        pltpu.VMEM((T, S), F32), pltpu.VMEM((T, S), F32)]
        + [pltpu.VMEM((C, S), BF16)] * 4 + [pltpu.VMEM((C, S), F32)] * 4 + bg_scratch,
        compiler_params=_params(_semantics(bg, ("parallel", "arbitrary"))),
    )(z, y0, dd_direct, dd_mm, s_re, s_im, s_re, s_im, bt_re, bt_im, ct_re, ct_im, a_re, a_im, d, *bg_args)
    return _results(res, 8, bg)


def _position():
    return lax.axis_index("x"), lax.axis_index("y"), lax.axis_index("c")


def _other_chips(x, y):
    return [(1 - x, y), (x, 1 - y), (1 - x, 1 - y)]


def _gather_phases(n, rows=None):
    def parts(ins, outs, sems):
        send_sems, recv_sems, local_sems = sems
        x, y, c = _position()
        me, sibling = (x, y, c), (x, y, 1 - c)
        chips = _other_chips(x, y)

        def block(a, pos):
            index = 4 * pos[0] + 2 * pos[1] + pos[2]
            if rows is not None:
                return outs[a].at[index, pl.ds(*rows)]
            return outs[a].at[pl.ds(index, 1)] if _is_row(ins[a]) else outs[a].at[index]

        def copy(a, k, pos, to, src=None):
            return pltpu.make_async_remote_copy(
                src_ref=block(a, pos) if src is None else src, dst_ref=block(a, pos),
                send_sem=send_sems.at[7 * a + k], recv_sem=recv_sems.at[7 * a + k],
                device_id=to, device_id_type=MESH)

        shard = [ins[a] if rows is None else ins[a].at[pl.ds(*rows)] for a in range(n)]
        mine = [pltpu.make_async_copy(shard[a], block(a, me), local_sems.at[a]) for a in range(n)]
        first = []
        for a in range(n):
            first.append(copy(a, 0, me, sibling, src=shard[a]))
            first += [copy(a, 1 + j, me, (*chip, c), src=shard[a]) for j, chip in enumerate(chips)]
        passed = [copy(a, 4 + j, (*chip, c), sibling) for a in range(n) for j, chip in enumerate(chips)]
        arrived = [copy(a, 1 + j, (*chip, c), me) for a in range(n) for j, chip in enumerate(chips)]
        from_sibling = []
        for a in range(n):
            from_sibling.append(copy(a, 0, sibling, me))
            from_sibling += [copy(a, 4 + j, (*chip, 1 - c), me) for j, chip in enumerate(chips)]
        return mine, first, passed, arrived, from_sibling

    def send(ins, outs, sems):
        mine, first, _, _, _ = parts(ins, outs, sems)
        for cp in mine + first:
            cp.start()

    def forward(ins, outs, sems):
        _, _, passed, arrived, _ = parts(ins, outs, sems)
        for got, fwd in zip(arrived, passed):
            got.wait_recv()
            fwd.start()

    def finish(ins, outs, sems):
        mine, first, passed, _, from_sibling = parts(ins, outs, sems)
        for cp in from_sibling:
            cp.wait_recv()
        for cp in first + passed:
            cp.wait_send()
        for cp in mine:
            cp.wait()

    return [(0.0, send), (GATHER_FORWARD_AT, forward), (1.0, finish)]


def _is_row(a):
    return len(a.shape) == 2 and a.shape[0] == 1


def _gather_shapes(shards):
    n = len(shards)
    return ([jax.ShapeDtypeStruct((N_DEV,) + (s.shape[1:] if _is_row(s) else s.shape), s.dtype) for s in shards],
            [pltpu.SemaphoreType.DMA((7 * n,)), pltpu.SemaphoreType.DMA((7 * n,)), pltpu.SemaphoreType.DMA((n,))])


def gather_background(shards, rows=None, into=None):
    out_shapes, scratch = _gather_shapes(shards)
    bg = Background(list(shards) + list(into or []), out_shapes, scratch, _gather_phases(len(shards), rows))
    bg.aliases = {len(shards) + k: k for k in range(len(into or []))}
    return bg


def all_gather_blocks(shards, *, name):
    n = len(shards)
    out_shapes, scratch = _gather_shapes(shards)

    def body(*refs):
        for _, phase in _gather_phases(n):
            phase(refs[:n], refs[n:2 * n], refs[2 * n:])

    return pl.pallas_call(
        body, name=name, in_specs=[ANY] * n, out_specs=[ANY] * n, out_shape=out_shapes, scratch_shapes=scratch,
    )(*shards)


def sibling_exchange(grads, *, name):
    n = len(grads)
    bg = sibling_exchange_background(grads)

    def body(*refs):
        for _, phase in bg.phases:
            phase(refs[:n], refs[n:2 * n], refs[2 * n:])

    return pl.pallas_call(
        body, name=name, in_specs=[ANY] * n, out_specs=[ANY] * n, out_shape=bg.out_shapes, scratch_shapes=bg.scratch,
    )(*grads)


def sibling_exchange_background(grads):
    n = len(grads)

    def copies(ins, outs, sems):
        x, y, c = _position()
        return [pltpu.make_async_remote_copy(
            src_ref=ins[a].at[2 * q + 1 - c], dst_ref=outs[a].at[q],
            send_sem=sems[0].at[4 * a + q], recv_sem=sems[1].at[4 * a + q],
            device_id=(x, y, 1 - c), device_id_type=MESH)
            for a in range(n) for q in range(4)]

    def send(ins, outs, sems):
        for cp in copies(ins, outs, sems):
            cp.start()

    def finish(ins, outs, sems):
        for cp in copies(ins, outs, sems):
            cp.wait()

    return Background(grads, [jax.ShapeDtypeStruct((4,) + g.shape[1:], g.dtype) for g in grads],
                      [pltpu.SemaphoreType.DMA((4 * n,)), pltpu.SemaphoreType.DMA((4 * n,))],
                      [(0.0, send), (1.0, finish)])


def _chip_exchange_phases(n):
    def copies(ins, outs, sems):
        x, y, c = _position()
        return [pltpu.make_async_remote_copy(
            src_ref=ins[a].at[2 * chip[0] + chip[1]], dst_ref=outs[a].at[j],
            send_sem=sems[0].at[3 * a + j], recv_sem=sems[1].at[3 * a + j],
            device_id=(*chip, c), device_id_type=MESH)
            for a in range(n) for j, chip in enumerate(_other_chips(x, y))]

    def send(ins, outs, sems):
        for cp in copies(ins, outs, sems):
            cp.start()

    def finish(ins, outs, sems):
        for cp in copies(ins, outs, sems):
            cp.wait()

    return [(0.0, send), (1.0, finish)]


def chip_exchange_background(parts):
    n = len(parts)
    return Background(parts, [jax.ShapeDtypeStruct((3,) + p.shape[1:], p.dtype) for p in parts],
                      [pltpu.SemaphoreType.DMA((3 * n,)), pltpu.SemaphoreType.DMA((3 * n,))],
                      _chip_exchange_phases(n))


def add_pairs(grads, theirs, core, *, name, tm=512):
    _, R, C = theirs.shape
    tm = _tile(R, tm, 16)

    def body(core_ref, a_ref, b_ref, o_ref):
        o_ref[...] = (a_ref[...].astype(F32) + b_ref[...].astype(F32)).astype(o_ref.dtype)

    spec = pl.BlockSpec((None, tm, C), lambda q, i, core_ref: (q, i, 0))
    return pl.pallas_call(
        body, name=name,
        grid_spec=pltpu.PrefetchScalarGridSpec(
            num_scalar_prefetch=1, grid=(4, R // tm),
            in_specs=[pl.BlockSpec((None, tm, C), lambda q, i, core_ref: (2 * q + core_ref[0], i, 0)), spec],
            out_specs=spec),
        out_shape=jax.ShapeDtypeStruct(theirs.shape, BF16),
        compiler_params=_params(("parallel", "parallel")),
    )(core, grads, theirs)


def _adamw(w, g, m, v):
    m = ADAM_B1 * m + (1.0 - ADAM_B1) * g
    v = ADAM_B2 * v + (1.0 - ADAM_B2) * (g * g)
    m_hat = m / (1.0 - ADAM_B1 ** ADAM_STEP)
    v_hat = v / (1.0 - ADAM_B2 ** ADAM_STEP)
    delta = -ADAM_LR * (m_hat / (jnp.sqrt(v_hat) + ADAM_EPS) + ADAM_WD * w)
    return delta, m, v


def adamw_sharded(w, m, v, grads, theirs, others, where, *, name, tm=256):
    R, C = w.shape
    tm = _tile(R, tm, 16)

    def body(where_ref, w_ref, m_ref, v_ref, a_ref, b_ref, o_ref, g_ref, d_ref, nm_ref, nv_ref):
        g = a_ref[...].astype(F32) + b_ref[...].astype(F32)
        for j in range(3):
            g = g + o_ref[j].astype(F32)
        g_ref[...] = g
        d_ref[...], nm_ref[...], nv_ref[...] = _adamw(w_ref[...], g, m_ref[...], v_ref[...])

    spec = pl.BlockSpec((tm, C), lambda i, where_ref: (i, 0))
    sd = jax.ShapeDtypeStruct((R, C), F32)
    return pl.pallas_call(
        body, name=name,
        grid_spec=pltpu.PrefetchScalarGridSpec(
            num_scalar_prefetch=1, grid=(R // tm,),
            in_specs=[spec, spec, spec,
                      pl.BlockSpec((None, tm, C), lambda i, where_ref: (where_ref[0], i, 0)),
                      pl.BlockSpec((None, tm, C), lambda i, where_ref: (where_ref[1], i, 0)),
                      pl.BlockSpec((3, tm, C), lambda i, where_ref: (0, i, 0))],
            out_specs=[spec, spec, spec, spec]),
        out_shape=[sd, sd, sd, sd],
        compiler_params=_params(("parallel",)),
    )(where, w, m, v, grads, theirs, others)


def sum_gathered(gathered, *, name):
    n = len(gathered)

    def body(*refs):
        for ga_ref, o_ref in zip(refs[:n], refs[n:]):
            rows = len(ga_ref.shape) == 2
            total = ga_ref[0:1] if rows else ga_ref[0]
            for dev in range(1, N_DEV):
                total = total + (ga_ref[dev:dev + 1] if rows else ga_ref[dev])
            o_ref[...] = total

    shapes = [jax.ShapeDtypeStruct((1,) + g.shape[1:] if g.ndim == 2 else g.shape[1:], F32) for g in gathered]
    return pl.pallas_call(body, name=name, out_shape=shapes, compiler_params=_params())(*gathered)


def adamw_replicated(ws, ms, vs, gs, *, name):
    n = len(ws)

    def body(*refs):
        w_refs, m_refs, v_refs, g_refs = refs[:n], refs[n:2 * n], refs[2 * n:3 * n], refs[3 * n:4 * n]
        outs = refs[4 * n:]
        for k in range(n):
            outs[k][...], outs[n + k][...], outs[2 * n + k][...] = _adamw(
                w_refs[k][...], g_refs[k][...], m_refs[k][...], v_refs[k][...])

    shapes = [jax.ShapeDtypeStruct(t.shape, F32) for t in ws]
    res = pl.pallas_call(body, name=name, out_shape=shapes * 3, compiler_params=_params())(*ws, *ms, *vs, *gs)
    return res[:n], res[n:2 * n], res[2 * n:]


SHARDED = ("w_in", "ssm_glu_w", "w_out", "w_ffn_in", "w_ffn_out", "w_ple_gate", "w_ple_proj")
SMALL_LAST = ("norm_mix_g",)
SMALL_WIDE = ("ssm_b_re", "ssm_b_im", "ssm_c_re", "ssm_c_im")
SMALL = ("ssm_lambda_re", "ssm_lambda_im", "ssm_log_step", "ssm_b_re", "ssm_b_im", "ssm_c_re",
         "ssm_c_im", "ssm_d", "ssm_glu_b", "sgu_ln_g", "sgu_ln_b", "sgu_w", "sgu_b", "out_norm_ssm_g",
         "out_norm_sgu_g", "norm_ffn_g", "norm_ple_g", "b_ple_gate", "final_norm_g")
WEIGHTS = ("norm_mix_g", "w_in", "ssm_lambda_re", "ssm_lambda_im", "ssm_log_step", "ssm_b_re", "ssm_b_im",
           "ssm_c_re", "ssm_c_im", "ssm_d", "ssm_glu_w", "ssm_glu_b", "sgu_ln_g", "sgu_ln_b", "sgu_w", "sgu_b",
           "out_norm_ssm_g", "out_norm_sgu_g", "w_out", "norm_ffn_g", "w_ffn_in", "w_ffn_out", "norm_ple_g",
           "w_ple_gate", "b_ple_gate", "w_ple_proj", "final_norm_g")


def _step(x, p, loss_target, w, m, v):
    L, D = x.shape[1], x.shape[2]
    x2d, p2d, tgt = x.reshape(L, D), p.reshape(L, -1), loss_target.reshape(L, D)
    d_ssm = w["ssm_glu_w"].shape[2]
    d_sgu = w["sgu_ln_g"].shape[1]
    G, P, H = w["ssm_b_re"].shape[1:]
    SG = min(SSM_SUPER, G)
    NK = G // SG
    row = lambda a: a.reshape(1, -1)

    shard2d = {n: w[n].reshape(w[n].shape[1:]) for n in SHARDED}
    shard_bf = {n: shard2d[n].astype(BF16) for n in SHARDED}
    (w_ple_blk,) = all_gather_blocks([shard_bf["w_ple_proj"]], name="gather_w_ple")
    bf = lambda t: t.astype(BF16)
    pp, (w_in_blk,) = mm_nn(bf(p2d), w_ple_blk, name="ple_proj", out_dtype=F32, tm=512, tn=512, tk=2048,
                            bg=gather_background([shard_bf["w_in"]]))
    w_in = jnp.transpose(w_in_blk, (1, 0, 2)).reshape(D, -1)
    F = shard2d["w_ffn_in"].shape[1] * 4

    lam_re, lam_im, log_step = w["ssm_lambda_re"][0], w["ssm_lambda_im"][0], w["ssm_log_step"][0].reshape(G, 1)
    a_re, a_im, q_re, q_im = disc_lambda_fwd(lam_re, lam_im, log_step, name="s5_discretise_lambda")
    bt_re = w["ssm_b_re"][0].transpose(2, 0, 1).reshape(H, G * P)
    bt_im = w["ssm_b_im"][0].transpose(2, 0, 1).reshape(H, G * P)
    bbar_re, bbar_im = disc_b_fwd(row(q_re), row(q_im), bt_re, bt_im, name="s5_discretise_b")
    ct_re = w["ssm_c_re"][0].transpose(1, 0, 2).reshape(H, G * P)
    ct_im = w["ssm_c_im"][0].transpose(1, 0, 2).reshape(H, G * P)
    a_re_k, a_im_k = a_re.reshape(NK, 1, SG * P), a_im.reshape(NK, 1, SG * P)
    d_k = w["ssm_d"][0].reshape(NK, 1, SG * H)

    h1 = norm_fwd(x2d, w["norm_mix_g"], name="norm_mix")
    z, (w_glu, w_out) = mm_nn(h1, w_in, name="in_proj", out_dtype=F32, tm=512, tn=1024, tk=2048,
                              bg=gather_background([shard_bf["ssm_glu_w"], shard_bf["w_out"]]))
    w_glu, w_out = w_glu.reshape(d_ssm, d_ssm), w_out.reshape(D, D)
    first_rows = (D * 11 // 16) // 16 * 16
    s5_mats = (bbar_re, bbar_im, ct_re, ct_im, a_re_k, a_im_k, d_k)
    (y0, s_re, s_im), (w_ffn_in_part,) = ssm_fwd(
        z, *s5_mats, name="s5_scan", bg=gather_background([shard_bf["w_ffn_in"]], rows=(0, first_rows)))
    ya1 = glu_pre(y0, name="s5_gelu")
    t_glu = mm_nn(ya1, w_glu, name="s5_glu_proj", out_dtype=F32, tm=512, tn=512, tk=2048)
    n_a = glu_post(y0, t_glu, w["ssm_glu_b"], w["out_norm_ssm_g"], name="s5_glu_norm")
    b_s3 = w["sgu_b"][0][:, :, None]
    n_b = sgu_fwd(z, w["sgu_ln_g"], w["sgu_ln_b"], w["sgu_w"][0], b_s3, w["out_norm_sgu_g"], name="sgu", d_sgu=d_sgu)
    ycat = jnp.concatenate([n_a, n_b], axis=1)
    x1, (w_ffn_in_blk,) = mm_nn(
        ycat, w_out, name="out_proj", out_dtype=F32, tm=512, tn=512, tk=2048, residual=x2d,
        bg=gather_background([shard_bf["w_ffn_in"]], rows=(first_rows, D - first_rows), into=[w_ffn_in_part]))
    h2 = norm_fwd(x1, w["norm_ffn_g"], name="norm_ffn")
    (act, gate_ff, up_ff), (w_ffn_out, w_gate) = ffn_in_swiglu(
        h2, w_ffn_in_blk, name="ffn_in_swiglu", tm=256,
        bg=gather_background([shard_bf["w_ffn_out"], shard_bf["w_ple_gate"]]))
    w_ffn_out, w_gate = w_ffn_out.reshape(F, D), w_gate.reshape(D, D)
    x2 = mm_nn(act, w_ffn_out, name="ffn_out", out_dtype=F32, tm=512, tn=512, tk=F, residual=x1)
    h3 = norm_fwd(x2, w["norm_ple_g"], name="norm_ple")
    gpre = mm_nn(h3, w_gate, name="ple_gate", out_dtype=F32, tm=512, tn=1024, tk=2048)

    dx3, dpre, dpp, loss_part, d_final_g, d_b_gate = head_and_loss(
        x2, gpre, w["b_ple_gate"], pp, row(w["final_norm_g"]), tgt, name="head_and_loss")
    x_pos, y_pos, c_pos = _position()
    where = jnp.stack([4 * x_pos + 2 * y_pos + c_pos, 2 * x_pos + y_pos]).astype(jnp.int32)
    core = jnp.reshape(c_pos, (1,)).astype(jnp.int32)
    own, others = {}, {}

    def blocks(named):
        g8 = {n: t.reshape((N_DEV,) + shard2d[n].shape) for n, t in named.items()}
        return g8, sibling_exchange_background(list(g8.values()))

    def chip_sums(g8, theirs):
        own.update(zip(g8, zip(g8.values(), theirs)))
        return [add_pairs(g, t, core, name="chip_sum_" + n) for (n, g), t in zip(g8.items(), theirs)]

    d_w_gate = mm_tn(h3, dpre, name="d_w_ple_gate", out_dtype=BF16, tm=L, tko=1024, tno=1024)
    d_w_ple = mm_tn(bf(p2d), dpp, name="d_w_ple_proj", out_dtype=BF16, tm=L, tko=1024, tno=1024, out_blocks=N_DEV)
    g8_ple, bg = blocks({"w_ple_gate": d_w_gate, "w_ple_proj": d_w_ple})
    dh3, theirs = mm_nt(dpre, w_gate, name="d_h_ple", out_dtype=F32, tm=512, tko=1024, tc=2048, bg=bg)
    bg = chip_exchange_background(chip_sums(g8_ple, theirs))
    dx2, dx2b, d_ple_g = norm_bwd(dh3, x2, w["norm_ple_g"], dx3, name="d_norm_ple", want_bf16=True)
    d_w_ffn_out, got = mm_tn(act, dx2b, name="d_w_ffn_out", out_dtype=BF16, tm=L, tko=1408, tno=512, bg=bg)
    others.update(zip(g8_ple, got))
    g8_fo, bg = blocks({"w_ffn_out": d_w_ffn_out})
    dgate, dup = ffn_out_bwd_swiglu(dx2b, w_ffn_out, gate_ff, up_ff, name="d_act_swiglu")
    half = N_DEV // 2
    d_w_ffn_in, theirs = mm_tn(h2, dgate, name="d_w_ffn_in_gate", out_dtype=BF16, tm=L, tko=512, tno=1408,
                               out_blocks=half, total_blocks=N_DEV, bg=bg)
    bg = chip_exchange_background(chip_sums(g8_fo, theirs))
    d_w_ffn_in, got = mm_tn(h2, dup, name="d_w_ffn_in_up", out_dtype=BF16, tm=L, tko=512, tno=1408,
                            out_blocks=half, block_offset=half, total_blocks=N_DEV, into=d_w_ffn_in, bg=bg)
    others.update(zip(g8_fo, got))
    g8_fi, bg = blocks({"w_ffn_in": d_w_ffn_in})
    dh2, theirs = mm_nt(dgate, w_ffn_in_blk, a2=dup, name="d_h_ffn", out_dtype=F32, tm=1024, tko=1024, tc=1408, bg=bg)
    late_parts = chip_sums(g8_fi, theirs)
    dx1, dx1b, d_ffn_g = norm_bwd(dh2, x1, w["norm_ffn_g"], dx2, name="d_norm_ffn", want_bf16=True)
    dycat = mm_nt(dx1b, w_out, name="d_ycat", out_dtype=F32, tm=512, tko=1024, tc=2048)
    d_w_out = mm_tn(ycat, dx1b, name="d_w_out", out_dtype=BF16, tm=L, tko=1024, tno=1024)
    g8_out, bg = blocks({"w_out": d_w_out})
    dzu, dzv, d_sgu_w, d_sgu_b, d_ln_g, d_ln_b, d_g_b = sgu_bwd(
        z, dycat, w["sgu_ln_g"], w["sgu_ln_b"], w["sgu_w"][0], b_s3, w["out_norm_sgu_g"], name="d_sgu", d_sgu=d_sgu)
    dt_glu, dd_direct, d_g_a, d_glu_b = glu_post_bwd(
        y0, t_glu, w["ssm_glu_b"], w["out_norm_ssm_g"], dycat, name="d_s5_glu_norm")
    d_w_glu, theirs = mm_tn(ya1, dt_glu, name="d_w_glu", out_dtype=BF16, tm=L, tko=1024, tno=1024, bg=bg)
    late_parts += chip_sums(g8_out, theirs)
    g8_glu, bg = blocks({"ssm_glu_w": d_w_glu})
    dd_mm, theirs = mm_nt(dt_glu, w_glu, name="d_s5_glu_proj", out_dtype=F32, tm=512, tko=1024, tc=2048, bg=bg)
    late_parts += chip_sums(g8_glu, theirs)
    (du, d_bbar_re, d_bbar_im, d_ct_re, d_ct_im, d_a_re, d_a_im, d_d), got = ssm_bwd(
        z, y0, dd_direct, dd_mm, s_re, s_im, *s5_mats, name="d_s5_scan", bg=chip_exchange_background(late_parts))
    others.update(zip(("w_ffn_in", "w_out", "ssm_glu_w"), got))
    dz = jnp.concatenate([du, dzu, dzv], axis=1)

    d_q_re, d_q_im, d_bt_re, d_bt_im = disc_b_bwd(row(q_re), row(q_im), bt_re, bt_im, d_bbar_re, d_bbar_im,
                                                  name="d_s5_discretise_b")
    d_lam_re, d_lam_im, d_log_step = disc_lambda_bwd(
        lam_re, lam_im, log_step,
        (d_a_re.reshape(G, P), d_a_im.reshape(G, P), d_q_re.reshape(G, P), d_q_im.reshape(G, P)),
        name="d_s5_discretise_lambda")
    small_grads = {
        "ssm_lambda_re": d_lam_re, "ssm_lambda_im": d_lam_im, "ssm_log_step": d_log_step,
        "ssm_b_re": d_bt_re, "ssm_b_im": d_bt_im, "ssm_c_re": d_ct_re, "ssm_c_im": d_ct_im,
        "ssm_d": d_d, "ssm_glu_b": d_glu_b, "sgu_ln_g": d_ln_g, "sgu_ln_b": d_ln_b,
        "sgu_w": d_sgu_w, "sgu_b": d_sgu_b, "out_norm_ssm_g": d_g_a, "out_norm_sgu_g": d_g_b,
        "norm_ffn_g": d_ffn_g, "norm_ple_g": d_ple_g, "b_ple_gate": d_b_gate, "final_norm_g": d_final_g,
    }

    d_w_in, got = mm_tn(h1, dz, name="d_w_in", out_dtype=BF16, tm=L, tko=1024, tno=1024, out_blocks=N_DEV,
                        bg=gather_background([loss_part] + [small_grads[n] for n in SMALL]))
    sums = sum_gathered(got, name="sum_small_grads")
    g8_in, _ = blocks({"w_in": d_w_in})
    theirs = sibling_exchange(list(g8_in.values()), name="grads_to_sibling_w_in")
    dh1, got = mm_nt(dz, w_in, name="d_h_mix", out_dtype=F32, tm=512, tko=1024, tc=3 * d_sgu,
                     bg=chip_exchange_background(chip_sums(g8_in, theirs)))
    others.update(zip(g8_in, got))
    grad_x, d_mix_g = norm_bwd(dh1, x2d, w["norm_mix_g"], dx1, name="d_norm_mix", want_bf16=False)
    loss, small_sum = sums[0][0, 0], dict(zip(SMALL, sums[1:]))
    (small_sum["norm_mix_g"],) = sum_gathered(all_gather_blocks([d_mix_g], name="gather_last_grad"),
                                              name="sum_last_grad")

    out = {}
    for n in SHARDED:
        res = adamw_sharded(shard2d[n], m[n].reshape(shard2d[n].shape), v[n].reshape(shard2d[n].shape),
                            own[n][0], own[n][1], others[n], where, name="adamw_" + n)
        out[n] = [r.reshape(w[n].shape) for r in res]

    def work_shape(n):
        s = w[n].shape
        return (1,) + s if len(s) == 1 else (s if len(s) == 2 else s[1:])

    for n in ("ssm_b_re", "ssm_b_im"):
        small_sum[n] = small_sum[n].reshape(H, G, P).transpose(1, 2, 0)
    for n in ("ssm_c_re", "ssm_c_im"):
        small_sum[n] = small_sum[n].reshape(H, G, P).transpose(1, 0, 2)

    def replicated(names_, name):
        gs = [small_sum[n].reshape(work_shape(n)) for n in names_]
        res = adamw_replicated(*[[t[n].reshape(work_shape(n)) for n in names_] for t in (w, m, v)], gs, name=name)
        for i, n in enumerate(names_):
            out[n] = [r.reshape(w[n].shape) for r in (gs[i], res[0][i], res[1][i], res[2][i])]

    replicated([n for n in SMALL + SMALL_LAST if n not in SMALL_WIDE], "adamw_small")
    replicated(list(SMALL_WIDE), "adamw_s5_b_c")

    grads = [out[n][0] for n in WEIGHTS]
    deltas = [out[n][1] for n in WEIGHTS]
    new_m = [out[n][2] for n in WEIGHTS]
    new_v = [out[n][3] for n in WEIGHTS]
    return (loss, grad_x.reshape(x.shape), *grads, *deltas, *new_m, *new_v)


def kernel(x, p, norm_mix_g, w_in, ssm_lambda_re, ssm_lambda_im, ssm_log_step, ssm_b_re, ssm_b_im, ssm_c_re, ssm_c_im, ssm_d, ssm_glu_w, ssm_glu_b, sgu_ln_g, sgu_ln_b, sgu_w, sgu_b, out_norm_ssm_g, out_norm_sgu_g, w_out, norm_ffn_g, w_ffn_in, w_ffn_out, norm_ple_g, w_ple_gate, b_ple_gate, w_ple_proj, final_norm_g, loss_target, m_norm_mix_g, m_w_in, m_ssm_lambda_re, m_ssm_lambda_im, m_ssm_log_step, m_ssm_b_re, m_ssm_b_im, m_ssm_c_re, m_ssm_c_im, m_ssm_d, m_ssm_glu_w, m_ssm_glu_b, m_sgu_ln_g, m_sgu_ln_b, m_sgu_w, m_sgu_b, m_out_norm_ssm_g, m_out_norm_sgu_g, m_w_out, m_norm_ffn_g, m_w_ffn_in, m_w_ffn_out, m_norm_ple_g, m_w_ple_gate, m_b_ple_gate, m_w_ple_proj, m_final_norm_g, v_norm_mix_g, v_w_in, v_ssm_lambda_re, v_ssm_lambda_im, v_ssm_log_step, v_ssm_b_re, v_ssm_b_im, v_ssm_c_re, v_ssm_c_im, v_ssm_d, v_ssm_glu_w, v_ssm_glu_b, v_sgu_ln_g, v_sgu_ln_b, v_sgu_w, v_sgu_b, v_out_norm_ssm_g, v_out_norm_sgu_g, v_w_out, v_norm_ffn_g, v_w_ffn_in, v_w_ffn_out, v_norm_ple_g, v_w_ple_gate, v_b_ple_gate, v_w_ple_proj, v_final_norm_g):
    given = dict(locals())
    w = {n: given[n] for n in WEIGHTS}
    m = {n: given["m_" + n] for n in WEIGHTS}
    v = {n: given["v_" + n] for n in WEIGHTS}
    return _step(x, p, loss_target, w, m, v)
```

```python
import functools
import math

import jax
import jax.numpy as jnp
from jax import lax
from jax.experimental import pallas as pl
from jax.experimental.pallas import tpu as pltpu

F32 = jnp.float32
BF16 = jnp.bfloat16
MESH = pl.DeviceIdType.MESH
ANY = pl.BlockSpec(memory_space=pl.ANY)

N_DEV = 8
EPS = 1e-6
LAMBDA_RE_MAX = -1e-4
SSM_GROUP = 16
SSM_STATE = 64
SSM_SUPER = 16
SGU_CHUNK = 128
ADAM_LR, ADAM_B1, ADAM_B2, ADAM_EPS, ADAM_WD, ADAM_STEP = 0.001, 0.9, 0.999, 1e-08, 0.01, 10
VMEM_LIMIT = 52 * 1024 * 1024
LANE = 128
GATHER_FORWARD_AT = 0.85

_GELU_C = math.sqrt(2.0 / math.pi)


def _params(sem=None):
    return pltpu.CompilerParams(dimension_semantics=sem, vmem_limit_bytes=VMEM_LIMIT)


def _tile(dim, pref, unit=LANE):
    if dim <= pref:
        return dim
    t = (pref // unit) * unit
    while t >= unit:
        if dim % t == 0:
            return t
        t -= unit
    return dim


def _gelu(x):
    return 0.5 * x * (1.0 + jnp.tanh(_GELU_C * (x + 0.044715 * x * x * x)))


def _gelu_grad(x):
    t = jnp.tanh(_GELU_C * (x + 0.044715 * x * x * x))
    return 0.5 * (1.0 + t) + 0.5 * x * (1.0 - t * t) * (_GELU_C * (1.0 + 3.0 * 0.044715 * x * x))


def _gelu_and_grad(x):
    t = jnp.tanh(_GELU_C * (x + 0.044715 * x * x * x))
    return (0.5 * x * (1.0 + t),
            0.5 * (1.0 + t) + 0.5 * x * (1.0 - t * t) * (_GELU_C * (1.0 + 3.0 * 0.044715 * x * x)))


def _rms(x):
    return lax.rsqrt(jnp.mean(x * x, axis=-1, keepdims=True) + EPS)


def _rmsnorm_bwd(dy, x, r, g):
    dyg = dy * g
    return r * dyg - x * (r * r * r) * jnp.mean(dyg * x, axis=-1, keepdims=True)


def _rowsum(v):
    return jnp.sum(v, axis=0, keepdims=True)


class Background:
    def __init__(self, inputs, out_shapes, scratch, phases):
        self.inputs, self.out_shapes, self.scratch, self.phases = list(inputs), list(out_shapes), list(scratch), phases
        self.aliases = {}

    def emit(self, step, nsteps, ins, outs, scratch):
        for place, phase in self.phases:
            at = min(int(place * nsteps), nsteps - 1)

            @pl.when(step == at)
            def _():
                phase(ins, outs, scratch)


def _carrier(bg, n_in, n_out, n_scratch, grid):
    nbi = len(bg.inputs) if bg else 0
    nbo = len(bg.out_shapes) if bg else 0
    nsteps = math.prod(grid)

    def split(refs):
        ins = refs[:n_in]
        bg_ins = refs[n_in:n_in + nbi]
        outs = refs[n_in + nbi:n_in + nbi + n_out]
        bg_outs = refs[n_in + nbi + n_out:n_in + nbi + n_out + nbo]
        rest = refs[n_in + nbi + n_out + nbo:]
        scratch, bg_scratch = rest[:n_scratch], rest[n_scratch:]

        def run_background():
            if bg is None:
                return
            step = pl.program_id(0)
            for axis in range(1, len(grid)):
                step = step * grid[axis] + pl.program_id(axis)
            bg.emit(step, nsteps, bg_ins, bg_outs, bg_scratch)

        return ins, outs, scratch, run_background

    if bg is None:
        return [], [], [], [], [], split
    return [ANY] * nbi, list(bg.inputs), [ANY] * nbo, list(bg.out_shapes), list(bg.scratch), split


def _semantics(bg, sem):
    return tuple("arbitrary" for _ in sem) if bg is not None else sem


def _results(res, n_out, bg):
    res = list(res) if isinstance(res, (list, tuple)) else [res]
    own = res[0] if n_out == 1 else res[:n_out]
    return (own, res[n_out:]) if bg is not None else own


def mm_nn(a, b, *, name, out_dtype, tm, tn, tk, residual=None, bg=None):
    M, K = a.shape
    blocked = b.ndim == 3
    if blocked:
        nb, _, Nb = b.shape
        N = nb * Nb
        tn = _tile(Nb, tn)
        per = Nb // tn
    else:
        N = b.shape[1]
        tn = _tile(N, tn)
    tm, tk = _tile(M, tm, 8), _tile(K, tk)
    nj, ni, nk = N // tn, M // tm, K // tk
    has_res = residual is not None
    grid = (nj, ni, nk)
    bg_in_specs, bg_args, bg_out_specs, bg_out_shapes, bg_scratch, split = _carrier(
        bg, 3 if has_res else 2, 1, 0 if nk == 1 else 1, grid)

    def body(*refs):
        ins, (o_ref,), scratch, run_background = split(refs)
        run_background()
        a_ref, b_ref = ins[0], ins[1]
        r_ref = ins[2] if has_res else None

        def finish(acc):
            if has_res:
                acc = acc + r_ref[...]
            o_ref[...] = acc.astype(o_ref.dtype)

        part = jnp.dot(a_ref[...], b_ref[...], preferred_element_type=F32)
        if nk == 1:
            finish(part)
        else:
            acc_ref = scratch[0]
            k = pl.program_id(2)

            @pl.when(k == 0)
            def _():
                acc_ref[...] = part

            @pl.when(k > 0)
            def _():
                acc_ref[...] += part

            @pl.when(k == nk - 1)
            def _():
                finish(acc_ref[...])

    if blocked:
        b_spec = pl.BlockSpec((None, tk, tn), lambda j, i, k: (j // per, k, j % per))
    else:
        b_spec = pl.BlockSpec((tk, tn), lambda j, i, k: (k, j))
    in_specs = [pl.BlockSpec((tm, tk), lambda j, i, k: (i, k)), b_spec]
    args = [a, b]
    if has_res:
        in_specs.append(pl.BlockSpec((tm, tn), lambda j, i, k: (i, j)))
        args.append(residual)
    res = pl.pallas_call(
        body, name=name, grid=grid,
        in_specs=in_specs + bg_in_specs,
        out_specs=[pl.BlockSpec((tm, tn), lambda j, i, k: (i, j))] + bg_out_specs,
        out_shape=[jax.ShapeDtypeStruct((M, N), out_dtype)] + bg_out_shapes,
        input_output_aliases={len(args) + k: 1 + o for k, o in (bg.aliases if bg else {}).items()},
        scratch_shapes=([] if nk == 1 else [pltpu.VMEM((tm, tn), F32)]) + bg_scratch,
        compiler_params=_params(_semantics(bg, ("parallel", "parallel", "arbitrary"))),
    )(*args, *bg_args)
    return _results(res, 1, bg)


def mm_nt(a, w, *, name, out_dtype, tm, tko, tc, a2=None, bg=None):
    M, N = a.shape
    if a2 is not None:
        N = 2 * N
    blocked = w.ndim == 3
    if blocked:
        nb, Ko, Nb = w.shape
        tc = _tile(Nb, tc)
        per = Nb // tc
    else:
        Ko = w.shape[0]
        tc = _tile(N, tc)
    tm, tko = _tile(M, tm, 8), _tile(Ko, tko)
    njo, ni, nc = Ko // tko, M // tm, N // tc
    grid = (njo, ni, nc)
    half = nc // 2
    bg_in_specs, bg_args, bg_out_specs, bg_out_shapes, bg_scratch, split = _carrier(
        bg, 2 if a2 is None else 3, 1, 0 if nc == 1 else 1, grid)

    def body(*refs):
        ins, (o_ref,), scratch, run_background = split(refs)
        run_background()
        a_val = ins[0][...]
        if a2 is not None:
            a_val = jnp.where(pl.program_id(2) < half, a_val, ins[1][...])
        part = lax.dot_general(a_val, ins[-1][...], (((1,), (1,)), ((), ())),
                               preferred_element_type=F32)
        if nc == 1:
            o_ref[...] = part.astype(o_ref.dtype)
        else:
            acc_ref = scratch[0]
            c = pl.program_id(2)

            @pl.when(c == 0)
            def _():
                acc_ref[...] = part

            @pl.when(c > 0)
            def _():
                acc_ref[...] += part

            @pl.when(c == nc - 1)
            def _():
                o_ref[...] = acc_ref[...].astype(o_ref.dtype)

    if blocked:
        w_spec = pl.BlockSpec((None, tko, tc), lambda j, i, c: (c // per, j, c % per))
    else:
        w_spec = pl.BlockSpec((tko, tc), lambda j, i, c: (j, c))
    if a2 is None:
        a_specs, a_args = [pl.BlockSpec((tm, tc), lambda j, i, c: (i, c))], [a]
    else:
        a_specs = [pl.BlockSpec((tm, tc), lambda j, i, c: (i, jnp.minimum(c, half - 1))),
                   pl.BlockSpec((tm, tc), lambda j, i, c: (i, jnp.maximum(c - half, 0)))]
        a_args = [a, a2]
    res = pl.pallas_call(
        body, name=name, grid=grid,
        in_specs=a_specs + [w_spec] + bg_in_specs,
        out_specs=[pl.BlockSpec((tm, tko), lambda j, i, c: (i, j))] + bg_out_specs,
        out_shape=[jax.ShapeDtypeStruct((M, Ko), out_dtype)] + bg_out_shapes,
        scratch_shapes=([] if nc == 1 else [pltpu.VMEM((tm, tko), F32)]) + bg_scratch,
        compiler_params=_params(_semantics(bg, ("parallel", "parallel", "arbitrary"))),
    )(*a_args, w, *bg_args)
    return _results(res, 1, bg)


def mm_tn(a, g, *, name, out_dtype, tm, tko, tno, out_blocks=None, block_offset=0, total_blocks=None, into=None,
          bg=None):
    M, K = a.shape
    N = g.shape[1]
    if out_blocks:
        Nb = N // out_blocks
        tno = _tile(Nb, tno)
        per = Nb // tno
    else:
        tno = _tile(N, tno)
    tm, tko = _tile(M, tm), _tile(K, tko)
    njo, njn, nm = K // tko, N // tno, M // tm
    grid = (njo, njn, nm)
    bg_in_specs, bg_args, bg_out_specs, bg_out_shapes, bg_scratch, split = _carrier(
        bg, 2 if into is None else 3, 1, 0 if nm == 1 else 1, grid)

    def body(*refs):
        ins, (o_ref,), scratch, run_background = split(refs)
        a_ref, g_ref = ins[0], ins[1]
        run_background()
        part = lax.dot_general(a_ref[...], g_ref[...], (((0,), (0,)), ((), ())),
                               preferred_element_type=F32)
        if nm == 1:
            o_ref[...] = part.astype(o_ref.dtype)
        else:
            acc_ref = scratch[0]
            m = pl.program_id(2)

            @pl.when(m == 0)
            def _():
                acc_ref[...] = part

            @pl.when(m > 0)
            def _():
                acc_ref[...] += part

            @pl.when(m == nm - 1)
            def _():
                o_ref[...] = acc_ref[...].astype(o_ref.dtype)

    if out_blocks:
        o_spec = pl.BlockSpec((None, tko, tno), lambda jo, jn, m: (jn // per + block_offset, jo, jn % per))
        o_shape = jax.ShapeDtypeStruct((total_blocks or out_blocks, K, Nb), out_dtype)
    else:
        o_spec = pl.BlockSpec((tko, tno), lambda jo, jn, m: (jo, jn))
        o_shape = jax.ShapeDtypeStruct((K, N), out_dtype)
    res = pl.pallas_call(
        body, name=name, grid=grid,
        in_specs=[pl.BlockSpec((tm, tko), lambda jo, jn, m: (m, jo)),
                  pl.BlockSpec((tm, tno), lambda jo, jn, m: (m, jn))] + ([] if into is None else [ANY]) + bg_in_specs,
        out_specs=[o_spec] + bg_out_specs, out_shape=[o_shape] + bg_out_shapes,
        scratch_shapes=([] if nm == 1 else [pltpu.VMEM((tko, tno), F32)]) + bg_scratch,
        input_output_aliases={} if into is None else {2: 0},
        compiler_params=_params(_semantics(bg, ("parallel", "parallel", "arbitrary"))),
    )(a, g, *([] if into is None else [into]), *bg_args)
    return _results(res, 1, bg)


def ffn_in_swiglu(h, w_blk, *, name, tm, bg=None):
    M, K = h.shape
    nb, _, Nb = w_blk.shape
    nh = nb // 2
    F = nh * Nb
    tm = _tile(M, tm, 8)
    grid = (nh, M // tm)
    bg_in_specs, bg_args, bg_out_specs, bg_out_shapes, bg_scratch, split = _carrier(bg, 3, 3, 0, grid)

    def body(*refs):
        (h_ref, wg_ref, wu_ref), (act_ref, gate_ref, up_ref), _, run_background = split(refs)
        run_background()
        hv = h_ref[...]
        gate = jnp.dot(hv, wg_ref[...], preferred_element_type=F32)
        up = jnp.dot(hv, wu_ref[...], preferred_element_type=F32)
        gate_ref[...] = gate
        up_ref[...] = up
        act_ref[...] = (gate * jax.nn.sigmoid(gate) * up).astype(act_ref.dtype)

    o_spec = pl.BlockSpec((tm, Nb), lambda j, i: (i, j))
    res = pl.pallas_call(
        body, name=name, grid=grid,
        in_specs=[pl.BlockSpec((tm, K), lambda j, i: (i, 0)),
                  pl.BlockSpec((None, K, Nb), lambda j, i: (j, 0, 0)),
                  pl.BlockSpec((None, K, Nb), lambda j, i: (j + nh, 0, 0))] + bg_in_specs,
        out_specs=[o_spec, o_spec, o_spec] + bg_out_specs,
        out_shape=[jax.ShapeDtypeStruct((M, F), BF16), jax.ShapeDtypeStruct((M, F), F32),
                   jax.ShapeDtypeStruct((M, F), F32)] + bg_out_shapes,
        scratch_shapes=bg_scratch,
        compiler_params=_params(_semantics(bg, ("parallel", "parallel"))),
    )(h, w_blk, w_blk, *bg_args)
    return _results(res, 3, bg)


def _row_spec(tm, d, col=0):
    return pl.BlockSpec((tm, d), lambda i: (i, col))


def _vec_spec(d):
    return pl.BlockSpec((1, d), lambda i: (0, 0))


def norm_fwd(x, g, *, name, tm=256):
    L, D = x.shape
    tm = _tile(L, tm, 8)

    def body(x_ref, g_ref, h_ref):
        xv = x_ref[...]
        h_ref[...] = (xv * _rms(xv) * g_ref[...]).astype(h_ref.dtype)

    return pl.pallas_call(
        body, name=name, grid=(L // tm,),
        in_specs=[_row_spec(tm, D), _vec_spec(D)],
        out_specs=_row_spec(tm, D),
        out_shape=jax.ShapeDtypeStruct((L, D), BF16),
        compiler_params=_params(("parallel",)),
    )(x, g)


def norm_bwd(dh, xin, g, dres, *, name, want_bf16, tm=128):
    L, D = xin.shape
    tm = _tile(L, tm, 8)

    def body(dh_ref, x_ref, g_ref, dres_ref, dx_ref, *rest):
        dg_ref = rest[-1]
        xv, dhv = x_ref[...], dh_ref[...]
        r = _rms(xv)
        dx = dres_ref[...] + _rmsnorm_bwd(dhv, xv, r, g_ref[...])
        dx_ref[...] = dx
        if want_bf16:
            rest[0][...] = dx.astype(BF16)
        part = _rowsum(dhv * xv * r)

        @pl.when(pl.program_id(0) == 0)
        def _():
            dg_ref[...] = part

        @pl.when(pl.program_id(0) > 0)
        def _():
            dg_ref[...] += part

    out_specs = [_row_spec(tm, D)] + ([_row_spec(tm, D)] if want_bf16 else []) + [_vec_spec(D)]
    out_shape = ([jax.ShapeDtypeStruct((L, D), F32)]
                 + ([jax.ShapeDtypeStruct((L, D), BF16)] if want_bf16 else [])
                 + [jax.ShapeDtypeStruct((1, D), F32)])
    return pl.pallas_call(
        body, name=name, grid=(L // tm,),
        in_specs=[_row_spec(tm, D), _row_spec(tm, D), _vec_spec(D), _row_spec(tm, D)],
        out_specs=out_specs, out_shape=out_shape,
        compiler_params=_params(("arbitrary",)),
    )(dh, xin, g, dres)


def glu_pre(y0, *, name, tm=256):
    L, D = y0.shape
    tm = _tile(L, tm, 8)

    def body(y_ref, o_ref):
        o_ref[...] = _gelu(y_ref[...]).astype(o_ref.dtype)

    return pl.pallas_call(
        body, name=name, grid=(L // tm,),
        in_specs=[_row_spec(tm, D)], out_specs=_row_spec(tm, D),
        out_shape=jax.ShapeDtypeStruct((L, D), BF16),
        compiler_params=_params(("parallel",)),
    )(y0)


def glu_post(y0, t, b_glu, g_a, *, name, tm=256):
    L, D = y0.shape
    tm = _tile(L, tm, 8)

    def body(y_ref, t_ref, b_ref, g_ref, o_ref):
        ya = _gelu(y_ref[...]) * jax.nn.sigmoid(t_ref[...] + b_ref[...])
        o_ref[...] = (ya * _rms(ya) * g_ref[...]).astype(o_ref.dtype)

    return pl.pallas_call(
        body, name=name, grid=(L // tm,),
        in_specs=[_row_spec(tm, D), _row_spec(tm, D), _vec_spec(D), _vec_spec(D)],
        out_specs=_row_spec(tm, D),
        out_shape=jax.ShapeDtypeStruct((L, D), BF16),
        compiler_params=_params(("parallel",)),
    )(y0, t, b_glu, g_a)


def glu_post_bwd(y0, t, b_glu, g_a, dycat, *, name, tm=128):
    L, D = y0.shape
    tm = _tile(L, tm, 8)

    def body(y_ref, t_ref, b_ref, g_ref, dn_ref, dt_ref, dd_ref, dga_ref, dbg_ref):
        ya1 = _gelu(y_ref[...])
        sg = jax.nn.sigmoid(t_ref[...] + b_ref[...])
        ya = ya1 * sg
        ra = _rms(ya)
        dn = dn_ref[...]
        dya = _rmsnorm_bwd(dn, ya, ra, g_ref[...])
        dt = dya * ya1 * sg * (1.0 - sg)
        dt_ref[...] = dt.astype(BF16)
        dd_ref[...] = dya * sg
        p_ga, p_bg = _rowsum(dn * ya * ra), _rowsum(dt)

        @pl.when(pl.program_id(0) == 0)
        def _():
            dga_ref[...] = p_ga
            dbg_ref[...] = p_bg

        @pl.when(pl.program_id(0) > 0)
        def _():
            dga_ref[...] += p_ga
            dbg_ref[...] += p_bg

    return pl.pallas_call(
        body, name=name, grid=(L // tm,),
        in_specs=[_row_spec(tm, D), _row_spec(tm, D), _vec_spec(D), _vec_spec(D), _row_spec(tm, D, 0)],
        out_specs=[_row_spec(tm, D), _row_spec(tm, D), _vec_spec(D), _vec_spec(D)],
        out_shape=[jax.ShapeDtypeStruct((L, D), BF16), jax.ShapeDtypeStruct((L, D), F32),
                   jax.ShapeDtypeStruct((1, D), F32), jax.ShapeDtypeStruct((1, D), F32)],
        compiler_params=_params(("arbitrary",)),
    )(y0, t, b_glu, g_a, dycat)


def head_and_loss(x2, gpre, b_g, pp, g_f, tgt, *, name, tm=128):
    L, D = x2.shape
    tm = _tile(L, tm, 8)

    def body(x2_ref, gp_ref, bg_ref, pp_ref, gf_ref, tg_ref,
             dx3_ref, dpre_ref, dpp_ref, loss_ref, dgf_ref, dbg_ref):
        gate = jax.nn.sigmoid(gp_ref[...] + bg_ref[...])
        ppv = pp_ref[...]
        x3 = x2_ref[...] + gate * ppv
        r = _rms(x3)
        xn = x3 * r
        gf = gf_ref[...]
        err = xn * gf - tg_ref[...]
        loss = 0.5 * jnp.sum(jnp.mean(err * err, axis=-1, keepdims=True), axis=0, keepdims=True)
        dout = err * (1.0 / D)
        dx3 = _rmsnorm_bwd(dout, x3, r, gf)
        dx3_ref[...] = dx3
        dpre = dx3 * ppv * gate * (1.0 - gate)
        dpre_ref[...] = dpre.astype(BF16)
        dpp_ref[...] = (dx3 * gate).astype(BF16)
        p_gf, p_bg = _rowsum(dout * xn), _rowsum(dpre)
        p_loss = jnp.broadcast_to(loss, loss_ref.shape)

        @pl.when(pl.program_id(0) == 0)
        def _():
            loss_ref[...] = p_loss
            dgf_ref[...] = p_gf
            dbg_ref[...] = p_bg

        @pl.when(pl.program_id(0) > 0)
        def _():
            loss_ref[...] += p_loss
            dgf_ref[...] += p_gf
            dbg_ref[...] += p_bg

    rs = _row_spec(tm, D)
    return pl.pallas_call(
        body, name=name, grid=(L // tm,),
        in_specs=[rs, rs, _vec_spec(D), rs, _vec_spec(D), rs],
        out_specs=[rs, rs, rs, pl.BlockSpec((8, LANE), lambda i: (0, 0)), _vec_spec(D), _vec_spec(D)],
        out_shape=[jax.ShapeDtypeStruct((L, D), F32), jax.ShapeDtypeStruct((L, D), BF16),
                   jax.ShapeDtypeStruct((L, D), BF16), jax.ShapeDtypeStruct((8, LANE), F32),
                   jax.ShapeDtypeStruct((1, D), F32), jax.ShapeDtypeStruct((1, D), F32)],
        compiler_params=_params(("arbitrary",)),
    )(x2, gpre, b_g, pp, g_f, tgt)


def ffn_out_bwd_swiglu(dx, w, gate, up, *, name, tm=512, tf=1408):
    M, D = dx.shape
    F = w.shape[0]
    tm, tf = _tile(M, tm, 8), _tile(F, tf)

    halves = 2 if tm % 32 == 0 else 1

    def body(dx_ref, w_ref, g_ref, u_ref, dg_ref, du_ref):
        for r in range(halves):
            rows = slice(r * (tm // halves), (r + 1) * (tm // halves))
            da = lax.dot_general(dx_ref[rows, :], w_ref[...], (((1,), (1,)), ((), ())), preferred_element_type=F32)
            gv = g_ref[rows, :]
            sg = jax.nn.sigmoid(gv)
            dg_ref[rows, :] = (da * u_ref[rows, :] * sg * (1.0 + gv * (1.0 - sg))).astype(BF16)
            du_ref[rows, :] = (da * gv * sg).astype(BF16)

    spec = pl.BlockSpec((tm, tf), lambda j, i: (i, j))
    return pl.pallas_call(
        body, name=name, grid=(F // tf, M // tm),
        in_specs=[pl.BlockSpec((tm, D), lambda j, i: (i, 0)), pl.BlockSpec((tf, D), lambda j, i: (j, 0)), spec, spec],
        out_specs=[spec, spec],
        out_shape=[jax.ShapeDtypeStruct((M, F), BF16), jax.ShapeDtypeStruct((M, F), BF16)],
        compiler_params=_params(("parallel", "parallel")),
    )(dx, w, gate, up)


def _sgu_forward_values(u1, v1, lng, lnb, w_ref, bs_ref, s_scr, heads, hd):
    xc = v1 - jnp.mean(v1, axis=-1, keepdims=True)
    r = lax.rsqrt(jnp.mean(xc * xc, axis=-1, keepdims=True) + EPS)
    xhat = xc * r
    v2 = xhat * lng + lnb
    tril = (lax.broadcasted_iota(jnp.int32, (SGU_CHUNK, SGU_CHUNK), 0)
            >= lax.broadcasted_iota(jnp.int32, (SGU_CHUNK, SGU_CHUNK), 1))
    for h in range(heads):
        wm = jnp.where(tril, w_ref[h], 0.0).astype(BF16)
        cols = slice(h * hd, (h + 1) * hd)
        s_scr[:, cols] = jnp.dot(wm, v2[:, cols].astype(BF16), preferred_element_type=F32) + bs_ref[h]
    return xhat, r, v2, tril


def sgu_fwd(z, ln_g, ln_b, w_s, b_s, g_b, *, name, d_sgu):
    L = z.shape[0]
    heads = w_s.shape[0]
    hd = d_sgu // heads

    def body(zu_ref, zv_ref, lng_ref, lnb_ref, w_ref, bs_ref, gb_ref, o_ref, s_scr):
        u1 = _gelu(zu_ref[...])
        _sgu_forward_values(u1, _gelu(zv_ref[...]), lng_ref[...], lnb_ref[...], w_ref, bs_ref, s_scr, heads, hd)
        yb = u1 * s_scr[...]
        o_ref[...] = (yb * _rms(yb) * gb_ref[...]).astype(o_ref.dtype)

    blk = lambda col: pl.BlockSpec((SGU_CHUNK, d_sgu), lambda n: (n, col))
    return pl.pallas_call(
        body, name=name, grid=(L // SGU_CHUNK,),
        in_specs=[blk(1), blk(2), _vec_spec(d_sgu), _vec_spec(d_sgu),
                  pl.BlockSpec(w_s.shape, lambda n: (0, 0, 0)), pl.BlockSpec(b_s.shape, lambda n: (0, 0, 0)),
                  _vec_spec(d_sgu)],
        out_specs=blk(0),
        out_shape=jax.ShapeDtypeStruct((L, d_sgu), BF16),
        scratch_shapes=[pltpu.VMEM((SGU_CHUNK, d_sgu), F32)],
        compiler_params=_params(("parallel",)),
    )(z, z, ln_g, ln_b, w_s, b_s, g_b)


def sgu_bwd(z, dycat, ln_g, ln_b, w_s, b_s, g_b, *, name, d_sgu):
    L = z.shape[0]
    heads = w_s.shape[0]
    hd = d_sgu // heads

    def body(zu_ref, zv_ref, dn_ref, lng_ref, lnb_ref, w_ref, bs_ref, gb_ref,
             dzu_ref, dzv_ref, dw_ref, dbs_ref, dlng_ref, dlnb_ref, dgb_ref, s_scr, dv_scr):
        first = pl.program_id(0) == 0
        lng = lng_ref[...]
        u1, du1 = _gelu_and_grad(zu_ref[...])
        v1, dv1_dz = _gelu_and_grad(zv_ref[...])
        xhat, r, v2, tril = _sgu_forward_values(u1, v1, lng, lnb_ref[...], w_ref, bs_ref, s_scr, heads, hd)
        s = s_scr[...]
        yb = u1 * s
        rb = _rms(yb)
        dn = dn_ref[...]
        dyb = _rmsnorm_bwd(dn, yb, rb, gb_ref[...])
        dzu_ref[...] = (dyb * s * du1).astype(BF16)
        ds = dyb * u1
        for h in range(heads):
            cols = slice(h * hd, (h + 1) * hd)
            ds_h = ds[:, cols]
            ds_hb = ds_h.astype(BF16)
            wm = jnp.where(tril, w_ref[h], 0.0).astype(BF16)
            dw_h = jnp.where(tril, lax.dot_general(ds_hb, v2[:, cols].astype(BF16), (((1,), (1,)), ((), ())),
                                                   preferred_element_type=F32), 0.0)
            db_h = jnp.sum(ds_h.T, axis=0, keepdims=True)
            dv_scr[:, cols] = lax.dot_general(wm, ds_hb, (((0,), (0,)), ((), ())), preferred_element_type=F32)

            @pl.when(first)
            def _():
                dw_ref[h] = dw_h
                dbs_ref[h] = db_h

            @pl.when(jnp.logical_not(first))
            def _():
                dw_ref[h] += dw_h
                dbs_ref[h] += db_h

        dv2 = dv_scr[...]
        dxh = dv2 * lng
        dv1 = r * (dxh - jnp.mean(dxh, axis=-1, keepdims=True)
                   - xhat * jnp.mean(dxh * xhat, axis=-1, keepdims=True))
        dzv_ref[...] = (dv1 * dv1_dz).astype(BF16)
        p_lng, p_lnb, p_gb = _rowsum(dv2 * xhat), _rowsum(dv2), _rowsum(dn * yb * rb)

        @pl.when(first)
        def _():
            dlng_ref[...] = p_lng
            dlnb_ref[...] = p_lnb
            dgb_ref[...] = p_gb

        @pl.when(jnp.logical_not(first))
        def _():
            dlng_ref[...] += p_lng
            dlnb_ref[...] += p_lnb
            dgb_ref[...] += p_gb

    blk = lambda col: pl.BlockSpec((SGU_CHUNK, d_sgu), lambda n: (n, col))
    full3 = lambda shape: pl.BlockSpec(shape, lambda n: (0, 0, 0))
    return pl.pallas_call(
        body, name=name, grid=(L // SGU_CHUNK,),
        in_specs=[blk(1), blk(2), blk(1), _vec_spec(d_sgu), _vec_spec(d_sgu),
                  full3(w_s.shape), full3(b_s.shape), _vec_spec(d_sgu)],
        out_specs=[blk(0), blk(0), full3(w_s.shape), full3((heads, 1, SGU_CHUNK)),
                   _vec_spec(d_sgu), _vec_spec(d_sgu), _vec_spec(d_sgu)],
        out_shape=[jax.ShapeDtypeStruct((L, d_sgu), BF16), jax.ShapeDtypeStruct((L, d_sgu), BF16),
                   jax.ShapeDtypeStruct(w_s.shape, F32), jax.ShapeDtypeStruct((heads, 1, SGU_CHUNK), F32),
                   jax.ShapeDtypeStruct((1, d_sgu), F32), jax.ShapeDtypeStruct((1, d_sgu), F32),
                   jax.ShapeDtypeStruct((1, d_sgu), F32)],
        scratch_shapes=[pltpu.VMEM((SGU_CHUNK, d_sgu), F32), pltpu.VMEM((SGU_CHUNK, d_sgu), F32)],
        compiler_params=_params(("arbitrary",)),
    )(z, z, dycat, ln_g, ln_b, w_s, b_s, g_b)


def _disc_lambda(lam_re, lam_im, log_step):
    lr = jnp.minimum(lam_re, LAMBDA_RE_MAX)
    li = lam_im
    dt = jnp.exp(log_step)
    mag = jnp.exp(lr * dt)
    ang = li * dt
    a_re = mag * jnp.cos(ang)
    a_im = mag * jnp.sin(ang)
    nr = a_re - 1.0
    ni = a_im
    den = lr * lr + li * li
    return a_re, a_im, (nr * lr + ni * li) / den, (ni * lr - nr * li) / den


def _disc_b(q_re, q_im, b_re, b_im):
    return q_re * b_re - q_im * b_im, q_re * b_im + q_im * b_re


def disc_lambda_fwd(lam_re, lam_im, log_step, *, name):
    def body(lr_ref, li_ref, ls_ref, ar_ref, ai_ref, qr_ref, qi_ref):
        ar_ref[...], ai_ref[...], qr_ref[...], qi_ref[...] = _disc_lambda(lr_ref[...], li_ref[...], ls_ref[...])

    sd = jax.ShapeDtypeStruct(lam_re.shape, F32)
    return pl.pallas_call(body, name=name, out_shape=[sd, sd, sd, sd], compiler_params=_params())(
        lam_re, lam_im, log_step)


def disc_lambda_bwd(lam_re, lam_im, log_step, cts, *, name):
    def body(lr_ref, li_ref, ls_ref, c0, c1, c2, c3, dlr_ref, dli_ref, dls_ref):
        _, vjp = jax.vjp(_disc_lambda, lr_ref[...], li_ref[...], ls_ref[...])
        dlr_ref[...], dli_ref[...], dls_ref[...] = vjp((c0[...], c1[...], c2[...], c3[...]))

    sd = jax.ShapeDtypeStruct(lam_re.shape, F32)
    return pl.pallas_call(body, name=name, out_shape=[sd, sd, jax.ShapeDtypeStruct(log_step.shape, F32)],
                          compiler_params=_params())(lam_re, lam_im, log_step, *cts)


def disc_b_fwd(q_re, q_im, b_re, b_im, *, name):
    def body(qr_ref, qi_ref, br_ref, bi_ref, or_ref, oi_ref):
        or_ref[...], oi_ref[...] = _disc_b(qr_ref[...], qi_ref[...], br_ref[...], bi_ref[...])

    sd = jax.ShapeDtypeStruct(b_re.shape, F32)
    return pl.pallas_call(body, name=name, out_shape=[sd, sd], compiler_params=_params())(q_re, q_im, b_re, b_im)


def disc_b_bwd(q_re, q_im, b_re, b_im, ct_re, ct_im, *, name):
    def body(qr_ref, qi_ref, br_ref, bi_ref, cr_ref, ci_ref, dqr_ref, dqi_ref, dbr_ref, dbi_ref):
        _, vjp = jax.vjp(_disc_b, qr_ref[...], qi_ref[...], br_ref[...], bi_ref[...])
        dqr_ref[...], dqi_ref[...], dbr_ref[...], dbi_ref[...] = vjp((cr_ref[...], ci_ref[...]))

    sq, sb = jax.ShapeDtypeStruct(q_re.shape, F32), jax.ShapeDtypeStruct(b_re.shape, F32)
    return pl.pallas_call(body, name=name, out_shape=[sq, sq, sb, sb], compiler_params=_params())(
        q_re, q_im, b_re, b_im, ct_re, ct_im)


def _lti_scan(xr, xi, ar, ai, reverse):
    T = xr.shape[0]
    row = lax.broadcasted_iota(jnp.int32, xr.shape, 0)
    k = 1
    while k < T:
        shift = T - k if reverse else k
        keep = (row < T - k) if reverse else (row >= k)
        sr = jnp.where(keep, pltpu.roll(xr, shift, 0), 0.0)
        si = jnp.where(keep, pltpu.roll(xi, shift, 0), 0.0)
        xr, xi = xr + ar * sr - ai * si, xi + ar * si + ai * sr
        ar, ai = ar * ar - ai * ai, 2.0 * ar * ai
        k *= 2
    return xr, xi


SUBLANES = 8


def _scan_rows(x_re, x_im, o_re, o_im, ar, ai, cr, ci, reverse):
    T, n = x_re.shape
    groups = T // SUBLANES
    row = lax.broadcasted_iota(jnp.int32, (SUBLANES, n), 0)
    edge = SUBLANES - 1 if reverse else 0
    pr, pi = _lti_scan(jnp.where(row == edge, ar, 0.0), jnp.where(row == edge, ai, 0.0), ar, ai, reverse)
    pows = []
    for level in range(3):
        k = 1 << level
        keep = (row < SUBLANES - k) if reverse else (row >= k)
        pows.append((jnp.where(keep, ar, 0.0), jnp.where(keep, ai, 0.0)))
        ar, ai = ar * ar - ai * ai, 2.0 * ar * ai

    def group(i, carry):
        cr, ci = carry
        at = pl.multiple_of((groups - 1 - i if reverse else i) * SUBLANES, SUBLANES)
        xr, xi = x_re[pl.ds(at, SUBLANES), :], x_im[pl.ds(at, SUBLANES), :]
        for level, (qr, qi) in enumerate(pows):
            shift = SUBLANES - (1 << level) if reverse else 1 << level
            sr, si = pltpu.roll(xr, shift, 0), pltpu.roll(xi, shift, 0)
            xr, xi = xr + qr * sr - qi * si, xi + qr * si + qi * sr
        xr, xi = xr + pr * cr - pi * ci, xi + pr * ci + pi * cr
        o_re[pl.ds(at, SUBLANES), :] = xr
        o_im[pl.ds(at, SUBLANES), :] = xi
        return spread(xr[last:last + 1, :]), spread(xi[last:last + 1, :])

    last = 0 if reverse else SUBLANES - 1
    spread = lambda v: jnp.broadcast_to(v, (SUBLANES, n))
    cr, ci = lax.fori_loop(0, groups, group, (spread(cr), spread(ci)), unroll=2)
    return cr[0:1, :], ci[0:1, :]


def _ssm_chunk(L):
    return _tile(L, 512, 8)


def _same_group(rows, cols):
    r = lax.broadcasted_iota(jnp.int32, (rows, cols), 0) // SSM_GROUP
    c = lax.broadcasted_iota(jnp.int32, (rows, cols), 1) // SSM_STATE
    return r == c


def _expand_groups(compact):
    H, S = compact.shape
    tiled = jnp.concatenate([compact] * (S // SSM_STATE), axis=0)
    return jnp.where(_same_group(tiled.shape[0], S), tiled, 0.0).astype(BF16)


def _collapse_groups(dense):
    C, S = dense.shape
    masked = jnp.where(_same_group(C, S), dense, 0.0)
    total = masked[0:SSM_GROUP]
    for g in range(1, C // SSM_GROUP):
        total = total + masked[g * SSM_GROUP:(g + 1) * SSM_GROUP]
    return total


def ssm_fwd(z, bt_re, bt_im, ct_re, ct_im, a_re, a_im, d, *, name, bg=None):
    L = z.shape[0]
    NK, _, S = a_re.shape
    H = bt_re.shape[0]
    C = S // SSM_STATE * SSM_GROUP
    T = _ssm_chunk(L)
    grid = (NK, L // T)
    bg_in_specs, bg_args, bg_out_specs, bg_out_shapes, bg_scratch, split = _carrier(bg, 8, 3, 8, grid)
    nt_dot = lambda p, q: lax.dot_general(p, q, (((1,), (1,)), ((), ())), preferred_element_type=F32)

    def body(*refs):
        ((u_ref, btr_ref, bti_ref, ctr_ref, cti_ref, ar_ref, ai_ref, d_ref), (y_ref, sr_ref, si_ref),
         (car_re, car_im, bu_re, bu_im, b_re, b_im, c_re, c_im), run_background) = split(refs)
        run_background()
        i = pl.program_id(1)
        ar, ai = ar_ref[...], ai_ref[...]

        @pl.when(i == 0)
        def _():
            car_re[...] = jnp.zeros_like(car_re)
            car_im[...] = jnp.zeros_like(car_im)
            b_re[...] = _expand_groups(btr_ref[...])
            b_im[...] = _expand_groups(bti_ref[...])
            c_re[...] = _expand_groups(ctr_ref[...])
            c_im[...] = _expand_groups(cti_ref[...])

        u = u_ref[...]
        ub = u.astype(BF16)
        bu_re[...] = jnp.dot(ub, b_re[...], preferred_element_type=F32)
        bu_im[...] = jnp.dot(ub, b_im[...], preferred_element_type=F32)
        car_re[...], car_im[...] = _scan_rows(bu_re, bu_im, sr_ref, si_ref, ar, ai, car_re[...], car_im[...], False)
        y_ref[...] = (nt_dot(sr_ref[...].astype(BF16), c_re[...]) - nt_dot(si_ref[...].astype(BF16), c_im[...])
                      + d_ref[...] * u)

    kspec = lambda shape: pl.BlockSpec((None,) + shape, lambda k, i: (k, 0, 0))
    compact = pl.BlockSpec((H, S), lambda k, i: (0, k))
    res = pl.pallas_call(
        body, name=name, grid=grid,
        in_specs=[pl.BlockSpec((T, C), lambda k, i: (i, k)), compact, compact, compact, compact,
                  kspec((1, S)), kspec((1, S)), kspec((1, C))] + bg_in_specs,
        out_specs=[pl.BlockSpec((T, C), lambda k, i: (i, k)),
                   pl.BlockSpec((T, S), lambda k, i: (i, k)), pl.BlockSpec((T, S), lambda k, i: (i, k))] + bg_out_specs,
        out_shape=[jax.ShapeDtypeStruct((L, NK * C), F32), jax.ShapeDtypeStruct((L, NK * S), F32),
                   jax.ShapeDtypeStruct((L, NK * S), F32)] + bg_out_shapes,
        scratch_shapes=[pltpu.VMEM((1, S), F32), pltpu.VMEM((1, S), F32),
                        pltpu.VMEM((T, S), F32), pltpu.VMEM((T, S), F32)]
        + [pltpu.VMEM((C, S), BF16)] * 4 + bg_scratch,
        compiler_params=_params(_semantics(bg, ("parallel", "arbitrary"))),
    )(z, bt_re, bt_im, ct_re, ct_im, a_re, a_im, d, *bg_args)
    return _results(res, 3, bg)


def ssm_bwd(z, y0, dd_direct, dd_mm, s_re, s_im, bt_re, bt_im, ct_re, ct_im, a_re, a_im, d, *, name, bg=None):
    L = z.shape[0]
    NK, _, S = a_re.shape
    H = bt_re.shape[0]
    C = S // SSM_STATE * SSM_GROUP
    T = _ssm_chunk(L)
    nchunk = L // T
    tail = T // 8
    grid = (NK, nchunk)
    bg_in_specs, bg_args, bg_out_specs, bg_out_shapes, bg_scratch, split = _carrier(bg, 15, 8, 12, grid)
    nt_dot = lambda p, q: lax.dot_general(p, q, (((1,), (1,)), ((), ())), preferred_element_type=F32)

    def body(*refs):
        ((u_ref, y_ref, d1_ref, d2_ref, sr_ref, si_ref, pr_ref, pi_ref,
          btr_ref, bti_ref, ctr_ref, cti_ref, ar_ref, ai_ref, d_ref),
         (du_ref, dbr_ref, dbi_ref, dcr_ref, dci_ref, dar_ref, dai_ref, dd_ref),
         (car_re, car_im, lam_re, lam_im, b_re, b_im, c_re, c_im, acc_br, acc_bi, acc_cr, acc_ci),
         run_background) = split(refs)
        run_background()
        i = pl.program_id(1)
        chunk = nchunk - 1 - i
        ar, ai = ar_ref[...], ai_ref[...]
        row = lax.broadcasted_iota(jnp.int32, (T, S), 0)

        @pl.when(i == 0)
        def _():
            car_re[...] = jnp.zeros_like(car_re)
            car_im[...] = jnp.zeros_like(car_im)
            b_re[...] = _expand_groups(btr_ref[...])
            b_im[...] = _expand_groups(bti_ref[...])
            c_re[...] = _expand_groups(ctr_ref[...])
            c_im[...] = _expand_groups(cti_ref[...])

        u = u_ref[...]
        dy = (d1_ref[...] + d2_ref[...]) * _gelu_grad(y_ref[...])
        dyb = dy.astype(BF16)
        lam_re[...] = jnp.dot(dyb, c_re[...], preferred_element_type=F32)
        lam_im[...] = -jnp.dot(dyb, c_im[...], preferred_element_type=F32)
        car_re[...], car_im[...] = _scan_rows(lam_re, lam_im, lam_re, lam_im, ar, -ai, car_re[...], car_im[...], True)
        lr, li = lam_re[...], lam_im[...]

        s_re, s_im = sr_ref[...], si_ref[...]
        has_prev = (chunk > 0).astype(F32)
        prev_re = pr_ref[7:8, :] * has_prev
        prev_im = pi_ref[7:8, :] * has_prev
        sp_re = jnp.where(row == 0, prev_re, pltpu.roll(s_re, 1, 0))
        sp_im = jnp.where(row == 0, prev_im, pltpu.roll(s_im, 1, 0))
        p_ar = _rowsum(lr * sp_re + li * sp_im)
        p_ai = _rowsum(li * sp_re - lr * sp_im)

        lrb, lib, ub = lr.astype(BF16), li.astype(BF16), u.astype(BF16)
        du = dy * d_ref[...] + nt_dot(lrb, b_re[...]) + nt_dot(lib, b_im[...])
        du_ref[...] = du.astype(BF16)
        tdot = lambda p, q: lax.dot_general(p, q, (((0,), (0,)), ((), ())), preferred_element_type=F32)
        p_br, p_bi = tdot(ub, lrb), tdot(ub, lib)
        p_cr, p_ci = tdot(dyb, s_re.astype(BF16)), -tdot(dyb, s_im.astype(BF16))
        p_dd = _rowsum(dy * u)

        @pl.when(i == 0)
        def _():
            dar_ref[...] = p_ar
            dai_ref[...] = p_ai
            acc_br[...] = p_br
            acc_bi[...] = p_bi
            acc_cr[...] = p_cr
            acc_ci[...] = p_ci
            dd_ref[...] = p_dd

        @pl.when(i > 0)
        def _():
            dar_ref[...] += p_ar
            dai_ref[...] += p_ai
            acc_br[...] += p_br
            acc_bi[...] += p_bi
            acc_cr[...] += p_cr
            acc_ci[...] += p_ci
            dd_ref[...] += p_dd

        @pl.when(i == nchunk - 1)
        def _():
            dbr_ref[...] = _collapse_groups(acc_br[...])
            dbi_ref[...] = _collapse_groups(acc_bi[...])
            dcr_ref[...] = _collapse_groups(acc_cr[...])
            dci_ref[...] = _collapse_groups(acc_ci[...])

    rev = lambda k, i: (nchunk - 1 - i, k)
    prev = lambda k, i: (jnp.maximum((nchunk - 1 - i) * tail - 1, 0), k)
    kspec = lambda shape: pl.BlockSpec((None,) + shape, lambda k, i: (k, 0, 0))
    compact = pl.BlockSpec((H, S), lambda k, i: (0, k))
    compact_shape = jax.ShapeDtypeStruct((H, NK * S), F32)
    res = pl.pallas_call(
        body, name=name, grid=grid,
        in_specs=[pl.BlockSpec((T, C), rev), pl.BlockSpec((T, C), rev), pl.BlockSpec((T, C), rev),
                  pl.BlockSpec((T, C), rev), pl.BlockSpec((T, S), rev), pl.BlockSpec((T, S), rev),
                  pl.BlockSpec((8, S), prev), pl.BlockSpec((8, S), prev),
                  compact, compact, compact, compact,
                  kspec((1, S)), kspec((1, S)), kspec((1, C))] + bg_in_specs,
        out_specs=[pl.BlockSpec((T, C), rev), compact, compact, compact, compact,
                   kspec((1, S)), kspec((1, S)), kspec((1, C))] + bg_out_specs,
        out_shape=[jax.ShapeDtypeStruct((L, NK * C), BF16),
                   compact_shape, compact_shape, compact_shape, compact_shape,
                   jax.ShapeDtypeStruct((NK, 1, S), F32), jax.ShapeDtypeStruct((NK, 1, S), F32),
                   jax.ShapeDtypeStruct((NK, 1, C), F32)] + bg_out_shapes,
        scratch_shapes=[pltpu.VMEM((1, S), F32), pltpu.VMEM((1, S), F32),
                        pltpu.VMEM((T, S), F32), pltpu.VMEM((T, S), F32)]
        + [pltpu.VMEM((C, S), BF16)] * 4 + [pltpu.VMEM((C, S), F32)] * 4 + bg_scratch,
        compiler_params=_params(_semantics(bg, ("parallel", "arbitrary"))),
    )(z, y0, dd_direct, dd_mm, s_re, s_im, s_re, s_im, bt_re, bt_im, ct_re, ct_im, a_re, a_im, d, *bg_args)
    return _results(res, 8, bg)


def _position():
    return lax.axis_index("x"), lax.axis_index("y"), lax.axis_index("c")


def _other_chips(x, y):
    return [(1 - x, y), (x, 1 - y), (1 - x, 1 - y)]


def _gather_phases(n, rows=None):
    def parts(ins, outs, sems):
        send_sems, recv_sems, local_sems = sems
        x, y, c = _position()
        me, sibling = (x, y, c), (x, y, 1 - c)
        chips = _other_chips(x, y)

        def block(a, pos):
            index = 4 * pos[0] + 2 * pos[1] + pos[2]
            if rows is not None:
                return outs[a].at[index, pl.ds(*rows)]
            return outs[a].at[pl.ds(index, 1)] if _is_row(ins[a]) else outs[a].at[index]

        def copy(a, k, pos, to, src=None):
            return pltpu.make_async_remote_copy(
                src_ref=block(a, pos) if src is None else src, dst_ref=block(a, pos),
                send_sem=send_sems.at[7 * a + k], recv_sem=recv_sems.at[7 * a + k],
                device_id=to, device_id_type=MESH)

        shard = [ins[a] if rows is None else ins[a].at[pl.ds(*rows)] for a in range(n)]
        mine = [pltpu.make_async_copy(shard[a], block(a, me), local_sems.at[a]) for a in range(n)]
        first = []
        for a in range(n):
            first.append(copy(a, 0, me, sibling, src=shard[a]))
            first += [copy(a, 1 + j, me, (*chip, c), src=shard[a]) for j, chip in enumerate(chips)]
        passed = [copy(a, 4 + j, (*chip, c), sibling) for a in range(n) for j, chip in enumerate(chips)]
        arrived = [copy(a, 1 + j, (*chip, c), me) for a in range(n) for j, chip in enumerate(chips)]
        from_sibling = []
        for a in range(n):
            from_sibling.append(copy(a, 0, sibling, me))
            from_sibling += [copy(a, 4 + j, (*chip, 1 - c), me) for j, chip in enumerate(chips)]
        return mine, first, passed, arrived, from_sibling

    def send(ins, outs, sems):
        mine, first, _, _, _ = parts(ins, outs, sems)
        for cp in mine + first:
            cp.start()

    def forward(ins, outs, sems):
        _, _, passed, arrived, _ = parts(ins, outs, sems)
        for got, fwd in zip(arrived, passed):
            got.wait_recv()
            fwd.start()

    def finish(ins, outs, sems):
        mine, first, passed, _, from_sibling = parts(ins, outs, sems)
        for cp in from_sibling:
            cp.wait_recv()
        for cp in first + passed:
            cp.wait_send()
        for cp in mine:
            cp.wait()

    return [(0.0, send), (GATHER_FORWARD_AT, forward), (1.0, finish)]


def _is_row(a):
    return len(a.shape) == 2 and a.shape[0] == 1


def _gather_shapes(shards):
    n = len(shards)
    return ([jax.ShapeDtypeStruct((N_DEV,) + (s.shape[1:] if _is_row(s) else s.shape), s.dtype) for s in shards],
            [pltpu.SemaphoreType.DMA((7 * n,)), pltpu.SemaphoreType.DMA((7 * n,)), pltpu.SemaphoreType.DMA((n,))])


def gather_background(shards, rows=None, into=None):
    out_shapes, scratch = _gather_shapes(shards)
    bg = Background(list(shards) + list(into or []), out_shapes, scratch, _gather_phases(len(shards), rows))
    bg.aliases = {len(shards) + k: k for k in range(len(into or []))}
    return bg


def all_gather_blocks(shards, *, name):
    n = len(shards)
    out_shapes, scratch = _gather_shapes(shards)

    def body(*refs):
        for _, phase in _gather_phases(n):
            phase(refs[:n], refs[n:2 * n], refs[2 * n:])

    return pl.pallas_call(
        body, name=name, in_specs=[ANY] * n, out_specs=[ANY] * n, out_shape=out_shapes, scratch_shapes=scratch,
    )(*shards)


def sibling_exchange(grads, *, name):
    n = len(grads)
    bg = sibling_exchange_background(grads)

    def body(*refs):
        for _, phase in bg.phases:
            phase(refs[:n], refs[n:2 * n], refs[2 * n:])

    return pl.pallas_call(
        body, name=name, in_specs=[ANY] * n, out_specs=[ANY] * n, out_shape=bg.out_shapes, scratch_shapes=bg.scratch,
    )(*grads)


def sibling_exchange_background(grads):
    n = len(grads)

    def copies(ins, outs, sems):
        x, y, c = _position()
        return [pltpu.make_async_remote_copy(
            src_ref=ins[a].at[2 * q + 1 - c], dst_ref=outs[a].at[q],
            send_sem=sems[0].at[4 * a + q], recv_sem=sems[1].at[4 * a + q],
            device_id=(x, y, 1 - c), device_id_type=MESH)
            for a in range(n) for q in range(4)]

    def send(ins, outs, sems):
        for cp in copies(ins, outs, sems):
            cp.start()

    def finish(ins, outs, sems):
        for cp in copies(ins, outs, sems):
            cp.wait()

    return Background(grads, [jax.ShapeDtypeStruct((4,) + g.shape[1:], g.dtype) for g in grads],
                      [pltpu.SemaphoreType.DMA((4 * n,)), pltpu.SemaphoreType.DMA((4 * n,))],
                      [(0.0, send), (1.0, finish)])


def _chip_exchange_phases(n):
    def copies(ins, outs, sems):
        x, y, c = _position()
        return [pltpu.make_async_remote_copy(
            src_ref=ins[a].at[2 * chip[0] + chip[1]], dst_ref=outs[a].at[j],
            send_sem=sems[0].at[3 * a + j], recv_sem=sems[1].at[3 * a + j],
            device_id=(*chip, c), device_id_type=MESH)
            for a in range(n) for j, chip in enumerate(_other_chips(x, y))]

    def send(ins, outs, sems):
        for cp in copies(ins, outs, sems):
            cp.start()

    def finish(ins, outs, sems):
        for cp in copies(ins, outs, sems):
            cp.wait()

    return [(0.0, send), (1.0, finish)]


def chip_exchange_background(parts):
    n = len(parts)
    return Background(parts, [jax.ShapeDtypeStruct((3,) + p.shape[1:], p.dtype) for p in parts],
                      [pltpu.SemaphoreType.DMA((3 * n,)), pltpu.SemaphoreType.DMA((3 * n,))],
                      _chip_exchange_phases(n))


def add_pairs(grads, theirs, core, *, name, tm=512):
    _, R, C = theirs.shape
    tm = _tile(R, tm, 16)

    def body(core_ref, a_ref, b_ref, o_ref):
        o_ref[...] = (a_ref[...].astype(F32) + b_ref[...].astype(F32)).astype(o_ref.dtype)

    spec = pl.BlockSpec((None, tm, C), lambda q, i, core_ref: (q, i, 0))
    return pl.pallas_call(
        body, name=name,
        grid_spec=pltpu.PrefetchScalarGridSpec(
            num_scalar_prefetch=1, grid=(4, R // tm),
            in_specs=[pl.BlockSpec((None, tm, C), lambda q, i, core_ref: (2 * q + core_ref[0], i, 0)), spec],
            out_specs=spec),
        out_shape=jax.ShapeDtypeStruct(theirs.shape, BF16),
        compiler_params=_params(("parallel", "parallel")),
    )(core, grads, theirs)


def _adamw(w, g, m, v):
    m = ADAM_B1 * m + (1.0 - ADAM_B1) * g
    v = ADAM_B2 * v + (1.0 - ADAM_B2) * (g * g)
    m_hat = m / (1.0 - ADAM_B1 ** ADAM_STEP)
    v_hat = v / (1.0 - ADAM_B2 ** ADAM_STEP)
    delta = -ADAM_LR * (m_hat / (jnp.sqrt(v_hat) + ADAM_EPS) + ADAM_WD * w)
    return delta, m, v


def adamw_sharded(w, m, v, grads, theirs, others, where, *, name, tm=256):
    R, C = w.shape
    tm = _tile(R, tm, 16)

    def body(where_ref, w_ref, m_ref, v_ref, a_ref, b_ref, o_ref, g_ref, d_ref, nm_ref, nv_ref):
        g = a_ref[...].astype(F32) + b_ref[...].astype(F32)
        for j in range(3):
            g = g + o_ref[j].astype(F32)
        g_ref[...] = g
        d_ref[...], nm_ref[...], nv_ref[...] = _adamw(w_ref[...], g, m_ref[...], v_ref[...])

    spec = pl.BlockSpec((tm, C), lambda i, where_ref: (i, 0))
    sd = jax.ShapeDtypeStruct((R, C), F32)
    return pl.pallas_call(
        body, name=name,
        grid_spec=pltpu.PrefetchScalarGridSpec(
            num_scalar_prefetch=1, grid=(R // tm,),
            in_specs=[spec, spec, spec,
                      pl.BlockSpec((None, tm, C), lambda i, where_ref: (where_ref[0], i, 0)),
                      pl.BlockSpec((None, tm, C), lambda i, where_ref: (where_ref[1], i, 0)),
                      pl.BlockSpec((3, tm, C), lambda i, where_ref: (0, i, 0))],
            out_specs=[spec, spec, spec, spec]),
        out_shape=[sd, sd, sd, sd],
        compiler_params=_params(("parallel",)),
    )(where, w, m, v, grads, theirs, others)


def sum_gathered(gathered, *, name):
    n = len(gathered)

    def body(*refs):
        for ga_ref, o_ref in zip(refs[:n], refs[n:]):
            rows = len(ga_ref.shape) == 2
            total = ga_ref[0:1] if rows else ga_ref[0]
            for dev in range(1, N_DEV):
                total = total + (ga_ref[dev:dev + 1] if rows else ga_ref[dev])
            o_ref[...] = total

    shapes = [jax.ShapeDtypeStruct((1,) + g.shape[1:] if g.ndim == 2 else g.shape[1:], F32) for g in gathered]
    return pl.pallas_call(body, name=name, out_shape=shapes, compiler_params=_params())(*gathered)


def adamw_replicated(ws, ms, vs, gs, *, name):
    n = len(ws)

    def body(*refs):
        w_refs, m_refs, v_refs, g_refs = refs[:n], refs[n:2 * n], refs[2 * n:3 * n], refs[3 * n:4 * n]
        outs = refs[4 * n:]
        for k in range(n):
            outs[k][...], outs[n + k][...], outs[2 * n + k][...] = _adamw(
                w_refs[k][...], g_refs[k][...], m_refs[k][...], v_refs[k][...])

    shapes = [jax.ShapeDtypeStruct(t.shape, F32) for t in ws]
    res = pl.pallas_call(body, name=name, out_shape=shapes * 3, compiler_params=_params())(*ws, *ms, *vs, *gs)
    return res[:n], res[n:2 * n], res[2 * n:]


SHARDED = ("w_in", "ssm_glu_w", "w_out", "w_ffn_in", "w_ffn_out", "w_ple_gate", "w_ple_proj")
SMALL_LAST = ("norm_mix_g",)
SMALL_WIDE = ("ssm_b_re", "ssm_b_im", "ssm_c_re", "ssm_c_im")
SMALL = ("ssm_lambda_re", "ssm_lambda_im", "ssm_log_step", "ssm_b_re", "ssm_b_im", "ssm_c_re",
         "ssm_c_im", "ssm_d", "ssm_glu_b", "sgu_ln_g", "sgu_ln_b", "sgu_w", "sgu_b", "out_norm_ssm_g",
         "out_norm_sgu_g", "norm_ffn_g", "norm_ple_g", "b_ple_gate", "final_norm_g")
WEIGHTS = ("norm_mix_g", "w_in", "ssm_lambda_re", "ssm_lambda_im", "ssm_log_step", "ssm_b_re", "ssm_b_im",
           "ssm_c_re", "ssm_c_im", "ssm_d", "ssm_glu_w", "ssm_glu_b", "sgu_ln_g", "sgu_ln_b", "sgu_w", "sgu_b",
           "out_norm_ssm_g", "out_norm_sgu_g", "w_out", "norm_ffn_g", "w_ffn_in", "w_ffn_out", "norm_ple_g",
           "w_ple_gate", "b_ple_gate", "w_ple_proj", "final_norm_g")


def _step(x, p, loss_target, w, m, v):
    L, D = x.shape[1], x.shape[2]
    x2d, p2d, tgt = x.reshape(L, D), p.reshape(L, -1), loss_target.reshape(L, D)
    d_ssm = w["ssm_glu_w"].shape[2]
    d_sgu = w["sgu_ln_g"].shape[1]
    G, P, H = w["ssm_b_re"].shape[1:]
    SG = min(SSM_SUPER, G)
    NK = G // SG
    row = lambda a: a.reshape(1, -1)

    shard2d = {n: w[n].reshape(w[n].shape[1:]) for n in SHARDED}
    shard_bf = {n: shard2d[n].astype(BF16) for n in SHARDED}
    (w_ple_blk,) = all_gather_blocks([shard_bf["w_ple_proj"]], name="gather_w_ple")
    bf = lambda t: t.astype(BF16)
    pp, (w_in_blk,) = mm_nn(bf(p2d), w_ple_blk, name="ple_proj", out_dtype=F32, tm=512, tn=512, tk=2048,
                            bg=gather_background([shard_bf["w_in"]]))
    w_in = jnp.transpose(w_in_blk, (1, 0, 2)).reshape(D, -1)
    F = shard2d["w_ffn_in"].shape[1] * 4

    lam_re, lam_im, log_step = w["ssm_lambda_re"][0], w["ssm_lambda_im"][0], w["ssm_log_step"][0].reshape(G, 1)
    a_re, a_im, q_re, q_im = disc_lambda_fwd(lam_re, lam_im, log_step, name="s5_discretise_lambda")
    bt_re = w["ssm_b_re"][0].transpose(2, 0, 1).reshape(H, G * P)
    bt_im = w["ssm_b_im"][0].transpose(2, 0, 1).reshape(H, G * P)
    bbar_re, bbar_im = disc_b_fwd(row(q_re), row(q_im), bt_re, bt_im, name="s5_discretise_b")
    ct_re = w["ssm_c_re"][0].transpose(1, 0, 2).reshape(H, G * P)
    ct_im = w["ssm_c_im"][0].transpose(1, 0, 2).reshape(H, G * P)
    a_re_k, a_im_k = a_re.reshape(NK, 1, SG * P), a_im.reshape(NK, 1, SG * P)
    d_k = w["ssm_d"][0].reshape(NK, 1, SG * H)

    h1 = norm_fwd(x2d, w["norm_mix_g"], name="norm_mix")
    z, (w_glu, w_out) = mm_nn(h1, w_in, name="in_proj", out_dtype=F32, tm=512, tn=1024, tk=2048,
                              bg=gather_background([shard_bf["ssm_glu_w"], shard_bf["w_out"]]))
    w_glu, w_out = w_glu.reshape(d_ssm, d_ssm), w_out.reshape(D, D)
    first_rows = (D * 11 // 16) // 16 * 16
    s5_mats = (bbar_re, bbar_im, ct_re, ct_im, a_re_k, a_im_k, d_k)
    (y0, s_re, s_im), (w_ffn_in_part,) = ssm_fwd(
        z, *s5_mats, name="s5_scan", bg=gather_background([shard_bf["w_ffn_in"]], rows=(0, first_rows)))
    ya1 = glu_pre(y0, name="s5_gelu")
    t_glu = mm_nn(ya1, w_glu, name="s5_glu_proj", out_dtype=F32, tm=512, tn=512, tk=2048)
    n_a = glu_post(y0, t_glu, w["ssm_glu_b"], w["out_norm_ssm_g"], name="s5_glu_norm")
    b_s3 = w["sgu_b"][0][:, :, None]
    n_b = sgu_fwd(z, w["sgu_ln_g"], w["sgu_ln_b"], w["sgu_w"][0], b_s3, w["out_norm_sgu_g"], name="sgu", d_sgu=d_sgu)
    ycat = jnp.concatenate([n_a, n_b], axis=1)
    x1, (w_ffn_in_blk,) = mm_nn(
        ycat, w_out, name="out_proj", out_dtype=F32, tm=512, tn=512, tk=2048, residual=x2d,
        bg=gather_background([shard_bf["w_ffn_in"]], rows=(first_rows, D - first_rows), into=[w_ffn_in_part]))
    h2 = norm_fwd(x1, w["norm_ffn_g"], name="norm_ffn")
    (act, gate_ff, up_ff), (w_ffn_out, w_gate) = ffn_in_swiglu(
        h2, w_ffn_in_blk, name="ffn_in_swiglu", tm=256,
        bg=gather_background([shard_bf["w_ffn_out"], shard_bf["w_ple_gate"]]))
    w_ffn_out, w_gate = w_ffn_out.reshape(F, D), w_gate.reshape(D, D)
    x2 = mm_nn(act, w_ffn_out, name="ffn_out", out_dtype=F32, tm=512, tn=512, tk=F, residual=x1)
    h3 = norm_fwd(x2, w["norm_ple_g"], name="norm_ple")
    gpre = mm_nn(h3, w_gate, name="ple_gate", out_dtype=F32, tm=512, tn=1024, tk=2048)

    dx3, dpre, dpp, loss_part, d_final_g, d_b_gate = head_and_loss(
        x2, gpre, w["b_ple_gate"], pp, row(w["final_norm_g"]), tgt, name="head_and_loss")
    x_pos, y_pos, c_pos = _position()
    where = jnp.stack([4 * x_pos + 2 * y_pos + c_pos, 2 * x_pos + y_pos]).astype(jnp.int32)
    core = jnp.reshape(c_pos, (1,)).astype(jnp.int32)
    own, others = {}, {}

    def blocks(named):
        g8 = {n: t.reshape((N_DEV,) + shard2d[n].shape) for n, t in named.items()}
        return g8, sibling_exchange_background(list(g8.values()))

    def chip_sums(g8, theirs):
        own.update(zip(g8, zip(g8.values(), theirs)))
        return [add_pairs(g, t, core, name="chip_sum_" + n) for (n, g), t in zip(g8.items(), theirs)]

    d_w_gate = mm_tn(h3, dpre, name="d_w_ple_gate", out_dtype=BF16, tm=L, tko=1024, tno=1024)
    d_w_ple = mm_tn(bf(p2d), dpp, name="d_w_ple_proj", out_dtype=BF16, tm=L, tko=1024, tno=1024, out_blocks=N_DEV)
    g8_ple, bg = blocks({"w_ple_gate": d_w_gate, "w_ple_proj": d_w_ple})
    dh3, theirs = mm_nt(dpre, w_gate, name="d_h_ple", out_dtype=F32, tm=512, tko=1024, tc=2048, bg=bg)
    bg = chip_exchange_background(chip_sums(g8_ple, theirs))
    dx2, dx2b, d_ple_g = norm_bwd(dh3, x2, w["norm_ple_g"], dx3, name="d_norm_ple", want_bf16=True)
    d_w_ffn_out, got = mm_tn(act, dx2b, name="d_w_ffn_out", out_dtype=BF16, tm=L, tko=1408, tno=512, bg=bg)
    others.update(zip(g8_ple, got))
    g8_fo, bg = blocks({"w_ffn_out": d_w_ffn_out})
    dgate, dup = ffn_out_bwd_swiglu(dx2b, w_ffn_out, gate_ff, up_ff, name="d_act_swiglu")
    half = N_DEV // 2
    d_w_ffn_in, theirs = mm_tn(h2, dgate, name="d_w_ffn_in_gate", out_dtype=BF16, tm=L, tko=512, tno=1408,
                               out_blocks=half, total_blocks=N_DEV, bg=bg)
    bg = chip_exchange_background(chip_sums(g8_fo, theirs))
    d_w_ffn_in, got = mm_tn(h2, dup, name="d_w_ffn_in_up", out_dtype=BF16, tm=L, tko=512, tno=1408,
                            out_blocks=half, block_offset=half, total_blocks=N_DEV, into=d_w_ffn_in, bg=bg)
    others.update(zip(g8_fo, got))
    g8_fi, bg = blocks({"w_ffn_in": d_w_ffn_in})
    w_ffn_in = jnp.transpose(w_ffn_in_blk, (1, 0, 2)).reshape(D, 2 * F)
    dh2, theirs = mm_nt(dgate, w_ffn_in, a2=dup, name="d_h_ffn", out_dtype=F32, tm=256, tko=1024, tc=F, bg=bg)
    late_parts = chip_sums(g8_fi, theirs)
    dx1, dx1b, d_ffn_g = norm_bwd(dh2, x1, w["norm_ffn_g"], dx2, name="d_norm_ffn", want_bf16=True)
    dycat = mm_nt(dx1b, w_out, name="d_ycat", out_dtype=F32, tm=512, tko=1024, tc=2048)
    d_w_out = mm_tn(ycat, dx1b, name="d_w_out", out_dtype=BF16, tm=L, tko=1024, tno=1024)
    g8_out, bg = blocks({"w_out": d_w_out})
    dzu, dzv, d_sgu_w, d_sgu_b, d_ln_g, d_ln_b, d_g_b = sgu_bwd(
        z, dycat, w["sgu_ln_g"], w["sgu_ln_b"], w["sgu_w"][0], b_s3, w["out_norm_sgu_g"], name="d_sgu", d_sgu=d_sgu)
    dt_glu, dd_direct, d_g_a, d_glu_b = glu_post_bwd(
        y0, t_glu, w["ssm_glu_b"], w["out_norm_ssm_g"], dycat, name="d_s5_glu_norm")
    d_w_glu, theirs = mm_tn(ya1, dt_glu, name="d_w_glu", out_dtype=BF16, tm=L, tko=1024, tno=1024, bg=bg)
    late_parts += chip_sums(g8_out, theirs)
    g8_glu, bg = blocks({"ssm_glu_w": d_w_glu})
    dd_mm, theirs = mm_nt(dt_glu, w_glu, name="d_s5_glu_proj", out_dtype=F32, tm=512, tko=1024, tc=2048, bg=bg)
    late_parts += chip_sums(g8_glu, theirs)
    (du, d_bbar_re, d_bbar_im, d_ct_re, d_ct_im, d_a_re, d_a_im, d_d), got = ssm_bwd(
        z, y0, dd_direct, dd_mm, s_re, s_im, *s5_mats, name="d_s5_scan", bg=chip_exchange_background(late_parts))
    others.update(zip(("w_ffn_in", "w_out", "ssm_glu_w"), got))
    dz = jnp.concatenate([du, dzu, dzv], axis=1)

    d_q_re, d_q_im, d_bt_re, d_bt_im = disc_b_bwd(row(q_re), row(q_im), bt_re, bt_im, d_bbar_re, d_bbar_im,
                                                  name="d_s5_discretise_b")
    d_lam_re, d_lam_im, d_log_step = disc_lambda_bwd(
        lam_re, lam_im, log_step,
        (d_a_re.reshape(G, P), d_a_im.reshape(G, P), d_q_re.reshape(G, P), d_q_im.reshape(G, P)),
        name="d_s5_discretise_lambda")
    small_grads = {
        "ssm_lambda_re": d_lam_re, "ssm_lambda_im": d_lam_im, "ssm_log_step": d_log_step,
        "ssm_b_re": d_bt_re, "ssm_b_im": d_bt_im, "ssm_c_re": d_ct_re, "ssm_c_im": d_ct_im,
        "ssm_d": d_d, "ssm_glu_b": d_glu_b, "sgu_ln_g": d_ln_g, "sgu_ln_b": d_ln_b,
        "sgu_w": d_sgu_w, "sgu_b": d_sgu_b, "out_norm_ssm_g": d_g_a, "out_norm_sgu_g": d_g_b,
        "norm_ffn_g": d_ffn_g, "norm_ple_g": d_ple_g, "b_ple_gate": d_b_gate, "final_norm_g": d_final_g,
    }

    d_w_in, got = mm_tn(h1, dz, name="d_w_in", out_dtype=BF16, tm=L, tko=1024, tno=1024, out_blocks=N_DEV,
                        bg=gather_background([loss_part] + [small_grads[n] for n in SMALL]))
    sums = sum_gathered(got, name="sum_small_grads")
    g8_in, _ = blocks({"w_in": d_w_in})
    theirs = sibling_exchange(list(g8_in.values()), name="grads_to_sibling_w_in")
    dh1, got = mm_nt(dz, w_in, name="d_h_mix", out_dtype=F32, tm=512, tko=1024, tc=3 * d_sgu,
                     bg=chip_exchange_background(chip_sums(g8_in, theirs)))
    others.update(zip(g8_in, got))
    grad_x, d_mix_g = norm_bwd(dh1, x2d, w["norm_mix_g"], dx1, name="d_norm_mix", want_bf16=False)
    loss, small_sum = sums[0][0, 0], dict(zip(SMALL, sums[1:]))
    (small_sum["norm_mix_g"],) = sum_gathered(all_gather_blocks([d_mix_g], name="gather_last_grad"),
                                              name="sum_last_grad")

    out = {}
    for n in SHARDED:
        res = adamw_sharded(shard2d[n], m[n].reshape(shard2d[n].shape), v[n].reshape(shard2d[n].shape),
                            own[n][0], own[n][1], others[n], where, name="adamw_" + n)
        out[n] = [r.reshape(w[n].shape) for r in res]

    def work_shape(n):
        s = w[n].shape
        return (1,) + s if len(s) == 1 else (s if len(s) == 2 else s[1:])

    for n in ("ssm_b_re", "ssm_b_im"):
        small_sum[n] = small_sum[n].reshape(H, G, P).transpose(1, 2, 0)
    for n in ("ssm_c_re", "ssm_c_im"):
        small_sum[n] = small_sum[n].reshape(H, G, P).transpose(1, 0, 2)

    def replicated(names_, name):
        gs = [small_sum[n].reshape(work_shape(n)) for n in names_]
        res = adamw_replicated(*[[t[n].reshape(work_shape(n)) for n in names_] for t in (w, m, v)], gs, name=name)
        for i, n in enumerate(names_):
            out[n] = [r.reshape(w[n].shape) for r in (gs[i], res[0][i], res[1][i], res[2][i])]

    replicated([n for n in SMALL + SMALL_LAST if n not in SMALL_WIDE], "adamw_small")
    replicated(list(SMALL_WIDE), "adamw_s5_b_c")

    grads = [out[n][0] for n in WEIGHTS]
    deltas = [out[n][1] for n in WEIGHTS]
    new_m = [out[n][2] for n in WEIGHTS]
    new_v = [out[n][3] for n in WEIGHTS]
    return (loss, grad_x.reshape(x.shape), *grads, *deltas, *new_m, *new_v)


def kernel(x, p, norm_mix_g, w_in, ssm_lambda_re, ssm_lambda_im, ssm_log_step, ssm_b_re, ssm_b_im, ssm_c_re, ssm_c_im, ssm_d, ssm_glu_w, ssm_glu_b, sgu_ln_g, sgu_ln_b, sgu_w, sgu_b, out_norm_ssm_g, out_norm_sgu_g, w_out, norm_ffn_g, w_ffn_in, w_ffn_out, norm_ple_g, w_ple_gate, b_ple_gate, w_ple_proj, final_norm_g, loss_target, m_norm_mix_g, m_w_in, m_ssm_lambda_re, m_ssm_lambda_im, m_ssm_log_step, m_ssm_b_re, m_ssm_b_im, m_ssm_c_re, m_ssm_c_im, m_ssm_d, m_ssm_glu_w, m_ssm_glu_b, m_sgu_ln_g, m_sgu_ln_b, m_sgu_w, m_sgu_b, m_out_norm_ssm_g, m_out_norm_sgu_g, m_w_out, m_norm_ffn_g, m_w_ffn_in, m_w_ffn_out, m_norm_ple_g, m_w_ple_gate, m_b_ple_gate, m_w_ple_proj, m_final_norm_g, v_norm_mix_g, v_w_in, v_ssm_lambda_re, v_ssm_lambda_im, v_ssm_log_step, v_ssm_b_re, v_ssm_b_im, v_ssm_c_re, v_ssm_c_im, v_ssm_d, v_ssm_glu_w, v_ssm_glu_b, v_sgu_ln_g, v_sgu_ln_b, v_sgu_w, v_sgu_b, v_out_norm_ssm_g, v_out_norm_sgu_g, v_w_out, v_norm_ffn_g, v_w_ffn_in, v_w_ffn_out, v_norm_ple_g, v_w_ple_gate, v_b_ple_gate, v_w_ple_proj, v_final_norm_g):
    given = dict(locals())
    w = {n: given[n] for n in WEIGHTS}
    m = {n: given["m_" + n] for n in WEIGHTS}
    v = {n: given["v_" + n] for n in WEIGHTS}
    return _step(x, p, loss_target, w, m, v)
```

```python
import functools
import math

import jax
import jax.numpy as jnp
from jax import lax
from jax.experimental import pallas as pl
from jax.experimental.pallas import tpu as pltpu

F32 = jnp.float32
BF16 = jnp.bfloat16
MESH = pl.DeviceIdType.MESH
ANY = pl.BlockSpec(memory_space=pl.ANY)

N_DEV = 8
EPS = 1e-6
LAMBDA_RE_MAX = -1e-4
SSM_GROUP = 16
SSM_STATE = 64
SSM_SUPER = 16
SGU_CHUNK = 128
ADAM_LR, ADAM_B1, ADAM_B2, ADAM_EPS, ADAM_WD, ADAM_STEP = 0.001, 0.9, 0.999, 1e-08, 0.01, 10
VMEM_LIMIT = 52 * 1024 * 1024
LANE = 128
GATHER_FORWARD_AT = 0.85

_GELU_C = math.sqrt(2.0 / math.pi)


def _params(sem=None):
    return pltpu.CompilerParams(dimension_semantics=sem, vmem_limit_bytes=VMEM_LIMIT)


def _tile(dim, pref, unit=LANE):
    if dim <= pref:
        return dim
    t = (pref // unit) * unit
    while t >= unit:
        if dim % t == 0:
            return t
        t -= unit
    return dim


def _gelu(x):
    return 0.5 * x * (1.0 + jnp.tanh(_GELU_C * (x + 0.044715 * x * x * x)))


def _gelu_grad(x):
    t = jnp.tanh(_GELU_C * (x + 0.044715 * x * x * x))
    return 0.5 * (1.0 + t) + 0.5 * x * (1.0 - t * t) * (_GELU_C * (1.0 + 3.0 * 0.044715 * x * x))


def _gelu_and_grad(x):
    t = jnp.tanh(_GELU_C * (x + 0.044715 * x * x * x))
    return (0.5 * x * (1.0 + t),
            0.5 * (1.0 + t) + 0.5 * x * (1.0 - t * t) * (_GELU_C * (1.0 + 3.0 * 0.044715 * x * x)))


def _rms(x):
    return lax.rsqrt(jnp.mean(x * x, axis=-1, keepdims=True) + EPS)


def _rmsnorm_bwd(dy, x, r, g):
    dyg = dy * g
    return r * dyg - x * (r * r * r) * jnp.mean(dyg * x, axis=-1, keepdims=True)


def _rowsum(v):
    return jnp.sum(v, axis=0, keepdims=True)


class Background:
    def __init__(self, inputs, out_shapes, scratch, phases):
        self.inputs, self.out_shapes, self.scratch, self.phases = list(inputs), list(out_shapes), list(scratch), phases
        self.aliases = {}

    def emit(self, step, nsteps, ins, outs, scratch):
        for place, phase in self.phases:
            at = min(int(place * nsteps), nsteps - 1)

            @pl.when(step == at)
            def _():
                phase(ins, outs, scratch)


def _carrier(bg, n_in, n_out, n_scratch, grid):
    nbi = len(bg.inputs) if bg else 0
    nbo = len(bg.out_shapes) if bg else 0
    nsteps = math.prod(grid)

    def split(refs):
        ins = refs[:n_in]
        bg_ins = refs[n_in:n_in + nbi]
        outs = refs[n_in + nbi:n_in + nbi + n_out]
        bg_outs = refs[n_in + nbi + n_out:n_in + nbi + n_out + nbo]
        rest = refs[n_in + nbi + n_out + nbo:]
        scratch, bg_scratch = rest[:n_scratch], rest[n_scratch:]

        def run_background():
            if bg is None:
                return
            step = pl.program_id(0)
            for axis in range(1, len(grid)):
                step = step * grid[axis] + pl.program_id(axis)
            bg.emit(step, nsteps, bg_ins, bg_outs, bg_scratch)

        return ins, outs, scratch, run_background

    if bg is None:
        return [], [], [], [], [], split
    return [ANY] * nbi, list(bg.inputs), [ANY] * nbo, list(bg.out_shapes), list(bg.scratch), split


def _semantics(bg, sem):
    return tuple("arbitrary" for _ in sem) if bg is not None else sem


def _results(res, n_out, bg):
    res = list(res) if isinstance(res, (list, tuple)) else [res]
    own = res[0] if n_out == 1 else res[:n_out]
    return (own, res[n_out:]) if bg is not None else own


def mm_nn(a, b, *, name, out_dtype, tm, tn, tk, residual=None, bg=None):
    M, K = a.shape
    blocked = b.ndim == 3
    if blocked:
        nb, _, Nb = b.shape
        N = nb * Nb
        tn = _tile(Nb, tn)
        per = Nb // tn
    else:
        N = b.shape[1]
        tn = _tile(N, tn)
    tm, tk = _tile(M, tm, 8), _tile(K, tk)
    nj, ni, nk = N // tn, M // tm, K // tk
    has_res = residual is not None
    grid = (nj, ni, nk)
    bg_in_specs, bg_args, bg_out_specs, bg_out_shapes, bg_scratch, split = _carrier(
        bg, 3 if has_res else 2, 1, 0 if nk == 1 else 1, grid)

    def body(*refs):
        ins, (o_ref,), scratch, run_background = split(refs)
        run_background()
        a_ref, b_ref = ins[0], ins[1]
        r_ref = ins[2] if has_res else None

        def finish(acc):
            if has_res:
                acc = acc + r_ref[...]
            o_ref[...] = acc.astype(o_ref.dtype)

        part = jnp.dot(a_ref[...], b_ref[...], preferred_element_type=F32)
        if nk == 1:
            finish(part)
        else:
            acc_ref = scratch[0]
            k = pl.program_id(2)

            @pl.when(k == 0)
            def _():
                acc_ref[...] = part

            @pl.when(k > 0)
            def _():
                acc_ref[...] += part

            @pl.when(k == nk - 1)
            def _():
                finish(acc_ref[...])

    if blocked:
        b_spec = pl.BlockSpec((None, tk, tn), lambda j, i, k: (j // per, k, j % per))
    else:
        b_spec = pl.BlockSpec((tk, tn), lambda j, i, k: (k, j))
    in_specs = [pl.BlockSpec((tm, tk), lambda j, i, k: (i, k)), b_spec]
    args = [a, b]
    if has_res:
        in_specs.append(pl.BlockSpec((tm, tn), lambda j, i, k: (i, j)))
        args.append(residual)
    res = pl.pallas_call(
        body, name=name, grid=grid,
        in_specs=in_specs + bg_in_specs,
        out_specs=[pl.BlockSpec((tm, tn), lambda j, i, k: (i, j))] + bg_out_specs,
        out_shape=[jax.ShapeDtypeStruct((M, N), out_dtype)] + bg_out_shapes,
        input_output_aliases={len(args) + k: 1 + o for k, o in (bg.aliases if bg else {}).items()},
        scratch_shapes=([] if nk == 1 else [pltpu.VMEM((tm, tn), F32)]) + bg_scratch,
        compiler_params=_params(_semantics(bg, ("parallel", "parallel", "arbitrary"))),
    )(*args, *bg_args)
    return _results(res, 1, bg)


def mm_nt(a, w, *, name, out_dtype, tm, tko, tc, a2=None, bg=None):
    M, N = a.shape
    if a2 is not None:
        N = 2 * N
    blocked = w.ndim == 3
    if blocked:
        nb, Ko, Nb = w.shape
        tc = _tile(Nb, tc)
        per = Nb // tc
    else:
        Ko = w.shape[0]
        tc = _tile(N, tc)
    tm, tko = _tile(M, tm, 8), _tile(Ko, tko)
    njo, ni, nc = Ko // tko, M // tm, N // tc
    grid = (njo, ni, nc)
    half = nc // 2
    bg_in_specs, bg_args, bg_out_specs, bg_out_shapes, bg_scratch, split = _carrier(
        bg, 2 if a2 is None else 3, 1, 0 if nc == 1 else 1, grid)

    def body(*refs):
        ins, (o_ref,), scratch, run_background = split(refs)
        run_background()
        a_val = ins[0][...]
        if a2 is not None:
            a_val = jnp.where(pl.program_id(2) < half, a_val, ins[1][...])
        part = lax.dot_general(a_val, ins[-1][...], (((1,), (1,)), ((), ())),
                               preferred_element_type=F32)
        if nc == 1:
            o_ref[...] = part.astype(o_ref.dtype)
        else:
            acc_ref = scratch[0]
            c = pl.program_id(2)

            @pl.when(c == 0)
            def _():
                acc_ref[...] = part

            @pl.when(c > 0)
            def _():
                acc_ref[...] += part

            @pl.when(c == nc - 1)
            def _():
                o_ref[...] = acc_ref[...].astype(o_ref.dtype)

    if blocked:
        w_spec = pl.BlockSpec((None, tko, tc), lambda j, i, c: (c // per, j, c % per))
    else:
        w_spec = pl.BlockSpec((tko, tc), lambda j, i, c: (j, c))
    if a2 is None:
        a_specs, a_args = [pl.BlockSpec((tm, tc), lambda j, i, c: (i, c))], [a]
    else:
        a_specs = [pl.BlockSpec((tm, tc), lambda j, i, c: (i, jnp.minimum(c, half - 1))),
                   pl.BlockSpec((tm, tc), lambda j, i, c: (i, jnp.maximum(c - half, 0)))]
        a_args = [a, a2]
    res = pl.pallas_call(
        body, name=name, grid=grid,
        in_specs=a_specs + [w_spec] + bg_in_specs,
        out_specs=[pl.BlockSpec((tm, tko), lambda j, i, c: (i, j))] + bg_out_specs,
        out_shape=[jax.ShapeDtypeStruct((M, Ko), out_dtype)] + bg_out_shapes,
        scratch_shapes=([] if nc == 1 else [pltpu.VMEM((tm, tko), F32)]) + bg_scratch,
        compiler_params=_params(_semantics(bg, ("parallel", "parallel", "arbitrary"))),
    )(*a_args, w, *bg_args)
    return _results(res, 1, bg)


def mm_tn(a, g, *, name, out_dtype, tm, tko, tno, out_blocks=None, block_offset=0, total_blocks=None, into=None,
          bg=None):
    M, K = a.shape
    N = g.shape[1]
    if out_blocks:
        Nb = N // out_blocks
        tno = _tile(Nb, tno)
        per = Nb // tno
    else:
        tno = _tile(N, tno)
    tm, tko = _tile(M, tm), _tile(K, tko)
    njo, njn, nm = K // tko, N // tno, M // tm
    grid = (njo, njn, nm)
    bg_in_specs, bg_args, bg_out_specs, bg_out_shapes, bg_scratch, split = _carrier(
        bg, 2 if into is None else 3, 1, 0 if nm == 1 else 1, grid)

    def body(*refs):
        ins, (o_ref,), scratch, run_background = split(refs)
        a_ref, g_ref = ins[0], ins[1]
        run_background()
        part = lax.dot_general(a_ref[...], g_ref[...], (((0,), (0,)), ((), ())),
                               preferred_element_type=F32)
        if nm == 1:
            o_ref[...] = part.astype(o_ref.dtype)
        else:
            acc_ref = scratch[0]
            m = pl.program_id(2)

            @pl.when(m == 0)
            def _():
                acc_ref[...] = part

            @pl.when(m > 0)
            def _():
                acc_ref[...] += part

            @pl.when(m == nm - 1)
            def _():
                o_ref[...] = acc_ref[...].astype(o_ref.dtype)

    if out_blocks:
        o_spec = pl.BlockSpec((None, tko, tno), lambda jo, jn, m: (jn // per + block_offset, jo, jn % per))
        o_shape = jax.ShapeDtypeStruct((total_blocks or out_blocks, K, Nb), out_dtype)
    else:
        o_spec = pl.BlockSpec((tko, tno), lambda jo, jn, m: (jo, jn))
        o_shape = jax.ShapeDtypeStruct((K, N), out_dtype)
    res = pl.pallas_call(
        body, name=name, grid=grid,
        in_specs=[pl.BlockSpec((tm, tko), lambda jo, jn, m: (m, jo)),
                  pl.BlockSpec((tm, tno), lambda jo, jn, m: (m, jn))] + ([] if into is None else [ANY]) + bg_in_specs,
        out_specs=[o_spec] + bg_out_specs, out_shape=[o_shape] + bg_out_shapes,
        scratch_shapes=([] if nm == 1 else [pltpu.VMEM((tko, tno), F32)]) + bg_scratch,
        input_output_aliases={} if into is None else {2: 0},
        compiler_params=_params(_semantics(bg, ("parallel", "parallel", "arbitrary"))),
    )(a, g, *([] if into is None else [into]), *bg_args)
    return _results(res, 1, bg)


def ffn_in_swiglu(h, w_blk, *, name, tm, bg=None):
    M, K = h.shape
    nb, _, Nb = w_blk.shape
    nh = nb // 2
    F = nh * Nb
    tm = _tile(M, tm, 8)
    grid = (nh, M // tm)
    bg_in_specs, bg_args, bg_out_specs, bg_out_shapes, bg_scratch, split = _carrier(bg, 3, 3, 0, grid)

    def body(*refs):
        (h_ref, wg_ref, wu_ref), (act_ref, gate_ref, up_ref), _, run_background = split(refs)
        run_background()
        hv = h_ref[...]
        gate = jnp.dot(hv, wg_ref[...], preferred_element_type=F32)
        up = jnp.dot(hv, wu_ref[...], preferred_element_type=F32)
        gate_ref[...] = gate
        up_ref[...] = up
        act_ref[...] = (gate * jax.nn.sigmoid(gate) * up).astype(act_ref.dtype)

    o_spec = pl.BlockSpec((tm, Nb), lambda j, i: (i, j))
    res = pl.pallas_call(
        body, name=name, grid=grid,
        in_specs=[pl.BlockSpec((tm, K), lambda j, i: (i, 0)),
                  pl.BlockSpec((None, K, Nb), lambda j, i: (j, 0, 0)),
                  pl.BlockSpec((None, K, Nb), lambda j, i: (j + nh, 0, 0))] + bg_in_specs,
        out_specs=[o_spec, o_spec, o_spec] + bg_out_specs,
        out_shape=[jax.ShapeDtypeStruct((M, F), BF16), jax.ShapeDtypeStruct((M, F), F32),
                   jax.ShapeDtypeStruct((M, F), F32)] + bg_out_shapes,
        scratch_shapes=bg_scratch,
        compiler_params=_params(_semantics(bg, ("parallel", "parallel"))),
    )(h, w_blk, w_blk, *bg_args)
    return _results(res, 3, bg)


def _row_spec(tm, d, col=0):
    return pl.BlockSpec((tm, d), lambda i: (i, col))


def _vec_spec(d):
    return pl.BlockSpec((1, d), lambda i: (0, 0))


def norm_fwd(x, g, *, name, tm=512):
    L, D = x.shape
    tm = _tile(L, tm, 8)

    def body(x_ref, g_ref, h_ref):
        xv = x_ref[...]
        h_ref[...] = (xv * _rms(xv) * g_ref[...]).astype(h_ref.dtype)

    return pl.pallas_call(
        body, name=name, grid=(L // tm,),
        in_specs=[_row_spec(tm, D), _vec_spec(D)],
        out_specs=_row_spec(tm, D),
        out_shape=jax.ShapeDtypeStruct((L, D), BF16),
        compiler_params=_params(("parallel",)),
    )(x, g)


def norm_bwd(dh, xin, g, dres, *, name, want_bf16, tm=256):
    L, D = xin.shape
    tm = _tile(L, tm, 8)

    def body(dh_ref, x_ref, g_ref, dres_ref, dx_ref, *rest):
        dg_ref = rest[-1]
        xv, dhv = x_ref[...], dh_ref[...]
        r = _rms(xv)
        dx = dres_ref[...] + _rmsnorm_bwd(dhv, xv, r, g_ref[...])
        dx_ref[...] = dx
        if want_bf16:
            rest[0][...] = dx.astype(BF16)
        part = _rowsum(dhv * xv * r)

        @pl.when(pl.program_id(0) == 0)
        def _():
            dg_ref[...] = part

        @pl.when(pl.program_id(0) > 0)
        def _():
            dg_ref[...] += part

    out_specs = [_row_spec(tm, D)] + ([_row_spec(tm, D)] if want_bf16 else []) + [_vec_spec(D)]
    out_shape = ([jax.ShapeDtypeStruct((L, D), F32)]
                 + ([jax.ShapeDtypeStruct((L, D), BF16)] if want_bf16 else [])
                 + [jax.ShapeDtypeStruct((1, D), F32)])
    return pl.pallas_call(
        body, name=name, grid=(L // tm,),
        in_specs=[_row_spec(tm, D), _row_spec(tm, D), _vec_spec(D), _row_spec(tm, D)],
        out_specs=out_specs, out_shape=out_shape,
        compiler_params=_params(("arbitrary",)),
    )(dh, xin, g, dres)


def glu_pre(y0, *, name, tm=256):
    L, D = y0.shape
    tm = _tile(L, tm, 8)

    def body(y_ref, o_ref):
        o_ref[...] = _gelu(y_ref[...]).astype(o_ref.dtype)

    return pl.pallas_call(
        body, name=name, grid=(L // tm,),
        in_specs=[_row_spec(tm, D)], out_specs=_row_spec(tm, D),
        out_shape=jax.ShapeDtypeStruct((L, D), BF16),
        compiler_params=_params(("parallel",)),
    )(y0)


def glu_post(y0, t, b_glu, g_a, *, name, tm=256):
    L, D = y0.shape
    tm = _tile(L, tm, 8)

    def body(y_ref, t_ref, b_ref, g_ref, o_ref):
        ya = _gelu(y_ref[...]) * jax.nn.sigmoid(t_ref[...] + b_ref[...])
        o_ref[...] = (ya * _rms(ya) * g_ref[...]).astype(o_ref.dtype)

    return pl.pallas_call(
        body, name=name, grid=(L // tm,),
        in_specs=[_row_spec(tm, D), _row_spec(tm, D), _vec_spec(D), _vec_spec(D)],
        out_specs=_row_spec(tm, D),
        out_shape=jax.ShapeDtypeStruct((L, D), BF16),
        compiler_params=_params(("parallel",)),
    )(y0, t, b_glu, g_a)


def glu_post_bwd(y0, t, b_glu, g_a, dycat, *, name, tm=256):
    L, D = y0.shape
    tm = _tile(L, tm, 8)

    def body(y_ref, t_ref, b_ref, g_ref, dn_ref, dt_ref, dd_ref, dga_ref, dbg_ref):
        ya1 = _gelu(y_ref[...])
        sg = jax.nn.sigmoid(t_ref[...] + b_ref[...])
        ya = ya1 * sg
        ra = _rms(ya)
        dn = dn_ref[...]
        dya = _rmsnorm_bwd(dn, ya, ra, g_ref[...])
        dt = dya * ya1 * sg * (1.0 - sg)
        dt_ref[...] = dt.astype(BF16)
        dd_ref[...] = dya * sg
        p_ga, p_bg = _rowsum(dn * ya * ra), _rowsum(dt)

        @pl.when(pl.program_id(0) == 0)
        def _():
            dga_ref[...] = p_ga
            dbg_ref[...] = p_bg

        @pl.when(pl.program_id(0) > 0)
        def _():
            dga_ref[...] += p_ga
            dbg_ref[...] += p_bg

    return pl.pallas_call(
        body, name=name, grid=(L // tm,),
        in_specs=[_row_spec(tm, D), _row_spec(tm, D), _vec_spec(D), _vec_spec(D), _row_spec(tm, D, 0)],
        out_specs=[_row_spec(tm, D), _row_spec(tm, D), _vec_spec(D), _vec_spec(D)],
        out_shape=[jax.ShapeDtypeStruct((L, D), BF16), jax.ShapeDtypeStruct((L, D), F32),
                   jax.ShapeDtypeStruct((1, D), F32), jax.ShapeDtypeStruct((1, D), F32)],
        compiler_params=_params(("arbitrary",)),
    )(y0, t, b_glu, g_a, dycat)


def head_and_loss(x2, gpre, b_g, pp, g_f, tgt, *, name, tm=256):
    L, D = x2.shape
    tm = _tile(L, tm, 8)

    def body(x2_ref, gp_ref, bg_ref, pp_ref, gf_ref, tg_ref,
             dx3_ref, dpre_ref, dpp_ref, loss_ref, dgf_ref, dbg_ref):
        gate = jax.nn.sigmoid(gp_ref[...] + bg_ref[...])
        ppv = pp_ref[...]
        x3 = x2_ref[...] + gate * ppv
        r = _rms(x3)
        xn = x3 * r
        gf = gf_ref[...]
        err = xn * gf - tg_ref[...]
        loss = 0.5 * jnp.sum(jnp.mean(err * err, axis=-1, keepdims=True), axis=0, keepdims=True)
        dout = err * (1.0 / D)
        dx3 = _rmsnorm_bwd(dout, x3, r, gf)
        dx3_ref[...] = dx3
        dpre = dx3 * ppv * gate * (1.0 - gate)
        dpre_ref[...] = dpre.astype(BF16)
        dpp_ref[...] = (dx3 * gate).astype(BF16)
        p_gf, p_bg = _rowsum(dout * xn), _rowsum(dpre)
        p_loss = jnp.broadcast_to(loss, loss_ref.shape)

        @pl.when(pl.program_id(0) == 0)
        def _():
            loss_ref[...] = p_loss
            dgf_ref[...] = p_gf
            dbg_ref[...] = p_bg

        @pl.when(pl.program_id(0) > 0)
        def _():
            loss_ref[...] += p_loss
            dgf_ref[...] += p_gf
            dbg_ref[...] += p_bg

    rs = _row_spec(tm, D)
    return pl.pallas_call(
        body, name=name, grid=(L // tm,),
        in_specs=[rs, rs, _vec_spec(D), rs, _vec_spec(D), rs],
        out_specs=[rs, rs, rs, pl.BlockSpec((8, LANE), lambda i: (0, 0)), _vec_spec(D), _vec_spec(D)],
        out_shape=[jax.ShapeDtypeStruct((L, D), F32), jax.ShapeDtypeStruct((L, D), BF16),
                   jax.ShapeDtypeStruct((L, D), BF16), jax.ShapeDtypeStruct((8, LANE), F32),
                   jax.ShapeDtypeStruct((1, D), F32), jax.ShapeDtypeStruct((1, D), F32)],
        compiler_params=_params(("arbitrary",)),
    )(x2, gpre, b_g, pp, g_f, tgt)


def ffn_out_bwd_swiglu(dx, w, gate, up, *, name, tm=512, tf=1408):
    M, D = dx.shape
    F = w.shape[0]
    tm, tf = _tile(M, tm, 8), _tile(F, tf)

    def body(dx_ref, w_ref, g_ref, u_ref, dg_ref, du_ref):
        da = lax.dot_general(dx_ref[...], w_ref[...], (((1,), (1,)), ((), ())), preferred_element_type=F32)
        gv = g_ref[...]
        sg = jax.nn.sigmoid(gv)
        dg_ref[...] = (da * u_ref[...] * sg * (1.0 + gv * (1.0 - sg))).astype(BF16)
        du_ref[...] = (da * gv * sg).astype(BF16)

    spec = pl.BlockSpec((tm, tf), lambda j, i: (i, j))
    return pl.pallas_call(
        body, name=name, grid=(F // tf, M // tm),
        in_specs=[pl.BlockSpec((tm, D), lambda j, i: (i, 0)), pl.BlockSpec((tf, D), lambda j, i: (j, 0)), spec, spec],
        out_specs=[spec, spec],
        out_shape=[jax.ShapeDtypeStruct((M, F), BF16), jax.ShapeDtypeStruct((M, F), BF16)],
        compiler_params=_params(("parallel", "parallel")),
    )(dx, w, gate, up)


def _sgu_forward_values(u1, v1, lng, lnb, w_ref, bs_ref, s_scr, heads, hd):
    xc = v1 - jnp.mean(v1, axis=-1, keepdims=True)
    r = lax.rsqrt(jnp.mean(xc * xc, axis=-1, keepdims=True) + EPS)
    xhat = xc * r
    v2 = xhat * lng + lnb
    tril = (lax.broadcasted_iota(jnp.int32, (SGU_CHUNK, SGU_CHUNK), 0)
            >= lax.broadcasted_iota(jnp.int32, (SGU_CHUNK, SGU_CHUNK), 1))
    for h in range(heads):
        wm = jnp.where(tril, w_ref[h], 0.0).astype(BF16)
        cols = slice(h * hd, (h + 1) * hd)
        s_scr[:, cols] = jnp.dot(wm, v2[:, cols].astype(BF16), preferred_element_type=F32) + bs_ref[h]
    return xhat, r, v2, tril


def sgu_fwd(z, ln_g, ln_b, w_s, b_s, g_b, *, name, d_sgu):
    L = z.shape[0]
    heads = w_s.shape[0]
    hd = d_sgu // heads

    def body(zu_ref, zv_ref, lng_ref, lnb_ref, w_ref, bs_ref, gb_ref, o_ref, s_scr):
        u1 = _gelu(zu_ref[...])
        _sgu_forward_values(u1, _gelu(zv_ref[...]), lng_ref[...], lnb_ref[...], w_ref, bs_ref, s_scr, heads, hd)
        yb = u1 * s_scr[...]
        o_ref[...] = (yb * _rms(yb) * gb_ref[...]).astype(o_ref.dtype)

    blk = lambda col: pl.BlockSpec((SGU_CHUNK, d_sgu), lambda n: (n, col))
    return pl.pallas_call(
        body, name=name, grid=(L // SGU_CHUNK,),
        in_specs=[blk(1), blk(2), _vec_spec(d_sgu), _vec_spec(d_sgu),
                  pl.BlockSpec(w_s.shape, lambda n: (0, 0, 0)), pl.BlockSpec(b_s.shape, lambda n: (0, 0, 0)),
                  _vec_spec(d_sgu)],
        out_specs=blk(0),
        out_shape=jax.ShapeDtypeStruct((L, d_sgu), BF16),
        scratch_shapes=[pltpu.VMEM((SGU_CHUNK, d_sgu), F32)],
        compiler_params=_params(("parallel",)),
    )(z, z, ln_g, ln_b, w_s, b_s, g_b)


def sgu_bwd(z, dycat, ln_g, ln_b, w_s, b_s, g_b, *, name, d_sgu):
    L = z.shape[0]
    heads = w_s.shape[0]
    hd = d_sgu // heads

    def body(zu_ref, zv_ref, dn_ref, lng_ref, lnb_ref, w_ref, bs_ref, gb_ref,
             dzu_ref, dzv_ref, dw_ref, dbs_ref, dlng_ref, dlnb_ref, dgb_ref, s_scr, dv_scr):
        first = pl.program_id(0) == 0
        lng = lng_ref[...]
        u1, du1 = _gelu_and_grad(zu_ref[...])
        v1, dv1_dz = _gelu_and_grad(zv_ref[...])
        xhat, r, v2, tril = _sgu_forward_values(u1, v1, lng, lnb_ref[...], w_ref, bs_ref, s_scr, heads, hd)
        s = s_scr[...]
        yb = u1 * s
        rb = _rms(yb)
        dn = dn_ref[...]
        dyb = _rmsnorm_bwd(dn, yb, rb, gb_ref[...])
        dzu_ref[...] = (dyb * s * du1).astype(BF16)
        ds = dyb * u1
        for h in range(heads):
            cols = slice(h * hd, (h + 1) * hd)
            ds_h = ds[:, cols]
            ds_hb = ds_h.astype(BF16)
            wm = jnp.where(tril, w_ref[h], 0.0).astype(BF16)
            dw_h = jnp.where(tril, lax.dot_general(ds_hb, v2[:, cols].astype(BF16), (((1,), (1,)), ((), ())),
                                                   preferred_element_type=F32), 0.0)
            db_h = jnp.sum(ds_h.T, axis=0, keepdims=True)
            dv_scr[:, cols] = lax.dot_general(wm, ds_hb, (((0,), (0,)), ((), ())), preferred_element_type=F32)

            @pl.when(first)
            def _():
                dw_ref[h] = dw_h
                dbs_ref[h] = db_h

            @pl.when(jnp.logical_not(first))
            def _():
                dw_ref[h] += dw_h
                dbs_ref[h] += db_h

        dv2 = dv_scr[...]
        dxh = dv2 * lng
        dv1 = r * (dxh - jnp.mean(dxh, axis=-1, keepdims=True)
                   - xhat * jnp.mean(dxh * xhat, axis=-1, keepdims=True))
        dzv_ref[...] = (dv1 * dv1_dz).astype(BF16)
        p_lng, p_lnb, p_gb = _rowsum(dv2 * xhat), _rowsum(dv2), _rowsum(dn * yb * rb)

        @pl.when(first)
        def _():
            dlng_ref[...] = p_lng
            dlnb_ref[...] = p_lnb
            dgb_ref[...] = p_gb

        @pl.when(jnp.logical_not(first))
        def _():
            dlng_ref[...] += p_lng
            dlnb_ref[...] += p_lnb
            dgb_ref[...] += p_gb

    blk = lambda col: pl.BlockSpec((SGU_CHUNK, d_sgu), lambda n: (n, col))
    full3 = lambda shape: pl.BlockSpec(shape, lambda n: (0, 0, 0))
    return pl.pallas_call(
        body, name=name, grid=(L // SGU_CHUNK,),
        in_specs=[blk(1), blk(2), blk(1), _vec_spec(d_sgu), _vec_spec(d_sgu),
                  full3(w_s.shape), full3(b_s.shape), _vec_spec(d_sgu)],
        out_specs=[blk(0), blk(0), full3(w_s.shape), full3((heads, 1, SGU_CHUNK)),
                   _vec_spec(d_sgu), _vec_spec(d_sgu), _vec_spec(d_sgu)],
        out_shape=[jax.ShapeDtypeStruct((L, d_sgu), BF16), jax.ShapeDtypeStruct((L, d_sgu), BF16),
                   jax.ShapeDtypeStruct(w_s.shape, F32), jax.ShapeDtypeStruct((heads, 1, SGU_CHUNK), F32),
                   jax.ShapeDtypeStruct((1, d_sgu), F32), jax.ShapeDtypeStruct((1, d_sgu), F32),
                   jax.ShapeDtypeStruct((1, d_sgu), F32)],
        scratch_shapes=[pltpu.VMEM((SGU_CHUNK, d_sgu), F32), pltpu.VMEM((SGU_CHUNK, d_sgu), F32)],
        compiler_params=_params(("arbitrary",)),
    )(z, z, dycat, ln_g, ln_b, w_s, b_s, g_b)


def _disc_lambda(lam_re, lam_im, log_step):
    lr = jnp.minimum(lam_re, LAMBDA_RE_MAX)
    li = lam_im
    dt = jnp.exp(log_step)
    mag = jnp.exp(lr * dt)
    ang = li * dt
    a_re = mag * jnp.cos(ang)
    a_im = mag * jnp.sin(ang)
    nr = a_re - 1.0
    ni = a_im
    den = lr * lr + li * li
    return a_re, a_im, (nr * lr + ni * li) / den, (ni * lr - nr * li) / den


def _disc_b(q_re, q_im, b_re, b_im):
    return q_re * b_re - q_im * b_im, q_re * b_im + q_im * b_re


def disc_lambda_fwd(lam_re, lam_im, log_step, *, name):
    def body(lr_ref, li_ref, ls_ref, ar_ref, ai_ref, qr_ref, qi_ref):
        ar_ref[...], ai_ref[...], qr_ref[...], qi_ref[...] = _disc_lambda(lr_ref[...], li_ref[...], ls_ref[...])

    sd = jax.ShapeDtypeStruct(lam_re.shape, F32)
    return pl.pallas_call(body, name=name, out_shape=[sd, sd, sd, sd], compiler_params=_params())(
        lam_re, lam_im, log_step)


def disc_lambda_bwd(lam_re, lam_im, log_step, cts, *, name):
    def body(lr_ref, li_ref, ls_ref, c0, c1, c2, c3, dlr_ref, dli_ref, dls_ref):
        _, vjp = jax.vjp(_disc_lambda, lr_ref[...], li_ref[...], ls_ref[...])
        dlr_ref[...], dli_ref[...], dls_ref[...] = vjp((c0[...], c1[...], c2[...], c3[...]))

    sd = jax.ShapeDtypeStruct(lam_re.shape, F32)
    return pl.pallas_call(body, name=name, out_shape=[sd, sd, jax.ShapeDtypeStruct(log_step.shape, F32)],
                          compiler_params=_params())(lam_re, lam_im, log_step, *cts)


def disc_b_fwd(q_re, q_im, b_re, b_im, *, name):
    def body(qr_ref, qi_ref, br_ref, bi_ref, or_ref, oi_ref):
        or_ref[...], oi_ref[...] = _disc_b(qr_ref[...], qi_ref[...], br_ref[...], bi_ref[...])

    sd = jax.ShapeDtypeStruct(b_re.shape, F32)
    return pl.pallas_call(body, name=name, out_shape=[sd, sd], compiler_params=_params())(q_re, q_im, b_re, b_im)


def disc_b_bwd(q_re, q_im, b_re, b_im, ct_re, ct_im, *, name):
    def body(qr_ref, qi_ref, br_ref, bi_ref, cr_ref, ci_ref, dqr_ref, dqi_ref, dbr_ref, dbi_ref):
        _, vjp = jax.vjp(_disc_b, qr_ref[...], qi_ref[...], br_ref[...], bi_ref[...])
        dqr_ref[...], dqi_ref[...], dbr_ref[...], dbi_ref[...] = vjp((cr_ref[...], ci_ref[...]))

    sq, sb = jax.ShapeDtypeStruct(q_re.shape, F32), jax.ShapeDtypeStruct(b_re.shape, F32)
    return pl.pallas_call(body, name=name, out_shape=[sq, sq, sb, sb], compiler_params=_params())(
        q_re, q_im, b_re, b_im, ct_re, ct_im)


def _lti_scan(xr, xi, ar, ai, reverse):
    T = xr.shape[0]
    row = lax.broadcasted_iota(jnp.int32, xr.shape, 0)
    k = 1
    while k < T:
        shift = T - k if reverse else k
        keep = (row < T - k) if reverse else (row >= k)
        sr = jnp.where(keep, pltpu.roll(xr, shift, 0), 0.0)
        si = jnp.where(keep, pltpu.roll(xi, shift, 0), 0.0)
        xr, xi = xr + ar * sr - ai * si, xi + ar * si + ai * sr
        ar, ai = ar * ar - ai * ai, 2.0 * ar * ai
        k *= 2
    return xr, xi


SUBLANES = 8


def _scan_rows(x_re, x_im, o_re, o_im, ar, ai, cr, ci, reverse):
    T, n = x_re.shape
    groups = T // SUBLANES
    row = lax.broadcasted_iota(jnp.int32, (SUBLANES, n), 0)
    edge = SUBLANES - 1 if reverse else 0
    pr, pi = _lti_scan(jnp.where(row == edge, ar, 0.0), jnp.where(row == edge, ai, 0.0), ar, ai, reverse)
    pows = []
    for level in range(3):
        k = 1 << level
        keep = (row < SUBLANES - k) if reverse else (row >= k)
        pows.append((jnp.where(keep, ar, 0.0), jnp.where(keep, ai, 0.0)))
        ar, ai = ar * ar - ai * ai, 2.0 * ar * ai

    def group(i, carry):
        cr, ci = carry
        at = pl.multiple_of((groups - 1 - i if reverse else i) * SUBLANES, SUBLANES)
        xr, xi = x_re[pl.ds(at, SUBLANES), :], x_im[pl.ds(at, SUBLANES), :]
        for level, (qr, qi) in enumerate(pows):
            shift = SUBLANES - (1 << level) if reverse else 1 << level
            sr, si = pltpu.roll(xr, shift, 0), pltpu.roll(xi, shift, 0)
            xr, xi = xr + qr * sr - qi * si, xi + qr * si + qi * sr
        xr, xi = xr + pr * cr - pi * ci, xi + pr * ci + pi * cr
        o_re[pl.ds(at, SUBLANES), :] = xr
        o_im[pl.ds(at, SUBLANES), :] = xi
        return spread(xr[last:last + 1, :]), spread(xi[last:last + 1, :])

    last = 0 if reverse else SUBLANES - 1
    spread = lambda v: jnp.broadcast_to(v, (SUBLANES, n))
    cr, ci = lax.fori_loop(0, groups, group, (spread(cr), spread(ci)), unroll=2)
    return cr[0:1, :], ci[0:1, :]


def _ssm_chunk(L):
    return _tile(L, 512, 8)


def _same_group(rows, cols):
    r = lax.broadcasted_iota(jnp.int32, (rows, cols), 0) // SSM_GROUP
    c = lax.broadcasted_iota(jnp.int32, (rows, cols), 1) // SSM_STATE
    return r == c


def _expand_groups(compact):
    H, S = compact.shape
    tiled = jnp.concatenate([compact] * (S // SSM_STATE), axis=0)
    return jnp.where(_same_group(tiled.shape[0], S), tiled, 0.0).astype(BF16)


def _collapse_groups(dense):
    C, S = dense.shape
    masked = jnp.where(_same_group(C, S), dense, 0.0)
    total = masked[0:SSM_GROUP]
    for g in range(1, C // SSM_GROUP):
        total = total + masked[g * SSM_GROUP:(g + 1) * SSM_GROUP]
    return total


def ssm_fwd(z, bt_re, bt_im, ct_re, ct_im, a_re, a_im, d, *, name, bg=None):
    L = z.shape[0]
    NK, _, S = a_re.shape
    H = bt_re.shape[0]
    C = S // SSM_STATE * SSM_GROUP
    T = _ssm_chunk(L)
    grid = (NK, L // T)
    bg_in_specs, bg_args, bg_out_specs, bg_out_shapes, bg_scratch, split = _carrier(bg, 8, 3, 8, grid)
    nt_dot = lambda p, q: lax.dot_general(p, q, (((1,), (1,)), ((), ())), preferred_element_type=F32)

    def body(*refs):
        ((u_ref, btr_ref, bti_ref, ctr_ref, cti_ref, ar_ref, ai_ref, d_ref), (y_ref, sr_ref, si_ref),
         (car_re, car_im, bu_re, bu_im, b_re, b_im, c_re, c_im), run_background) = split(refs)
        run_background()
        i = pl.program_id(1)
        ar, ai = ar_ref[...], ai_ref[...]

        @pl.when(i == 0)
        def _():
            car_re[...] = jnp.zeros_like(car_re)
            car_im[...] = jnp.zeros_like(car_im)
            b_re[...] = _expand_groups(btr_ref[...])
            b_im[...] = _expand_groups(bti_ref[...])
            c_re[...] = _expand_groups(ctr_ref[...])
            c_im[...] = _expand_groups(cti_ref[...])

        u = u_ref[...]
        ub = u.astype(BF16)
        bu_re[...] = jnp.dot(ub, b_re[...], preferred_element_type=F32)
        bu_im[...] = jnp.dot(ub, b_im[...], preferred_element_type=F32)
        car_re[...], car_im[...] = _scan_rows(bu_re, bu_im, sr_ref, si_ref, ar, ai, car_re[...], car_im[...], False)
        y_ref[...] = (nt_dot(sr_ref[...].astype(BF16), c_re[...]) - nt_dot(si_ref[...].astype(BF16), c_im[...])
                      + d_ref[...] * u)

    kspec = lambda shape: pl.BlockSpec((None,) + shape, lambda k, i: (k, 0, 0))
    compact = pl.BlockSpec((H, S), lambda k, i: (0, k))
    res = pl.pallas_call(
        body, name=name, grid=grid,
        in_specs=[pl.BlockSpec((T, C), lambda k, i: (i, k)), compact, compact, compact, compact,
                  kspec((1, S)), kspec((1, S)), kspec((1, C))] + bg_in_specs,
        out_specs=[pl.BlockSpec((T, C), lambda k, i: (i, k)),
                   pl.BlockSpec((T, S), lambda k, i: (i, k)), pl.BlockSpec((T, S), lambda k, i: (i, k))] + bg_out_specs,
        out_shape=[jax.ShapeDtypeStruct((L, NK * C), F32), jax.ShapeDtypeStruct((L, NK * S), F32),
                   jax.ShapeDtypeStruct((L, NK * S), F32)] + bg_out_shapes,
        scratch_shapes=[pltpu.VMEM((1, S), F32), pltpu.VMEM((1, S), F32),
                        pltpu.VMEM((T, S), F32), pltpu.VMEM((T, S), F32)]
        + [pltpu.VMEM((C, S), BF16)] * 4 + bg_scratch,
        compiler_params=_params(_semantics(bg, ("parallel", "arbitrary"))),
    )(z, bt_re, bt_im, ct_re, ct_im, a_re, a_im, d, *bg_args)
    return _results(res, 3, bg)


def ssm_bwd(z, y0, dd_direct, dd_mm, s_re, s_im, bt_re, bt_im, ct_re, ct_im, a_re, a_im, d, *, name, bg=None):
    L = z.shape[0]
    NK, _, S = a_re.shape
    H = bt_re.shape[0]
    C = S // SSM_STATE * SSM_GROUP
    T = _ssm_chunk(L)
    nchunk = L // T
    tail = T // 8
    grid = (NK, nchunk)
    bg_in_specs, bg_args, bg_out_specs, bg_out_shapes, bg_scratch, split = _carrier(bg, 15, 8, 12, grid)
    nt_dot = lambda p, q: lax.dot_general(p, q, (((1,), (1,)), ((), ())), preferred_element_type=F32)

    def body(*refs):
        ((u_ref, y_ref, d1_ref, d2_ref, sr_ref, si_ref, pr_ref, pi_ref,
          btr_ref, bti_ref, ctr_ref, cti_ref, ar_ref, ai_ref, d_ref),
         (du_ref, dbr_ref, dbi_ref, dcr_ref, dci_ref, dar_ref, dai_ref, dd_ref),
         (car_re, car_im, lam_re, lam_im, b_re, b_im, c_re, c_im, acc_br, acc_bi, acc_cr, acc_ci),
         run_background) = split(refs)
        run_background()
        i = pl.program_id(1)
        chunk = nchunk - 1 - i
        ar, ai = ar_ref[...], ai_ref[...]
        row = lax.broadcasted_iota(jnp.int32, (T, S), 0)

        @pl.when(i == 0)
        def _():
            car_re[...] = jnp.zeros_like(car_re)
            car_im[...] = jnp.zeros_like(car_im)
            b_re[...] = _expand_groups(btr_ref[...])
            b_im[...] = _expand_groups(bti_ref[...])
            c_re[...] = _expand_groups(ctr_ref[...])
            c_im[...] = _expand_groups(cti_ref[...])

        u = u_ref[...]
        dy = (d1_ref[...] + d2_ref[...]) * _gelu_grad(y_ref[...])
        dyb = dy.astype(BF16)
        lam_re[...] = jnp.dot(dyb, c_re[...], preferred_element_type=F32)
        lam_im[...] = -jnp.dot(dyb, c_im[...], preferred_element_type=F32)
        car_re[...], car_im[...] = _scan_rows(lam_re, lam_im, lam_re, lam_im, ar, -ai, car_re[...], car_im[...], True)
        lr, li = lam_re[...], lam_im[...]

        s_re, s_im = sr_ref[...], si_ref[...]
        has_prev = (chunk > 0).astype(F32)
        prev_re = pr_ref[7:8, :] * has_prev
        prev_im = pi_ref[7:8, :] * has_prev
        sp_re = jnp.where(row == 0, prev_re, pltpu.roll(s_re, 1, 0))
        sp_im = jnp.where(row == 0, prev_im, pltpu.roll(s_im, 1, 0))
        p_ar = _rowsum(lr * sp_re + li * sp_im)
        p_ai = _rowsum(li * sp_re - lr * sp_im)

        lrb, lib, ub = lr.astype(BF16), li.astype(BF16), u.astype(BF16)
        du = dy * d_ref[...] + nt_dot(lrb, b_re[...]) + nt_dot(lib, b_im[...])
        du_ref[...] = du.astype(BF16)
        tdot = lambda p, q: lax.dot_general(p, q, (((0,), (0,)), ((), ())), preferred_element_type=F32)
        p_br, p_bi = tdot(ub, lrb), tdot(ub, lib)
        p_cr, p_ci = tdot(dyb, s_re.astype(BF16)), -tdot(dyb, s_im.astype(BF16))
        p_dd = _rowsum(dy * u)

        @pl.when(i == 0)
        def _():
            dar_ref[...] = p_ar
            dai_ref[...] = p_ai
            acc_br[...] = p_br
            acc_bi[...] = p_bi
            acc_cr[...] = p_cr
            acc_ci[...] = p_ci
            dd_ref[...] = p_dd

        @pl.when(i > 0)
        def _():
            dar_ref[...] += p_ar
            dai_ref[...] += p_ai
            acc_br[...] += p_br
            acc_bi[...] += p_bi
            acc_cr[...] += p_cr
            acc_ci[...] += p_ci
            dd_ref[...] += p_dd

        @pl.when(i == nchunk - 1)
        def _():
            dbr_ref[...] = _collapse_groups(acc_br[...])
            dbi_ref[...] = _collapse_groups(acc_bi[...])
            dcr_ref[...] = _collapse_groups(acc_cr[...])
            dci_ref[...] = _collapse_groups(acc_ci[...])

    rev = lambda k, i: (nchunk - 1 - i, k)
    prev = lambda k, i: (jnp.maximum((nchunk - 1 - i) * tail - 1, 0), k)
    kspec = lambda shape: pl.BlockSpec((None,) + shape, lambda k, i: (k, 0, 0))
    compact = pl.BlockSpec((H, S), lambda k, i: (0, k))
    compact_shape = jax.ShapeDtypeStruct((H, NK * S), F32)
    res = pl.pallas_call(
        body, name=name, grid=grid,
        in_specs=[pl.BlockSpec((T, C), rev), pl.BlockSpec((T, C), rev), pl.BlockSpec((T, C), rev),
                  pl.BlockSpec((T, C), rev), pl.BlockSpec((T, S), rev), pl.BlockSpec((T, S), rev),
                  pl.BlockSpec((8, S), prev), pl.BlockSpec((8, S), prev),
                  compact, compact, compact, compact,
                  kspec((1, S)), kspec((1, S)), kspec((1, C))] + bg_in_specs,
        out_specs=[pl.BlockSpec((T, C), rev), compact, compact, compact, compact,
                   kspec((1, S)), kspec((1, S)), kspec((1, C))] + bg_out_specs,
        out_shape=[jax.ShapeDtypeStruct((L, NK * C), BF16),
                   compact_shape, compact_shape, compact_shape, compact_shape,
                   jax.ShapeDtypeStruct((NK, 1, S), F32), jax.ShapeDtypeStruct((NK, 1, S), F32),
                   jax.ShapeDtypeStruct((NK, 1, C), F32)] + bg_out_shapes,
        scratch_shapes=[pltpu.VMEM((1, S), F32), pltpu.VMEM((1, S), F32),
                        pltpu.VMEM((T, S), F32), pltpu.VMEM((T, S), F32)]
        + [pltpu.VMEM((C, S), BF16)] * 4 + [pltpu.VMEM((C, S), F32)] * 4 + bg_scratch,
        compiler_params=_params(_semantics(bg, ("parallel", "arbitrary"))),
    )(z, y0, dd_direct, dd_mm, s_re, s_im, s_re, s_im, bt_re, bt_im, ct_re, ct_im, a_re, a_im, d, *bg_args)
    return _results(res, 8, bg)


def _position():
    return lax.axis_index("x"), lax.axis_index("y"), lax.axis_index("c")


def _other_chips(x, y):
    return [(1 - x, y), (x, 1 - y), (1 - x, 1 - y)]


def _gather_phases(n, rows=None):
    def parts(ins, outs, sems):
        send_sems, recv_sems, local_sems = sems
        x, y, c = _position()
        me, sibling = (x, y, c), (x, y, 1 - c)
        chips = _other_chips(x, y)

        def block(a, pos):
            index = 4 * pos[0] + 2 * pos[1] + pos[2]
            if rows is not None:
                return outs[a].at[index, pl.ds(*rows)]
            return outs[a].at[pl.ds(index, 1)] if _is_row(ins[a]) else outs[a].at[index]

        def copy(a, k, pos, to, src=None):
            return pltpu.make_async_remote_copy(
                src_ref=block(a, pos) if src is None else src, dst_ref=block(a, pos),
                send_sem=send_sems.at[7 * a + k], recv_sem=recv_sems.at[7 * a + k],
                device_id=to, device_id_type=MESH)

        shard = [ins[a] if rows is None else ins[a].at[pl.ds(*rows)] for a in range(n)]
        mine = [pltpu.make_async_copy(shard[a], block(a, me), local_sems.at[a]) for a in range(n)]
        first = []
        for a in range(n):
            first.append(copy(a, 0, me, sibling, src=shard[a]))
            first += [copy(a, 1 + j, me, (*chip, c), src=shard[a]) for j, chip in enumerate(chips)]
        passed = [copy(a, 4 + j, (*chip, c), sibling) for a in range(n) for j, chip in enumerate(chips)]
        arrived = [copy(a, 1 + j, (*chip, c), me) for a in range(n) for j, chip in enumerate(chips)]
        from_sibling = []
        for a in range(n):
            from_sibling.append(copy(a, 0, sibling, me))
            from_sibling += [copy(a, 4 + j, (*chip, 1 - c), me) for j, chip in enumerate(chips)]
        return mine, first, passed, arrived, from_sibling

    def send(ins, outs, sems):
        mine, first, _, _, _ = parts(ins, outs, sems)
        for cp in mine + first:
            cp.start()

    def forward(ins, outs, sems):
        _, _, passed, arrived, _ = parts(ins, outs, sems)
        for got, fwd in zip(arrived, passed):
            got.wait_recv()
            fwd.start()

    def finish(ins, outs, sems):
        mine, first, passed, _, from_sibling = parts(ins, outs, sems)
        for cp in from_sibling:
            cp.wait_recv()
        for cp in first + passed:
            cp.wait_send()
        for cp in mine:
            cp.wait()

    return [(0.0, send), (GATHER_FORWARD_AT, forward), (1.0, finish)]


def _is_row(a):
    return len(a.shape) == 2 and a.shape[0] == 1


def _gather_shapes(shards):
    n = len(shards)
    return ([jax.ShapeDtypeStruct((N_DEV,) + (s.shape[1:] if _is_row(s) else s.shape), s.dtype) for s in shards],
            [pltpu.SemaphoreType.DMA((7 * n,)), pltpu.SemaphoreType.DMA((7 * n,)), pltpu.SemaphoreType.DMA((n,))])


def gather_background(shards, rows=None, into=None):
    out_shapes, scratch = _gather_shapes(shards)
    bg = Background(list(shards) + list(into or []), out_shapes, scratch, _gather_phases(len(shards), rows))
    bg.aliases = {len(shards) + k: k for k in range(len(into or []))}
    return bg


def all_gather_blocks(shards, *, name):
    n = len(shards)
    out_shapes, scratch = _gather_shapes(shards)

    def body(*refs):
        for _, phase in _gather_phases(n):
            phase(refs[:n], refs[n:2 * n], refs[2 * n:])

    return pl.pallas_call(
        body, name=name, in_specs=[ANY] * n, out_specs=[ANY] * n, out_shape=out_shapes, scratch_shapes=scratch,
    )(*shards)


def sibling_exchange(grads, *, name):
    n = len(grads)
    bg = sibling_exchange_background(grads)

    def body(*refs):
        for _, phase in bg.phases:
            phase(refs[:n], refs[n:2 * n], refs[2 * n:])

    return pl.pallas_call(
        body, name=name, in_specs=[ANY] * n, out_specs=[ANY] * n, out_shape=bg.out_shapes, scratch_shapes=bg.scratch,
    )(*grads)


def sibling_exchange_background(grads):
    n = len(grads)

    def copies(ins, outs, sems):
        x, y, c = _position()
        return [pltpu.make_async_remote_copy(
            src_ref=ins[a].at[2 * q + 1 - c], dst_ref=outs[a].at[q],
            send_sem=sems[0].at[4 * a + q], recv_sem=sems[1].at[4 * a + q],
            device_id=(x, y, 1 - c), device_id_type=MESH)
            for a in range(n) for q in range(4)]

    def send(ins, outs, sems):
        for cp in copies(ins, outs, sems):
            cp.start()

    def finish(ins, outs, sems):
        for cp in copies(ins, outs, sems):
            cp.wait()

    return Background(grads, [jax.ShapeDtypeStruct((4,) + g.shape[1:], g.dtype) for g in grads],
                      [pltpu.SemaphoreType.DMA((4 * n,)), pltpu.SemaphoreType.DMA((4 * n,))],
                      [(0.0, send), (1.0, finish)])


def _chip_exchange_phases(n):
    def copies(ins, outs, sems):
        x, y, c = _position()
        return [pltpu.make_async_remote_copy(
            src_ref=ins[a].at[2 * chip[0] + chip[1]], dst_ref=outs[a].at[j],
            send_sem=sems[0].at[3 * a + j], recv_sem=sems[1].at[3 * a + j],
            device_id=(*chip, c), device_id_type=MESH)
            for a in range(n) for j, chip in enumerate(_other_chips(x, y))]

    def send(ins, outs, sems):
        for cp in copies(ins, outs, sems):
            cp.start()

    def finish(ins, outs, sems):
        for cp in copies(ins, outs, sems):
            cp.wait()

    return [(0.0, send), (1.0, finish)]


def chip_exchange_background(parts):
    n = len(parts)
    return Background(parts, [jax.ShapeDtypeStruct((3,) + p.shape[1:], p.dtype) for p in parts],
                      [pltpu.SemaphoreType.DMA((3 * n,)), pltpu.SemaphoreType.DMA((3 * n,))],
                      _chip_exchange_phases(n))


def add_pairs(grads, theirs, core, *, name, tm=512):
    _, R, C = theirs.shape
    tm = _tile(R, tm, 16)

    def body(core_ref, a_ref, b_ref, o_ref):
        o_ref[...] = (a_ref[...].astype(F32) + b_ref[...].astype(F32)).astype(o_ref.dtype)

    spec = pl.BlockSpec((None, tm, C), lambda q, i, core_ref: (q, i, 0))
    return pl.pallas_call(
        body, name=name,
        grid_spec=pltpu.PrefetchScalarGridSpec(
            num_scalar_prefetch=1, grid=(4, R // tm),
            in_specs=[pl.BlockSpec((None, tm, C), lambda q, i, core_ref: (2 * q + core_ref[0], i, 0)), spec],
            out_specs=spec),
        out_shape=jax.ShapeDtypeStruct(theirs.shape, BF16),
        compiler_params=_params(("parallel", "parallel")),
    )(core, grads, theirs)


def _adamw(w, g, m, v):
    m = ADAM_B1 * m + (1.0 - ADAM_B1) * g
    v = ADAM_B2 * v + (1.0 - ADAM_B2) * (g * g)
    m_hat = m / (1.0 - ADAM_B1 ** ADAM_STEP)
    v_hat = v / (1.0 - ADAM_B2 ** ADAM_STEP)
    delta = -ADAM_LR * (m_hat / (jnp.sqrt(v_hat) + ADAM_EPS) + ADAM_WD * w)
    return delta, m, v


def adamw_sharded(w, m, v, grads, theirs, others, where, *, name, tm=256):
    R, C = w.shape
    tm = _tile(R, tm, 16)

    def body(where_ref, w_ref, m_ref, v_ref, a_ref, b_ref, o_ref, g_ref, d_ref, nm_ref, nv_ref):
        g = a_ref[...].astype(F32) + b_ref[...].astype(F32)
        for j in range(3):
            g = g + o_ref[j].astype(F32)
        g_ref[...] = g
        d_ref[...], nm_ref[...], nv_ref[...] = _adamw(w_ref[...], g, m_ref[...], v_ref[...])

    spec = pl.BlockSpec((tm, C), lambda i, where_ref: (i, 0))
    sd = jax.ShapeDtypeStruct((R, C), F32)
    return pl.pallas_call(
        body, name=name,
        grid_spec=pltpu.PrefetchScalarGridSpec(
            num_scalar_prefetch=1, grid=(R // tm,),
            in_specs=[spec, spec, spec,
                      pl.BlockSpec((None, tm, C), lambda i, where_ref: (where_ref[0], i, 0)),
                      pl.BlockSpec((None, tm, C), lambda i, where_ref: (where_ref[1], i, 0)),
                      pl.BlockSpec((3, tm, C), lambda i, where_ref: (0, i, 0))],
            out_specs=[spec, spec, spec, spec]),
        out_shape=[sd, sd, sd, sd],
        compiler_params=_params(("parallel",)),
    )(where, w, m, v, grads, theirs, others)


def sum_gathered(gathered, *, name):
    n = len(gathered)

    def body(*refs):
        for ga_ref, o_ref in zip(refs[:n], refs[n:]):
            rows = len(ga_ref.shape) == 2
            total = ga_ref[0:1] if rows else ga_ref[0]
            for dev in range(1, N_DEV):
                total = total + (ga_ref[dev:dev + 1] if rows else ga_ref[dev])
            o_ref[...] = total

    shapes = [jax.ShapeDtypeStruct((1,) + g.shape[1:] if g.ndim == 2 else g.shape[1:], F32) for g in gathered]
    return pl.pallas_call(body, name=name, out_shape=shapes, compiler_params=_params())(*gathered)


def adamw_replicated(ws, ms, vs, gs, *, name):
    n = len(ws)

    def body(*refs):
        w_refs, m_refs, v_refs, g_refs = refs[:n], refs[n:2 * n], refs[2 * n:3 * n], refs[3 * n:4 * n]
        outs = refs[4 * n:]
        for k in range(n):
            outs[k][...], outs[n + k][...], outs[2 * n + k][...] = _adamw(
                w_refs[k][...], g_refs[k][...], m_refs[k][...], v_refs[k][...])

    shapes = [jax.ShapeDtypeStruct(t.shape, F32) for t in ws]
    res = pl.pallas_call(body, name=name, out_shape=shapes * 3, compiler_params=_params())(*ws, *ms, *vs, *gs)
    return res[:n], res[n:2 * n], res[2 * n:]


SHARDED = ("w_in", "ssm_glu_w", "w_out", "w_ffn_in", "w_ffn_out", "w_ple_gate", "w_ple_proj")
SMALL_LAST = ("norm_mix_g",)
SMALL_WIDE = ("ssm_b_re", "ssm_b_im", "ssm_c_re", "ssm_c_im")
SMALL = ("ssm_lambda_re", "ssm_lambda_im", "ssm_log_step", "ssm_b_re", "ssm_b_im", "ssm_c_re",
         "ssm_c_im", "ssm_d", "ssm_glu_b", "sgu_ln_g", "sgu_ln_b", "sgu_w", "sgu_b", "out_norm_ssm_g",
         "out_norm_sgu_g", "norm_ffn_g", "norm_ple_g", "b_ple_gate", "final_norm_g")
WEIGHTS = ("norm_mix_g", "w_in", "ssm_lambda_re", "ssm_lambda_im", "ssm_log_step", "ssm_b_re", "ssm_b_im",
           "ssm_c_re", "ssm_c_im", "ssm_d", "ssm_glu_w", "ssm_glu_b", "sgu_ln_g", "sgu_ln_b", "sgu_w", "sgu_b",
           "out_norm_ssm_g", "out_norm_sgu_g", "w_out", "norm_ffn_g", "w_ffn_in", "w_ffn_out", "norm_ple_g",
           "w_ple_gate", "b_ple_gate", "w_ple_proj", "final_norm_g")


def _step(x, p, loss_target, w, m, v):
    L, D = x.shape[1], x.shape[2]
    x2d, p2d, tgt = x.reshape(L, D), p.reshape(L, -1), loss_target.reshape(L, D)
    d_ssm = w["ssm_glu_w"].shape[2]
    d_sgu = w["sgu_ln_g"].shape[1]
    G, P, H = w["ssm_b_re"].shape[1:]
    SG = min(SSM_SUPER, G)
    NK = G // SG
    row = lambda a: a.reshape(1, -1)

    shard2d = {n: w[n].reshape(w[n].shape[1:]) for n in SHARDED}
    shard_bf = {n: shard2d[n].astype(BF16) for n in SHARDED}
    (w_ple_blk,) = all_gather_blocks([shard_bf["w_ple_proj"]], name="gather_w_ple")
    bf = lambda t: t.astype(BF16)
    pp, (w_in_blk,) = mm_nn(bf(p2d), w_ple_blk, name="ple_proj", out_dtype=F32, tm=512, tn=512, tk=2048,
                            bg=gather_background([shard_bf["w_in"]]))
    w_in = jnp.transpose(w_in_blk, (1, 0, 2)).reshape(D, -1)
    F = shard2d["w_ffn_in"].shape[1] * 4

    lam_re, lam_im, log_step = w["ssm_lambda_re"][0], w["ssm_lambda_im"][0], w["ssm_log_step"][0].reshape(G, 1)
    a_re, a_im, q_re, q_im = disc_lambda_fwd(lam_re, lam_im, log_step, name="s5_discretise_lambda")
    bt_re = w["ssm_b_re"][0].transpose(2, 0, 1).reshape(H, G * P)
    bt_im = w["ssm_b_im"][0].transpose(2, 0, 1).reshape(H, G * P)
    bbar_re, bbar_im = disc_b_fwd(row(q_re), row(q_im), bt_re, bt_im, name="s5_discretise_b")
    ct_re = w["ssm_c_re"][0].transpose(1, 0, 2).reshape(H, G * P)
    ct_im = w["ssm_c_im"][0].transpose(1, 0, 2).reshape(H, G * P)
    a_re_k, a_im_k = a_re.reshape(NK, 1, SG * P), a_im.reshape(NK, 1, SG * P)
    d_k = w["ssm_d"][0].reshape(NK, 1, SG * H)

    h1 = norm_fwd(x2d, w["norm_mix_g"], name="norm_mix")
    z, (w_glu, w_out) = mm_nn(h1, w_in, name="in_proj", out_dtype=F32, tm=512, tn=1024, tk=2048,
                              bg=gather_background([shard_bf["ssm_glu_w"], shard_bf["w_out"]]))
    w_glu, w_out = w_glu.reshape(d_ssm, d_ssm), w_out.reshape(D, D)
    first_rows = (D * 11 // 16) // 16 * 16
    s5_mats = (bbar_re, bbar_im, ct_re, ct_im, a_re_k, a_im_k, d_k)
    (y0, s_re, s_im), (w_ffn_in_part,) = ssm_fwd(
        z, *s5_mats, name="s5_scan", bg=gather_background([shard_bf["w_ffn_in"]], rows=(0, first_rows)))
    ya1 = glu_pre(y0, name="s5_gelu")
    t_glu = mm_nn(ya1, w_glu, name="s5_glu_proj", out_dtype=F32, tm=512, tn=512, tk=2048)
    n_a = glu_post(y0, t_glu, w["ssm_glu_b"], w["out_norm_ssm_g"], name="s5_glu_norm")
    b_s3 = w["sgu_b"][0][:, :, None]
    n_b = sgu_fwd(z, w["sgu_ln_g"], w["sgu_ln_b"], w["sgu_w"][0], b_s3, w["out_norm_sgu_g"], name="sgu", d_sgu=d_sgu)
    ycat = jnp.concatenate([n_a, n_b], axis=1)
    x1, (w_ffn_in_blk,) = mm_nn(
        ycat, w_out, name="out_proj", out_dtype=F32, tm=512, tn=512, tk=2048, residual=x2d,
        bg=gather_background([shard_bf["w_ffn_in"]], rows=(first_rows, D - first_rows), into=[w_ffn_in_part]))
    h2 = norm_fwd(x1, w["norm_ffn_g"], name="norm_ffn")
    (act, gate_ff, up_ff), (w_ffn_out, w_gate) = ffn_in_swiglu(
        h2, w_ffn_in_blk, name="ffn_in_swiglu", tm=256,
        bg=gather_background([shard_bf["w_ffn_out"], shard_bf["w_ple_gate"]]))
    w_ffn_out, w_gate = w_ffn_out.reshape(F, D), w_gate.reshape(D, D)
    x2 = mm_nn(act, w_ffn_out, name="ffn_out", out_dtype=F32, tm=512, tn=512, tk=F, residual=x1)
    h3 = norm_fwd(x2, w["norm_ple_g"], name="norm_ple")
    gpre = mm_nn(h3, w_gate, name="ple_gate", out_dtype=F32, tm=512, tn=1024, tk=2048)

    dx3, dpre, dpp, loss_part, d_final_g, d_b_gate = head_and_loss(
        x2, gpre, w["b_ple_gate"], pp, row(w["final_norm_g"]), tgt, name="head_and_loss")
    x_pos, y_pos, c_pos = _position()
    where = jnp.stack([4 * x_pos + 2 * y_pos + c_pos, 2 * x_pos + y_pos]).astype(jnp.int32)
    core = jnp.reshape(c_pos, (1,)).astype(jnp.int32)
    own, others = {}, {}

    def blocks(named):
        g8 = {n: t.reshape((N_DEV,) + shard2d[n].shape) for n, t in named.items()}
        return g8, sibling_exchange_background(list(g8.values()))

    def chip_sums(g8, theirs):
        own.update(zip(g8, zip(g8.values(), theirs)))
        return [add_pairs(g, t, core, name="chip_sum_" + n) for (n, g), t in zip(g8.items(), theirs)]

    d_w_gate = mm_tn(h3, dpre, name="d_w_ple_gate", out_dtype=BF16, tm=L, tko=1024, tno=1024)
    d_w_ple = mm_tn(bf(p2d), dpp, name="d_w_ple_proj", out_dtype=BF16, tm=L, tko=1024, tno=1024, out_blocks=N_DEV)
    g8_ple, bg = blocks({"w_ple_gate": d_w_gate, "w_ple_proj": d_w_ple})
    dh3, theirs = mm_nt(dpre, w_gate, name="d_h_ple", out_dtype=F32, tm=512, tko=1024, tc=2048, bg=bg)
    bg = chip_exchange_background(chip_sums(g8_ple, theirs))
    dx2, dx2b, d_ple_g = norm_bwd(dh3, x2, w["norm_ple_g"], dx3, name="d_norm_ple", want_bf16=True)
    d_w_ffn_out, got = mm_tn(act, dx2b, name="d_w_ffn_out", out_dtype=BF16, tm=L, tko=1408, tno=512, bg=bg)
    others.update(zip(g8_ple, got))
    g8_fo, bg = blocks({"w_ffn_out": d_w_ffn_out})
    dgate, dup = ffn_out_bwd_swiglu(dx2b, w_ffn_out, gate_ff, up_ff, name="d_act_swiglu")
    half = N_DEV // 2
    d_w_ffn_in, theirs = mm_tn(h2, dgate, name="d_w_ffn_in_gate", out_dtype=BF16, tm=L, tko=512, tno=1408,
                               out_blocks=half, total_blocks=N_DEV, bg=bg)
    bg = chip_exchange_background(chip_sums(g8_fo, theirs))
    d_w_ffn_in, got = mm_tn(h2, dup, name="d_w_ffn_in_up", out_dtype=BF16, tm=L, tko=512, tno=1408,
                            out_blocks=half, block_offset=half, total_blocks=N_DEV, into=d_w_ffn_in, bg=bg)
    others.update(zip(g8_fo, got))
    g8_fi, bg = blocks({"w_ffn_in": d_w_ffn_in})
    dh2, theirs = mm_nt(dgate, w_ffn_in_blk, a2=dup, name="d_h_ffn", out_dtype=F32, tm=1024, tko=1024, tc=1408, bg=bg)
    late_parts = chip_sums(g8_fi, theirs)
    dx1, dx1b, d_ffn_g = norm_bwd(dh2, x1, w["norm_ffn_g"], dx2, name="d_norm_ffn", want_bf16=True)
    dycat = mm_nt(dx1b, w_out, name="d_ycat", out_dtype=F32, tm=512, tko=1024, tc=2048)
    d_w_out = mm_tn(ycat, dx1b, name="d_w_out", out_dtype=BF16, tm=L, tko=1024, tno=1024)
    g8_out, bg = blocks({"w_out": d_w_out})
    dzu, dzv, d_sgu_w, d_sgu_b, d_ln_g, d_ln_b, d_g_b = sgu_bwd(
        z, dycat, w["sgu_ln_g"], w["sgu_ln_b"], w["sgu_w"][0], b_s3, w["out_norm_sgu_g"], name="d_sgu", d_sgu=d_sgu)
    dt_glu, dd_direct, d_g_a, d_glu_b = glu_post_bwd(
        y0, t_glu, w["ssm_glu_b"], w["out_norm_ssm_g"], dycat, name="d_s5_glu_norm")
    d_w_glu, theirs = mm_tn(ya1, dt_glu, name="d_w_glu", out_dtype=BF16, tm=L, tko=1024, tno=1024, bg=bg)
    late_parts += chip_sums(g8_out, theirs)
    g8_glu, bg = blocks({"ssm_glu_w": d_w_glu})
    dd_mm, theirs = mm_nt(dt_glu, w_glu, name="d_s5_glu_proj", out_dtype=F32, tm=512, tko=1024, tc=2048, bg=bg)
    late_parts += chip_sums(g8_glu, theirs)
    (du, d_bbar_re, d_bbar_im, d_ct_re, d_ct_im, d_a_re, d_a_im, d_d), got = ssm_bwd(
        z, y0, dd_direct, dd_mm, s_re, s_im, *s5_mats, name="d_s5_scan", bg=chip_exchange_background(late_parts))
    others.update(zip(("w_ffn_in", "w_out", "ssm_glu_w"), got))
    dz = jnp.concatenate([du, dzu, dzv], axis=1)

    d_q_re, d_q_im, d_bt_re, d_bt_im = disc_b_bwd(row(q_re), row(q_im), bt_re, bt_im, d_bbar_re, d_bbar_im,
                                                  name="d_s5_discretise_b")
    d_lam_re, d_lam_im, d_log_step = disc_lambda_bwd(
        lam_re, lam_im, log_step,
        (d_a_re.reshape(G, P), d_a_im.reshape(G, P), d_q_re.reshape(G, P), d_q_im.reshape(G, P)),
        name="d_s5_discretise_lambda")
    small_grads = {
        "ssm_lambda_re": d_lam_re, "ssm_lambda_im": d_lam_im, "ssm_log_step": d_log_step,
        "ssm_b_re": d_bt_re, "ssm_b_im": d_bt_im, "ssm_c_re": d_ct_re, "ssm_c_im": d_ct_im,
        "ssm_d": d_d, "ssm_glu_b": d_glu_b, "sgu_ln_g": d_ln_g, "sgu_ln_b": d_ln_b,
        "sgu_w": d_sgu_w, "sgu_b": d_sgu_b, "out_norm_ssm_g": d_g_a, "out_norm_sgu_g": d_g_b,
        "norm_ffn_g": d_ffn_g, "norm_ple_g": d_ple_g, "b_ple_gate": d_b_gate, "final_norm_g": d_final_g,
    }

    d_w_in, got = mm_tn(h1, dz, name="d_w_in", out_dtype=BF16, tm=L, tko=1024, tno=1024, out_blocks=N_DEV,
                        bg=gather_background([loss_part] + [small_grads[n] for n in SMALL]))
    sums = sum_gathered(got, name="sum_small_grads")
    g8_in, _ = blocks({"w_in": d_w_in})
    theirs = sibling_exchange(list(g8_in.values()), name="grads_to_sibling_w_in")
    dh1, got = mm_nt(dz, w_in, name="d_h_mix", out_dtype=F32, tm=512, tko=1024, tc=3 * d_sgu,
                     bg=chip_exchange_background(chip_sums(g8_in, theirs)))
    others.update(zip(g8_in, got))
    grad_x, d_mix_g = norm_bwd(dh1, x2d, w["norm_mix_g"], dx1, name="d_norm_mix", want_bf16=False)
    loss, small_sum = sums[0][0, 0], dict(zip(SMALL, sums[1:]))
    (small_sum["norm_mix_g"],) = sum_gathered(all_gather_blocks([d_mix_g], name="gather_last_grad"),
                                              name="sum_last_grad")

    out = {}
    for n in SHARDED:
        res = adamw_sharded(shard2d[n], m[n].reshape(shard2d[n].shape), v[n].reshape(shard2d[n].shape),
                            own[n][0], own[n][1], others[n], where, name="adamw_" + n)
        out[n] = [r.reshape(w[n].shape) for r in res]

    def work_shape(n):
        s = w[n].shape
        return (1,) + s if len(s) == 1 else (s if len(s) == 2 else s[1:])

    for n in ("ssm_b_re", "ssm_b_im"):
        small_sum[n] = small_sum[n].reshape(H, G, P).transpose(1, 2, 0)
    for n in ("ssm_c_re", "ssm_c_im"):
        small_sum[n] = small_sum[n].reshape(H, G, P).transpose(1, 0, 2)

    def replicated(names_, name):
        gs = [small_sum[n].reshape(work_shape(n)) for n in names_]
        res = adamw_replicated(*[[t[n].reshape(work_shape(n)) for n in names_] for t in (w, m, v)], gs, name=name)
        for i, n in enumerate(names_):
            out[n] = [r.reshape(w[n].shape) for r in (gs[i], res[0][i], res[1][i], res[2][i])]

    replicated([n for n in SMALL + SMALL_LAST if n not in SMALL_WIDE], "adamw_small")
    replicated(list(SMALL_WIDE), "adamw_s5_b_c")

    grads = [out[n][0] for n in WEIGHTS]
    deltas = [out[n][1] for n in WEIGHTS]
    new_m = [out[n][2] for n in WEIGHTS]
    new_v = [out[n][3] for n in WEIGHTS]
    return (loss, grad_x.reshape(x.shape), *grads, *deltas, *new_m, *new_v)


def kernel(x, p, norm_mix_g, w_in, ssm_lambda_re, ssm_lambda_im, ssm_log_step, ssm_b_re, ssm_b_im, ssm_c_re, ssm_c_im, ssm_d, ssm_glu_w, ssm_glu_b, sgu_ln_g, sgu_ln_b, sgu_w, sgu_b, out_norm_ssm_g, out_norm_sgu_g, w_out, norm_ffn_g, w_ffn_in, w_ffn_out, norm_ple_g, w_ple_gate, b_ple_gate, w_ple_proj, final_norm_g, loss_target, m_norm_mix_g, m_w_in, m_ssm_lambda_re, m_ssm_lambda_im, m_ssm_log_step, m_ssm_b_re, m_ssm_b_im, m_ssm_c_re, m_ssm_c_im, m_ssm_d, m_ssm_glu_w, m_ssm_glu_b, m_sgu_ln_g, m_sgu_ln_b, m_sgu_w, m_sgu_b, m_out_norm_ssm_g, m_out_norm_sgu_g, m_w_out, m_norm_ffn_g, m_w_ffn_in, m_w_ffn_out, m_norm_ple_g, m_w_ple_gate, m_b_ple_gate, m_w_ple_proj, m_final_norm_g, v_norm_mix_g, v_w_in, v_ssm_lambda_re, v_ssm_lambda_im, v_ssm_log_step, v_ssm_b_re, v_ssm_b_im, v_ssm_c_re, v_ssm_c_im, v_ssm_d, v_ssm_glu_w, v_ssm_glu_b, v_sgu_ln_g, v_sgu_ln_b, v_sgu_w, v_sgu_b, v_out_norm_ssm_g, v_out_norm_sgu_g, v_w_out, v_norm_ffn_g, v_w_ffn_in, v_w_ffn_out, v_norm_ple_g, v_w_ple_gate, v_b_ple_gate, v_w_ple_proj, v_final_norm_g):
    given = dict(locals())
    w = {n: given[n] for n in WEIGHTS}
    m = {n: given["m_" + n] for n in WEIGHTS}
    v = {n: given["v_" + n] for n in WEIGHTS}
    return _step(x, p, loss_target, w, m, v)
```

```python
import functools
import math

import jax
import jax.numpy as jnp
from jax import lax
from jax.experimental import pallas as pl
from jax.experimental.pallas import tpu as pltpu

F32 = jnp.float32
BF16 = jnp.bfloat16
MESH = pl.DeviceIdType.MESH
ANY = pl.BlockSpec(memory_space=pl.ANY)

N_DEV = 8
EPS = 1e-6
LAMBDA_RE_MAX = -1e-4
SSM_GROUP = 16
SSM_STATE = 64
SSM_SUPER = 16
SGU_CHUNK = 128
ADAM_LR, ADAM_B1, ADAM_B2, ADAM_EPS, ADAM_WD, ADAM_STEP = 0.001, 0.9, 0.999, 1e-08, 0.01, 10
VMEM_LIMIT = 52 * 1024 * 1024
LANE = 128
GATHER_FORWARD_AT = 0.85

_GELU_C = math.sqrt(2.0 / math.pi)


def _params(sem=None):
    return pltpu.CompilerParams(dimension_semantics=sem, vmem_limit_bytes=VMEM_LIMIT)


def _tile(dim, pref, unit=LANE):
    if dim <= pref:
        return dim
    t = (pref // unit) * unit
    while t >= unit:
        if dim % t == 0:
            return t
        t -= unit
    return dim


def _gelu(x):
    return 0.5 * x * (1.0 + jnp.tanh(_GELU_C * (x + 0.044715 * x * x * x)))


def _gelu_grad(x):
    t = jnp.tanh(_GELU_C * (x + 0.044715 * x * x * x))
    return 0.5 * (1.0 + t) + 0.5 * x * (1.0 - t * t) * (_GELU_C * (1.0 + 3.0 * 0.044715 * x * x))


def _gelu_and_grad(x):
    t = jnp.tanh(_GELU_C * (x + 0.044715 * x * x * x))
    return (0.5 * x * (1.0 + t),
            0.5 * (1.0 + t) + 0.5 * x * (1.0 - t * t) * (_GELU_C * (1.0 + 3.0 * 0.044715 * x * x)))


def _rms(x):
    return lax.rsqrt(jnp.mean(x * x, axis=-1, keepdims=True) + EPS)


def _rmsnorm_bwd(dy, x, r, g):
    dyg = dy * g
    return r * dyg - x * (r * r * r) * jnp.mean(dyg * x, axis=-1, keepdims=True)


def _rowsum(v):
    return jnp.sum(v, axis=0, keepdims=True)


class Background:
    def __init__(self, inputs, out_shapes, scratch, phases):
        self.inputs, self.out_shapes, self.scratch, self.phases = list(inputs), list(out_shapes), list(scratch), phases
        self.aliases = {}

    def emit(self, step, nsteps, ins, outs, scratch):
        for place, phase in self.phases:
            at = min(int(place * nsteps), nsteps - 1)

            @pl.when(step == at)
            def _():
                phase(ins, outs, scratch)


def _carrier(bg, n_in, n_out, n_scratch, grid):
    nbi = len(bg.inputs) if bg else 0
    nbo = len(bg.out_shapes) if bg else 0
    nsteps = math.prod(grid)

    def split(refs):
        ins = refs[:n_in]
        bg_ins = refs[n_in:n_in + nbi]
        outs = refs[n_in + nbi:n_in + nbi + n_out]
        bg_outs = refs[n_in + nbi + n_out:n_in + nbi + n_out + nbo]
        rest = refs[n_in + nbi + n_out + nbo:]
        scratch, bg_scratch = rest[:n_scratch], rest[n_scratch:]

        def run_background():
            if bg is None:
                return
            step = pl.program_id(0)
            for axis in range(1, len(grid)):
                step = step * grid[axis] + pl.program_id(axis)
            bg.emit(step, nsteps, bg_ins, bg_outs, bg_scratch)

        return ins, outs, scratch, run_background

    if bg is None:
        return [], [], [], [], [], split
    return [ANY] * nbi, list(bg.inputs), [ANY] * nbo, list(bg.out_shapes), list(bg.scratch), split


def _semantics(bg, sem):
    return tuple("arbitrary" for _ in sem) if bg is not None else sem


def _results(res, n_out, bg):
    res = list(res) if isinstance(res, (list, tuple)) else [res]
    own = res[0] if n_out == 1 else res[:n_out]
    return (own, res[n_out:]) if bg is not None else own


def mm_nn(a, b, *, name, out_dtype, tm, tn, tk, residual=None, bg=None):
    M, K = a.shape
    blocked = b.ndim == 3
    if blocked:
        nb, _, Nb = b.shape
        N = nb * Nb
        tn = _tile(Nb, tn)
        per = Nb // tn
    else:
        N = b.shape[1]
        tn = _tile(N, tn)
    tm, tk = _tile(M, tm, 8), _tile(K, tk)
    nj, ni, nk = N // tn, M // tm, K // tk
    has_res = residual is not None
    grid = (nj, ni, nk)
    bg_in_specs, bg_args, bg_out_specs, bg_out_shapes, bg_scratch, split = _carrier(
        bg, 3 if has_res else 2, 1, 0 if nk == 1 else 1, grid)

    def body(*refs):
        ins, (o_ref,), scratch, run_background = split(refs)
        run_background()
        a_ref, b_ref = ins[0], ins[1]
        r_ref = ins[2] if has_res else None

        def finish(acc):
            if has_res:
                acc = acc + r_ref[...]
            o_ref[...] = acc.astype(o_ref.dtype)

        part = jnp.dot(a_ref[...], b_ref[...], preferred_element_type=F32)
        if nk == 1:
            finish(part)
        else:
            acc_ref = scratch[0]
            k = pl.program_id(2)

            @pl.when(k == 0)
            def _():
                acc_ref[...] = part

            @pl.when(k > 0)
            def _():
                acc_ref[...] += part

            @pl.when(k == nk - 1)
            def _():
                finish(acc_ref[...])

    if blocked:
        b_spec = pl.BlockSpec((None, tk, tn), lambda j, i, k: (j // per, k, j % per))
    else:
        b_spec = pl.BlockSpec((tk, tn), lambda j, i, k: (k, j))
    in_specs = [pl.BlockSpec((tm, tk), lambda j, i, k: (i, k)), b_spec]
    args = [a, b]
    if has_res:
        in_specs.append(pl.BlockSpec((tm, tn), lambda j, i, k: (i, j)))
        args.append(residual)
    res = pl.pallas_call(
        body, name=name, grid=grid,
        in_specs=in_specs + bg_in_specs,
        out_specs=[pl.BlockSpec((tm, tn), lambda j, i, k: (i, j))] + bg_out_specs,
        out_shape=[jax.ShapeDtypeStruct((M, N), out_dtype)] + bg_out_shapes,
        input_output_aliases={len(args) + k: 1 + o for k, o in (bg.aliases if bg else {}).items()},
        scratch_shapes=([] if nk == 1 else [pltpu.VMEM((tm, tn), F32)]) + bg_scratch,
        compiler_params=_params(_semantics(bg, ("parallel", "parallel", "arbitrary"))),
    )(*args, *bg_args)
    return _results(res, 1, bg)


def mm_nt(a, w, *, name, out_dtype, tm, tko, tc, a2=None, bg=None):
    M, N = a.shape
    if a2 is not None:
        N = 2 * N
    blocked = w.ndim == 3
    if blocked:
        nb, Ko, Nb = w.shape
        tc = _tile(Nb, tc)
        per = Nb // tc
    else:
        Ko = w.shape[0]
        tc = _tile(N, tc)
    tm, tko = _tile(M, tm, 8), _tile(Ko, tko)
    njo, ni, nc = Ko // tko, M // tm, N // tc
    grid = (njo, ni, nc)
    half = nc // 2
    bg_in_specs, bg_args, bg_out_specs, bg_out_shapes, bg_scratch, split = _carrier(
        bg, 2 if a2 is None else 3, 1, 0 if nc == 1 else 1, grid)

    def body(*refs):
        ins, (o_ref,), scratch, run_background = split(refs)
        run_background()
        a_val = ins[0][...]
        if a2 is not None:
            a_val = jnp.where(pl.program_id(2) < half, a_val, ins[1][...])
        part = lax.dot_general(a_val, ins[-1][...], (((1,), (1,)), ((), ())),
                               preferred_element_type=F32)
        if nc == 1:
            o_ref[...] = part.astype(o_ref.dtype)
        else:
            acc_ref = scratch[0]
            c = pl.program_id(2)

            @pl.when(c == 0)
            def _():
                acc_ref[...] = part

            @pl.when(c > 0)
            def _():
                acc_ref[...] += part

            @pl.when(c == nc - 1)
            def _():
                o_ref[...] = acc_ref[...].astype(o_ref.dtype)

    if blocked:
        w_spec = pl.BlockSpec((None, tko, tc), lambda j, i, c: (c // per, j, c % per))
    else:
        w_spec = pl.BlockSpec((tko, tc), lambda j, i, c: (j, c))
    if a2 is None:
        a_specs, a_args = [pl.BlockSpec((tm, tc), lambda j, i, c: (i, c))], [a]
    else:
        a_specs = [pl.BlockSpec((tm, tc), lambda j, i, c: (i, jnp.minimum(c, half - 1))),
                   pl.BlockSpec((tm, tc), lambda j, i, c: (i, jnp.maximum(c - half, 0)))]
        a_args = [a, a2]
    res = pl.pallas_call(
        body, name=name, grid=grid,
        in_specs=a_specs + [w_spec] + bg_in_specs,
        out_specs=[pl.BlockSpec((tm, tko), lambda j, i, c: (i, j))] + bg_out_specs,
        out_shape=[jax.ShapeDtypeStruct((M, Ko), out_dtype)] + bg_out_shapes,
        scratch_shapes=([] if nc == 1 else [pltpu.VMEM((tm, tko), F32)]) + bg_scratch,
        compiler_params=_params(_semantics(bg, ("parallel", "parallel", "arbitrary"))),
    )(*a_args, w, *bg_args)
    return _results(res, 1, bg)


def mm_tn(a, g, *, name, out_dtype, tm, tko, tno, out_blocks=None, block_offset=0, total_blocks=None, into=None,
          bg=None):
    M, K = a.shape
    N = g.shape[1]
    if out_blocks:
        Nb = N // out_blocks
        tno = _tile(Nb, tno)
        per = Nb // tno
    else:
        tno = _tile(N, tno)
    tm, tko = _tile(M, tm), _tile(K, tko)
    njo, njn, nm = K // tko, N // tno, M // tm
    grid = (njo, njn, nm)
    bg_in_specs, bg_args, bg_out_specs, bg_out_shapes, bg_scratch, split = _carrier(
        bg, 2 if into is None else 3, 1, 0 if nm == 1 else 1, grid)

    def body(*refs):
        ins, (o_ref,), scratch, run_background = split(refs)
        a_ref, g_ref = ins[0], ins[1]
        run_background()
        part = lax.dot_general(a_ref[...], g_ref[...], (((0,), (0,)), ((), ())),
                               preferred_element_type=F32)
        if nm == 1:
            o_ref[...] = part.astype(o_ref.dtype)
        else:
            acc_ref = scratch[0]
            m = pl.program_id(2)

            @pl.when(m == 0)
            def _():
                acc_ref[...] = part

            @pl.when(m > 0)
            def _():
                acc_ref[...] += part

            @pl.when(m == nm - 1)
            def _():
                o_ref[...] = acc_ref[...].astype(o_ref.dtype)

    if out_blocks:
        o_spec = pl.BlockSpec((None, tko, tno), lambda jo, jn, m: (jn // per + block_offset, jo, jn % per))
        o_shape = jax.ShapeDtypeStruct((total_blocks or out_blocks, K, Nb), out_dtype)
    else:
        o_spec = pl.BlockSpec((tko, tno), lambda jo, jn, m: (jo, jn))
        o_shape = jax.ShapeDtypeStruct((K, N), out_dtype)
    res = pl.pallas_call(
        body, name=name, grid=grid,
        in_specs=[pl.BlockSpec((tm, tko), lambda jo, jn, m: (m, jo)),
                  pl.BlockSpec((tm, tno), lambda jo, jn, m: (m, jn))] + ([] if into is None else [ANY]) + bg_in_specs,
        out_specs=[o_spec] + bg_out_specs, out_shape=[o_shape] + bg_out_shapes,
        scratch_shapes=([] if nm == 1 else [pltpu.VMEM((tko, tno), F32)]) + bg_scratch,
        input_output_aliases={} if into is None else {2: 0},
        compiler_params=_params(_semantics(bg, ("parallel", "parallel", "arbitrary"))),
    )(a, g, *([] if into is None else [into]), *bg_args)
    return _results(res, 1, bg)


def ffn_in_swiglu(h, w_blk, *, name, tm, bg=None):
    M, K = h.shape
    nb, _, Nb = w_blk.shape
    nh = nb // 2
    F = nh * Nb
    tm = _tile(M, tm, 8)
    grid = (nh, M // tm)
    bg_in_specs, bg_args, bg_out_specs, bg_out_shapes, bg_scratch, split = _carrier(bg, 3, 3, 0, grid)

    def body(*refs):
        (h_ref, wg_ref, wu_ref), (act_ref, gate_ref, up_ref), _, run_background = split(refs)
        run_background()
        hv = h_ref[...]
        gate = jnp.dot(hv, wg_ref[...], preferred_element_type=F32)
        up = jnp.dot(hv, wu_ref[...], preferred_element_type=F32)
        gate_ref[...] = gate
        up_ref[...] = up
        act_ref[...] = (gate * jax.nn.sigmoid(gate) * up).astype(act_ref.dtype)

    o_spec = pl.BlockSpec((tm, Nb), lambda j, i: (i, j))
    res = pl.pallas_call(
        body, name=name, grid=grid,
        in_specs=[pl.BlockSpec((tm, K), lambda j, i: (i, 0)),
                  pl.BlockSpec((None, K, Nb), lambda j, i: (j, 0, 0)),
                  pl.BlockSpec((None, K, Nb), lambda j, i: (j + nh, 0, 0))] + bg_in_specs,
        out_specs=[o_spec, o_spec, o_spec] + bg_out_specs,
        out_shape=[jax.ShapeDtypeStruct((M, F), BF16), jax.ShapeDtypeStruct((M, F), F32),
                   jax.ShapeDtypeStruct((M, F), F32)] + bg_out_shapes,
        scratch_shapes=bg_scratch,
        compiler_params=_params(_semantics(bg, ("parallel", "parallel"))),
    )(h, w_blk, w_blk, *bg_args)
    return _results(res, 3, bg)


def _row_spec(tm, d, col=0):
    return pl.BlockSpec((tm, d), lambda i: (i, col))


def _vec_spec(d):
    return pl.BlockSpec((1, d), lambda i: (0, 0))


def norm_fwd(x, g, *, name, tm=512):
    L, D = x.shape
    tm = _tile(L, tm, 8)

    def body(x_ref, g_ref, h_ref):
        xv = x_ref[...]
        h_ref[...] = (xv * _rms(xv) * g_ref[...]).astype(h_ref.dtype)

    return pl.pallas_call(
        body, name=name, grid=(L // tm,),
        in_specs=[_row_spec(tm, D), _vec_spec(D)],
        out_specs=_row_spec(tm, D),
        out_shape=jax.ShapeDtypeStruct((L, D), BF16),
        compiler_params=_params(("parallel",)),
    )(x, g)


def norm_bwd(dh, xin, g, dres, *, name, want_bf16, tm=512):
    L, D = xin.shape
    tm = _tile(L, tm, 8)

    def body(dh_ref, x_ref, g_ref, dres_ref, dx_ref, *rest):
        dg_ref = rest[-1]
        xv, dhv = x_ref[...], dh_ref[...]
        r = _rms(xv)
        dx = dres_ref[...] + _rmsnorm_bwd(dhv, xv, r, g_ref[...])
        dx_ref[...] = dx
        if want_bf16:
            rest[0][...] = dx.astype(BF16)
        part = _rowsum(dhv * xv * r)

        @pl.when(pl.program_id(0) == 0)
        def _():
            dg_ref[...] = part

        @pl.when(pl.program_id(0) > 0)
        def _():
            dg_ref[...] += part

    out_specs = [_row_spec(tm, D)] + ([_row_spec(tm, D)] if want_bf16 else []) + [_vec_spec(D)]
    out_shape = ([jax.ShapeDtypeStruct((L, D), F32)]
                 + ([jax.ShapeDtypeStruct((L, D), BF16)] if want_bf16 else [])
                 + [jax.ShapeDtypeStruct((1, D), F32)])
    return pl.pallas_call(
        body, name=name, grid=(L // tm,),
        in_specs=[_row_spec(tm, D), _row_spec(tm, D), _vec_spec(D), _row_spec(tm, D)],
        out_specs=out_specs, out_shape=out_shape,
        compiler_params=_params(("arbitrary",)),
    )(dh, xin, g, dres)


def glu_pre(y0, *, name, tm=512):
    L, D = y0.shape
    tm = _tile(L, tm, 8)

    def body(y_ref, o_ref):
        o_ref[...] = _gelu(y_ref[...]).astype(o_ref.dtype)

    return pl.pallas_call(
        body, name=name, grid=(L // tm,),
        in_specs=[_row_spec(tm, D)], out_specs=_row_spec(tm, D),
        out_shape=jax.ShapeDtypeStruct((L, D), BF16),
        compiler_params=_params(("parallel",)),
    )(y0)


def glu_post(y0, t, b_glu, g_a, *, name, tm=512):
    L, D = y0.shape
    tm = _tile(L, tm, 8)

    def body(y_ref, t_ref, b_ref, g_ref, o_ref):
        ya = _gelu(y_ref[...]) * jax.nn.sigmoid(t_ref[...] + b_ref[...])
        o_ref[...] = (ya * _rms(ya) * g_ref[...]).astype(o_ref.dtype)

    return pl.pallas_call(
        body, name=name, grid=(L // tm,),
        in_specs=[_row_spec(tm, D), _row_spec(tm, D), _vec_spec(D), _vec_spec(D)],
        out_specs=_row_spec(tm, D),
        out_shape=jax.ShapeDtypeStruct((L, D), BF16),
        compiler_params=_params(("parallel",)),
    )(y0, t, b_glu, g_a)


def glu_post_bwd(y0, t, b_glu, g_a, dycat, *, name, tm=512):
    L, D = y0.shape
    tm = _tile(L, tm, 8)

    def body(y_ref, t_ref, b_ref, g_ref, dn_ref, dt_ref, dd_ref, dga_ref, dbg_ref):
        ya1 = _gelu(y_ref[...])
        sg = jax.nn.sigmoid(t_ref[...] + b_ref[...])
        ya = ya1 * sg
        ra = _rms(ya)
        dn = dn_ref[...]
        dya = _rmsnorm_bwd(dn, ya, ra, g_ref[...])
        dt = dya * ya1 * sg * (1.0 - sg)
        dt_ref[...] = dt.astype(BF16)
        dd_ref[...] = dya * sg
        p_ga, p_bg = _rowsum(dn * ya * ra), _rowsum(dt)

        @pl.when(pl.program_id(0) == 0)
        def _():
            dga_ref[...] = p_ga
            dbg_ref[...] = p_bg

        @pl.when(pl.program_id(0) > 0)
        def _():
            dga_ref[...] += p_ga
            dbg_ref[...] += p_bg

    return pl.pallas_call(
        body, name=name, grid=(L // tm,),
        in_specs=[_row_spec(tm, D), _row_spec(tm, D), _vec_spec(D), _vec_spec(D), _row_spec(tm, D, 0)],
        out_specs=[_row_spec(tm, D), _row_spec(tm, D), _vec_spec(D), _vec_spec(D)],
        out_shape=[jax.ShapeDtypeStruct((L, D), BF16), jax.ShapeDtypeStruct((L, D), F32),
                   jax.ShapeDtypeStruct((1, D), F32), jax.ShapeDtypeStruct((1, D), F32)],
        compiler_params=_params(("arbitrary",)),
    )(y0, t, b_glu, g_a, dycat)


def head_and_loss(x2, gpre, b_g, pp, g_f, tgt, *, name, tm=256):
    L, D = x2.shape
    tm = _tile(L, tm, 8)

    def body(x2_ref, gp_ref, bg_ref, pp_ref, gf_ref, tg_ref,
             dx3_ref, dpre_ref, dpp_ref, loss_ref, dgf_ref, dbg_ref):
        gate = jax.nn.sigmoid(gp_ref[...] + bg_ref[...])
        ppv = pp_ref[...]
        x3 = x2_ref[...] + gate * ppv
        r = _rms(x3)
        xn = x3 * r
        gf = gf_ref[...]
        err = xn * gf - tg_ref[...]
        loss = 0.5 * jnp.sum(jnp.mean(err * err, axis=-1, keepdims=True), axis=0, keepdims=True)
        dout = err * (1.0 / D)
        dx3 = _rmsnorm_bwd(dout, x3, r, gf)
        dx3_ref[...] = dx3
        dpre = dx3 * ppv * gate * (1.0 - gate)
        dpre_ref[...] = dpre.astype(BF16)
        dpp_ref[...] = (dx3 * gate).astype(BF16)
        p_gf, p_bg = _rowsum(dout * xn), _rowsum(dpre)
        p_loss = jnp.broadcast_to(loss, loss_ref.shape)

        @pl.when(pl.program_id(0) == 0)
        def _():
            loss_ref[...] = p_loss
            dgf_ref[...] = p_gf
            dbg_ref[...] = p_bg

        @pl.when(pl.program_id(0) > 0)
        def _():
            loss_ref[...] += p_loss
            dgf_ref[...] += p_gf
            dbg_ref[...] += p_bg

    rs = _row_spec(tm, D)
    return pl.pallas_call(
        body, name=name, grid=(L // tm,),
        in_specs=[rs, rs, _vec_spec(D), rs, _vec_spec(D), rs],
        out_specs=[rs, rs, rs, pl.BlockSpec((8, LANE), lambda i: (0, 0)), _vec_spec(D), _vec_spec(D)],
        out_shape=[jax.ShapeDtypeStruct((L, D), F32), jax.ShapeDtypeStruct((L, D), BF16),
                   jax.ShapeDtypeStruct((L, D), BF16), jax.ShapeDtypeStruct((8, LANE), F32),
                   jax.ShapeDtypeStruct((1, D), F32), jax.ShapeDtypeStruct((1, D), F32)],
        compiler_params=_params(("arbitrary",)),
    )(x2, gpre, b_g, pp, g_f, tgt)


def ffn_out_bwd_swiglu(dx, w, gate, up, *, name, tm=512, tf=1408):
    M, D = dx.shape
    F = w.shape[0]
    tm, tf = _tile(M, tm, 8), _tile(F, tf)

    def body(dx_ref, w_ref, g_ref, u_ref, dg_ref, du_ref):
        da = lax.dot_general(dx_ref[...], w_ref[...], (((1,), (1,)), ((), ())), preferred_element_type=F32)
        gv = g_ref[...]
        sg = jax.nn.sigmoid(gv)
        dg_ref[...] = (da * u_ref[...] * sg * (1.0 + gv * (1.0 - sg))).astype(BF16)
        du_ref[...] = (da * gv * sg).astype(BF16)

    spec = pl.BlockSpec((tm, tf), lambda j, i: (i, j))
    return pl.pallas_call(
        body, name=name, grid=(F // tf, M // tm),
        in_specs=[pl.BlockSpec((tm, D), lambda j, i: (i, 0)), pl.BlockSpec((tf, D), lambda j, i: (j, 0)), spec, spec],
        out_specs=[spec, spec],
        out_shape=[jax.ShapeDtypeStruct((M, F), BF16), jax.ShapeDtypeStruct((M, F), BF16)],
        compiler_params=_params(("parallel", "parallel")),
    )(dx, w, gate, up)


def _sgu_forward_values(u1, v1, lng, lnb, w_ref, bs_ref, s_scr, heads, hd):
    xc = v1 - jnp.mean(v1, axis=-1, keepdims=True)
    r = lax.rsqrt(jnp.mean(xc * xc, axis=-1, keepdims=True) + EPS)
    xhat = xc * r
    v2 = xhat * lng + lnb
    tril = (lax.broadcasted_iota(jnp.int32, (SGU_CHUNK, SGU_CHUNK), 0)
            >= lax.broadcasted_iota(jnp.int32, (SGU_CHUNK, SGU_CHUNK), 1))
    for h in range(heads):
        wm = jnp.where(tril, w_ref[h], 0.0).astype(BF16)
        cols = slice(h * hd, (h + 1) * hd)
        s_scr[:, cols] = jnp.dot(wm, v2[:, cols].astype(BF16), preferred_element_type=F32) + bs_ref[h]
    return xhat, r, v2, tril


def sgu_fwd(z, ln_g, ln_b, w_s, b_s, g_b, *, name, d_sgu):
    L = z.shape[0]
    heads = w_s.shape[0]
    hd = d_sgu // heads

    def body(zu_ref, zv_ref, lng_ref, lnb_ref, w_ref, bs_ref, gb_ref, o_ref, s_scr):
        u1 = _gelu(zu_ref[...])
        _sgu_forward_values(u1, _gelu(zv_ref[...]), lng_ref[...], lnb_ref[...], w_ref, bs_ref, s_scr, heads, hd)
        yb = u1 * s_scr[...]
        o_ref[...] = (yb * _rms(yb) * gb_ref[...]).astype(o_ref.dtype)

    blk = lambda col: pl.BlockSpec((SGU_CHUNK, d_sgu), lambda n: (n, col))
    return pl.pallas_call(
        body, name=name, grid=(L // SGU_CHUNK,),
        in_specs=[blk(1), blk(2), _vec_spec(d_sgu), _vec_spec(d_sgu),
                  pl.BlockSpec(w_s.shape, lambda n: (0, 0, 0)), pl.BlockSpec(b_s.shape, lambda n: (0, 0, 0)),
                  _vec_spec(d_sgu)],
        out_specs=blk(0),
        out_shape=jax.ShapeDtypeStruct((L, d_sgu), BF16),
        scratch_shapes=[pltpu.VMEM((SGU_CHUNK, d_sgu), F32)],
        compiler_params=_params(("parallel",)),
    )(z, z, ln_g, ln_b, w_s, b_s, g_b)


def sgu_bwd(z, dycat, ln_g, ln_b, w_s, b_s, g_b, *, name, d_sgu):
    L = z.shape[0]
    heads = w_s.shape[0]
    hd = d_sgu // heads

    def body(zu_ref, zv_ref, dn_ref, lng_ref, lnb_ref, w_ref, bs_ref, gb_ref,
             dzu_ref, dzv_ref, dw_ref, dbs_ref, dlng_ref, dlnb_ref, dgb_ref, s_scr, dv_scr):
        first = pl.program_id(0) == 0
        lng = lng_ref[...]
        u1, du1 = _gelu_and_grad(zu_ref[...])
        v1, dv1_dz = _gelu_and_grad(zv_ref[...])
        xhat, r, v2, tril = _sgu_forward_values(u1, v1, lng, lnb_ref[...], w_ref, bs_ref, s_scr, heads, hd)
        s = s_scr[...]
        yb = u1 * s
        rb = _rms(yb)
        dn = dn_ref[...]
        dyb = _rmsnorm_bwd(dn, yb, rb, gb_ref[...])
        dzu_ref[...] = (dyb * s * du1).astype(BF16)
        ds = dyb * u1
        for h in range(heads):
            cols = slice(h * hd, (h + 1) * hd)
            ds_h = ds[:, cols]
            ds_hb = ds_h.astype(BF16)
            wm = jnp.where(tril, w_ref[h], 0.0).astype(BF16)
            dw_h = jnp.where(tril, lax.dot_general(ds_hb, v2[:, cols].astype(BF16), (((1,), (1,)), ((), ())),
                                                   preferred_element_type=F32), 0.0)
            db_h = jnp.sum(ds_h.T, axis=0, keepdims=True)
            dv_scr[:, cols] = lax.dot_general(wm, ds_hb, (((0,), (0,)), ((), ())), preferred_element_type=F32)

            @pl.when(first)
            def _():
                dw_ref[h] = dw_h
                dbs_ref[h] = db_h

            @pl.when(jnp.logical_not(first))
            def _():
                dw_ref[h] += dw_h
                dbs_ref[h] += db_h

        dv2 = dv_scr[...]
        dxh = dv2 * lng
        dv1 = r * (dxh - jnp.mean(dxh, axis=-1, keepdims=True)
                   - xhat * jnp.mean(dxh * xhat, axis=-1, keepdims=True))
        dzv_ref[...] = (dv1 * dv1_dz).astype(BF16)
        p_lng, p_lnb, p_gb = _rowsum(dv2 * xhat), _rowsum(dv2), _rowsum(dn * yb * rb)

        @pl.when(first)
        def _():
            dlng_ref[...] = p_lng
            dlnb_ref[...] = p_lnb
            dgb_ref[...] = p_gb

        @pl.when(jnp.logical_not(first))
        def _():
            dlng_ref[...] += p_lng
            dlnb_ref[...] += p_lnb
            dgb_ref[...] += p_gb

    blk = lambda col: pl.BlockSpec((SGU_CHUNK, d_sgu), lambda n: (n, col))
    full3 = lambda shape: pl.BlockSpec(shape, lambda n: (0, 0, 0))
    return pl.pallas_call(
        body, name=name, grid=(L // SGU_CHUNK,),
        in_specs=[blk(1), blk(2), blk(1), _vec_spec(d_sgu), _vec_spec(d_sgu),
                  full3(w_s.shape), full3(b_s.shape), _vec_spec(d_sgu)],
        out_specs=[blk(0), blk(0), full3(w_s.shape), full3((heads, 1, SGU_CHUNK)),
                   _vec_spec(d_sgu), _vec_spec(d_sgu), _vec_spec(d_sgu)],
        out_shape=[jax.ShapeDtypeStruct((L, d_sgu), BF16), jax.ShapeDtypeStruct((L, d_sgu), BF16),
                   jax.ShapeDtypeStruct(w_s.shape, F32), jax.ShapeDtypeStruct((heads, 1, SGU_CHUNK), F32),
                   jax.ShapeDtypeStruct((1, d_sgu), F32), jax.ShapeDtypeStruct((1, d_sgu), F32),
                   jax.ShapeDtypeStruct((1, d_sgu), F32)],
        scratch_shapes=[pltpu.VMEM((SGU_CHUNK, d_sgu), F32), pltpu.VMEM((SGU_CHUNK, d_sgu), F32)],
        compiler_params=_params(("arbitrary",)),
    )(z, z, dycat, ln_g, ln_b, w_s, b_s, g_b)


def _disc_lambda(lam_re, lam_im, log_step):
    lr = jnp.minimum(lam_re, LAMBDA_RE_MAX)
    li = lam_im
    dt = jnp.exp(log_step)
    mag = jnp.exp(lr * dt)
    ang = li * dt
    a_re = mag * jnp.cos(ang)
    a_im = mag * jnp.sin(ang)
    nr = a_re - 1.0
    ni = a_im
    den = lr * lr + li * li
    return a_re, a_im, (nr * lr + ni * li) / den, (ni * lr - nr * li) / den


def _disc_b(q_re, q_im, b_re, b_im):
    return q_re * b_re - q_im * b_im, q_re * b_im + q_im * b_re


def disc_lambda_fwd(lam_re, lam_im, log_step, *, name):
    def body(lr_ref, li_ref, ls_ref, ar_ref, ai_ref, qr_ref, qi_ref):
        ar_ref[...], ai_ref[...], qr_ref[...], qi_ref[...] = _disc_lambda(lr_ref[...], li_ref[...], ls_ref[...])

    sd = jax.ShapeDtypeStruct(lam_re.shape, F32)
    return pl.pallas_call(body, name=name, out_shape=[sd, sd, sd, sd], compiler_params=_params())(
        lam_re, lam_im, log_step)


def disc_lambda_bwd(lam_re, lam_im, log_step, cts, *, name):
    def body(lr_ref, li_ref, ls_ref, c0, c1, c2, c3, dlr_ref, dli_ref, dls_ref):
        _, vjp = jax.vjp(_disc_lambda, lr_ref[...], li_ref[...], ls_ref[...])
        dlr_ref[...], dli_ref[...], dls_ref[...] = vjp((c0[...], c1[...], c2[...], c3[...]))

    sd = jax.ShapeDtypeStruct(lam_re.shape, F32)
    return pl.pallas_call(body, name=name, out_shape=[sd, sd, jax.ShapeDtypeStruct(log_step.shape, F32)],
                          compiler_params=_params())(lam_re, lam_im, log_step, *cts)


def disc_b_fwd(q_re, q_im, b_re, b_im, *, name):
    def body(qr_ref, qi_ref, br_ref, bi_ref, or_ref, oi_ref):
        or_ref[...], oi_ref[...] = _disc_b(qr_ref[...], qi_ref[...], br_ref[...], bi_ref[...])

    sd = jax.ShapeDtypeStruct(b_re.shape, F32)
    return pl.pallas_call(body, name=name, out_shape=[sd, sd], compiler_params=_params())(q_re, q_im, b_re, b_im)


def disc_b_bwd(q_re, q_im, b_re, b_im, ct_re, ct_im, *, name):
    def body(qr_ref, qi_ref, br_ref, bi_ref, cr_ref, ci_ref, dqr_ref, dqi_ref, dbr_ref, dbi_ref):
        _, vjp = jax.vjp(_disc_b, qr_ref[...], qi_ref[...], br_ref[...], bi_ref[...])
        dqr_ref[...], dqi_ref[...], dbr_ref[...], dbi_ref[...] = vjp((cr_ref[...], ci_ref[...]))

    sq, sb = jax.ShapeDtypeStruct(q_re.shape, F32), jax.ShapeDtypeStruct(b_re.shape, F32)
    return pl.pallas_call(body, name=name, out_shape=[sq, sq, sb, sb], compiler_params=_params())(
        q_re, q_im, b_re, b_im, ct_re, ct_im)


def _lti_scan(xr, xi, ar, ai, reverse):
    T = xr.shape[0]
    row = lax.broadcasted_iota(jnp.int32, xr.shape, 0)
    k = 1
    while k < T:
        shift = T - k if reverse else k
        keep = (row < T - k) if reverse else (row >= k)
        sr = jnp.where(keep, pltpu.roll(xr, shift, 0), 0.0)
        si = jnp.where(keep, pltpu.roll(xi, shift, 0), 0.0)
        xr, xi = xr + ar * sr - ai * si, xi + ar * si + ai * sr
        ar, ai = ar * ar - ai * ai, 2.0 * ar * ai
        k *= 2
    return xr, xi


SUBLANES = 8


def _scan_rows(x_re, x_im, o_re, o_im, ar, ai, cr, ci, reverse):
    T, n = x_re.shape
    groups = T // SUBLANES
    row = lax.broadcasted_iota(jnp.int32, (SUBLANES, n), 0)
    edge = SUBLANES - 1 if reverse else 0
    pr, pi = _lti_scan(jnp.where(row == edge, ar, 0.0), jnp.where(row == edge, ai, 0.0), ar, ai, reverse)
    pows = []
    for level in range(3):
        k = 1 << level
        keep = (row < SUBLANES - k) if reverse else (row >= k)
        pows.append((jnp.where(keep, ar, 0.0), jnp.where(keep, ai, 0.0)))
        ar, ai = ar * ar - ai * ai, 2.0 * ar * ai

    def group(i, carry):
        cr, ci = carry
        at = pl.multiple_of((groups - 1 - i if reverse else i) * SUBLANES, SUBLANES)
        xr, xi = x_re[pl.ds(at, SUBLANES), :], x_im[pl.ds(at, SUBLANES), :]
        for level, (qr, qi) in enumerate(pows):
            shift = SUBLANES - (1 << level) if reverse else 1 << level
            sr, si = pltpu.roll(xr, shift, 0), pltpu.roll(xi, shift, 0)
            xr, xi = xr + qr * sr - qi * si, xi + qr * si + qi * sr
        xr, xi = xr + pr * cr - pi * ci, xi + pr * ci + pi * cr
        o_re[pl.ds(at, SUBLANES), :] = xr
        o_im[pl.ds(at, SUBLANES), :] = xi
        return spread(xr[last:last + 1, :]), spread(xi[last:last + 1, :])

    last = 0 if reverse else SUBLANES - 1
    spread = lambda v: jnp.broadcast_to(v, (SUBLANES, n))
    cr, ci = lax.fori_loop(0, groups, group, (spread(cr), spread(ci)), unroll=2)
    return cr[0:1, :], ci[0:1, :]


def _ssm_chunk(L):
    return _tile(L, 512, 8)


def _same_group(rows, cols):
    r = lax.broadcasted_iota(jnp.int32, (rows, cols), 0) // SSM_GROUP
    c = lax.broadcasted_iota(jnp.int32, (rows, cols), 1) // SSM_STATE
    return r == c


def _expand_groups(compact):
    H, S = compact.shape
    tiled = jnp.concatenate([compact] * (S // SSM_STATE), axis=0)
    return jnp.where(_same_group(tiled.shape[0], S), tiled, 0.0).astype(BF16)


def _collapse_groups(dense):
    C, S = dense.shape
    masked = jnp.where(_same_group(C, S), dense, 0.0)
    total = masked[0:SSM_GROUP]
    for g in range(1, C // SSM_GROUP):
        total = total + masked[g * SSM_GROUP:(g + 1) * SSM_GROUP]
    return total


def ssm_fwd(z, bt_re, bt_im, ct_re, ct_im, a_re, a_im, d, *, name, bg=None):
    L = z.shape[0]
    NK, _, S = a_re.shape
    H = bt_re.shape[0]
    C = S // SSM_STATE * SSM_GROUP
    T = _ssm_chunk(L)
    grid = (NK, L // T)
    bg_in_specs, bg_args, bg_out_specs, bg_out_shapes, bg_scratch, split = _carrier(bg, 8, 3, 8, grid)
    nt_dot = lambda p, q: lax.dot_general(p, q, (((1,), (1,)), ((), ())), preferred_element_type=F32)

    def body(*refs):
        ((u_ref, btr_ref, bti_ref, ctr_ref, cti_ref, ar_ref, ai_ref, d_ref), (y_ref, sr_ref, si_ref),
         (car_re, car_im, bu_re, bu_im, b_re, b_im, c_re, c_im), run_background) = split(refs)
        run_background()
        i = pl.program_id(1)
        ar, ai = ar_ref[...], ai_ref[...]

        @pl.when(i == 0)
        def _():
            car_re[...] = jnp.zeros_like(car_re)
            car_im[...] = jnp.zeros_like(car_im)
            b_re[...] = _expand_groups(btr_ref[...])
            b_im[...] = _expand_groups(bti_ref[...])
            c_re[...] = _expand_groups(ctr_ref[...])
            c_im[...] = _expand_groups(cti_ref[...])

        u = u_ref[...]
        ub = u.astype(BF16)
        bu_re[...] = jnp.dot(ub, b_re[...], preferred_element_type=F32)
        bu_im[...] = jnp.dot(ub, b_im[...], preferred_element_type=F32)
        car_re[...], car_im[...] = _scan_rows(bu_re, bu_im, sr_ref, si_ref, ar, ai, car_re[...], car_im[...], False)
        y_ref[...] = (nt_dot(sr_ref[...].astype(BF16), c_re[...]) - nt_dot(si_ref[...].astype(BF16), c_im[...])
                      + d_ref[...] * u)

    kspec = lambda shape: pl.BlockSpec((None,) + shape, lambda k, i: (k, 0, 0))
    compact = pl.BlockSpec((H, S), lambda k, i: (0, k))
    res = pl.pallas_call(
        body, name=name, grid=grid,
        in_specs=[pl.BlockSpec((T, C), lambda k, i: (i, k)), compact, compact, compact, compact,
                  kspec((1, S)), kspec((1, S)), kspec((1, C))] + bg_in_specs,
        out_specs=[pl.BlockSpec((T, C), lambda k, i: (i, k)),
                   pl.BlockSpec((T, S), lambda k, i: (i, k)), pl.BlockSpec((T, S), lambda k, i: (i, k))] + bg_out_specs,
        out_shape=[jax.ShapeDtypeStruct((L, NK * C), F32), jax.ShapeDtypeStruct((L, NK * S), F32),
                   jax.ShapeDtypeStruct((L, NK * S), F32)] + bg_out_shapes,
        scratch_shapes=[pltpu.VMEM((1, S), F32), pltpu.VMEM((1, S), F32),
                        pltpu.VMEM((T, S), F32), pltpu.VMEM((T, S), F32)]
        + [pltpu.VMEM((C, S), BF16)] * 4 + bg_scratch,
        compiler_params=_params(_semantics(bg, ("parallel", "arbitrary"))),
    )(z, bt_re, bt_im, ct_re, ct_im, a_re, a_im, d, *bg_args)
    return _results(res, 3, bg)


def ssm_bwd(z, y0, dd_direct, dd_mm, s_re, s_im, bt_re, bt_im, ct_re, ct_im, a_re, a_im, d, *, name, bg=None):
    L = z.shape[0]
    NK, _, S = a_re.shape
    H = bt_re.shape[0]
    C = S // SSM_STATE * SSM_GROUP
    T = _ssm_chunk(L)
    nchunk = L // T
    tail = T // 8
    grid = (NK, nchunk)
    bg_in_specs, bg_args, bg_out_specs, bg_out_shapes, bg_scratch, split = _carrier(bg, 15, 8, 12, grid)
    nt_dot = lambda p, q: lax.dot_general(p, q, (((1,), (1,)), ((), ())), preferred_element_type=F32)

    def body(*refs):
        ((u_ref, y_ref, d1_ref, d2_ref, sr_ref, si_ref, pr_ref, pi_ref,
          btr_ref, bti_ref, ctr_ref, cti_ref, ar_ref, ai_ref, d_ref),
         (du_ref, dbr_ref, dbi_ref, dcr_ref, dci_ref, dar_ref, dai_ref, dd_ref),
         (car_re, car_im, lam_re, lam_im, b_re, b_im, c_re, c_im, acc_br, acc_bi, acc_cr, acc_ci),
         run_background) = split(refs)
        run_background()
        i = pl.program_id(1)
        chunk = nchunk - 1 - i
        ar, ai = ar_ref[...], ai_ref[...]
        row = lax.broadcasted_iota(jnp.int32, (T, S), 0)

        @pl.when(i == 0)
        def _():
            car_re[...] = jnp.zeros_like(car_re)
            car_im[...] = jnp.zeros_like(car_im)
            b_re[...] = _expand_groups(btr_ref[...])
            b_im[...] = _expand_groups(bti_ref[...])
            c_re[...] = _expand_groups(ctr_ref[...])
            c_im[...] = _expand_groups(cti_ref[...])

        u = u_ref[...]
        dy = (d1_ref[...] + d2_ref[...]) * _gelu_grad(y_ref[...])
        dyb = dy.astype(BF16)
        lam_re[...] = jnp.dot(dyb, c_re[...], preferred_element_type=F32)
        lam_im[...] = -jnp.dot(dyb, c_im[...], preferred_element_type=F32)
        car_re[...], car_im[...] = _scan_rows(lam_re, lam_im, lam_re, lam_im, ar, -ai, car_re[...], car_im[...], True)
        lr, li = lam_re[...], lam_im[...]

        s_re, s_im = sr_ref[...], si_ref[...]
        has_prev = (chunk > 0).astype(F32)
        prev_re = pr_ref[7:8, :] * has_prev
        prev_im = pi_ref[7:8, :] * has_prev
        sp_re = jnp.where(row == 0, prev_re, pltpu.roll(s_re, 1, 0))
        sp_im = jnp.where(row == 0, prev_im, pltpu.roll(s_im, 1, 0))
        p_ar = _rowsum(lr * sp_re + li * sp_im)
        p_ai = _rowsum(li * sp_re - lr * sp_im)

        lrb, lib, ub = lr.astype(BF16), li.astype(BF16), u.astype(BF16)
        du = dy * d_ref[...] + nt_dot(lrb, b_re[...]) + nt_dot(lib, b_im[...])
        du_ref[...] = du.astype(BF16)
        tdot = lambda p, q: lax.dot_general(p, q, (((0,), (0,)), ((), ())), preferred_element_type=F32)
        p_br, p_bi = tdot(ub, lrb), tdot(ub, lib)
        p_cr, p_ci = tdot(dyb, s_re.astype(BF16)), -tdot(dyb, s_im.astype(BF16))
        p_dd = _rowsum(dy * u)

        @pl.when(i == 0)
        def _():
            dar_ref[...] = p_ar
            dai_ref[...] = p_ai
            acc_br[...] = p_br
            acc_bi[...] = p_bi
            acc_cr[...] = p_cr
            acc_ci[...] = p_ci
            dd_ref[...] = p_dd

        @pl.when(i > 0)
        def _():
            dar_ref[...] += p_ar
            dai_ref[...] += p_ai
            acc_br[...] += p_br
            acc_bi[...] += p_bi
            acc_cr[...] += p_cr
            acc_ci[...] += p_ci
            dd_ref[...] += p_dd

        @pl.when(i == nchunk - 1)
        def _():
            dbr_ref[...] = _collapse_groups(acc_br[...])
            dbi_ref[...] = _collapse_groups(acc_bi[...])
            dcr_ref[...] = _collapse_groups(acc_cr[...])
            dci_ref[...] = _collapse_groups(acc_ci[...])

    rev = lambda k, i: (nchunk - 1 - i, k)
    prev = lambda k, i: (jnp.maximum((nchunk - 1 - i) * tail - 1, 0), k)
    kspec = lambda shape: pl.BlockSpec((None,) + shape, lambda k, i: (k, 0, 0))
    compact = pl.BlockSpec((H, S), lambda k, i: (0, k))
    compact_shape = jax.ShapeDtypeStruct((H, NK * S), F32)
    res = pl.pallas_call(
        body, name=name, grid=grid,
        in_specs=[pl.BlockSpec((T, C), rev), pl.BlockSpec((T, C), rev), pl.BlockSpec((T, C), rev),
                  pl.BlockSpec((T, C), rev), pl.BlockSpec((T, S), rev), pl.BlockSpec((T, S), rev),
                  pl.BlockSpec((8, S), prev), pl.BlockSpec((8, S), prev),
                  compact, compact, compact, compact,
                  kspec((1, S)), kspec((1, S)), kspec((1, C))] + bg_in_specs,
        out_specs=[pl.BlockSpec((T, C), rev), compact, compact, compact, compact,
                   kspec((1, S)), kspec((1, S)), kspec((1, C))] + bg_out_specs,
        out_shape=[jax.ShapeDtypeStruct((L, NK * C), BF16),
                   compact_shape, compact_shape, compact_shape, compact_shape,
                   jax.ShapeDtypeStruct((NK, 1, S), F32), jax.ShapeDtypeStruct((NK, 1, S), F32),
                   jax.ShapeDtypeStruct((NK, 1, C), F32)] + bg_out_shapes,
        scratch_shapes=[pltpu.VMEM((1, S), F32), pltpu.VMEM((1, S), F32),
                        pltpu.VMEM((T, S), F32), pltpu.VMEM((T, S), F32)]
        + [pltpu.VMEM((C, S), BF16)] * 4 + [pltpu.VMEM((C, S), F32)] * 4 + bg_scratch,
        compiler_params=_params(_semantics(bg, ("parallel", "arbitrary"))),
    )(z, y0, dd_direct, dd_mm, s_re, s_im, s_re, s_im, bt_re, bt_im, ct_re, ct_im, a_re, a_im, d, *bg_args)
    return _results(res, 8, bg)


def _position():
    return lax.axis_index("x"), lax.axis_index("y"), lax.axis_index("c")


def _other_chips(x, y):
    return [(1 - x, y), (x, 1 - y), (1 - x, 1 - y)]


def _gather_phases(n, rows=None):
    def parts(ins, outs, sems):
        send_sems, recv_sems, local_sems = sems
        x, y, c = _position()
        me, sibling = (x, y, c), (x, y, 1 - c)
        chips = _other_chips(x, y)

        def block(a, pos):
            index = 4 * pos[0] + 2 * pos[1] + pos[2]
            if rows is not None:
                return outs[a].at[index, pl.ds(*rows)]
            return outs[a].at[pl.ds(index, 1)] if _is_row(ins[a]) else outs[a].at[index]

        def copy(a, k, pos, to, src=None):
            return pltpu.make_async_remote_copy(
                src_ref=block(a, pos) if src is None else src, dst_ref=block(a, pos),
                send_sem=send_sems.at[7 * a + k], recv_sem=recv_sems.at[7 * a + k],
                device_id=to, device_id_type=MESH)

        shard = [ins[a] if rows is None else ins[a].at[pl.ds(*rows)] for a in range(n)]
        mine = [pltpu.make_async_copy(shard[a], block(a, me), local_sems.at[a]) for a in range(n)]
        first = []
        for a in range(n):
            first.append(copy(a, 0, me, sibling, src=shard[a]))
            first += [copy(a, 1 + j, me, (*chip, c), src=shard[a]) for j, chip in enumerate(chips)]
        passed = [copy(a, 4 + j, (*chip, c), sibling) for a in range(n) for j, chip in enumerate(chips)]
        arrived = [copy(a, 1 + j, (*chip, c), me) for a in range(n) for j, chip in enumerate(chips)]
        from_sibling = []
        for a in range(n):
            from_sibling.append(copy(a, 0, sibling, me))
            from_sibling += [copy(a, 4 + j, (*chip, 1 - c), me) for j, chip in enumerate(chips)]
        return mine, first, passed, arrived, from_sibling

    def send(ins, outs, sems):
        mine, first, _, _, _ = parts(ins, outs, sems)
        for cp in mine + first:
            cp.start()

    def forward(ins, outs, sems):
        _, _, passed, arrived, _ = parts(ins, outs, sems)
        for got, fwd in zip(arrived, passed):
            got.wait_recv()
            fwd.start()

    def finish(ins, outs, sems):
        mine, first, passed, _, from_sibling = parts(ins, outs, sems)
        for cp in from_sibling:
            cp.wait_recv()
        for cp in first + passed:
            cp.wait_send()
        for cp in mine:
            cp.wait()

    return [(0.0, send), (GATHER_FORWARD_AT, forward), (1.0, finish)]


def _is_row(a):
    return len(a.shape) == 2 and a.shape[0] == 1


def _gather_shapes(shards):
    n = len(shards)
    return ([jax.ShapeDtypeStruct((N_DEV,) + (s.shape[1:] if _is_row(s) else s.shape), s.dtype) for s in shards],
            [pltpu.SemaphoreType.DMA((7 * n,)), pltpu.SemaphoreType.DMA((7 * n,)), pltpu.SemaphoreType.DMA((n,))])


def gather_background(shards, rows=None, into=None):
    out_shapes, scratch = _gather_shapes(shards)
    bg = Background(list(shards) + list(into or []), out_shapes, scratch, _gather_phases(len(shards), rows))
    bg.aliases = {len(shards) + k: k for k in range(len(into or []))}
    return bg


def all_gather_blocks(shards, *, name):
    n = len(shards)
    out_shapes, scratch = _gather_shapes(shards)

    def body(*refs):
        for _, phase in _gather_phases(n):
            phase(refs[:n], refs[n:2 * n], refs[2 * n:])

    return pl.pallas_call(
        body, name=name, in_specs=[ANY] * n, out_specs=[ANY] * n, out_shape=out_shapes, scratch_shapes=scratch,
    )(*shards)


def sibling_exchange(grads, *, name):
    n = len(grads)
    bg = sibling_exchange_background(grads)

    def body(*refs):
        for _, phase in bg.phases:
            phase(refs[:n], refs[n:2 * n], refs[2 * n:])

    return pl.pallas_call(
        body, name=name, in_specs=[ANY] * n, out_specs=[ANY] * n, out_shape=bg.out_shapes, scratch_shapes=bg.scratch,
    )(*grads)


def sibling_exchange_background(grads):
    n = len(grads)

    def copies(ins, outs, sems):
        x, y, c = _position()
        return [pltpu.make_async_remote_copy(
            src_ref=ins[a].at[2 * q + 1 - c], dst_ref=outs[a].at[q],
            send_sem=sems[0].at[4 * a + q], recv_sem=sems[1].at[4 * a + q],
            device_id=(x, y, 1 - c), device_id_type=MESH)
            for a in range(n) for q in range(4)]

    def send(ins, outs, sems):
        for cp in copies(ins, outs, sems):
            cp.start()

    def finish(ins, outs, sems):
        for cp in copies(ins, outs, sems):
            cp.wait()

    return Background(grads, [jax.ShapeDtypeStruct((4,) + g.shape[1:], g.dtype) for g in grads],
                      [pltpu.SemaphoreType.DMA((4 * n,)), pltpu.SemaphoreType.DMA((4 * n,))],
                      [(0.0, send), (1.0, finish)])


def _chip_exchange_phases(n):
    def copies(ins, outs, sems):
        x, y, c = _position()
        return [pltpu.make_async_remote_copy(
            src_ref=ins[a].at[2 * chip[0] + chip[1]], dst_ref=outs[a].at[j],
            send_sem=sems[0].at[3 * a + j], recv_sem=sems[1].at[3 * a + j],
            device_id=(*chip, c), device_id_type=MESH)
            for a in range(n) for j, chip in enumerate(_other_chips(x, y))]

    def send(ins, outs, sems):
        for cp in copies(ins, outs, sems):
            cp.start()

    def finish(ins, outs, sems):
        for cp in copies(ins, outs, sems):
            cp.wait()

    return [(0.0, send), (1.0, finish)]


def chip_exchange_background(parts):
    n = len(parts)
    return Background(parts, [jax.ShapeDtypeStruct((3,) + p.shape[1:], p.dtype) for p in parts],
                      [pltpu.SemaphoreType.DMA((3 * n,)), pltpu.SemaphoreType.DMA((3 * n,))],
                      _chip_exchange_phases(n))


def add_pairs(grads, theirs, core, *, name, tm=512):
    _, R, C = theirs.shape
    tm = _tile(R, tm, 16)

    def body(core_ref, a_ref, b_ref, o_ref):
        o_ref[...] = (a_ref[...].astype(F32) + b_ref[...].astype(F32)).astype(o_ref.dtype)

    spec = pl.BlockSpec((None, tm, C), lambda q, i, core_ref: (q, i, 0))
    return pl.pallas_call(
        body, name=name,
        grid_spec=pltpu.PrefetchScalarGridSpec(
            num_scalar_prefetch=1, grid=(4, R // tm),
            in_specs=[pl.BlockSpec((None, tm, C), lambda q, i, core_ref: (2 * q + core_ref[0], i, 0)), spec],
            out_specs=spec),
        out_shape=jax.ShapeDtypeStruct(theirs.shape, BF16),
        compiler_params=_params(("parallel", "parallel")),
    )(core, grads, theirs)


def _adamw(w, g, m, v):
    m = ADAM_B1 * m + (1.0 - ADAM_B1) * g
    v = ADAM_B2 * v + (1.0 - ADAM_B2) * (g * g)
    m_hat = m / (1.0 - ADAM_B1 ** ADAM_STEP)
    v_hat = v / (1.0 - ADAM_B2 ** ADAM_STEP)
    delta = -ADAM_LR * (m_hat / (jnp.sqrt(v_hat) + ADAM_EPS) + ADAM_WD * w)
    return delta, m, v


def adamw_sharded(w, m, v, grads, theirs, others, where, *, name, tm=256):
    R, C = w.shape
    tm = _tile(R, tm, 16)

    def body(where_ref, w_ref, m_ref, v_ref, a_ref, b_ref, o_ref, g_ref, d_ref, nm_ref, nv_ref):
        g = a_ref[...].astype(F32) + b_ref[...].astype(F32)
        for j in range(3):
            g = g + o_ref[j].astype(F32)
        g_ref[...] = g
        d_ref[...], nm_ref[...], nv_ref[...] = _adamw(w_ref[...], g, m_ref[...], v_ref[...])

    spec = pl.BlockSpec((tm, C), lambda i, where_ref: (i, 0))
    sd = jax.ShapeDtypeStruct((R, C), F32)
    return pl.pallas_call(
        body, name=name,
        grid_spec=pltpu.PrefetchScalarGridSpec(
            num_scalar_prefetch=1, grid=(R // tm,),
            in_specs=[spec, spec, spec,
                      pl.BlockSpec((None, tm, C), lambda i, where_ref: (where_ref[0], i, 0)),
                      pl.BlockSpec((None, tm, C), lambda i, where_ref: (where_ref[1], i, 0)),
                      pl.BlockSpec((3, tm, C), lambda i, where_ref: (0, i, 0))],
            out_specs=[spec, spec, spec, spec]),
        out_shape=[sd, sd, sd, sd],
        compiler_params=_params(("parallel",)),
    )(where, w, m, v, grads, theirs, others)


def sum_gathered(gathered, *, name):
    n = len(gathered)

    def body(*refs):
        for ga_ref, o_ref in zip(refs[:n], refs[n:]):
            rows = len(ga_ref.shape) == 2
            total = ga_ref[0:1] if rows else ga_ref[0]
            for dev in range(1, N_DEV):
                total = total + (ga_ref[dev:dev + 1] if rows else ga_ref[dev])
            o_ref[...] = total

    shapes = [jax.ShapeDtypeStruct((1,) + g.shape[1:] if g.ndim == 2 else g.shape[1:], F32) for g in gathered]
    return pl.pallas_call(body, name=name, out_shape=shapes, compiler_params=_params())(*gathered)


def adamw_replicated(ws, ms, vs, gs, *, name):
    n = len(ws)

    def body(*refs):
        w_refs, m_refs, v_refs, g_refs = refs[:n], refs[n:2 * n], refs[2 * n:3 * n], refs[3 * n:4 * n]
        outs = refs[4 * n:]
        for k in range(n):
            outs[k][...], outs[n + k][...], outs[2 * n + k][...] = _adamw(
                w_refs[k][...], g_refs[k][...], m_refs[k][...], v_refs[k][...])

    shapes = [jax.ShapeDtypeStruct(t.shape, F32) for t in ws]
    res = pl.pallas_call(body, name=name, out_shape=shapes * 3, compiler_params=_params())(*ws, *ms, *vs, *gs)
    return res[:n], res[n:2 * n], res[2 * n:]


SHARDED = ("w_in", "ssm_glu_w", "w_out", "w_ffn_in", "w_ffn_out", "w_ple_gate", "w_ple_proj")
SMALL_LAST = ("norm_mix_g",)
SMALL_WIDE = ("ssm_b_re", "ssm_b_im", "ssm_c_re", "ssm_c_im")
SMALL = ("ssm_lambda_re", "ssm_lambda_im", "ssm_log_step", "ssm_b_re", "ssm_b_im", "ssm_c_re",
         "ssm_c_im", "ssm_d", "ssm_glu_b", "sgu_ln_g", "sgu_ln_b", "sgu_w", "sgu_b", "out_norm_ssm_g",
         "out_norm_sgu_g", "norm_ffn_g", "norm_ple_g", "b_ple_gate", "final_norm_g")
WEIGHTS = ("norm_mix_g", "w_in", "ssm_lambda_re", "ssm_lambda_im", "ssm_log_step", "ssm_b_re", "ssm_b_im",
           "ssm_c_re", "ssm_c_im", "ssm_d", "ssm_glu_w", "ssm_glu_b", "sgu_ln_g", "sgu_ln_b", "sgu_w", "sgu_b",
           "out_norm_ssm_g", "out_norm_sgu_g", "w_out", "norm_ffn_g", "w_ffn_in", "w_ffn_out", "norm_ple_g",
           "w_ple_gate", "b_ple_gate", "w_ple_proj", "final_norm_g")


def _step(x, p, loss_target, w, m, v):
    L, D = x.shape[1], x.shape[2]
    x2d, p2d, tgt = x.reshape(L, D), p.reshape(L, -1), loss_target.reshape(L, D)
    d_ssm = w["ssm_glu_w"].shape[2]
    d_sgu = w["sgu_ln_g"].shape[1]
    G, P, H = w["ssm_b_re"].shape[1:]
    SG = min(SSM_SUPER, G)
    NK = G // SG
    row = lambda a: a.reshape(1, -1)

    shard2d = {n: w[n].reshape(w[n].shape[1:]) for n in SHARDED}
    shard_bf = {n: shard2d[n].astype(BF16) for n in SHARDED}
    (w_ple_blk,) = all_gather_blocks([shard_bf["w_ple_proj"]], name="gather_w_ple")
    bf = lambda t: t.astype(BF16)
    pp, (w_in_blk,) = mm_nn(bf(p2d), w_ple_blk, name="ple_proj", out_dtype=F32, tm=512, tn=512, tk=2048,
                            bg=gather_background([shard_bf["w_in"]]))
    w_in = jnp.transpose(w_in_blk, (1, 0, 2)).reshape(D, -1)
    F = shard2d["w_ffn_in"].shape[1] * 4

    lam_re, lam_im, log_step = w["ssm_lambda_re"][0], w["ssm_lambda_im"][0], w["ssm_log_step"][0].reshape(G, 1)
    a_re, a_im, q_re, q_im = disc_lambda_fwd(lam_re, lam_im, log_step, name="s5_discretise_lambda")
    bt_re = w["ssm_b_re"][0].transpose(2, 0, 1).reshape(H, G * P)
    bt_im = w["ssm_b_im"][0].transpose(2, 0, 1).reshape(H, G * P)
    bbar_re, bbar_im = disc_b_fwd(row(q_re), row(q_im), bt_re, bt_im, name="s5_discretise_b")
    ct_re = w["ssm_c_re"][0].transpose(1, 0, 2).reshape(H, G * P)
    ct_im = w["ssm_c_im"][0].transpose(1, 0, 2).reshape(H, G * P)
    a_re_k, a_im_k = a_re.reshape(NK, 1, SG * P), a_im.reshape(NK, 1, SG * P)
    d_k = w["ssm_d"][0].reshape(NK, 1, SG * H)

    h1 = norm_fwd(x2d, w["norm_mix_g"], name="norm_mix")
    z, (w_glu, w_out) = mm_nn(h1, w_in, name="in_proj", out_dtype=F32, tm=512, tn=1024, tk=2048,
                              bg=gather_background([shard_bf["ssm_glu_w"], shard_bf["w_out"]]))
    w_glu, w_out = w_glu.reshape(d_ssm, d_ssm), w_out.reshape(D, D)
    first_rows = (D * 11 // 16) // 16 * 16
    s5_mats = (bbar_re, bbar_im, ct_re, ct_im, a_re_k, a_im_k, d_k)
    (y0, s_re, s_im), (w_ffn_in_part,) = ssm_fwd(
        z, *s5_mats, name="s5_scan", bg=gather_background([shard_bf["w_ffn_in"]], rows=(0, first_rows)))
    ya1 = glu_pre(y0, name="s5_gelu")
    t_glu = mm_nn(ya1, w_glu, name="s5_glu_proj", out_dtype=F32, tm=512, tn=512, tk=2048)
    n_a = glu_post(y0, t_glu, w["ssm_glu_b"], w["out_norm_ssm_g"], name="s5_glu_norm")
    b_s3 = w["sgu_b"][0][:, :, None]
    n_b = sgu_fwd(z, w["sgu_ln_g"], w["sgu_ln_b"], w["sgu_w"][0], b_s3, w["out_norm_sgu_g"], name="sgu", d_sgu=d_sgu)
    ycat = jnp.concatenate([n_a, n_b], axis=1)
    x1, (w_ffn_in_blk,) = mm_nn(
        ycat, w_out, name="out_proj", out_dtype=F32, tm=512, tn=512, tk=2048, residual=x2d,
        bg=gather_background([shard_bf["w_ffn_in"]], rows=(first_rows, D - first_rows), into=[w_ffn_in_part]))
    h2 = norm_fwd(x1, w["norm_ffn_g"], name="norm_ffn")
    (act, gate_ff, up_ff), (w_ffn_out, w_gate) = ffn_in_swiglu(
        h2, w_ffn_in_blk, name="ffn_in_swiglu", tm=256,
        bg=gather_background([shard_bf["w_ffn_out"], shard_bf["w_ple_gate"]]))
    w_ffn_out, w_gate = w_ffn_out.reshape(F, D), w_gate.reshape(D, D)
    x2 = mm_nn(act, w_ffn_out, name="ffn_out", out_dtype=F32, tm=512, tn=512, tk=F, residual=x1)
    h3 = norm_fwd(x2, w["norm_ple_g"], name="norm_ple")
    gpre = mm_nn(h3, w_gate, name="ple_gate", out_dtype=F32, tm=512, tn=1024, tk=2048)

    dx3, dpre, dpp, loss_part, d_final_g, d_b_gate = head_and_loss(
        x2, gpre, w["b_ple_gate"], pp, row(w["final_norm_g"]), tgt, name="head_and_loss")
    x_pos, y_pos, c_pos = _position()
    where = jnp.stack([4 * x_pos + 2 * y_pos + c_pos, 2 * x_pos + y_pos]).astype(jnp.int32)
    core = jnp.reshape(c_pos, (1,)).astype(jnp.int32)
    own, others = {}, {}

    def blocks(named):
        g8 = {n: t.reshape((N_DEV,) + shard2d[n].shape) for n, t in named.items()}
        return g8, sibling_exchange_background(list(g8.values()))

    def chip_sums(g8, theirs):
        own.update(zip(g8, zip(g8.values(), theirs)))
        return [add_pairs(g, t, core, name="chip_sum_" + n) for (n, g), t in zip(g8.items(), theirs)]

    d_w_gate = mm_tn(h3, dpre, name="d_w_ple_gate", out_dtype=BF16, tm=L, tko=1024, tno=1024)
    d_w_ple = mm_tn(bf(p2d), dpp, name="d_w_ple_proj", out_dtype=BF16, tm=L, tko=1024, tno=1024, out_blocks=N_DEV)
    g8_ple, bg = blocks({"w_ple_gate": d_w_gate, "w_ple_proj": d_w_ple})
    dh3, theirs = mm_nt(dpre, w_gate, name="d_h_ple", out_dtype=F32, tm=512, tko=1024, tc=2048, bg=bg)
    bg = chip_exchange_background(chip_sums(g8_ple, theirs))
    dx2, dx2b, d_ple_g = norm_bwd(dh3, x2, w["norm_ple_g"], dx3, name="d_norm_ple", want_bf16=True)
    d_w_ffn_out, got = mm_tn(act, dx2b, name="d_w_ffn_out", out_dtype=BF16, tm=L, tko=1408, tno=512, bg=bg)
    others.update(zip(g8_ple, got))
    g8_fo, bg = blocks({"w_ffn_out": d_w_ffn_out})
    dgate, dup = ffn_out_bwd_swiglu(dx2b, w_ffn_out, gate_ff, up_ff, name="d_act_swiglu")
    half = N_DEV // 2
    d_w_ffn_in, theirs = mm_tn(h2, dgate, name="d_w_ffn_in_gate", out_dtype=BF16, tm=L, tko=512, tno=1408,
                               out_blocks=half, total_blocks=N_DEV, bg=bg)
    bg = chip_exchange_background(chip_sums(g8_fo, theirs))
    d_w_ffn_in, got = mm_tn(h2, dup, name="d_w_ffn_in_up", out_dtype=BF16, tm=L, tko=512, tno=1408,
                            out_blocks=half, block_offset=half, total_blocks=N_DEV, into=d_w_ffn_in, bg=bg)
    others.update(zip(g8_fo, got))
    g8_fi, bg = blocks({"w_ffn_in": d_w_ffn_in})
    dh2, theirs = mm_nt(dgate, w_ffn_in_blk, a2=dup, name="d_h_ffn", out_dtype=F32, tm=1024, tko=1024, tc=1408, bg=bg)
    late_parts = chip_sums(g8_fi, theirs)
    dx1, dx1b, d_ffn_g = norm_bwd(dh2, x1, w["norm_ffn_g"], dx2, name="d_norm_ffn", want_bf16=True)
    dycat = mm_nt(dx1b, w_out, name="d_ycat", out_dtype=F32, tm=512, tko=1024, tc=2048)
    d_w_out = mm_tn(ycat, dx1b, name="d_w_out", out_dtype=BF16, tm=L, tko=1024, tno=1024)
    g8_out, bg = blocks({"w_out": d_w_out})
    dzu, dzv, d_sgu_w, d_sgu_b, d_ln_g, d_ln_b, d_g_b = sgu_bwd(
        z, dycat, w["sgu_ln_g"], w["sgu_ln_b"], w["sgu_w"][0], b_s3, w["out_norm_sgu_g"], name="d_sgu", d_sgu=d_sgu)
    dt_glu, dd_direct, d_g_a, d_glu_b = glu_post_bwd(
        y0, t_glu, w["ssm_glu_b"], w["out_norm_ssm_g"], dycat, name="d_s5_glu_norm")
    d_w_glu, theirs = mm_tn(ya1, dt_glu, name="d_w_glu", out_dtype=BF16, tm=L, tko=1024, tno=1024, bg=bg)
    late_parts += chip_sums(g8_out, theirs)
    g8_glu, bg = blocks({"ssm_glu_w": d_w_glu})
    dd_mm, theirs = mm_nt(dt_glu, w_glu, name="d_s5_glu_proj", out_dtype=F32, tm=512, tko=1024, tc=2048, bg=bg)
    late_parts += chip_sums(g8_glu, theirs)
    (du, d_bbar_re, d_bbar_im, d_ct_re, d_ct_im, d_a_re, d_a_im, d_d), got = ssm_bwd(
        z, y0, dd_direct, dd_mm, s_re, s_im, *s5_mats, name="d_s5_scan", bg=chip_exchange_background(late_parts))
    others.update(zip(("w_ffn_in", "w_out", "ssm_glu_w"), got))
    dz = jnp.concatenate([du, dzu, dzv], axis=1)

    d_q_re, d_q_im, d_bt_re, d_bt_im = disc_b_bwd(row(q_re), row(q_im), bt_re, bt_im, d_bbar_re, d_bbar_im,
                                                  name="d_s5_discretise_b")
    d_lam_re, d_lam_im, d_log_step = disc_lambda_bwd(
        lam_re, lam_im, log_step,
        (d_a_re.reshape(G, P), d_a_im.reshape(G, P), d_q_re.reshape(G, P), d_q_im.reshape(G, P)),
        name="d_s5_discretise_lambda")
    small_grads = {
        "ssm_lambda_re": d_lam_re, "ssm_lambda_im": d_lam_im, "ssm_log_step": d_log_step,
        "ssm_b_re": d_bt_re, "ssm_b_im": d_bt_im, "ssm_c_re": d_ct_re, "ssm_c_im": d_ct_im,
        "ssm_d": d_d, "ssm_glu_b": d_glu_b, "sgu_ln_g": d_ln_g, "sgu_ln_b": d_ln_b,
        "sgu_w": d_sgu_w, "sgu_b": d_sgu_b, "out_norm_ssm_g": d_g_a, "out_norm_sgu_g": d_g_b,
        "norm_ffn_g": d_ffn_g, "norm_ple_g": d_ple_g, "b_ple_gate": d_b_gate, "final_norm_g": d_final_g,
    }

    d_w_in, got = mm_tn(h1, dz, name="d_w_in", out_dtype=BF16, tm=L, tko=1024, tno=1024, out_blocks=N_DEV,
                        bg=gather_background([loss_part] + [small_grads[n] for n in SMALL]))
    sums = sum_gathered(got, name="sum_small_grads")
    g8_in, _ = blocks({"w_in": d_w_in})
    theirs = sibling_exchange(list(g8_in.values()), name="grads_to_sibling_w_in")
    dh1, got = mm_nt(dz, w_in, name="d_h_mix", out_dtype=F32, tm=512, tko=1024, tc=3 * d_sgu,
                     bg=chip_exchange_background(chip_sums(g8_in, theirs)))
    others.update(zip(g8_in, got))
    grad_x, d_mix_g = norm_bwd(dh1, x2d, w["norm_mix_g"], dx1, name="d_norm_mix", want_bf16=False)
    loss, small_sum = sums[0][0, 0], dict(zip(SMALL, sums[1:]))
    (small_sum["norm_mix_g"],) = sum_gathered(all_gather_blocks([d_mix_g], name="gather_last_grad"),
                                              name="sum_last_grad")

    out = {}
    for n in SHARDED:
        res = adamw_sharded(shard2d[n], m[n].reshape(shard2d[n].shape), v[n].reshape(shard2d[n].shape),
                            own[n][0], own[n][1], others[n], where, name="adamw_" + n)
        out[n] = [r.reshape(w[n].shape) for r in res]

    def work_shape(n):
        s = w[n].shape
        return (1,) + s if len(s) == 1 else (s if len(s) == 2 else s[1:])

    for n in ("ssm_b_re", "ssm_b_im"):
        small_sum[n] = small_sum[n].reshape(H, G, P).transpose(1, 2, 0)
    for n in ("ssm_c_re", "ssm_c_im"):
        small_sum[n] = small_sum[n].reshape(H, G, P).transpose(1, 0, 2)

    def replicated(names_, name):
        gs = [small_sum[n].reshape(work_shape(n)) for n in names_]
        res = adamw_replicated(*[[t[n].reshape(work_shape(n)) for n in names_] for t in (w, m, v)], gs, name=name)
        for i, n in enumerate(names_):
            out[n] = [r.reshape(w[n].shape) for r in (gs[i], res[0][i], res[1][i], res[2][i])]

    replicated([n for n in SMALL + SMALL_LAST if n not in SMALL_WIDE], "adamw_small")
    replicated(list(SMALL_WIDE), "adamw_s5_b_c")

    grads = [out[n][0] for n in WEIGHTS]
    deltas = [out[n][1] for n in WEIGHTS]
    new_m = [out[n][2] for n in WEIGHTS]
    new_v = [out[n][3] for n in WEIGHTS]
    return (loss, grad_x.reshape(x.shape), *grads, *deltas, *new_m, *new_v)


def kernel(x, p, norm_mix_g, w_in, ssm_lambda_re, ssm_lambda_im, ssm_log_step, ssm_b_re, ssm_b_im, ssm_c_re, ssm_c_im, ssm_d, ssm_glu_w, ssm_glu_b, sgu_ln_g, sgu_ln_b, sgu_w, sgu_b, out_norm_ssm_g, out_norm_sgu_g, w_out, norm_ffn_g, w_ffn_in, w_ffn_out, norm_ple_g, w_ple_gate, b_ple_gate, w_ple_proj, final_norm_g, loss_target, m_norm_mix_g, m_w_in, m_ssm_lambda_re, m_ssm_lambda_im, m_ssm_log_step, m_ssm_b_re, m_ssm_b_im, m_ssm_c_re, m_ssm_c_im, m_ssm_d, m_ssm_glu_w, m_ssm_glu_b, m_sgu_ln_g, m_sgu_ln_b, m_sgu_w, m_sgu_b, m_out_norm_ssm_g, m_out_norm_sgu_g, m_w_out, m_norm_ffn_g, m_w_ffn_in, m_w_ffn_out, m_norm_ple_g, m_w_ple_gate, m_b_ple_gate, m_w_ple_proj, m_final_norm_g, v_norm_mix_g, v_w_in, v_ssm_lambda_re, v_ssm_lambda_im, v_ssm_log_step, v_ssm_b_re, v_ssm_b_im, v_ssm_c_re, v_ssm_c_im, v_ssm_d, v_ssm_glu_w, v_ssm_glu_b, v_sgu_ln_g, v_sgu_ln_b, v_sgu_w, v_sgu_b, v_out_norm_ssm_g, v_out_norm_sgu_g, v_w_out, v_norm_ffn_g, v_w_ffn_in, v_w_ffn_out, v_norm_ple_g, v_w_ple_gate, v_b_ple_gate, v_w_ple_proj, v_final_norm_g):
    given = dict(locals())
    w = {n: given[n] for n in WEIGHTS}
    m = {n: given["m_" + n] for n in WEIGHTS}
    v = {n: given["v_" + n] for n in WEIGHTS}
    return _step(x, p, loss_target, w, m, v)
```

```python
import functools
import math

import jax
import jax.numpy as jnp
from jax import lax
from jax.experimental import pallas as pl
from jax.experimental.pallas import tpu as pltpu

F32 = jnp.float32
BF16 = jnp.bfloat16
MESH = pl.DeviceIdType.MESH
ANY = pl.BlockSpec(memory_space=pl.ANY)

N_DEV = 8
EPS = 1e-6
LAMBDA_RE_MAX = -1e-4
SSM_GROUP = 16
SSM_STATE = 64
SSM_SUPER = 16
SGU_CHUNK = 128
ADAM_LR, ADAM_B1, ADAM_B2, ADAM_EPS, ADAM_WD, ADAM_STEP = 0.001, 0.9, 0.999, 1e-08, 0.01, 10
VMEM_LIMIT = 52 * 1024 * 1024
LANE = 128
GATHER_FORWARD_AT = 0.85

_GELU_C = math.sqrt(2.0 / math.pi)


def _params(sem=None):
    return pltpu.CompilerParams(dimension_semantics=sem, vmem_limit_bytes=VMEM_LIMIT)


def _tile(dim, pref, unit=LANE):
    if dim <= pref:
        return dim
    t = (pref // unit) * unit
    while t >= unit:
        if dim % t == 0:
            return t
        t -= unit
    return dim


def _gelu(x):
    return 0.5 * x * (1.0 + jnp.tanh(_GELU_C * (x + 0.044715 * x * x * x)))


def _gelu_grad(x):
    t = jnp.tanh(_GELU_C * (x + 0.044715 * x * x * x))
    return 0.5 * (1.0 + t) + 0.5 * x * (1.0 - t * t) * (_GELU_C * (1.0 + 3.0 * 0.044715 * x * x))


def _gelu_and_grad(x):
    t = jnp.tanh(_GELU_C * (x + 0.044715 * x * x * x))
    return (0.5 * x * (1.0 + t),
            0.5 * (1.0 + t) + 0.5 * x * (1.0 - t * t) * (_GELU_C * (1.0 + 3.0 * 0.044715 * x * x)))


def _rms(x):
    return lax.rsqrt(jnp.mean(x * x, axis=-1, keepdims=True) + EPS)


def _rmsnorm_bwd(dy, x, r, g):
    dyg = dy * g
    return r * dyg - x * (r * r * r) * jnp.mean(dyg * x, axis=-1, keepdims=True)


def _rowsum(v):
    return jnp.sum(v, axis=0, keepdims=True)


class Background:
    def __init__(self, inputs, out_shapes, scratch, phases):
        self.inputs, self.out_shapes, self.scratch, self.phases = list(inputs), list(out_shapes), list(scratch), phases
        self.aliases = {}

    def emit(self, step, nsteps, ins, outs, scratch):
        for place, phase in self.phases:
            at = min(int(place * nsteps), nsteps - 1)

            @pl.when(step == at)
            def _():
                phase(ins, outs, scratch)


def _carrier(bg, n_in, n_out, n_scratch, grid):
    nbi = len(bg.inputs) if bg else 0
    nbo = len(bg.out_shapes) if bg else 0
    nsteps = math.prod(grid)

    def split(refs):
        ins = refs[:n_in]
        bg_ins = refs[n_in:n_in + nbi]
        outs = refs[n_in + nbi:n_in + nbi + n_out]
        bg_outs = refs[n_in + nbi + n_out:n_in + nbi + n_out + nbo]
        rest = refs[n_in + nbi + n_out + nbo:]
        scratch, bg_scratch = rest[:n_scratch], rest[n_scratch:]

        def run_background():
            if bg is None:
                return
            step = pl.program_id(0)
            for axis in range(1, len(grid)):
                step = step * grid[axis] + pl.program_id(axis)
            bg.emit(step, nsteps, bg_ins, bg_outs, bg_scratch)

        return ins, outs, scratch, run_background

    if bg is None:
        return [], [], [], [], [], split
    return [ANY] * nbi, list(bg.inputs), [ANY] * nbo, list(bg.out_shapes), list(bg.scratch), split


def _semantics(bg, sem):
    return tuple("arbitrary" for _ in sem) if bg is not None else sem


def _results(res, n_out, bg):
    res = list(res) if isinstance(res, (list, tuple)) else [res]
    own = res[0] if n_out == 1 else res[:n_out]
    return (own, res[n_out:]) if bg is not None else own


def mm_nn(a, b, *, name, out_dtype, tm, tn, tk, residual=None, bg=None):
    M, K = a.shape
    blocked = b.ndim == 3
    if blocked:
        nb, _, Nb = b.shape
        N = nb * Nb
        tn = _tile(Nb, tn)
        per = Nb // tn
    else:
        N = b.shape[1]
        tn = _tile(N, tn)
    tm, tk = _tile(M, tm, 8), _tile(K, tk)
    nj, ni, nk = N // tn, M // tm, K // tk
    has_res = residual is not None
    grid = (nj, ni, nk)
    bg_in_specs, bg_args, bg_out_specs, bg_out_shapes, bg_scratch, split = _carrier(
        bg, 3 if has_res else 2, 1, 0 if nk == 1 else 1, grid)

    def body(*refs):
        ins, (o_ref,), scratch, run_background = split(refs)
        run_background()
        a_ref, b_ref = ins[0], ins[1]
        r_ref = ins[2] if has_res else None

        def finish(acc):
            if has_res:
                acc = acc + r_ref[...]
            o_ref[...] = acc.astype(o_ref.dtype)

        part = jnp.dot(a_ref[...], b_ref[...], preferred_element_type=F32)
        if nk == 1:
            finish(part)
        else:
            acc_ref = scratch[0]
            k = pl.program_id(2)

            @pl.when(k == 0)
            def _():
                acc_ref[...] = part

            @pl.when(k > 0)
            def _():
                acc_ref[...] += part

            @pl.when(k == nk - 1)
            def _():
                finish(acc_ref[...])

    if blocked:
        b_spec = pl.BlockSpec((None, tk, tn), lambda j, i, k: (j // per, k, j % per))
    else:
        b_spec = pl.BlockSpec((tk, tn), lambda j, i, k: (k, j))
    in_specs = [pl.BlockSpec((tm, tk), lambda j, i, k: (i, k)), b_spec]
    args = [a, b]
    if has_res:
        in_specs.append(pl.BlockSpec((tm, tn), lambda j, i, k: (i, j)))
        args.append(residual)
    res = pl.pallas_call(
        body, name=name, grid=grid,
        in_specs=in_specs + bg_in_specs,
        out_specs=[pl.BlockSpec((tm, tn), lambda j, i, k: (i, j))] + bg_out_specs,
        out_shape=[jax.ShapeDtypeStruct((M, N), out_dtype)] + bg_out_shapes,
        input_output_aliases={len(args) + k: 1 + o for k, o in (bg.aliases if bg else {}).items()},
        scratch_shapes=([] if nk == 1 else [pltpu.VMEM((tm, tn), F32)]) + bg_scratch,
        compiler_params=_params(_semantics(bg, ("parallel", "parallel", "arbitrary"))),
    )(*args, *bg_args)
    return _results(res, 1, bg)


def mm_nt(a, w, *, name, out_dtype, tm, tko, tc, a2=None, bg=None):
    M, N = a.shape
    if a2 is not None:
        N = 2 * N
    blocked = w.ndim == 3
    if blocked:
        nb, Ko, Nb = w.shape
        tc = _tile(Nb, tc)
        per = Nb // tc
    else:
        Ko = w.shape[0]
        tc = _tile(N, tc)
    tm, tko = _tile(M, tm, 8), _tile(Ko, tko)
    njo, ni, nc = Ko // tko, M // tm, N // tc
    grid = (njo, ni, nc)
    half = nc // 2
    bg_in_specs, bg_args, bg_out_specs, bg_out_shapes, bg_scratch, split = _carrier(
        bg, 2 if a2 is None else 3, 1, 0 if nc == 1 else 1, grid)

    def body(*refs):
        ins, (o_ref,), scratch, run_background = split(refs)
        run_background()
        a_val = ins[0][...]
        if a2 is not None:
            a_val = jnp.where(pl.program_id(2) < half, a_val, ins[1][...])
        part = lax.dot_general(a_val, ins[-1][...], (((1,), (1,)), ((), ())),
                               preferred_element_type=F32)
        if nc == 1:
            o_ref[...] = part.astype(o_ref.dtype)
        else:
            acc_ref = scratch[0]
            c = pl.program_id(2)

            @pl.when(c == 0)
            def _():
                acc_ref[...] = part

            @pl.when(c > 0)
            def _():
                acc_ref[...] += part

            @pl.when(c == nc - 1)
            def _():
                o_ref[...] = acc_ref[...].astype(o_ref.dtype)

    if blocked:
        w_spec = pl.BlockSpec((None, tko, tc), lambda j, i, c: (c // per, j, c % per))
    else:
        w_spec = pl.BlockSpec((tko, tc), lambda j, i, c: (j, c))
    if a2 is None:
        a_specs, a_args = [pl.BlockSpec((tm, tc), lambda j, i, c: (i, c))], [a]
    else:
        a_specs = [pl.BlockSpec((tm, tc), lambda j, i, c: (i, jnp.minimum(c, half - 1))),
                   pl.BlockSpec((tm, tc), lambda j, i, c: (i, jnp.maximum(c - half, 0)))]
        a_args = [a, a2]
    res = pl.pallas_call(
        body, name=name, grid=grid,
        in_specs=a_specs + [w_spec] + bg_in_specs,
        out_specs=[pl.BlockSpec((tm, tko), lambda j, i, c: (i, j))] + bg_out_specs,
        out_shape=[jax.ShapeDtypeStruct((M, Ko), out_dtype)] + bg_out_shapes,
        scratch_shapes=([] if nc == 1 else [pltpu.VMEM((tm, tko), F32)]) + bg_scratch,
        compiler_params=_params(_semantics(bg, ("parallel", "parallel", "arbitrary"))),
    )(*a_args, w, *bg_args)
    return _results(res, 1, bg)


def mm_tn(a, g, *, name, out_dtype, tm, tko, tno, out_blocks=None, block_offset=0, total_blocks=None, into=None,
          bg=None):
    M, K = a.shape
    N = g.shape[1]
    if out_blocks:
        Nb = N // out_blocks
        tno = _tile(Nb, tno)
        per = Nb // tno
    else:
        tno = _tile(N, tno)
    tm, tko = _tile(M, tm), _tile(K, tko)
    njo, njn, nm = K // tko, N // tno, M // tm
    grid = (njo, njn, nm)
    bg_in_specs, bg_args, bg_out_specs, bg_out_shapes, bg_scratch, split = _carrier(
        bg, 2 if into is None else 3, 1, 0 if nm == 1 else 1, grid)

    def body(*refs):
        ins, (o_ref,), scratch, run_background = split(refs)
        a_ref, g_ref = ins[0], ins[1]
        run_background()
        part = lax.dot_general(a_ref[...], g_ref[...], (((0,), (0,)), ((), ())),
                               preferred_element_type=F32)
        if nm == 1:
            o_ref[...] = part.astype(o_ref.dtype)
        else:
            acc_ref = scratch[0]
            m = pl.program_id(2)

            @pl.when(m == 0)
            def _():
                acc_ref[...] = part

            @pl.when(m > 0)
            def _():
                acc_ref[...] += part

            @pl.when(m == nm - 1)
            def _():
                o_ref[...] = acc_ref[...].astype(o_ref.dtype)

    if out_blocks:
        o_spec = pl.BlockSpec((None, tko, tno), lambda jo, jn, m: (jn // per + block_offset, jo, jn % per))
        o_shape = jax.ShapeDtypeStruct((total_blocks or out_blocks, K, Nb), out_dtype)
    else:
        o_spec = pl.BlockSpec((tko, tno), lambda jo, jn, m: (jo, jn))
        o_shape = jax.ShapeDtypeStruct((K, N), out_dtype)
    res = pl.pallas_call(
        body, name=name, grid=grid,
        in_specs=[pl.BlockSpec((tm, tko), lambda jo, jn, m: (m, jo)),
                  pl.BlockSpec((tm, tno), lambda jo, jn, m: (m, jn))] + ([] if into is None else [ANY]) + bg_in_specs,
        out_specs=[o_spec] + bg_out_specs, out_shape=[o_shape] + bg_out_shapes,
        scratch_shapes=([] if nm == 1 else [pltpu.VMEM((tko, tno), F32)]) + bg_scratch,
        input_output_aliases={} if into is None else {2: 0},
        compiler_params=_params(_semantics(bg, ("parallel", "parallel", "arbitrary"))),
    )(a, g, *([] if into is None else [into]), *bg_args)
    return _results(res, 1, bg)


def ffn_in_swiglu(h, w_blk, *, name, tm, bg=None):
    M, K = h.shape
    nb, _, Nb = w_blk.shape
    nh = nb // 2
    F = nh * Nb
    tm = _tile(M, tm, 8)
    grid = (nh, M // tm)
    bg_in_specs, bg_args, bg_out_specs, bg_out_shapes, bg_scratch, split = _carrier(bg, 3, 3, 0, grid)

    def body(*refs):
        (h_ref, wg_ref, wu_ref), (act_ref, gate_ref, up_ref), _, run_background = split(refs)
        run_background()
        hv = h_ref[...]
        gate = jnp.dot(hv, wg_ref[...], preferred_element_type=F32)
        up = jnp.dot(hv, wu_ref[...], preferred_element_type=F32)
        gate_ref[...] = gate
        up_ref[...] = up
        act_ref[...] = (gate * jax.nn.sigmoid(gate) * up).astype(act_ref.dtype)

    o_spec = pl.BlockSpec((tm, Nb), lambda j, i: (i, j))
    res = pl.pallas_call(
        body, name=name, grid=grid,
        in_specs=[pl.BlockSpec((tm, K), lambda j, i: (i, 0)),
                  pl.BlockSpec((None, K, Nb), lambda j, i: (j, 0, 0)),
                  pl.BlockSpec((None, K, Nb), lambda j, i: (j + nh, 0, 0))] + bg_in_specs,
        out_specs=[o_spec, o_spec, o_spec] + bg_out_specs,
        out_shape=[jax.ShapeDtypeStruct((M, F), BF16), jax.ShapeDtypeStruct((M, F), F32),
                   jax.ShapeDtypeStruct((M, F), F32)] + bg_out_shapes,
        scratch_shapes=bg_scratch,
        compiler_params=_params(_semantics(bg, ("parallel", "parallel"))),
    )(h, w_blk, w_blk, *bg_args)
    return _results(res, 3, bg)


def _row_spec(tm, d, col=0):
    return pl.BlockSpec((tm, d), lambda i: (i, col))


def _vec_spec(d):
    return pl.BlockSpec((1, d), lambda i: (0, 0))


def norm_fwd(x, g, *, name, tm=512):
    L, D = x.shape
    tm = _tile(L, tm, 8)

    def body(x_ref, g_ref, h_ref):
        xv = x_ref[...]
        h_ref[...] = (xv * _rms(xv) * g_ref[...]).astype(h_ref.dtype)

    return pl.pallas_call(
        body, name=name, grid=(L // tm,),
        in_specs=[_row_spec(tm, D), _vec_spec(D)],
        out_specs=_row_spec(tm, D),
        out_shape=jax.ShapeDtypeStruct((L, D), BF16),
        compiler_params=_params(("parallel",)),
    )(x, g)


def norm_bwd(dh, xin, g, dres, *, name, want_bf16, tm=512):
    L, D = xin.shape
    tm = _tile(L, tm, 8)

    def body(dh_ref, x_ref, g_ref, dres_ref, dx_ref, *rest):
        dg_ref = rest[-1]
        xv, dhv = x_ref[...], dh_ref[...]
        r = _rms(xv)
        dx = dres_ref[...] + _rmsnorm_bwd(dhv, xv, r, g_ref[...])
        dx_ref[...] = dx
        if want_bf16:
            rest[0][...] = dx.astype(BF16)
        part = _rowsum(dhv * xv * r)

        @pl.when(pl.program_id(0) == 0)
        def _():
            dg_ref[...] = part

        @pl.when(pl.program_id(0) > 0)
        def _():
            dg_ref[...] += part

    out_specs = [_row_spec(tm, D)] + ([_row_spec(tm, D)] if want_bf16 else []) + [_vec_spec(D)]
    out_shape = ([jax.ShapeDtypeStruct((L, D), F32)]
                 + ([jax.ShapeDtypeStruct((L, D), BF16)] if want_bf16 else [])
                 + [jax.ShapeDtypeStruct((1, D), F32)])
    return pl.pallas_call(
        body, name=name, grid=(L // tm,),
        in_specs=[_row_spec(tm, D), _row_spec(tm, D), _vec_spec(D), _row_spec(tm, D)],
        out_specs=out_specs, out_shape=out_shape,
        compiler_params=_params(("arbitrary",)),
    )(dh, xin, g, dres)


def glu_pre(y0, *, name, tm=512):
    L, D = y0.shape
    tm = _tile(L, tm, 8)

    def body(y_ref, o_ref):
        o_ref[...] = _gelu(y_ref[...]).astype(o_ref.dtype)

    return pl.pallas_call(
        body, name=name, grid=(L // tm,),
        in_specs=[_row_spec(tm, D)], out_specs=_row_spec(tm, D),
        out_shape=jax.ShapeDtypeStruct((L, D), BF16),
        compiler_params=_params(("parallel",)),
    )(y0)


def glu_post(y0, t, b_glu, g_a, *, name, tm=512):
    L, D = y0.shape
    tm = _tile(L, tm, 8)

    def body(y_ref, t_ref, b_ref, g_ref, o_ref):
        ya = _gelu(y_ref[...]) * jax.nn.sigmoid(t_ref[...] + b_ref[...])
        o_ref[...] = (ya * _rms(ya) * g_ref[...]).astype(o_ref.dtype)

    return pl.pallas_call(
        body, name=name, grid=(L // tm,),
        in_specs=[_row_spec(tm, D), _row_spec(tm, D), _vec_spec(D), _vec_spec(D)],
        out_specs=_row_spec(tm, D),
        out_shape=jax.ShapeDtypeStruct((L, D), BF16),
        compiler_params=_params(("parallel",)),
    )(y0, t, b_glu, g_a)


def glu_post_bwd(y0, t, b_glu, g_a, dycat, *, name, tm=512):
    L, D = y0.shape
    tm = _tile(L, tm, 8)

    def body(y_ref, t_ref, b_ref, g_ref, dn_ref, dt_ref, dd_ref, dga_ref, dbg_ref):
        ya1 = _gelu(y_ref[...])
        sg = jax.nn.sigmoid(t_ref[...] + b_ref[...])
        ya = ya1 * sg
        ra = _rms(ya)
        dn = dn_ref[...]
        dya = _rmsnorm_bwd(dn, ya, ra, g_ref[...])
        dt = dya * ya1 * sg * (1.0 - sg)
        dt_ref[...] = dt.astype(BF16)
        dd_ref[...] = dya * sg
        p_ga, p_bg = _rowsum(dn * ya * ra), _rowsum(dt)

        @pl.when(pl.program_id(0) == 0)
        def _():
            dga_ref[...] = p_ga
            dbg_ref[...] = p_bg

        @pl.when(pl.program_id(0) > 0)
        def _():
            dga_ref[...] += p_ga
            dbg_ref[...] += p_bg

    return pl.pallas_call(
        body, name=name, grid=(L // tm,),
        in_specs=[_row_spec(tm, D), _row_spec(tm, D), _vec_spec(D), _vec_spec(D), _row_spec(tm, D, 0)],
        out_specs=[_row_spec(tm, D), _row_spec(tm, D), _vec_spec(D), _vec_spec(D)],
        out_shape=[jax.ShapeDtypeStruct((L, D), BF16), jax.ShapeDtypeStruct((L, D), F32),
                   jax.ShapeDtypeStruct((1, D), F32), jax.ShapeDtypeStruct((1, D), F32)],
        compiler_params=_params(("arbitrary",)),
    )(y0, t, b_glu, g_a, dycat)


def head_and_loss(x2, gpre, b_g, pp, g_f, tgt, *, name, tm=256):
    L, D = x2.shape
    tm = _tile(L, tm, 8)

    def body(x2_ref, gp_ref, bg_ref, pp_ref, gf_ref, tg_ref,
             dx3_ref, dpre_ref, dpp_ref, loss_ref, dgf_ref, dbg_ref):
        gate = jax.nn.sigmoid(gp_ref[...] + bg_ref[...])
        ppv = pp_ref[...]
        x3 = x2_ref[...] + gate * ppv
        r = _rms(x3)
        xn = x3 * r
        gf = gf_ref[...]
        err = xn * gf - tg_ref[...]
        loss = 0.5 * jnp.sum(jnp.mean(err * err, axis=-1, keepdims=True), axis=0, keepdims=True)
        dout = err * (1.0 / D)
        dx3 = _rmsnorm_bwd(dout, x3, r, gf)
        dx3_ref[...] = dx3
        dpre = dx3 * ppv * gate * (1.0 - gate)
        dpre_ref[...] = dpre.astype(BF16)
        dpp_ref[...] = (dx3 * gate).astype(BF16)
        p_gf, p_bg = _rowsum(dout * xn), _rowsum(dpre)
        p_loss = jnp.broadcast_to(loss, loss_ref.shape)

        @pl.when(pl.program_id(0) == 0)
        def _():
            loss_ref[...] = p_loss
            dgf_ref[...] = p_gf
            dbg_ref[...] = p_bg

        @pl.when(pl.program_id(0) > 0)
        def _():
            loss_ref[...] += p_loss
            dgf_ref[...] += p_gf
            dbg_ref[...] += p_bg

    rs = _row_spec(tm, D)
    return pl.pallas_call(
        body, name=name, grid=(L // tm,),
        in_specs=[rs, rs, _vec_spec(D), rs, _vec_spec(D), rs],
        out_specs=[rs, rs, rs, pl.BlockSpec((8, LANE), lambda i: (0, 0)), _vec_spec(D), _vec_spec(D)],
        out_shape=[jax.ShapeDtypeStruct((L, D), F32), jax.ShapeDtypeStruct((L, D), BF16),
                   jax.ShapeDtypeStruct((L, D), BF16), jax.ShapeDtypeStruct((8, LANE), F32),
                   jax.ShapeDtypeStruct((1, D), F32), jax.ShapeDtypeStruct((1, D), F32)],
        compiler_params=_params(("arbitrary",)),
    )(x2, gpre, b_g, pp, g_f, tgt)


def ffn_out_bwd_swiglu(dx, w, gate, up, *, name, tm=512, tf=1408):
    M, D = dx.shape
    F = w.shape[0]
    tm, tf = _tile(M, tm, 8), _tile(F, tf)

    def body(dx_ref, w_ref, g_ref, u_ref, dg_ref, du_ref):
        da = lax.dot_general(dx_ref[...], w_ref[...], (((1,), (1,)), ((), ())), preferred_element_type=F32)
        gv = g_ref[...]
        sg = jax.nn.sigmoid(gv)
        dg_ref[...] = (da * u_ref[...] * sg * (1.0 + gv * (1.0 - sg))).astype(BF16)
        du_ref[...] = (da * gv * sg).astype(BF16)

    spec = pl.BlockSpec((tm, tf), lambda j, i: (i, j))
    return pl.pallas_call(
        body, name=name, grid=(F // tf, M // tm),
        in_specs=[pl.BlockSpec((tm, D), lambda j, i: (i, 0)), pl.BlockSpec((tf, D), lambda j, i: (j, 0)), spec, spec],
        out_specs=[spec, spec],
        out_shape=[jax.ShapeDtypeStruct((M, F), BF16), jax.ShapeDtypeStruct((M, F), BF16)],
        compiler_params=_params(("parallel", "parallel")),
    )(dx, w, gate, up)


def _sgu_forward_values(u1, v1, lng, lnb, w_ref, bs_ref, s_scr, heads, hd):
    xc = v1 - jnp.mean(v1, axis=-1, keepdims=True)
    r = lax.rsqrt(jnp.mean(xc * xc, axis=-1, keepdims=True) + EPS)
    xhat = xc * r
    v2 = xhat * lng + lnb
    tril = (lax.broadcasted_iota(jnp.int32, (SGU_CHUNK, SGU_CHUNK), 0)
            >= lax.broadcasted_iota(jnp.int32, (SGU_CHUNK, SGU_CHUNK), 1))
    for h in range(heads):
        wm = jnp.where(tril, w_ref[h], 0.0).astype(BF16)
        cols = slice(h * hd, (h + 1) * hd)
        s_scr[:, cols] = jnp.dot(wm, v2[:, cols].astype(BF16), preferred_element_type=F32) + bs_ref[h]
    return xhat, r, v2, tril


def sgu_fwd(z, ln_g, ln_b, w_s, b_s, g_b, *, name, d_sgu):
    L = z.shape[0]
    heads = w_s.shape[0]
    hd = d_sgu // heads

    def body(zu_ref, zv_ref, lng_ref, lnb_ref, w_ref, bs_ref, gb_ref, o_ref, s_scr):
        u1 = _gelu(zu_ref[...])
        _sgu_forward_values(u1, _gelu(zv_ref[...]), lng_ref[...], lnb_ref[...], w_ref, bs_ref, s_scr, heads, hd)
        yb = u1 * s_scr[...]
        o_ref[...] = (yb * _rms(yb) * gb_ref[...]).astype(o_ref.dtype)

    blk = lambda col: pl.BlockSpec((SGU_CHUNK, d_sgu), lambda n: (n, col))
    return pl.pallas_call(
        body, name=name, grid=(L // SGU_CHUNK,),
        in_specs=[blk(1), blk(2), _vec_spec(d_sgu), _vec_spec(d_sgu),
                  pl.BlockSpec(w_s.shape, lambda n: (0, 0, 0)), pl.BlockSpec(b_s.shape, lambda n: (0, 0, 0)),
                  _vec_spec(d_sgu)],
        out_specs=blk(0),
        out_shape=jax.ShapeDtypeStruct((L, d_sgu), BF16),
        scratch_shapes=[pltpu.VMEM((SGU_CHUNK, d_sgu), F32)],
        compiler_params=_params(("parallel",)),
    )(z, z, ln_g, ln_b, w_s, b_s, g_b)


def sgu_bwd(z, dycat, ln_g, ln_b, w_s, b_s, g_b, *, name, d_sgu):
    L = z.shape[0]
    heads = w_s.shape[0]
    hd = d_sgu // heads

    def body(zu_ref, zv_ref, dn_ref, lng_ref, lnb_ref, w_ref, bs_ref, gb_ref,
             dzu_ref, dzv_ref, dw_ref, dbs_ref, dlng_ref, dlnb_ref, dgb_ref, s_scr, dv_scr):
        first = pl.program_id(0) == 0
        lng = lng_ref[...]
        u1, du1 = _gelu_and_grad(zu_ref[...])
        v1, dv1_dz = _gelu_and_grad(zv_ref[...])
        xhat, r, v2, tril = _sgu_forward_values(u1, v1, lng, lnb_ref[...], w_ref, bs_ref, s_scr, heads, hd)
        s = s_scr[...]
        yb = u1 * s
        rb = _rms(yb)
        dn = dn_ref[...]
        dyb = _rmsnorm_bwd(dn, yb, rb, gb_ref[...])
        dzu_ref[...] = (dyb * s * du1).astype(BF16)
        ds = dyb * u1
        for h in range(heads):
            cols = slice(h * hd, (h + 1) * hd)
            ds_h = ds[:, cols]
            ds_hb = ds_h.astype(BF16)
            wm = jnp.where(tril, w_ref[h], 0.0).astype(BF16)
            dw_h = jnp.where(tril, lax.dot_general(ds_hb, v2[:, cols].astype(BF16), (((1,), (1,)), ((), ())),
                                                   preferred_element_type=F32), 0.0)
            db_h = jnp.sum(ds_h.T, axis=0, keepdims=True)
            dv_scr[:, cols] = lax.dot_general(wm, ds_hb, (((0,), (0,)), ((), ())), preferred_element_type=F32)

            @pl.when(first)
            def _():
                dw_ref[h] = dw_h
                dbs_ref[h] = db_h

            @pl.when(jnp.logical_not(first))
            def _():
                dw_ref[h] += dw_h
                dbs_ref[h] += db_h

        dv2 = dv_scr[...]
        dxh = dv2 * lng
        dv1 = r * (dxh - jnp.mean(dxh, axis=-1, keepdims=True)
                   - xhat * jnp.mean(dxh * xhat, axis=-1, keepdims=True))
        dzv_ref[...] = (dv1 * dv1_dz).astype(BF16)
        p_lng, p_lnb, p_gb = _rowsum(dv2 * xhat), _rowsum(dv2), _rowsum(dn * yb * rb)

        @pl.when(first)
        def _():
            dlng_ref[...] = p_lng
            dlnb_ref[...] = p_lnb
            dgb_ref[...] = p_gb

        @pl.when(jnp.logical_not(first))
        def _():
            dlng_ref[...] += p_lng
            dlnb_ref[...] += p_lnb
            dgb_ref[...] += p_gb

    blk = lambda col: pl.BlockSpec((SGU_CHUNK, d_sgu), lambda n: (n, col))
    full3 = lambda shape: pl.BlockSpec(shape, lambda n: (0, 0, 0))
    return pl.pallas_call(
        body, name=name, grid=(L // SGU_CHUNK,),
        in_specs=[blk(1), blk(2), blk(1), _vec_spec(d_sgu), _vec_spec(d_sgu),
                  full3(w_s.shape), full3(b_s.shape), _vec_spec(d_sgu)],
        out_specs=[blk(0), blk(0), full3(w_s.shape), full3((heads, 1, SGU_CHUNK)),
                   _vec_spec(d_sgu), _vec_spec(d_sgu), _vec_spec(d_sgu)],
        out_shape=[jax.ShapeDtypeStruct((L, d_sgu), BF16), jax.ShapeDtypeStruct((L, d_sgu), BF16),
                   jax.ShapeDtypeStruct(w_s.shape, F32), jax.ShapeDtypeStruct((heads, 1, SGU_CHUNK), F32),
                   jax.ShapeDtypeStruct((1, d_sgu), F32), jax.ShapeDtypeStruct((1, d_sgu), F32),
                   jax.ShapeDtypeStruct((1, d_sgu), F32)],
        scratch_shapes=[pltpu.VMEM((SGU_CHUNK, d_sgu), F32), pltpu.VMEM((SGU_CHUNK, d_sgu), F32)],
        compiler_params=_params(("arbitrary",)),
    )(z, z, dycat, ln_g, ln_b, w_s, b_s, g_b)


def _disc_lambda(lam_re, lam_im, log_step):
    lr = jnp.minimum(lam_re, LAMBDA_RE_MAX)
    li = lam_im
    dt = jnp.exp(log_step)
    mag = jnp.exp(lr * dt)
    ang = li * dt
    a_re = mag * jnp.cos(ang)
    a_im = mag * jnp.sin(ang)
    nr = a_re - 1.0
    ni = a_im
    den = lr * lr + li * li
    return a_re, a_im, (nr * lr + ni * li) / den, (ni * lr - nr * li) / den


def _disc_b(q_re, q_im, b_re, b_im):
    return q_re * b_re - q_im * b_im, q_re * b_im + q_im * b_re


def disc_lambda_fwd(lam_re, lam_im, log_step, *, name):
    def body(lr_ref, li_ref, ls_ref, ar_ref, ai_ref, qr_ref, qi_ref):
        ar_ref[...], ai_ref[...], qr_ref[...], qi_ref[...] = _disc_lambda(lr_ref[...], li_ref[...], ls_ref[...])

    sd = jax.ShapeDtypeStruct(lam_re.shape, F32)
    return pl.pallas_call(body, name=name, out_shape=[sd, sd, sd, sd], compiler_params=_params())(
        lam_re, lam_im, log_step)


def disc_lambda_bwd(lam_re, lam_im, log_step, cts, *, name):
    def body(lr_ref, li_ref, ls_ref, c0, c1, c2, c3, dlr_ref, dli_ref, dls_ref):
        _, vjp = jax.vjp(_disc_lambda, lr_ref[...], li_ref[...], ls_ref[...])
        dlr_ref[...], dli_ref[...], dls_ref[...] = vjp((c0[...], c1[...], c2[...], c3[...]))

    sd = jax.ShapeDtypeStruct(lam_re.shape, F32)
    return pl.pallas_call(body, name=name, out_shape=[sd, sd, jax.ShapeDtypeStruct(log_step.shape, F32)],
                          compiler_params=_params())(lam_re, lam_im, log_step, *cts)


def disc_b_fwd(q_re, q_im, b_re, b_im, *, name):
    def body(qr_ref, qi_ref, br_ref, bi_ref, or_ref, oi_ref):
        or_ref[...], oi_ref[...] = _disc_b(qr_ref[...], qi_ref[...], br_ref[...], bi_ref[...])

    sd = jax.ShapeDtypeStruct(b_re.shape, F32)
    return pl.pallas_call(body, name=name, out_shape=[sd, sd], compiler_params=_params())(q_re, q_im, b_re, b_im)


def disc_b_bwd(q_re, q_im, b_re, b_im, ct_re, ct_im, *, name):
    def body(qr_ref, qi_ref, br_ref, bi_ref, cr_ref, ci_ref, dqr_ref, dqi_ref, dbr_ref, dbi_ref):
        _, vjp = jax.vjp(_disc_b, qr_ref[...], qi_ref[...], br_ref[...], bi_ref[...])
        dqr_ref[...], dqi_ref[...], dbr_ref[...], dbi_ref[...] = vjp((cr_ref[...], ci_ref[...]))

    sq, sb = jax.ShapeDtypeStruct(q_re.shape, F32), jax.ShapeDtypeStruct(b_re.shape, F32)
    return pl.pallas_call(body, name=name, out_shape=[sq, sq, sb, sb], compiler_params=_params())(
        q_re, q_im, b_re, b_im, ct_re, ct_im)


def _lti_scan(xr, xi, ar, ai, reverse):
    T = xr.shape[0]
    row = lax.broadcasted_iota(jnp.int32, xr.shape, 0)
    k = 1
    while k < T:
        shift = T - k if reverse else k
        keep = (row < T - k) if reverse else (row >= k)
        sr = jnp.where(keep, pltpu.roll(xr, shift, 0), 0.0)
        si = jnp.where(keep, pltpu.roll(xi, shift, 0), 0.0)
        xr, xi = xr + ar * sr - ai * si, xi + ar * si + ai * sr
        ar, ai = ar * ar - ai * ai, 2.0 * ar * ai
        k *= 2
    return xr, xi


SUBLANES = 8


def _scan_rows(x_re, x_im, o_re, o_im, ar, ai, cr, ci, reverse):
    T, n = x_re.shape
    groups = T // SUBLANES
    row = lax.broadcasted_iota(jnp.int32, (SUBLANES, n), 0)
    edge = SUBLANES - 1 if reverse else 0
    pr, pi = _lti_scan(jnp.where(row == edge, ar, 0.0), jnp.where(row == edge, ai, 0.0), ar, ai, reverse)
    pows = []
    for level in range(3):
        k = 1 << level
        keep = (row < SUBLANES - k) if reverse else (row >= k)
        pows.append((jnp.where(keep, ar, 0.0), jnp.where(keep, ai, 0.0)))
        ar, ai = ar * ar - ai * ai, 2.0 * ar * ai

    def group(i, carry):
        cr, ci = carry
        at = pl.multiple_of((groups - 1 - i if reverse else i) * SUBLANES, SUBLANES)
        xr, xi = x_re[pl.ds(at, SUBLANES), :], x_im[pl.ds(at, SUBLANES), :]
        for level, (qr, qi) in enumerate(pows):
            shift = SUBLANES - (1 << level) if reverse else 1 << level
            sr, si = pltpu.roll(xr, shift, 0), pltpu.roll(xi, shift, 0)
            xr, xi = xr + qr * sr - qi * si, xi + qr * si + qi * sr
        xr, xi = xr + pr * cr - pi * ci, xi + pr * ci + pi * cr
        o_re[pl.ds(at, SUBLANES), :] = xr
        o_im[pl.ds(at, SUBLANES), :] = xi
        return spread(xr[last:last + 1, :]), spread(xi[last:last + 1, :])

    last = 0 if reverse else SUBLANES - 1
    spread = lambda v: jnp.broadcast_to(v, (SUBLANES, n))
    cr, ci = lax.fori_loop(0, groups, group, (spread(cr), spread(ci)), unroll=2)
    return cr[0:1, :], ci[0:1, :]


def _ssm_chunk(L):
    return _tile(L, 512, 8)


def _same_group(rows, cols):
    r = lax.broadcasted_iota(jnp.int32, (rows, cols), 0) // SSM_GROUP
    c = lax.broadcasted_iota(jnp.int32, (rows, cols), 1) // SSM_STATE
    return r == c


def _expand_groups(compact):
    H, S = compact.shape
    tiled = jnp.concatenate([compact] * (S // SSM_STATE), axis=0)
    return jnp.where(_same_group(tiled.shape[0], S), tiled, 0.0).astype(BF16)


def _collapse_groups(dense):
    C, S = dense.shape
    masked = jnp.where(_same_group(C, S), dense, 0.0)
    total = masked[0:SSM_GROUP]
    for g in range(1, C // SSM_GROUP):
        total = total + masked[g * SSM_GROUP:(g + 1) * SSM_GROUP]
    return total


def ssm_fwd(z, bt_re, bt_im, ct_re, ct_im, a_re, a_im, d, *, name, bg=None):
    L = z.shape[0]
    NK, _, S = a_re.shape
    H = bt_re.shape[0]
    C = S // SSM_STATE * SSM_GROUP
    T = _ssm_chunk(L)
    grid = (NK, L // T)
    bg_in_specs, bg_args, bg_out_specs, bg_out_shapes, bg_scratch, split = _carrier(bg, 8, 3, 8, grid)
    nt_dot = lambda p, q: lax.dot_general(p, q, (((1,), (1,)), ((), ())), preferred_element_type=F32)

    def body(*refs):
        ((u_ref, btr_ref, bti_ref, ctr_ref, cti_ref, ar_ref, ai_ref, d_ref), (y_ref, sr_ref, si_ref),
         (car_re, car_im, bu_re, bu_im, b_re, b_im, c_re, c_im), run_background) = split(refs)
        run_background()
        i = pl.program_id(1)
        ar, ai = ar_ref[...], ai_ref[...]

        @pl.when(i == 0)
        def _():
            car_re[...] = jnp.zeros_like(car_re)
            car_im[...] = jnp.zeros_like(car_im)
            b_re[...] = _expand_groups(btr_ref[...])
            b_im[...] = _expand_groups(bti_ref[...])
            c_re[...] = _expand_groups(ctr_ref[...])
            c_im[...] = _expand_groups(cti_ref[...])

        u = u_ref[...]
        ub = u.astype(BF16)
        bu_re[...] = jnp.dot(ub, b_re[...], preferred_element_type=F32)
        bu_im[...] = jnp.dot(ub, b_im[...], preferred_element_type=F32)
        car_re[...], car_im[...] = _scan_rows(bu_re, bu_im, sr_ref, si_ref, ar, ai, car_re[...], car_im[...], False)
        y_ref[...] = (nt_dot(sr_ref[...].astype(BF16), c_re[...]) - nt_dot(si_ref[...].astype(BF16), c_im[...])
                      + d_ref[...] * u)

    kspec = lambda shape: pl.BlockSpec((None,) + shape, lambda k, i: (k, 0, 0))
    compact = pl.BlockSpec((H, S), lambda k, i: (0, k))
    res = pl.pallas_call(
        body, name=name, grid=grid,
        in_specs=[pl.BlockSpec((T, C), lambda k, i: (i, k)), compact, compact, compact, compact,
                  kspec((1, S)), kspec((1, S)), kspec((1, C))] + bg_in_specs,
        out_specs=[pl.BlockSpec((T, C), lambda k, i: (i, k)),
                   pl.BlockSpec((T, S), lambda k, i: (i, k)), pl.BlockSpec((T, S), lambda k, i: (i, k))] + bg_out_specs,
        out_shape=[jax.ShapeDtypeStruct((L, NK * C), F32), jax.ShapeDtypeStruct((L, NK * S), F32),
                   jax.ShapeDtypeStruct((L, NK * S), F32)] + bg_out_shapes,
        scratch_shapes=[pltpu.VMEM((1, S), F32), pltpu.VMEM((1, S), F32),
                        pltpu.VMEM((T, S), F32), pltpu.VMEM((T, S), F32)]
        + [pltpu.VMEM((C, S), BF16)] * 4 + bg_scratch,
        compiler_params=_params(_semantics(bg, ("parallel", "arbitrary"))),
    )(z, bt_re, bt_im, ct_re, ct_im, a_re, a_im, d, *bg_args)
    return _results(res, 3, bg)


def ssm_bwd(z, y0, dd_direct, dd_mm, s_re, s_im, bt_re, bt_im, ct_re, ct_im, a_re, a_im, d, *, name, bg=None):
    L = z.shape[0]
    NK, _, S = a_re.shape
    H = bt_re.shape[0]
    C = S // SSM_STATE * SSM_GROUP
    T = _ssm_chunk(L)
    nchunk = L // T
    tail = T // 8
    grid = (NK, nchunk)
    bg_in_specs, bg_args, bg_out_specs, bg_out_shapes, bg_scratch, split = _carrier(bg, 15, 8, 12, grid)
    nt_dot = lambda p, q: lax.dot_general(p, q, (((1,), (1,)), ((), ())), preferred_element_type=F32)

    def body(*refs):
        ((u_ref, y_ref, d1_ref, d2_ref, sr_ref, si_ref, pr_ref, pi_ref,
          btr_ref, bti_ref, ctr_ref, cti_ref, ar_ref, ai_ref, d_ref),
         (du_ref, dbr_ref, dbi_ref, dcr_ref, dci_ref, dar_ref, dai_ref, dd_ref),
         (car_re, car_im, lam_re, lam_im, b_re, b_im, c_re, c_im, acc_br, acc_bi, acc_cr, acc_ci),
         run_background) = split(refs)
        run_background()
        i = pl.program_id(1)
        chunk = nchunk - 1 - i
        ar, ai = ar_ref[...], ai_ref[...]
        row = lax.broadcasted_iota(jnp.int32, (T, S), 0)

        @pl.when(i == 0)
        def _():
            car_re[...] = jnp.zeros_like(car_re)
            car_im[...] = jnp.zeros_like(car_im)
            b_re[...] = _expand_groups(btr_ref[...])
            b_im[...] = _expand_groups(bti_ref[...])
            c_re[...] = _expand_groups(ctr_ref[...])
            c_im[...] = _expand_groups(cti_ref[...])

        u = u_ref[...]
        dy = (d1_ref[...] + d2_ref[...]) * _gelu_grad(y_ref[...])
        dyb = dy.astype(BF16)
        lam_re[...] = jnp.dot(dyb, c_re[...], preferred_element_type=F32)
        lam_im[...] = -jnp.dot(dyb, c_im[...], preferred_element_type=F32)
        car_re[...], car_im[...] = _scan_rows(lam_re, lam_im, lam_re, lam_im, ar, -ai, car_re[...], car_im[...], True)
        lr, li = lam_re[...], lam_im[...]

        s_re, s_im = sr_ref[...], si_ref[...]
        has_prev = (chunk > 0).astype(F32)
        prev_re = pr_ref[7:8, :] * has_prev
        prev_im = pi_ref[7:8, :] * has_prev
        sp_re = jnp.where(row == 0, prev_re, pltpu.roll(s_re, 1, 0))
        sp_im = jnp.where(row == 0, prev_im, pltpu.roll(s_im, 1, 0))
        p_ar = _rowsum(lr * sp_re + li * sp_im)
        p_ai = _rowsum(li * sp_re - lr * sp_im)

        lrb, lib, ub = lr.astype(BF16), li.astype(BF16), u.astype(BF16)
        du = dy * d_ref[...] + nt_dot(lrb, b_re[...]) + nt_dot(lib, b_im[...])
        du_ref[...] = du.astype(BF16)
        tdot = lambda p, q: lax.dot_general(p, q, (((0,), (0,)), ((), ())), preferred_element_type=F32)
        p_br, p_bi = tdot(ub, lrb), tdot(ub, lib)
        p_cr, p_ci = tdot(dyb, s_re.astype(BF16)), -tdot(dyb, s_im.astype(BF16))
        p_dd = _rowsum(dy * u)

        @pl.when(i == 0)
        def _():
            dar_ref[...] = p_ar
            dai_ref[...] = p_ai
            acc_br[...] = p_br
            acc_bi[...] = p_bi
            acc_cr[...] = p_cr
            acc_ci[...] = p_ci
            dd_ref[...] = p_dd

        @pl.when(i > 0)
        def _():
            dar_ref[...] += p_ar
            dai_ref[...] += p_ai
            acc_br[...] += p_br
            acc_bi[...] += p_bi
            acc_cr[...] += p_cr
            acc_ci[...] += p_ci
            dd_ref[...] += p_dd

        @pl.when(i == nchunk - 1)
        def _():
            dbr_ref[...] = _collapse_groups(acc_br[...])
            dbi_ref[...] = _collapse_groups(acc_bi[...])
            dcr_ref[...] = _collapse_groups(acc_cr[...])
            dci_ref[...] = _collapse_groups(acc_ci[...])

    rev = lambda k, i: (nchunk - 1 - i, k)
    prev = lambda k, i: (jnp.maximum((nchunk - 1 - i) * tail - 1, 0), k)
    kspec = lambda shape: pl.BlockSpec((None,) + shape, lambda k, i: (k, 0, 0))
    compact = pl.BlockSpec((H, S), lambda k, i: (0, k))
    compact_shape = jax.ShapeDtypeStruct((H, NK * S), F32)
    res = pl.pallas_call(
        body, name=name, grid=grid,
        in_specs=[pl.BlockSpec((T, C), rev), pl.BlockSpec((T, C), rev), pl.BlockSpec((T, C), rev),
                  pl.BlockSpec((T, C), rev), pl.BlockSpec((T, S), rev), pl.BlockSpec((T, S), rev),
                  pl.BlockSpec((8, S), prev), pl.BlockSpec((8, S), prev),
                  compact, compact, compact, compact,
                  kspec((1, S)), kspec((1, S)), kspec((1, C))] + bg_in_specs,
        out_specs=[pl.BlockSpec((T, C), rev), compact, compact, compact, compact,
                   kspec((1, S)), kspec((1, S)), kspec((1, C))] + bg_out_specs,
        out_shape=[jax.ShapeDtypeStruct((L, NK * C), BF16),
                   compact_shape, compact_shape, compact_shape, compact_shape,
                   jax.ShapeDtypeStruct((NK, 1, S), F32), jax.ShapeDtypeStruct((NK, 1, S), F32),
                   jax.ShapeDtypeStruct((NK, 1, C), F32)] + bg_out_shapes,
        scratch_shapes=[pltpu.VMEM((1, S), F32), pltpu.VMEM((1, S), F32),
                        pltpu.VMEM((T, S), F32), pltpu.VMEM((T, S), F32)]
        + [pltpu.VMEM((C, S), BF16)] * 4 + [pltpu.VMEM((C, S), F32)] * 4 + bg_scratch,
        compiler_params=_params(_semantics(bg, ("parallel", "arbitrary"))),
    )(z, y0, dd_direct, dd_mm, s_re, s_im, s_re, s_im, bt_re, bt_im, ct_re, ct_im, a_re, a_im, d, *bg_args)
    return _results(res, 8, bg)


def _position():
    return lax.axis_index("x"), lax.axis_index("y"), lax.axis_index("c")


def _other_chips(x, y):
    return [(1 - x, y), (x, 1 - y), (1 - x, 1 - y)]


def _gather_phases(n, rows=None):
    def parts(ins, outs, sems):
        send_sems, recv_sems, local_sems = sems
        x, y, c = _position()
        me, sibling = (x, y, c), (x, y, 1 - c)
        chips = _other_chips(x, y)

        def block(a, pos):
            index = 4 * pos[0] + 2 * pos[1] + pos[2]
            if rows is not None:
                return outs[a].at[index, pl.ds(*rows)]
            return outs[a].at[pl.ds(index, 1)] if _is_row(ins[a]) else outs[a].at[index]

        def copy(a, k, pos, to, src=None):
            return pltpu.make_async_remote_copy(
                src_ref=block(a, pos) if src is None else src, dst_ref=block(a, pos),
                send_sem=send_sems.at[7 * a + k], recv_sem=recv_sems.at[7 * a + k],
                device_id=to, device_id_type=MESH)

        shard = [ins[a] if rows is None else ins[a].at[pl.ds(*rows)] for a in range(n)]
        mine = [pltpu.make_async_copy(shard[a], block(a, me), local_sems.at[a]) for a in range(n)]
        first = []
        for a in range(n):
            first.append(copy(a, 0, me, sibling, src=shard[a]))
            first += [copy(a, 1 + j, me, (*chip, c), src=shard[a]) for j, chip in enumerate(chips)]
        passed = [copy(a, 4 + j, (*chip, c), sibling) for a in range(n) for j, chip in enumerate(chips)]
        arrived = [copy(a, 1 + j, (*chip, c), me) for a in range(n) for j, chip in enumerate(chips)]
        from_sibling = []
        for a in range(n):
            from_sibling.append(copy(a, 0, sibling, me))
            from_sibling += [copy(a, 4 + j, (*chip, 1 - c), me) for j, chip in enumerate(chips)]
        return mine, first, passed, arrived, from_sibling

    def send(ins, outs, sems):
        mine, first, _, _, _ = parts(ins, outs, sems)
        for cp in mine + first:
            cp.start()

    def forward(ins, outs, sems):
        _, _, passed, arrived, _ = parts(ins, outs, sems)
        for got, fwd in zip(arrived, passed):
            got.wait_recv()
            fwd.start()

    def finish(ins, outs, sems):
        mine, first, passed, _, from_sibling = parts(ins, outs, sems)
        for cp in from_sibling:
            cp.wait_recv()
        for cp in first + passed:
            cp.wait_send()
        for cp in mine:
            cp.wait()

    return [(0.0, send), (GATHER_FORWARD_AT, forward), (1.0, finish)]


def _is_row(a):
    return len(a.shape) == 2 and a.shape[0] == 1


def _gather_shapes(shards):
    n = len(shards)
    return ([jax.ShapeDtypeStruct((N_DEV,) + (s.shape[1:] if _is_row(s) else s.shape), s.dtype) for s in shards],
            [pltpu.SemaphoreType.DMA((7 * n,)), pltpu.SemaphoreType.DMA((7 * n,)), pltpu.SemaphoreType.DMA((n,))])


def gather_background(shards, rows=None, into=None):
    out_shapes, scratch = _gather_shapes(shards)
    bg = Background(list(shards) + list(into or []), out_shapes, scratch, _gather_phases(len(shards), rows))
    bg.aliases = {len(shards) + k: k for k in range(len(into or []))}
    return bg


def all_gather_blocks(shards, *, name):
    n = len(shards)
    out_shapes, scratch = _gather_shapes(shards)

    def body(*refs):
        for _, phase in _gather_phases(n):
            phase(refs[:n], refs[n:2 * n], refs[2 * n:])

    return pl.pallas_call(
        body, name=name, in_specs=[ANY] * n, out_specs=[ANY] * n, out_shape=out_shapes, scratch_shapes=scratch,
    )(*shards)


def sibling_exchange(grads, *, name):
    n = len(grads)
    bg = sibling_exchange_background(grads)

    def body(*refs):
        for _, phase in bg.phases:
            phase(refs[:n], refs[n:2 * n], refs[2 * n:])

    return pl.pallas_call(
        body, name=name, in_specs=[ANY] * n, out_specs=[ANY] * n, out_shape=bg.out_shapes, scratch_shapes=bg.scratch,
    )(*grads)


def sibling_exchange_background(grads):
    n = len(grads)

    def copies(ins, outs, sems):
        x, y, c = _position()
        return [pltpu.make_async_remote_copy(
            src_ref=ins[a].at[2 * q + 1 - c], dst_ref=outs[a].at[q],
            send_sem=sems[0].at[4 * a + q], recv_sem=sems[1].at[4 * a + q],
            device_id=(x, y, 1 - c), device_id_type=MESH)
            for a in range(n) for q in range(4)]

    def send(ins, outs, sems):
        for cp in copies(ins, outs, sems):
            cp.start()

    def finish(ins, outs, sems):
        for cp in copies(ins, outs, sems):
            cp.wait()

    return Background(grads, [jax.ShapeDtypeStruct((4,) + g.shape[1:], g.dtype) for g in grads],
                      [pltpu.SemaphoreType.DMA((4 * n,)), pltpu.SemaphoreType.DMA((4 * n,))],
                      [(0.0, send), (1.0, finish)])


def _chip_exchange_phases(n):
    def copies(ins, outs, sems):
        x, y, c = _position()
        return [pltpu.make_async_remote_copy(
            src_ref=ins[a].at[2 * chip[0] + chip[1]], dst_ref=outs[a].at[j],
            send_sem=sems[0].at[3 * a + j], recv_sem=sems[1].at[3 * a + j],
            device_id=(*chip, c), device_id_type=MESH)
            for a in range(n) for j, chip in enumerate(_other_chips(x, y))]

    def send(ins, outs, sems):
        for cp in copies(ins, outs, sems):
            cp.start()

    def finish(ins, outs, sems):
        for cp in copies(ins, outs, sems):
            cp.wait()

    return [(0.0, send), (1.0, finish)]


def chip_exchange_background(parts):
    n = len(parts)
    return Background(parts, [jax.ShapeDtypeStruct((3,) + p.shape[1:], p.dtype) for p in parts],
                      [pltpu.SemaphoreType.DMA((3 * n,)), pltpu.SemaphoreType.DMA((3 * n,))],
                      _chip_exchange_phases(n))


def add_pairs(grads, theirs, core, *, name, tm=512):
    _, R, C = theirs.shape
    tm = _tile(R, tm, 16)

    def body(core_ref, a_ref, b_ref, o_ref):
        o_ref[...] = (a_ref[...].astype(F32) + b_ref[...].astype(F32)).astype(o_ref.dtype)

    spec = pl.BlockSpec((None, tm, C), lambda q, i, core_ref: (q, i, 0))
    return pl.pallas_call(
        body, name=name,
        grid_spec=pltpu.PrefetchScalarGridSpec(
            num_scalar_prefetch=1, grid=(4, R // tm),
            in_specs=[pl.BlockSpec((None, tm, C), lambda q, i, core_ref: (2 * q + core_ref[0], i, 0)), spec],
            out_specs=spec),
        out_shape=jax.ShapeDtypeStruct(theirs.shape, BF16),
        compiler_params=_params(("parallel", "parallel")),
    )(core, grads, theirs)


def _adamw(w, g, m, v):
    m = ADAM_B1 * m + (1.0 - ADAM_B1) * g
    v = ADAM_B2 * v + (1.0 - ADAM_B2) * (g * g)
    m_hat = m / (1.0 - ADAM_B1 ** ADAM_STEP)
    v_hat = v / (1.0 - ADAM_B2 ** ADAM_STEP)
    delta = -ADAM_LR * (m_hat / (jnp.sqrt(v_hat) + ADAM_EPS) + ADAM_WD * w)
    return delta, m, v


def adamw_sharded(w, m, v, grads, theirs, others, where, *, name, tm=256):
    R, C = w.shape
    tm = _tile(R, tm, 16)

    def body(where_ref, w_ref, m_ref, v_ref, a_ref, b_ref, o_ref, g_ref, d_ref, nm_ref, nv_ref):
        g = a_ref[...].astype(F32) + b_ref[...].astype(F32)
        for j in range(3):
            g = g + o_ref[j].astype(F32)
        g_ref[...] = g
        d_ref[...], nm_ref[...], nv_ref[...] = _adamw(w_ref[...], g, m_ref[...], v_ref[...])

    spec = pl.BlockSpec((tm, C), lambda i, where_ref: (i, 0))
    sd = jax.ShapeDtypeStruct((R, C), F32)
    return pl.pallas_call(
        body, name=name,
        grid_spec=pltpu.PrefetchScalarGridSpec(
            num_scalar_prefetch=1, grid=(R // tm,),
            in_specs=[spec, spec, spec,
                      pl.BlockSpec((None, tm, C), lambda i, where_ref: (where_ref[0], i, 0)),
                      pl.BlockSpec((None, tm, C), lambda i, where_ref: (where_ref[1], i, 0)),
                      pl.BlockSpec((3, tm, C), lambda i, where_ref: (0, i, 0))],
            out_specs=[spec, spec, spec, spec]),
        out_shape=[sd, sd, sd, sd],
        compiler_params=_params(("parallel",)),
    )(where, w, m, v, grads, theirs, others)


def sum_gathered(gathered, *, name):
    n = len(gathered)

    def body(*refs):
        for ga_ref, o_ref in zip(refs[:n], refs[n:]):
            rows = len(ga_ref.shape) == 2
            total = ga_ref[0:1] if rows else ga_ref[0]
            for dev in range(1, N_DEV):
                total = total + (ga_ref[dev:dev + 1] if rows else ga_ref[dev])
            o_ref[...] = total

    shapes = [jax.ShapeDtypeStruct((1,) + g.shape[1:] if g.ndim == 2 else g.shape[1:], F32) for g in gathered]
    return pl.pallas_call(body, name=name, out_shape=shapes, compiler_params=_params())(*gathered)


def adamw_replicated(ws, ms, vs, gs, *, name):
    n = len(ws)

    def body(*refs):
        w_refs, m_refs, v_refs, g_refs = refs[:n], refs[n:2 * n], refs[2 * n:3 * n], refs[3 * n:4 * n]
        outs = refs[4 * n:]
        for k in range(n):
            outs[k][...], outs[n + k][...], outs[2 * n + k][...] = _adamw(
                w_refs[k][...], g_refs[k][...], m_refs[k][...], v_refs[k][...])

    shapes = [jax.ShapeDtypeStruct(t.shape, F32) for t in ws]
    res = pl.pallas_call(body, name=name, out_shape=shapes * 3, compiler_params=_params())(*ws, *ms, *vs, *gs)
    return res[:n], res[n:2 * n], res[2 * n:]


SHARDED = ("w_in", "ssm_glu_w", "w_out", "w_ffn_in", "w_ffn_out", "w_ple_gate", "w_ple_proj")
SMALL_LAST = ("norm_mix_g",)
SMALL_WIDE = ("ssm_b_re", "ssm_b_im", "ssm_c_re", "ssm_c_im")
SMALL = ("ssm_lambda_re", "ssm_lambda_im", "ssm_log_step", "ssm_b_re", "ssm_b_im", "ssm_c_re",
         "ssm_c_im", "ssm_d", "ssm_glu_b", "sgu_ln_g", "sgu_ln_b", "sgu_w", "sgu_b", "out_norm_ssm_g",
         "out_norm_sgu_g", "norm_ffn_g", "norm_ple_g", "b_ple_gate", "final_norm_g")
WEIGHTS = ("norm_mix_g", "w_in", "ssm_lambda_re", "ssm_lambda_im", "ssm_log_step", "ssm_b_re", "ssm_b_im",
           "ssm_c_re", "ssm_c_im", "ssm_d", "ssm_glu_w", "ssm_glu_b", "sgu_ln_g", "sgu_ln_b", "sgu_w", "sgu_b",
           "out_norm_ssm_g", "out_norm_sgu_g", "w_out", "norm_ffn_g", "w_ffn_in", "w_ffn_out", "norm_ple_g",
           "w_ple_gate", "b_ple_gate", "w_ple_proj", "final_norm_g")


def _step(x, p, loss_target, w, m, v):
    L, D = x.shape[1], x.shape[2]
    x2d, p2d, tgt = x.reshape(L, D), p.reshape(L, -1), loss_target.reshape(L, D)
    d_ssm = w["ssm_glu_w"].shape[2]
    d_sgu = w["sgu_ln_g"].shape[1]
    G, P, H = w["ssm_b_re"].shape[1:]
    SG = min(SSM_SUPER, G)
    NK = G // SG
    row = lambda a: a.reshape(1, -1)

    shard2d = {n: w[n].reshape(w[n].shape[1:]) for n in SHARDED}
    shard_bf = {n: shard2d[n].astype(BF16) for n in SHARDED}
    (w_ple_blk,) = all_gather_blocks([shard_bf["w_ple_proj"]], name="gather_w_ple")
    bf = lambda t: t.astype(BF16)
    pp, (w_in_blk,) = mm_nn(bf(p2d), w_ple_blk, name="ple_proj", out_dtype=F32, tm=512, tn=512, tk=2048,
                            bg=gather_background([shard_bf["w_in"]]))
    w_in = jnp.transpose(w_in_blk, (1, 0, 2)).reshape(D, -1)
    F = shard2d["w_ffn_in"].shape[1] * 4

    lam_re, lam_im, log_step = w["ssm_lambda_re"][0], w["ssm_lambda_im"][0], w["ssm_log_step"][0].reshape(G, 1)
    a_re, a_im, q_re, q_im = disc_lambda_fwd(lam_re, lam_im, log_step, name="s5_discretise_lambda")
    bt_re = w["ssm_b_re"][0].transpose(2, 0, 1).reshape(H, G * P)
    bt_im = w["ssm_b_im"][0].transpose(2, 0, 1).reshape(H, G * P)
    bbar_re, bbar_im = disc_b_fwd(row(q_re), row(q_im), bt_re, bt_im, name="s5_discretise_b")
    ct_re = w["ssm_c_re"][0].transpose(1, 0, 2).reshape(H, G * P)
    ct_im = w["ssm_c_im"][0].transpose(1, 0, 2).reshape(H, G * P)
    a_re_k, a_im_k = a_re.reshape(NK, 1, SG * P), a_im.reshape(NK, 1, SG * P)
    d_k = w["ssm_d"][0].reshape(NK, 1, SG * H)

    h1 = norm_fwd(x2d, w["norm_mix_g"], name="norm_mix")
    z, (w_glu, w_out) = mm_nn(h1, w_in, name="in_proj", out_dtype=F32, tm=512, tn=1024, tk=2048,
                              bg=gather_background([shard_bf["ssm_glu_w"], shard_bf["w_out"]]))
    w_glu, w_out = w_glu.reshape(d_ssm, d_ssm), w_out.reshape(D, D)
    first_rows = (D * 11 // 16) // 16 * 16
    s5_mats = (bbar_re, bbar_im, ct_re, ct_im, a_re_k, a_im_k, d_k)
    (y0, s_re, s_im), (w_ffn_in_part,) = ssm_fwd(
        z, *s5_mats, name="s5_scan", bg=gather_background([shard_bf["w_ffn_in"]], rows=(0, first_rows)))
    ya1 = glu_pre(y0, name="s5_gelu")
    t_glu = mm_nn(ya1, w_glu, name="s5_glu_proj", out_dtype=F32, tm=512, tn=512, tk=2048)
    n_a = glu_post(y0, t_glu, w["ssm_glu_b"], w["out_norm_ssm_g"], name="s5_glu_norm")
    b_s3 = w["sgu_b"][0][:, :, None]
    n_b = sgu_fwd(z, w["sgu_ln_g"], w["sgu_ln_b"], w["sgu_w"][0], b_s3, w["out_norm_sgu_g"], name="sgu", d_sgu=d_sgu)
    ycat = jnp.concatenate([n_a, n_b], axis=1)
    x1, (w_ffn_in_blk,) = mm_nn(
        ycat, w_out, name="out_proj", out_dtype=F32, tm=512, tn=512, tk=2048, residual=x2d,
        bg=gather_background([shard_bf["w_ffn_in"]], rows=(first_rows, D - first_rows), into=[w_ffn_in_part]))
    h2 = norm_fwd(x1, w["norm_ffn_g"], name="norm_ffn")
    (act, gate_ff, up_ff), (w_ffn_out, w_gate) = ffn_in_swiglu(
        h2, w_ffn_in_blk, name="ffn_in_swiglu", tm=512,
        bg=gather_background([shard_bf["w_ffn_out"], shard_bf["w_ple_gate"]]))
    w_ffn_out, w_gate = w_ffn_out.reshape(F, D), w_gate.reshape(D, D)
    x2 = mm_nn(act, w_ffn_out, name="ffn_out", out_dtype=F32, tm=512, tn=1024, tk=F, residual=x1)
    h3 = norm_fwd(x2, w["norm_ple_g"], name="norm_ple")
    gpre = mm_nn(h3, w_gate, name="ple_gate", out_dtype=F32, tm=512, tn=1024, tk=2048)

    dx3, dpre, dpp, loss_part, d_final_g, d_b_gate = head_and_loss(
        x2, gpre, w["b_ple_gate"], pp, row(w["final_norm_g"]), tgt, name="head_and_loss")
    x_pos, y_pos, c_pos = _position()
    where = jnp.stack([4 * x_pos + 2 * y_pos + c_pos, 2 * x_pos + y_pos]).astype(jnp.int32)
    core = jnp.reshape(c_pos, (1,)).astype(jnp.int32)
    own, others = {}, {}

    def blocks(named):
        g8 = {n: t.reshape((N_DEV,) + shard2d[n].shape) for n, t in named.items()}
        return g8, sibling_exchange_background(list(g8.values()))

    def chip_sums(g8, theirs):
        own.update(zip(g8, zip(g8.values(), theirs)))
        return [add_pairs(g, t, core, name="chip_sum_" + n) for (n, g), t in zip(g8.items(), theirs)]

    d_w_gate = mm_tn(h3, dpre, name="d_w_ple_gate", out_dtype=BF16, tm=L, tko=1024, tno=1024)
    d_w_ple = mm_tn(bf(p2d), dpp, name="d_w_ple_proj", out_dtype=BF16, tm=L, tko=1024, tno=1024, out_blocks=N_DEV)
    g8_ple, bg = blocks({"w_ple_gate": d_w_gate, "w_ple_proj": d_w_ple})
    dh3, theirs = mm_nt(dpre, w_gate, name="d_h_ple", out_dtype=F32, tm=512, tko=1024, tc=2048, bg=bg)
    bg = chip_exchange_background(chip_sums(g8_ple, theirs))
    dx2, dx2b, d_ple_g = norm_bwd(dh3, x2, w["norm_ple_g"], dx3, name="d_norm_ple", want_bf16=True)
    d_w_ffn_out, got = mm_tn(act, dx2b, name="d_w_ffn_out", out_dtype=BF16, tm=L, tko=1408, tno=512, bg=bg)
    others.update(zip(g8_ple, got))
    g8_fo, bg = blocks({"w_ffn_out": d_w_ffn_out})
    dgate, dup = ffn_out_bwd_swiglu(dx2b, w_ffn_out, gate_ff, up_ff, name="d_act_swiglu")
    half = N_DEV // 2
    d_w_ffn_in, theirs = mm_tn(h2, dgate, name="d_w_ffn_in_gate", out_dtype=BF16, tm=L, tko=512, tno=1408,
                               out_blocks=half, total_blocks=N_DEV, bg=bg)
    bg = chip_exchange_background(chip_sums(g8_fo, theirs))
    d_w_ffn_in, got = mm_tn(h2, dup, name="d_w_ffn_in_up", out_dtype=BF16, tm=L, tko=512, tno=1408,
                            out_blocks=half, block_offset=half, total_blocks=N_DEV, into=d_w_ffn_in, bg=bg)
    others.update(zip(g8_fo, got))
    g8_fi, bg = blocks({"w_ffn_in": d_w_ffn_in})
    dh2, theirs = mm_nt(dgate, w_ffn_in_blk, a2=dup, name="d_h_ffn", out_dtype=F32, tm=1024, tko=1024, tc=1408, bg=bg)
    late_parts = chip_sums(g8_fi, theirs)
    dx1, dx1b, d_ffn_g = norm_bwd(dh2, x1, w["norm_ffn_g"], dx2, name="d_norm_ffn", want_bf16=True)
    dycat = mm_nt(dx1b, w_out, name="d_ycat", out_dtype=F32, tm=512, tko=1024, tc=2048)
    d_w_out = mm_tn(ycat, dx1b, name="d_w_out", out_dtype=BF16, tm=L, tko=1024, tno=1024)
    g8_out, bg = blocks({"w_out": d_w_out})
    dzu, dzv, d_sgu_w, d_sgu_b, d_ln_g, d_ln_b, d_g_b = sgu_bwd(
        z, dycat, w["sgu_ln_g"], w["sgu_ln_b"], w["sgu_w"][0], b_s3, w["out_norm_sgu_g"], name="d_sgu", d_sgu=d_sgu)
    dt_glu, dd_direct, d_g_a, d_glu_b = glu_post_bwd(
        y0, t_glu, w["ssm_glu_b"], w["out_norm_ssm_g"], dycat, name="d_s5_glu_norm")
    d_w_glu, theirs = mm_tn(ya1, dt_glu, name="d_w_glu", out_dtype=BF16, tm=L, tko=1024, tno=1024, bg=bg)
    late_parts += chip_sums(g8_out, theirs)
    g8_glu, bg = blocks({"ssm_glu_w": d_w_glu})
    dd_mm, theirs = mm_nt(dt_glu, w_glu, name="d_s5_glu_proj", out_dtype=F32, tm=512, tko=1024, tc=2048, bg=bg)
    late_parts += chip_sums(g8_glu, theirs)
    (du, d_bbar_re, d_bbar_im, d_ct_re, d_ct_im, d_a_re, d_a_im, d_d), got = ssm_bwd(
        z, y0, dd_direct, dd_mm, s_re, s_im, *s5_mats, name="d_s5_scan", bg=chip_exchange_background(late_parts))
    others.update(zip(("w_ffn_in", "w_out", "ssm_glu_w"), got))
    dz = jnp.concatenate([du, dzu, dzv], axis=1)

    d_q_re, d_q_im, d_bt_re, d_bt_im = disc_b_bwd(row(q_re), row(q_im), bt_re, bt_im, d_bbar_re, d_bbar_im,
                                                  name="d_s5_discretise_b")
    d_lam_re, d_lam_im, d_log_step = disc_lambda_bwd(
        lam_re, lam_im, log_step,
        (d_a_re.reshape(G, P), d_a_im.reshape(G, P), d_q_re.reshape(G, P), d_q_im.reshape(G, P)),
        name="d_s5_discretise_lambda")
    small_grads = {
        "ssm_lambda_re": d_lam_re, "ssm_lambda_im": d_lam_im, "ssm_log_step": d_log_step,
        "ssm_b_re": d_bt_re, "ssm_b_im": d_bt_im, "ssm_c_re": d_ct_re, "ssm_c_im": d_ct_im,
        "ssm_d": d_d, "ssm_glu_b": d_glu_b, "sgu_ln_g": d_ln_g, "sgu_ln_b": d_ln_b,
        "sgu_w": d_sgu_w, "sgu_b": d_sgu_b, "out_norm_ssm_g": d_g_a, "out_norm_sgu_g": d_g_b,
        "norm_ffn_g": d_ffn_g, "norm_ple_g": d_ple_g, "b_ple_gate": d_b_gate, "final_norm_g": d_final_g,
    }

    d_w_in, got = mm_tn(h1, dz, name="d_w_in", out_dtype=BF16, tm=L, tko=1024, tno=1024, out_blocks=N_DEV,
                        bg=gather_background([loss_part] + [small_grads[n] for n in SMALL]))
    sums = sum_gathered(got, name="sum_small_grads")
    g8_in, _ = blocks({"w_in": d_w_in})
    theirs = sibling_exchange(list(g8_in.values()), name="grads_to_sibling_w_in")
    dh1, got = mm_nt(dz, w_in, name="d_h_mix", out_dtype=F32, tm=512, tko=1024, tc=3 * d_sgu,
                     bg=chip_exchange_background(chip_sums(g8_in, theirs)))
    others.update(zip(g8_in, got))
    grad_x, d_mix_g = norm_bwd(dh1, x2d, w["norm_mix_g"], dx1, name="d_norm_mix", want_bf16=False)
    loss, small_sum = sums[0][0, 0], dict(zip(SMALL, sums[1:]))
    (small_sum["norm_mix_g"],) = sum_gathered(all_gather_blocks([d_mix_g], name="gather_last_grad"),
                                              name="sum_last_grad")

    out = {}
    for n in SHARDED:
        res = adamw_sharded(shard2d[n], m[n].reshape(shard2d[n].shape), v[n].reshape(shard2d[n].shape),
                            own[n][0], own[n][1], others[n], where, name="adamw_" + n)
        out[n] = [r.reshape(w[n].shape) for r in res]

    def work_shape(n):
        s = w[n].shape
        return (1,) + s if len(s) == 1 else (s if len(s) == 2 else s[1:])

    for n in ("ssm_b_re", "ssm_b_im"):
        small_sum[n] = small_sum[n].reshape(H, G, P).transpose(1, 2, 0)
    for n in ("ssm_c_re", "ssm_c_im"):
        small_sum[n] = small_sum[n].reshape(H, G, P).transpose(1, 0, 2)

    def replicated(names_, name):
        gs = [small_sum[n].reshape(work_shape(n)) for n in names_]
        res = adamw_replicated(*[[t[n].reshape(work_shape(n)) for n in names_] for t in (w, m, v)], gs, name=name)
        for i, n in enumerate(names_):
            out[n] = [r.reshape(w[n].shape) for r in (gs[i], res[0][i], res[1][i], res[2][i])]

    replicated([n for n in SMALL + SMALL_LAST if n not in SMALL_WIDE], "adamw_small")
    replicated(list(SMALL_WIDE), "adamw_s5_b_c")

    grads = [out[n][0] for n in WEIGHTS]
    deltas = [out[n][1] for n in WEIGHTS]
    new_m = [out[n][2] for n in WEIGHTS]
    new_v = [out[n][3] for n in WEIGHTS]
    return (loss, grad_x.reshape(x.shape), *grads, *deltas, *new_m, *new_v)


def kernel(x, p, norm_mix_g, w_in, ssm_lambda_re, ssm_lambda_im, ssm_log_step, ssm_b_re, ssm_b_im, ssm_c_re, ssm_c_im, ssm_d, ssm_glu_w, ssm_glu_b, sgu_ln_g, sgu_ln_b, sgu_w, sgu_b, out_norm_ssm_g, out_norm_sgu_g, w_out, norm_ffn_g, w_ffn_in, w_ffn_out, norm_ple_g, w_ple_gate, b_ple_gate, w_ple_proj, final_norm_g, loss_target, m_norm_mix_g, m_w_in, m_ssm_lambda_re, m_ssm_lambda_im, m_ssm_log_step, m_ssm_b_re, m_ssm_b_im, m_ssm_c_re, m_ssm_c_im, m_ssm_d, m_ssm_glu_w, m_ssm_glu_b, m_sgu_ln_g, m_sgu_ln_b, m_sgu_w, m_sgu_b, m_out_norm_ssm_g, m_out_norm_sgu_g, m_w_out, m_norm_ffn_g, m_w_ffn_in, m_w_ffn_out, m_norm_ple_g, m_w_ple_gate, m_b_ple_gate, m_w_ple_proj, m_final_norm_g, v_norm_mix_g, v_w_in, v_ssm_lambda_re, v_ssm_lambda_im, v_ssm_log_step, v_ssm_b_re, v_ssm_b_im, v_ssm_c_re, v_ssm_c_im, v_ssm_d, v_ssm_glu_w, v_ssm_glu_b, v_sgu_ln_g, v_sgu_ln_b, v_sgu_w, v_sgu_b, v_out_norm_ssm_g, v_out_norm_sgu_g, v_w_out, v_norm_ffn_g, v_w_ffn_in, v_w_ffn_out, v_norm_ple_g, v_w_ple_gate, v_b_ple_gate, v_w_ple_proj, v_final_norm_g):
    given = dict(locals())
    w = {n: given[n] for n in WEIGHTS}
    m = {n: given["m_" + n] for n in WEIGHTS}
    v = {n: given["v_" + n] for n in WEIGHTS}
    return _step(x, p, loss_target, w, m, v)
```

```python
import functools
import math

import jax
import jax.numpy as jnp
from jax import lax
from jax.experimental import pallas as pl
from jax.experimental.pallas import tpu as pltpu

F32 = jnp.float32
BF16 = jnp.bfloat16
MESH = pl.DeviceIdType.MESH
ANY = pl.BlockSpec(memory_space=pl.ANY)

N_DEV = 8
EPS = 1e-6
LAMBDA_RE_MAX = -1e-4
SSM_GROUP = 16
SSM_STATE = 64
SSM_SUPER = 16
SGU_CHUNK = 128
ADAM_LR, ADAM_B1, ADAM_B2, ADAM_EPS, ADAM_WD, ADAM_STEP = 0.001, 0.9, 0.999, 1e-08, 0.01, 10
VMEM_LIMIT = 52 * 1024 * 1024
LANE = 128
GATHER_FORWARD_AT = 0.85

_GELU_C = math.sqrt(2.0 / math.pi)


def _params(sem=None):
    return pltpu.CompilerParams(dimension_semantics=sem, vmem_limit_bytes=VMEM_LIMIT)


def _tile(dim, pref, unit=LANE):
    if dim <= pref:
        return dim
    t = (pref // unit) * unit
    while t >= unit:
        if dim % t == 0:
            return t
        t -= unit
    return dim


def _gelu(x):
    return 0.5 * x * (1.0 + jnp.tanh(_GELU_C * (x + 0.044715 * x * x * x)))


def _gelu_grad(x):
    t = jnp.tanh(_GELU_C * (x + 0.044715 * x * x * x))
    return 0.5 * (1.0 + t) + 0.5 * x * (1.0 - t * t) * (_GELU_C * (1.0 + 3.0 * 0.044715 * x * x))


def _gelu_and_grad(x):
    t = jnp.tanh(_GELU_C * (x + 0.044715 * x * x * x))
    return (0.5 * x * (1.0 + t),
            0.5 * (1.0 + t) + 0.5 * x * (1.0 - t * t) * (_GELU_C * (1.0 + 3.0 * 0.044715 * x * x)))


def _rms(x):
    return lax.rsqrt(jnp.mean(x * x, axis=-1, keepdims=True) + EPS)


def _rmsnorm_bwd(dy, x, r, g):
    dyg = dy * g
    return r * dyg - x * (r * r * r) * jnp.mean(dyg * x, axis=-1, keepdims=True)


def _rowsum(v):
    return jnp.sum(v, axis=0, keepdims=True)


class Background:
    def __init__(self, inputs, out_shapes, scratch, phases):
        self.inputs, self.out_shapes, self.scratch, self.phases = list(inputs), list(out_shapes), list(scratch), phases
        self.aliases = {}

    def emit(self, step, nsteps, ins, outs, scratch):
        for place, phase in self.phases:
            at = min(int(place * nsteps), nsteps - 1)

            @pl.when(step == at)
            def _():
                phase(ins, outs, scratch)


def _carrier(bg, n_in, n_out, n_scratch, grid):
    nbi = len(bg.inputs) if bg else 0
    nbo = len(bg.out_shapes) if bg else 0
    nsteps = math.prod(grid)

    def split(refs):
        ins = refs[:n_in]
        bg_ins = refs[n_in:n_in + nbi]
        outs = refs[n_in + nbi:n_in + nbi + n_out]
        bg_outs = refs[n_in + nbi + n_out:n_in + nbi + n_out + nbo]
        rest = refs[n_in + nbi + n_out + nbo:]
        scratch, bg_scratch = rest[:n_scratch], rest[n_scratch:]

        def run_background():
            if bg is None:
                return
            step = pl.program_id(0)
            for axis in range(1, len(grid)):
                step = step * grid[axis] + pl.program_id(axis)
            bg.emit(step, nsteps, bg_ins, bg_outs, bg_scratch)

        return ins, outs, scratch, run_background

    if bg is None:
        return [], [], [], [], [], split
    return [ANY] * nbi, list(bg.inputs), [ANY] * nbo, list(bg.out_shapes), list(bg.scratch), split


def _semantics(bg, sem):
    return tuple("arbitrary" for _ in sem) if bg is not None else sem


def _results(res, n_out, bg):
    res = list(res) if isinstance(res, (list, tuple)) else [res]
    own = res[0] if n_out == 1 else res[:n_out]
    return (own, res[n_out:]) if bg is not None else own


def mm_nn(a, b, *, name, out_dtype, tm, tn, tk, residual=None, bg=None):
    M, K = a.shape
    blocked = b.ndim == 3
    if blocked:
        nb, _, Nb = b.shape
        N = nb * Nb
        tn = _tile(Nb, tn)
        per = Nb // tn
    else:
        N = b.shape[1]
        tn = _tile(N, tn)
    tm, tk = _tile(M, tm, 8), _tile(K, tk)
    nj, ni, nk = N // tn, M // tm, K // tk
    has_res = residual is not None
    grid = (nj, ni, nk)
    bg_in_specs, bg_args, bg_out_specs, bg_out_shapes, bg_scratch, split = _carrier(
        bg, 3 if has_res else 2, 1, 0 if nk == 1 else 1, grid)

    def body(*refs):
        ins, (o_ref,), scratch, run_background = split(refs)
        run_background()
        a_ref, b_ref = ins[0], ins[1]
        r_ref = ins[2] if has_res else None

        def finish(acc):
            if has_res:
                acc = acc + r_ref[...]
            o_ref[...] = acc.astype(o_ref.dtype)

        part = jnp.dot(a_ref[...], b_ref[...], preferred_element_type=F32)
        if nk == 1:
            finish(part)
        else:
            acc_ref = scratch[0]
            k = pl.program_id(2)

            @pl.when(k == 0)
            def _():
                acc_ref[...] = part

            @pl.when(k > 0)
            def _():
                acc_ref[...] += part

            @pl.when(k == nk - 1)
            def _():
                finish(acc_ref[...])

    if blocked:
        b_spec = pl.BlockSpec((None, tk, tn), lambda j, i, k: (j // per, k, j % per))
    else:
        b_spec = pl.BlockSpec((tk, tn), lambda j, i, k: (k, j))
    in_specs = [pl.BlockSpec((tm, tk), lambda j, i, k: (i, k)), b_spec]
    args = [a, b]
    if has_res:
        in_specs.append(pl.BlockSpec((tm, tn), lambda j, i, k: (i, j)))
        args.append(residual)
    res = pl.pallas_call(
        body, name=name, grid=grid,
        in_specs=in_specs + bg_in_specs,
        out_specs=[pl.BlockSpec((tm, tn), lambda j, i, k: (i, j))] + bg_out_specs,
        out_shape=[jax.ShapeDtypeStruct((M, N), out_dtype)] + bg_out_shapes,
        input_output_aliases={len(args) + k: 1 + o for k, o in (bg.aliases if bg else {}).items()},
        scratch_shapes=([] if nk == 1 else [pltpu.VMEM((tm, tn), F32)]) + bg_scratch,
        compiler_params=_params(_semantics(bg, ("parallel", "parallel", "arbitrary"))),
    )(*args, *bg_args)
    return _results(res, 1, bg)


def mm_nt(a, w, *, name, out_dtype, tm, tko, tc, a2=None, bg=None):
    M, N = a.shape
    if a2 is not None:
        N = 2 * N
    blocked = w.ndim == 3
    if blocked:
        nb, Ko, Nb = w.shape
        tc = _tile(Nb, tc)
        per = Nb // tc
    else:
        Ko = w.shape[0]
        tc = _tile(N, tc)
    tm, tko = _tile(M, tm, 8), _tile(Ko, tko)
    njo, ni, nc = Ko // tko, M // tm, N // tc
    grid = (njo, ni, nc)
    half = nc // 2
    bg_in_specs, bg_args, bg_out_specs, bg_out_shapes, bg_scratch, split = _carrier(
        bg, 2 if a2 is None else 3, 1, 0 if nc == 1 else 1, grid)

    def body(*refs):
        ins, (o_ref,), scratch, run_background = split(refs)
        run_background()
        a_val = ins[0][...]
        if a2 is not None:
            a_val = jnp.where(pl.program_id(2) < half, a_val, ins[1][...])
        part = lax.dot_general(a_val, ins[-1][...], (((1,), (1,)), ((), ())),
                               preferred_element_type=F32)
        if nc == 1:
            o_ref[...] = part.astype(o_ref.dtype)
        else:
            acc_ref = scratch[0]
            c = pl.program_id(2)

            @pl.when(c == 0)
            def _():
                acc_ref[...] = part

            @pl.when(c > 0)
            def _():
                acc_ref[...] += part

            @pl.when(c == nc - 1)
            def _():
                o_ref[...] = acc_ref[...].astype(o_ref.dtype)

    if blocked:
        w_spec = pl.BlockSpec((None, tko, tc), lambda j, i, c: (c // per, j, c % per))
    else:
        w_spec = pl.BlockSpec((tko, tc), lambda j, i, c: (j, c))
    if a2 is None:
        a_specs, a_args = [pl.BlockSpec((tm, tc), lambda j, i, c: (i, c))], [a]
    else:
        a_specs = [pl.BlockSpec((tm, tc), lambda j, i, c: (i, jnp.minimum(c, half - 1))),
                   pl.BlockSpec((tm, tc), lambda j, i, c: (i, jnp.maximum(c - half, 0)))]
        a_args = [a, a2]
    res = pl.pallas_call(
        body, name=name, grid=grid,
        in_specs=a_specs + [w_spec] + bg_in_specs,
        out_specs=[pl.BlockSpec((tm, tko), lambda j, i, c: (i, j))] + bg_out_specs,
        out_shape=[jax.ShapeDtypeStruct((M, Ko), out_dtype)] + bg_out_shapes,
        scratch_shapes=([] if nc == 1 else [pltpu.VMEM((tm, tko), F32)]) + bg_scratch,
        compiler_params=_params(_semantics(bg, ("parallel", "parallel", "arbitrary"))),
    )(*a_args, w, *bg_args)
    return _results(res, 1, bg)


def mm_tn(a, g, *, name, out_dtype, tm, tko, tno, out_blocks=None, block_offset=0, total_blocks=None, into=None,
          bg=None):
    M, K = a.shape
    N = g.shape[1]
    if out_blocks:
        Nb = N // out_blocks
        tno = _tile(Nb, tno)
        per = Nb // tno
    else:
        tno = _tile(N, tno)
    tm, tko = _tile(M, tm), _tile(K, tko)
    njo, njn, nm = K // tko, N // tno, M // tm
    grid = (njo, njn, nm)
    bg_in_specs, bg_args, bg_out_specs, bg_out_shapes, bg_scratch, split = _carrier(
        bg, 2 if into is None else 3, 1, 0 if nm == 1 else 1, grid)

    def body(*refs):
        ins, (o_ref,), scratch, run_background = split(refs)
        a_ref, g_ref = ins[0], ins[1]
        run_background()
        part = lax.dot_general(a_ref[...], g_ref[...], (((0,), (0,)), ((), ())),
                               preferred_element_type=F32)
        if nm == 1:
            o_ref[...] = part.astype(o_ref.dtype)
        else:
            acc_ref = scratch[0]
            m = pl.program_id(2)

            @pl.when(m == 0)
            def _():
                acc_ref[...] = part

            @pl.when(m > 0)
            def _():
                acc_ref[...] += part

            @pl.when(m == nm - 1)
            def _():
                o_ref[...] = acc_ref[...].astype(o_ref.dtype)

    if out_blocks:
        o_spec = pl.BlockSpec((None, tko, tno), lambda jo, jn, m: (jn // per + block_offset, jo, jn % per))
        o_shape = jax.ShapeDtypeStruct((total_blocks or out_blocks, K, Nb), out_dtype)
    else:
        o_spec = pl.BlockSpec((tko, tno), lambda jo, jn, m: (jo, jn))
        o_shape = jax.ShapeDtypeStruct((K, N), out_dtype)
    res = pl.pallas_call(
        body, name=name, grid=grid,
        in_specs=[pl.BlockSpec((tm, tko), lambda jo, jn, m: (m, jo)),
                  pl.BlockSpec((tm, tno), lambda jo, jn, m: (m, jn))] + ([] if into is None else [ANY]) + bg_in_specs,
        out_specs=[o_spec] + bg_out_specs, out_shape=[o_shape] + bg_out_shapes,
        scratch_shapes=([] if nm == 1 else [pltpu.VMEM((tko, tno), F32)]) + bg_scratch,
        input_output_aliases={} if into is None else {2: 0},
        compiler_params=_params(_semantics(bg, ("parallel", "parallel", "arbitrary"))),
    )(a, g, *([] if into is None else [into]), *bg_args)
    return _results(res, 1, bg)


def ffn_in_swiglu(h, w_blk, *, name, tm, bg=None):
    M, K = h.shape
    nb, _, Nb = w_blk.shape
    nh = nb // 2
    F = nh * Nb
    tm = _tile(M, tm, 8)
    grid = (nh, M // tm)
    bg_in_specs, bg_args, bg_out_specs, bg_out_shapes, bg_scratch, split = _carrier(bg, 3, 3, 0, grid)

    def body(*refs):
        (h_ref, wg_ref, wu_ref), (act_ref, gate_ref, up_ref), _, run_background = split(refs)
        run_background()
        hv = h_ref[...]
        gate = jnp.dot(hv, wg_ref[...], preferred_element_type=F32)
        up = jnp.dot(hv, wu_ref[...], preferred_element_type=F32)
        gate_ref[...] = gate
        up_ref[...] = up
        act_ref[...] = (gate * jax.nn.sigmoid(gate) * up).astype(act_ref.dtype)

    o_spec = pl.BlockSpec((tm, Nb), lambda j, i: (i, j))
    res = pl.pallas_call(
        body, name=name, grid=grid,
        in_specs=[pl.BlockSpec((tm, K), lambda j, i: (i, 0)),
                  pl.BlockSpec((None, K, Nb), lambda j, i: (j, 0, 0)),
                  pl.BlockSpec((None, K, Nb), lambda j, i: (j + nh, 0, 0))] + bg_in_specs,
        out_specs=[o_spec, o_spec, o_spec] + bg_out_specs,
        out_shape=[jax.ShapeDtypeStruct((M, F), BF16), jax.ShapeDtypeStruct((M, F), F32),
                   jax.ShapeDtypeStruct((M, F), F32)] + bg_out_shapes,
        scratch_shapes=bg_scratch,
        compiler_params=_params(_semantics(bg, ("parallel", "parallel"))),
    )(h, w_blk, w_blk, *bg_args)
    return _results(res, 3, bg)


def _row_spec(tm, d, col=0):
    return pl.BlockSpec((tm, d), lambda i: (i, col))


def _vec_spec(d):
    return pl.BlockSpec((1, d), lambda i: (0, 0))


def norm_fwd(x, g, *, name, tm=512):
    L, D = x.shape
    tm = _tile(L, tm, 8)

    def body(x_ref, g_ref, h_ref):
        xv = x_ref[...]
        h_ref[...] = (xv * _rms(xv) * g_ref[...]).astype(h_ref.dtype)

    return pl.pallas_call(
        body, name=name, grid=(L // tm,),
        in_specs=[_row_spec(tm, D), _vec_spec(D)],
        out_specs=_row_spec(tm, D),
        out_shape=jax.ShapeDtypeStruct((L, D), BF16),
        compiler_params=_params(("parallel",)),
    )(x, g)


def norm_bwd(dh, xin, g, dres, *, name, want_bf16, tm=512):
    L, D = xin.shape
    tm = _tile(L, tm, 8)

    def body(dh_ref, x_ref, g_ref, dres_ref, dx_ref, *rest):
        dg_ref = rest[-1]
        xv, dhv = x_ref[...], dh_ref[...]
        r = _rms(xv)
        dx = dres_ref[...] + _rmsnorm_bwd(dhv, xv, r, g_ref[...])
        dx_ref[...] = dx
        if want_bf16:
            rest[0][...] = dx.astype(BF16)
        part = _rowsum(dhv * xv * r)

        @pl.when(pl.program_id(0) == 0)
        def _():
            dg_ref[...] = part

        @pl.when(pl.program_id(0) > 0)
        def _():
            dg_ref[...] += part

    out_specs = [_row_spec(tm, D)] + ([_row_spec(tm, D)] if want_bf16 else []) + [_vec_spec(D)]
    out_shape = ([jax.ShapeDtypeStruct((L, D), F32)]
                 + ([jax.ShapeDtypeStruct((L, D), BF16)] if want_bf16 else [])
                 + [jax.ShapeDtypeStruct((1, D), F32)])
    return pl.pallas_call(
        body, name=name, grid=(L // tm,),
        in_specs=[_row_spec(tm, D), _row_spec(tm, D), _vec_spec(D), _row_spec(tm, D)],
        out_specs=out_specs, out_shape=out_shape,
        compiler_params=_params(("arbitrary",)),
    )(dh, xin, g, dres)


def glu_pre(y0, *, name, tm=512):
    L, D = y0.shape
    tm = _tile(L, tm, 8)

    def body(y_ref, o_ref):
        o_ref[...] = _gelu(y_ref[...]).astype(o_ref.dtype)

    return pl.pallas_call(
        body, name=name, grid=(L // tm,),
        in_specs=[_row_spec(tm, D)], out_specs=_row_spec(tm, D),
        out_shape=jax.ShapeDtypeStruct((L, D), BF16),
        compiler_params=_params(("parallel",)),
    )(y0)


def glu_post(y0, t, b_glu, g_a, *, name, tm=512):
    L, D = y0.shape
    tm = _tile(L, tm, 8)

    def body(y_ref, t_ref, b_ref, g_ref, o_ref):
        ya = _gelu(y_ref[...]) * jax.nn.sigmoid(t_ref[...] + b_ref[...])
        o_ref[...] = (ya * _rms(ya) * g_ref[...]).astype(o_ref.dtype)

    return pl.pallas_call(
        body, name=name, grid=(L // tm,),
        in_specs=[_row_spec(tm, D), _row_spec(tm, D), _vec_spec(D), _vec_spec(D)],
        out_specs=_row_spec(tm, D),
        out_shape=jax.ShapeDtypeStruct((L, D), BF16),
        compiler_params=_params(("parallel",)),
    )(y0, t, b_glu, g_a)


def glu_post_bwd(y0, t, b_glu, g_a, dycat, *, name, tm=512):
    L, D = y0.shape
    tm = _tile(L, tm, 8)

    def body(y_ref, t_ref, b_ref, g_ref, dn_ref, dt_ref, dd_ref, dga_ref, dbg_ref):
        ya1 = _gelu(y_ref[...])
        sg = jax.nn.sigmoid(t_ref[...] + b_ref[...])
        ya = ya1 * sg
        ra = _rms(ya)
        dn = dn_ref[...]
        dya = _rmsnorm_bwd(dn, ya, ra, g_ref[...])
        dt = dya * ya1 * sg * (1.0 - sg)
        dt_ref[...] = dt.astype(BF16)
        dd_ref[...] = dya * sg
        p_ga, p_bg = _rowsum(dn * ya * ra), _rowsum(dt)

        @pl.when(pl.program_id(0) == 0)
        def _():
            dga_ref[...] = p_ga
            dbg_ref[...] = p_bg

        @pl.when(pl.program_id(0) > 0)
        def _():
            dga_ref[...] += p_ga
            dbg_ref[...] += p_bg

    return pl.pallas_call(
        body, name=name, grid=(L // tm,),
        in_specs=[_row_spec(tm, D), _row_spec(tm, D), _vec_spec(D), _vec_spec(D), _row_spec(tm, D, 0)],
        out_specs=[_row_spec(tm, D), _row_spec(tm, D), _vec_spec(D), _vec_spec(D)],
        out_shape=[jax.ShapeDtypeStruct((L, D), BF16), jax.ShapeDtypeStruct((L, D), F32),
                   jax.ShapeDtypeStruct((1, D), F32), jax.ShapeDtypeStruct((1, D), F32)],
        compiler_params=_params(("arbitrary",)),
    )(y0, t, b_glu, g_a, dycat)


def head_and_loss(x2, gpre, b_g, pp, g_f, tgt, *, name, tm=256):
    L, D = x2.shape
    tm = _tile(L, tm, 8)

    def body(x2_ref, gp_ref, bg_ref, pp_ref, gf_ref, tg_ref,
             dx3_ref, dpre_ref, dpp_ref, loss_ref, dgf_ref, dbg_ref):
        gate = jax.nn.sigmoid(gp_ref[...] + bg_ref[...])
        ppv = pp_ref[...]
        x3 = x2_ref[...] + gate * ppv
        r = _rms(x3)
        xn = x3 * r
        gf = gf_ref[...]
        err = xn * gf - tg_ref[...]
        loss = 0.5 * jnp.sum(jnp.mean(err * err, axis=-1, keepdims=True), axis=0, keepdims=True)
        dout = err * (1.0 / D)
        dx3 = _rmsnorm_bwd(dout, x3, r, gf)
        dx3_ref[...] = dx3
        dpre = dx3 * ppv * gate * (1.0 - gate)
        dpre_ref[...] = dpre.astype(BF16)
        dpp_ref[...] = (dx3 * gate).astype(BF16)
        p_gf, p_bg = _rowsum(dout * xn), _rowsum(dpre)
        p_loss = jnp.broadcast_to(loss, loss_ref.shape)

        @pl.when(pl.program_id(0) == 0)
        def _():
            loss_ref[...] = p_loss
            dgf_ref[...] = p_gf
            dbg_ref[...] = p_bg

        @pl.when(pl.program_id(0) > 0)
        def _():
            loss_ref[...] += p_loss
            dgf_ref[...] += p_gf
            dbg_ref[...] += p_bg

    rs = _row_spec(tm, D)
    return pl.pallas_call(
        body, name=name, grid=(L // tm,),
        in_specs=[rs, rs, _vec_spec(D), rs, _vec_spec(D), rs],
        out_specs=[rs, rs, rs, pl.BlockSpec((8, LANE), lambda i: (0, 0)), _vec_spec(D), _vec_spec(D)],
        out_shape=[jax.ShapeDtypeStruct((L, D), F32), jax.ShapeDtypeStruct((L, D), BF16),
                   jax.ShapeDtypeStruct((L, D), BF16), jax.ShapeDtypeStruct((8, LANE), F32),
                   jax.ShapeDtypeStruct((1, D), F32), jax.ShapeDtypeStruct((1, D), F32)],
        compiler_params=_params(("arbitrary",)),
    )(x2, gpre, b_g, pp, g_f, tgt)


def ffn_out_bwd_swiglu(dx, w, gate, up, *, name, tm=512, tf=1408):
    M, D = dx.shape
    F = w.shape[0]
    tm, tf = _tile(M, tm, 8), _tile(F, tf)

    def body(dx_ref, w_ref, g_ref, u_ref, dg_ref, du_ref):
        da = lax.dot_general(dx_ref[...], w_ref[...], (((1,), (1,)), ((), ())), preferred_element_type=F32)
        gv = g_ref[...]
        sg = jax.nn.sigmoid(gv)
        dg_ref[...] = (da * u_ref[...] * sg * (1.0 + gv * (1.0 - sg))).astype(BF16)
        du_ref[...] = (da * gv * sg).astype(BF16)

    spec = pl.BlockSpec((tm, tf), lambda j, i: (i, j))
    return pl.pallas_call(
        body, name=name, grid=(F // tf, M // tm),
        in_specs=[pl.BlockSpec((tm, D), lambda j, i: (i, 0)), pl.BlockSpec((tf, D), lambda j, i: (j, 0)), spec, spec],
        out_specs=[spec, spec],
        out_shape=[jax.ShapeDtypeStruct((M, F), BF16), jax.ShapeDtypeStruct((M, F), BF16)],
        compiler_params=_params(("parallel", "parallel")),
    )(dx, w, gate, up)


def _sgu_forward_values(u1, v1, lng, lnb, w_ref, bs_ref, s_scr, heads, hd):
    xc = v1 - jnp.mean(v1, axis=-1, keepdims=True)
    r = lax.rsqrt(jnp.mean(xc * xc, axis=-1, keepdims=True) + EPS)
    xhat = xc * r
    v2 = xhat * lng + lnb
    tril = (lax.broadcasted_iota(jnp.int32, (SGU_CHUNK, SGU_CHUNK), 0)
            >= lax.broadcasted_iota(jnp.int32, (SGU_CHUNK, SGU_CHUNK), 1))
    for h in range(heads):
        wm = jnp.where(tril, w_ref[h], 0.0).astype(BF16)
        cols = slice(h * hd, (h + 1) * hd)
        s_scr[:, cols] = jnp.dot(wm, v2[:, cols].astype(BF16), preferred_element_type=F32) + bs_ref[h]
    return xhat, r, v2, tril


def sgu_fwd(z, ln_g, ln_b, w_s, b_s, g_b, *, name, d_sgu):
    L = z.shape[0]
    heads = w_s.shape[0]
    hd = d_sgu // heads

    def body(zu_ref, zv_ref, lng_ref, lnb_ref, w_ref, bs_ref, gb_ref, o_ref, s_scr):
        u1 = _gelu(zu_ref[...])
        _sgu_forward_values(u1, _gelu(zv_ref[...]), lng_ref[...], lnb_ref[...], w_ref, bs_ref, s_scr, heads, hd)
        yb = u1 * s_scr[...]
        o_ref[...] = (yb * _rms(yb) * gb_ref[...]).astype(o_ref.dtype)

    blk = lambda col: pl.BlockSpec((SGU_CHUNK, d_sgu), lambda n: (n, col))
    return pl.pallas_call(
        body, name=name, grid=(L // SGU_CHUNK,),
        in_specs=[blk(1), blk(2), _vec_spec(d_sgu), _vec_spec(d_sgu),
                  pl.BlockSpec(w_s.shape, lambda n: (0, 0, 0)), pl.BlockSpec(b_s.shape, lambda n: (0, 0, 0)),
                  _vec_spec(d_sgu)],
        out_specs=blk(0),
        out_shape=jax.ShapeDtypeStruct((L, d_sgu), BF16),
        scratch_shapes=[pltpu.VMEM((SGU_CHUNK, d_sgu), F32)],
        compiler_params=_params(("parallel",)),
    )(z, z, ln_g, ln_b, w_s, b_s, g_b)


def sgu_bwd(z, dycat, ln_g, ln_b, w_s, b_s, g_b, *, name, d_sgu):
    L = z.shape[0]
    heads = w_s.shape[0]
    hd = d_sgu // heads

    def body(zu_ref, zv_ref, dn_ref, lng_ref, lnb_ref, w_ref, bs_ref, gb_ref,
             dzu_ref, dzv_ref, dw_ref, dbs_ref, dlng_ref, dlnb_ref, dgb_ref, s_scr, dv_scr):
        first = pl.program_id(0) == 0
        lng = lng_ref[...]
        u1, du1 = _gelu_and_grad(zu_ref[...])
        v1, dv1_dz = _gelu_and_grad(zv_ref[...])
        xhat, r, v2, tril = _sgu_forward_values(u1, v1, lng, lnb_ref[...], w_ref, bs_ref, s_scr, heads, hd)
        s = s_scr[...]
        yb = u1 * s
        rb = _rms(yb)
        dn = dn_ref[...]
        dyb = _rmsnorm_bwd(dn, yb, rb, gb_ref[...])
        dzu_ref[...] = (dyb * s * du1).astype(BF16)
        ds = dyb * u1
        for h in range(heads):
            cols = slice(h * hd, (h + 1) * hd)
            ds_h = ds[:, cols]
            ds_hb = ds_h.astype(BF16)
            wm = jnp.where(tril, w_ref[h], 0.0).astype(BF16)
            dw_h = jnp.where(tril, lax.dot_general(ds_hb, v2[:, cols].astype(BF16), (((1,), (1,)), ((), ())),
                                                   preferred_element_type=F32), 0.0)
            db_h = jnp.sum(ds_h.T, axis=0, keepdims=True)
            dv_scr[:, cols] = lax.dot_general(wm, ds_hb, (((0,), (0,)), ((), ())), preferred_element_type=F32)

            @pl.when(first)
            def _():
                dw_ref[h] = dw_h
                dbs_ref[h] = db_h

            @pl.when(jnp.logical_not(first))
            def _():
                dw_ref[h] += dw_h
                dbs_ref[h] += db_h

        dv2 = dv_scr[...]
        dxh = dv2 * lng
        dv1 = r * (dxh - jnp.mean(dxh, axis=-1, keepdims=True)
                   - xhat * jnp.mean(dxh * xhat, axis=-1, keepdims=True))
        dzv_ref[...] = (dv1 * dv1_dz).astype(BF16)
        p_lng, p_lnb, p_gb = _rowsum(dv2 * xhat), _rowsum(dv2), _rowsum(dn * yb * rb)

        @pl.when(first)
        def _():
            dlng_ref[...] = p_lng
            dlnb_ref[...] = p_lnb
            dgb_ref[...] = p_gb

        @pl.when(jnp.logical_not(first))
        def _():
            dlng_ref[...] += p_lng
            dlnb_ref[...] += p_lnb
            dgb_ref[...] += p_gb

    blk = lambda col: pl.BlockSpec((SGU_CHUNK, d_sgu), lambda n: (n, col))
    full3 = lambda shape: pl.BlockSpec(shape, lambda n: (0, 0, 0))
    return pl.pallas_call(
        body, name=name, grid=(L // SGU_CHUNK,),
        in_specs=[blk(1), blk(2), blk(1), _vec_spec(d_sgu), _vec_spec(d_sgu),
                  full3(w_s.shape), full3(b_s.shape), _vec_spec(d_sgu)],
        out_specs=[blk(0), blk(0), full3(w_s.shape), full3((heads, 1, SGU_CHUNK)),
                   _vec_spec(d_sgu), _vec_spec(d_sgu), _vec_spec(d_sgu)],
        out_shape=[jax.ShapeDtypeStruct((L, d_sgu), BF16), jax.ShapeDtypeStruct((L, d_sgu), BF16),
                   jax.ShapeDtypeStruct(w_s.shape, F32), jax.ShapeDtypeStruct((heads, 1, SGU_CHUNK), F32),
                   jax.ShapeDtypeStruct((1, d_sgu), F32), jax.ShapeDtypeStruct((1, d_sgu), F32),
                   jax.ShapeDtypeStruct((1, d_sgu), F32)],
        scratch_shapes=[pltpu.VMEM((SGU_CHUNK, d_sgu), F32), pltpu.VMEM((SGU_CHUNK, d_sgu), F32)],
        compiler_params=_params(("arbitrary",)),
    )(z, z, dycat, ln_g, ln_b, w_s, b_s, g_b)


def _disc_lambda(lam_re, lam_im, log_step):
    lr = jnp.minimum(lam_re, LAMBDA_RE_MAX)
    li = lam_im
    dt = jnp.exp(log_step)
    mag = jnp.exp(lr * dt)
    ang = li * dt
    a_re = mag * jnp.cos(ang)
    a_im = mag * jnp.sin(ang)
    nr = a_re - 1.0
    ni = a_im
    den = lr * lr + li * li
    return a_re, a_im, (nr * lr + ni * li) / den, (ni * lr - nr * li) / den


def _disc_b(q_re, q_im, b_re, b_im):
    return q_re * b_re - q_im * b_im, q_re * b_im + q_im * b_re


def disc_lambda_fwd(lam_re, lam_im, log_step, *, name):
    def body(lr_ref, li_ref, ls_ref, ar_ref, ai_ref, qr_ref, qi_ref):
        ar_ref[...], ai_ref[...], qr_ref[...], qi_ref[...] = _disc_lambda(lr_ref[...], li_ref[...], ls_ref[...])

    sd = jax.ShapeDtypeStruct(lam_re.shape, F32)
    return pl.pallas_call(body, name=name, out_shape=[sd, sd, sd, sd], compiler_params=_params())(
        lam_re, lam_im, log_step)


def disc_lambda_bwd(lam_re, lam_im, log_step, cts, *, name):
    def body(lr_ref, li_ref, ls_ref, c0, c1, c2, c3, dlr_ref, dli_ref, dls_ref):
        _, vjp = jax.vjp(_disc_lambda, lr_ref[...], li_ref[...], ls_ref[...])
        dlr_ref[...], dli_ref[...], dls_ref[...] = vjp((c0[...], c1[...], c2[...], c3[...]))

    sd = jax.ShapeDtypeStruct(lam_re.shape, F32)
    return pl.pallas_call(body, name=name, out_shape=[sd, sd, jax.ShapeDtypeStruct(log_step.shape, F32)],
                          compiler_params=_params())(lam_re, lam_im, log_step, *cts)


def disc_b_fwd(q_re, q_im, b_re, b_im, *, name):
    def body(qr_ref, qi_ref, br_ref, bi_ref, or_ref, oi_ref):
        or_ref[...], oi_ref[...] = _disc_b(qr_ref[...], qi_ref[...], br_ref[...], bi_ref[...])

    sd = jax.ShapeDtypeStruct(b_re.shape, F32)
    return pl.pallas_call(body, name=name, out_shape=[sd, sd], compiler_params=_params())(q_re, q_im, b_re, b_im)


def disc_b_bwd(q_re, q_im, b_re, b_im, ct_re, ct_im, *, name):
    def body(qr_ref, qi_ref, br_ref, bi_ref, cr_ref, ci_ref, dqr_ref, dqi_ref, dbr_ref, dbi_ref):
        _, vjp = jax.vjp(_disc_b, qr_ref[...], qi_ref[...], br_ref[...], bi_ref[...])
        dqr_ref[...], dqi_ref[...], dbr_ref[...], dbi_ref[...] = vjp((cr_ref[...], ci_ref[...]))

    sq, sb = jax.ShapeDtypeStruct(q_re.shape, F32), jax.ShapeDtypeStruct(b_re.shape, F32)
    return pl.pallas_call(body, name=name, out_shape=[sq, sq, sb, sb], compiler_params=_params())(
        q_re, q_im, b_re, b_im, ct_re, ct_im)


def _lti_scan(xr, xi, ar, ai, reverse):
    T = xr.shape[0]
    row = lax.broadcasted_iota(jnp.int32, xr.shape, 0)
    k = 1
    while k < T:
        shift = T - k if reverse else k
        keep = (row < T - k) if reverse else (row >= k)
        sr = jnp.where(keep, pltpu.roll(xr, shift, 0), 0.0)
        si = jnp.where(keep, pltpu.roll(xi, shift, 0), 0.0)
        xr, xi = xr + ar * sr - ai * si, xi + ar * si + ai * sr
        ar, ai = ar * ar - ai * ai, 2.0 * ar * ai
        k *= 2
    return xr, xi


SUBLANES = 8


def _scan_rows(x_re, x_im, o_re, o_im, ar, ai, cr, ci, reverse):
    T, n = x_re.shape
    groups = T // SUBLANES
    row = lax.broadcasted_iota(jnp.int32, (SUBLANES, n), 0)
    edge = SUBLANES - 1 if reverse else 0
    pr, pi = _lti_scan(jnp.where(row == edge, ar, 0.0), jnp.where(row == edge, ai, 0.0), ar, ai, reverse)
    pows = []
    for level in range(3):
        k = 1 << level
        keep = (row < SUBLANES - k) if reverse else (row >= k)
        pows.append((jnp.where(keep, ar, 0.0), jnp.where(keep, ai, 0.0)))
        ar, ai = ar * ar - ai * ai, 2.0 * ar * ai

    def group(i, carry):
        cr, ci = carry
        at = pl.multiple_of((groups - 1 - i if reverse else i) * SUBLANES, SUBLANES)
        xr, xi = x_re[pl.ds(at, SUBLANES), :], x_im[pl.ds(at, SUBLANES), :]
        for level, (qr, qi) in enumerate(pows):
            shift = SUBLANES - (1 << level) if reverse else 1 << level
            sr, si = pltpu.roll(xr, shift, 0), pltpu.roll(xi, shift, 0)
            xr, xi = xr + qr * sr - qi * si, xi + qr * si + qi * sr
        xr, xi = xr + pr * cr - pi * ci, xi + pr * ci + pi * cr
        o_re[pl.ds(at, SUBLANES), :] = xr
        o_im[pl.ds(at, SUBLANES), :] = xi
        return spread(xr[last:last + 1, :]), spread(xi[last:last + 1, :])

    last = 0 if reverse else SUBLANES - 1
    spread = lambda v: jnp.broadcast_to(v, (SUBLANES, n))
    cr, ci = lax.fori_loop(0, groups, group, (spread(cr), spread(ci)), unroll=2)
    return cr[0:1, :], ci[0:1, :]


def _ssm_chunk(L):
    return _tile(L, 512, 8)


def _same_group(rows, cols):
    r = lax.broadcasted_iota(jnp.int32, (rows, cols), 0) // SSM_GROUP
    c = lax.broadcasted_iota(jnp.int32, (rows, cols), 1) // SSM_STATE
    return r == c


def _expand_groups(compact):
    H, S = compact.shape
    tiled = jnp.concatenate([compact] * (S // SSM_STATE), axis=0)
    return jnp.where(_same_group(tiled.shape[0], S), tiled, 0.0).astype(BF16)


def _collapse_groups(dense):
    C, S = dense.shape
    masked = jnp.where(_same_group(C, S), dense, 0.0)
    total = masked[0:SSM_GROUP]
    for g in range(1, C // SSM_GROUP):
        total = total + masked[g * SSM_GROUP:(g + 1) * SSM_GROUP]
    return total


def ssm_fwd(z, bt_re, bt_im, ct_re, ct_im, a_re, a_im, d, *, name, bg=None):
    L = z.shape[0]
    NK, _, S = a_re.shape
    H = bt_re.shape[0]
    C = S // SSM_STATE * SSM_GROUP
    T = _ssm_chunk(L)
    grid = (NK, L // T)
    bg_in_specs, bg_args, bg_out_specs, bg_out_shapes, bg_scratch, split = _carrier(bg, 8, 3, 8, grid)
    nt_dot = lambda p, q: lax.dot_general(p, q, (((1,), (1,)), ((), ())), preferred_element_type=F32)

    def body(*refs):
        ((u_ref, btr_ref, bti_ref, ctr_ref, cti_ref, ar_ref, ai_ref, d_ref), (y_ref, sr_ref, si_ref),
         (car_re, car_im, bu_re, bu_im, b_re, b_im, c_re, c_im), run_background) = split(refs)
        run_background()
        i = pl.program_id(1)
        ar, ai = ar_ref[...], ai_ref[...]

        @pl.when(i == 0)
        def _():
            car_re[...] = jnp.zeros_like(car_re)
            car_im[...] = jnp.zeros_like(car_im)
            b_re[...] = _expand_groups(btr_ref[...])
            b_im[...] = _expand_groups(bti_ref[...])
            c_re[...] = _expand_groups(ctr_ref[...])
            c_im[...] = _expand_groups(cti_ref[...])

        u = u_ref[...]
        ub = u.astype(BF16)
        bu_re[...] = jnp.dot(ub, b_re[...], preferred_element_type=F32)
        bu_im[...] = jnp.dot(ub, b_im[...], preferred_element_type=F32)
        car_re[...], car_im[...] = _scan_rows(bu_re, bu_im, sr_ref, si_ref, ar, ai, car_re[...], car_im[...], False)
        y_ref[...] = (nt_dot(sr_ref[...].astype(BF16), c_re[...]) - nt_dot(si_ref[...].astype(BF16), c_im[...])
                      + d_ref[...] * u)

    kspec = lambda shape: pl.BlockSpec((None,) + shape, lambda k, i: (k, 0, 0))
    compact = pl.BlockSpec((H, S), lambda k, i: (0, k))
    res = pl.pallas_call(
        body, name=name, grid=grid,
        in_specs=[pl.BlockSpec((T, C), lambda k, i: (i, k)), compact, compact, compact, compact,
                  kspec((1, S)), kspec((1, S)), kspec((1, C))] + bg_in_specs,
        out_specs=[pl.BlockSpec((T, C), lambda k, i: (i, k)),
                   pl.BlockSpec((T, S), lambda k, i: (i, k)), pl.BlockSpec((T, S), lambda k, i: (i, k))] + bg_out_specs,
        out_shape=[jax.ShapeDtypeStruct((L, NK * C), F32), jax.ShapeDtypeStruct((L, NK * S), F32),
                   jax.ShapeDtypeStruct((L, NK * S), F32)] + bg_out_shapes,
        scratch_shapes=[pltpu.VMEM((1, S), F32), pltpu.VMEM((1, S), F32),
                        pltpu.VMEM((T, S), F32), pltpu.VMEM((T, S), F32)]
        + [pltpu.VMEM((C, S), BF16)] * 4 + bg_scratch,
        compiler_params=_params(_semantics(bg, ("parallel", "arbitrary"))),
    )(z, bt_re, bt_im, ct_re, ct_im, a_re, a_im, d, *bg_args)
    return _results(res, 3, bg)


def ssm_bwd(z, y0, dd_direct, dd_mm, s_re, s_im, bt_re, bt_im, ct_re, ct_im, a_re, a_im, d, *, name, bg=None):
    L = z.shape[0]
    NK, _, S = a_re.shape
    H = bt_re.shape[0]
    C = S // SSM_STATE * SSM_GROUP
    T = _ssm_chunk(L)
    nchunk = L // T
    tail = T // 8
    grid = (NK, nchunk)
    bg_in_specs, bg_args, bg_out_specs, bg_out_shapes, bg_scratch, split = _carrier(bg, 15, 8, 12, grid)
    nt_dot = lambda p, q: lax.dot_general(p, q, (((1,), (1,)), ((), ())), preferred_element_type=F32)

    def body(*refs):
        ((u_ref, y_ref, d1_ref, d2_ref, sr_ref, si_ref, pr_ref, pi_ref,
          btr_ref, bti_ref, ctr_ref, cti_ref, ar_ref, ai_ref, d_ref),
         (du_ref, dbr_ref, dbi_ref, dcr_ref, dci_ref, dar_ref, dai_ref, dd_ref),
         (car_re, car_im, lam_re, lam_im, b_re, b_im, c_re, c_im, acc_br, acc_bi, acc_cr, acc_ci),
         run_background) = split(refs)
        run_background()
        i = pl.program_id(1)
        chunk = nchunk - 1 - i
        ar, ai = ar_ref[...], ai_ref[...]
        row = lax.broadcasted_iota(jnp.int32, (T, S), 0)

        @pl.when(i == 0)
        def _():
            car_re[...] = jnp.zeros_like(car_re)
            car_im[...] = jnp.zeros_like(car_im)
            b_re[...] = _expand_groups(btr_ref[...])
            b_im[...] = _expand_groups(bti_ref[...])
            c_re[...] = _expand_groups(ctr_ref[...])
            c_im[...] = _expand_groups(cti_ref[...])

        u = u_ref[...]
        dy = (d1_ref[...] + d2_ref[...]) * _gelu_grad(y_ref[...])
        dyb = dy.astype(BF16)
        lam_re[...] = jnp.dot(dyb, c_re[...], preferred_element_type=F32)
        lam_im[...] = -jnp.dot(dyb, c_im[...], preferred_element_type=F32)
        car_re[...], car_im[...] = _scan_rows(lam_re, lam_im, lam_re, lam_im, ar, -ai, car_re[...], car_im[...], True)
        lr, li = lam_re[...], lam_im[...]

        s_re, s_im = sr_ref[...], si_ref[...]
        has_prev = (chunk > 0).astype(F32)
        prev_re = pr_ref[7:8, :] * has_prev
        prev_im = pi_ref[7:8, :] * has_prev
        sp_re = jnp.where(row == 0, prev_re, pltpu.roll(s_re, 1, 0))
        sp_im = jnp.where(row == 0, prev_im, pltpu.roll(s_im, 1, 0))
        p_ar = _rowsum(lr * sp_re + li * sp_im)
        p_ai = _rowsum(li * sp_re - lr * sp_im)

        lrb, lib, ub = lr.astype(BF16), li.astype(BF16), u.astype(BF16)
        du = dy * d_ref[...] + nt_dot(lrb, b_re[...]) + nt_dot(lib, b_im[...])
        du_ref[...] = du.astype(BF16)
        tdot = lambda p, q: lax.dot_general(p, q, (((0,), (0,)), ((), ())), preferred_element_type=F32)
        p_br, p_bi = tdot(ub, lrb), tdot(ub, lib)
        p_cr, p_ci = tdot(dyb, s_re.astype(BF16)), -tdot(dyb, s_im.astype(BF16))
        p_dd = _rowsum(dy * u)

        @pl.when(i == 0)
        def _():
            dar_ref[...] = p_ar
            dai_ref[...] = p_ai
            acc_br[...] = p_br
            acc_bi[...] = p_bi
            acc_cr[...] = p_cr
            acc_ci[...] = p_ci
            dd_ref[...] = p_dd

        @pl.when(i > 0)
        def _():
            dar_ref[...] += p_ar
            dai_ref[...] += p_ai
            acc_br[...] += p_br
            acc_bi[...] += p_bi
            acc_cr[...] += p_cr
            acc_ci[...] += p_ci
            dd_ref[...] += p_dd

        @pl.when(i == nchunk - 1)
        def _():
            dbr_ref[...] = _collapse_groups(acc_br[...])
            dbi_ref[...] = _collapse_groups(acc_bi[...])
            dcr_ref[...] = _collapse_groups(acc_cr[...])
            dci_ref[...] = _collapse_groups(acc_ci[...])

    rev = lambda k, i: (nchunk - 1 - i, k)
    prev = lambda k, i: (jnp.maximum((nchunk - 1 - i) * tail - 1, 0), k)
    kspec = lambda shape: pl.BlockSpec((None,) + shape, lambda k, i: (k, 0, 0))
    compact = pl.BlockSpec((H, S), lambda k, i: (0, k))
    compact_shape = jax.ShapeDtypeStruct((H, NK * S), F32)
    res = pl.pallas_call(
        body, name=name, grid=grid,
        in_specs=[pl.BlockSpec((T, C), rev), pl.BlockSpec((T, C), rev), pl.BlockSpec((T, C), rev),
                  pl.BlockSpec((T, C), rev), pl.BlockSpec((T, S), rev), pl.BlockSpec((T, S), rev),
                  pl.BlockSpec((8, S), prev), pl.BlockSpec((8, S), prev),
                  compact, compact, compact, compact,
                  kspec((1, S)), kspec((1, S)), kspec((1, C))] + bg_in_specs,
        out_specs=[pl.BlockSpec((T, C), rev), compact, compact, compact, compact,
                   kspec((1, S)), kspec((1, S)), kspec((1, C))] + bg_out_specs,
        out_shape=[jax.ShapeDtypeStruct((L, NK * C), BF16),
                   compact_shape, compact_shape, compact_shape, compact_shape,
                   jax.ShapeDtypeStruct((NK, 1, S), F32), jax.ShapeDtypeStruct((NK, 1, S), F32),
                   jax.ShapeDtypeStruct((NK, 1, C), F32)] + bg_out_shapes,
        scratch_shapes=[pltpu.VMEM((1, S), F32), pltpu.VMEM((1, S), F32),
                        pltpu.VMEM((T, S), F32), pltpu.VMEM((T, S), F32)]
        + [pltpu.VMEM((C, S), BF16)] * 4 + [pltpu.VMEM((C, S), F32)] * 4 + bg_scratch,
        compiler_params=_params(_semantics(bg, ("parallel", "arbitrary"))),
    )(z, y0, dd_direct, dd_mm, s_re, s_im, s_re, s_im, bt_re, bt_im, ct_re, ct_im, a_re, a_im, d, *bg_args)
    return _results(res, 8, bg)


def _position():
    return lax.axis_index("x"), lax.axis_index("y"), lax.axis_index("c")


def _other_chips(x, y):
    return [(1 - x, y), (x, 1 - y), (1 - x, 1 - y)]


def _gather_phases(n, rows=None):
    def parts(ins, outs, sems):
        send_sems, recv_sems, local_sems = sems
        x, y, c = _position()
        me, sibling = (x, y, c), (x, y, 1 - c)
        chips = _other_chips(x, y)

        def block(a, pos):
            index = 4 * pos[0] + 2 * pos[1] + pos[2]
            if rows is not None:
                return outs[a].at[index, pl.ds(*rows)]
            return outs[a].at[pl.ds(index, 1)] if _is_row(ins[a]) else outs[a].at[index]

        def copy(a, k, pos, to, src=None):
            return pltpu.make_async_remote_copy(
                src_ref=block(a, pos) if src is None else src, dst_ref=block(a, pos),
                send_sem=send_sems.at[7 * a + k], recv_sem=recv_sems.at[7 * a + k],
                device_id=to, device_id_type=MESH)

        shard = [ins[a] if rows is None else ins[a].at[pl.ds(*rows)] for a in range(n)]
        mine = [pltpu.make_async_copy(shard[a], block(a, me), local_sems.at[a]) for a in range(n)]
        first = []
        for a in range(n):
            first.append(copy(a, 0, me, sibling, src=shard[a]))
            first += [copy(a, 1 + j, me, (*chip, c), src=shard[a]) for j, chip in enumerate(chips)]
        passed = [copy(a, 4 + j, (*chip, c), sibling) for a in range(n) for j, chip in enumerate(chips)]
        arrived = [copy(a, 1 + j, (*chip, c), me) for a in range(n) for j, chip in enumerate(chips)]
        from_sibling = []
        for a in range(n):
            from_sibling.append(copy(a, 0, sibling, me))
            from_sibling += [copy(a, 4 + j, (*chip, 1 - c), me) for j, chip in enumerate(chips)]
        return mine, first, passed, arrived, from_sibling

    def send(ins, outs, sems):
        mine, first, _, _, _ = parts(ins, outs, sems)
        for cp in mine + first:
            cp.start()

    def forward(ins, outs, sems):
        _, _, passed, arrived, _ = parts(ins, outs, sems)
        for got, fwd in zip(arrived, passed):
            got.wait_recv()
            fwd.start()

    def finish(ins, outs, sems):
        mine, first, passed, _, from_sibling = parts(ins, outs, sems)
        for cp in from_sibling:
            cp.wait_recv()
        for cp in first + passed:
            cp.wait_send()
        for cp in mine:
            cp.wait()

    return [(0.0, send), (GATHER_FORWARD_AT, forward), (1.0, finish)]


def _is_row(a):
    return len(a.shape) == 2 and a.shape[0] == 1


def _gather_shapes(shards):
    n = len(shards)
    return ([jax.ShapeDtypeStruct((N_DEV,) + (s.shape[1:] if _is_row(s) else s.shape), s.dtype) for s in shards],
            [pltpu.SemaphoreType.DMA((7 * n,)), pltpu.SemaphoreType.DMA((7 * n,)), pltpu.SemaphoreType.DMA((n,))])


def gather_background(shards, rows=None, into=None):
    out_shapes, scratch = _gather_shapes(shards)
    bg = Background(list(shards) + list(into or []), out_shapes, scratch, _gather_phases(len(shards), rows))
    bg.aliases = {len(shards) + k: k for k in range(len(into or []))}
    return bg


def all_gather_blocks(shards, *, name):
    n = len(shards)
    out_shapes, scratch = _gather_shapes(shards)

    def body(*refs):
        for _, phase in _gather_phases(n):
            phase(refs[:n], refs[n:2 * n], refs[2 * n:])

    return pl.pallas_call(
        body, name=name, in_specs=[ANY] * n, out_specs=[ANY] * n, out_shape=out_shapes, scratch_shapes=scratch,
    )(*shards)


def sibling_exchange(grads, *, name):
    n = len(grads)
    bg = sibling_exchange_background(grads)

    def body(*refs):
        for _, phase in bg.phases:
            phase(refs[:n], refs[n:2 * n], refs[2 * n:])

    return pl.pallas_call(
        body, name=name, in_specs=[ANY] * n, out_specs=[ANY] * n, out_shape=bg.out_shapes, scratch_shapes=bg.scratch,
    )(*grads)


def sibling_exchange_background(grads):
    n = len(grads)

    def copies(ins, outs, sems):
        x, y, c = _position()
        return [pltpu.make_async_remote_copy(
            src_ref=ins[a].at[2 * q + 1 - c], dst_ref=outs[a].at[q],
            send_sem=sems[0].at[4 * a + q], recv_sem=sems[1].at[4 * a + q],
            device_id=(x, y, 1 - c), device_id_type=MESH)
            for a in range(n) for q in range(4)]

    def send(ins, outs, sems):
        for cp in copies(ins, outs, sems):
            cp.start()

    def finish(ins, outs, sems):
        for cp in copies(ins, outs, sems):
            cp.wait()

    return Background(grads, [jax.ShapeDtypeStruct((4,) + g.shape[1:], g.dtype) for g in grads],
                      [pltpu.SemaphoreType.DMA((4 * n,)), pltpu.SemaphoreType.DMA((4 * n,))],
                      [(0.0, send), (1.0, finish)])


def _chip_exchange_phases(n):
    def copies(ins, outs, sems):
        x, y, c = _position()
        return [pltpu.make_async_remote_copy(
            src_ref=ins[a].at[2 * chip[0] + chip[1]], dst_ref=outs[a].at[j],
            send_sem=sems[0].at[3 * a + j], recv_sem=sems[1].at[3 * a + j],
            device_id=(*chip, c), device_id_type=MESH)
            for a in range(n) for j, chip in enumerate(_other_chips(x, y))]

    def send(ins, outs, sems):
        for cp in copies(ins, outs, sems):
            cp.start()

    def finish(ins, outs, sems):
        for cp in copies(ins, outs, sems):
            cp.wait()

    return [(0.0, send), (1.0, finish)]


def chip_exchange_background(parts):
    n = len(parts)
    return Background(parts, [jax.ShapeDtypeStruct((3,) + p.shape[1:], p.dtype) for p in parts],
                      [pltpu.SemaphoreType.DMA((3 * n,)), pltpu.SemaphoreType.DMA((3 * n,))],
                      _chip_exchange_phases(n))


def add_pairs(grads, theirs, core, *, name, tm=512):
    _, R, C = theirs.shape
    tm = _tile(R, tm, 16)

    def body(core_ref, a_ref, b_ref, o_ref):
        o_ref[...] = (a_ref[...].astype(F32) + b_ref[...].astype(F32)).astype(o_ref.dtype)

    spec = pl.BlockSpec((None, tm, C), lambda q, i, core_ref: (q, i, 0))
    return pl.pallas_call(
        body, name=name,
        grid_spec=pltpu.PrefetchScalarGridSpec(
            num_scalar_prefetch=1, grid=(4, R // tm),
            in_specs=[pl.BlockSpec((None, tm, C), lambda q, i, core_ref: (2 * q + core_ref[0], i, 0)), spec],
            out_specs=spec),
        out_shape=jax.ShapeDtypeStruct(theirs.shape, BF16),
        compiler_params=_params(("parallel", "parallel")),
    )(core, grads, theirs)


def _adamw(w, g, m, v):
    m = ADAM_B1 * m + (1.0 - ADAM_B1) * g
    v = ADAM_B2 * v + (1.0 - ADAM_B2) * (g * g)
    m_hat = m / (1.0 - ADAM_B1 ** ADAM_STEP)
    v_hat = v / (1.0 - ADAM_B2 ** ADAM_STEP)
    delta = -ADAM_LR * (m_hat / (jnp.sqrt(v_hat) + ADAM_EPS) + ADAM_WD * w)
    return delta, m, v


def adamw_sharded(w, m, v, grads, theirs, others, where, *, name, tm=256):
    R, C = w.shape
    tm = _tile(R, tm, 16)

    def body(where_ref, w_ref, m_ref, v_ref, a_ref, b_ref, o_ref, g_ref, d_ref, nm_ref, nv_ref):
        g = a_ref[...].astype(F32) + b_ref[...].astype(F32)
        for j in range(3):
            g = g + o_ref[j].astype(F32)
        g_ref[...] = g
        d_ref[...], nm_ref[...], nv_ref[...] = _adamw(w_ref[...], g, m_ref[...], v_ref[...])

    spec = pl.BlockSpec((tm, C), lambda i, where_ref: (i, 0))
    sd = jax.ShapeDtypeStruct((R, C), F32)
    return pl.pallas_call(
        body, name=name,
        grid_spec=pltpu.PrefetchScalarGridSpec(
            num_scalar_prefetch=1, grid=(R // tm,),
            in_specs=[spec, spec, spec,
                      pl.BlockSpec((None, tm, C), lambda i, where_ref: (where_ref[0], i, 0)),
                      pl.BlockSpec((None, tm, C), lambda i, where_ref: (where_ref[1], i, 0)),
                      pl.BlockSpec((3, tm, C), lambda i, where_ref: (0, i, 0))],
            out_specs=[spec, spec, spec, spec]),
        out_shape=[sd, sd, sd, sd],
        compiler_params=_params(("parallel",)),
    )(where, w, m, v, grads, theirs, others)


def sum_gathered(gathered, *, name):
    n = len(gathered)

    def body(*refs):
        for ga_ref, o_ref in zip(refs[:n], refs[n:]):
            rows = len(ga_ref.shape) == 2
            total = ga_ref[0:1] if rows else ga_ref[0]
            for dev in range(1, N_DEV):
                total = total + (ga_ref[dev:dev + 1] if rows else ga_ref[dev])
            o_ref[...] = total

    shapes = [jax.ShapeDtypeStruct((1,) + g.shape[1:] if g.ndim == 2 else g.shape[1:], F32) for g in gathered]
    return pl.pallas_call(body, name=name, out_shape=shapes, compiler_params=_params())(*gathered)


def adamw_replicated(ws, ms, vs, gs, *, name):
    n = len(ws)

    def body(*refs):
        w_refs, m_refs, v_refs, g_refs = refs[:n], refs[n:2 * n], refs[2 * n:3 * n], refs[3 * n:4 * n]
        outs = refs[4 * n:]
        for k in range(n):
            outs[k][...], outs[n + k][...], outs[2 * n + k][...] = _adamw(
                w_refs[k][...], g_refs[k][...], m_refs[k][...], v_refs[k][...])

    shapes = [jax.ShapeDtypeStruct(t.shape, F32) for t in ws]
    res = pl.pallas_call(body, name=name, out_shape=shapes * 3, compiler_params=_params())(*ws, *ms, *vs, *gs)
    return res[:n], res[n:2 * n], res[2 * n:]


SHARDED = ("w_in", "ssm_glu_w", "w_out", "w_ffn_in", "w_ffn_out", "w_ple_gate", "w_ple_proj")
SMALL_LAST = ("norm_mix_g",)
SMALL_WIDE = ("ssm_b_re", "ssm_b_im", "ssm_c_re", "ssm_c_im")
SMALL = ("ssm_lambda_re", "ssm_lambda_im", "ssm_log_step", "ssm_b_re", "ssm_b_im", "ssm_c_re",
         "ssm_c_im", "ssm_d", "ssm_glu_b", "sgu_ln_g", "sgu_ln_b", "sgu_w", "sgu_b", "out_norm_ssm_g",
         "out_norm_sgu_g", "norm_ffn_g", "norm_ple_g", "b_ple_gate", "final_norm_g")
WEIGHTS = ("norm_mix_g", "w_in", "ssm_lambda_re", "ssm_lambda_im", "ssm_log_step", "ssm_b_re", "ssm_b_im",
           "ssm_c_re", "ssm_c_im", "ssm_d", "ssm_glu_w", "ssm_glu_b", "sgu_ln_g", "sgu_ln_b", "sgu_w", "sgu_b",
           "out_norm_ssm_g", "out_norm_sgu_g", "w_out", "norm_ffn_g", "w_ffn_in", "w_ffn_out", "norm_ple_g",
           "w_ple_gate", "b_ple_gate", "w_ple_proj", "final_norm_g")


def _step(x, p, loss_target, w, m, v):
    L, D = x.shape[1], x.shape[2]
    x2d, p2d, tgt = x.reshape(L, D), p.reshape(L, -1), loss_target.reshape(L, D)
    d_ssm = w["ssm_glu_w"].shape[2]
    d_sgu = w["sgu_ln_g"].shape[1]
    G, P, H = w["ssm_b_re"].shape[1:]
    SG = min(SSM_SUPER, G)
    NK = G // SG
    row = lambda a: a.reshape(1, -1)

    shard2d = {n: w[n].reshape(w[n].shape[1:]) for n in SHARDED}
    shard_bf = {n: shard2d[n].astype(BF16) for n in SHARDED}
    (w_ple_blk,) = all_gather_blocks([shard_bf["w_ple_proj"]], name="gather_w_ple")
    bf = lambda t: t.astype(BF16)
    pp, (w_in_blk,) = mm_nn(bf(p2d), w_ple_blk, name="ple_proj", out_dtype=F32, tm=512, tn=512, tk=2048,
                            bg=gather_background([shard_bf["w_in"]]))
    w_in = jnp.transpose(w_in_blk, (1, 0, 2)).reshape(D, -1)
    F = shard2d["w_ffn_in"].shape[1] * 4

    lam_re, lam_im, log_step = w["ssm_lambda_re"][0], w["ssm_lambda_im"][0], w["ssm_log_step"][0].reshape(G, 1)
    a_re, a_im, q_re, q_im = disc_lambda_fwd(lam_re, lam_im, log_step, name="s5_discretise_lambda")
    bt_re = w["ssm_b_re"][0].transpose(2, 0, 1).reshape(H, G * P)
    bt_im = w["ssm_b_im"][0].transpose(2, 0, 1).reshape(H, G * P)
    bbar_re, bbar_im = disc_b_fwd(row(q_re), row(q_im), bt_re, bt_im, name="s5_discretise_b")
    ct_re = w["ssm_c_re"][0].transpose(1, 0, 2).reshape(H, G * P)
    ct_im = w["ssm_c_im"][0].transpose(1, 0, 2).reshape(H, G * P)
    a_re_k, a_im_k = a_re.reshape(NK, 1, SG * P), a_im.reshape(NK, 1, SG * P)
    d_k = w["ssm_d"][0].reshape(NK, 1, SG * H)

    h1 = norm_fwd(x2d, w["norm_mix_g"], name="norm_mix")
    z, (w_glu, w_out) = mm_nn(h1, w_in, name="in_proj", out_dtype=F32, tm=1024, tn=1024, tk=2048,
                              bg=gather_background([shard_bf["ssm_glu_w"], shard_bf["w_out"]]))
    w_glu, w_out = w_glu.reshape(d_ssm, d_ssm), w_out.reshape(D, D)
    first_rows = (D * 11 // 16) // 16 * 16
    s5_mats = (bbar_re, bbar_im, ct_re, ct_im, a_re_k, a_im_k, d_k)
    (y0, s_re, s_im), (w_ffn_in_part,) = ssm_fwd(
        z, *s5_mats, name="s5_scan", bg=gather_background([shard_bf["w_ffn_in"]], rows=(0, first_rows)))
    ya1 = glu_pre(y0, name="s5_gelu")
    t_glu = mm_nn(ya1, w_glu, name="s5_glu_proj", out_dtype=F32, tm=512, tn=512, tk=2048)
    n_a = glu_post(y0, t_glu, w["ssm_glu_b"], w["out_norm_ssm_g"], name="s5_glu_norm")
    b_s3 = w["sgu_b"][0][:, :, None]
    n_b = sgu_fwd(z, w["sgu_ln_g"], w["sgu_ln_b"], w["sgu_w"][0], b_s3, w["out_norm_sgu_g"], name="sgu", d_sgu=d_sgu)
    ycat = jnp.concatenate([n_a, n_b], axis=1)
    x1, (w_ffn_in_blk,) = mm_nn(
        ycat, w_out, name="out_proj", out_dtype=F32, tm=512, tn=512, tk=2048, residual=x2d,
        bg=gather_background([shard_bf["w_ffn_in"]], rows=(first_rows, D - first_rows), into=[w_ffn_in_part]))
    h2 = norm_fwd(x1, w["norm_ffn_g"], name="norm_ffn")
    (act, gate_ff, up_ff), (w_ffn_out, w_gate) = ffn_in_swiglu(
        h2, w_ffn_in_blk, name="ffn_in_swiglu", tm=512,
        bg=gather_background([shard_bf["w_ffn_out"], shard_bf["w_ple_gate"]]))
    w_ffn_out, w_gate = w_ffn_out.reshape(F, D), w_gate.reshape(D, D)
    x2 = mm_nn(act, w_ffn_out, name="ffn_out", out_dtype=F32, tm=512, tn=1024, tk=F, residual=x1)
    h3 = norm_fwd(x2, w["norm_ple_g"], name="norm_ple")
    gpre = mm_nn(h3, w_gate, name="ple_gate", out_dtype=F32, tm=512, tn=1024, tk=2048)

    dx3, dpre, dpp, loss_part, d_final_g, d_b_gate = head_and_loss(
        x2, gpre, w["b_ple_gate"], pp, row(w["final_norm_g"]), tgt, name="head_and_loss")
    x_pos, y_pos, c_pos = _position()
    where = jnp.stack([4 * x_pos + 2 * y_pos + c_pos, 2 * x_pos + y_pos]).astype(jnp.int32)
    core = jnp.reshape(c_pos, (1,)).astype(jnp.int32)
    own, others = {}, {}

    def blocks(named):
        g8 = {n: t.reshape((N_DEV,) + shard2d[n].shape) for n, t in named.items()}
        return g8, sibling_exchange_background(list(g8.values()))

    def chip_sums(g8, theirs):
        own.update(zip(g8, zip(g8.values(), theirs)))
        return [add_pairs(g, t, core, name="chip_sum_" + n) for (n, g), t in zip(g8.items(), theirs)]

    d_w_gate = mm_tn(h3, dpre, name="d_w_ple_gate", out_dtype=BF16, tm=L, tko=1024, tno=1024)
    d_w_ple = mm_tn(bf(p2d), dpp, name="d_w_ple_proj", out_dtype=BF16, tm=L, tko=1024, tno=1024, out_blocks=N_DEV)
    g8_ple, bg = blocks({"w_ple_gate": d_w_gate, "w_ple_proj": d_w_ple})
    dh3, theirs = mm_nt(dpre, w_gate, name="d_h_ple", out_dtype=F32, tm=512, tko=1024, tc=2048, bg=bg)
    bg = chip_exchange_background(chip_sums(g8_ple, theirs))
    dx2, dx2b, d_ple_g = norm_bwd(dh3, x2, w["norm_ple_g"], dx3, name="d_norm_ple", want_bf16=True)
    d_w_ffn_out, got = mm_tn(act, dx2b, name="d_w_ffn_out", out_dtype=BF16, tm=L, tko=1408, tno=512, bg=bg)
    others.update(zip(g8_ple, got))
    g8_fo, bg = blocks({"w_ffn_out": d_w_ffn_out})
    dgate, dup = ffn_out_bwd_swiglu(dx2b, w_ffn_out, gate_ff, up_ff, name="d_act_swiglu")
    half = N_DEV // 2
    d_w_ffn_in, theirs = mm_tn(h2, dgate, name="d_w_ffn_in_gate", out_dtype=BF16, tm=L, tko=512, tno=1408,
                               out_blocks=half, total_blocks=N_DEV, bg=bg)
    bg = chip_exchange_background(chip_sums(g8_fo, theirs))
    d_w_ffn_in, got = mm_tn(h2, dup, name="d_w_ffn_in_up", out_dtype=BF16, tm=L, tko=512, tno=1408,
                            out_blocks=half, block_offset=half, total_blocks=N_DEV, into=d_w_ffn_in, bg=bg)
    others.update(zip(g8_fo, got))
    g8_fi, bg = blocks({"w_ffn_in": d_w_ffn_in})
    dh2, theirs = mm_nt(dgate, w_ffn_in_blk, a2=dup, name="d_h_ffn", out_dtype=F32, tm=1024, tko=1024, tc=1408, bg=bg)
    late_parts = chip_sums(g8_fi, theirs)
    dx1, dx1b, d_ffn_g = norm_bwd(dh2, x1, w["norm_ffn_g"], dx2, name="d_norm_ffn", want_bf16=True)
    dycat = mm_nt(dx1b, w_out, name="d_ycat", out_dtype=F32, tm=512, tko=1024, tc=2048)
    d_w_out = mm_tn(ycat, dx1b, name="d_w_out", out_dtype=BF16, tm=L, tko=1024, tno=1024)
    g8_out, bg = blocks({"w_out": d_w_out})
    dzu, dzv, d_sgu_w, d_sgu_b, d_ln_g, d_ln_b, d_g_b = sgu_bwd(
        z, dycat, w["sgu_ln_g"], w["sgu_ln_b"], w["sgu_w"][0], b_s3, w["out_norm_sgu_g"], name="d_sgu", d_sgu=d_sgu)
    dt_glu, dd_direct, d_g_a, d_glu_b = glu_post_bwd(
        y0, t_glu, w["ssm_glu_b"], w["out_norm_ssm_g"], dycat, name="d_s5_glu_norm")
    d_w_glu, theirs = mm_tn(ya1, dt_glu, name="d_w_glu", out_dtype=BF16, tm=L, tko=1024, tno=1024, bg=bg)
    late_parts += chip_sums(g8_out, theirs)
    g8_glu, bg = blocks({"ssm_glu_w": d_w_glu})
    dd_mm, theirs = mm_nt(dt_glu, w_glu, name="d_s5_glu_proj", out_dtype=F32, tm=512, tko=1024, tc=2048, bg=bg)
    late_parts += chip_sums(g8_glu, theirs)
    (du, d_bbar_re, d_bbar_im, d_ct_re, d_ct_im, d_a_re, d_a_im, d_d), got = ssm_bwd(
        z, y0, dd_direct, dd_mm, s_re, s_im, *s5_mats, name="d_s5_scan", bg=chip_exchange_background(late_parts))
    others.update(zip(("w_ffn_in", "w_out", "ssm_glu_w"), got))
    dz = jnp.concatenate([du, dzu, dzv], axis=1)

    d_q_re, d_q_im, d_bt_re, d_bt_im = disc_b_bwd(row(q_re), row(q_im), bt_re, bt_im, d_bbar_re, d_bbar_im,
                                                  name="d_s5_discretise_b")
    d_lam_re, d_lam_im, d_log_step = disc_lambda_bwd(
        lam_re, lam_im, log_step,
        (d_a_re.reshape(G, P), d_a_im.reshape(G, P), d_q_re.reshape(G, P), d_q_im.reshape(G, P)),
        name="d_s5_discretise_lambda")
    small_grads = {
        "ssm_lambda_re": d_lam_re, "ssm_lambda_im": d_lam_im, "ssm_log_step": d_log_step,
        "ssm_b_re": d_bt_re, "ssm_b_im": d_bt_im, "ssm_c_re": d_ct_re, "ssm_c_im": d_ct_im,
        "ssm_d": d_d, "ssm_glu_b": d_glu_b, "sgu_ln_g": d_ln_g, "sgu_ln_b": d_ln_b,
        "sgu_w": d_sgu_w, "sgu_b": d_sgu_b, "out_norm_ssm_g": d_g_a, "out_norm_sgu_g": d_g_b,
        "norm_ffn_g": d_ffn_g, "norm_ple_g": d_ple_g, "b_ple_gate": d_b_gate, "final_norm_g": d_final_g,
    }

    d_w_in, got = mm_tn(h1, dz, name="d_w_in", out_dtype=BF16, tm=L, tko=1024, tno=1024, out_blocks=N_DEV,
                        bg=gather_background([loss_part] + [small_grads[n] for n in SMALL]))
    sums = sum_gathered(got, name="sum_small_grads")
    g8_in, _ = blocks({"w_in": d_w_in})
    theirs = sibling_exchange(list(g8_in.values()), name="grads_to_sibling_w_in")
    dh1, got = mm_nt(dz, w_in, name="d_h_mix", out_dtype=F32, tm=512, tko=1024, tc=3 * d_sgu,
                     bg=chip_exchange_background(chip_sums(g8_in, theirs)))
    others.update(zip(g8_in, got))
    grad_x, d_mix_g = norm_bwd(dh1, x2d, w["norm_mix_g"], dx1, name="d_norm_mix", want_bf16=False)
    loss, small_sum = sums[0][0, 0], dict(zip(SMALL, sums[1:]))
    (small_sum["norm_mix_g"],) = sum_gathered(all_gather_blocks([d_mix_g], name="gather_last_grad"),
                                              name="sum_last_grad")

    out = {}
    for n in SHARDED:
        res = adamw_sharded(shard2d[n], m[n].reshape(shard2d[n].shape), v[n].reshape(shard2d[n].shape),
                            own[n][0], own[n][1], others[n], where, name="adamw_" + n)
        out[n] = [r.reshape(w[n].shape) for r in res]

    def work_shape(n):
        s = w[n].shape
        return (1,) + s if len(s) == 1 else (s if len(s) == 2 else s[1:])

    for n in ("ssm_b_re", "ssm_b_im"):
        small_sum[n] = small_sum[n].reshape(H, G, P).transpose(1, 2, 0)
    for n in ("ssm_c_re", "ssm_c_im"):
        small_sum[n] = small_sum[n].reshape(H, G, P).transpose(1, 0, 2)

    def replicated(names_, name):
        gs = [small_sum[n].reshape(work_shape(n)) for n in names_]
        res = adamw_replicated(*[[t[n].reshape(work_shape(n)) for n in names_] for t in (w, m, v)], gs, name=name)
        for i, n in enumerate(names_):
            out[n] = [r.reshape(w[n].shape) for r in (gs[i], res[0][i], res[1][i], res[2][i])]

    replicated([n for n in SMALL + SMALL_LAST if n not in SMALL_WIDE], "adamw_small")
    replicated(list(SMALL_WIDE), "adamw_s5_b_c")

    grads = [out[n][0] for n in WEIGHTS]
    deltas = [out[n][1] for n in WEIGHTS]
    new_m = [out[n][2] for n in WEIGHTS]
    new_v = [out[n][3] for n in WEIGHTS]
    return (loss, grad_x.reshape(x.shape), *grads, *deltas, *new_m, *new_v)


def kernel(x, p, norm_mix_g, w_in, ssm_lambda_re, ssm_lambda_im, ssm_log_step, ssm_b_re, ssm_b_im, ssm_c_re, ssm_c_im, ssm_d, ssm_glu_w, ssm_glu_b, sgu_ln_g, sgu_ln_b, sgu_w, sgu_b, out_norm_ssm_g, out_norm_sgu_g, w_out, norm_ffn_g, w_ffn_in, w_ffn_out, norm_ple_g, w_ple_gate, b_ple_gate, w_ple_proj, final_norm_g, loss_target, m_norm_mix_g, m_w_in, m_ssm_lambda_re, m_ssm_lambda_im, m_ssm_log_step, m_ssm_b_re, m_ssm_b_im, m_ssm_c_re, m_ssm_c_im, m_ssm_d, m_ssm_glu_w, m_ssm_glu_b, m_sgu_ln_g, m_sgu_ln_b, m_sgu_w, m_sgu_b, m_out_norm_ssm_g, m_out_norm_sgu_g, m_w_out, m_norm_ffn_g, m_w_ffn_in, m_w_ffn_out, m_norm_ple_g, m_w_ple_gate, m_b_ple_gate, m_w_ple_proj, m_final_norm_g, v_norm_mix_g, v_w_in, v_ssm_lambda_re, v_ssm_lambda_im, v_ssm_log_step, v_ssm_b_re, v_ssm_b_im, v_ssm_c_re, v_ssm_c_im, v_ssm_d, v_ssm_glu_w, v_ssm_glu_b, v_sgu_ln_g, v_sgu_ln_b, v_sgu_w, v_sgu_b, v_out_norm_ssm_g, v_out_norm_sgu_g, v_w_out, v_norm_ffn_g, v_w_ffn_in, v_w_ffn_out, v_norm_ple_g, v_w_ple_gate, v_b_ple_gate, v_w_ple_proj, v_final_norm_g):
    given = dict(locals())
    w = {n: given[n] for n in WEIGHTS}
    m = {n: given["m_" + n] for n in WEIGHTS}
    v = {n: given["v_" + n] for n in WEIGHTS}
    return _step(x, p, loss_target, w, m, v)
```

```python
import functools
import math

import jax
import jax.numpy as jnp
from jax import lax
from jax.experimental import pallas as pl
from jax.experimental.pallas import tpu as pltpu

F32 = jnp.float32
BF16 = jnp.bfloat16
MESH = pl.DeviceIdType.MESH
ANY = pl.BlockSpec(memory_space=pl.ANY)

N_DEV = 8
EPS = 1e-6
LAMBDA_RE_MAX = -1e-4
SSM_GROUP = 16
SSM_STATE = 64
SSM_SUPER = 16
SGU_CHUNK = 128
ADAM_LR, ADAM_B1, ADAM_B2, ADAM_EPS, ADAM_WD, ADAM_STEP = 0.001, 0.9, 0.999, 1e-08, 0.01, 10
VMEM_LIMIT = 52 * 1024 * 1024
LANE = 128
GATHER_FORWARD_AT = 0.85

_GELU_C = math.sqrt(2.0 / math.pi)


def _params(sem=None):
    return pltpu.CompilerParams(dimension_semantics=sem, vmem_limit_bytes=VMEM_LIMIT)


def _tile(dim, pref, unit=LANE):
    if dim <= pref:
        return dim
    t = (pref // unit) * unit
    while t >= unit:
        if dim % t == 0:
            return t
        t -= unit
    return dim


def _gelu(x):
    return 0.5 * x * (1.0 + jnp.tanh(_GELU_C * (x + 0.044715 * x * x * x)))


def _gelu_grad(x):
    t = jnp.tanh(_GELU_C * (x + 0.044715 * x * x * x))
    return 0.5 * (1.0 + t) + 0.5 * x * (1.0 - t * t) * (_GELU_C * (1.0 + 3.0 * 0.044715 * x * x))


def _gelu_and_grad(x):
    t = jnp.tanh(_GELU_C * (x + 0.044715 * x * x * x))
    return (0.5 * x * (1.0 + t),
            0.5 * (1.0 + t) + 0.5 * x * (1.0 - t * t) * (_GELU_C * (1.0 + 3.0 * 0.044715 * x * x)))


def _rms(x):
    return lax.rsqrt(jnp.mean(x * x, axis=-1, keepdims=True) + EPS)


def _rmsnorm_bwd(dy, x, r, g):
    dyg = dy * g
    return r * dyg - x * (r * r * r) * jnp.mean(dyg * x, axis=-1, keepdims=True)


def _rowsum(v):
    return jnp.sum(v, axis=0, keepdims=True)


class Background:
    def __init__(self, inputs, out_shapes, scratch, phases):
        self.inputs, self.out_shapes, self.scratch, self.phases = list(inputs), list(out_shapes), list(scratch), phases
        self.aliases = {}

    def emit(self, step, nsteps, ins, outs, scratch):
        for place, phase in self.phases:
            at = min(int(place * nsteps), nsteps - 1)

            @pl.when(step == at)
            def _():
                phase(ins, outs, scratch)


def _carrier(bg, n_in, n_out, n_scratch, grid):
    nbi = len(bg.inputs) if bg else 0
    nbo = len(bg.out_shapes) if bg else 0
    nsteps = math.prod(grid)

    def split(refs):
        ins = refs[:n_in]
        bg_ins = refs[n_in:n_in + nbi]
        outs = refs[n_in + nbi:n_in + nbi + n_out]
        bg_outs = refs[n_in + nbi + n_out:n_in + nbi + n_out + nbo]
        rest = refs[n_in + nbi + n_out + nbo:]
        scratch, bg_scratch = rest[:n_scratch], rest[n_scratch:]

        def run_background():
            if bg is None:
                return
            step = pl.program_id(0)
            for axis in range(1, len(grid)):
                step = step * grid[axis] + pl.program_id(axis)
            bg.emit(step, nsteps, bg_ins, bg_outs, bg_scratch)

        return ins, outs, scratch, run_background

    if bg is None:
        return [], [], [], [], [], split
    return [ANY] * nbi, list(bg.inputs), [ANY] * nbo, list(bg.out_shapes), list(bg.scratch), split


def _semantics(bg, sem):
    return tuple("arbitrary" for _ in sem) if bg is not None else sem


def _results(res, n_out, bg):
    res = list(res) if isinstance(res, (list, tuple)) else [res]
    own = res[0] if n_out == 1 else res[:n_out]
    return (own, res[n_out:]) if bg is not None else own


def mm_nn(a, b, *, name, out_dtype, tm, tn, tk, residual=None, bg=None):
    M, K = a.shape
    blocked = b.ndim == 3
    if blocked:
        nb, _, Nb = b.shape
        N = nb * Nb
        tn = _tile(Nb, tn)
        per = Nb // tn
    else:
        N = b.shape[1]
        tn = _tile(N, tn)
    tm, tk = _tile(M, tm, 8), _tile(K, tk)
    nj, ni, nk = N // tn, M // tm, K // tk
    has_res = residual is not None
    grid = (nj, ni, nk)
    bg_in_specs, bg_args, bg_out_specs, bg_out_shapes, bg_scratch, split = _carrier(
        bg, 3 if has_res else 2, 1, 0 if nk == 1 else 1, grid)

    def body(*refs):
        ins, (o_ref,), scratch, run_background = split(refs)
        run_background()
        a_ref, b_ref = ins[0], ins[1]
        r_ref = ins[2] if has_res else None

        def finish(acc):
            if has_res:
                acc = acc + r_ref[...]
            o_ref[...] = acc.astype(o_ref.dtype)

        part = jnp.dot(a_ref[...], b_ref[...], preferred_element_type=F32)
        if nk == 1:
            finish(part)
        else:
            acc_ref = scratch[0]
            k = pl.program_id(2)

            @pl.when(k == 0)
            def _():
                acc_ref[...] = part

            @pl.when(k > 0)
            def _():
                acc_ref[...] += part

            @pl.when(k == nk - 1)
            def _():
                finish(acc_ref[...])

    if blocked:
        b_spec = pl.BlockSpec((None, tk, tn), lambda j, i, k: (j // per, k, j % per))
    else:
        b_spec = pl.BlockSpec((tk, tn), lambda j, i, k: (k, j))
    in_specs = [pl.BlockSpec((tm, tk), lambda j, i, k: (i, k)), b_spec]
    args = [a, b]
    if has_res:
        in_specs.append(pl.BlockSpec((tm, tn), lambda j, i, k: (i, j)))
        args.append(residual)
    res = pl.pallas_call(
        body, name=name, grid=grid,
        in_specs=in_specs + bg_in_specs,
        out_specs=[pl.BlockSpec((tm, tn), lambda j, i, k: (i, j))] + bg_out_specs,
        out_shape=[jax.ShapeDtypeStruct((M, N), out_dtype)] + bg_out_shapes,
        input_output_aliases={len(args) + k: 1 + o for k, o in (bg.aliases if bg else {}).items()},
        scratch_shapes=([] if nk == 1 else [pltpu.VMEM((tm, tn), F32)]) + bg_scratch,
        compiler_params=_params(_semantics(bg, ("parallel", "parallel", "arbitrary"))),
    )(*args, *bg_args)
    return _results(res, 1, bg)


def mm_nt(a, w, *, name, out_dtype, tm, tko, tc, a2=None, bg=None):
    M, N = a.shape
    if a2 is not None:
        N = 2 * N
    blocked = w.ndim == 3
    if blocked:
        nb, Ko, Nb = w.shape
        tc = _tile(Nb, tc)
        per = Nb // tc
    else:
        Ko = w.shape[0]
        tc = _tile(N, tc)
    tm, tko = _tile(M, tm, 8), _tile(Ko, tko)
    njo, ni, nc = Ko // tko, M // tm, N // tc
    grid = (njo, ni, nc)
    half = nc // 2
    bg_in_specs, bg_args, bg_out_specs, bg_out_shapes, bg_scratch, split = _carrier(
        bg, 2 if a2 is None else 3, 1, 0 if nc == 1 else 1, grid)

    def body(*refs):
        ins, (o_ref,), scratch, run_background = split(refs)
        run_background()
        a_val = ins[0][...]
        if a2 is not None:
            a_val = jnp.where(pl.program_id(2) < half, a_val, ins[1][...])
        part = lax.dot_general(a_val, ins[-1][...], (((1,), (1,)), ((), ())),
                               preferred_element_type=F32)
        if nc == 1:
            o_ref[...] = part.astype(o_ref.dtype)
        else:
            acc_ref = scratch[0]
            c = pl.program_id(2)

            @pl.when(c == 0)
            def _():
                acc_ref[...] = part

            @pl.when(c > 0)
            def _():
                acc_ref[...] += part

            @pl.when(c == nc - 1)
            def _():
                o_ref[...] = acc_ref[...].astype(o_ref.dtype)

    if blocked:
        w_spec = pl.BlockSpec((None, tko, tc), lambda j, i, c: (c // per, j, c % per))
    else:
        w_spec = pl.BlockSpec((tko, tc), lambda j, i, c: (j, c))
    if a2 is None:
        a_specs, a_args = [pl.BlockSpec((tm, tc), lambda j, i, c: (i, c))], [a]
    else:
        a_specs = [pl.BlockSpec((tm, tc), lambda j, i, c: (i, jnp.minimum(c, half - 1))),
                   pl.BlockSpec((tm, tc), lambda j, i, c: (i, jnp.maximum(c - half, 0)))]
        a_args = [a, a2]
    res = pl.pallas_call(
        body, name=name, grid=grid,
        in_specs=a_specs + [w_spec] + bg_in_specs,
        out_specs=[pl.BlockSpec((tm, tko), lambda j, i, c: (i, j))] + bg_out_specs,
        out_shape=[jax.ShapeDtypeStruct((M, Ko), out_dtype)] + bg_out_shapes,
        scratch_shapes=([] if nc == 1 else [pltpu.VMEM((tm, tko), F32)]) + bg_scratch,
        compiler_params=_params(_semantics(bg, ("parallel", "parallel", "arbitrary"))),
    )(*a_args, w, *bg_args)
    return _results(res, 1, bg)


def mm_tn(a, g, *, name, out_dtype, tm, tko, tno, out_blocks=None, block_offset=0, total_blocks=None, into=None,
          bg=None):
    M, K = a.shape
    N = g.shape[1]
    if out_blocks:
        Nb = N // out_blocks
        tno = _tile(Nb, tno)
        per = Nb // tno
    else:
        tno = _tile(N, tno)
    tm, tko = _tile(M, tm), _tile(K, tko)
    njo, njn, nm = K // tko, N // tno, M // tm
    grid = (njo, njn, nm)
    bg_in_specs, bg_args, bg_out_specs, bg_out_shapes, bg_scratch, split = _carrier(
        bg, 2 if into is None else 3, 1, 0 if nm == 1 else 1, grid)

    def body(*refs):
        ins, (o_ref,), scratch, run_background = split(refs)
        a_ref, g_ref = ins[0], ins[1]
        run_background()
        part = lax.dot_general(a_ref[...], g_ref[...], (((0,), (0,)), ((), ())),
                               preferred_element_type=F32)
        if nm == 1:
            o_ref[...] = part.astype(o_ref.dtype)
        else:
            acc_ref = scratch[0]
            m = pl.program_id(2)

            @pl.when(m == 0)
            def _():
                acc_ref[...] = part

            @pl.when(m > 0)
            def _():
                acc_ref[...] += part

            @pl.when(m == nm - 1)
            def _():
                o_ref[...] = acc_ref[...].astype(o_ref.dtype)

    if out_blocks:
        o_spec = pl.BlockSpec((None, tko, tno), lambda jo, jn, m: (jn // per + block_offset, jo, jn % per))
        o_shape = jax.ShapeDtypeStruct((total_blocks or out_blocks, K, Nb), out_dtype)
    else:
        o_spec = pl.BlockSpec((tko, tno), lambda jo, jn, m: (jo, jn))
        o_shape = jax.ShapeDtypeStruct((K, N), out_dtype)
    res = pl.pallas_call(
        body, name=name, grid=grid,
        in_specs=[pl.BlockSpec((tm, tko), lambda jo, jn, m: (m, jo)),
                  pl.BlockSpec((tm, tno), lambda jo, jn, m: (m, jn))] + ([] if into is None else [ANY]) + bg_in_specs,
        out_specs=[o_spec] + bg_out_specs, out_shape=[o_shape] + bg_out_shapes,
        scratch_shapes=([] if nm == 1 else [pltpu.VMEM((tko, tno), F32)]) + bg_scratch,
        input_output_aliases={} if into is None else {2: 0},
        compiler_params=_params(_semantics(bg, ("parallel", "parallel", "arbitrary"))),
    )(a, g, *([] if into is None else [into]), *bg_args)
    return _results(res, 1, bg)


def ffn_in_swiglu(h, w_blk, *, name, tm, bg=None):
    M, K = h.shape
    nb, _, Nb = w_blk.shape
    nh = nb // 2
    F = nh * Nb
    tm = _tile(M, tm, 8)
    grid = (nh, M // tm)
    bg_in_specs, bg_args, bg_out_specs, bg_out_shapes, bg_scratch, split = _carrier(bg, 3, 3, 0, grid)

    def body(*refs):
        (h_ref, wg_ref, wu_ref), (act_ref, gate_ref, up_ref), _, run_background = split(refs)
        run_background()
        hv = h_ref[...]
        gate = jnp.dot(hv, wg_ref[...], preferred_element_type=F32)
        up = jnp.dot(hv, wu_ref[...], preferred_element_type=F32)
        gate_ref[...] = gate.astype(gate_ref.dtype)
        up_ref[...] = up.astype(up_ref.dtype)
        act_ref[...] = (gate * jax.nn.sigmoid(gate) * up).astype(act_ref.dtype)

    o_spec = pl.BlockSpec((tm, Nb), lambda j, i: (i, j))
    res = pl.pallas_call(
        body, name=name, grid=grid,
        in_specs=[pl.BlockSpec((tm, K), lambda j, i: (i, 0)),
                  pl.BlockSpec((None, K, Nb), lambda j, i: (j, 0, 0)),
                  pl.BlockSpec((None, K, Nb), lambda j, i: (j + nh, 0, 0))] + bg_in_specs,
        out_specs=[o_spec, o_spec, o_spec] + bg_out_specs,
        out_shape=[jax.ShapeDtypeStruct((M, F), BF16), jax.ShapeDtypeStruct((M, F), BF16),
                   jax.ShapeDtypeStruct((M, F), BF16)] + bg_out_shapes,
        scratch_shapes=bg_scratch,
        compiler_params=_params(_semantics(bg, ("parallel", "parallel"))),
    )(h, w_blk, w_blk, *bg_args)
    return _results(res, 3, bg)


def _row_spec(tm, d, col=0):
    return pl.BlockSpec((tm, d), lambda i: (i, col))


def _vec_spec(d):
    return pl.BlockSpec((1, d), lambda i: (0, 0))


def norm_fwd(x, g, *, name, tm=512):
    L, D = x.shape
    tm = _tile(L, tm, 8)

    def body(x_ref, g_ref, h_ref):
        xv = x_ref[...]
        h_ref[...] = (xv * _rms(xv) * g_ref[...]).astype(h_ref.dtype)

    return pl.pallas_call(
        body, name=name, grid=(L // tm,),
        in_specs=[_row_spec(tm, D), _vec_spec(D)],
        out_specs=_row_spec(tm, D),
        out_shape=jax.ShapeDtypeStruct((L, D), BF16),
        compiler_params=_params(("parallel",)),
    )(x, g)


def norm_bwd(dh, xin, g, dres, *, name, want_bf16, tm=512):
    L, D = xin.shape
    tm = _tile(L, tm, 8)

    def body(dh_ref, x_ref, g_ref, dres_ref, dx_ref, *rest):
        dg_ref = rest[-1]
        xv, dhv = x_ref[...], dh_ref[...]
        r = _rms(xv)
        dx = dres_ref[...] + _rmsnorm_bwd(dhv, xv, r, g_ref[...])
        dx_ref[...] = dx
        if want_bf16:
            rest[0][...] = dx.astype(BF16)
        part = _rowsum(dhv * xv * r)

        @pl.when(pl.program_id(0) == 0)
        def _():
            dg_ref[...] = part

        @pl.when(pl.program_id(0) > 0)
        def _():
            dg_ref[...] += part

    out_specs = [_row_spec(tm, D)] + ([_row_spec(tm, D)] if want_bf16 else []) + [_vec_spec(D)]
    out_shape = ([jax.ShapeDtypeStruct((L, D), F32)]
                 + ([jax.ShapeDtypeStruct((L, D), BF16)] if want_bf16 else [])
                 + [jax.ShapeDtypeStruct((1, D), F32)])
    return pl.pallas_call(
        body, name=name, grid=(L // tm,),
        in_specs=[_row_spec(tm, D), _row_spec(tm, D), _vec_spec(D), _row_spec(tm, D)],
        out_specs=out_specs, out_shape=out_shape,
        compiler_params=_params(("arbitrary",)),
    )(dh, xin, g, dres)


def glu_pre(y0, *, name, tm=512):
    L, D = y0.shape
    tm = _tile(L, tm, 8)

    def body(y_ref, o_ref):
        o_ref[...] = _gelu(y_ref[...]).astype(o_ref.dtype)

    return pl.pallas_call(
        body, name=name, grid=(L // tm,),
        in_specs=[_row_spec(tm, D)], out_specs=_row_spec(tm, D),
        out_shape=jax.ShapeDtypeStruct((L, D), BF16),
        compiler_params=_params(("parallel",)),
    )(y0)


def glu_post(y0, t, b_glu, g_a, *, name, tm=512):
    L, D = y0.shape
    tm = _tile(L, tm, 8)

    def body(y_ref, t_ref, b_ref, g_ref, o_ref):
        ya = _gelu(y_ref[...]) * jax.nn.sigmoid(t_ref[...] + b_ref[...])
        o_ref[...] = (ya * _rms(ya) * g_ref[...]).astype(o_ref.dtype)

    return pl.pallas_call(
        body, name=name, grid=(L // tm,),
        in_specs=[_row_spec(tm, D), _row_spec(tm, D), _vec_spec(D), _vec_spec(D)],
        out_specs=_row_spec(tm, D),
        out_shape=jax.ShapeDtypeStruct((L, D), BF16),
        compiler_params=_params(("parallel",)),
    )(y0, t, b_glu, g_a)


def glu_post_bwd(y0, t, b_glu, g_a, dycat, *, name, tm=512):
    L, D = y0.shape
    tm = _tile(L, tm, 8)

    def body(y_ref, t_ref, b_ref, g_ref, dn_ref, dt_ref, dd_ref, dga_ref, dbg_ref):
        ya1 = _gelu(y_ref[...])
        sg = jax.nn.sigmoid(t_ref[...] + b_ref[...])
        ya = ya1 * sg
        ra = _rms(ya)
        dn = dn_ref[...]
        dya = _rmsnorm_bwd(dn, ya, ra, g_ref[...])
        dt = dya * ya1 * sg * (1.0 - sg)
        dt_ref[...] = dt.astype(BF16)
        dd_ref[...] = dya * sg
        p_ga, p_bg = _rowsum(dn * ya * ra), _rowsum(dt)

        @pl.when(pl.program_id(0) == 0)
        def _():
            dga_ref[...] = p_ga
            dbg_ref[...] = p_bg

        @pl.when(pl.program_id(0) > 0)
        def _():
            dga_ref[...] += p_ga
            dbg_ref[...] += p_bg

    return pl.pallas_call(
        body, name=name, grid=(L // tm,),
        in_specs=[_row_spec(tm, D), _row_spec(tm, D), _vec_spec(D), _vec_spec(D), _row_spec(tm, D, 0)],
        out_specs=[_row_spec(tm, D), _row_spec(tm, D), _vec_spec(D), _vec_spec(D)],
        out_shape=[jax.ShapeDtypeStruct((L, D), BF16), jax.ShapeDtypeStruct((L, D), F32),
                   jax.ShapeDtypeStruct((1, D), F32), jax.ShapeDtypeStruct((1, D), F32)],
        compiler_params=_params(("arbitrary",)),
    )(y0, t, b_glu, g_a, dycat)


def head_and_loss(x2, gpre, b_g, pp, g_f, tgt, *, name, tm=256):
    L, D = x2.shape
    tm = _tile(L, tm, 8)

    def body(x2_ref, gp_ref, bg_ref, pp_ref, gf_ref, tg_ref,
             dx3_ref, dpre_ref, dpp_ref, loss_ref, dgf_ref, dbg_ref):
        gate = jax.nn.sigmoid(gp_ref[...] + bg_ref[...])
        ppv = pp_ref[...]
        x3 = x2_ref[...] + gate * ppv
        r = _rms(x3)
        xn = x3 * r
        gf = gf_ref[...]
        err = xn * gf - tg_ref[...]
        loss = 0.5 * jnp.sum(jnp.mean(err * err, axis=-1, keepdims=True), axis=0, keepdims=True)
        dout = err * (1.0 / D)
        dx3 = _rmsnorm_bwd(dout, x3, r, gf)
        dx3_ref[...] = dx3
        dpre = dx3 * ppv * gate * (1.0 - gate)
        dpre_ref[...] = dpre.astype(BF16)
        dpp_ref[...] = (dx3 * gate).astype(BF16)
        p_gf, p_bg = _rowsum(dout * xn), _rowsum(dpre)
        p_loss = jnp.broadcast_to(loss, loss_ref.shape)

        @pl.when(pl.program_id(0) == 0)
        def _():
            loss_ref[...] = p_loss
            dgf_ref[...] = p_gf
            dbg_ref[...] = p_bg

        @pl.when(pl.program_id(0) > 0)
        def _():
            loss_ref[...] += p_loss
            dgf_ref[...] += p_gf
            dbg_ref[...] += p_bg

    rs = _row_spec(tm, D)
    return pl.pallas_call(
        body, name=name, grid=(L // tm,),
        in_specs=[rs, rs, _vec_spec(D), rs, _vec_spec(D), rs],
        out_specs=[rs, rs, rs, pl.BlockSpec((8, LANE), lambda i: (0, 0)), _vec_spec(D), _vec_spec(D)],
        out_shape=[jax.ShapeDtypeStruct((L, D), F32), jax.ShapeDtypeStruct((L, D), BF16),
                   jax.ShapeDtypeStruct((L, D), BF16), jax.ShapeDtypeStruct((8, LANE), F32),
                   jax.ShapeDtypeStruct((1, D), F32), jax.ShapeDtypeStruct((1, D), F32)],
        compiler_params=_params(("arbitrary",)),
    )(x2, gpre, b_g, pp, g_f, tgt)


def ffn_out_bwd_swiglu(dx, w, gate, up, *, name, tm=512, tf=1408):
    M, D = dx.shape
    F = w.shape[0]
    tm, tf = _tile(M, tm, 8), _tile(F, tf)

    def body(dx_ref, w_ref, g_ref, u_ref, dg_ref, du_ref):
        da = lax.dot_general(dx_ref[...], w_ref[...], (((1,), (1,)), ((), ())), preferred_element_type=F32)
        gv = g_ref[...].astype(F32)
        sg = jax.nn.sigmoid(gv)
        dg_ref[...] = (da * u_ref[...].astype(F32) * sg * (1.0 + gv * (1.0 - sg))).astype(BF16)
        du_ref[...] = (da * gv * sg).astype(BF16)

    spec = pl.BlockSpec((tm, tf), lambda j, i: (i, j))
    return pl.pallas_call(
        body, name=name, grid=(F // tf, M // tm),
        in_specs=[pl.BlockSpec((tm, D), lambda j, i: (i, 0)), pl.BlockSpec((tf, D), lambda j, i: (j, 0)), spec, spec],
        out_specs=[spec, spec],
        out_shape=[jax.ShapeDtypeStruct((M, F), BF16), jax.ShapeDtypeStruct((M, F), BF16)],
        compiler_params=_params(("parallel", "parallel")),
    )(dx, w, gate, up)


def _sgu_forward_values(u1, v1, lng, lnb, w_ref, bs_ref, s_scr, heads, hd):
    xc = v1 - jnp.mean(v1, axis=-1, keepdims=True)
    r = lax.rsqrt(jnp.mean(xc * xc, axis=-1, keepdims=True) + EPS)
    xhat = xc * r
    v2 = xhat * lng + lnb
    tril = (lax.broadcasted_iota(jnp.int32, (SGU_CHUNK, SGU_CHUNK), 0)
            >= lax.broadcasted_iota(jnp.int32, (SGU_CHUNK, SGU_CHUNK), 1))
    for h in range(heads):
        wm = jnp.where(tril, w_ref[h], 0.0).astype(BF16)
        cols = slice(h * hd, (h + 1) * hd)
        s_scr[:, cols] = jnp.dot(wm, v2[:, cols].astype(BF16), preferred_element_type=F32) + bs_ref[h]
    return xhat, r, v2, tril


def sgu_fwd(z, ln_g, ln_b, w_s, b_s, g_b, *, name, d_sgu):
    L = z.shape[0]
    heads = w_s.shape[0]
    hd = d_sgu // heads

    def body(zu_ref, zv_ref, lng_ref, lnb_ref, w_ref, bs_ref, gb_ref, o_ref, s_scr):
        u1 = _gelu(zu_ref[...])
        _sgu_forward_values(u1, _gelu(zv_ref[...]), lng_ref[...], lnb_ref[...], w_ref, bs_ref, s_scr, heads, hd)
        yb = u1 * s_scr[...]
        o_ref[...] = (yb * _rms(yb) * gb_ref[...]).astype(o_ref.dtype)

    blk = lambda col: pl.BlockSpec((SGU_CHUNK, d_sgu), lambda n: (n, col))
    return pl.pallas_call(
        body, name=name, grid=(L // SGU_CHUNK,),
        in_specs=[blk(1), blk(2), _vec_spec(d_sgu), _vec_spec(d_sgu),
                  pl.BlockSpec(w_s.shape, lambda n: (0, 0, 0)), pl.BlockSpec(b_s.shape, lambda n: (0, 0, 0)),
                  _vec_spec(d_sgu)],
        out_specs=blk(0),
        out_shape=jax.ShapeDtypeStruct((L, d_sgu), BF16),
        scratch_shapes=[pltpu.VMEM((SGU_CHUNK, d_sgu), F32)],
        compiler_params=_params(("parallel",)),
    )(z, z, ln_g, ln_b, w_s, b_s, g_b)


def sgu_bwd(z, dycat, ln_g, ln_b, w_s, b_s, g_b, *, name, d_sgu):
    L = z.shape[0]
    heads = w_s.shape[0]
    hd = d_sgu // heads

    def body(zu_ref, zv_ref, dn_ref, lng_ref, lnb_ref, w_ref, bs_ref, gb_ref,
             dzu_ref, dzv_ref, dw_ref, dbs_ref, dlng_ref, dlnb_ref, dgb_ref, s_scr, dv_scr):
        first = pl.program_id(0) == 0
        lng = lng_ref[...]
        u1, du1 = _gelu_and_grad(zu_ref[...])
        v1, dv1_dz = _gelu_and_grad(zv_ref[...])
        xhat, r, v2, tril = _sgu_forward_values(u1, v1, lng, lnb_ref[...], w_ref, bs_ref, s_scr, heads, hd)
        s = s_scr[...]
        yb = u1 * s
        rb = _rms(yb)
        dn = dn_ref[...]
        dyb = _rmsnorm_bwd(dn, yb, rb, gb_ref[...])
        dzu_ref[...] = (dyb * s * du1).astype(BF16)
        ds = dyb * u1
        for h in range(heads):
            cols = slice(h * hd, (h + 1) * hd)
            ds_h = ds[:, cols]
            ds_hb = ds_h.astype(BF16)
            wm = jnp.where(tril, w_ref[h], 0.0).astype(BF16)
            dw_h = jnp.where(tril, lax.dot_general(ds_hb, v2[:, cols].astype(BF16), (((1,), (1,)), ((), ())),
                                                   preferred_element_type=F32), 0.0)
            db_h = jnp.sum(ds_h.T, axis=0, keepdims=True)
            dv_scr[:, cols] = lax.dot_general(wm, ds_hb, (((0,), (0,)), ((), ())), preferred_element_type=F32)

            @pl.when(first)
            def _():
                dw_ref[h] = dw_h
                dbs_ref[h] = db_h

            @pl.when(jnp.logical_not(first))
            def _():
                dw_ref[h] += dw_h
                dbs_ref[h] += db_h

        dv2 = dv_scr[...]
        dxh = dv2 * lng
        dv1 = r * (dxh - jnp.mean(dxh, axis=-1, keepdims=True)
                   - xhat * jnp.mean(dxh * xhat, axis=-1, keepdims=True))
        dzv_ref[...] = (dv1 * dv1_dz).astype(BF16)
        p_lng, p_lnb, p_gb = _rowsum(dv2 * xhat), _rowsum(dv2), _rowsum(dn * yb * rb)

        @pl.when(first)
        def _():
            dlng_ref[...] = p_lng
            dlnb_ref[...] = p_lnb
            dgb_ref[...] = p_gb

        @pl.when(jnp.logical_not(first))
        def _():
            dlng_ref[...] += p_lng
            dlnb_ref[...] += p_lnb
            dgb_ref[...] += p_gb

    blk = lambda col: pl.BlockSpec((SGU_CHUNK, d_sgu), lambda n: (n, col))
    full3 = lambda shape: pl.BlockSpec(shape, lambda n: (0, 0, 0))
    return pl.pallas_call(
        body, name=name, grid=(L // SGU_CHUNK,),
        in_specs=[blk(1), blk(2), blk(1), _vec_spec(d_sgu), _vec_spec(d_sgu),
                  full3(w_s.shape), full3(b_s.shape), _vec_spec(d_sgu)],
        out_specs=[blk(0), blk(0), full3(w_s.shape), full3((heads, 1, SGU_CHUNK)),
                   _vec_spec(d_sgu), _vec_spec(d_sgu), _vec_spec(d_sgu)],
        out_shape=[jax.ShapeDtypeStruct((L, d_sgu), BF16), jax.ShapeDtypeStruct((L, d_sgu), BF16),
                   jax.ShapeDtypeStruct(w_s.shape, F32), jax.ShapeDtypeStruct((heads, 1, SGU_CHUNK), F32),
                   jax.ShapeDtypeStruct((1, d_sgu), F32), jax.ShapeDtypeStruct((1, d_sgu), F32),
                   jax.ShapeDtypeStruct((1, d_sgu), F32)],
        scratch_shapes=[pltpu.VMEM((SGU_CHUNK, d_sgu), F32), pltpu.VMEM((SGU_CHUNK, d_sgu), F32)],
        compiler_params=_params(("arbitrary",)),
    )(z, z, dycat, ln_g, ln_b, w_s, b_s, g_b)


def _disc_lambda(lam_re, lam_im, log_step):
    lr = jnp.minimum(lam_re, LAMBDA_RE_MAX)
    li = lam_im
    dt = jnp.exp(log_step)
    mag = jnp.exp(lr * dt)
    ang = li * dt
    a_re = mag * jnp.cos(ang)
    a_im = mag * jnp.sin(ang)
    nr = a_re - 1.0
    ni = a_im
    den = lr * lr + li * li
    return a_re, a_im, (nr * lr + ni * li) / den, (ni * lr - nr * li) / den


def _disc_b(q_re, q_im, b_re, b_im):
    return q_re * b_re - q_im * b_im, q_re * b_im + q_im * b_re


def disc_lambda_fwd(lam_re, lam_im, log_step, *, name):
    def body(lr_ref, li_ref, ls_ref, ar_ref, ai_ref, qr_ref, qi_ref):
        ar_ref[...], ai_ref[...], qr_ref[...], qi_ref[...] = _disc_lambda(lr_ref[...], li_ref[...], ls_ref[...])

    sd = jax.ShapeDtypeStruct(lam_re.shape, F32)
    return pl.pallas_call(body, name=name, out_shape=[sd, sd, sd, sd], compiler_params=_params())(
        lam_re, lam_im, log_step)


def disc_lambda_bwd(lam_re, lam_im, log_step, cts, *, name):
    def body(lr_ref, li_ref, ls_ref, c0, c1, c2, c3, dlr_ref, dli_ref, dls_ref):
        _, vjp = jax.vjp(_disc_lambda, lr_ref[...], li_ref[...], ls_ref[...])
        dlr_ref[...], dli_ref[...], dls_ref[...] = vjp((c0[...], c1[...], c2[...], c3[...]))

    sd = jax.ShapeDtypeStruct(lam_re.shape, F32)
    return pl.pallas_call(body, name=name, out_shape=[sd, sd, jax.ShapeDtypeStruct(log_step.shape, F32)],
                          compiler_params=_params())(lam_re, lam_im, log_step, *cts)


def disc_b_fwd(q_re, q_im, b_re, b_im, *, name):
    def body(qr_ref, qi_ref, br_ref, bi_ref, or_ref, oi_ref):
        or_ref[...], oi_ref[...] = _disc_b(qr_ref[...], qi_ref[...], br_ref[...], bi_ref[...])

    sd = jax.ShapeDtypeStruct(b_re.shape, F32)
    return pl.pallas_call(body, name=name, out_shape=[sd, sd], compiler_params=_params())(q_re, q_im, b_re, b_im)


def disc_b_bwd(q_re, q_im, b_re, b_im, ct_re, ct_im, *, name):
    def body(qr_ref, qi_ref, br_ref, bi_ref, cr_ref, ci_ref, dqr_ref, dqi_ref, dbr_ref, dbi_ref):
        _, vjp = jax.vjp(_disc_b, qr_ref[...], qi_ref[...], br_ref[...], bi_ref[...])
        dqr_ref[...], dqi_ref[...], dbr_ref[...], dbi_ref[...] = vjp((cr_ref[...], ci_ref[...]))

    sq, sb = jax.ShapeDtypeStruct(q_re.shape, F32), jax.ShapeDtypeStruct(b_re.shape, F32)
    return pl.pallas_call(body, name=name, out_shape=[sq, sq, sb, sb], compiler_params=_params())(
        q_re, q_im, b_re, b_im, ct_re, ct_im)


def _lti_scan(xr, xi, ar, ai, reverse):
    T = xr.shape[0]
    row = lax.broadcasted_iota(jnp.int32, xr.shape, 0)
    k = 1
    while k < T:
        shift = T - k if reverse else k
        keep = (row < T - k) if reverse else (row >= k)
        sr = jnp.where(keep, pltpu.roll(xr, shift, 0), 0.0)
        si = jnp.where(keep, pltpu.roll(xi, shift, 0), 0.0)
        xr, xi = xr + ar * sr - ai * si, xi + ar * si + ai * sr
        ar, ai = ar * ar - ai * ai, 2.0 * ar * ai
        k *= 2
    return xr, xi


SUBLANES = 8


def _scan_rows(x_re, x_im, o_re, o_im, ar, ai, cr, ci, reverse):
    T, n = x_re.shape
    groups = T // SUBLANES
    row = lax.broadcasted_iota(jnp.int32, (SUBLANES, n), 0)
    edge = SUBLANES - 1 if reverse else 0
    pr, pi = _lti_scan(jnp.where(row == edge, ar, 0.0), jnp.where(row == edge, ai, 0.0), ar, ai, reverse)
    pows = []
    for level in range(3):
        k = 1 << level
        keep = (row < SUBLANES - k) if reverse else (row >= k)
        pows.append((jnp.where(keep, ar, 0.0), jnp.where(keep, ai, 0.0)))
        ar, ai = ar * ar - ai * ai, 2.0 * ar * ai

    def group(i, carry):
        cr, ci = carry
        at = pl.multiple_of((groups - 1 - i if reverse else i) * SUBLANES, SUBLANES)
        xr, xi = x_re[pl.ds(at, SUBLANES), :], x_im[pl.ds(at, SUBLANES), :]
        for level, (qr, qi) in enumerate(pows):
            shift = SUBLANES - (1 << level) if reverse else 1 << level
            sr, si = pltpu.roll(xr, shift, 0), pltpu.roll(xi, shift, 0)
            xr, xi = xr + qr * sr - qi * si, xi + qr * si + qi * sr
        xr, xi = xr + pr * cr - pi * ci, xi + pr * ci + pi * cr
        o_re[pl.ds(at, SUBLANES), :] = xr
        o_im[pl.ds(at, SUBLANES), :] = xi
        return spread(xr[last:last + 1, :]), spread(xi[last:last + 1, :])

    last = 0 if reverse else SUBLANES - 1
    spread = lambda v: jnp.broadcast_to(v, (SUBLANES, n))
    cr, ci = lax.fori_loop(0, groups, group, (spread(cr), spread(ci)), unroll=2)
    return cr[0:1, :], ci[0:1, :]


def _ssm_chunk(L):
    return _tile(L, 512, 8)


def _same_group(rows, cols):
    r = lax.broadcasted_iota(jnp.int32, (rows, cols), 0) // SSM_GROUP
    c = lax.broadcasted_iota(jnp.int32, (rows, cols), 1) // SSM_STATE
    return r == c


def _expand_groups(compact):
    H, S = compact.shape
    tiled = jnp.concatenate([compact] * (S // SSM_STATE), axis=0)
    return jnp.where(_same_group(tiled.shape[0], S), tiled, 0.0).astype(BF16)


def _collapse_groups(dense):
    C, S = dense.shape
    masked = jnp.where(_same_group(C, S), dense, 0.0)
    total = masked[0:SSM_GROUP]
    for g in range(1, C // SSM_GROUP):
        total = total + masked[g * SSM_GROUP:(g + 1) * SSM_GROUP]
    return total


def ssm_fwd(z, bt_re, bt_im, ct_re, ct_im, a_re, a_im, d, *, name, bg=None):
    L = z.shape[0]
    NK, _, S = a_re.shape
    H = bt_re.shape[0]
    C = S // SSM_STATE * SSM_GROUP
    T = _ssm_chunk(L)
    grid = (NK, L // T)
    bg_in_specs, bg_args, bg_out_specs, bg_out_shapes, bg_scratch, split = _carrier(bg, 8, 3, 8, grid)
    nt_dot = lambda p, q: lax.dot_general(p, q, (((1,), (1,)), ((), ())), preferred_element_type=F32)

    def body(*refs):
        ((u_ref, btr_ref, bti_ref, ctr_ref, cti_ref, ar_ref, ai_ref, d_ref), (y_ref, sr_ref, si_ref),
         (car_re, car_im, bu_re, bu_im, b_re, b_im, c_re, c_im), run_background) = split(refs)
        run_background()
        i = pl.program_id(1)
        ar, ai = ar_ref[...], ai_ref[...]

        @pl.when(i == 0)
        def _():
            car_re[...] = jnp.zeros_like(car_re)
            car_im[...] = jnp.zeros_like(car_im)
            b_re[...] = _expand_groups(btr_ref[...])
            b_im[...] = _expand_groups(bti_ref[...])
            c_re[...] = _expand_groups(ctr_ref[...])
            c_im[...] = _expand_groups(cti_ref[...])

        u = u_ref[...]
        ub = u.astype(BF16)
        bu_re[...] = jnp.dot(ub, b_re[...], preferred_element_type=F32)
        bu_im[...] = jnp.dot(ub, b_im[...], preferred_element_type=F32)
        car_re[...], car_im[...] = _scan_rows(bu_re, bu_im, sr_ref, si_ref, ar, ai, car_re[...], car_im[...], False)
        y_ref[...] = (nt_dot(sr_ref[...].astype(BF16), c_re[...]) - nt_dot(si_ref[...].astype(BF16), c_im[...])
                      + d_ref[...] * u)

    kspec = lambda shape: pl.BlockSpec((None,) + shape, lambda k, i: (k, 0, 0))
    compact = pl.BlockSpec((H, S), lambda k, i: (0, k))
    res = pl.pallas_call(
        body, name=name, grid=grid,
        in_specs=[pl.BlockSpec((T, C), lambda k, i: (i, k)), compact, compact, compact, compact,
                  kspec((1, S)), kspec((1, S)), kspec((1, C))] + bg_in_specs,
        out_specs=[pl.BlockSpec((T, C), lambda k, i: (i, k)),
                   pl.BlockSpec((T, S), lambda k, i: (i, k)), pl.BlockSpec((T, S), lambda k, i: (i, k))] + bg_out_specs,
        out_shape=[jax.ShapeDtypeStruct((L, NK * C), F32), jax.ShapeDtypeStruct((L, NK * S), F32),
                   jax.ShapeDtypeStruct((L, NK * S), F32)] + bg_out_shapes,
        scratch_shapes=[pltpu.VMEM((1, S), F32), pltpu.VMEM((1, S), F32),
                        pltpu.VMEM((T, S), F32), pltpu.VMEM((T, S), F32)]
        + [pltpu.VMEM((C, S), BF16)] * 4 + bg_scratch,
        compiler_params=_params(_semantics(bg, ("parallel", "arbitrary"))),
    )(z, bt_re, bt_im, ct_re, ct_im, a_re, a_im, d, *bg_args)
    return _results(res, 3, bg)


def ssm_bwd(z, y0, dd_direct, dd_mm, s_re, s_im, bt_re, bt_im, ct_re, ct_im, a_re, a_im, d, *, name, bg=None):
    L = z.shape[0]
    NK, _, S = a_re.shape
    H = bt_re.shape[0]
    C = S // SSM_STATE * SSM_GROUP
    T = _ssm_chunk(L)
    nchunk = L // T
    tail = T // 8
    grid = (NK, nchunk)
    bg_in_specs, bg_args, bg_out_specs, bg_out_shapes, bg_scratch, split = _carrier(bg, 15, 8, 12, grid)
    nt_dot = lambda p, q: lax.dot_general(p, q, (((1,), (1,)), ((), ())), preferred_element_type=F32)

    def body(*refs):
        ((u_ref, y_ref, d1_ref, d2_ref, sr_ref, si_ref, pr_ref, pi_ref,
          btr_ref, bti_ref, ctr_ref, cti_ref, ar_ref, ai_ref, d_ref),
         (du_ref, dbr_ref, dbi_ref, dcr_ref, dci_ref, dar_ref, dai_ref, dd_ref),
         (car_re, car_im, lam_re, lam_im, b_re, b_im, c_re, c_im, acc_br, acc_bi, acc_cr, acc_ci),
         run_background) = split(refs)
        run_background()
        i = pl.program_id(1)
        chunk = nchunk - 1 - i
        ar, ai = ar_ref[...], ai_ref[...]
        row = lax.broadcasted_iota(jnp.int32, (T, S), 0)

        @pl.when(i == 0)
        def _():
            car_re[...] = jnp.zeros_like(car_re)
            car_im[...] = jnp.zeros_like(car_im)
            b_re[...] = _expand_groups(btr_ref[...])
            b_im[...] = _expand_groups(bti_ref[...])
            c_re[...] = _expand_groups(ctr_ref[...])
            c_im[...] = _expand_groups(cti_ref[...])

        u = u_ref[...]
        dy = (d1_ref[...] + d2_ref[...]) * _gelu_grad(y_ref[...])
        dyb = dy.astype(BF16)
        lam_re[...] = jnp.dot(dyb, c_re[...], preferred_element_type=F32)
        lam_im[...] = -jnp.dot(dyb, c_im[...], preferred_element_type=F32)
        car_re[...], car_im[...] = _scan_rows(lam_re, lam_im, lam_re, lam_im, ar, -ai, car_re[...], car_im[...], True)
        lr, li = lam_re[...], lam_im[...]

        s_re, s_im = sr_ref[...], si_ref[...]
        has_prev = (chunk > 0).astype(F32)
        prev_re = pr_ref[7:8, :] * has_prev
        prev_im = pi_ref[7:8, :] * has_prev
        sp_re = jnp.where(row == 0, prev_re, pltpu.roll(s_re, 1, 0))
        sp_im = jnp.where(row == 0, prev_im, pltpu.roll(s_im, 1, 0))
        p_ar = _rowsum(lr * sp_re + li * sp_im)
        p_ai = _rowsum(li * sp_re - lr * sp_im)

        lrb, lib, ub = lr.astype(BF16), li.astype(BF16), u.astype(BF16)
        du = dy * d_ref[...] + nt_dot(lrb, b_re[...]) + nt_dot(lib, b_im[...])
        du_ref[...] = du.astype(BF16)
        tdot = lambda p, q: lax.dot_general(p, q, (((0,), (0,)), ((), ())), preferred_element_type=F32)
        p_br, p_bi = tdot(ub, lrb), tdot(ub, lib)
        p_cr, p_ci = tdot(dyb, s_re.astype(BF16)), -tdot(dyb, s_im.astype(BF16))
        p_dd = _rowsum(dy * u)

        @pl.when(i == 0)
        def _():
            dar_ref[...] = p_ar
            dai_ref[...] = p_ai
            acc_br[...] = p_br
            acc_bi[...] = p_bi
            acc_cr[...] = p_cr
            acc_ci[...] = p_ci
            dd_ref[...] = p_dd

        @pl.when(i > 0)
        def _():
            dar_ref[...] += p_ar
            dai_ref[...] += p_ai
            acc_br[...] += p_br
            acc_bi[...] += p_bi
            acc_cr[...] += p_cr
            acc_ci[...] += p_ci
            dd_ref[...] += p_dd

        @pl.when(i == nchunk - 1)
        def _():
            dbr_ref[...] = _collapse_groups(acc_br[...])
            dbi_ref[...] = _collapse_groups(acc_bi[...])
            dcr_ref[...] = _collapse_groups(acc_cr[...])
            dci_ref[...] = _collapse_groups(acc_ci[...])

    rev = lambda k, i: (nchunk - 1 - i, k)
    prev = lambda k, i: (jnp.maximum((nchunk - 1 - i) * tail - 1, 0), k)
    kspec = lambda shape: pl.BlockSpec((None,) + shape, lambda k, i: (k, 0, 0))
    compact = pl.BlockSpec((H, S), lambda k, i: (0, k))
    compact_shape = jax.ShapeDtypeStruct((H, NK * S), F32)
    res = pl.pallas_call(
        body, name=name, grid=grid,
        in_specs=[pl.BlockSpec((T, C), rev), pl.BlockSpec((T, C), rev), pl.BlockSpec((T, C), rev),
                  pl.BlockSpec((T, C), rev), pl.BlockSpec((T, S), rev), pl.BlockSpec((T, S), rev),
                  pl.BlockSpec((8, S), prev), pl.BlockSpec((8, S), prev),
                  compact, compact, compact, compact,
                  kspec((1, S)), kspec((1, S)), kspec((1, C))] + bg_in_specs,
        out_specs=[pl.BlockSpec((T, C), rev), compact, compact, compact, compact,
                   kspec((1, S)), kspec((1, S)), kspec((1, C))] + bg_out_specs,
        out_shape=[jax.ShapeDtypeStruct((L, NK * C), BF16),
                   compact_shape, compact_shape, compact_shape, compact_shape,
                   jax.ShapeDtypeStruct((NK, 1, S), F32), jax.ShapeDtypeStruct((NK, 1, S), F32),
                   jax.ShapeDtypeStruct((NK, 1, C), F32)] + bg_out_shapes,
        scratch_shapes=[pltpu.VMEM((1, S), F32), pltpu.VMEM((1, S), F32),
                        pltpu.VMEM((T, S), F32), pltpu.VMEM((T, S), F32)]
        + [pltpu.VMEM((C, S), BF16)] * 4 + [pltpu.VMEM((C, S), F32)] * 4 + bg_scratch,
        compiler_params=_params(_semantics(bg, ("parallel", "arbitrary"))),
    )(z, y0, dd_direct, dd_mm, s_re, s_im, s_re, s_im, bt_re, bt_im, ct_re, ct_im, a_re, a_im, d, *bg_args)
    return _results(res, 8, bg)


def _position():
    return lax.axis_index("x"), lax.axis_index("y"), lax.axis_index("c")


def _other_chips(x, y):
    return [(1 - x, y), (x, 1 - y), (1 - x, 1 - y)]


def _gather_phases(n, rows=None):
    def parts(ins, outs, sems):
        send_sems, recv_sems, local_sems = sems
        x, y, c = _position()
        me, sibling = (x, y, c), (x, y, 1 - c)
        chips = _other_chips(x, y)

        def block(a, pos):
            index = 4 * pos[0] + 2 * pos[1] + pos[2]
            if rows is not None:
                return outs[a].at[index, pl.ds(*rows)]
            return outs[a].at[pl.ds(index, 1)] if _is_row(ins[a]) else outs[a].at[index]

        def copy(a, k, pos, to, src=None):
            return pltpu.make_async_remote_copy(
                src_ref=block(a, pos) if src is None else src, dst_ref=block(a, pos),
                send_sem=send_sems.at[7 * a + k], recv_sem=recv_sems.at[7 * a + k],
                device_id=to, device_id_type=MESH)

        shard = [ins[a] if rows is None else ins[a].at[pl.ds(*rows)] for a in range(n)]
        mine = [pltpu.make_async_copy(shard[a], block(a, me), local_sems.at[a]) for a in range(n)]
        first = []
        for a in range(n):
            first.append(copy(a, 0, me, sibling, src=shard[a]))
            first += [copy(a, 1 + j, me, (*chip, c), src=shard[a]) for j, chip in enumerate(chips)]
        passed = [copy(a, 4 + j, (*chip, c), sibling) for a in range(n) for j, chip in enumerate(chips)]
        arrived = [copy(a, 1 + j, (*chip, c), me) for a in range(n) for j, chip in enumerate(chips)]
        from_sibling = []
        for a in range(n):
            from_sibling.append(copy(a, 0, sibling, me))
            from_sibling += [copy(a, 4 + j, (*chip, 1 - c), me) for j, chip in enumerate(chips)]
        return mine, first, passed, arrived, from_sibling

    def send(ins, outs, sems):
        mine, first, _, _, _ = parts(ins, outs, sems)
        for cp in mine + first:
            cp.start()

    def forward(ins, outs, sems):
        _, _, passed, arrived, _ = parts(ins, outs, sems)
        for got, fwd in zip(arrived, passed):
            got.wait_recv()
            fwd.start()

    def finish(ins, outs, sems):
        mine, first, passed, _, from_sibling = parts(ins, outs, sems)
        for cp in from_sibling:
            cp.wait_recv()
        for cp in first + passed:
            cp.wait_send()
        for cp in mine:
            cp.wait()

    return [(0.0, send), (GATHER_FORWARD_AT, forward), (1.0, finish)]


def _is_row(a):
    return len(a.shape) == 2 and a.shape[0] == 1


def _gather_shapes(shards):
    n = len(shards)
    return ([jax.ShapeDtypeStruct((N_DEV,) + (s.shape[1:] if _is_row(s) else s.shape), s.dtype) for s in shards],
            [pltpu.SemaphoreType.DMA((7 * n,)), pltpu.SemaphoreType.DMA((7 * n,)), pltpu.SemaphoreType.DMA((n,))])


def gather_background(shards, rows=None, into=None):
    out_shapes, scratch = _gather_shapes(shards)
    bg = Background(list(shards) + list(into or []), out_shapes, scratch, _gather_phases(len(shards), rows))
    bg.aliases = {len(shards) + k: k for k in range(len(into or []))}
    return bg


def all_gather_blocks(shards, *, name):
    n = len(shards)
    out_shapes, scratch = _gather_shapes(shards)

    def body(*refs):
        for _, phase in _gather_phases(n):
            phase(refs[:n], refs[n:2 * n], refs[2 * n:])

    return pl.pallas_call(
        body, name=name, in_specs=[ANY] * n, out_specs=[ANY] * n, out_shape=out_shapes, scratch_shapes=scratch,
    )(*shards)


def sibling_exchange(grads, *, name):
    n = len(grads)
    bg = sibling_exchange_background(grads)

    def body(*refs):
        for _, phase in bg.phases:
            phase(refs[:n], refs[n:2 * n], refs[2 * n:])

    return pl.pallas_call(
        body, name=name, in_specs=[ANY] * n, out_specs=[ANY] * n, out_shape=bg.out_shapes, scratch_shapes=bg.scratch,
    )(*grads)


def sibling_exchange_background(grads):
    n = len(grads)

    def copies(ins, outs, sems):
        x, y, c = _position()
        return [pltpu.make_async_remote_copy(
            src_ref=ins[a].at[2 * q + 1 - c], dst_ref=outs[a].at[q],
            send_sem=sems[0].at[4 * a + q], recv_sem=sems[1].at[4 * a + q],
            device_id=(x, y, 1 - c), device_id_type=MESH)
            for a in range(n) for q in range(4)]

    def send(ins, outs, sems):
        for cp in copies(ins, outs, sems):
            cp.start()

    def finish(ins, outs, sems):
        for cp in copies(ins, outs, sems):
            cp.wait()

    return Background(grads, [jax.ShapeDtypeStruct((4,) + g.shape[1:], g.dtype) for g in grads],
                      [pltpu.SemaphoreType.DMA((4 * n,)), pltpu.SemaphoreType.DMA((4 * n,))],
                      [(0.0, send), (1.0, finish)])


def _chip_exchange_phases(n):
    def copies(ins, outs, sems):
        x, y, c = _position()
        return [pltpu.make_async_remote_copy(
            src_ref=ins[a].at[2 * chip[0] + chip[1]], dst_ref=outs[a].at[j],
            send_sem=sems[0].at[3 * a + j], recv_sem=sems[1].at[3 * a + j],
            device_id=(*chip, c), device_id_type=MESH)
            for a in range(n) for j, chip in enumerate(_other_chips(x, y))]

    def send(ins, outs, sems):
        for cp in copies(ins, outs, sems):
            cp.start()

    def finish(ins, outs, sems):
        for cp in copies(ins, outs, sems):
            cp.wait()

    return [(0.0, send), (1.0, finish)]


def chip_exchange_background(parts):
    n = len(parts)
    return Background(parts, [jax.ShapeDtypeStruct((3,) + p.shape[1:], p.dtype) for p in parts],
                      [pltpu.SemaphoreType.DMA((3 * n,)), pltpu.SemaphoreType.DMA((3 * n,))],
                      _chip_exchange_phases(n))


def add_pairs(grads, theirs, core, *, name, tm=512):
    _, R, C = theirs.shape
    tm = _tile(R, tm, 16)

    def body(core_ref, a_ref, b_ref, o_ref):
        o_ref[...] = (a_ref[...].astype(F32) + b_ref[...].astype(F32)).astype(o_ref.dtype)

    spec = pl.BlockSpec((None, tm, C), lambda q, i, core_ref: (q, i, 0))
    return pl.pallas_call(
        body, name=name,
        grid_spec=pltpu.PrefetchScalarGridSpec(
            num_scalar_prefetch=1, grid=(4, R // tm),
            in_specs=[pl.BlockSpec((None, tm, C), lambda q, i, core_ref: (2 * q + core_ref[0], i, 0)), spec],
            out_specs=spec),
        out_shape=jax.ShapeDtypeStruct(theirs.shape, BF16),
        compiler_params=_params(("parallel", "parallel")),
    )(core, grads, theirs)


def _adamw(w, g, m, v):
    m = ADAM_B1 * m + (1.0 - ADAM_B1) * g
    v = ADAM_B2 * v + (1.0 - ADAM_B2) * (g * g)
    m_hat = m / (1.0 - ADAM_B1 ** ADAM_STEP)
    v_hat = v / (1.0 - ADAM_B2 ** ADAM_STEP)
    delta = -ADAM_LR * (m_hat / (jnp.sqrt(v_hat) + ADAM_EPS) + ADAM_WD * w)
    return delta, m, v


def adamw_sharded(w, m, v, grads, theirs, others, where, *, name, tm=256):
    R, C = w.shape
    tm = _tile(R, tm, 16)

    def body(where_ref, w_ref, m_ref, v_ref, a_ref, b_ref, o_ref, g_ref, d_ref, nm_ref, nv_ref):
        g = a_ref[...].astype(F32) + b_ref[...].astype(F32)
        for j in range(3):
            g = g + o_ref[j].astype(F32)
        g_ref[...] = g
        d_ref[...], nm_ref[...], nv_ref[...] = _adamw(w_ref[...], g, m_ref[...], v_ref[...])

    spec = pl.BlockSpec((tm, C), lambda i, where_ref: (i, 0))
    sd = jax.ShapeDtypeStruct((R, C), F32)
    return pl.pallas_call(
        body, name=name,
        grid_spec=pltpu.PrefetchScalarGridSpec(
            num_scalar_prefetch=1, grid=(R // tm,),
            in_specs=[spec, spec, spec,
                      pl.BlockSpec((None, tm, C), lambda i, where_ref: (where_ref[0], i, 0)),
                      pl.BlockSpec((None, tm, C), lambda i, where_ref: (where_ref[1], i, 0)),
                      pl.BlockSpec((3, tm, C), lambda i, where_ref: (0, i, 0))],
            out_specs=[spec, spec, spec, spec]),
        out_shape=[sd, sd, sd, sd],
        compiler_params=_params(("parallel",)),
    )(where, w, m, v, grads, theirs, others)


def sum_gathered(gathered, *, name):
    n = len(gathered)

    def body(*refs):
        for ga_ref, o_ref in zip(refs[:n], refs[n:]):
            rows = len(ga_ref.shape) == 2
            total = ga_ref[0:1] if rows else ga_ref[0]
            for dev in range(1, N_DEV):
                total = total + (ga_ref[dev:dev + 1] if rows else ga_ref[dev])
            o_ref[...] = total

    shapes = [jax.ShapeDtypeStruct((1,) + g.shape[1:] if g.ndim == 2 else g.shape[1:], F32) for g in gathered]
    return pl.pallas_call(body, name=name, out_shape=shapes, compiler_params=_params())(*gathered)


def adamw_replicated(ws, ms, vs, gs, *, name):
    n = len(ws)

    def body(*refs):
        w_refs, m_refs, v_refs, g_refs = refs[:n], refs[n:2 * n], refs[2 * n:3 * n], refs[3 * n:4 * n]
        outs = refs[4 * n:]
        for k in range(n):
            outs[k][...], outs[n + k][...], outs[2 * n + k][...] = _adamw(
                w_refs[k][...], g_refs[k][...], m_refs[k][...], v_refs[k][...])

    shapes = [jax.ShapeDtypeStruct(t.shape, F32) for t in ws]
    res = pl.pallas_call(body, name=name, out_shape=shapes * 3, compiler_params=_params())(*ws, *ms, *vs, *gs)
    return res[:n], res[n:2 * n], res[2 * n:]


SHARDED = ("w_in", "ssm_glu_w", "w_out", "w_ffn_in", "w_ffn_out", "w_ple_gate", "w_ple_proj")
SMALL_LAST = ("norm_mix_g",)
SMALL_WIDE = ("ssm_b_re", "ssm_b_im", "ssm_c_re", "ssm_c_im")
SMALL = ("ssm_lambda_re", "ssm_lambda_im", "ssm_log_step", "ssm_b_re", "ssm_b_im", "ssm_c_re",
         "ssm_c_im", "ssm_d", "ssm_glu_b", "sgu_ln_g", "sgu_ln_b", "sgu_w", "sgu_b", "out_norm_ssm_g",
         "out_norm_sgu_g", "norm_ffn_g", "norm_ple_g", "b_ple_gate", "final_norm_g")
WEIGHTS = ("norm_mix_g", "w_in", "ssm_lambda_re", "ssm_lambda_im", "ssm_log_step", "ssm_b_re", "ssm_b_im",
           "ssm_c_re", "ssm_c_im", "ssm_d", "ssm_glu_w", "ssm_glu_b", "sgu_ln_g", "sgu_ln_b", "sgu_w", "sgu_b",
           "out_norm_ssm_g", "out_norm_sgu_g", "w_out", "norm_ffn_g", "w_ffn_in", "w_ffn_out", "norm_ple_g",
           "w_ple_gate", "b_ple_gate", "w_ple_proj", "final_norm_g")


def _step(x, p, loss_target, w, m, v):
    L, D = x.shape[1], x.shape[2]
    x2d, p2d, tgt = x.reshape(L, D), p.reshape(L, -1), loss_target.reshape(L, D)
    d_ssm = w["ssm_glu_w"].shape[2]
    d_sgu = w["sgu_ln_g"].shape[1]
    G, P, H = w["ssm_b_re"].shape[1:]
    SG = min(SSM_SUPER, G)
    NK = G // SG
    row = lambda a: a.reshape(1, -1)

    shard2d = {n: w[n].reshape(w[n].shape[1:]) for n in SHARDED}
    shard_bf = {n: shard2d[n].astype(BF16) for n in SHARDED}
    (w_ple_blk,) = all_gather_blocks([shard_bf["w_ple_proj"]], name="gather_w_ple")
    bf = lambda t: t.astype(BF16)
    pp, (w_in_blk,) = mm_nn(bf(p2d), w_ple_blk, name="ple_proj", out_dtype=F32, tm=512, tn=512, tk=2048,
                            bg=gather_background([shard_bf["w_in"]]))
    w_in = jnp.transpose(w_in_blk, (1, 0, 2)).reshape(D, -1)
    F = shard2d["w_ffn_in"].shape[1] * 4

    lam_re, lam_im, log_step = w["ssm_lambda_re"][0], w["ssm_lambda_im"][0], w["ssm_log_step"][0].reshape(G, 1)
    a_re, a_im, q_re, q_im = disc_lambda_fwd(lam_re, lam_im, log_step, name="s5_discretise_lambda")
    bt_re = w["ssm_b_re"][0].transpose(2, 0, 1).reshape(H, G * P)
    bt_im = w["ssm_b_im"][0].transpose(2, 0, 1).reshape(H, G * P)
    bbar_re, bbar_im = disc_b_fwd(row(q_re), row(q_im), bt_re, bt_im, name="s5_discretise_b")
    ct_re = w["ssm_c_re"][0].transpose(1, 0, 2).reshape(H, G * P)
    ct_im = w["ssm_c_im"][0].transpose(1, 0, 2).reshape(H, G * P)
    a_re_k, a_im_k = a_re.reshape(NK, 1, SG * P), a_im.reshape(NK, 1, SG * P)
    d_k = w["ssm_d"][0].reshape(NK, 1, SG * H)

    h1 = norm_fwd(x2d, w["norm_mix_g"], name="norm_mix")
    z, (w_glu, w_out) = mm_nn(h1, w_in, name="in_proj", out_dtype=F32, tm=512, tn=1024, tk=2048,
                              bg=gather_background([shard_bf["ssm_glu_w"], shard_bf["w_out"]]))
    w_glu, w_out = w_glu.reshape(d_ssm, d_ssm), w_out.reshape(D, D)
    first_rows = (D * 11 // 16) // 16 * 16
    s5_mats = (bbar_re, bbar_im, ct_re, ct_im, a_re_k, a_im_k, d_k)
    (y0, s_re, s_im), (w_ffn_in_part,) = ssm_fwd(
        z, *s5_mats, name="s5_scan", bg=gather_background([shard_bf["w_ffn_in"]], rows=(0, first_rows)))
    ya1 = glu_pre(y0, name="s5_gelu")
    t_glu = mm_nn(ya1, w_glu, name="s5_glu_proj", out_dtype=F32, tm=512, tn=512, tk=2048)
    n_a = glu_post(y0, t_glu, w["ssm_glu_b"], w["out_norm_ssm_g"], name="s5_glu_norm")
    b_s3 = w["sgu_b"][0][:, :, None]
    n_b = sgu_fwd(z, w["sgu_ln_g"], w["sgu_ln_b"], w["sgu_w"][0], b_s3, w["out_norm_sgu_g"], name="sgu", d_sgu=d_sgu)
    ycat = jnp.concatenate([n_a, n_b], axis=1)
    x1, (w_ffn_in_blk,) = mm_nn(
        ycat, w_out, name="out_proj", out_dtype=F32, tm=512, tn=512, tk=2048, residual=x2d,
        bg=gather_background([shard_bf["w_ffn_in"]], rows=(first_rows, D - first_rows), into=[w_ffn_in_part]))
    h2 = norm_fwd(x1, w["norm_ffn_g"], name="norm_ffn")
    (act, gate_ff, up_ff), (w_ffn_out, w_gate) = ffn_in_swiglu(
        h2, w_ffn_in_blk, name="ffn_in_swiglu", tm=512,
        bg=gather_background([shard_bf["w_ffn_out"], shard_bf["w_ple_gate"]]))
    w_ffn_out, w_gate = w_ffn_out.reshape(F, D), w_gate.reshape(D, D)
    x2 = mm_nn(act, w_ffn_out, name="ffn_out", out_dtype=F32, tm=512, tn=1024, tk=F, residual=x1)
    h3 = norm_fwd(x2, w["norm_ple_g"], name="norm_ple")
    gpre = mm_nn(h3, w_gate, name="ple_gate", out_dtype=F32, tm=512, tn=1024, tk=2048)

    dx3, dpre, dpp, loss_part, d_final_g, d_b_gate = head_and_loss(
        x2, gpre, w["b_ple_gate"], pp, row(w["final_norm_g"]), tgt, name="head_and_loss")
    x_pos, y_pos, c_pos = _position()
    where = jnp.stack([4 * x_pos + 2 * y_pos + c_pos, 2 * x_pos + y_pos]).astype(jnp.int32)
    core = jnp.reshape(c_pos, (1,)).astype(jnp.int32)
    own, others = {}, {}

    def blocks(named):
        g8 = {n: t.reshape((N_DEV,) + shard2d[n].shape) for n, t in named.items()}
        return g8, sibling_exchange_background(list(g8.values()))

    def chip_sums(g8, theirs):
        own.update(zip(g8, zip(g8.values(), theirs)))
        return [add_pairs(g, t, core, name="chip_sum_" + n) for (n, g), t in zip(g8.items(), theirs)]

    d_w_gate = mm_tn(h3, dpre, name="d_w_ple_gate", out_dtype=BF16, tm=L, tko=1024, tno=1024)
    d_w_ple = mm_tn(bf(p2d), dpp, name="d_w_ple_proj", out_dtype=BF16, tm=L, tko=1024, tno=1024, out_blocks=N_DEV)
    g8_ple, bg = blocks({"w_ple_gate": d_w_gate, "w_ple_proj": d_w_ple})
    dh3, theirs = mm_nt(dpre, w_gate, name="d_h_ple", out_dtype=F32, tm=512, tko=1024, tc=2048, bg=bg)
    bg = chip_exchange_background(chip_sums(g8_ple, theirs))
    dx2, dx2b, d_ple_g = norm_bwd(dh3, x2, w["norm_ple_g"], dx3, name="d_norm_ple", want_bf16=True)
    d_w_ffn_out, got = mm_tn(act, dx2b, name="d_w_ffn_out", out_dtype=BF16, tm=L, tko=1408, tno=512, bg=bg)
    others.update(zip(g8_ple, got))
    g8_fo, bg = blocks({"w_ffn_out": d_w_ffn_out})
    dgate, dup = ffn_out_bwd_swiglu(dx2b, w_ffn_out, gate_ff, up_ff, name="d_act_swiglu")
    half = N_DEV // 2
    d_w_ffn_in, theirs = mm_tn(h2, dgate, name="d_w_ffn_in_gate", out_dtype=BF16, tm=L, tko=512, tno=1408,
                               out_blocks=half, total_blocks=N_DEV, bg=bg)
    bg = chip_exchange_background(chip_sums(g8_fo, theirs))
    d_w_ffn_in, got = mm_tn(h2, dup, name="d_w_ffn_in_up", out_dtype=BF16, tm=L, tko=512, tno=1408,
                            out_blocks=half, block_offset=half, total_blocks=N_DEV, into=d_w_ffn_in, bg=bg)
    others.update(zip(g8_fo, got))
    g8_fi, bg = blocks({"w_ffn_in": d_w_ffn_in})
    dh2, theirs = mm_nt(dgate, w_ffn_in_blk, a2=dup, name="d_h_ffn", out_dtype=F32, tm=1024, tko=1024, tc=1408, bg=bg)
    late_parts = chip_sums(g8_fi, theirs)
    dx1, dx1b, d_ffn_g = norm_bwd(dh2, x1, w["norm_ffn_g"], dx2, name="d_norm_ffn", want_bf16=True)
    dycat = mm_nt(dx1b, w_out, name="d_ycat", out_dtype=F32, tm=512, tko=1024, tc=2048)
    d_w_out = mm_tn(ycat, dx1b, name="d_w_out", out_dtype=BF16, tm=L, tko=1024, tno=1024)
    g8_out, bg = blocks({"w_out": d_w_out})
    dzu, dzv, d_sgu_w, d_sgu_b, d_ln_g, d_ln_b, d_g_b = sgu_bwd(
        z, dycat, w["sgu_ln_g"], w["sgu_ln_b"], w["sgu_w"][0], b_s3, w["out_norm_sgu_g"], name="d_sgu", d_sgu=d_sgu)
    dt_glu, dd_direct, d_g_a, d_glu_b = glu_post_bwd(
        y0, t_glu, w["ssm_glu_b"], w["out_norm_ssm_g"], dycat, name="d_s5_glu_norm")
    d_w_glu, theirs = mm_tn(ya1, dt_glu, name="d_w_glu", out_dtype=BF16, tm=L, tko=1024, tno=1024, bg=bg)
    late_parts += chip_sums(g8_out, theirs)
    g8_glu, bg = blocks({"ssm_glu_w": d_w_glu})
    dd_mm, theirs = mm_nt(dt_glu, w_glu, name="d_s5_glu_proj", out_dtype=F32, tm=512, tko=1024, tc=2048, bg=bg)
    late_parts += chip_sums(g8_glu, theirs)
    (du, d_bbar_re, d_bbar_im, d_ct_re, d_ct_im, d_a_re, d_a_im, d_d), got = ssm_bwd(
        z, y0, dd_direct, dd_mm, s_re, s_im, *s5_mats, name="d_s5_scan", bg=chip_exchange_background(late_parts))
    others.update(zip(("w_ffn_in", "w_out", "ssm_glu_w"), got))
    dz = jnp.concatenate([du, dzu, dzv], axis=1)

    d_q_re, d_q_im, d_bt_re, d_bt_im = disc_b_bwd(row(q_re), row(q_im), bt_re, bt_im, d_bbar_re, d_bbar_im,
                                                  name="d_s5_discretise_b")
    d_lam_re, d_lam_im, d_log_step = disc_lambda_bwd(
        lam_re, lam_im, log_step,
        (d_a_re.reshape(G, P), d_a_im.reshape(G, P), d_q_re.reshape(G, P), d_q_im.reshape(G, P)),
        name="d_s5_discretise_lambda")
    small_grads = {
        "ssm_lambda_re": d_lam_re, "ssm_lambda_im": d_lam_im, "ssm_log_step": d_log_step,
        "ssm_b_re": d_bt_re, "ssm_b_im": d_bt_im, "ssm_c_re": d_ct_re, "ssm_c_im": d_ct_im,
        "ssm_d": d_d, "ssm_glu_b": d_glu_b, "sgu_ln_g": d_ln_g, "sgu_ln_b": d_ln_b,
        "sgu_w": d_sgu_w, "sgu_b": d_sgu_b, "out_norm_ssm_g": d_g_a, "out_norm_sgu_g": d_g_b,
        "norm_ffn_g": d_ffn_g, "norm_ple_g": d_ple_g, "b_ple_gate": d_b_gate, "final_norm_g": d_final_g,
    }

    d_w_in, got = mm_tn(h1, dz, name="d_w_in", out_dtype=BF16, tm=L, tko=1024, tno=1024, out_blocks=N_DEV,
                        bg=gather_background([loss_part] + [small_grads[n] for n in SMALL]))
    sums = sum_gathered(got, name="sum_small_grads")
    g8_in, _ = blocks({"w_in": d_w_in})
    theirs = sibling_exchange(list(g8_in.values()), name="grads_to_sibling_w_in")
    dh1, got = mm_nt(dz, w_in, name="d_h_mix", out_dtype=F32, tm=512, tko=1024, tc=3 * d_sgu,
                     bg=chip_exchange_background(chip_sums(g8_in, theirs)))
    others.update(zip(g8_in, got))
    grad_x, d_mix_g = norm_bwd(dh1, x2d, w["norm_mix_g"], dx1, name="d_norm_mix", want_bf16=False)
    loss, small_sum = sums[0][0, 0], dict(zip(SMALL, sums[1:]))
    (small_sum["norm_mix_g"],) = sum_gathered(all_gather_blocks([d_mix_g], name="gather_last_grad"),
                                              name="sum_last_grad")

    out = {}
    for n in SHARDED:
        res = adamw_sharded(shard2d[n], m[n].reshape(shard2d[n].shape), v[n].reshape(shard2d[n].shape),
                            own[n][0], own[n][1], others[n], where, name="adamw_" + n)
        out[n] = [r.reshape(w[n].shape) for r in res]

    def work_shape(n):
        s = w[n].shape
        return (1,) + s if len(s) == 1 else (s if len(s) == 2 else s[1:])

    for n in ("ssm_b_re", "ssm_b_im"):
        small_sum[n] = small_sum[n].reshape(H, G, P).transpose(1, 2, 0)
    for n in ("ssm_c_re", "ssm_c_im"):
        small_sum[n] = small_sum[n].reshape(H, G, P).transpose(1, 0, 2)

    def replicated(names_, name):
        gs = [small_sum[n].reshape(work_shape(n)) for n in names_]
        res = adamw_replicated(*[[t[n].reshape(work_shape(n)) for n in names_] for t in (w, m, v)], gs, name=name)
        for i, n in enumerate(names_):
            out[n] = [r.reshape(w[n].shape) for r in (gs[i], res[0][i], res[1][i], res[2][i])]

    replicated([n for n in SMALL + SMALL_LAST if n not in SMALL_WIDE], "adamw_small")
    replicated(list(SMALL_WIDE), "adamw_s5_b_c")

    grads = [out[n][0] for n in WEIGHTS]
    deltas = [out[n][1] for n in WEIGHTS]
    new_m = [out[n][2] for n in WEIGHTS]
    new_v = [out[n][3] for n in WEIGHTS]
    return (loss, grad_x.reshape(x.shape), *grads, *deltas, *new_m, *new_v)


def kernel(x, p, norm_mix_g, w_in, ssm_lambda_re, ssm_lambda_im, ssm_log_step, ssm_b_re, ssm_b_im, ssm_c_re, ssm_c_im, ssm_d, ssm_glu_w, ssm_glu_b, sgu_ln_g, sgu_ln_b, sgu_w, sgu_b, out_norm_ssm_g, out_norm_sgu_g, w_out, norm_ffn_g, w_ffn_in, w_ffn_out, norm_ple_g, w_ple_gate, b_ple_gate, w_ple_proj, final_norm_g, loss_target, m_norm_mix_g, m_w_in, m_ssm_lambda_re, m_ssm_lambda_im, m_ssm_log_step, m_ssm_b_re, m_ssm_b_im, m_ssm_c_re, m_ssm_c_im, m_ssm_d, m_ssm_glu_w, m_ssm_glu_b, m_sgu_ln_g, m_sgu_ln_b, m_sgu_w, m_sgu_b, m_out_norm_ssm_g, m_out_norm_sgu_g, m_w_out, m_norm_ffn_g, m_w_ffn_in, m_w_ffn_out, m_norm_ple_g, m_w_ple_gate, m_b_ple_gate, m_w_ple_proj, m_final_norm_g, v_norm_mix_g, v_w_in, v_ssm_lambda_re, v_ssm_lambda_im, v_ssm_log_step, v_ssm_b_re, v_ssm_b_im, v_ssm_c_re, v_ssm_c_im, v_ssm_d, v_ssm_glu_w, v_ssm_glu_b, v_sgu_ln_g, v_sgu_ln_b, v_sgu_w, v_sgu_b, v_out_norm_ssm_g, v_out_norm_sgu_g, v_w_out, v_norm_ffn_g, v_w_ffn_in, v_w_ffn_out, v_norm_ple_g, v_w_ple_gate, v_b_ple_gate, v_w_ple_proj, v_final_norm_g):
    given = dict(locals())
    w = {n: given[n] for n in WEIGHTS}
    m = {n: given["m_" + n] for n in WEIGHTS}
    v = {n: given["v_" + n] for n in WEIGHTS}
    return _step(x, p, loss_target, w, m, v)
```
